```python
import math
import jax, jax.numpy as jnp
from jax import lax
import numpy as np

D_MODEL = 1024
BATCH = 8
SEQ = 2048
DEPTH = 1

CHUNK = 64
Q_BLOCK = 128
N_MEM = 256
N_DIFF_HEADS = 4
DIFF_V_DIM = D_MODEL // 2 // N_DIFF_HEADS
DIFF_QK_DIM = DIFF_V_DIM // 2
N_FOX_HEADS = 4
FOX_HEAD_DIM = D_MODEL // 2 // N_FOX_HEADS
N_CROSS_HEADS = 4
CROSS_HEAD_DIM = D_MODEL // N_CROSS_HEADS
N_BUCKETS = 32
MAX_DISTANCE = 128
N_GROUPS = 4
EXPERTS_PER_GROUP = 4
N_EXPERTS = N_GROUPS * EXPERTS_PER_GROUP
TOP_K_IN_GROUP = 2
D_EXPERT = D_MODEL // 2
FORGET_BIAS_INIT = 2.0
EPS = 1e-6
NEG_INF = -1e30

DIFF_WIDTH = N_DIFF_HEADS * DIFF_V_DIM
FOX_WIDTH = N_FOX_HEADS * FOX_HEAD_DIM
MIX_WIDTH = DIFF_WIDTH + FOX_WIDTH
COL_SIZES = [
    N_DIFF_HEADS * 2 * DIFF_QK_DIM,
    N_DIFF_HEADS * 2 * DIFF_QK_DIM,
    DIFF_WIDTH,
    FOX_WIDTH,
    FOX_WIDTH,
    FOX_WIDTH,
    N_FOX_HEADS,
]
IN_COLS = sum(COL_SIZES)
SPLITS = [int(v) for v in np.cumsum(COL_SIZES)[:-1]]

kernel_name = "hybrid_diff_fox_memxattn_hmoe"


def rms_norm(x, g):
    xf = x.astype(jnp.float32)
    y = xf * lax.rsqrt(jnp.mean(xf * xf, axis=-1, keepdims=True) + EPS)
    return (y * g.astype(jnp.float32)).astype(x.dtype)


def t5_bucket(rel):
    nb = N_BUCKETS // 2
    max_exact = nb // 2
    ret = (rel > 0).astype(jnp.int32) * nb
    n = jnp.abs(rel)
    nf = jnp.maximum(n, 1).astype(jnp.float32)
    large = max_exact + (jnp.log(nf / max_exact) / math.log(MAX_DISTANCE / max_exact)
                         * (nb - max_exact)).astype(jnp.int32)
    large = jnp.minimum(large, nb - 1)
    return ret + jnp.where(n < max_exact, n, large)


def differential_attention(q, k, v, rel_bias, lam, lam_init, subln_g):
    s_len = q.shape[1]
    scale = DIFF_QK_DIM ** -0.5
    pos = jnp.arange(s_len, dtype=jnp.int32)
    chunk = pos // CHUNK
    outs = []
    for i in range(s_len // Q_BLOCK):
        q0, q1 = i * Q_BLOCK, (i + 1) * Q_BLOCK
        logits = jnp.einsum('bqhcd,bkhcd->bhcqk', q[:, q0:q1], k[:, :q1]).astype(jnp.float32) * scale
        rel = pos[None, :q1] - pos[q0:q1, None]
        bias = jnp.transpose(rel_bias[t5_bucket(rel)], (2, 0, 1)).astype(jnp.float32)
        mask = chunk[None, :q1] <= chunk[q0:q1, None]
        logits = jnp.where(mask, logits + bias[None, :, None], NEG_INF)
        probs = jax.nn.softmax(logits, axis=-1)
        attn = probs[:, :, 0] - lam * probs[:, :, 1]
        outs.append(jnp.einsum('bhqk,bkhe->bqhe', attn.astype(v.dtype), v[:, :q1]))
    o = jnp.concatenate(outs, axis=1)
    return rms_norm(o, subln_g) * (1.0 - lam_init)


def forgetting_attention(q, k, v, cum_logf, out_g):
    s_len = q.shape[1]
    scale = FOX_HEAD_DIM ** -0.5
    pos = jnp.arange(s_len, dtype=jnp.int32)
    outs = []
    for i in range(s_len // Q_BLOCK):
        q0, q1 = i * Q_BLOCK, (i + 1) * Q_BLOCK
        logits = jnp.einsum('bqhd,bkhd->bhqk', q[:, q0:q1], k[:, :q1]).astype(jnp.float32) * scale
        decay = cum_logf[:, :, q0:q1, None] - cum_logf[:, :, None, :q1]
        mask = pos[None, :q1] <= pos[q0:q1, None]
        logits = jnp.where(mask, logits + decay, NEG_INF)
        probs = jax.nn.softmax(logits, axis=-1)
        outs.append(jnp.einsum('bhqk,bkhd->bqhd', probs.astype(v.dtype), v[:, :q1]))
    o = jnp.concatenate(outs, axis=1)
    return rms_norm(o, out_g)


def memory_cross_attention(h, mem_n, w_cq, w_ckv, q_g, k_g, w_co):
    b, s_len, _ = h.shape
    q = (h @ w_cq).reshape(b, s_len, N_CROSS_HEADS, CROSS_HEAD_DIM)
    kv = (mem_n @ w_ckv).reshape(b, mem_n.shape[1], 2, N_CROSS_HEADS, CROSS_HEAD_DIM)
    q = rms_norm(q, q_g)
    k = rms_norm(kv[:, :, 0], k_g)
    v = kv[:, :, 1]
    logits = jnp.einsum('bshd,bmhd->bhsm', q, k).astype(jnp.float32) * CROSS_HEAD_DIM ** -0.5
    probs = jax.nn.softmax(logits, axis=-1)
    o = jnp.einsum('bhsm,bmhd->bshd', probs.astype(v.dtype), v)
    return o.reshape(b, s_len, D_MODEL) @ w_co


def hierarchical_moe(h, w_gr, b_gr, w_er, b_er, w_gate, w_up, w_down):
    b, s_len, d = h.shape
    t = h.reshape(-1, d)
    n_tok = t.shape[0]
    p_group = jax.nn.softmax((t @ w_gr + b_gr).astype(jnp.float32), axis=-1)
    g_idx = jnp.argmax(p_group, axis=-1)
    p_g = jnp.take_along_axis(p_group, g_idx[:, None], axis=-1)
    e_logits = (t @ w_er + b_er).astype(jnp.float32).reshape(n_tok, N_GROUPS, EXPERTS_PER_GROUP)
    sel = jnp.take_along_axis(e_logits, g_idx[:, None, None], axis=1)[:, 0]
    p_in = jax.nn.softmax(sel, axis=-1)
    top_v, top_i = lax.top_k(p_in, TOP_K_IN_GROUP)
    top_v = top_v / jnp.sum(top_v, axis=-1, keepdims=True)
    w = p_g * top_v
    eid = g_idx[:, None] * EXPERTS_PER_GROUP + top_i
    gates = jnp.sum(jax.nn.one_hot(eid, N_EXPERTS, dtype=jnp.float32) * w[..., None], axis=1)
    gates = gates.astype(t.dtype)
    y = jnp.zeros_like(t)
    for e in range(N_EXPERTS):
        a = jax.nn.silu(t @ w_gate[e]) * (t @ w_up[e])
        y = y + gates[:, e:e + 1] * (a @ w_down[e])
    return y.reshape(b, s_len, d)


def setup_inputs(seed: int = 0) -> dict:
    key = jax.random.key(seed)
    ks = iter(jax.random.split(key, 48))
    L, D = DEPTH, D_MODEL

    def nrm(shape, scale):
        return scale * jax.random.normal(next(ks), shape, jnp.float32)

    def gain(shape):
        return 1.0 + nrm(shape, 0.02)

    return {
        "x": nrm((BATCH, SEQ, D), 1.0),
        "mem": nrm((BATCH, N_MEM, D), 1.0),
        "rel_bias": nrm((N_BUCKETS, N_DIFF_HEADS), 0.5),
        "norm_mix_g": gain((L, D)),
        "w_in": nrm((L, D, IN_COLS), D ** -0.5),
        "b_forget": FORGET_BIAS_INIT + nrm((L, N_FOX_HEADS), 0.1),
        "diff_q_norm_g": gain((L, DIFF_QK_DIM)),
        "diff_k_norm_g": gain((L, DIFF_QK_DIM)),
        "diff_lambda_q1": nrm((L, DIFF_QK_DIM), 0.1),
        "diff_lambda_k1": nrm((L, DIFF_QK_DIM), 0.1),
        "diff_lambda_q2": nrm((L, DIFF_QK_DIM), 0.1),
        "diff_lambda_k2": nrm((L, DIFF_QK_DIM), 0.1),
        "diff_subln_g": gain((L, DIFF_V_DIM)),
        "fox_q_norm_g": gain((L, FOX_HEAD_DIM)),
        "fox_k_norm_g": gain((L, FOX_HEAD_DIM)),
        "fox_out_norm_g": gain((L, FOX_HEAD_DIM)),
        "w_out": nrm((L, MIX_WIDTH, D), MIX_WIDTH ** -0.5),
        "norm_cross_g": gain((L, D)),
        "norm_mem_g": gain((L, D)),
        "w_cq": nrm((L, D, D), D ** -0.5),
        "w_ckv": nrm((L, D, 2 * D), D ** -0.5),
        "cross_q_norm_g": gain((L, CROSS_HEAD_DIM)),
        "cross_k_norm_g": gain((L, CROSS_HEAD_DIM)),
        "w_co": nrm((L, D, D), D ** -0.5),
        "norm_ffn_g": gain((L, D)),
        "w_group_router": nrm((L, D, N_GROUPS), D ** -0.5),
        "b_group_router": nrm((L, N_GROUPS), 0.01),
        "w_expert_router": nrm((L, D, N_EXPERTS), D ** -0.5),
        "b_expert_router": nrm((L, N_EXPERTS), 0.01),
        "w_exp_gate": nrm((L, N_EXPERTS, D, D_EXPERT), D ** -0.5),
        "w_exp_up": nrm((L, N_EXPERTS, D, D_EXPERT), D ** -0.5),
        "w_exp_down": nrm((L, N_EXPERTS, D_EXPERT, D), D_EXPERT ** -0.5),
    }


def reference(x, mem, rel_bias, norm_mix_g, w_in, b_forget, diff_q_norm_g, diff_k_norm_g,
              diff_lambda_q1, diff_lambda_k1, diff_lambda_q2, diff_lambda_k2, diff_subln_g,
              fox_q_norm_g, fox_k_norm_g, fox_out_norm_g, w_out, norm_cross_g, norm_mem_g,
              w_cq, w_ckv, cross_q_norm_g, cross_k_norm_g, w_co, norm_ffn_g,
              w_group_router, b_group_router, w_expert_router, b_expert_router,
              w_exp_gate, w_exp_up, w_exp_down):
    b, s_len, _ = x.shape
    for l in range(DEPTH):
        h = rms_norm(x, norm_mix_g[l])
        proj = h @ w_in[l]
        dq, dk, dv, fq, fk, fv, f_logit = jnp.split(proj, SPLITS, axis=-1)
        dq = rms_norm(dq.reshape(b, s_len, N_DIFF_HEADS, 2, DIFF_QK_DIM), diff_q_norm_g[l])
        dk = rms_norm(dk.reshape(b, s_len, N_DIFF_HEADS, 2, DIFF_QK_DIM), diff_k_norm_g[l])
        dv = dv.reshape(b, s_len, N_DIFF_HEADS, DIFF_V_DIM)
        lam_init = 0.8 - 0.6 * math.exp(-0.3 * l)
        lam = (jnp.exp(jnp.sum(diff_lambda_q1[l].astype(jnp.float32) * diff_lambda_k1[l].astype(jnp.float32)))
               - jnp.exp(jnp.sum(diff_lambda_q2[l].astype(jnp.float32) * diff_lambda_k2[l].astype(jnp.float32)))
               + lam_init)
        o_diff = differential_attention(dq, dk, dv, rel_bias, lam, lam_init, diff_subln_g[l])

        fq = rms_norm(fq.reshape(b, s_len, N_FOX_HEADS, FOX_HEAD_DIM), fox_q_norm_g[l])
        fk = rms_norm(fk.reshape(b, s_len, N_FOX_HEADS, FOX_HEAD_DIM), fox_k_norm_g[l])
        fv = fv.reshape(b, s_len, N_FOX_HEADS, FOX_HEAD_DIM)
        log_f = jax.nn.log_sigmoid(f_logit.astype(jnp.float32) + b_forget[l].astype(jnp.float32))
        cum_logf = jnp.transpose(jnp.cumsum(log_f, axis=1), (0, 2, 1))
        o_fox = forgetting_attention(fq, fk, fv, cum_logf, fox_out_norm_g[l])

        mixed = jnp.concatenate([o_diff.reshape(b, s_len, DIFF_WIDTH),
                                 o_fox.reshape(b, s_len, FOX_WIDTH)], axis=-1)
        x = x + mixed @ w_out[l]

        x = x + memory_cross_attention(rms_norm(x, norm_cross_g[l]), rms_norm(mem, norm_mem_g[l]),
                                       w_cq[l], w_ckv[l], cross_q_norm_g[l], cross_k_norm_g[l], w_co[l])

        x = x + hierarchical_moe(rms_norm(x, norm_ffn_g[l]), w_group_router[l], b_group_router[l],
                                 w_expert_router[l], b_expert_router[l],
                                 w_exp_gate[l], w_exp_up[l], w_exp_down[l])
    return x
```

```python
import functools
import math

import jax
import jax.numpy as jnp
import numpy as np
from jax import lax
from jax.experimental import pallas as pl
from jax.experimental.pallas import tpu as pltpu

F32 = jnp.float32
BF16 = jnp.bfloat16
I32 = jnp.int32

D_MODEL = 1024
CHUNK = 64
N_MEM = 256
N_HEADS = 4
HEAD_DIM = 128
DIFF_QK_DIM = 64
CROSS_HEAD_DIM = 256
N_BUCKETS_T5 = 32
MAX_DISTANCE = 128
N_GROUPS = 4
EXPERTS_PER_GROUP = 4
N_EXPERTS = N_GROUPS * EXPERTS_PER_GROUP
D_EXPERT = 512
EPS = 1e-6
NEG_INF = -1e30
LAM_INIT = 0.8 - 0.6 * math.exp(-0.3 * 0)

PAIRS = ((0, 1), (0, 2), (0, 3), (1, 2), (1, 3), (2, 3))
N_PAIR_BUCKETS = N_GROUPS * len(PAIRS)

LANES = 128
VMEM_LIMIT_CAP = 56 * 1024 * 1024

PROJ_ROWS = 512
ATTN_TILE = 256
CROSS_ROWS = 256
MOE_ROWS = 256
AUX_COLS = LANES
ROW_W = D_MODEL + AUX_COLS
DMA_CHUNK = 512


def _vmem_limit(nbytes):
    return int(min(max(nbytes * 5 // 4, 32 * 1024 * 1024), VMEM_LIMIT_CAP))


def _nt_dot(a, b):
    return lax.dot_general(a, b, (((1,), (1,)), ((), ())), preferred_element_type=F32)


def _rms(x, g):
    ms = jnp.mean(x * x, axis=-1, keepdims=True)
    return x * lax.rsqrt(ms + EPS) * g


def _mix_proj_kernel(x_ref, g_ref, w_ref, wft_ref, bf_ref, qg_ref, o_ref, cum_ref, carry_ref, *, tm):
    si = pl.program_id(1)
    h = _rms(x_ref[...], g_ref[...]).astype(BF16)
    lane = lax.broadcasted_iota(I32, (tm, HEAD_DIM), 1)
    lo = lane < DIFF_QK_DIM
    for kind in range(6):
        p = jnp.dot(h, w_ref[:, kind * 512:(kind + 1) * 512], preferred_element_type=F32)
        for hh in range(N_HEADS):
            ph = p[:, hh * HEAD_DIM:(hh + 1) * HEAD_DIM]
            if kind in (0, 1):
                sq = ph * ph
                s_lo = jnp.sum(jnp.where(lo, sq, 0.0), axis=-1, keepdims=True)
                s_hi = jnp.sum(jnp.where(lo, 0.0, sq), axis=-1, keepdims=True)
                ms = jnp.where(lo, s_lo, s_hi) * (1.0 / DIFF_QK_DIM)
                ph = ph * lax.rsqrt(ms + EPS) * qg_ref[kind:kind + 1, :]
            elif kind in (3, 4):
                ph = _rms(ph, qg_ref[kind:kind + 1, :])
            if kind in (0, 3):
                ph = ph * qg_ref[kind + 2:kind + 3, :]
            o_ref[0, kind * N_HEADS + hh] = ph.astype(BF16)

    z = _nt_dot(wft_ref[...], h) + bf_ref[...]
    logf = jnp.minimum(z, 0.0) - jnp.log(1.0 + jnp.exp(-jnp.abs(z)))
    lane8 = lax.broadcasted_iota(I32, (8, tm), 1)
    c = logf
    k = 1
    while k < tm:
        c = c + jnp.where(lane8 >= k, pltpu.roll(c, k, axis=1), 0.0)
        k *= 2

    @pl.when(si == 0)
    def _():
        carry_ref[...] = jnp.zeros_like(carry_ref)

    c = c + carry_ref[:, 0:1]
    cum_ref[0] = c
    carry_ref[...] = jnp.broadcast_to(c[:, tm - 1:tm], carry_ref.shape)


def _mix_proj(x2d, g, w_main, wft, bf, qg, *, batch, seq):
    tm = min(PROJ_ROWS, seq)
    ns = seq // tm
    est = 2 * (tm * D_MODEL * 4 + D_MODEL * 3072 * 2 + 24 * tm * HEAD_DIM * 2) + 8 * tm * D_MODEL
    return pl.pallas_call(
        functools.partial(_mix_proj_kernel, tm=tm),
        grid=(batch, ns),
        in_specs=[
            pl.BlockSpec((tm, D_MODEL), lambda b, s: (b * ns + s, 0)),
            pl.BlockSpec((1, D_MODEL), lambda b, s: (0, 0)),
            pl.BlockSpec((D_MODEL, 3072), lambda b, s: (0, 0)),
            pl.BlockSpec((8, D_MODEL), lambda b, s: (0, 0)),
            pl.BlockSpec((8, 1), lambda b, s: (0, 0)),
            pl.BlockSpec((8, HEAD_DIM), lambda b, s: (0, 0)),
        ],
        out_specs=[
            pl.BlockSpec((1, 24, tm, HEAD_DIM), lambda b, s: (b, 0, s, 0)),
            pl.BlockSpec((1, 8, tm), lambda b, s: (b, 0, s)),
        ],
        out_shape=[
            jax.ShapeDtypeStruct((batch, 24, seq, HEAD_DIM), BF16),
            jax.ShapeDtypeStruct((batch, 8, seq), F32),
        ],
        scratch_shapes=[pltpu.VMEM((8, LANES), F32)],
        compiler_params=pltpu.CompilerParams(
            dimension_semantics=("arbitrary", "arbitrary"), vmem_limit_bytes=_vmem_limit(est)),
        name="mix_proj",
    )(x2d, g, w_main, wft, bf, qg)


def _softmax_step(s, v, m_sc, l_sc, acc_sc):
    m_prev = m_sc[...]
    m_new = jnp.maximum(m_prev, jnp.max(s, axis=-1, keepdims=True))
    alpha = jnp.exp(m_prev - m_new)
    p = jnp.exp(s - m_new)
    l_sc[...] = alpha * l_sc[...] + jnp.sum(p, axis=-1, keepdims=True)
    acc_sc[...] = alpha * acc_sc[...] + jnp.dot(p.astype(BF16), v, preferred_element_type=F32)
    m_sc[...] = m_new


def _diff_attn_kernel(lam_ref, far_ref, q_ref, k_ref, v_ref, bias_ref, g_ref, o_ref,
                      m_sc, l_sc, acc_sc, *, t):
    hh = pl.program_id(1)
    qi = pl.program_id(2)
    q = q_ref[0, 0]
    lane = lax.broadcasted_iota(I32, q.shape, 1)
    zero = jnp.zeros_like(q)
    qq = jnp.concatenate([jnp.where(lane < DIFF_QK_DIM, q, zero),
                          jnp.where(lane < DIFF_QK_DIM, zero, q)], axis=0)
    m_sc[...] = jnp.full(m_sc.shape, NEG_INF, F32)
    l_sc[...] = jnp.zeros_like(l_sc)
    acc_sc[...] = jnp.zeros_like(acc_sc)

    def tile(j, bias):
        off = pl.multiple_of(j * t, t)
        k = k_ref[0, 0, pl.ds(off, t), :]
        v = v_ref[0, 0, pl.ds(off, t), :]
        s = _nt_dot(qq, k)
        if bias is None:
            s = s + far_ref[hh]
        else:
            s = (s.reshape(2, t, t) + bias[None]).reshape(2 * t, t)
        _softmax_step(s, v, m_sc, l_sc, acc_sc)

    def far_body(j, carry):
        tile(j, None)
        return carry

    lax.fori_loop(0, jnp.maximum(qi - 1, 0), far_body, 0)

    @pl.when(qi >= 1)
    def _():
        tile(qi - 1, bias_ref[0, 1])

    tile(qi, bias_ref[0, 0])

    lam = lam_ref[0]
    acc = acc_sc[...]
    l = l_sc[...]
    o = acc[:t] / l[:t] - lam * (acc[t:] / l[t:])
    o_ref[...] = (_rms(o, g_ref[...]) * (1.0 - LAM_INIT)).astype(BF16)


def _diff_attn(qkv, bias_tiles, bias_far, lam, subln_g, *, batch, seq):
    t = min(ATTN_TILE, seq)
    nq = seq // t
    est = 2 * (t * HEAD_DIM * 2 + 2 * seq * HEAD_DIM * 2 + 2 * t * t * 4) + 2 * t * (2 * LANES + HEAD_DIM) * 4 + 16 * t * t * 4
    return pl.pallas_call(
        functools.partial(_diff_attn_kernel, t=t),
        grid=(batch, N_HEADS, nq),
        in_specs=[
            pl.BlockSpec(memory_space=pltpu.SMEM),
            pl.BlockSpec(memory_space=pltpu.SMEM),
            pl.BlockSpec((1, 1, t, HEAD_DIM), lambda b, h, i: (b, h, i, 0)),
            pl.BlockSpec((1, 1, seq, HEAD_DIM), lambda b, h, i: (b, N_HEADS + h, 0, 0)),
            pl.BlockSpec((1, 1, seq, HEAD_DIM), lambda b, h, i: (b, 2 * N_HEADS + h, 0, 0)),
            pl.BlockSpec((1, 2, t, t), lambda b, h, i: (h, 0, 0, 0)),
            pl.BlockSpec((1, HEAD_DIM), lambda b, h, i: (0, 0)),
        ],
        out_specs=pl.BlockSpec((t, HEAD_DIM), lambda b, h, i: (b * nq + i, h)),
        out_shape=jax.ShapeDtypeStruct((batch * seq, N_HEADS * HEAD_DIM), BF16),
        scratch_shapes=[pltpu.VMEM((2 * t, 1), F32), pltpu.VMEM((2 * t, 1), F32),
                        pltpu.VMEM((2 * t, HEAD_DIM), F32)],
        compiler_params=pltpu.CompilerParams(
            dimension_semantics=("arbitrary", "arbitrary", "arbitrary"), vmem_limit_bytes=_vmem_limit(est)),
        name="diff_attn",
    )(lam, bias_far, qkv, qkv, qkv, bias_tiles, subln_g)


def _fox_attn_kernel(q_ref, k_ref, v_ref, c_ref, g_ref, o_ref, m_sc, l_sc, acc_sc, *, t):
    qi = pl.program_id(2)
    q = q_ref[0, 0]
    m_sc[...] = jnp.full(m_sc.shape, NEG_INF, F32)
    l_sc[...] = jnp.zeros_like(l_sc)
    acc_sc[...] = jnp.zeros_like(acc_sc)

    row = lax.broadcasted_iota(I32, (t, t), 0)
    col = lax.broadcasted_iota(I32, (t, t), 1)
    q_off = pl.multiple_of(qi * t, t)
    c_q_row = c_ref[0, 0, :, pl.ds(q_off, t)]
    c_q = jnp.sum(jnp.where(row == col, jnp.broadcast_to(c_q_row, (t, t)), 0.0), axis=-1, keepdims=True)

    def tile(j, causal):
        off = pl.multiple_of(j * t, t)
        k = k_ref[0, 0, pl.ds(off, t), :]
        v = v_ref[0, 0, pl.ds(off, t), :]
        c_k = c_ref[0, 0, :, pl.ds(off, t)]
        s = _nt_dot(q, k) + (c_q - c_k)
        if causal:
            s = jnp.where(col <= row, s, NEG_INF)
        _softmax_step(s, v, m_sc, l_sc, acc_sc)

    def body(j, carry):
        tile(j, False)
        return carry

    lax.fori_loop(0, qi, body, 0)
    tile(qi, True)

    o = acc_sc[...] / l_sc[...]
    o_ref[...] = _rms(o, g_ref[...]).astype(BF16)


def _fox_attn(qkv, cum4, out_g, *, batch, seq):
    t = min(ATTN_TILE, seq)
    nq = seq // t
    est = 2 * (t * HEAD_DIM * 2 + 2 * seq * HEAD_DIM * 2 + 8 * seq * 4) + t * (2 * LANES + HEAD_DIM) * 4 + 16 * t * t * 4
    return pl.pallas_call(
        functools.partial(_fox_attn_kernel, t=t),
        grid=(batch, N_HEADS, nq),
        in_specs=[
            pl.BlockSpec((1, 1, t, HEAD_DIM), lambda b, h, i: (b, 3 * N_HEADS + h, i, 0)),
            pl.BlockSpec((1, 1, seq, HEAD_DIM), lambda b, h, i: (b, 4 * N_HEADS + h, 0, 0)),
            pl.BlockSpec((1, 1, seq, HEAD_DIM), lambda b, h, i: (b, 5 * N_HEADS + h, 0, 0)),
            pl.BlockSpec((1, 1, 1, seq), lambda b, h, i: (b, h, 0, 0)),
            pl.BlockSpec((1, HEAD_DIM), lambda b, h, i: (0, 0)),
        ],
        out_specs=pl.BlockSpec((t, HEAD_DIM), lambda b, h, i: (b * nq + i, h)),
        out_shape=jax.ShapeDtypeStruct((batch * seq, N_HEADS * HEAD_DIM), BF16),
        scratch_shapes=[pltpu.VMEM((t, 1), F32), pltpu.VMEM((t, 1), F32), pltpu.VMEM((t, HEAD_DIM), F32)],
        compiler_params=pltpu.CompilerParams(
            dimension_semantics=("arbitrary", "arbitrary", "arbitrary"), vmem_limit_bytes=_vmem_limit(est)),
        name="fox_attn",
    )(qkv, qkv, qkv, cum4, out_g)


def _out_q_kernel(x_ref, md_ref, mf_ref, wo_ref, gc_ref, wq_ref, qg_ref, x1_ref, qc_ref):
    half = N_HEADS * HEAD_DIM
    x1 = (x_ref[...]
          + jnp.dot(md_ref[...], wo_ref[:half, :], preferred_element_type=F32)
          + jnp.dot(mf_ref[...], wo_ref[half:, :], preferred_element_type=F32))
    x1_ref[...] = x1
    hc = _rms(x1, gc_ref[...]).astype(BF16)
    q = jnp.dot(hc, wq_ref[...], preferred_element_type=F32)
    for hh in range(N_HEADS):
        sl = slice(hh * CROSS_HEAD_DIM, (hh + 1) * CROSS_HEAD_DIM)
        qh = _rms(q[:, sl], qg_ref[...]) * (CROSS_HEAD_DIM ** -0.5)
        qc_ref[:, sl] = qh.astype(BF16)


def _out_q(x2d, m_diff, m_fox, w_out, g_cross, w_cq, q_g):
    n = x2d.shape[0]
    tm = min(PROJ_ROWS, n)
    half = N_HEADS * HEAD_DIM
    est = 2 * (tm * D_MODEL * 4 * 2 + 2 * tm * half * 2 + 2 * D_MODEL * D_MODEL * 2 + tm * D_MODEL * 2) + 6 * tm * D_MODEL * 4
    return pl.pallas_call(
        _out_q_kernel,
        grid=(n // tm,),
        in_specs=[
            pl.BlockSpec((tm, D_MODEL), lambda i: (i, 0)),
            pl.BlockSpec((tm, half), lambda i: (i, 0)),
            pl.BlockSpec((tm, half), lambda i: (i, 0)),
            pl.BlockSpec((D_MODEL, D_MODEL), lambda i: (0, 0)),
            pl.BlockSpec((1, D_MODEL), lambda i: (0, 0)),
            pl.BlockSpec((D_MODEL, D_MODEL), lambda i: (0, 0)),
            pl.BlockSpec((1, CROSS_HEAD_DIM), lambda i: (0, 0)),
        ],
        out_specs=[pl.BlockSpec((tm, D_MODEL), lambda i: (i, 0)),
                   pl.BlockSpec((tm, D_MODEL), lambda i: (i, 0))],
        out_shape=[jax.ShapeDtypeStruct((n, D_MODEL), F32), jax.ShapeDtypeStruct((n, D_MODEL), BF16)],
        compiler_params=pltpu.CompilerParams(
            dimension_semantics=("arbitrary",), vmem_limit_bytes=_vmem_limit(est)),
        name="out_q",
    )(x2d, m_diff, m_fox, w_out, g_cross, w_cq, q_g)


def _mem_kv_kernel(mem_ref, gm_ref, w_ref, kg_ref, k_ref, v_ref):
    mn = _rms(mem_ref[...], gm_ref[...]).astype(BF16)
    kv = jnp.dot(mn, w_ref[...], preferred_element_type=F32)
    for hh in range(N_HEADS):
        sl = slice(hh * CROSS_HEAD_DIM, (hh + 1) * CROSS_HEAD_DIM)
        k_ref[0, :, sl] = _rms(kv[:, sl], kg_ref[...]).astype(BF16)
    v_ref[0] = kv[:, D_MODEL:].astype(BF16)


def _mem_kv(mem2d, g_mem, w_ckv, k_g, *, batch):
    est = 2 * (N_MEM * D_MODEL * 4 + D_MODEL * 2 * D_MODEL * 2 + 2 * N_MEM * D_MODEL * 2) + 4 * N_MEM * 2 * D_MODEL * 4
    return pl.pallas_call(
        _mem_kv_kernel,
        grid=(batch,),
        in_specs=[
            pl.BlockSpec((N_MEM, D_MODEL), lambda b: (b, 0)),
            pl.BlockSpec((1, D_MODEL), lambda b: (0, 0)),
            pl.BlockSpec((D_MODEL, 2 * D_MODEL), lambda b: (0, 0)),
            pl.BlockSpec((1, CROSS_HEAD_DIM), lambda b: (0, 0)),
        ],
        out_specs=[pl.BlockSpec((1, N_MEM, D_MODEL), lambda b: (b, 0, 0)),
                   pl.BlockSpec((1, N_MEM, D_MODEL), lambda b: (b, 0, 0))],
        out_shape=[jax.ShapeDtypeStruct((batch, N_MEM, D_MODEL), BF16),
                   jax.ShapeDtypeStruct((batch, N_MEM, D_MODEL), BF16)],
        compiler_params=pltpu.CompilerParams(
            dimension_semantics=("arbitrary",), vmem_limit_bytes=_vmem_limit(est)),
        name="mem_kv",
    )(mem2d, g_mem, w_ckv, k_g)


def _route(logits_t):
    gl = logits_t[0:N_GROUPS]
    gmax = jnp.max(gl, axis=0, keepdims=True)
    eg = jnp.exp(gl - gmax)
    p_group = eg / jnp.sum(eg, axis=0, keepdims=True)
    p_g = jnp.max(p_group, axis=0, keepdims=True)
    g_idx = jnp.full_like(p_g, float(N_GROUPS))
    for g in reversed(range(N_GROUPS)):
        g_idx = jnp.where(p_group[g:g + 1] == p_g, float(g), g_idx)

    sel = []
    for j in range(EXPERTS_PER_GROUP):
        acc = jnp.zeros_like(p_g)
        for g in range(N_GROUPS):
            r = N_GROUPS + g * EXPERTS_PER_GROUP + j
            acc = jnp.where(g_idx == float(g), logits_t[r:r + 1], acc)
        sel.append(acc)
    smax = functools.reduce(jnp.maximum, sel)
    es = [jnp.exp(s - smax) for s in sel]
    den = functools.reduce(jnp.add, es)
    p_in = [e / den for e in es]

    v0 = functools.reduce(jnp.maximum, p_in)
    i0 = jnp.full_like(v0, float(EXPERTS_PER_GROUP))
    for j in reversed(range(EXPERTS_PER_GROUP)):
        i0 = jnp.where(p_in[j] == v0, float(j), i0)
    rest = [jnp.where(i0 == float(j), -1.0, p_in[j]) for j in range(EXPERTS_PER_GROUP)]
    v1 = functools.reduce(jnp.maximum, rest)
    i1 = jnp.full_like(v1, float(EXPERTS_PER_GROUP))
    for j in reversed(range(EXPERTS_PER_GROUP)):
        i1 = jnp.where(rest[j] == v1, float(j), i1)

    tot = v0 + v1
    w0 = p_g * (v0 / tot)
    w1 = p_g * (v1 / tot)
    first = i0 < i1
    a = jnp.where(first, i0, i1)
    b = jnp.where(first, i1, i0)
    wa = jnp.where(first, w0, w1)
    wb = jnp.where(first, w1, w0)
    pair = jnp.where(a == 0.0, 0.0, jnp.where(a == 1.0, 3.0, 5.0)) + (b - a - 1.0)
    bucket = g_idx * float(len(PAIRS)) + pair
    return bucket, wa, wb


def _cross_router_kernel(q_ref, k_ref, v_ref, x1_ref, wco_ref, gf_ref, whi_ref, wlo_ref, rb_ref,
                         xa_ref, rt_ref, *, tm):
    q = q_ref[...]
    k = k_ref[0]
    v = v_ref[0]
    outs = []
    for hh in range(N_HEADS):
        sl = slice(hh * CROSS_HEAD_DIM, (hh + 1) * CROSS_HEAD_DIM)
        s = _nt_dot(q[:, sl], k[:, sl])
        p = jnp.exp(s - jnp.max(s, axis=-1, keepdims=True))
        l = jnp.sum(p, axis=-1, keepdims=True)
        o = jnp.dot(p.astype(BF16), v[:, sl], preferred_element_type=F32) / l
        outs.append(o.astype(BF16))
    o = jnp.concatenate(outs, axis=1)
    x2 = x1_ref[...] + jnp.dot(o, wco_ref[...], preferred_element_type=F32)
    xa_ref[:, :D_MODEL] = x2

    h3 = _rms(x2, gf_ref[...])
    hi = h3.astype(BF16)
    lo = (h3 - hi.astype(F32)).astype(BF16)
    lt = _nt_dot(whi_ref[...], hi) + (_nt_dot(whi_ref[...], lo) + _nt_dot(wlo_ref[...], hi))
    lt = lt + rb_ref[...]
    bucket, wa, wb = _route(lt)

    sub = lax.broadcasted_iota(I32, (8, tm), 0)
    rows = jnp.where(sub == 0, bucket, jnp.where(sub == 1, wa, jnp.where(sub == 2, wb, 0.0)))
    rt_ref[0] = rows
    sub_a = lax.broadcasted_iota(I32, (AUX_COLS, tm), 0)
    aux_t = jnp.where(sub_a == 0, bucket, jnp.where(sub_a == 1, wa, jnp.where(sub_a == 2, wb, 0.0)))
    xa_ref[:, D_MODEL:] = aux_t.T


def _cross_router(qc, kc, vc, x1, w_co, g_ffn, wr_hi, wr_lo, rb, *, batch, seq):
    tm = min(CROSS_ROWS, seq)
    ns = seq // tm
    n = batch * seq
    est = 2 * (tm * D_MODEL * 2 + 2 * N_MEM * D_MODEL * 2 + tm * D_MODEL * 4 + D_MODEL * D_MODEL * 2
               + tm * ROW_W * 4) + 10 * tm * D_MODEL * 4
    return pl.pallas_call(
        functools.partial(_cross_router_kernel, tm=tm),
        grid=(batch, ns),
        in_specs=[
            pl.BlockSpec((tm, D_MODEL), lambda b, s: (b * ns + s, 0)),
            pl.BlockSpec((1, N_MEM, D_MODEL), lambda b, s: (b, 0, 0)),
            pl.BlockSpec((1, N_MEM, D_MODEL), lambda b, s: (b, 0, 0)),
            pl.BlockSpec((tm, D_MODEL), lambda b, s: (b * ns + s, 0)),
            pl.BlockSpec((D_MODEL, D_MODEL), lambda b, s: (0, 0)),
            pl.BlockSpec((1, D_MODEL), lambda b, s: (0, 0)),
            pl.BlockSpec((32, D_MODEL), lambda b, s: (0, 0)),
            pl.BlockSpec((32, D_MODEL), lambda b, s: (0, 0)),
            pl.BlockSpec((32, 1), lambda b, s: (0, 0)),
        ],
        out_specs=[pl.BlockSpec((tm, ROW_W), lambda b, s: (b * ns + s, 0)),
                   pl.BlockSpec((1, 8, tm), lambda b, s: (b * ns + s, 0, 0))],
        out_shape=[jax.ShapeDtypeStruct((n, ROW_W), F32),
                   jax.ShapeDtypeStruct((n // tm, 8, tm), F32)],
        compiler_params=pltpu.CompilerParams(
            dimension_semantics=("arbitrary", "arbitrary"), vmem_limit_bytes=_vmem_limit(est)),
        name="cross_router",
    )(qc, kc, vc, x1, w_co, g_ffn, wr_hi, wr_lo, rb)


def _rank_kernel(rt_ref, rank_ref, cnt_ref, carry_sc, *, tm):
    @pl.when(pl.program_id(0) == 0)
    def _():
        carry_sc[...] = jnp.zeros_like(carry_sc)

    bucket = rt_ref[0, 0:1, :]
    sub = lax.broadcasted_iota(I32, (32, tm), 0).astype(F32)
    hit = sub == bucket
    r = lax.broadcasted_iota(I32, (tm, tm), 0)
    c = lax.broadcasted_iota(I32, (tm, tm), 1)
    upper = jnp.where(r <= c, 1.0, 0.0).astype(BF16)
    cum = jnp.dot(jnp.where(hit, 1.0, 0.0).astype(BF16), upper, preferred_element_type=F32)
    carry = carry_sc[:, 0:1]
    rank = jnp.sum(jnp.where(hit, cum - 1.0 + carry, 0.0), axis=0, keepdims=True)
    rank_ref[0] = rank.astype(I32)
    new_carry = carry_sc[...] + jnp.broadcast_to(cum[:, tm - 1:tm], carry_sc.shape)
    carry_sc[...] = new_carry
    cnt_ref[...] = new_carry


def _rank(rt):
    nt, _, tm = rt.shape
    return pl.pallas_call(
        functools.partial(_rank_kernel, tm=tm),
        grid=(nt,),
        in_specs=[pl.BlockSpec((1, 8, tm), lambda i: (i, 0, 0))],
        out_specs=[pl.BlockSpec((1, 1, tm), lambda i: (i, 0, 0)),
                   pl.BlockSpec((32, LANES), lambda i: (0, 0))],
        out_shape=[jax.ShapeDtypeStruct((nt, 1, tm), I32), jax.ShapeDtypeStruct((32, LANES), F32)],
        scratch_shapes=[pltpu.VMEM((32, LANES), F32)],
        compiler_params=pltpu.CompilerParams(dimension_semantics=("arbitrary",)),
        name="rank",
    )(rt)


def _row_copy(src, dst, s_row, d_row, sem):
    return pltpu.make_async_copy(src.at[pl.ds(s_row, 1)], dst.at[pl.ds(d_row, 1)], sem)


def _permute_rows(pos_ref, src, dst, sems, *, n_rows, chunk, gather):
    n_chunks = n_rows // chunk

    def copy(t, slot):
        p = pos_ref[t]
        if gather:
            return _row_copy(src, dst, p, t, sems.at[slot])
        return _row_copy(src, dst, t, p, sems.at[slot])

    def start_chunk(c):
        def body(r, carry):
            copy(c * chunk + r, c % 2).start()
            return carry
        lax.fori_loop(0, chunk, body, 0)

    def wait_chunk(c):
        def body(r, carry):
            copy(c * chunk + r, c % 2).wait()
            return carry
        lax.fori_loop(0, chunk, body, 0)

    start_chunk(0)

    def loop(c, carry):
        start_chunk(c)
        wait_chunk(c - 1)
        return carry

    lax.fori_loop(1, n_chunks, loop, 0)
    wait_chunk(n_chunks - 1)


def _dispatch_kernel(pos_ref, zrow_ref, zflag_ref, nact_ref, xa_ref, xs_ref, zbuf, zsem, sems, *,
                     n_tok, n_tiles, chunk):
    zbuf[...] = jnp.zeros_like(zbuf)

    def zero_copy(row):
        return pltpu.make_async_copy(zbuf, xs_ref.at[pl.ds(pl.multiple_of(row, MOE_ROWS), MOE_ROWS)], zsem)

    def bucket_tiles(op):
        def body(b, carry):
            @pl.when(zflag_ref[b] > 0)
            def _():
                op(zero_copy(zrow_ref[b]))
            return carry
        lax.fori_loop(0, N_PAIR_BUCKETS, body, 0)

    def unused_tiles(op):
        def body(i, carry):
            op(zero_copy(i * MOE_ROWS))
            return carry
        lax.fori_loop(nact_ref[0], n_tiles, body, 0)

    bucket_tiles(lambda cp: cp.start())
    unused_tiles(lambda cp: cp.start())
    bucket_tiles(lambda cp: cp.wait())
    unused_tiles(lambda cp: cp.wait())
    _permute_rows(pos_ref, xa_ref, xs_ref, sems, n_rows=n_tok, chunk=chunk, gather=False)


def _dispatch(pos, zrow, zflag, n_active, xa, *, n_tiles):
    n_tok = xa.shape[0]
    n_sorted = n_tiles * MOE_ROWS
    chunk = min(DMA_CHUNK, n_tok)
    return pl.pallas_call(
        functools.partial(_dispatch_kernel, n_tok=n_tok, n_tiles=n_tiles, chunk=chunk),
        grid_spec=pltpu.PrefetchScalarGridSpec(
            num_scalar_prefetch=4,
            grid=(1,),
            in_specs=[pl.BlockSpec(memory_space=pl.ANY)],
            out_specs=pl.BlockSpec(memory_space=pl.ANY),
            scratch_shapes=[pltpu.VMEM((MOE_ROWS, ROW_W), F32), pltpu.SemaphoreType.DMA(()),
                            pltpu.SemaphoreType.DMA((2,))],
        ),
        out_shape=jax.ShapeDtypeStruct((n_sorted, ROW_W), F32),
        compiler_params=pltpu.CompilerParams(dimension_semantics=("arbitrary",), has_side_effects=True),
        name="dispatch",
    )(pos, zrow, zflag, n_active, xa)


def _combine_kernel(pos_ref, ys_ref, out_ref, sems, *, n_tok, chunk):
    _permute_rows(pos_ref, ys_ref, out_ref, sems, n_rows=n_tok, chunk=chunk, gather=True)


def _combine(pos, ys, *, n_tok):
    chunk = min(DMA_CHUNK, n_tok)
    return pl.pallas_call(
        functools.partial(_combine_kernel, n_tok=n_tok, chunk=chunk),
        grid_spec=pltpu.PrefetchScalarGridSpec(
            num_scalar_prefetch=1,
            grid=(1,),
            in_specs=[pl.BlockSpec(memory_space=pl.ANY)],
            out_specs=pl.BlockSpec(memory_space=pl.ANY),
            scratch_shapes=[pltpu.SemaphoreType.DMA((2,))],
        ),
        out_shape=jax.ShapeDtypeStruct((n_tok, D_MODEL), F32),
        compiler_params=pltpu.CompilerParams(dimension_semantics=("arbitrary",), has_side_effects=True),
        name="combine",
    )(pos, ys)


def _moe_kernel(ta_ref, tb_ref, xi_ref, nact_ref, xs_ref, gf_ref, wga_ref, wua_ref, wda_ref,
                wgb_ref, wub_ref, wdb_ref, ys_ref):
    i = pl.program_id(0)

    @pl.when(i < nact_ref[0])
    def _():
        x2 = xs_ref[:, :D_MODEL]
        aux = xs_ref[:, D_MODEL:]
        wa = aux[:, 1:2]
        wb = aux[:, 2:3]
        h = _rms(x2, gf_ref[...]).astype(BF16)

        def mlp(wg, wu, wd):
            g = jnp.dot(h, wg[0], preferred_element_type=F32)
            u = jnp.dot(h, wu[0], preferred_element_type=F32)
            act = (g / (1.0 + jnp.exp(-g))) * u
            return jnp.dot(act.astype(BF16), wd[0], preferred_element_type=F32)

        ys_ref[...] = x2 + (wa * mlp(wga_ref, wua_ref, wda_ref) + wb * mlp(wgb_ref, wub_ref, wdb_ref))

    @pl.when(i >= nact_ref[0])
    def _():
        ys_ref[...] = jnp.zeros_like(ys_ref)


def _moe(tile_a, tile_b, tile_x, n_active, xs, g_ffn, w_gate, w_up, w_down):
    n_tiles = xs.shape[0] // MOE_ROWS
    t = MOE_ROWS
    wa_map = lambda i, ta, tb, xi, na: (ta[i], 0, 0)
    wb_map = lambda i, ta, tb, xi, na: (tb[i], 0, 0)
    est = 2 * (t * ROW_W * 4 + 6 * D_MODEL * D_EXPERT * 2 + t * D_MODEL * 4) + 10 * t * D_MODEL * 4
    return pl.pallas_call(
        _moe_kernel,
        grid_spec=pltpu.PrefetchScalarGridSpec(
            num_scalar_prefetch=4,
            grid=(n_tiles,),
            in_specs=[
                pl.BlockSpec((t, ROW_W), lambda i, ta, tb, xi, na: (xi[i], 0)),
                pl.BlockSpec((1, D_MODEL), lambda i, ta, tb, xi, na: (0, 0)),
                pl.BlockSpec((1, D_MODEL, D_EXPERT), wa_map),
                pl.BlockSpec((1, D_MODEL, D_EXPERT), wa_map),
                pl.BlockSpec((1, D_EXPERT, D_MODEL), wa_map),
                pl.BlockSpec((1, D_MODEL, D_EXPERT), wb_map),
                pl.BlockSpec((1, D_MODEL, D_EXPERT), wb_map),
                pl.BlockSpec((1, D_EXPERT, D_MODEL), wb_map),
            ],
            out_specs=pl.BlockSpec((t, D_MODEL), lambda i, ta, tb, xi, na: (i, 0)),
        ),
        out_shape=jax.ShapeDtypeStruct((n_tiles * t, D_MODEL), F32),
        compiler_params=pltpu.CompilerParams(
            dimension_semantics=("arbitrary",), vmem_limit_bytes=_vmem_limit(est)),
        name="moe",
    )(tile_a, tile_b, tile_x, n_active, xs, g_ffn, w_gate, w_up, w_down, w_gate, w_up, w_down)


def _t5_bucket(rel):
    nb = N_BUCKETS_T5 // 2
    max_exact = nb // 2
    ret = (rel > 0).astype(I32) * nb
    n = jnp.abs(rel)
    nf = jnp.maximum(n, 1).astype(F32)
    large = max_exact + (jnp.log(nf / max_exact) / math.log(MAX_DISTANCE / max_exact)
                         * (nb - max_exact)).astype(I32)
    large = jnp.minimum(large, nb - 1)
    return ret + jnp.where(n < max_exact, n, large)


def _diff_bias(rel_bias, t):
    assert t % CHUNK == 0 and t >= MAX_DISTANCE
    qpos = jnp.arange(t, dtype=I32)[:, None]
    kpos = jnp.arange(t, dtype=I32)[None, :]
    b0 = jnp.transpose(rel_bias[_t5_bucket(kpos - qpos)], (2, 0, 1)).astype(F32)
    b0 = jnp.where((kpos // CHUNK <= qpos // CHUNK)[None], b0, NEG_INF)
    b1 = jnp.transpose(rel_bias[_t5_bucket(kpos - qpos - t)], (2, 0, 1)).astype(F32)
    far = rel_bias[_t5_bucket(jnp.asarray(-2 * t, I32))].astype(F32)
    return jnp.stack([b0, b1], axis=1), far


def _tile_plan(counts, n_tiles):
    t = MOE_ROWS
    tiles = (counts + t - 1) // t
    ends = jnp.cumsum(tiles)
    starts = ends - tiles
    n_active = ends[-1]
    tile_idx = jnp.minimum(jnp.arange(n_tiles, dtype=I32), n_active - 1)
    tile_bucket = jnp.minimum(jnp.searchsorted(ends, tile_idx, side="right"), N_PAIR_BUCKETS - 1).astype(I32)
    pa = jnp.asarray([p[0] for p in PAIRS], I32)
    pb = jnp.asarray([p[1] for p in PAIRS], I32)
    grp = tile_bucket // len(PAIRS)
    pair = tile_bucket % len(PAIRS)
    tile_a = grp * EXPERTS_PER_GROUP + pa[pair]
    tile_b = grp * EXPERTS_PER_GROUP + pb[pair]
    row_off = (starts * t).astype(I32)
    zrow = ((ends - 1) * t).astype(I32)
    zflag = (counts > 0).astype(I32)
    return row_off, zrow, zflag, tile_a.astype(I32), tile_b.astype(I32), tile_idx.astype(I32), n_active.astype(I32)


def kernel(x, mem, rel_bias, norm_mix_g, w_in, b_forget, diff_q_norm_g, diff_k_norm_g, diff_lambda_q1, diff_lambda_k1, diff_lambda_q2, diff_lambda_k2, diff_subln_g, fox_q_norm_g, fox_k_norm_g, fox_out_norm_g, w_out, norm_cross_g, norm_mem_g, w_cq, w_ckv, cross_q_norm_g, cross_k_norm_g, w_co, norm_ffn_g, w_group_router, b_group_router, w_expert_router, b_expert_router, w_exp_gate, w_exp_up, w_exp_down):
    batch, seq, d = x.shape
    assert d == D_MODEL and norm_mix_g.shape[0] == 1 and mem.shape[1] == N_MEM
    n_tok = batch * seq
    assert seq % ATTN_TILE == 0 or seq < ATTN_TILE
    l = 0

    w_main = w_in[l, :, :3072].astype(BF16)
    wft = jnp.zeros((8, D_MODEL), F32).at[:N_HEADS].set(w_in[l, :, 3072:].T).astype(BF16)
    bf = jnp.zeros((8, 1), F32).at[:N_HEADS, 0].set(b_forget[l].astype(F32))
    ones = jnp.ones((HEAD_DIM,), F32)
    qg = jnp.stack([
        jnp.tile(diff_q_norm_g[l].astype(F32), 2), jnp.tile(diff_k_norm_g[l].astype(F32), 2),
        ones * DIFF_QK_DIM ** -0.5,
        fox_q_norm_g[l].astype(F32), fox_k_norm_g[l].astype(F32),
        ones * HEAD_DIM ** -0.5, ones, ones])
    lam = (jnp.exp(jnp.sum(diff_lambda_q1[l].astype(F32) * diff_lambda_k1[l].astype(F32)))
           - jnp.exp(jnp.sum(diff_lambda_q2[l].astype(F32) * diff_lambda_k2[l].astype(F32)))
           + LAM_INIT).reshape(1)
    t_attn = min(ATTN_TILE, seq)
    bias_tiles, bias_far = _diff_bias(rel_bias, t_attn)
    wr = jnp.zeros((32, D_MODEL), F32)
    wr = wr.at[:N_GROUPS].set(w_group_router[l].T).at[N_GROUPS:N_GROUPS + N_EXPERTS].set(w_expert_router[l].T)
    wr_hi = wr.astype(BF16)
    wr_lo = (wr - wr_hi.astype(F32)).astype(BF16)
    rb = jnp.zeros((32, 1), F32)
    rb = rb.at[:N_GROUPS, 0].set(b_group_router[l]).at[N_GROUPS:N_GROUPS + N_EXPERTS, 0].set(b_expert_router[l])

    row = lambda v: v.astype(F32).reshape(1, -1)
    x2d = x.reshape(n_tok, D_MODEL)

    qkv, cum = _mix_proj(x2d, row(norm_mix_g[l]), w_main, wft, bf, qg, batch=batch, seq=seq)
    m_diff = _diff_attn(qkv, bias_tiles, bias_far, lam, row(diff_subln_g[l]), batch=batch, seq=seq)
    cum4 = cum[:, :N_HEADS].reshape(batch, N_HEADS, 1, seq)
    m_fox = _fox_attn(qkv, cum4, row(fox_out_norm_g[l]), batch=batch, seq=seq)
    x1, qc = _out_q(x2d, m_diff, m_fox, w_out[l].astype(BF16), row(norm_cross_g[l]),
                    w_cq[l].astype(BF16), row(cross_q_norm_g[l]))

    kc, vc = _mem_kv(mem.reshape(batch * N_MEM, D_MODEL), row(norm_mem_g[l]), w_ckv[l].astype(BF16),
                     row(cross_k_norm_g[l]), batch=batch)
    xa, rt = _cross_router(qc, kc, vc, x1, w_co[l].astype(BF16), row(norm_ffn_g[l]), wr_hi, wr_lo, rb,
                           batch=batch, seq=seq)

    rank, cnt = _rank(rt)
    counts = cnt[:N_PAIR_BUCKETS, 0].astype(I32)
    n_tiles = n_tok // MOE_ROWS + N_PAIR_BUCKETS
    row_off, zrow, zflag, tile_a, tile_b, tile_x, n_active = _tile_plan(counts, n_tiles)
    bucket = rt[:, 0, :].reshape(n_tok).astype(I32)
    pos = row_off[bucket] + rank.reshape(n_tok)
    n_active = n_active.reshape(1)
    xs = _dispatch(pos, zrow, zflag, n_active, xa, n_tiles=n_tiles)
    ys = _moe(tile_a, tile_b, tile_x, n_active, xs, row(norm_ffn_g[l]),
              w_exp_gate[l].astype(BF16), w_exp_up[l].astype(BF16), w_exp_down[l].astype(BF16))
    out = _combine(pos, ys, n_tok=n_tok)
    return out.reshape(batch, seq, D_MODEL)
```

```python
import functools
import math

import jax
import jax.numpy as jnp
import numpy as np
from jax import lax
from jax.experimental import pallas as pl
from jax.experimental.pallas import tpu as pltpu

F32 = jnp.float32
BF16 = jnp.bfloat16
I32 = jnp.int32

D_MODEL = 1024
CHUNK = 64
N_MEM = 256
N_HEADS = 4
HEAD_DIM = 128
DIFF_QK_DIM = 64
CROSS_HEAD_DIM = 256
N_BUCKETS_T5 = 32
MAX_DISTANCE = 128
N_GROUPS = 4
EXPERTS_PER_GROUP = 4
N_EXPERTS = N_GROUPS * EXPERTS_PER_GROUP
D_EXPERT = 512
EPS = 1e-6
NEG_INF = -1e30
LAM_INIT = 0.8 - 0.6 * math.exp(-0.3 * 0)

PAIRS = ((0, 1), (0, 2), (0, 3), (1, 2), (1, 3), (2, 3))
N_PAIR_BUCKETS = N_GROUPS * len(PAIRS)

LANES = 128
VMEM_LIMIT_CAP = 56 * 1024 * 1024

PROJ_ROWS = 512
ATTN_TILE = 256
CROSS_ROWS = 256
MOE_ROWS = 256
AUX_COLS = LANES
ROW_W = D_MODEL + AUX_COLS
DMA_ROWS = 256


def _vmem_limit(nbytes):
    return int(min(max(nbytes * 5 // 4, 32 * 1024 * 1024), VMEM_LIMIT_CAP))


def _nt_dot(a, b):
    return lax.dot_general(a, b, (((1,), (1,)), ((), ())), preferred_element_type=F32)


def _rms(x, g):
    ms = jnp.mean(x * x, axis=-1, keepdims=True)
    return x * lax.rsqrt(ms + EPS) * g


def _mix_proj_kernel(x_ref, g_ref, w_ref, wft_ref, bf_ref, qg_ref, o_ref, cum_ref, carry_ref, *, tm):
    si = pl.program_id(1)
    h = _rms(x_ref[...], g_ref[...]).astype(BF16)
    lane = lax.broadcasted_iota(I32, (tm, HEAD_DIM), 1)
    lo = lane < DIFF_QK_DIM
    for kind in range(6):
        p = jnp.dot(h, w_ref[:, kind * 512:(kind + 1) * 512], preferred_element_type=F32)
        for hh in range(N_HEADS):
            ph = p[:, hh * HEAD_DIM:(hh + 1) * HEAD_DIM]
            if kind in (0, 1):
                sq = ph * ph
                s_lo = jnp.sum(jnp.where(lo, sq, 0.0), axis=-1, keepdims=True)
                s_hi = jnp.sum(jnp.where(lo, 0.0, sq), axis=-1, keepdims=True)
                ms = jnp.where(lo, s_lo, s_hi) * (1.0 / DIFF_QK_DIM)
                ph = ph * lax.rsqrt(ms + EPS) * qg_ref[kind:kind + 1, :]
            elif kind in (3, 4):
                ph = _rms(ph, qg_ref[kind:kind + 1, :])
            if kind in (0, 3):
                ph = ph * qg_ref[kind + 2:kind + 3, :]
            o_ref[0, kind * N_HEADS + hh] = ph.astype(BF16)

    z = _nt_dot(wft_ref[...], h) + bf_ref[...]
    logf = jnp.minimum(z, 0.0) - jnp.log(1.0 + jnp.exp(-jnp.abs(z)))
    lane8 = lax.broadcasted_iota(I32, (8, tm), 1)
    c = logf
    k = 1
    while k < tm:
        c = c + jnp.where(lane8 >= k, pltpu.roll(c, k, axis=1), 0.0)
        k *= 2

    @pl.when(si == 0)
    def _():
        carry_ref[...] = jnp.zeros_like(carry_ref)

    c = c + carry_ref[:, 0:1]
    cum_ref[0] = c
    carry_ref[...] = jnp.broadcast_to(c[:, tm - 1:tm], carry_ref.shape)


def _mix_proj(x2d, g, w_main, wft, bf, qg, *, batch, seq):
    tm = min(PROJ_ROWS, seq)
    ns = seq // tm
    est = 2 * (tm * D_MODEL * 4 + D_MODEL * 3072 * 2 + 24 * tm * HEAD_DIM * 2) + 8 * tm * D_MODEL
    return pl.pallas_call(
        functools.partial(_mix_proj_kernel, tm=tm),
        grid=(batch, ns),
        in_specs=[
            pl.BlockSpec((tm, D_MODEL), lambda b, s: (b * ns + s, 0)),
            pl.BlockSpec((1, D_MODEL), lambda b, s: (0, 0)),
            pl.BlockSpec((D_MODEL, 3072), lambda b, s: (0, 0)),
            pl.BlockSpec((8, D_MODEL), lambda b, s: (0, 0)),
            pl.BlockSpec((8, 1), lambda b, s: (0, 0)),
            pl.BlockSpec((8, HEAD_DIM), lambda b, s: (0, 0)),
        ],
        out_specs=[
            pl.BlockSpec((1, 24, tm, HEAD_DIM), lambda b, s: (b, 0, s, 0)),
            pl.BlockSpec((1, 8, tm), lambda b, s: (b, 0, s)),
        ],
        out_shape=[
            jax.ShapeDtypeStruct((batch, 24, seq, HEAD_DIM), BF16),
            jax.ShapeDtypeStruct((batch, 8, seq), F32),
        ],
        scratch_shapes=[pltpu.VMEM((8, LANES), F32)],
        compiler_params=pltpu.CompilerParams(
            dimension_semantics=("arbitrary", "arbitrary"), vmem_limit_bytes=_vmem_limit(est)),
        name="mix_proj",
    )(x2d, g, w_main, wft, bf, qg)


def _softmax_step(s, v, m_sc, l_sc, acc_sc):
    m_prev = m_sc[...]
    m_new = jnp.maximum(m_prev, jnp.max(s, axis=-1, keepdims=True))
    alpha = jnp.exp(m_prev - m_new)
    p = jnp.exp(s - m_new)
    l_sc[...] = alpha * l_sc[...] + jnp.sum(p, axis=-1, keepdims=True)
    acc_sc[...] = alpha * acc_sc[...] + jnp.dot(p.astype(BF16), v, preferred_element_type=F32)
    m_sc[...] = m_new


def _diff_attn_kernel(lam_ref, far_ref, q_ref, k_ref, v_ref, bias_ref, g_ref, o_ref,
                      m_sc, l_sc, acc_sc, *, t):
    hh = pl.program_id(1)
    qi = pl.program_id(2)
    q = q_ref[0, 0]
    lane = lax.broadcasted_iota(I32, q.shape, 1)
    zero = jnp.zeros_like(q)
    qq = jnp.concatenate([jnp.where(lane < DIFF_QK_DIM, q, zero),
                          jnp.where(lane < DIFF_QK_DIM, zero, q)], axis=0)
    m_sc[...] = jnp.full(m_sc.shape, NEG_INF, F32)
    l_sc[...] = jnp.zeros_like(l_sc)
    acc_sc[...] = jnp.zeros_like(acc_sc)

    def tile(j, bias):
        off = pl.multiple_of(j * t, t)
        k = k_ref[0, 0, pl.ds(off, t), :]
        v = v_ref[0, 0, pl.ds(off, t), :]
        s = _nt_dot(qq, k)
        if bias is None:
            s = s + far_ref[hh]
        else:
            s = (s.reshape(2, t, t) + bias[None]).reshape(2 * t, t)
        _softmax_step(s, v, m_sc, l_sc, acc_sc)

    def far_body(j, carry):
        tile(j, None)
        return carry

    lax.fori_loop(0, jnp.maximum(qi - 1, 0), far_body, 0)

    @pl.when(qi >= 1)
    def _():
        tile(qi - 1, bias_ref[0, 1])

    tile(qi, bias_ref[0, 0])

    lam = lam_ref[0]
    acc = acc_sc[...]
    l = l_sc[...]
    o = acc[:t] / l[:t] - lam * (acc[t:] / l[t:])
    o_ref[...] = (_rms(o, g_ref[...]) * (1.0 - LAM_INIT)).astype(BF16)


def _diff_attn(qkv, bias_tiles, bias_far, lam, subln_g, *, batch, seq):
    t = min(ATTN_TILE, seq)
    nq = seq // t
    est = 2 * (t * HEAD_DIM * 2 + 2 * seq * HEAD_DIM * 2 + 2 * t * t * 4) + 2 * t * (2 * LANES + HEAD_DIM) * 4 + 16 * t * t * 4
    return pl.pallas_call(
        functools.partial(_diff_attn_kernel, t=t),
        grid=(batch, N_HEADS, nq),
        in_specs=[
            pl.BlockSpec(memory_space=pltpu.SMEM),
            pl.BlockSpec(memory_space=pltpu.SMEM),
            pl.BlockSpec((1, 1, t, HEAD_DIM), lambda b, h, i: (b, h, i, 0)),
            pl.BlockSpec((1, 1, seq, HEAD_DIM), lambda b, h, i: (b, N_HEADS + h, 0, 0)),
            pl.BlockSpec((1, 1, seq, HEAD_DIM), lambda b, h, i: (b, 2 * N_HEADS + h, 0, 0)),
            pl.BlockSpec((1, 2, t, t), lambda b, h, i: (h, 0, 0, 0)),
            pl.BlockSpec((1, HEAD_DIM), lambda b, h, i: (0, 0)),
        ],
        out_specs=pl.BlockSpec((t, HEAD_DIM), lambda b, h, i: (b * nq + i, h)),
        out_shape=jax.ShapeDtypeStruct((batch * seq, N_HEADS * HEAD_DIM), BF16),
        scratch_shapes=[pltpu.VMEM((2 * t, 1), F32), pltpu.VMEM((2 * t, 1), F32),
                        pltpu.VMEM((2 * t, HEAD_DIM), F32)],
        compiler_params=pltpu.CompilerParams(
            dimension_semantics=("arbitrary", "arbitrary", "arbitrary"), vmem_limit_bytes=_vmem_limit(est)),
        name="diff_attn",
    )(lam, bias_far, qkv, qkv, qkv, bias_tiles, subln_g)


def _fox_attn_kernel(q_ref, k_ref, v_ref, c_ref, g_ref, o_ref, m_sc, l_sc, acc_sc, *, t):
    qi = pl.program_id(2)
    q = q_ref[0, 0]
    m_sc[...] = jnp.full(m_sc.shape, NEG_INF, F32)
    l_sc[...] = jnp.zeros_like(l_sc)
    acc_sc[...] = jnp.zeros_like(acc_sc)

    row = lax.broadcasted_iota(I32, (t, t), 0)
    col = lax.broadcasted_iota(I32, (t, t), 1)
    q_off = pl.multiple_of(qi * t, t)
    c_q_row = c_ref[0, 0, :, pl.ds(q_off, t)]
    c_q = jnp.sum(jnp.where(row == col, jnp.broadcast_to(c_q_row, (t, t)), 0.0), axis=-1, keepdims=True)

    def tile(j, causal):
        off = pl.multiple_of(j * t, t)
        k = k_ref[0, 0, pl.ds(off, t), :]
        v = v_ref[0, 0, pl.ds(off, t), :]
        c_k = c_ref[0, 0, :, pl.ds(off, t)]
        s = _nt_dot(q, k) + (c_q - c_k)
        if causal:
            s = jnp.where(col <= row, s, NEG_INF)
        _softmax_step(s, v, m_sc, l_sc, acc_sc)

    def body(j, carry):
        tile(j, False)
        return carry

    lax.fori_loop(0, qi, body, 0)
    tile(qi, True)

    o = acc_sc[...] / l_sc[...]
    o_ref[...] = _rms(o, g_ref[...]).astype(BF16)


def _fox_attn(qkv, cum4, out_g, *, batch, seq):
    t = min(ATTN_TILE, seq)
    nq = seq // t
    est = 2 * (t * HEAD_DIM * 2 + 2 * seq * HEAD_DIM * 2 + 8 * seq * 4) + t * (2 * LANES + HEAD_DIM) * 4 + 16 * t * t * 4
    return pl.pallas_call(
        functools.partial(_fox_attn_kernel, t=t),
        grid=(batch, N_HEADS, nq),
        in_specs=[
            pl.BlockSpec((1, 1, t, HEAD_DIM), lambda b, h, i: (b, 3 * N_HEADS + h, i, 0)),
            pl.BlockSpec((1, 1, seq, HEAD_DIM), lambda b, h, i: (b, 4 * N_HEADS + h, 0, 0)),
            pl.BlockSpec((1, 1, seq, HEAD_DIM), lambda b, h, i: (b, 5 * N_HEADS + h, 0, 0)),
            pl.BlockSpec((1, 1, 1, seq), lambda b, h, i: (b, h, 0, 0)),
            pl.BlockSpec((1, HEAD_DIM), lambda b, h, i: (0, 0)),
        ],
        out_specs=pl.BlockSpec((t, HEAD_DIM), lambda b, h, i: (b * nq + i, h)),
        out_shape=jax.ShapeDtypeStruct((batch * seq, N_HEADS * HEAD_DIM), BF16),
        scratch_shapes=[pltpu.VMEM((t, 1), F32), pltpu.VMEM((t, 1), F32), pltpu.VMEM((t, HEAD_DIM), F32)],
        compiler_params=pltpu.CompilerParams(
            dimension_semantics=("arbitrary", "arbitrary", "arbitrary"), vmem_limit_bytes=_vmem_limit(est)),
        name="fox_attn",
    )(qkv, qkv, qkv, cum4, out_g)


def _out_q_kernel(x_ref, md_ref, mf_ref, wo_ref, gc_ref, wq_ref, qg_ref, x1_ref, qc_ref):
    half = N_HEADS * HEAD_DIM
    x1 = (x_ref[...]
          + jnp.dot(md_ref[...], wo_ref[:half, :], preferred_element_type=F32)
          + jnp.dot(mf_ref[...], wo_ref[half:, :], preferred_element_type=F32))
    x1_ref[...] = x1
    hc = _rms(x1, gc_ref[...]).astype(BF16)
    q = jnp.dot(hc, wq_ref[...], preferred_element_type=F32)
    for hh in range(N_HEADS):
        sl = slice(hh * CROSS_HEAD_DIM, (hh + 1) * CROSS_HEAD_DIM)
        qh = _rms(q[:, sl], qg_ref[...]) * (CROSS_HEAD_DIM ** -0.5)
        qc_ref[:, sl] = qh.astype(BF16)


def _out_q(x2d, m_diff, m_fox, w_out, g_cross, w_cq, q_g):
    n = x2d.shape[0]
    tm = min(PROJ_ROWS, n)
    half = N_HEADS * HEAD_DIM
    est = 2 * (tm * D_MODEL * 4 * 2 + 2 * tm * half * 2 + 2 * D_MODEL * D_MODEL * 2 + tm * D_MODEL * 2) + 6 * tm * D_MODEL * 4
    return pl.pallas_call(
        _out_q_kernel,
        grid=(n // tm,),
        in_specs=[
            pl.BlockSpec((tm, D_MODEL), lambda i: (i, 0)),
            pl.BlockSpec((tm, half), lambda i: (i, 0)),
            pl.BlockSpec((tm, half), lambda i: (i, 0)),
            pl.BlockSpec((D_MODEL, D_MODEL), lambda i: (0, 0)),
            pl.BlockSpec((1, D_MODEL), lambda i: (0, 0)),
            pl.BlockSpec((D_MODEL, D_MODEL), lambda i: (0, 0)),
            pl.BlockSpec((1, CROSS_HEAD_DIM), lambda i: (0, 0)),
        ],
        out_specs=[pl.BlockSpec((tm, D_MODEL), lambda i: (i, 0)),
                   pl.BlockSpec((tm, D_MODEL), lambda i: (i, 0))],
        out_shape=[jax.ShapeDtypeStruct((n, D_MODEL), F32), jax.ShapeDtypeStruct((n, D_MODEL), BF16)],
        compiler_params=pltpu.CompilerParams(
            dimension_semantics=("arbitrary",), vmem_limit_bytes=_vmem_limit(est)),
        name="out_q",
    )(x2d, m_diff, m_fox, w_out, g_cross, w_cq, q_g)


def _mem_kv_kernel(mem_ref, gm_ref, w_ref, kg_ref, k_ref, v_ref):
    mn = _rms(mem_ref[...], gm_ref[...]).astype(BF16)
    kv = jnp.dot(mn, w_ref[...], preferred_element_type=F32)
    for hh in range(N_HEADS):
        sl = slice(hh * CROSS_HEAD_DIM, (hh + 1) * CROSS_HEAD_DIM)
        k_ref[0, :, sl] = _rms(kv[:, sl], kg_ref[...]).astype(BF16)
    v_ref[0] = kv[:, D_MODEL:].astype(BF16)


def _mem_kv(mem2d, g_mem, w_ckv, k_g, *, batch):
    est = 2 * (N_MEM * D_MODEL * 4 + D_MODEL * 2 * D_MODEL * 2 + 2 * N_MEM * D_MODEL * 2) + 4 * N_MEM * 2 * D_MODEL * 4
    return pl.pallas_call(
        _mem_kv_kernel,
        grid=(batch,),
        in_specs=[
            pl.BlockSpec((N_MEM, D_MODEL), lambda b: (b, 0)),
            pl.BlockSpec((1, D_MODEL), lambda b: (0, 0)),
            pl.BlockSpec((D_MODEL, 2 * D_MODEL), lambda b: (0, 0)),
            pl.BlockSpec((1, CROSS_HEAD_DIM), lambda b: (0, 0)),
        ],
        out_specs=[pl.BlockSpec((1, N_MEM, D_MODEL), lambda b: (b, 0, 0)),
                   pl.BlockSpec((1, N_MEM, D_MODEL), lambda b: (b, 0, 0))],
        out_shape=[jax.ShapeDtypeStruct((batch, N_MEM, D_MODEL), BF16),
                   jax.ShapeDtypeStruct((batch, N_MEM, D_MODEL), BF16)],
        compiler_params=pltpu.CompilerParams(
            dimension_semantics=("arbitrary",), vmem_limit_bytes=_vmem_limit(est)),
        name="mem_kv",
    )(mem2d, g_mem, w_ckv, k_g)


def _route(logits_t):
    gl = logits_t[0:N_GROUPS]
    gmax = jnp.max(gl, axis=0, keepdims=True)
    eg = jnp.exp(gl - gmax)
    p_group = eg / jnp.sum(eg, axis=0, keepdims=True)
    p_g = jnp.max(p_group, axis=0, keepdims=True)
    g_idx = jnp.full_like(p_g, float(N_GROUPS))
    for g in reversed(range(N_GROUPS)):
        g_idx = jnp.where(p_group[g:g + 1] == p_g, float(g), g_idx)

    sel = []
    for j in range(EXPERTS_PER_GROUP):
        acc = jnp.zeros_like(p_g)
        for g in range(N_GROUPS):
            r = N_GROUPS + g * EXPERTS_PER_GROUP + j
            acc = jnp.where(g_idx == float(g), logits_t[r:r + 1], acc)
        sel.append(acc)
    smax = functools.reduce(jnp.maximum, sel)
    es = [jnp.exp(s - smax) for s in sel]
    den = functools.reduce(jnp.add, es)
    p_in = [e / den for e in es]

    v0 = functools.reduce(jnp.maximum, p_in)
    i0 = jnp.full_like(v0, float(EXPERTS_PER_GROUP))
    for j in reversed(range(EXPERTS_PER_GROUP)):
        i0 = jnp.where(p_in[j] == v0, float(j), i0)
    rest = [jnp.where(i0 == float(j), -1.0, p_in[j]) for j in range(EXPERTS_PER_GROUP)]
    v1 = functools.reduce(jnp.maximum, rest)
    i1 = jnp.full_like(v1, float(EXPERTS_PER_GROUP))
    for j in reversed(range(EXPERTS_PER_GROUP)):
        i1 = jnp.where(rest[j] == v1, float(j), i1)

    tot = v0 + v1
    w0 = p_g * (v0 / tot)
    w1 = p_g * (v1 / tot)
    first = i0 < i1
    a = jnp.where(first, i0, i1)
    b = jnp.where(first, i1, i0)
    wa = jnp.where(first, w0, w1)
    wb = jnp.where(first, w1, w0)
    pair = jnp.where(a == 0.0, 0.0, jnp.where(a == 1.0, 3.0, 5.0)) + (b - a - 1.0)
    bucket = g_idx * float(len(PAIRS)) + pair
    return bucket, wa, wb


def _cross_router_kernel(q_ref, k_ref, v_ref, x1_ref, wco_ref, gf_ref, whi_ref, wlo_ref, rb_ref,
                         xa_ref, rt_ref, *, tm):
    q = q_ref[...]
    k = k_ref[0]
    v = v_ref[0]
    outs = []
    for hh in range(N_HEADS):
        sl = slice(hh * CROSS_HEAD_DIM, (hh + 1) * CROSS_HEAD_DIM)
        s = _nt_dot(q[:, sl], k[:, sl])
        p = jnp.exp(s - jnp.max(s, axis=-1, keepdims=True))
        l = jnp.sum(p, axis=-1, keepdims=True)
        o = jnp.dot(p.astype(BF16), v[:, sl], preferred_element_type=F32) / l
        outs.append(o.astype(BF16))
    o = jnp.concatenate(outs, axis=1)
    x2 = x1_ref[...] + jnp.dot(o, wco_ref[...], preferred_element_type=F32)
    xa_ref[:, :D_MODEL] = x2

    h3 = _rms(x2, gf_ref[...])
    hi = h3.astype(BF16)
    lo = (h3 - hi.astype(F32)).astype(BF16)
    lt = _nt_dot(whi_ref[...], hi) + (_nt_dot(whi_ref[...], lo) + _nt_dot(wlo_ref[...], hi))
    lt = lt + rb_ref[...]
    bucket, wa, wb = _route(lt)

    sub = lax.broadcasted_iota(I32, (8, tm), 0)
    rows = jnp.where(sub == 0, bucket, jnp.where(sub == 1, wa, jnp.where(sub == 2, wb, 0.0)))
    rt_ref[0] = rows
    sub_a = lax.broadcasted_iota(I32, (AUX_COLS, tm), 0)
    aux_t = jnp.where(sub_a == 0, bucket, jnp.where(sub_a == 1, wa, jnp.where(sub_a == 2, wb, 0.0)))
    xa_ref[:, D_MODEL:] = aux_t.T


def _cross_router(qc, kc, vc, x1, w_co, g_ffn, wr_hi, wr_lo, rb, *, batch, seq):
    tm = min(CROSS_ROWS, seq)
    ns = seq // tm
    n = batch * seq
    est = 2 * (tm * D_MODEL * 2 + 2 * N_MEM * D_MODEL * 2 + tm * D_MODEL * 4 + D_MODEL * D_MODEL * 2
               + tm * ROW_W * 4) + 10 * tm * D_MODEL * 4
    return pl.pallas_call(
        functools.partial(_cross_router_kernel, tm=tm),
        grid=(batch, ns),
        in_specs=[
            pl.BlockSpec((tm, D_MODEL), lambda b, s: (b * ns + s, 0)),
            pl.BlockSpec((1, N_MEM, D_MODEL), lambda b, s: (b, 0, 0)),
            pl.BlockSpec((1, N_MEM, D_MODEL), lambda b, s: (b, 0, 0)),
            pl.BlockSpec((tm, D_MODEL), lambda b, s: (b * ns + s, 0)),
            pl.BlockSpec((D_MODEL, D_MODEL), lambda b, s: (0, 0)),
            pl.BlockSpec((1, D_MODEL), lambda b, s: (0, 0)),
            pl.BlockSpec((32, D_MODEL), lambda b, s: (0, 0)),
            pl.BlockSpec((32, D_MODEL), lambda b, s: (0, 0)),
            pl.BlockSpec((32, 1), lambda b, s: (0, 0)),
        ],
        out_specs=[pl.BlockSpec((tm, ROW_W), lambda b, s: (b * ns + s, 0)),
                   pl.BlockSpec((1, 8, tm), lambda b, s: (b * ns + s, 0, 0))],
        out_shape=[jax.ShapeDtypeStruct((n, ROW_W), F32),
                   jax.ShapeDtypeStruct((n // tm, 8, tm), F32)],
        compiler_params=pltpu.CompilerParams(
            dimension_semantics=("arbitrary", "arbitrary"), vmem_limit_bytes=_vmem_limit(est)),
        name="cross_router",
    )(qc, kc, vc, x1, w_co, g_ffn, wr_hi, wr_lo, rb)


def _rank_kernel(rt_ref, rank_ref, cnt_ref, carry_sc, *, tm):
    @pl.when(pl.program_id(0) == 0)
    def _():
        carry_sc[...] = jnp.zeros_like(carry_sc)

    bucket = rt_ref[0, 0:1, :]
    sub = lax.broadcasted_iota(I32, (32, tm), 0).astype(F32)
    hit = sub == bucket
    r = lax.broadcasted_iota(I32, (tm, tm), 0)
    c = lax.broadcasted_iota(I32, (tm, tm), 1)
    upper = jnp.where(r <= c, 1.0, 0.0).astype(BF16)
    cum = jnp.dot(jnp.where(hit, 1.0, 0.0).astype(BF16), upper, preferred_element_type=F32)
    carry = carry_sc[:, 0:1]
    rank = jnp.sum(jnp.where(hit, cum - 1.0 + carry, 0.0), axis=0, keepdims=True)
    rank_ref[0] = rank.astype(I32)
    new_carry = carry_sc[...] + jnp.broadcast_to(cum[:, tm - 1:tm], carry_sc.shape)
    carry_sc[...] = new_carry
    cnt_ref[...] = new_carry


def _rank(rt):
    nt, _, tm = rt.shape
    return pl.pallas_call(
        functools.partial(_rank_kernel, tm=tm),
        grid=(nt,),
        in_specs=[pl.BlockSpec((1, 8, tm), lambda i: (i, 0, 0))],
        out_specs=[pl.BlockSpec((1, 1, tm), lambda i: (i, 0, 0)),
                   pl.BlockSpec((32, LANES), lambda i: (0, 0))],
        out_shape=[jax.ShapeDtypeStruct((nt, 1, tm), I32), jax.ShapeDtypeStruct((32, LANES), F32)],
        scratch_shapes=[pltpu.VMEM((32, LANES), F32)],
        compiler_params=pltpu.CompilerParams(dimension_semantics=("arbitrary",)),
        name="rank",
    )(rt)


def _row_copies(pos_ref, base, hbm, vmem, sem, n_rows, *, to_hbm, wait):
    def body(r, carry):
        h = hbm.at[pl.ds(pos_ref[base + r], 1)]
        v = vmem.at[pl.ds(r, 1)]
        cp = pltpu.make_async_copy(v, h, sem) if to_hbm else pltpu.make_async_copy(h, v, sem)
        if wait:
            cp.wait()
        else:
            cp.start()
        return carry
    lax.fori_loop(0, n_rows, body, 0, unroll=8)


def _dispatch_kernel(pos_ref, zrow_ref, zflag_ref, nact_ref, xa_ref, xs_ref, zbuf, zsem, sem, *,
                     n_tiles, rows):
    @pl.when(pl.program_id(0) == 0)
    def _():
        _zero_partial_tiles(zrow_ref, zflag_ref, nact_ref, xs_ref, zbuf, zsem, n_tiles=n_tiles)

    base = pl.program_id(0) * rows
    _row_copies(pos_ref, base, xs_ref, xa_ref, sem, rows, to_hbm=True, wait=False)
    _row_copies(pos_ref, base, xs_ref, xa_ref, sem, rows, to_hbm=True, wait=True)


def _zero_partial_tiles(zrow_ref, zflag_ref, nact_ref, xs_ref, zbuf, zsem, *, n_tiles):
    zbuf[...] = jnp.zeros_like(zbuf)

    def zero_copy(row):
        return pltpu.make_async_copy(zbuf, xs_ref.at[pl.ds(pl.multiple_of(row, MOE_ROWS), MOE_ROWS)], zsem)

    def bucket_tiles(op):
        def body(b, carry):
            @pl.when(zflag_ref[b] > 0)
            def _():
                op(zero_copy(zrow_ref[b]))
            return carry
        lax.fori_loop(0, N_PAIR_BUCKETS, body, 0)

    def unused_tiles(op):
        def body(i, carry):
            op(zero_copy(i * MOE_ROWS))
            return carry
        lax.fori_loop(nact_ref[0], n_tiles, body, 0)

    bucket_tiles(lambda cp: cp.start())
    unused_tiles(lambda cp: cp.start())
    bucket_tiles(lambda cp: cp.wait())
    unused_tiles(lambda cp: cp.wait())


def _dispatch(pos, zrow, zflag, n_active, xa, *, n_tiles):
    n_tok = xa.shape[0]
    n_sorted = n_tiles * MOE_ROWS
    rows = min(DMA_ROWS, n_tok)
    return pl.pallas_call(
        functools.partial(_dispatch_kernel, n_tiles=n_tiles, rows=rows),
        grid_spec=pltpu.PrefetchScalarGridSpec(
            num_scalar_prefetch=4,
            grid=(n_tok // rows,),
            in_specs=[pl.BlockSpec((rows, ROW_W), lambda i, *_: (i, 0))],
            out_specs=pl.BlockSpec(memory_space=pl.ANY),
            scratch_shapes=[pltpu.VMEM((MOE_ROWS, ROW_W), F32), pltpu.SemaphoreType.DMA(()),
                            pltpu.SemaphoreType.DMA(())],
        ),
        out_shape=jax.ShapeDtypeStruct((n_sorted, ROW_W), F32),
        compiler_params=pltpu.CompilerParams(dimension_semantics=("arbitrary",), has_side_effects=True),
        name="dispatch",
    )(pos, zrow, zflag, n_active, xa)


def _combine_kernel(pos_ref, ys_ref, out_ref, sem, *, rows):
    base = pl.program_id(0) * rows
    _row_copies(pos_ref, base, ys_ref, out_ref, sem, rows, to_hbm=False, wait=False)
    _row_copies(pos_ref, base, ys_ref, out_ref, sem, rows, to_hbm=False, wait=True)


def _combine(pos, ys, *, n_tok):
    rows = min(DMA_ROWS, n_tok)
    return pl.pallas_call(
        functools.partial(_combine_kernel, rows=rows),
        grid_spec=pltpu.PrefetchScalarGridSpec(
            num_scalar_prefetch=1,
            grid=(n_tok // rows,),
            in_specs=[pl.BlockSpec(memory_space=pl.ANY)],
            out_specs=pl.BlockSpec((rows, D_MODEL), lambda i, *_: (i, 0)),
            scratch_shapes=[pltpu.SemaphoreType.DMA(())],
        ),
        out_shape=jax.ShapeDtypeStruct((n_tok, D_MODEL), F32),
        compiler_params=pltpu.CompilerParams(dimension_semantics=("arbitrary",), has_side_effects=True),
        name="combine",
    )(pos, ys)


def _moe_kernel(ta_ref, tb_ref, xi_ref, nact_ref, xs_ref, gf_ref, wga_ref, wua_ref, wda_ref,
                wgb_ref, wub_ref, wdb_ref, ys_ref):
    i = pl.program_id(0)

    @pl.when(i < nact_ref[0])
    def _():
        x2 = xs_ref[:, :D_MODEL]
        aux = xs_ref[:, D_MODEL:]
        wa = aux[:, 1:2]
        wb = aux[:, 2:3]
        h = _rms(x2, gf_ref[...]).astype(BF16)

        def mlp(wg, wu, wd):
            g = jnp.dot(h, wg[0], preferred_element_type=F32)
            u = jnp.dot(h, wu[0], preferred_element_type=F32)
            act = (g / (1.0 + jnp.exp(-g))) * u
            return jnp.dot(act.astype(BF16), wd[0], preferred_element_type=F32)

        ys_ref[...] = x2 + (wa * mlp(wga_ref, wua_ref, wda_ref) + wb * mlp(wgb_ref, wub_ref, wdb_ref))

    @pl.when(i >= nact_ref[0])
    def _():
        ys_ref[...] = jnp.zeros_like(ys_ref)


def _moe(tile_a, tile_b, tile_x, n_active, xs, g_ffn, w_gate, w_up, w_down):
    n_tiles = xs.shape[0] // MOE_ROWS
    t = MOE_ROWS
    wa_map = lambda i, ta, tb, xi, na: (ta[i], 0, 0)
    wb_map = lambda i, ta, tb, xi, na: (tb[i], 0, 0)
    est = 2 * (t * ROW_W * 4 + 6 * D_MODEL * D_EXPERT * 2 + t * D_MODEL * 4) + 10 * t * D_MODEL * 4
    return pl.pallas_call(
        _moe_kernel,
        grid_spec=pltpu.PrefetchScalarGridSpec(
            num_scalar_prefetch=4,
            grid=(n_tiles,),
            in_specs=[
                pl.BlockSpec((t, ROW_W), lambda i, ta, tb, xi, na: (xi[i], 0)),
                pl.BlockSpec((1, D_MODEL), lambda i, ta, tb, xi, na: (0, 0)),
                pl.BlockSpec((1, D_MODEL, D_EXPERT), wa_map),
                pl.BlockSpec((1, D_MODEL, D_EXPERT), wa_map),
                pl.BlockSpec((1, D_EXPERT, D_MODEL), wa_map),
                pl.BlockSpec((1, D_MODEL, D_EXPERT), wb_map),
                pl.BlockSpec((1, D_MODEL, D_EXPERT), wb_map),
                pl.BlockSpec((1, D_EXPERT, D_MODEL), wb_map),
            ],
            out_specs=pl.BlockSpec((t, D_MODEL), lambda i, ta, tb, xi, na: (i, 0)),
        ),
        out_shape=jax.ShapeDtypeStruct((n_tiles * t, D_MODEL), F32),
        compiler_params=pltpu.CompilerParams(
            dimension_semantics=("arbitrary",), vmem_limit_bytes=_vmem_limit(est)),
        name="moe",
    )(tile_a, tile_b, tile_x, n_active, xs, g_ffn, w_gate, w_up, w_down, w_gate, w_up, w_down)


def _t5_bucket(rel):
    nb = N_BUCKETS_T5 // 2
    max_exact = nb // 2
    ret = (rel > 0).astype(I32) * nb
    n = jnp.abs(rel)
    nf = jnp.maximum(n, 1).astype(F32)
    large = max_exact + (jnp.log(nf / max_exact) / math.log(MAX_DISTANCE / max_exact)
                         * (nb - max_exact)).astype(I32)
    large = jnp.minimum(large, nb - 1)
    return ret + jnp.where(n < max_exact, n, large)


def _diff_bias(rel_bias, t):
    assert t % CHUNK == 0 and t >= MAX_DISTANCE
    n_rel = 3 * t
    vec = rel_bias[_t5_bucket(jnp.arange(n_rel, dtype=I32) - (2 * t - 1))].astype(F32).T
    toe = jnp.tile(vec, (1, t))[:, :t * (n_rel - 1)].reshape(N_HEADS, t, n_rel - 1)
    qpos = jnp.arange(t, dtype=I32)[:, None]
    kpos = jnp.arange(t, dtype=I32)[None, :]
    b0 = toe[:, :, 2 * t - 1:3 * t - 1]
    b0 = jnp.where((kpos // CHUNK <= qpos // CHUNK)[None], b0, NEG_INF)
    b1 = toe[:, :, t - 1:2 * t - 1]
    far = rel_bias[_t5_bucket(jnp.asarray(-2 * t, I32))].astype(F32)
    return jnp.stack([b0, b1], axis=1), far


def _tile_plan(counts, n_tiles):
    t = MOE_ROWS
    tiles = (counts + t - 1) // t
    ends = jnp.cumsum(tiles)
    starts = ends - tiles
    n_active = ends[-1]
    tile_idx = jnp.minimum(jnp.arange(n_tiles, dtype=I32), n_active - 1)
    tile_bucket = jnp.minimum(jnp.searchsorted(ends, tile_idx, side="right"), N_PAIR_BUCKETS - 1).astype(I32)
    pa = jnp.asarray([p[0] for p in PAIRS], I32)
    pb = jnp.asarray([p[1] for p in PAIRS], I32)
    grp = tile_bucket // len(PAIRS)
    pair = tile_bucket % len(PAIRS)
    tile_a = grp * EXPERTS_PER_GROUP + pa[pair]
    tile_b = grp * EXPERTS_PER_GROUP + pb[pair]
    row_off = (starts * t).astype(I32)
    zrow = ((ends - 1) * t).astype(I32)
    zflag = (counts > 0).astype(I32)
    return row_off, zrow, zflag, tile_a.astype(I32), tile_b.astype(I32), tile_idx.astype(I32), n_active.astype(I32)


def kernel(x, mem, rel_bias, norm_mix_g, w_in, b_forget, diff_q_norm_g, diff_k_norm_g, diff_lambda_q1, diff_lambda_k1, diff_lambda_q2, diff_lambda_k2, diff_subln_g, fox_q_norm_g, fox_k_norm_g, fox_out_norm_g, w_out, norm_cross_g, norm_mem_g, w_cq, w_ckv, cross_q_norm_g, cross_k_norm_g, w_co, norm_ffn_g, w_group_router, b_group_router, w_expert_router, b_expert_router, w_exp_gate, w_exp_up, w_exp_down):
    batch, seq, d = x.shape
    assert d == D_MODEL and norm_mix_g.shape[0] == 1 and mem.shape[1] == N_MEM
    n_tok = batch * seq
    assert seq % ATTN_TILE == 0 or seq < ATTN_TILE
    l = 0

    w_main = w_in[l, :, :3072].astype(BF16)
    wft = jnp.zeros((8, D_MODEL), F32).at[:N_HEADS].set(w_in[l, :, 3072:].T).astype(BF16)
    bf = jnp.zeros((8, 1), F32).at[:N_HEADS, 0].set(b_forget[l].astype(F32))
    ones = jnp.ones((HEAD_DIM,), F32)
    qg = jnp.stack([
        jnp.tile(diff_q_norm_g[l].astype(F32), 2), jnp.tile(diff_k_norm_g[l].astype(F32), 2),
        ones * DIFF_QK_DIM ** -0.5,
        fox_q_norm_g[l].astype(F32), fox_k_norm_g[l].astype(F32),
        ones * HEAD_DIM ** -0.5, ones, ones])
    lam = (jnp.exp(jnp.sum(diff_lambda_q1[l].astype(F32) * diff_lambda_k1[l].astype(F32)))
           - jnp.exp(jnp.sum(diff_lambda_q2[l].astype(F32) * diff_lambda_k2[l].astype(F32)))
           + LAM_INIT).reshape(1)
    t_attn = min(ATTN_TILE, seq)
    bias_tiles, bias_far = _diff_bias(rel_bias, t_attn)
    wr = jnp.zeros((32, D_MODEL), F32)
    wr = wr.at[:N_GROUPS].set(w_group_router[l].T).at[N_GROUPS:N_GROUPS + N_EXPERTS].set(w_expert_router[l].T)
    wr_hi = wr.astype(BF16)
    wr_lo = (wr - wr_hi.astype(F32)).astype(BF16)
    rb = jnp.zeros((32, 1), F32)
    rb = rb.at[:N_GROUPS, 0].set(b_group_router[l]).at[N_GROUPS:N_GROUPS + N_EXPERTS, 0].set(b_expert_router[l])

    row = lambda v: v.astype(F32).reshape(1, -1)
    x2d = x.reshape(n_tok, D_MODEL)

    qkv, cum = _mix_proj(x2d, row(norm_mix_g[l]), w_main, wft, bf, qg, batch=batch, seq=seq)
    m_diff = _diff_attn(qkv, bias_tiles, bias_far, lam, row(diff_subln_g[l]), batch=batch, seq=seq)
    cum4 = cum[:, :N_HEADS].reshape(batch, N_HEADS, 1, seq)
    m_fox = _fox_attn(qkv, cum4, row(fox_out_norm_g[l]), batch=batch, seq=seq)
    x1, qc = _out_q(x2d, m_diff, m_fox, w_out[l].astype(BF16), row(norm_cross_g[l]),
                    w_cq[l].astype(BF16), row(cross_q_norm_g[l]))

    kc, vc = _mem_kv(mem.reshape(batch * N_MEM, D_MODEL), row(norm_mem_g[l]), w_ckv[l].astype(BF16),
                     row(cross_k_norm_g[l]), batch=batch)
    xa, rt = _cross_router(qc, kc, vc, x1, w_co[l].astype(BF16), row(norm_ffn_g[l]), wr_hi, wr_lo, rb,
                           batch=batch, seq=seq)

    rank, cnt = _rank(rt)
    counts = cnt[:N_PAIR_BUCKETS, 0].astype(I32)
    n_tiles = n_tok // MOE_ROWS + N_PAIR_BUCKETS
    row_off, zrow, zflag, tile_a, tile_b, tile_x, n_active = _tile_plan(counts, n_tiles)
    bucket = rt[:, 0, :].reshape(n_tok).astype(I32)
    pos = row_off[bucket] + rank.reshape(n_tok)
    n_active = n_active.reshape(1)
    xs = _dispatch(pos, zrow, zflag, n_active, xa, n_tiles=n_tiles)
    ys = _moe(tile_a, tile_b, tile_x, n_active, xs, row(norm_ffn_g[l]),
              w_exp_gate[l].astype(BF16), w_exp_up[l].astype(BF16), w_exp_down[l].astype(BF16))
    out = _combine(pos, ys, n_tok=n_tok)
    return out.reshape(batch, seq, D_MODEL)
```

```python
import functools
import math

import jax
import jax.numpy as jnp
import numpy as np
from jax import lax
from jax.experimental import pallas as pl
from jax.experimental.pallas import tpu as pltpu

F32 = jnp.float32
BF16 = jnp.bfloat16
I32 = jnp.int32

D_MODEL = 1024
CHUNK = 64
N_MEM = 256
N_HEADS = 4
HEAD_DIM = 128
DIFF_QK_DIM = 64
CROSS_HEAD_DIM = 256
N_BUCKETS_T5 = 32
MAX_DISTANCE = 128
N_GROUPS = 4
EXPERTS_PER_GROUP = 4
N_EXPERTS = N_GROUPS * EXPERTS_PER_GROUP
D_EXPERT = 512
EPS = 1e-6
NEG_INF = -1e30
LAM_INIT = 0.8 - 0.6 * math.exp(-0.3 * 0)
LOG2E = math.log2(math.e)

PAIRS = ((0, 1), (0, 2), (0, 3), (1, 2), (1, 3), (2, 3))
N_PAIR_BUCKETS = N_GROUPS * len(PAIRS)

LANES = 128
VMEM_LIMIT_CAP = 56 * 1024 * 1024

PROJ_ROWS = 512
ATTN_TILE = 256
CROSS_ROWS = 256
MOE_ROWS = 256
AUX_COLS = LANES
ROW_W = D_MODEL + AUX_COLS
DMA_ROWS = 256


def _vmem_limit(nbytes):
    return int(min(max(nbytes * 5 // 4, 32 * 1024 * 1024), VMEM_LIMIT_CAP))


def _nt_dot(a, b):
    return lax.dot_general(a, b, (((1,), (1,)), ((), ())), preferred_element_type=F32)


def _rms(x, g):
    ms = jnp.mean(x * x, axis=-1, keepdims=True)
    return x * lax.rsqrt(ms + EPS) * g


def _mix_proj_kernel(x_ref, g_ref, w_ref, wvt_ref, wft_ref, bf_ref, qg_ref, o_ref, vt_ref, cum_ref, carry_ref,
                     *, tm):
    si = pl.program_id(1)
    h = _rms(x_ref[...], g_ref[...]).astype(BF16)
    lane = lax.broadcasted_iota(I32, (tm, HEAD_DIM), 1)
    lo = lane < DIFF_QK_DIM
    for kind in range(4):
        p = jnp.dot(h, w_ref[:, kind * 512:(kind + 1) * 512], preferred_element_type=F32)
        for hh in range(N_HEADS):
            ph = p[:, hh * HEAD_DIM:(hh + 1) * HEAD_DIM]
            if kind in (0, 1):
                sq = ph * ph
                s_lo = jnp.sum(jnp.where(lo, sq, 0.0), axis=-1, keepdims=True)
                s_hi = jnp.sum(jnp.where(lo, 0.0, sq), axis=-1, keepdims=True)
                ms = jnp.where(lo, s_lo, s_hi) * (1.0 / DIFF_QK_DIM)
                ph = ph * lax.rsqrt(ms + EPS) * qg_ref[kind:kind + 1, :]
            else:
                ph = _rms(ph, qg_ref[kind:kind + 1, :])
            o_ref[0, kind * N_HEADS + hh] = ph.astype(BF16)
    for kind in range(2):
        pt = _nt_dot(wvt_ref[kind * 512:(kind + 1) * 512, :], h)
        for hh in range(N_HEADS):
            vt_ref[0, kind * N_HEADS + hh] = pt[hh * HEAD_DIM:(hh + 1) * HEAD_DIM, :].astype(BF16)

    z = _nt_dot(wft_ref[...], h) + bf_ref[...]
    logf = jnp.minimum(z, 0.0) - jnp.log(1.0 + jnp.exp(-jnp.abs(z)))
    lane8 = lax.broadcasted_iota(I32, (8, tm), 1)
    c = logf * LOG2E
    k = 1
    while k < tm:
        c = c + jnp.where(lane8 >= k, pltpu.roll(c, k, axis=1), 0.0)
        k *= 2

    @pl.when(si == 0)
    def _():
        carry_ref[...] = jnp.zeros_like(carry_ref)

    c = c + carry_ref[:, 0:1]
    cum_ref[0] = c
    carry_ref[...] = jnp.broadcast_to(c[:, tm - 1:tm], carry_ref.shape)


def _mix_proj(x2d, g, w_qk, w_vt, wft, bf, qg, *, batch, seq):
    tm = min(PROJ_ROWS, seq)
    ns = seq // tm
    est = 2 * (tm * D_MODEL * 4 + D_MODEL * 3072 * 2 + 24 * tm * HEAD_DIM * 2) + 8 * tm * D_MODEL
    return pl.pallas_call(
        functools.partial(_mix_proj_kernel, tm=tm),
        grid=(batch, ns),
        in_specs=[
            pl.BlockSpec((tm, D_MODEL), lambda b, s: (b * ns + s, 0)),
            pl.BlockSpec((1, D_MODEL), lambda b, s: (0, 0)),
            pl.BlockSpec((D_MODEL, 2048), lambda b, s: (0, 0)),
            pl.BlockSpec((1024, D_MODEL), lambda b, s: (0, 0)),
            pl.BlockSpec((8, D_MODEL), lambda b, s: (0, 0)),
            pl.BlockSpec((8, 1), lambda b, s: (0, 0)),
            pl.BlockSpec((8, HEAD_DIM), lambda b, s: (0, 0)),
        ],
        out_specs=[
            pl.BlockSpec((1, 16, tm, HEAD_DIM), lambda b, s: (b, 0, s, 0)),
            pl.BlockSpec((1, 8, HEAD_DIM, tm), lambda b, s: (b, 0, 0, s)),
            pl.BlockSpec((1, 8, tm), lambda b, s: (b, 0, s)),
        ],
        out_shape=[
            jax.ShapeDtypeStruct((batch, 16, seq, HEAD_DIM), BF16),
            jax.ShapeDtypeStruct((batch, 8, HEAD_DIM, seq), BF16),
            jax.ShapeDtypeStruct((batch, 8, seq), F32),
        ],
        scratch_shapes=[pltpu.VMEM((8, LANES), F32)],
        compiler_params=pltpu.CompilerParams(
            dimension_semantics=("arbitrary", "arbitrary"), vmem_limit_bytes=_vmem_limit(est)),
        name="mix_proj",
    )(x2d, g, w_qk, w_vt, wft, bf, qg)


def _init_softmax(m_sc, l_sc, acc_sc):
    m_sc[...] = jnp.full(m_sc.shape, NEG_INF, F32)
    l_sc[...] = jnp.zeros_like(l_sc)
    acc_sc[...] = jnp.zeros_like(acc_sc)


N_CHAINS = 3 * N_HEADS


def _attn_kernel(lam_ref, dq_ref, dk_ref, dvt_ref, fq_ref, fk_ref, fvt_ref, bias_ref, c_ref,
                 gd_ref, gf_ref, o_ref, m_sc, l_sc, acc_sc, qm_sc, ccol_sc, s_sc, mt_sc, *, t, seq):
    qi = pl.program_id(1)
    _init_softmax(m_sc, l_sc, acc_sc)
    key = lax.broadcasted_iota(I32, (t, t), 0)
    qry = lax.broadcasted_iota(I32, (t, t), 1)

    lane = lax.broadcasted_iota(I32, (t, HEAD_DIM), 1)
    for hh in range(N_HEADS):
        q = dq_ref[0, hh]
        zero = jnp.zeros_like(q)
        qm_sc[2 * hh] = jnp.where(lane < DIFF_QK_DIM, q, zero)
        qm_sc[2 * hh + 1] = jnp.where(lane < DIFF_QK_DIM, zero, q)

    @pl.when(qi == 0)
    def _():
        for hh in range(N_HEADS):
            for j in range(seq // t):
                row = c_ref[0, hh, :, j * t:(j + 1) * t]
                ccol_sc[hh, j * t:(j + 1) * t, :] = jnp.sum(
                    jnp.where(key == qry, jnp.broadcast_to(row, (t, t)), 0.0), axis=1, keepdims=True)

    q_off = pl.multiple_of(qi * t, t)

    def tile(j, where):
        off = pl.multiple_of(j * t, t)

        for hh in range(N_HEADS):
            k = dk_ref[0, hh, pl.ds(off, t), :]
            for c in range(2):
                s_t = _nt_dot(k, qm_sc[2 * hh + c])
                if where != "far":
                    s_t = s_t + bias_ref[hh, 1 if where == "near" else 0]
                s_sc[2 * hh + c] = s_t
                mt_sc[2 * hh + c] = jnp.max(s_t, axis=0, keepdims=True)
        for hh in range(N_HEADS):
            k = fk_ref[0, hh, pl.ds(off, t), :]
            c_k = ccol_sc[hh, pl.ds(off, t), :]
            s_t = _nt_dot(k, fq_ref[0, hh]) - c_k
            if where == "diag":
                s_t = jnp.where(key <= qry, s_t, NEG_INF)
            s_sc[2 * N_HEADS + hh] = s_t
            mt_sc[2 * N_HEADS + hh] = jnp.max(s_t, axis=0, keepdims=True) + c_ref[0, hh, :, pl.ds(q_off, t)]

        for slot in range(N_CHAINS):
            vt_ref = dvt_ref if slot < 2 * N_HEADS else fvt_ref
            hh = slot // 2 if slot < 2 * N_HEADS else slot - 2 * N_HEADS
            v_t = vt_ref[0, hh, :, pl.ds(off, t)]
            m_prev = m_sc[slot]
            m_new = jnp.maximum(m_prev, mt_sc[slot])
            alpha = jnp.exp2(m_prev - m_new)
            if slot < 2 * N_HEADS:
                p_t = jnp.exp2(s_sc[slot] - m_new)
            else:
                p_t = jnp.exp2(s_sc[slot] - (m_new - c_ref[0, hh, :, pl.ds(q_off, t)]))
            l_sc[slot] = alpha * l_sc[slot] + jnp.sum(p_t, axis=0, keepdims=True)
            acc_sc[slot] = alpha * acc_sc[slot] + jnp.dot(v_t, p_t.astype(BF16), preferred_element_type=F32)
            m_sc[slot] = m_new

    def far_body(j, carry):
        tile(j, "far")
        return carry

    lax.fori_loop(0, jnp.maximum(qi - 1, 0), far_body, 0)

    @pl.when(qi >= 1)
    def _():
        tile(qi - 1, "near")

    tile(qi, "diag")

    for hh in range(N_HEADS):
        a, b = 2 * hh, 2 * hh + 1
        o_t = acc_sc[a] / l_sc[a] - lam_ref[0] * (acc_sc[b] / l_sc[b])
        o_ref[:, hh * HEAD_DIM:(hh + 1) * HEAD_DIM] = (
            _rms(o_t.T, gd_ref[...]) * (1.0 - LAM_INIT)).astype(BF16)
    for hh in range(N_HEADS):
        s = 2 * N_HEADS + hh
        o_t = acc_sc[s] / l_sc[s]
        o_ref[:, (N_HEADS + hh) * HEAD_DIM:(N_HEADS + hh + 1) * HEAD_DIM] = _rms(o_t.T, gf_ref[...]).astype(BF16)


def _attn(qk, vt, bias_tiles_t, lam, cum4, subln_g, fox_g, *, batch, seq):
    t = min(ATTN_TILE, seq)
    nq = seq // t
    head_blk = N_HEADS * seq * HEAD_DIM * 2
    est = (2 * (2 * N_HEADS * t * HEAD_DIM * 2 + 4 * head_blk + N_HEADS * 2 * t * t * 4 + N_HEADS * 8 * seq * 4
                + t * 2 * N_HEADS * HEAD_DIM * 2)
           + N_CHAINS * (HEAD_DIM + 16) * t * 4 + 2 * N_HEADS * t * HEAD_DIM * 2 + N_HEADS * seq * LANES * 4
           + 48 * t * t * 4)
    qspec = lambda blk: pl.BlockSpec((1, N_HEADS, t, HEAD_DIM), lambda b, i: (b, blk, i, 0))
    kspec = lambda blk: pl.BlockSpec((1, N_HEADS, seq, HEAD_DIM), lambda b, i: (b, blk, 0, 0))
    vspec = lambda blk: pl.BlockSpec((1, N_HEADS, HEAD_DIM, seq), lambda b, i: (b, blk, 0, 0))
    return pl.pallas_call(
        functools.partial(_attn_kernel, t=t, seq=seq),
        grid=(batch, nq),
        in_specs=[
            pl.BlockSpec(memory_space=pltpu.SMEM),
            qspec(0), kspec(1), vspec(0),
            qspec(2), kspec(3), vspec(1),
            pl.BlockSpec((N_HEADS, 2, t, t), lambda b, i: (0, 0, 0, 0)),
            pl.BlockSpec((1, N_HEADS, 1, seq), lambda b, i: (b, 0, 0, 0)),
            pl.BlockSpec((1, HEAD_DIM), lambda b, i: (0, 0)),
            pl.BlockSpec((1, HEAD_DIM), lambda b, i: (0, 0)),
        ],
        out_specs=pl.BlockSpec((t, 2 * N_HEADS * HEAD_DIM), lambda b, i: (b * nq + i, 0)),
        out_shape=jax.ShapeDtypeStruct((batch * seq, 2 * N_HEADS * HEAD_DIM), BF16),
        scratch_shapes=[pltpu.VMEM((N_CHAINS, 1, t), F32), pltpu.VMEM((N_CHAINS, 1, t), F32),
                        pltpu.VMEM((N_CHAINS, HEAD_DIM, t), F32),
                        pltpu.VMEM((2 * N_HEADS, t, HEAD_DIM), BF16),
                        pltpu.VMEM((N_HEADS, seq, 1), F32),
                        pltpu.VMEM((N_CHAINS, t, t), F32), pltpu.VMEM((N_CHAINS, 1, t), F32)],
        compiler_params=pltpu.CompilerParams(
            dimension_semantics=("arbitrary", "arbitrary"), vmem_limit_bytes=_vmem_limit(est)),
        name="attn",
    )(lam, qk, qk, vt, qk, qk, vt, bias_tiles_t, cum4, subln_g, fox_g)


def _out_q_kernel(x_ref, m_ref, wo_ref, gc_ref, wq_ref, qg_ref, x1_ref, qc_ref):
    x1 = x_ref[...] + jnp.dot(m_ref[...], wo_ref[...], preferred_element_type=F32)
    x1_ref[...] = x1
    hc = _rms(x1, gc_ref[...]).astype(BF16)
    q = jnp.dot(hc, wq_ref[...], preferred_element_type=F32)
    for hh in range(N_HEADS):
        sl = slice(hh * CROSS_HEAD_DIM, (hh + 1) * CROSS_HEAD_DIM)
        qh = _rms(q[:, sl], qg_ref[...]) * (CROSS_HEAD_DIM ** -0.5)
        qc_ref[:, sl] = qh.astype(BF16)


def _out_q(x2d, mixed, w_out, g_cross, w_cq, q_g):
    n = x2d.shape[0]
    tm = min(PROJ_ROWS, n)
    est = 2 * (tm * D_MODEL * 4 * 2 + tm * D_MODEL * 2 + 2 * D_MODEL * D_MODEL * 2 + tm * D_MODEL * 2) + 6 * tm * D_MODEL * 4
    return pl.pallas_call(
        _out_q_kernel,
        grid=(n // tm,),
        in_specs=[
            pl.BlockSpec((tm, D_MODEL), lambda i: (i, 0)),
            pl.BlockSpec((tm, D_MODEL), lambda i: (i, 0)),
            pl.BlockSpec((D_MODEL, D_MODEL), lambda i: (0, 0)),
            pl.BlockSpec((1, D_MODEL), lambda i: (0, 0)),
            pl.BlockSpec((D_MODEL, D_MODEL), lambda i: (0, 0)),
            pl.BlockSpec((1, CROSS_HEAD_DIM), lambda i: (0, 0)),
        ],
        out_specs=[pl.BlockSpec((tm, D_MODEL), lambda i: (i, 0)),
                   pl.BlockSpec((tm, D_MODEL), lambda i: (i, 0))],
        out_shape=[jax.ShapeDtypeStruct((n, D_MODEL), F32), jax.ShapeDtypeStruct((n, D_MODEL), BF16)],
        compiler_params=pltpu.CompilerParams(
            dimension_semantics=("arbitrary",), vmem_limit_bytes=_vmem_limit(est)),
        name="out_q",
    )(x2d, mixed, w_out, g_cross, w_cq, q_g)


def _mem_kv_kernel(mem_ref, gm_ref, w_ref, kg_ref, k_ref, v_ref):
    mn = _rms(mem_ref[...], gm_ref[...]).astype(BF16)
    kv = jnp.dot(mn, w_ref[...], preferred_element_type=F32)
    for hh in range(N_HEADS):
        sl = slice(hh * CROSS_HEAD_DIM, (hh + 1) * CROSS_HEAD_DIM)
        k_ref[0, :, sl] = _rms(kv[:, sl], kg_ref[...]).astype(BF16)
    v_ref[0] = kv[:, D_MODEL:].astype(BF16)


def _mem_kv(mem2d, g_mem, w_ckv, k_g, *, batch):
    est = 2 * (N_MEM * D_MODEL * 4 + D_MODEL * 2 * D_MODEL * 2 + 2 * N_MEM * D_MODEL * 2) + 4 * N_MEM * 2 * D_MODEL * 4
    return pl.pallas_call(
        _mem_kv_kernel,
        grid=(batch,),
        in_specs=[
            pl.BlockSpec((N_MEM, D_MODEL), lambda b: (b, 0)),
            pl.BlockSpec((1, D_MODEL), lambda b: (0, 0)),
            pl.BlockSpec((D_MODEL, 2 * D_MODEL), lambda b: (0, 0)),
            pl.BlockSpec((1, CROSS_HEAD_DIM), lambda b: (0, 0)),
        ],
        out_specs=[pl.BlockSpec((1, N_MEM, D_MODEL), lambda b: (b, 0, 0)),
                   pl.BlockSpec((1, N_MEM, D_MODEL), lambda b: (b, 0, 0))],
        out_shape=[jax.ShapeDtypeStruct((batch, N_MEM, D_MODEL), BF16),
                   jax.ShapeDtypeStruct((batch, N_MEM, D_MODEL), BF16)],
        compiler_params=pltpu.CompilerParams(
            dimension_semantics=("arbitrary",), vmem_limit_bytes=_vmem_limit(est)),
        name="mem_kv",
    )(mem2d, g_mem, w_ckv, k_g)


def _route(logits_t):
    gl = logits_t[0:N_GROUPS]
    gmax = jnp.max(gl, axis=0, keepdims=True)
    eg = jnp.exp(gl - gmax)
    p_group = eg / jnp.sum(eg, axis=0, keepdims=True)
    p_g = jnp.max(p_group, axis=0, keepdims=True)
    g_idx = jnp.full_like(p_g, float(N_GROUPS))
    for g in reversed(range(N_GROUPS)):
        g_idx = jnp.where(p_group[g:g + 1] == p_g, float(g), g_idx)

    sel = []
    for j in range(EXPERTS_PER_GROUP):
        acc = jnp.zeros_like(p_g)
        for g in range(N_GROUPS):
            r = N_GROUPS + g * EXPERTS_PER_GROUP + j
            acc = jnp.where(g_idx == float(g), logits_t[r:r + 1], acc)
        sel.append(acc)
    smax = functools.reduce(jnp.maximum, sel)
    es = [jnp.exp(s - smax) for s in sel]
    den = functools.reduce(jnp.add, es)
    p_in = [e / den for e in es]

    v0 = functools.reduce(jnp.maximum, p_in)
    i0 = jnp.full_like(v0, float(EXPERTS_PER_GROUP))
    for j in reversed(range(EXPERTS_PER_GROUP)):
        i0 = jnp.where(p_in[j] == v0, float(j), i0)
    rest = [jnp.where(i0 == float(j), -1.0, p_in[j]) for j in range(EXPERTS_PER_GROUP)]
    v1 = functools.reduce(jnp.maximum, rest)
    i1 = jnp.full_like(v1, float(EXPERTS_PER_GROUP))
    for j in reversed(range(EXPERTS_PER_GROUP)):
        i1 = jnp.where(rest[j] == v1, float(j), i1)

    tot = v0 + v1
    w0 = p_g * (v0 / tot)
    w1 = p_g * (v1 / tot)
    first = i0 < i1
    a = jnp.where(first, i0, i1)
    b = jnp.where(first, i1, i0)
    wa = jnp.where(first, w0, w1)
    wb = jnp.where(first, w1, w0)
    pair = jnp.where(a == 0.0, 0.0, jnp.where(a == 1.0, 3.0, 5.0)) + (b - a - 1.0)
    bucket = g_idx * float(len(PAIRS)) + pair
    return bucket, wa, wb


def _cross_router_kernel(q_ref, k_ref, v_ref, x1_ref, wco_ref, gf_ref, whi_ref, wlo_ref, rb_ref,
                         xa_ref, rt_ref, *, tm):
    q = q_ref[...]
    k = k_ref[0]
    v = v_ref[0]
    outs = []
    for hh in range(N_HEADS):
        sl = slice(hh * CROSS_HEAD_DIM, (hh + 1) * CROSS_HEAD_DIM)
        s = _nt_dot(q[:, sl], k[:, sl])
        p = jnp.exp(s - jnp.max(s, axis=-1, keepdims=True))
        l = jnp.sum(p, axis=-1, keepdims=True)
        o = jnp.dot(p.astype(BF16), v[:, sl], preferred_element_type=F32) / l
        outs.append(o.astype(BF16))
    o = jnp.concatenate(outs, axis=1)
    x2 = x1_ref[...] + jnp.dot(o, wco_ref[...], preferred_element_type=F32)
    xa_ref[:, :D_MODEL] = x2

    h3 = _rms(x2, gf_ref[...])
    hi = h3.astype(BF16)
    lo = (h3 - hi.astype(F32)).astype(BF16)
    lt = _nt_dot(whi_ref[...], hi) + (_nt_dot(whi_ref[...], lo) + _nt_dot(wlo_ref[...], hi))
    lt = lt + rb_ref[...]
    bucket, wa, wb = _route(lt)

    sub = lax.broadcasted_iota(I32, (8, tm), 0)
    rows = jnp.where(sub == 0, bucket, jnp.where(sub == 1, wa, jnp.where(sub == 2, wb, 0.0)))
    rt_ref[0] = rows
    sub_a = lax.broadcasted_iota(I32, (AUX_COLS, tm), 0)
    aux_t = jnp.where(sub_a == 0, bucket, jnp.where(sub_a == 1, wa, jnp.where(sub_a == 2, wb, 0.0)))
    xa_ref[:, D_MODEL:] = aux_t.T


def _cross_router(qc, kc, vc, x1, w_co, g_ffn, wr_hi, wr_lo, rb, *, batch, seq):
    tm = min(CROSS_ROWS, seq)
    ns = seq // tm
    n = batch * seq
    est = 2 * (tm * D_MODEL * 2 + 2 * N_MEM * D_MODEL * 2 + tm * D_MODEL * 4 + D_MODEL * D_MODEL * 2
               + tm * ROW_W * 4) + 10 * tm * D_MODEL * 4
    return pl.pallas_call(
        functools.partial(_cross_router_kernel, tm=tm),
        grid=(batch, ns),
        in_specs=[
            pl.BlockSpec((tm, D_MODEL), lambda b, s: (b * ns + s, 0)),
            pl.BlockSpec((1, N_MEM, D_MODEL), lambda b, s: (b, 0, 0)),
            pl.BlockSpec((1, N_MEM, D_MODEL), lambda b, s: (b, 0, 0)),
            pl.BlockSpec((tm, D_MODEL), lambda b, s: (b * ns + s, 0)),
            pl.BlockSpec((D_MODEL, D_MODEL), lambda b, s: (0, 0)),
            pl.BlockSpec((1, D_MODEL), lambda b, s: (0, 0)),
            pl.BlockSpec((32, D_MODEL), lambda b, s: (0, 0)),
            pl.BlockSpec((32, D_MODEL), lambda b, s: (0, 0)),
            pl.BlockSpec((32, 1), lambda b, s: (0, 0)),
        ],
        out_specs=[pl.BlockSpec((tm, ROW_W), lambda b, s: (b * ns + s, 0)),
                   pl.BlockSpec((1, 8, tm), lambda b, s: (b * ns + s, 0, 0))],
        out_shape=[jax.ShapeDtypeStruct((n, ROW_W), F32),
                   jax.ShapeDtypeStruct((n // tm, 8, tm), F32)],
        compiler_params=pltpu.CompilerParams(
            dimension_semantics=("arbitrary", "arbitrary"), vmem_limit_bytes=_vmem_limit(est)),
        name="cross_router",
    )(qc, kc, vc, x1, w_co, g_ffn, wr_hi, wr_lo, rb)


def _rank_kernel(rt_ref, rank_ref, cnt_ref, carry_sc, *, tm):
    @pl.when(pl.program_id(0) == 0)
    def _():
        carry_sc[...] = jnp.zeros_like(carry_sc)

    bucket = rt_ref[0, 0:1, :]
    sub = lax.broadcasted_iota(I32, (32, tm), 0).astype(F32)
    hit = sub == bucket
    r = lax.broadcasted_iota(I32, (tm, tm), 0)
    c = lax.broadcasted_iota(I32, (tm, tm), 1)
    upper = jnp.where(r <= c, 1.0, 0.0).astype(BF16)
    cum = jnp.dot(jnp.where(hit, 1.0, 0.0).astype(BF16), upper, preferred_element_type=F32)
    carry = carry_sc[:, 0:1]
    rank = jnp.sum(jnp.where(hit, cum - 1.0 + carry, 0.0), axis=0, keepdims=True)
    rank_ref[0] = rank.astype(I32)
    new_carry = carry_sc[...] + jnp.broadcast_to(cum[:, tm - 1:tm], carry_sc.shape)
    carry_sc[...] = new_carry
    cnt_ref[...] = new_carry


def _rank(rt):
    nt, _, tm = rt.shape
    return pl.pallas_call(
        functools.partial(_rank_kernel, tm=tm),
        grid=(nt,),
        in_specs=[pl.BlockSpec((1, 8, tm), lambda i: (i, 0, 0))],
        out_specs=[pl.BlockSpec((1, 1, tm), lambda i: (i, 0, 0)),
                   pl.BlockSpec((32, LANES), lambda i: (0, 0))],
        out_shape=[jax.ShapeDtypeStruct((nt, 1, tm), I32), jax.ShapeDtypeStruct((32, LANES), F32)],
        scratch_shapes=[pltpu.VMEM((32, LANES), F32)],
        compiler_params=pltpu.CompilerParams(dimension_semantics=("arbitrary",)),
        name="rank",
    )(rt)


def _row_copies(pos_ref, base, hbm, vmem, sem, n_rows, *, to_hbm, wait):
    def body(r, carry):
        h = hbm.at[pl.ds(pos_ref[base + r], 1)]
        v = vmem.at[pl.ds(r, 1)]
        cp = pltpu.make_async_copy(v, h, sem) if to_hbm else pltpu.make_async_copy(h, v, sem)
        if wait:
            cp.wait()
        else:
            cp.start()
        return carry
    lax.fori_loop(0, n_rows, body, 0, unroll=8)


def _dispatch_kernel(pos_ref, zrow_ref, zflag_ref, nact_ref, xa_ref, xs_ref, zbuf, zsem, sem, *,
                     n_tiles, rows):
    @pl.when(pl.program_id(0) == 0)
    def _():
        _zero_partial_tiles(zrow_ref, zflag_ref, nact_ref, xs_ref, zbuf, zsem, n_tiles=n_tiles)

    base = pl.program_id(0) * rows
    _row_copies(pos_ref, base, xs_ref, xa_ref, sem, rows, to_hbm=True, wait=False)
    _row_copies(pos_ref, base, xs_ref, xa_ref, sem, rows, to_hbm=True, wait=True)


def _zero_partial_tiles(zrow_ref, zflag_ref, nact_ref, xs_ref, zbuf, zsem, *, n_tiles):
    zbuf[...] = jnp.zeros_like(zbuf)

    def zero_copy(row):
        return pltpu.make_async_copy(zbuf, xs_ref.at[pl.ds(pl.multiple_of(row, MOE_ROWS), MOE_ROWS)], zsem)

    def bucket_tiles(op):
        def body(b, carry):
            @pl.when(zflag_ref[b] > 0)
            def _():
                op(zero_copy(zrow_ref[b]))
            return carry
        lax.fori_loop(0, N_PAIR_BUCKETS, body, 0)

    def unused_tiles(op):
        def body(i, carry):
            op(zero_copy(i * MOE_ROWS))
            return carry
        lax.fori_loop(nact_ref[0], n_tiles, body, 0)

    bucket_tiles(lambda cp: cp.start())
    unused_tiles(lambda cp: cp.start())
    bucket_tiles(lambda cp: cp.wait())
    unused_tiles(lambda cp: cp.wait())


def _dispatch(pos, zrow, zflag, n_active, xa, *, n_tiles):
    n_tok = xa.shape[0]
    n_sorted = n_tiles * MOE_ROWS
    rows = min(DMA_ROWS, n_tok)
    return pl.pallas_call(
        functools.partial(_dispatch_kernel, n_tiles=n_tiles, rows=rows),
        grid_spec=pltpu.PrefetchScalarGridSpec(
            num_scalar_prefetch=4,
            grid=(n_tok // rows,),
            in_specs=[pl.BlockSpec((rows, ROW_W), lambda i, *_: (i, 0))],
            out_specs=pl.BlockSpec(memory_space=pl.ANY),
            scratch_shapes=[pltpu.VMEM((MOE_ROWS, ROW_W), F32), pltpu.SemaphoreType.DMA(()),
                            pltpu.SemaphoreType.DMA(())],
        ),
        out_shape=jax.ShapeDtypeStruct((n_sorted, ROW_W), F32),
        compiler_params=pltpu.CompilerParams(dimension_semantics=("arbitrary",), has_side_effects=True),
        name="dispatch",
    )(pos, zrow, zflag, n_active, xa)


def _combine_kernel(pos_ref, ys_ref, out_ref, sem, *, rows):
    base = pl.program_id(0) * rows
    _row_copies(pos_ref, base, ys_ref, out_ref, sem, rows, to_hbm=False, wait=False)
    _row_copies(pos_ref, base, ys_ref, out_ref, sem, rows, to_hbm=False, wait=True)


def _combine(pos, ys, *, n_tok):
    rows = min(DMA_ROWS, n_tok)
    return pl.pallas_call(
        functools.partial(_combine_kernel, rows=rows),
        grid_spec=pltpu.PrefetchScalarGridSpec(
            num_scalar_prefetch=1,
            grid=(n_tok // rows,),
            in_specs=[pl.BlockSpec(memory_space=pl.ANY)],
            out_specs=pl.BlockSpec((rows, D_MODEL), lambda i, *_: (i, 0)),
            scratch_shapes=[pltpu.SemaphoreType.DMA(())],
        ),
        out_shape=jax.ShapeDtypeStruct((n_tok, D_MODEL), F32),
        compiler_params=pltpu.CompilerParams(dimension_semantics=("arbitrary",), has_side_effects=True),
        name="combine",
    )(pos, ys)


def _moe_kernel(ta_ref, tb_ref, xi_ref, nact_ref, xs_ref, gf_ref, wga_ref, wua_ref, wda_ref,
                wgb_ref, wub_ref, wdb_ref, ys_ref):
    i = pl.program_id(0)

    @pl.when(i < nact_ref[0])
    def _():
        x2 = xs_ref[:, :D_MODEL]
        aux = xs_ref[:, D_MODEL:]
        wa = aux[:, 1:2]
        wb = aux[:, 2:3]
        h = _rms(x2, gf_ref[...]).astype(BF16)

        def mlp(wg, wu, wd):
            g = jnp.dot(h, wg[0], preferred_element_type=F32)
            u = jnp.dot(h, wu[0], preferred_element_type=F32)
            act = (g / (1.0 + jnp.exp(-g))) * u
            return jnp.dot(act.astype(BF16), wd[0], preferred_element_type=F32)

        ys_ref[...] = x2 + (wa * mlp(wga_ref, wua_ref, wda_ref) + wb * mlp(wgb_ref, wub_ref, wdb_ref))

    @pl.when(i >= nact_ref[0])
    def _():
        ys_ref[...] = jnp.zeros_like(ys_ref)


def _moe(tile_a, tile_b, tile_x, n_active, xs, g_ffn, w_gate, w_up, w_down):
    n_tiles = xs.shape[0] // MOE_ROWS
    t = MOE_ROWS
    wa_map = lambda i, ta, tb, xi, na: (ta[i], 0, 0)
    wb_map = lambda i, ta, tb, xi, na: (tb[i], 0, 0)
    est = 2 * (t * ROW_W * 4 + 6 * D_MODEL * D_EXPERT * 2 + t * D_MODEL * 4) + 10 * t * D_MODEL * 4
    return pl.pallas_call(
        _moe_kernel,
        grid_spec=pltpu.PrefetchScalarGridSpec(
            num_scalar_prefetch=4,
            grid=(n_tiles,),
            in_specs=[
                pl.BlockSpec((t, ROW_W), lambda i, ta, tb, xi, na: (xi[i], 0)),
                pl.BlockSpec((1, D_MODEL), lambda i, ta, tb, xi, na: (0, 0)),
                pl.BlockSpec((1, D_MODEL, D_EXPERT), wa_map),
                pl.BlockSpec((1, D_MODEL, D_EXPERT), wa_map),
                pl.BlockSpec((1, D_EXPERT, D_MODEL), wa_map),
                pl.BlockSpec((1, D_MODEL, D_EXPERT), wb_map),
                pl.BlockSpec((1, D_MODEL, D_EXPERT), wb_map),
                pl.BlockSpec((1, D_EXPERT, D_MODEL), wb_map),
            ],
            out_specs=pl.BlockSpec((t, D_MODEL), lambda i, ta, tb, xi, na: (i, 0)),
        ),
        out_shape=jax.ShapeDtypeStruct((n_tiles * t, D_MODEL), F32),
        compiler_params=pltpu.CompilerParams(
            dimension_semantics=("arbitrary",), vmem_limit_bytes=_vmem_limit(est)),
        name="moe",
    )(tile_a, tile_b, tile_x, n_active, xs, g_ffn, w_gate, w_up, w_down, w_gate, w_up, w_down)


def _t5_bucket(rel):
    nb = N_BUCKETS_T5 // 2
    max_exact = nb // 2
    ret = (rel > 0).astype(I32) * nb
    n = jnp.abs(rel)
    nf = jnp.maximum(n, 1).astype(F32)
    large = max_exact + (jnp.log(nf / max_exact) / math.log(MAX_DISTANCE / max_exact)
                         * (nb - max_exact)).astype(I32)
    large = jnp.minimum(large, nb - 1)
    return ret + jnp.where(n < max_exact, n, large)


def _diff_bias(rel_bias, t):
    assert t % CHUNK == 0 and t >= MAX_DISTANCE
    n_rel = 3 * t
    vec = rel_bias[_t5_bucket(jnp.arange(n_rel, dtype=I32) - (2 * t - 1))].astype(F32).T
    toe = jnp.tile(vec, (1, t))[:, :t * (n_rel - 1)].reshape(N_HEADS, t, n_rel - 1)
    qpos = jnp.arange(t, dtype=I32)[:, None]
    kpos = jnp.arange(t, dtype=I32)[None, :]
    far = rel_bias[_t5_bucket(jnp.asarray(-2 * t, I32))].astype(F32)[:, None, None]
    b0 = (toe[:, :, 2 * t - 1:3 * t - 1] - far) * LOG2E
    b0 = jnp.where((kpos // CHUNK <= qpos // CHUNK)[None], b0, NEG_INF)
    b1 = (toe[:, :, t - 1:2 * t - 1] - far) * LOG2E
    return jnp.swapaxes(jnp.stack([b0, b1], axis=1), -1, -2)


def _tile_plan(counts, n_tiles):
    t = MOE_ROWS
    tiles = (counts + t - 1) // t
    ends = jnp.cumsum(tiles)
    starts = ends - tiles
    n_active = ends[-1]
    tile_idx = jnp.minimum(jnp.arange(n_tiles, dtype=I32), n_active - 1)
    tile_bucket = jnp.minimum(jnp.searchsorted(ends, tile_idx, side="right"), N_PAIR_BUCKETS - 1).astype(I32)
    pa = jnp.asarray([p[0] for p in PAIRS], I32)
    pb = jnp.asarray([p[1] for p in PAIRS], I32)
    grp = tile_bucket // len(PAIRS)
    pair = tile_bucket % len(PAIRS)
    tile_a = grp * EXPERTS_PER_GROUP + pa[pair]
    tile_b = grp * EXPERTS_PER_GROUP + pb[pair]
    row_off = (starts * t).astype(I32)
    zrow = ((ends - 1) * t).astype(I32)
    zflag = (counts > 0).astype(I32)
    return row_off, zrow, zflag, tile_a.astype(I32), tile_b.astype(I32), tile_idx.astype(I32), n_active.astype(I32)


def kernel(x, mem, rel_bias, norm_mix_g, w_in, b_forget, diff_q_norm_g, diff_k_norm_g, diff_lambda_q1, diff_lambda_k1, diff_lambda_q2, diff_lambda_k2, diff_subln_g, fox_q_norm_g, fox_k_norm_g, fox_out_norm_g, w_out, norm_cross_g, norm_mem_g, w_cq, w_ckv, cross_q_norm_g, cross_k_norm_g, w_co, norm_ffn_g, w_group_router, b_group_router, w_expert_router, b_expert_router, w_exp_gate, w_exp_up, w_exp_down):
    batch, seq, d = x.shape
    assert d == D_MODEL and norm_mix_g.shape[0] == 1 and mem.shape[1] == N_MEM
    n_tok = batch * seq
    assert seq % ATTN_TILE == 0 or seq < ATTN_TILE
    l = 0

    w_qk = jnp.concatenate([w_in[l, :, :1024], w_in[l, :, 1536:2560]], axis=1).astype(BF16)
    w_vt = jnp.concatenate([w_in[l, :, 1024:1536], w_in[l, :, 2560:3072]], axis=1).T.astype(BF16)
    wft = jnp.zeros((8, D_MODEL), F32).at[:N_HEADS].set(w_in[l, :, 3072:].T).astype(BF16)
    bf = jnp.zeros((8, 1), F32).at[:N_HEADS, 0].set(b_forget[l].astype(F32))
    ones = jnp.ones((HEAD_DIM,), F32)
    qg = jnp.stack([
        jnp.tile(diff_q_norm_g[l].astype(F32), 2) * (DIFF_QK_DIM ** -0.5 * LOG2E),
        jnp.tile(diff_k_norm_g[l].astype(F32), 2),
        fox_q_norm_g[l].astype(F32) * (HEAD_DIM ** -0.5 * LOG2E), fox_k_norm_g[l].astype(F32),
        ones, ones, ones, ones])
    lam = (jnp.exp(jnp.sum(diff_lambda_q1[l].astype(F32) * diff_lambda_k1[l].astype(F32)))
           - jnp.exp(jnp.sum(diff_lambda_q2[l].astype(F32) * diff_lambda_k2[l].astype(F32)))
           + LAM_INIT).reshape(1)
    t_attn = min(ATTN_TILE, seq)
    bias_tiles_t = _diff_bias(rel_bias, t_attn)
    wr = jnp.zeros((32, D_MODEL), F32)
    wr = wr.at[:N_GROUPS].set(w_group_router[l].T).at[N_GROUPS:N_GROUPS + N_EXPERTS].set(w_expert_router[l].T)
    wr_hi = wr.astype(BF16)
    wr_lo = (wr - wr_hi.astype(F32)).astype(BF16)
    rb = jnp.zeros((32, 1), F32)
    rb = rb.at[:N_GROUPS, 0].set(b_group_router[l]).at[N_GROUPS:N_GROUPS + N_EXPERTS, 0].set(b_expert_router[l])

    row = lambda v: v.astype(F32).reshape(1, -1)
    x2d = x.reshape(n_tok, D_MODEL)

    qk, vt, cum = _mix_proj(x2d, row(norm_mix_g[l]), w_qk, w_vt, wft, bf, qg, batch=batch, seq=seq)
    cum4 = cum[:, :N_HEADS].reshape(batch, N_HEADS, 1, seq)
    mixed = _attn(qk, vt, bias_tiles_t, lam, cum4, row(diff_subln_g[l]), row(fox_out_norm_g[l]),
                  batch=batch, seq=seq)
    x1, qc = _out_q(x2d, mixed, w_out[l].astype(BF16), row(norm_cross_g[l]),
                    w_cq[l].astype(BF16), row(cross_q_norm_g[l]))

    kc, vc = _mem_kv(mem.reshape(batch * N_MEM, D_MODEL), row(norm_mem_g[l]), w_ckv[l].astype(BF16),
                     row(cross_k_norm_g[l]), batch=batch)
    xa, rt = _cross_router(qc, kc, vc, x1, w_co[l].astype(BF16), row(norm_ffn_g[l]), wr_hi, wr_lo, rb,
                           batch=batch, seq=seq)

    rank, cnt = _rank(rt)
    counts = cnt[:N_PAIR_BUCKETS, 0].astype(I32)
    n_tiles = n_tok // MOE_ROWS + N_PAIR_BUCKETS
    row_off, zrow, zflag, tile_a, tile_b, tile_x, n_active = _tile_plan(counts, n_tiles)
    bucket = rt[:, 0, :].reshape(n_tok).astype(I32)
    pos = row_off[bucket] + rank.reshape(n_tok)
    n_active = n_active.reshape(1)
    xs = _dispatch(pos, zrow, zflag, n_active, xa, n_tiles=n_tiles)
    ys = _moe(tile_a, tile_b, tile_x, n_active, xs, row(norm_ffn_g[l]),
              w_exp_gate[l].astype(BF16), w_exp_up[l].astype(BF16), w_exp_down[l].astype(BF16))
    out = _combine(pos, ys, n_tok=n_tok)
    return out.reshape(batch, seq, D_MODEL)
```

```python
import functools
import math

import jax
import jax.numpy as jnp
import numpy as np
from jax import lax
from jax.experimental import pallas as pl
from jax.experimental.pallas import tpu as pltpu

F32 = jnp.float32
BF16 = jnp.bfloat16
I32 = jnp.int32

D_MODEL = 1024
CHUNK = 64
N_MEM = 256
N_HEADS = 4
HEAD_DIM = 128
DIFF_QK_DIM = 64
CROSS_HEAD_DIM = 256
N_BUCKETS_T5 = 32
MAX_DISTANCE = 128
N_GROUPS = 4
EXPERTS_PER_GROUP = 4
N_EXPERTS = N_GROUPS * EXPERTS_PER_GROUP
D_EXPERT = 512
EPS = 1e-6
NEG_INF = -1e30
LAM_INIT = 0.8 - 0.6 * math.exp(-0.3 * 0)
LOG2E = math.log2(math.e)

PAIRS = ((0, 1), (0, 2), (0, 3), (1, 2), (1, 3), (2, 3))
N_PAIR_BUCKETS = N_GROUPS * len(PAIRS)

LANES = 128
VMEM_LIMIT_CAP = 56 * 1024 * 1024

PROJ_ROWS = 512
ATTN_TILE = 256
CROSS_ROWS = 512
MOE_ROWS = 256
AUX_COLS = LANES
ROW_W = D_MODEL + AUX_COLS
DMA_ROWS = 256


def _vmem_limit(nbytes):
    return int(min(max(nbytes * 5 // 4, 32 * 1024 * 1024), VMEM_LIMIT_CAP))


def _nt_dot(a, b):
    return lax.dot_general(a, b, (((1,), (1,)), ((), ())), preferred_element_type=F32)


def _rms(x, g):
    ms = jnp.mean(x * x, axis=-1, keepdims=True)
    return x * lax.rsqrt(ms + EPS) * g


def _mix_proj_kernel(x_ref, g_ref, w_ref, wvt_ref, wft_ref, bf_ref, qg_ref, o_ref, vt_ref, cum_ref, carry_ref,
                     *, tm):
    si = pl.program_id(1)
    h = _rms(x_ref[...], g_ref[...]).astype(BF16)
    lane = lax.broadcasted_iota(I32, (tm, HEAD_DIM), 1)
    lo = lane < DIFF_QK_DIM
    for kind in range(4):
        p = jnp.dot(h, w_ref[:, kind * 512:(kind + 1) * 512], preferred_element_type=F32)
        for hh in range(N_HEADS):
            ph = p[:, hh * HEAD_DIM:(hh + 1) * HEAD_DIM]
            if kind in (0, 1):
                sq = ph * ph
                s_lo = jnp.sum(jnp.where(lo, sq, 0.0), axis=-1, keepdims=True)
                s_hi = jnp.sum(jnp.where(lo, 0.0, sq), axis=-1, keepdims=True)
                ms = jnp.where(lo, s_lo, s_hi) * (1.0 / DIFF_QK_DIM)
                ph = ph * lax.rsqrt(ms + EPS) * qg_ref[kind:kind + 1, :]
            else:
                ph = _rms(ph, qg_ref[kind:kind + 1, :])
            o_ref[0, kind * N_HEADS + hh] = ph.astype(BF16)
    for kind in range(2):
        pt = _nt_dot(wvt_ref[kind * 512:(kind + 1) * 512, :], h)
        for hh in range(N_HEADS):
            vt_ref[0, kind * N_HEADS + hh] = pt[hh * HEAD_DIM:(hh + 1) * HEAD_DIM, :].astype(BF16)

    z = _nt_dot(wft_ref[...], h) + bf_ref[...]
    logf = jnp.minimum(z, 0.0) - jnp.log(1.0 + jnp.exp(-jnp.abs(z)))
    lane8 = lax.broadcasted_iota(I32, (8, tm), 1)
    c = logf * LOG2E
    k = 1
    while k < tm:
        c = c + jnp.where(lane8 >= k, pltpu.roll(c, k, axis=1), 0.0)
        k *= 2

    @pl.when(si == 0)
    def _():
        carry_ref[...] = jnp.zeros_like(carry_ref)

    c = c + carry_ref[:, 0:1]
    cum_ref[0] = c
    carry_ref[...] = jnp.broadcast_to(c[:, tm - 1:tm], carry_ref.shape)


def _mix_proj(x2d, g, w_qk, w_vt, wft, bf, qg, *, batch, seq):
    tm = min(PROJ_ROWS, seq)
    ns = seq // tm
    est = 2 * (tm * D_MODEL * 4 + D_MODEL * 3072 * 2 + 24 * tm * HEAD_DIM * 2) + 8 * tm * D_MODEL
    return pl.pallas_call(
        functools.partial(_mix_proj_kernel, tm=tm),
        grid=(batch, ns),
        in_specs=[
            pl.BlockSpec((tm, D_MODEL), lambda b, s: (b * ns + s, 0)),
            pl.BlockSpec((1, D_MODEL), lambda b, s: (0, 0)),
            pl.BlockSpec((D_MODEL, 2048), lambda b, s: (0, 0)),
            pl.BlockSpec((1024, D_MODEL), lambda b, s: (0, 0)),
            pl.BlockSpec((8, D_MODEL), lambda b, s: (0, 0)),
            pl.BlockSpec((8, 1), lambda b, s: (0, 0)),
            pl.BlockSpec((8, HEAD_DIM), lambda b, s: (0, 0)),
        ],
        out_specs=[
            pl.BlockSpec((1, 16, tm, HEAD_DIM), lambda b, s: (b, 0, s, 0)),
            pl.BlockSpec((1, 8, HEAD_DIM, tm), lambda b, s: (b, 0, 0, s)),
            pl.BlockSpec((1, 8, tm), lambda b, s: (b, 0, s)),
        ],
        out_shape=[
            jax.ShapeDtypeStruct((batch, 16, seq, HEAD_DIM), BF16),
            jax.ShapeDtypeStruct((batch, 8, HEAD_DIM, seq), BF16),
            jax.ShapeDtypeStruct((batch, 8, seq), F32),
        ],
        scratch_shapes=[pltpu.VMEM((8, LANES), F32)],
        compiler_params=pltpu.CompilerParams(
            dimension_semantics=("arbitrary", "arbitrary"), vmem_limit_bytes=_vmem_limit(est)),
        name="mix_proj",
    )(x2d, g, w_qk, w_vt, wft, bf, qg)


def _init_softmax(m_sc, l_sc, acc_sc):
    m_sc[...] = jnp.full(m_sc.shape, NEG_INF, F32)
    l_sc[...] = jnp.zeros_like(l_sc)
    acc_sc[...] = jnp.zeros_like(acc_sc)


N_CHAINS = 3 * N_HEADS


def _attn_kernel(lam_ref, dq_ref, dk_ref, dvt_ref, fq_ref, fk_ref, fvt_ref, bias_ref, c_ref,
                 gd_ref, gf_ref, o_ref, m_sc, l_sc, acc_sc, qm_sc, ccol_sc, s_sc, mt_sc, *, t, seq):
    qi = pl.program_id(1)
    _init_softmax(m_sc, l_sc, acc_sc)
    key = lax.broadcasted_iota(I32, (t, t), 0)
    qry = lax.broadcasted_iota(I32, (t, t), 1)

    lane = lax.broadcasted_iota(I32, (t, HEAD_DIM), 1)
    for hh in range(N_HEADS):
        q = dq_ref[0, hh]
        zero = jnp.zeros_like(q)
        qm_sc[2 * hh] = jnp.where(lane < DIFF_QK_DIM, q, zero)
        qm_sc[2 * hh + 1] = jnp.where(lane < DIFF_QK_DIM, zero, q)

    @pl.when(qi == 0)
    def _():
        for hh in range(N_HEADS):
            for j in range(seq // t):
                row = c_ref[0, hh, :, j * t:(j + 1) * t]
                ccol_sc[hh, j * t:(j + 1) * t, :] = jnp.sum(
                    jnp.where(key == qry, jnp.broadcast_to(row, (t, t)), 0.0), axis=1, keepdims=True)

    q_off = pl.multiple_of(qi * t, t)

    def tile(j, where):
        off = pl.multiple_of(j * t, t)

        for hh in range(N_HEADS):
            k = dk_ref[0, hh, pl.ds(off, t), :]
            for c in range(2):
                s_t = _nt_dot(k, qm_sc[2 * hh + c])
                if where != "far":
                    s_t = s_t + bias_ref[hh, 1 if where == "near" else 0]
                s_sc[2 * hh + c] = s_t
                mt_sc[2 * hh + c] = jnp.max(s_t, axis=0, keepdims=True)
        for hh in range(N_HEADS):
            k = fk_ref[0, hh, pl.ds(off, t), :]
            c_k = ccol_sc[hh, pl.ds(off, t), :]
            s_t = _nt_dot(k, fq_ref[0, hh]) - c_k
            if where == "diag":
                s_t = jnp.where(key <= qry, s_t, NEG_INF)
            s_sc[2 * N_HEADS + hh] = s_t
            mt_sc[2 * N_HEADS + hh] = jnp.max(s_t, axis=0, keepdims=True) + c_ref[0, hh, :, pl.ds(q_off, t)]

        for slot in range(N_CHAINS):
            vt_ref = dvt_ref if slot < 2 * N_HEADS else fvt_ref
            hh = slot // 2 if slot < 2 * N_HEADS else slot - 2 * N_HEADS
            v_t = vt_ref[0, hh, :, pl.ds(off, t)]
            m_prev = m_sc[slot]
            m_new = jnp.maximum(m_prev, mt_sc[slot])
            alpha = jnp.exp2(m_prev - m_new)
            if slot < 2 * N_HEADS:
                p_t = jnp.exp2(s_sc[slot] - m_new)
            else:
                p_t = jnp.exp2(s_sc[slot] - (m_new - c_ref[0, hh, :, pl.ds(q_off, t)]))
            l_sc[slot] = alpha * l_sc[slot] + jnp.sum(p_t, axis=0, keepdims=True)
            acc_sc[slot] = alpha * acc_sc[slot] + jnp.dot(v_t, p_t.astype(BF16), preferred_element_type=F32)
            m_sc[slot] = m_new

    def far_body(j, carry):
        tile(j, "far")
        return carry

    lax.fori_loop(0, jnp.maximum(qi - 1, 0), far_body, 0)

    @pl.when(qi >= 1)
    def _():
        tile(qi - 1, "near")

    tile(qi, "diag")

    for hh in range(N_HEADS):
        a, b = 2 * hh, 2 * hh + 1
        o_t = acc_sc[a] / l_sc[a] - lam_ref[0] * (acc_sc[b] / l_sc[b])
        o_ref[:, hh * HEAD_DIM:(hh + 1) * HEAD_DIM] = (
            _rms(o_t.T, gd_ref[...]) * (1.0 - LAM_INIT)).astype(BF16)
    for hh in range(N_HEADS):
        s = 2 * N_HEADS + hh
        o_t = acc_sc[s] / l_sc[s]
        o_ref[:, (N_HEADS + hh) * HEAD_DIM:(N_HEADS + hh + 1) * HEAD_DIM] = _rms(o_t.T, gf_ref[...]).astype(BF16)


def _attn(qk, vt, bias_tiles_t, lam, cum4, subln_g, fox_g, *, batch, seq):
    t = min(ATTN_TILE, seq)
    nq = seq // t
    head_blk = N_HEADS * seq * HEAD_DIM * 2
    est = (2 * (2 * N_HEADS * t * HEAD_DIM * 2 + 4 * head_blk + N_HEADS * 2 * t * t * 4 + N_HEADS * 8 * seq * 4
                + t * 2 * N_HEADS * HEAD_DIM * 2)
           + N_CHAINS * (HEAD_DIM + 16) * t * 4 + 2 * N_HEADS * t * HEAD_DIM * 2 + N_HEADS * seq * LANES * 4
           + 48 * t * t * 4)
    qspec = lambda blk: pl.BlockSpec((1, N_HEADS, t, HEAD_DIM), lambda b, i: (b, blk, i, 0))
    kspec = lambda blk: pl.BlockSpec((1, N_HEADS, seq, HEAD_DIM), lambda b, i: (b, blk, 0, 0))
    vspec = lambda blk: pl.BlockSpec((1, N_HEADS, HEAD_DIM, seq), lambda b, i: (b, blk, 0, 0))
    return pl.pallas_call(
        functools.partial(_attn_kernel, t=t, seq=seq),
        grid=(batch, nq),
        in_specs=[
            pl.BlockSpec(memory_space=pltpu.SMEM),
            qspec(0), kspec(1), vspec(0),
            qspec(2), kspec(3), vspec(1),
            pl.BlockSpec((N_HEADS, 2, t, t), lambda b, i: (0, 0, 0, 0)),
            pl.BlockSpec((1, N_HEADS, 1, seq), lambda b, i: (b, 0, 0, 0)),
            pl.BlockSpec((1, HEAD_DIM), lambda b, i: (0, 0)),
            pl.BlockSpec((1, HEAD_DIM), lambda b, i: (0, 0)),
        ],
        out_specs=pl.BlockSpec((t, 2 * N_HEADS * HEAD_DIM), lambda b, i: (b * nq + i, 0)),
        out_shape=jax.ShapeDtypeStruct((batch * seq, 2 * N_HEADS * HEAD_DIM), BF16),
        scratch_shapes=[pltpu.VMEM((N_CHAINS, 1, t), F32), pltpu.VMEM((N_CHAINS, 1, t), F32),
                        pltpu.VMEM((N_CHAINS, HEAD_DIM, t), F32),
                        pltpu.VMEM((2 * N_HEADS, t, HEAD_DIM), BF16),
                        pltpu.VMEM((N_HEADS, seq, 1), F32),
                        pltpu.VMEM((N_CHAINS, t, t), F32), pltpu.VMEM((N_CHAINS, 1, t), F32)],
        compiler_params=pltpu.CompilerParams(
            dimension_semantics=("arbitrary", "arbitrary"), vmem_limit_bytes=_vmem_limit(est)),
        name="attn",
    )(lam, qk, qk, vt, qk, qk, vt, bias_tiles_t, cum4, subln_g, fox_g)


def _out_q_kernel(x_ref, m_ref, wo_ref, gc_ref, wq_ref, qg_ref, x1_ref, qc_ref):
    x1 = x_ref[...] + jnp.dot(m_ref[...], wo_ref[...], preferred_element_type=F32)
    x1_ref[...] = x1
    hc = _rms(x1, gc_ref[...]).astype(BF16)
    q = jnp.dot(hc, wq_ref[...], preferred_element_type=F32)
    for hh in range(N_HEADS):
        sl = slice(hh * CROSS_HEAD_DIM, (hh + 1) * CROSS_HEAD_DIM)
        qh = _rms(q[:, sl], qg_ref[...]) * (CROSS_HEAD_DIM ** -0.5)
        qc_ref[:, sl] = qh.astype(BF16)


def _out_q(x2d, mixed, w_out, g_cross, w_cq, q_g):
    n = x2d.shape[0]
    tm = min(PROJ_ROWS, n)
    est = 2 * (tm * D_MODEL * 4 * 2 + tm * D_MODEL * 2 + 2 * D_MODEL * D_MODEL * 2 + tm * D_MODEL * 2) + 6 * tm * D_MODEL * 4
    return pl.pallas_call(
        _out_q_kernel,
        grid=(n // tm,),
        in_specs=[
            pl.BlockSpec((tm, D_MODEL), lambda i: (i, 0)),
            pl.BlockSpec((tm, D_MODEL), lambda i: (i, 0)),
            pl.BlockSpec((D_MODEL, D_MODEL), lambda i: (0, 0)),
            pl.BlockSpec((1, D_MODEL), lambda i: (0, 0)),
            pl.BlockSpec((D_MODEL, D_MODEL), lambda i: (0, 0)),
            pl.BlockSpec((1, CROSS_HEAD_DIM), lambda i: (0, 0)),
        ],
        out_specs=[pl.BlockSpec((tm, D_MODEL), lambda i: (i, 0)),
                   pl.BlockSpec((tm, D_MODEL), lambda i: (i, 0))],
        out_shape=[jax.ShapeDtypeStruct((n, D_MODEL), F32), jax.ShapeDtypeStruct((n, D_MODEL), BF16)],
        compiler_params=pltpu.CompilerParams(
            dimension_semantics=("arbitrary",), vmem_limit_bytes=_vmem_limit(est)),
        name="out_q",
    )(x2d, mixed, w_out, g_cross, w_cq, q_g)


def _mem_kv_kernel(mem_ref, gm_ref, w_ref, kg_ref, k_ref, v_ref):
    mn = _rms(mem_ref[...], gm_ref[...]).astype(BF16)
    kv = jnp.dot(mn, w_ref[...], preferred_element_type=F32)
    for hh in range(N_HEADS):
        sl = slice(hh * CROSS_HEAD_DIM, (hh + 1) * CROSS_HEAD_DIM)
        k_ref[0, :, sl] = _rms(kv[:, sl], kg_ref[...]).astype(BF16)
    v_ref[0] = kv[:, D_MODEL:].astype(BF16)


def _mem_kv(mem2d, g_mem, w_ckv, k_g, *, batch):
    est = 2 * (N_MEM * D_MODEL * 4 + D_MODEL * 2 * D_MODEL * 2 + 2 * N_MEM * D_MODEL * 2) + 4 * N_MEM * 2 * D_MODEL * 4
    return pl.pallas_call(
        _mem_kv_kernel,
        grid=(batch,),
        in_specs=[
            pl.BlockSpec((N_MEM, D_MODEL), lambda b: (b, 0)),
            pl.BlockSpec((1, D_MODEL), lambda b: (0, 0)),
            pl.BlockSpec((D_MODEL, 2 * D_MODEL), lambda b: (0, 0)),
            pl.BlockSpec((1, CROSS_HEAD_DIM), lambda b: (0, 0)),
        ],
        out_specs=[pl.BlockSpec((1, N_MEM, D_MODEL), lambda b: (b, 0, 0)),
                   pl.BlockSpec((1, N_MEM, D_MODEL), lambda b: (b, 0, 0))],
        out_shape=[jax.ShapeDtypeStruct((batch, N_MEM, D_MODEL), BF16),
                   jax.ShapeDtypeStruct((batch, N_MEM, D_MODEL), BF16)],
        compiler_params=pltpu.CompilerParams(
            dimension_semantics=("arbitrary",), vmem_limit_bytes=_vmem_limit(est)),
        name="mem_kv",
    )(mem2d, g_mem, w_ckv, k_g)


def _route(logits_t):
    gl = logits_t[0:N_GROUPS]
    gmax = jnp.max(gl, axis=0, keepdims=True)
    eg = jnp.exp(gl - gmax)
    p_group = eg / jnp.sum(eg, axis=0, keepdims=True)
    p_g = jnp.max(p_group, axis=0, keepdims=True)
    g_idx = jnp.full_like(p_g, float(N_GROUPS))
    for g in reversed(range(N_GROUPS)):
        g_idx = jnp.where(p_group[g:g + 1] == p_g, float(g), g_idx)

    sel = []
    for j in range(EXPERTS_PER_GROUP):
        acc = jnp.zeros_like(p_g)
        for g in range(N_GROUPS):
            r = N_GROUPS + g * EXPERTS_PER_GROUP + j
            acc = jnp.where(g_idx == float(g), logits_t[r:r + 1], acc)
        sel.append(acc)
    smax = functools.reduce(jnp.maximum, sel)
    es = [jnp.exp(s - smax) for s in sel]
    den = functools.reduce(jnp.add, es)
    p_in = [e / den for e in es]

    v0 = functools.reduce(jnp.maximum, p_in)
    i0 = jnp.full_like(v0, float(EXPERTS_PER_GROUP))
    for j in reversed(range(EXPERTS_PER_GROUP)):
        i0 = jnp.where(p_in[j] == v0, float(j), i0)
    rest = [jnp.where(i0 == float(j), -1.0, p_in[j]) for j in range(EXPERTS_PER_GROUP)]
    v1 = functools.reduce(jnp.maximum, rest)
    i1 = jnp.full_like(v1, float(EXPERTS_PER_GROUP))
    for j in reversed(range(EXPERTS_PER_GROUP)):
        i1 = jnp.where(rest[j] == v1, float(j), i1)

    tot = v0 + v1
    w0 = p_g * (v0 / tot)
    w1 = p_g * (v1 / tot)
    first = i0 < i1
    a = jnp.where(first, i0, i1)
    b = jnp.where(first, i1, i0)
    wa = jnp.where(first, w0, w1)
    wb = jnp.where(first, w1, w0)
    pair = jnp.where(a == 0.0, 0.0, jnp.where(a == 1.0, 3.0, 5.0)) + (b - a - 1.0)
    bucket = g_idx * float(len(PAIRS)) + pair
    return bucket, wa, wb


def _cross_router_kernel(q_ref, k_ref, v_ref, x1_ref, wco_ref, gf_ref, whi_ref, wlo_ref, rb_ref,
                         xa_ref, rt_ref, *, tm):
    q = q_ref[...]
    k = k_ref[0]
    v = v_ref[0]
    outs = []
    for hh in range(N_HEADS):
        sl = slice(hh * CROSS_HEAD_DIM, (hh + 1) * CROSS_HEAD_DIM)
        s = _nt_dot(q[:, sl], k[:, sl])
        p = jnp.exp(s - jnp.max(s, axis=-1, keepdims=True))
        l = jnp.sum(p, axis=-1, keepdims=True)
        o = jnp.dot(p.astype(BF16), v[:, sl], preferred_element_type=F32) / l
        outs.append(o.astype(BF16))
    o = jnp.concatenate(outs, axis=1)
    x2 = x1_ref[...] + jnp.dot(o, wco_ref[...], preferred_element_type=F32)
    xa_ref[:, :D_MODEL] = x2

    h3 = _rms(x2, gf_ref[...])
    hi = h3.astype(BF16)
    lo = (h3 - hi.astype(F32)).astype(BF16)
    lt = _nt_dot(whi_ref[...], hi) + (_nt_dot(whi_ref[...], lo) + _nt_dot(wlo_ref[...], hi))
    lt = lt + rb_ref[...]
    bucket, wa, wb = _route(lt)

    sub = lax.broadcasted_iota(I32, (8, tm), 0)
    rows = jnp.where(sub == 0, bucket, jnp.where(sub == 1, wa, jnp.where(sub == 2, wb, 0.0)))
    rt_ref[0] = rows
    sub_a = lax.broadcasted_iota(I32, (AUX_COLS, tm), 0)
    aux_t = jnp.where(sub_a == 0, bucket, jnp.where(sub_a == 1, wa, jnp.where(sub_a == 2, wb, 0.0)))
    xa_ref[:, D_MODEL:] = aux_t.T


def _cross_router(qc, kc, vc, x1, w_co, g_ffn, wr_hi, wr_lo, rb, *, batch, seq):
    tm = min(CROSS_ROWS, seq)
    ns = seq // tm
    n = batch * seq
    est = 2 * (tm * D_MODEL * 2 + 2 * N_MEM * D_MODEL * 2 + tm * D_MODEL * 4 + D_MODEL * D_MODEL * 2
               + tm * ROW_W * 4) + 10 * tm * D_MODEL * 4
    return pl.pallas_call(
        functools.partial(_cross_router_kernel, tm=tm),
        grid=(batch, ns),
        in_specs=[
            pl.BlockSpec((tm, D_MODEL), lambda b, s: (b * ns + s, 0)),
            pl.BlockSpec((1, N_MEM, D_MODEL), lambda b, s: (b, 0, 0)),
            pl.BlockSpec((1, N_MEM, D_MODEL), lambda b, s: (b, 0, 0)),
            pl.BlockSpec((tm, D_MODEL), lambda b, s: (b * ns + s, 0)),
            pl.BlockSpec((D_MODEL, D_MODEL), lambda b, s: (0, 0)),
            pl.BlockSpec((1, D_MODEL), lambda b, s: (0, 0)),
            pl.BlockSpec((32, D_MODEL), lambda b, s: (0, 0)),
            pl.BlockSpec((32, D_MODEL), lambda b, s: (0, 0)),
            pl.BlockSpec((32, 1), lambda b, s: (0, 0)),
        ],
        out_specs=[pl.BlockSpec((tm, ROW_W), lambda b, s: (b * ns + s, 0)),
                   pl.BlockSpec((1, 8, tm), lambda b, s: (b * ns + s, 0, 0))],
        out_shape=[jax.ShapeDtypeStruct((n, ROW_W), F32),
                   jax.ShapeDtypeStruct((n // tm, 8, tm), F32)],
        compiler_params=pltpu.CompilerParams(
            dimension_semantics=("arbitrary", "arbitrary"), vmem_limit_bytes=_vmem_limit(est)),
        name="cross_router",
    )(qc, kc, vc, x1, w_co, g_ffn, wr_hi, wr_lo, rb)


def _rank_kernel(rt_ref, rank_ref, cnt_ref, carry_sc, *, tm):
    @pl.when(pl.program_id(0) == 0)
    def _():
        carry_sc[...] = jnp.zeros_like(carry_sc)

    bucket = rt_ref[0, 0:1, :]
    sub = lax.broadcasted_iota(I32, (32, tm), 0).astype(F32)
    hit = sub == bucket
    r = lax.broadcasted_iota(I32, (tm, tm), 0)
    c = lax.broadcasted_iota(I32, (tm, tm), 1)
    upper = jnp.where(r <= c, 1.0, 0.0).astype(BF16)
    cum = jnp.dot(jnp.where(hit, 1.0, 0.0).astype(BF16), upper, preferred_element_type=F32)
    carry = carry_sc[:, 0:1]
    rank = jnp.sum(jnp.where(hit, cum - 1.0 + carry, 0.0), axis=0, keepdims=True)
    rank_ref[0] = rank.astype(I32)
    new_carry = carry_sc[...] + jnp.broadcast_to(cum[:, tm - 1:tm], carry_sc.shape)
    carry_sc[...] = new_carry
    cnt_ref[...] = new_carry


def _rank(rt):
    nt, _, tm = rt.shape
    return pl.pallas_call(
        functools.partial(_rank_kernel, tm=tm),
        grid=(nt,),
        in_specs=[pl.BlockSpec((1, 8, tm), lambda i: (i, 0, 0))],
        out_specs=[pl.BlockSpec((1, 1, tm), lambda i: (i, 0, 0)),
                   pl.BlockSpec((32, LANES), lambda i: (0, 0))],
        out_shape=[jax.ShapeDtypeStruct((nt, 1, tm), I32), jax.ShapeDtypeStruct((32, LANES), F32)],
        scratch_shapes=[pltpu.VMEM((32, LANES), F32)],
        compiler_params=pltpu.CompilerParams(dimension_semantics=("arbitrary",)),
        name="rank",
    )(rt)


def _row_copies(pos_ref, base, hbm, vmem, sem, n_rows, *, to_hbm, wait):
    if wait:
        h = hbm.at[pl.ds(0, n_rows)]
        cp = pltpu.make_async_copy(vmem, h, sem) if to_hbm else pltpu.make_async_copy(h, vmem, sem)
        cp.wait()
        return

    def body(r, carry):
        h = hbm.at[pl.ds(pos_ref[base + r], 1)]
        v = vmem.at[pl.ds(r, 1)]
        cp = pltpu.make_async_copy(v, h, sem) if to_hbm else pltpu.make_async_copy(h, v, sem)
        cp.start()
        return carry
    lax.fori_loop(0, n_rows, body, 0, unroll=8)


def _dispatch_kernel(pos_ref, zrow_ref, zflag_ref, nact_ref, xa_ref, xs_ref, zbuf, zsem, sem, *,
                     n_tiles, rows):
    @pl.when(pl.program_id(0) == 0)
    def _():
        _zero_partial_tiles(zrow_ref, zflag_ref, nact_ref, xs_ref, zbuf, zsem, n_tiles=n_tiles)

    base = pl.program_id(0) * rows
    _row_copies(pos_ref, base, xs_ref, xa_ref, sem, rows, to_hbm=True, wait=False)
    _row_copies(pos_ref, base, xs_ref, xa_ref, sem, rows, to_hbm=True, wait=True)


def _zero_partial_tiles(zrow_ref, zflag_ref, nact_ref, xs_ref, zbuf, zsem, *, n_tiles):
    zbuf[...] = jnp.zeros_like(zbuf)

    def zero_copy(row):
        return pltpu.make_async_copy(zbuf, xs_ref.at[pl.ds(pl.multiple_of(row, MOE_ROWS), MOE_ROWS)], zsem)

    def bucket_tiles(op):
        def body(b, carry):
            @pl.when(zflag_ref[b] > 0)
            def _():
                op(zero_copy(zrow_ref[b]))
            return carry
        lax.fori_loop(0, N_PAIR_BUCKETS, body, 0)

    def unused_tiles(op):
        def body(i, carry):
            op(zero_copy(i * MOE_ROWS))
            return carry
        lax.fori_loop(nact_ref[0], n_tiles, body, 0)

    bucket_tiles(lambda cp: cp.start())
    unused_tiles(lambda cp: cp.start())
    bucket_tiles(lambda cp: cp.wait())
    unused_tiles(lambda cp: cp.wait())


def _dispatch(pos, zrow, zflag, n_active, xa, *, n_tiles):
    n_tok = xa.shape[0]
    n_sorted = n_tiles * MOE_ROWS
    rows = min(DMA_ROWS, n_tok)
    return pl.pallas_call(
        functools.partial(_dispatch_kernel, n_tiles=n_tiles, rows=rows),
        grid_spec=pltpu.PrefetchScalarGridSpec(
            num_scalar_prefetch=4,
            grid=(n_tok // rows,),
            in_specs=[pl.BlockSpec((rows, ROW_W), lambda i, *_: (i, 0))],
            out_specs=pl.BlockSpec(memory_space=pl.ANY),
            scratch_shapes=[pltpu.VMEM((MOE_ROWS, ROW_W), F32), pltpu.SemaphoreType.DMA(()),
                            pltpu.SemaphoreType.DMA(())],
        ),
        out_shape=jax.ShapeDtypeStruct((n_sorted, ROW_W), F32),
        compiler_params=pltpu.CompilerParams(dimension_semantics=("arbitrary",), has_side_effects=True),
        name="dispatch",
    )(pos, zrow, zflag, n_active, xa)


def _combine_kernel(pos_ref, ys_ref, out_ref, sem, *, rows):
    base = pl.program_id(0) * rows
    _row_copies(pos_ref, base, ys_ref, out_ref, sem, rows, to_hbm=False, wait=False)
    _row_copies(pos_ref, base, ys_ref, out_ref, sem, rows, to_hbm=False, wait=True)


def _combine(pos, ys, *, n_tok):
    rows = min(DMA_ROWS, n_tok)
    return pl.pallas_call(
        functools.partial(_combine_kernel, rows=rows),
        grid_spec=pltpu.PrefetchScalarGridSpec(
            num_scalar_prefetch=1,
            grid=(n_tok // rows,),
            in_specs=[pl.BlockSpec(memory_space=pl.ANY)],
            out_specs=pl.BlockSpec((rows, D_MODEL), lambda i, *_: (i, 0)),
            scratch_shapes=[pltpu.SemaphoreType.DMA(())],
        ),
        out_shape=jax.ShapeDtypeStruct((n_tok, D_MODEL), F32),
        compiler_params=pltpu.CompilerParams(dimension_semantics=("arbitrary",), has_side_effects=True),
        name="combine",
    )(pos, ys)


def _moe_kernel(ta_ref, tb_ref, xi_ref, nact_ref, xs_ref, gf_ref, wga_ref, wua_ref, wda_ref,
                wgb_ref, wub_ref, wdb_ref, ys_ref):
    i = pl.program_id(0)

    @pl.when(i < nact_ref[0])
    def _():
        x2 = xs_ref[:, :D_MODEL]
        aux = xs_ref[:, D_MODEL:]
        wa = aux[:, 1:2]
        wb = aux[:, 2:3]
        h = _rms(x2, gf_ref[...]).astype(BF16)

        def mlp(wg, wu, wd):
            g = jnp.dot(h, wg[0], preferred_element_type=F32)
            u = jnp.dot(h, wu[0], preferred_element_type=F32)
            act = (g / (1.0 + jnp.exp(-g))) * u
            return jnp.dot(act.astype(BF16), wd[0], preferred_element_type=F32)

        ys_ref[...] = x2 + (wa * mlp(wga_ref, wua_ref, wda_ref) + wb * mlp(wgb_ref, wub_ref, wdb_ref))

    @pl.when(i >= nact_ref[0])
    def _():
        ys_ref[...] = jnp.zeros_like(ys_ref)


def _moe(tile_a, tile_b, tile_x, n_active, xs, g_ffn, w_gate, w_up, w_down):
    n_tiles = xs.shape[0] // MOE_ROWS
    t = MOE_ROWS
    wa_map = lambda i, ta, tb, xi, na: (ta[i], 0, 0)
    wb_map = lambda i, ta, tb, xi, na: (tb[i], 0, 0)
    est = 2 * (t * ROW_W * 4 + 6 * D_MODEL * D_EXPERT * 2 + t * D_MODEL * 4) + 10 * t * D_MODEL * 4
    return pl.pallas_call(
        _moe_kernel,
        grid_spec=pltpu.PrefetchScalarGridSpec(
            num_scalar_prefetch=4,
            grid=(n_tiles,),
            in_specs=[
                pl.BlockSpec((t, ROW_W), lambda i, ta, tb, xi, na: (xi[i], 0)),
                pl.BlockSpec((1, D_MODEL), lambda i, ta, tb, xi, na: (0, 0)),
                pl.BlockSpec((1, D_MODEL, D_EXPERT), wa_map),
                pl.BlockSpec((1, D_MODEL, D_EXPERT), wa_map),
                pl.BlockSpec((1, D_EXPERT, D_MODEL), wa_map),
                pl.BlockSpec((1, D_MODEL, D_EXPERT), wb_map),
                pl.BlockSpec((1, D_MODEL, D_EXPERT), wb_map),
                pl.BlockSpec((1, D_EXPERT, D_MODEL), wb_map),
            ],
            out_specs=pl.BlockSpec((t, D_MODEL), lambda i, ta, tb, xi, na: (i, 0)),
        ),
        out_shape=jax.ShapeDtypeStruct((n_tiles * t, D_MODEL), F32),
        compiler_params=pltpu.CompilerParams(
            dimension_semantics=("arbitrary",), vmem_limit_bytes=_vmem_limit(est)),
        name="moe",
    )(tile_a, tile_b, tile_x, n_active, xs, g_ffn, w_gate, w_up, w_down, w_gate, w_up, w_down)


def _t5_bucket(rel):
    nb = N_BUCKETS_T5 // 2
    max_exact = nb // 2
    ret = (rel > 0).astype(I32) * nb
    n = jnp.abs(rel)
    nf = jnp.maximum(n, 1).astype(F32)
    large = max_exact + (jnp.log(nf / max_exact) / math.log(MAX_DISTANCE / max_exact)
                         * (nb - max_exact)).astype(I32)
    large = jnp.minimum(large, nb - 1)
    return ret + jnp.where(n < max_exact, n, large)


def _diff_bias(rel_bias, t):
    assert t % CHUNK == 0 and t >= MAX_DISTANCE
    n_rel = 3 * t
    vec = rel_bias[_t5_bucket(jnp.arange(n_rel, dtype=I32) - (2 * t - 1))].astype(F32).T
    toe = jnp.tile(vec, (1, t))[:, :t * (n_rel - 1)].reshape(N_HEADS, t, n_rel - 1)
    qpos = jnp.arange(t, dtype=I32)[:, None]
    kpos = jnp.arange(t, dtype=I32)[None, :]
    far = rel_bias[_t5_bucket(jnp.asarray(-2 * t, I32))].astype(F32)[:, None, None]
    b0 = (toe[:, :, 2 * t - 1:3 * t - 1] - far) * LOG2E
    b0 = jnp.where((kpos // CHUNK <= qpos // CHUNK)[None], b0, NEG_INF)
    b1 = (toe[:, :, t - 1:2 * t - 1] - far) * LOG2E
    return jnp.swapaxes(jnp.stack([b0, b1], axis=1), -1, -2)


def _tile_plan(counts, n_tiles):
    t = MOE_ROWS
    tiles = (counts + t - 1) // t
    ends = jnp.cumsum(tiles)
    starts = ends - tiles
    n_active = ends[-1]
    tile_idx = jnp.minimum(jnp.arange(n_tiles, dtype=I32), n_active - 1)
    tile_bucket = jnp.minimum(jnp.sum((ends[None, :] <= tile_idx[:, None]).astype(I32), axis=1), N_PAIR_BUCKETS - 1)
    pa = jnp.asarray([p[0] for p in PAIRS], I32)
    pb = jnp.asarray([p[1] for p in PAIRS], I32)
    grp = tile_bucket // len(PAIRS)
    pair = tile_bucket % len(PAIRS)
    tile_a = grp * EXPERTS_PER_GROUP + pa[pair]
    tile_b = grp * EXPERTS_PER_GROUP + pb[pair]
    row_off = (starts * t).astype(I32)
    zrow = ((ends - 1) * t).astype(I32)
    zflag = (counts > 0).astype(I32)
    return row_off, zrow, zflag, tile_a.astype(I32), tile_b.astype(I32), tile_idx.astype(I32), n_active.astype(I32)


def kernel(x, mem, rel_bias, norm_mix_g, w_in, b_forget, diff_q_norm_g, diff_k_norm_g, diff_lambda_q1, diff_lambda_k1, diff_lambda_q2, diff_lambda_k2, diff_subln_g, fox_q_norm_g, fox_k_norm_g, fox_out_norm_g, w_out, norm_cross_g, norm_mem_g, w_cq, w_ckv, cross_q_norm_g, cross_k_norm_g, w_co, norm_ffn_g, w_group_router, b_group_router, w_expert_router, b_expert_router, w_exp_gate, w_exp_up, w_exp_down):
    batch, seq, d = x.shape
    assert d == D_MODEL and norm_mix_g.shape[0] == 1 and mem.shape[1] == N_MEM
    n_tok = batch * seq
    assert seq % ATTN_TILE == 0 or seq < ATTN_TILE
    l = 0

    w_qk = jnp.concatenate([w_in[l, :, :1024], w_in[l, :, 1536:2560]], axis=1).astype(BF16)
    w_vt = jnp.concatenate([w_in[l, :, 1024:1536], w_in[l, :, 2560:3072]], axis=1).T.astype(BF16)
    wft = jnp.zeros((8, D_MODEL), F32).at[:N_HEADS].set(w_in[l, :, 3072:].T).astype(BF16)
    bf = jnp.zeros((8, 1), F32).at[:N_HEADS, 0].set(b_forget[l].astype(F32))
    ones = jnp.ones((HEAD_DIM,), F32)
    qg = jnp.stack([
        jnp.tile(diff_q_norm_g[l].astype(F32), 2) * (DIFF_QK_DIM ** -0.5 * LOG2E),
        jnp.tile(diff_k_norm_g[l].astype(F32), 2),
        fox_q_norm_g[l].astype(F32) * (HEAD_DIM ** -0.5 * LOG2E), fox_k_norm_g[l].astype(F32),
        ones, ones, ones, ones])
    lam = (jnp.exp(jnp.sum(diff_lambda_q1[l].astype(F32) * diff_lambda_k1[l].astype(F32)))
           - jnp.exp(jnp.sum(diff_lambda_q2[l].astype(F32) * diff_lambda_k2[l].astype(F32)))
           + LAM_INIT).reshape(1)
    t_attn = min(ATTN_TILE, seq)
    bias_tiles_t = _diff_bias(rel_bias, t_attn)
    wr = jnp.zeros((32, D_MODEL), F32)
    wr = wr.at[:N_GROUPS].set(w_group_router[l].T).at[N_GROUPS:N_GROUPS + N_EXPERTS].set(w_expert_router[l].T)
    wr_hi = wr.astype(BF16)
    wr_lo = (wr - wr_hi.astype(F32)).astype(BF16)
    rb = jnp.zeros((32, 1), F32)
    rb = rb.at[:N_GROUPS, 0].set(b_group_router[l]).at[N_GROUPS:N_GROUPS + N_EXPERTS, 0].set(b_expert_router[l])

    row = lambda v: v.astype(F32).reshape(1, -1)
    x2d = x.reshape(n_tok, D_MODEL)

    qk, vt, cum = _mix_proj(x2d, row(norm_mix_g[l]), w_qk, w_vt, wft, bf, qg, batch=batch, seq=seq)
    cum4 = cum[:, :N_HEADS].reshape(batch, N_HEADS, 1, seq)
    mixed = _attn(qk, vt, bias_tiles_t, lam, cum4, row(diff_subln_g[l]), row(fox_out_norm_g[l]),
                  batch=batch, seq=seq)
    x1, qc = _out_q(x2d, mixed, w_out[l].astype(BF16), row(norm_cross_g[l]),
                    w_cq[l].astype(BF16), row(cross_q_norm_g[l]))

    kc, vc = _mem_kv(mem.reshape(batch * N_MEM, D_MODEL), row(norm_mem_g[l]), w_ckv[l].astype(BF16),
                     row(cross_k_norm_g[l]), batch=batch)
    xa, rt = _cross_router(qc, kc, vc, x1, w_co[l].astype(BF16), row(norm_ffn_g[l]), wr_hi, wr_lo, rb,
                           batch=batch, seq=seq)

    rank, cnt = _rank(rt)
    counts = cnt[:N_PAIR_BUCKETS, 0].astype(I32)
    n_tiles = n_tok // MOE_ROWS + N_PAIR_BUCKETS
    row_off, zrow, zflag, tile_a, tile_b, tile_x, n_active = _tile_plan(counts, n_tiles)
    bucket = rt[:, 0, :].reshape(n_tok).astype(I32)
    pos = row_off[bucket] + rank.reshape(n_tok)
    n_active = n_active.reshape(1)
    xs = _dispatch(pos, zrow, zflag, n_active, xa, n_tiles=n_tiles)
    ys = _moe(tile_a, tile_b, tile_x, n_active, xs, row(norm_ffn_g[l]),
              w_exp_gate[l].astype(BF16), w_exp_up[l].astype(BF16), w_exp_down[l].astype(BF16))
    out = _combine(pos, ys, n_tok=n_tok)
    return out.reshape(batch, seq, D_MODEL)
```

```python
import functools
import math

import jax
import jax.numpy as jnp
import numpy as np
from jax import lax
from jax.experimental import pallas as pl
from jax.experimental.pallas import tpu as pltpu

F32 = jnp.float32
BF16 = jnp.bfloat16
I32 = jnp.int32

D_MODEL = 1024
CHUNK = 64
N_MEM = 256
N_HEADS = 4
HEAD_DIM = 128
DIFF_QK_DIM = 64
CROSS_HEAD_DIM = 256
N_BUCKETS_T5 = 32
MAX_DISTANCE = 128
N_GROUPS = 4
EXPERTS_PER_GROUP = 4
N_EXPERTS = N_GROUPS * EXPERTS_PER_GROUP
D_EXPERT = 512
EPS = 1e-6
NEG_INF = -1e30
LAM_INIT = 0.8 - 0.6 * math.exp(-0.3 * 0)
LOG2E = math.log2(math.e)

PAIRS = ((0, 1), (0, 2), (0, 3), (1, 2), (1, 3), (2, 3))
N_PAIR_BUCKETS = N_GROUPS * len(PAIRS)

LANES = 128
VMEM_LIMIT_CAP = 56 * 1024 * 1024

PROJ_ROWS = 512
ATTN_TILE = 256
CROSS_ROWS = 512
MOE_ROWS = 256
XA_ROWS = D_MODEL // LANES + 1


def _vmem_limit(nbytes):
    return int(min(max(nbytes * 5 // 4, 32 * 1024 * 1024), VMEM_LIMIT_CAP))


def _nt_dot(a, b):
    return lax.dot_general(a, b, (((1,), (1,)), ((), ())), preferred_element_type=F32)


def _rms(x, g):
    ms = jnp.mean(x * x, axis=-1, keepdims=True)
    return x * lax.rsqrt(ms + EPS) * g


def _mix_proj_kernel(x_ref, g_ref, w_ref, wvt_ref, wft_ref, bf_ref, qg_ref, o_ref, vt_ref, cum_ref, carry_ref,
                     *, tm):
    si = pl.program_id(1)
    h = _rms(x_ref[...], g_ref[...]).astype(BF16)
    lane = lax.broadcasted_iota(I32, (tm, HEAD_DIM), 1)
    lo = lane < DIFF_QK_DIM
    for kind in range(4):
        p = jnp.dot(h, w_ref[:, kind * 512:(kind + 1) * 512], preferred_element_type=F32)
        for hh in range(N_HEADS):
            ph = p[:, hh * HEAD_DIM:(hh + 1) * HEAD_DIM]
            if kind in (0, 1):
                sq = ph * ph
                s_lo = jnp.sum(jnp.where(lo, sq, 0.0), axis=-1, keepdims=True)
                s_hi = jnp.sum(jnp.where(lo, 0.0, sq), axis=-1, keepdims=True)
                ms = jnp.where(lo, s_lo, s_hi) * (1.0 / DIFF_QK_DIM)
                ph = ph * lax.rsqrt(ms + EPS) * qg_ref[kind:kind + 1, :]
            else:
                ph = _rms(ph, qg_ref[kind:kind + 1, :])
            o_ref[0, kind * N_HEADS + hh] = ph.astype(BF16)
    for kind in range(2):
        pt = _nt_dot(wvt_ref[kind * 512:(kind + 1) * 512, :], h)
        for hh in range(N_HEADS):
            vt_ref[0, kind * N_HEADS + hh] = pt[hh * HEAD_DIM:(hh + 1) * HEAD_DIM, :].astype(BF16)

    z = _nt_dot(wft_ref[...], h) + bf_ref[...]
    logf = jnp.minimum(z, 0.0) - jnp.log(1.0 + jnp.exp(-jnp.abs(z)))
    lane8 = lax.broadcasted_iota(I32, (8, tm), 1)
    c = logf * LOG2E
    k = 1
    while k < tm:
        c = c + jnp.where(lane8 >= k, pltpu.roll(c, k, axis=1), 0.0)
        k *= 2

    @pl.when(si == 0)
    def _():
        carry_ref[...] = jnp.zeros_like(carry_ref)

    c = c + carry_ref[:, 0:1]
    cum_ref[0] = c
    carry_ref[...] = jnp.broadcast_to(c[:, tm - 1:tm], carry_ref.shape)


def _mix_proj(x2d, g, w_qk, w_vt, wft, bf, qg, *, batch, seq):
    tm = min(PROJ_ROWS, seq)
    ns = seq // tm
    est = 2 * (tm * D_MODEL * 4 + D_MODEL * 3072 * 2 + 24 * tm * HEAD_DIM * 2) + 8 * tm * D_MODEL
    return pl.pallas_call(
        functools.partial(_mix_proj_kernel, tm=tm),
        grid=(batch, ns),
        in_specs=[
            pl.BlockSpec((tm, D_MODEL), lambda b, s: (b * ns + s, 0)),
            pl.BlockSpec((1, D_MODEL), lambda b, s: (0, 0)),
            pl.BlockSpec((D_MODEL, 2048), lambda b, s: (0, 0)),
            pl.BlockSpec((1024, D_MODEL), lambda b, s: (0, 0)),
            pl.BlockSpec((8, D_MODEL), lambda b, s: (0, 0)),
            pl.BlockSpec((8, 1), lambda b, s: (0, 0)),
            pl.BlockSpec((8, HEAD_DIM), lambda b, s: (0, 0)),
        ],
        out_specs=[
            pl.BlockSpec((1, 16, tm, HEAD_DIM), lambda b, s: (b, 0, s, 0)),
            pl.BlockSpec((1, 8, HEAD_DIM, tm), lambda b, s: (b, 0, 0, s)),
            pl.BlockSpec((1, 8, tm), lambda b, s: (b, 0, s)),
        ],
        out_shape=[
            jax.ShapeDtypeStruct((batch, 16, seq, HEAD_DIM), BF16),
            jax.ShapeDtypeStruct((batch, 8, HEAD_DIM, seq), BF16),
            jax.ShapeDtypeStruct((batch, 8, seq), F32),
        ],
        scratch_shapes=[pltpu.VMEM((8, LANES), F32)],
        compiler_params=pltpu.CompilerParams(
            dimension_semantics=("arbitrary", "arbitrary"), vmem_limit_bytes=_vmem_limit(est)),
        name="mix_proj",
    )(x2d, g, w_qk, w_vt, wft, bf, qg)


def _init_softmax(m_sc, l_sc, acc_sc):
    m_sc[...] = jnp.full(m_sc.shape, NEG_INF, F32)
    l_sc[...] = jnp.zeros_like(l_sc)
    acc_sc[...] = jnp.zeros_like(acc_sc)


N_CHAINS = 3 * N_HEADS


def _attn_kernel(lam_ref, dq_ref, dk_ref, dvt_ref, fq_ref, fk_ref, fvt_ref, bias_ref, c_ref,
                 gd_ref, gf_ref, o_ref, m_sc, l_sc, acc_sc, qm_sc, ccol_sc, s_sc, mt_sc, *, t, seq):
    qi = pl.program_id(1)
    _init_softmax(m_sc, l_sc, acc_sc)
    key = lax.broadcasted_iota(I32, (t, t), 0)
    qry = lax.broadcasted_iota(I32, (t, t), 1)

    lane = lax.broadcasted_iota(I32, (t, HEAD_DIM), 1)
    for hh in range(N_HEADS):
        q = dq_ref[0, hh]
        zero = jnp.zeros_like(q)
        qm_sc[2 * hh] = jnp.where(lane < DIFF_QK_DIM, q, zero)
        qm_sc[2 * hh + 1] = jnp.where(lane < DIFF_QK_DIM, zero, q)

    @pl.when(qi == 0)
    def _():
        for hh in range(N_HEADS):
            for j in range(seq // t):
                row = c_ref[0, hh, :, j * t:(j + 1) * t]
                ccol_sc[hh, j * t:(j + 1) * t, :] = jnp.sum(
                    jnp.where(key == qry, jnp.broadcast_to(row, (t, t)), 0.0), axis=1, keepdims=True)

    q_off = pl.multiple_of(qi * t, t)

    def tile(j, where):
        off = pl.multiple_of(j * t, t)

        for hh in range(N_HEADS):
            k = dk_ref[0, hh, pl.ds(off, t), :]
            for c in range(2):
                s_t = _nt_dot(k, qm_sc[2 * hh + c])
                if where != "far":
                    s_t = s_t + bias_ref[hh, 1 if where == "near" else 0]
                s_sc[2 * hh + c] = s_t
                mt_sc[2 * hh + c] = jnp.max(s_t, axis=0, keepdims=True)
        for hh in range(N_HEADS):
            k = fk_ref[0, hh, pl.ds(off, t), :]
            c_k = ccol_sc[hh, pl.ds(off, t), :]
            s_t = _nt_dot(k, fq_ref[0, hh]) - c_k
            if where == "diag":
                s_t = jnp.where(key <= qry, s_t, NEG_INF)
            s_sc[2 * N_HEADS + hh] = s_t
            mt_sc[2 * N_HEADS + hh] = jnp.max(s_t, axis=0, keepdims=True) + c_ref[0, hh, :, pl.ds(q_off, t)]

        for slot in range(N_CHAINS):
            vt_ref = dvt_ref if slot < 2 * N_HEADS else fvt_ref
            hh = slot // 2 if slot < 2 * N_HEADS else slot - 2 * N_HEADS
            v_t = vt_ref[0, hh, :, pl.ds(off, t)]
            m_prev = m_sc[slot]
            m_new = jnp.maximum(m_prev, mt_sc[slot])
            alpha = jnp.exp2(m_prev - m_new)
            if slot < 2 * N_HEADS:
                p_t = jnp.exp2(s_sc[slot] - m_new)
            else:
                p_t = jnp.exp2(s_sc[slot] - (m_new - c_ref[0, hh, :, pl.ds(q_off, t)]))
            l_sc[slot] = alpha * l_sc[slot] + jnp.sum(p_t, axis=0, keepdims=True)
            acc_sc[slot] = alpha * acc_sc[slot] + jnp.dot(v_t, p_t.astype(BF16), preferred_element_type=F32)
            m_sc[slot] = m_new

    def far_body(j, carry):
        tile(j, "far")
        return carry

    lax.fori_loop(0, jnp.maximum(qi - 1, 0), far_body, 0)

    @pl.when(qi >= 1)
    def _():
        tile(qi - 1, "near")

    tile(qi, "diag")

    for hh in range(N_HEADS):
        a, b = 2 * hh, 2 * hh + 1
        o_t = acc_sc[a] / l_sc[a] - lam_ref[0] * (acc_sc[b] / l_sc[b])
        o_ref[:, hh * HEAD_DIM:(hh + 1) * HEAD_DIM] = (
            _rms(o_t.T, gd_ref[...]) * (1.0 - LAM_INIT)).astype(BF16)
    for hh in range(N_HEADS):
        s = 2 * N_HEADS + hh
        o_t = acc_sc[s] / l_sc[s]
        o_ref[:, (N_HEADS + hh) * HEAD_DIM:(N_HEADS + hh + 1) * HEAD_DIM] = _rms(o_t.T, gf_ref[...]).astype(BF16)


def _attn(qk, vt, bias_tiles_t, lam, cum4, subln_g, fox_g, *, batch, seq):
    t = min(ATTN_TILE, seq)
    nq = seq // t
    head_blk = N_HEADS * seq * HEAD_DIM * 2
    est = (2 * (2 * N_HEADS * t * HEAD_DIM * 2 + 4 * head_blk + N_HEADS * 2 * t * t * 4 + N_HEADS * 8 * seq * 4
                + t * 2 * N_HEADS * HEAD_DIM * 2)
           + N_CHAINS * (HEAD_DIM + 16) * t * 4 + 2 * N_HEADS * t * HEAD_DIM * 2 + N_HEADS * seq * LANES * 4
           + 48 * t * t * 4)
    qspec = lambda blk: pl.BlockSpec((1, N_HEADS, t, HEAD_DIM), lambda b, i: (b, blk, i, 0))
    kspec = lambda blk: pl.BlockSpec((1, N_HEADS, seq, HEAD_DIM), lambda b, i: (b, blk, 0, 0))
    vspec = lambda blk: pl.BlockSpec((1, N_HEADS, HEAD_DIM, seq), lambda b, i: (b, blk, 0, 0))
    return pl.pallas_call(
        functools.partial(_attn_kernel, t=t, seq=seq),
        grid=(batch, nq),
        in_specs=[
            pl.BlockSpec(memory_space=pltpu.SMEM),
            qspec(0), kspec(1), vspec(0),
            qspec(2), kspec(3), vspec(1),
            pl.BlockSpec((N_HEADS, 2, t, t), lambda b, i: (0, 0, 0, 0)),
            pl.BlockSpec((1, N_HEADS, 1, seq), lambda b, i: (b, 0, 0, 0)),
            pl.BlockSpec((1, HEAD_DIM), lambda b, i: (0, 0)),
            pl.BlockSpec((1, HEAD_DIM), lambda b, i: (0, 0)),
        ],
        out_specs=pl.BlockSpec((t, 2 * N_HEADS * HEAD_DIM), lambda b, i: (b * nq + i, 0)),
        out_shape=jax.ShapeDtypeStruct((batch * seq, 2 * N_HEADS * HEAD_DIM), BF16),
        scratch_shapes=[pltpu.VMEM((N_CHAINS, 1, t), F32), pltpu.VMEM((N_CHAINS, 1, t), F32),
                        pltpu.VMEM((N_CHAINS, HEAD_DIM, t), F32),
                        pltpu.VMEM((2 * N_HEADS, t, HEAD_DIM), BF16),
                        pltpu.VMEM((N_HEADS, seq, 1), F32),
                        pltpu.VMEM((N_CHAINS, t, t), F32), pltpu.VMEM((N_CHAINS, 1, t), F32)],
        compiler_params=pltpu.CompilerParams(
            dimension_semantics=("arbitrary", "arbitrary"), vmem_limit_bytes=_vmem_limit(est)),
        name="attn",
    )(lam, qk, qk, vt, qk, qk, vt, bias_tiles_t, cum4, subln_g, fox_g)


def _out_q_kernel(x_ref, m_ref, wo_ref, gc_ref, wq_ref, qg_ref, x1_ref, qc_ref):
    x1 = x_ref[...] + jnp.dot(m_ref[...], wo_ref[...], preferred_element_type=F32)
    x1_ref[...] = x1
    hc = _rms(x1, gc_ref[...]).astype(BF16)
    q = jnp.dot(hc, wq_ref[...], preferred_element_type=F32)
    for hh in range(N_HEADS):
        sl = slice(hh * CROSS_HEAD_DIM, (hh + 1) * CROSS_HEAD_DIM)
        qh = _rms(q[:, sl], qg_ref[...]) * (CROSS_HEAD_DIM ** -0.5)
        qc_ref[:, sl] = qh.astype(BF16)


def _out_q(x2d, mixed, w_out, g_cross, w_cq, q_g):
    n = x2d.shape[0]
    tm = min(PROJ_ROWS, n)
    est = 2 * (tm * D_MODEL * 4 * 2 + tm * D_MODEL * 2 + 2 * D_MODEL * D_MODEL * 2 + tm * D_MODEL * 2) + 6 * tm * D_MODEL * 4
    return pl.pallas_call(
        _out_q_kernel,
        grid=(n // tm,),
        in_specs=[
            pl.BlockSpec((tm, D_MODEL), lambda i: (i, 0)),
            pl.BlockSpec((tm, D_MODEL), lambda i: (i, 0)),
            pl.BlockSpec((D_MODEL, D_MODEL), lambda i: (0, 0)),
            pl.BlockSpec((1, D_MODEL), lambda i: (0, 0)),
            pl.BlockSpec((D_MODEL, D_MODEL), lambda i: (0, 0)),
            pl.BlockSpec((1, CROSS_HEAD_DIM), lambda i: (0, 0)),
        ],
        out_specs=[pl.BlockSpec((tm, D_MODEL), lambda i: (i, 0)),
                   pl.BlockSpec((tm, D_MODEL), lambda i: (i, 0))],
        out_shape=[jax.ShapeDtypeStruct((n, D_MODEL), F32), jax.ShapeDtypeStruct((n, D_MODEL), BF16)],
        compiler_params=pltpu.CompilerParams(
            dimension_semantics=("arbitrary",), vmem_limit_bytes=_vmem_limit(est)),
        name="out_q",
    )(x2d, mixed, w_out, g_cross, w_cq, q_g)


def _mem_kv_kernel(mem_ref, gm_ref, w_ref, kg_ref, k_ref, v_ref):
    mn = _rms(mem_ref[...], gm_ref[...]).astype(BF16)
    kv = jnp.dot(mn, w_ref[...], preferred_element_type=F32)
    for hh in range(N_HEADS):
        sl = slice(hh * CROSS_HEAD_DIM, (hh + 1) * CROSS_HEAD_DIM)
        k_ref[0, :, sl] = _rms(kv[:, sl], kg_ref[...]).astype(BF16)
    v_ref[0] = kv[:, D_MODEL:].astype(BF16)


def _mem_kv(mem2d, g_mem, w_ckv, k_g, *, batch):
    est = 2 * (N_MEM * D_MODEL * 4 + D_MODEL * 2 * D_MODEL * 2 + 2 * N_MEM * D_MODEL * 2) + 4 * N_MEM * 2 * D_MODEL * 4
    return pl.pallas_call(
        _mem_kv_kernel,
        grid=(batch,),
        in_specs=[
            pl.BlockSpec((N_MEM, D_MODEL), lambda b: (b, 0)),
            pl.BlockSpec((1, D_MODEL), lambda b: (0, 0)),
            pl.BlockSpec((D_MODEL, 2 * D_MODEL), lambda b: (0, 0)),
            pl.BlockSpec((1, CROSS_HEAD_DIM), lambda b: (0, 0)),
        ],
        out_specs=[pl.BlockSpec((1, N_MEM, D_MODEL), lambda b: (b, 0, 0)),
                   pl.BlockSpec((1, N_MEM, D_MODEL), lambda b: (b, 0, 0))],
        out_shape=[jax.ShapeDtypeStruct((batch, N_MEM, D_MODEL), BF16),
                   jax.ShapeDtypeStruct((batch, N_MEM, D_MODEL), BF16)],
        compiler_params=pltpu.CompilerParams(
            dimension_semantics=("arbitrary",), vmem_limit_bytes=_vmem_limit(est)),
        name="mem_kv",
    )(mem2d, g_mem, w_ckv, k_g)


def _route(logits_t):
    gl = logits_t[0:N_GROUPS]
    gmax = jnp.max(gl, axis=0, keepdims=True)
    eg = jnp.exp(gl - gmax)
    p_group = eg / jnp.sum(eg, axis=0, keepdims=True)
    p_g = jnp.max(p_group, axis=0, keepdims=True)
    g_idx = jnp.full_like(p_g, float(N_GROUPS))
    for g in reversed(range(N_GROUPS)):
        g_idx = jnp.where(p_group[g:g + 1] == p_g, float(g), g_idx)

    sel = []
    for j in range(EXPERTS_PER_GROUP):
        acc = jnp.zeros_like(p_g)
        for g in range(N_GROUPS):
            r = N_GROUPS + g * EXPERTS_PER_GROUP + j
            acc = jnp.where(g_idx == float(g), logits_t[r:r + 1], acc)
        sel.append(acc)
    smax = functools.reduce(jnp.maximum, sel)
    es = [jnp.exp(s - smax) for s in sel]
    den = functools.reduce(jnp.add, es)
    p_in = [e / den for e in es]

    v0 = functools.reduce(jnp.maximum, p_in)
    i0 = jnp.full_like(v0, float(EXPERTS_PER_GROUP))
    for j in reversed(range(EXPERTS_PER_GROUP)):
        i0 = jnp.where(p_in[j] == v0, float(j), i0)
    rest = [jnp.where(i0 == float(j), -1.0, p_in[j]) for j in range(EXPERTS_PER_GROUP)]
    v1 = functools.reduce(jnp.maximum, rest)
    i1 = jnp.full_like(v1, float(EXPERTS_PER_GROUP))
    for j in reversed(range(EXPERTS_PER_GROUP)):
        i1 = jnp.where(rest[j] == v1, float(j), i1)

    tot = v0 + v1
    w0 = p_g * (v0 / tot)
    w1 = p_g * (v1 / tot)
    first = i0 < i1
    a = jnp.where(first, i0, i1)
    b = jnp.where(first, i1, i0)
    wa = jnp.where(first, w0, w1)
    wb = jnp.where(first, w1, w0)
    pair = jnp.where(a == 0.0, 0.0, jnp.where(a == 1.0, 3.0, 5.0)) + (b - a - 1.0)
    bucket = g_idx * float(len(PAIRS)) + pair
    return bucket, wa, wb


def _cross_router_kernel(q_ref, k_ref, v_ref, x1_ref, wco_ref, gf_ref, whi_ref, wlo_ref, rb_ref,
                         xa_ref, rt_ref, *, tm):
    q = q_ref[...]
    k = k_ref[0]
    v = v_ref[0]
    outs = []
    for hh in range(N_HEADS):
        sl = slice(hh * CROSS_HEAD_DIM, (hh + 1) * CROSS_HEAD_DIM)
        s = _nt_dot(q[:, sl], k[:, sl])
        p = jnp.exp(s - jnp.max(s, axis=-1, keepdims=True))
        l = jnp.sum(p, axis=-1, keepdims=True)
        o = jnp.dot(p.astype(BF16), v[:, sl], preferred_element_type=F32) / l
        outs.append(o.astype(BF16))
    o = jnp.concatenate(outs, axis=1)
    x2 = x1_ref[...] + jnp.dot(o, wco_ref[...], preferred_element_type=F32)
    for j in range(D_MODEL // LANES):
        xa_ref[pl.ds(j, tm, stride=XA_ROWS), :] = x2[:, j * LANES:(j + 1) * LANES]

    h3 = _rms(x2, gf_ref[...])
    hi = h3.astype(BF16)
    lo = (h3 - hi.astype(F32)).astype(BF16)
    lt = _nt_dot(whi_ref[...], hi) + (_nt_dot(whi_ref[...], lo) + _nt_dot(wlo_ref[...], hi))
    lt = lt + rb_ref[...]
    bucket, wa, wb = _route(lt)

    sub = lax.broadcasted_iota(I32, (8, tm), 0)
    rows = jnp.where(sub == 0, bucket, jnp.where(sub == 1, wa, jnp.where(sub == 2, wb, 0.0)))
    rt_ref[0] = rows
    sub_a = lax.broadcasted_iota(I32, (LANES, tm), 0)
    aux_t = jnp.where(sub_a == 0, bucket, jnp.where(sub_a == 1, wa, jnp.where(sub_a == 2, wb, 0.0)))
    xa_ref[pl.ds(XA_ROWS - 1, tm, stride=XA_ROWS), :] = aux_t.T


def _cross_router(qc, kc, vc, x1, w_co, g_ffn, wr_hi, wr_lo, rb, *, batch, seq):
    tm = min(CROSS_ROWS, seq)
    ns = seq // tm
    n = batch * seq
    est = 2 * (tm * D_MODEL * 2 + 2 * N_MEM * D_MODEL * 2 + tm * D_MODEL * 4 + D_MODEL * D_MODEL * 2
               + tm * XA_ROWS * LANES * 4) + 10 * tm * D_MODEL * 4
    return pl.pallas_call(
        functools.partial(_cross_router_kernel, tm=tm),
        grid=(batch, ns),
        in_specs=[
            pl.BlockSpec((tm, D_MODEL), lambda b, s: (b * ns + s, 0)),
            pl.BlockSpec((1, N_MEM, D_MODEL), lambda b, s: (b, 0, 0)),
            pl.BlockSpec((1, N_MEM, D_MODEL), lambda b, s: (b, 0, 0)),
            pl.BlockSpec((tm, D_MODEL), lambda b, s: (b * ns + s, 0)),
            pl.BlockSpec((D_MODEL, D_MODEL), lambda b, s: (0, 0)),
            pl.BlockSpec((1, D_MODEL), lambda b, s: (0, 0)),
            pl.BlockSpec((32, D_MODEL), lambda b, s: (0, 0)),
            pl.BlockSpec((32, D_MODEL), lambda b, s: (0, 0)),
            pl.BlockSpec((32, 1), lambda b, s: (0, 0)),
        ],
        out_specs=[pl.BlockSpec((tm * XA_ROWS, LANES), lambda b, s: (b * ns + s, 0)),
                   pl.BlockSpec((1, 8, tm), lambda b, s: (b * ns + s, 0, 0))],
        out_shape=[jax.ShapeDtypeStruct((n * XA_ROWS, LANES), F32),
                   jax.ShapeDtypeStruct((n // tm, 8, tm), F32)],
        compiler_params=pltpu.CompilerParams(
            dimension_semantics=("arbitrary", "arbitrary"), vmem_limit_bytes=_vmem_limit(est)),
        name="cross_router",
    )(qc, kc, vc, x1, w_co, g_ffn, wr_hi, wr_lo, rb)


def _rank_kernel(rt_ref, rank_ref, cnt_ref, carry_sc, *, tm):
    @pl.when(pl.program_id(0) == 0)
    def _():
        carry_sc[...] = jnp.zeros_like(carry_sc)

    bucket = rt_ref[0, 0:1, :]
    sub = lax.broadcasted_iota(I32, (32, tm), 0).astype(F32)
    hit = sub == bucket
    r = lax.broadcasted_iota(I32, (tm, tm), 0)
    c = lax.broadcasted_iota(I32, (tm, tm), 1)
    upper = jnp.where(r <= c, 1.0, 0.0).astype(BF16)
    cum = jnp.dot(jnp.where(hit, 1.0, 0.0).astype(BF16), upper, preferred_element_type=F32)
    carry = carry_sc[:, 0:1]
    rank = jnp.sum(jnp.where(hit, cum - 1.0 + carry, 0.0), axis=0, keepdims=True)
    rank_ref[0] = rank.astype(I32)
    new_carry = carry_sc[...] + jnp.broadcast_to(cum[:, tm - 1:tm], carry_sc.shape)
    carry_sc[...] = new_carry
    cnt_ref[...] = new_carry


def _rank(rt):
    nt, _, tm = rt.shape
    return pl.pallas_call(
        functools.partial(_rank_kernel, tm=tm),
        grid=(nt,),
        in_specs=[pl.BlockSpec((1, 8, tm), lambda i: (i, 0, 0))],
        out_specs=[pl.BlockSpec((1, 1, tm), lambda i: (i, 0, 0)),
                   pl.BlockSpec((32, LANES), lambda i: (0, 0))],
        out_shape=[jax.ShapeDtypeStruct((nt, 1, tm), I32), jax.ShapeDtypeStruct((32, LANES), F32)],
        scratch_shapes=[pltpu.VMEM((32, LANES), F32)],
        compiler_params=pltpu.CompilerParams(dimension_semantics=("arbitrary",)),
        name="rank",
    )(rt)


def _moe_kernel(ta_ref, tb_ref, nv_ref, nact_ref, idx_ref, xa_hbm, gf_ref, *refs):
    t_rows = MOE_ROWS
    weights = (refs[0:6], refs[6:12])
    out_hbm, xbuf0, xbuf1, ybuf0, ybuf1, gsem, ssem = refs[12:]
    xbuf = (xbuf0, xbuf1)
    ybuf = (ybuf0, ybuf1)
    n_act = nact_ref[0]

    def gather_row(t, r, par):
        return pltpu.make_async_copy(xa_hbm.at[pl.ds(idx_ref[t * t_rows + r] * XA_ROWS, XA_ROWS)],
                                     xbuf[par].at[pl.ds(r * XA_ROWS, XA_ROWS)], gsem.at[par])

    def gather_all(par):
        return pltpu.make_async_copy(xa_hbm.at[pl.ds(0, t_rows * XA_ROWS)], xbuf[par], gsem.at[par])

    def scatter_row(t, r, par):
        return pltpu.make_async_copy(ybuf[par].at[pl.ds(r, 1)],
                                     out_hbm.at[pl.ds(idx_ref[t * t_rows + r], 1)], ssem.at[par])

    def scatter_all(par):
        return pltpu.make_async_copy(ybuf[par], out_hbm.at[pl.ds(0, t_rows)], ssem.at[par])

    def scatter_some(t, par, wait):
        def body(r, carry):
            cp = scatter_row(t, r, par)
            if wait:
                cp.wait()
            else:
                cp.start()
            return carry
        lax.fori_loop(0, nv_ref[t], body, 0)

    def compute(par):
        x2 = jnp.concatenate([xbuf[par][pl.ds(j, t_rows, stride=XA_ROWS), :]
                              for j in range(D_MODEL // LANES)], axis=1)
        aux = xbuf[par][pl.ds(XA_ROWS - 1, t_rows, stride=XA_ROWS), :]
        wa = aux[:, 1:2]
        wb = aux[:, 2:3]
        h = _rms(x2, gf_ref[...]).astype(BF16)
        wga, wua, wda, wgb, wub, wdb = weights[par]

        def mlp(wg, wu, wd):
            g = jnp.dot(h, wg[0], preferred_element_type=F32)
            u = jnp.dot(h, wu[0], preferred_element_type=F32)
            act = (g / (1.0 + jnp.exp(-g))) * u
            return jnp.dot(act.astype(BF16), wd[0], preferred_element_type=F32)

        ybuf[par][...] = x2 + (wa * mlp(wga, wua, wda) + wb * mlp(wgb, wub, wdb))

    @pl.when(pl.program_id(0) == 0)
    def _():
        for r in range(t_rows):
            gather_row(0, r, 0).start()

    for par in range(2):
        t = 2 * pl.program_id(0) + par
        prev = jnp.maximum(t - 1, 0)
        prev_full = jnp.logical_and(t >= 1, nv_ref[prev] == t_rows)

        def front(t=t, par=par):
            gather_all(par).wait()
            nxt = jnp.minimum(t + 1, n_act - 1)
            for r in range(t_rows):
                gather_row(nxt, r, 1 - par).start()

        @pl.when(jnp.logical_and(t < n_act, prev_full))
        def _(t=t, par=par, prev=prev, front=front):
            front()
            for r in range(t_rows):
                scatter_row(prev, r, 1 - par).start()
            compute(par)
            scatter_all(1 - par).wait()

        @pl.when(jnp.logical_and(t < n_act, jnp.logical_and(t >= 1, jnp.logical_not(prev_full))))
        def _(t=t, par=par, prev=prev, front=front):
            front()
            scatter_some(prev, 1 - par, wait=False)
            compute(par)
            scatter_some(prev, 1 - par, wait=True)

        if par == 0:
            @pl.when(t == 0)
            def _(par=par, front=front):
                front()
                compute(par)

        @pl.when(t == n_act)
        def _(par=par, prev=prev):
            gather_all(par).wait()
            scatter_some(prev, 1 - par, wait=False)
            scatter_some(prev, 1 - par, wait=True)


def _moe(tile_a, tile_b, n_valid, n_active, idx, xa, g_ffn, w_gate, w_up, w_down, *, n_slots):
    n_tok = xa.shape[0] // XA_ROWS
    t = MOE_ROWS

    def wspec(shape, experts, par):
        if experts == "a":
            return pl.BlockSpec(shape, lambda g, ta, tb, nv, na, ix: (ta[2 * g + par], 0, 0))
        return pl.BlockSpec(shape, lambda g, ta, tb, nv, na, ix: (tb[2 * g + par], 0, 0))

    up_shape, down_shape = (1, D_MODEL, D_EXPERT), (1, D_EXPERT, D_MODEL)
    w_specs, w_args = [], []
    for par in range(2):
        for experts in ("a", "b"):
            w_specs += [wspec(up_shape, experts, par), wspec(up_shape, experts, par), wspec(down_shape, experts, par)]
            w_args += [w_gate, w_up, w_down]
    est = (2 * 12 * D_MODEL * D_EXPERT * 2 + 2 * t * XA_ROWS * LANES * 4 + 2 * t * D_MODEL * 4
           + 10 * t * D_MODEL * 4)
    return pl.pallas_call(
        _moe_kernel,
        grid_spec=pltpu.PrefetchScalarGridSpec(
            num_scalar_prefetch=5,
            grid=(n_slots // 2,),
            in_specs=[pl.BlockSpec(memory_space=pl.ANY),
                      pl.BlockSpec((1, D_MODEL), lambda g, *_: (0, 0))] + w_specs,
            out_specs=pl.BlockSpec(memory_space=pl.ANY),
            scratch_shapes=[pltpu.VMEM((t * XA_ROWS, LANES), F32), pltpu.VMEM((t * XA_ROWS, LANES), F32),
                            pltpu.VMEM((t, D_MODEL), F32), pltpu.VMEM((t, D_MODEL), F32),
                            pltpu.SemaphoreType.DMA((2,)), pltpu.SemaphoreType.DMA((2,))],
        ),
        out_shape=jax.ShapeDtypeStruct((n_tok, D_MODEL), F32),
        compiler_params=pltpu.CompilerParams(
            dimension_semantics=("arbitrary",), vmem_limit_bytes=_vmem_limit(est), has_side_effects=True),
        name="moe",
    )(tile_a, tile_b, n_valid, n_active, idx, xa, g_ffn, *w_args)


def _t5_bucket(rel):
    nb = N_BUCKETS_T5 // 2
    max_exact = nb // 2
    ret = (rel > 0).astype(I32) * nb
    n = jnp.abs(rel)
    nf = jnp.maximum(n, 1).astype(F32)
    large = max_exact + (jnp.log(nf / max_exact) / math.log(MAX_DISTANCE / max_exact)
                         * (nb - max_exact)).astype(I32)
    large = jnp.minimum(large, nb - 1)
    return ret + jnp.where(n < max_exact, n, large)


def _diff_bias(rel_bias, t):
    assert t % CHUNK == 0 and t >= MAX_DISTANCE
    n_rel = 3 * t
    vec = rel_bias[_t5_bucket(jnp.arange(n_rel, dtype=I32) - (2 * t - 1))].astype(F32).T
    toe = jnp.tile(vec, (1, t))[:, :t * (n_rel - 1)].reshape(N_HEADS, t, n_rel - 1)
    qpos = jnp.arange(t, dtype=I32)[:, None]
    kpos = jnp.arange(t, dtype=I32)[None, :]
    far = rel_bias[_t5_bucket(jnp.asarray(-2 * t, I32))].astype(F32)[:, None, None]
    b0 = (toe[:, :, 2 * t - 1:3 * t - 1] - far) * LOG2E
    b0 = jnp.where((kpos // CHUNK <= qpos // CHUNK)[None], b0, NEG_INF)
    b1 = (toe[:, :, t - 1:2 * t - 1] - far) * LOG2E
    return jnp.swapaxes(jnp.stack([b0, b1], axis=1), -1, -2)


def _tile_plan(counts, n_slots):
    t = MOE_ROWS
    tiles = (counts + t - 1) // t
    ends = jnp.cumsum(tiles)
    starts = ends - tiles
    n_active = ends[-1]
    slot = jnp.arange(n_slots, dtype=I32)
    tile_idx = jnp.minimum(slot, n_active - 1)
    tile_bucket = jnp.minimum(jnp.sum((ends[None, :] <= tile_idx[:, None]).astype(I32), axis=1), N_PAIR_BUCKETS - 1)
    pa = jnp.asarray([p[0] for p in PAIRS], I32)
    pb = jnp.asarray([p[1] for p in PAIRS], I32)
    grp = tile_bucket // len(PAIRS)
    pair = tile_bucket % len(PAIRS)
    tile_a = grp * EXPERTS_PER_GROUP + pa[pair]
    tile_b = grp * EXPERTS_PER_GROUP + pb[pair]
    row_off = (starts * t).astype(I32)
    in_bucket = tile_idx - starts[tile_bucket]
    n_valid = jnp.clip(counts[tile_bucket] - in_bucket * t, 0, t)
    n_valid = jnp.where(slot < n_active, n_valid, 0)
    return row_off, tile_a.astype(I32), tile_b.astype(I32), n_valid.astype(I32), n_active.astype(I32)


def kernel(x, mem, rel_bias, norm_mix_g, w_in, b_forget, diff_q_norm_g, diff_k_norm_g, diff_lambda_q1, diff_lambda_k1, diff_lambda_q2, diff_lambda_k2, diff_subln_g, fox_q_norm_g, fox_k_norm_g, fox_out_norm_g, w_out, norm_cross_g, norm_mem_g, w_cq, w_ckv, cross_q_norm_g, cross_k_norm_g, w_co, norm_ffn_g, w_group_router, b_group_router, w_expert_router, b_expert_router, w_exp_gate, w_exp_up, w_exp_down):
    batch, seq, d = x.shape
    assert d == D_MODEL and norm_mix_g.shape[0] == 1 and mem.shape[1] == N_MEM
    n_tok = batch * seq
    assert seq % ATTN_TILE == 0 or seq < ATTN_TILE
    l = 0

    w_qk = jnp.concatenate([w_in[l, :, :1024], w_in[l, :, 1536:2560]], axis=1).astype(BF16)
    w_vt = jnp.concatenate([w_in[l, :, 1024:1536], w_in[l, :, 2560:3072]], axis=1).T.astype(BF16)
    wft = jnp.zeros((8, D_MODEL), F32).at[:N_HEADS].set(w_in[l, :, 3072:].T).astype(BF16)
    bf = jnp.zeros((8, 1), F32).at[:N_HEADS, 0].set(b_forget[l].astype(F32))
    ones = jnp.ones((HEAD_DIM,), F32)
    qg = jnp.stack([
        jnp.tile(diff_q_norm_g[l].astype(F32), 2) * (DIFF_QK_DIM ** -0.5 * LOG2E),
        jnp.tile(diff_k_norm_g[l].astype(F32), 2),
        fox_q_norm_g[l].astype(F32) * (HEAD_DIM ** -0.5 * LOG2E), fox_k_norm_g[l].astype(F32),
        ones, ones, ones, ones])
    lam = (jnp.exp(jnp.sum(diff_lambda_q1[l].astype(F32) * diff_lambda_k1[l].astype(F32)))
           - jnp.exp(jnp.sum(diff_lambda_q2[l].astype(F32) * diff_lambda_k2[l].astype(F32)))
           + LAM_INIT).reshape(1)
    t_attn = min(ATTN_TILE, seq)
    bias_tiles_t = _diff_bias(rel_bias, t_attn)
    wr = jnp.zeros((32, D_MODEL), F32)
    wr = wr.at[:N_GROUPS].set(w_group_router[l].T).at[N_GROUPS:N_GROUPS + N_EXPERTS].set(w_expert_router[l].T)
    wr_hi = wr.astype(BF16)
    wr_lo = (wr - wr_hi.astype(F32)).astype(BF16)
    rb = jnp.zeros((32, 1), F32)
    rb = rb.at[:N_GROUPS, 0].set(b_group_router[l]).at[N_GROUPS:N_GROUPS + N_EXPERTS, 0].set(b_expert_router[l])

    row = lambda v: v.astype(F32).reshape(1, -1)
    x2d = x.reshape(n_tok, D_MODEL)

    qk, vt, cum = _mix_proj(x2d, row(norm_mix_g[l]), w_qk, w_vt, wft, bf, qg, batch=batch, seq=seq)
    cum4 = cum[:, :N_HEADS].reshape(batch, N_HEADS, 1, seq)
    mixed = _attn(qk, vt, bias_tiles_t, lam, cum4, row(diff_subln_g[l]), row(fox_out_norm_g[l]),
                  batch=batch, seq=seq)
    x1, qc = _out_q(x2d, mixed, w_out[l].astype(BF16), row(norm_cross_g[l]),
                    w_cq[l].astype(BF16), row(cross_q_norm_g[l]))

    kc, vc = _mem_kv(mem.reshape(batch * N_MEM, D_MODEL), row(norm_mem_g[l]), w_ckv[l].astype(BF16),
                     row(cross_k_norm_g[l]), batch=batch)
    xa, rt = _cross_router(qc, kc, vc, x1, w_co[l].astype(BF16), row(norm_ffn_g[l]), wr_hi, wr_lo, rb,
                           batch=batch, seq=seq)

    rank, cnt = _rank(rt)
    counts = cnt[:N_PAIR_BUCKETS, 0].astype(I32)
    n_slots = 2 * ((n_tok // MOE_ROWS + N_PAIR_BUCKETS + 1) // 2)
    row_off, tile_a, tile_b, n_valid, n_active = _tile_plan(counts, n_slots)
    bucket = rt[:, 0, :].reshape(n_tok).astype(I32)
    pos = row_off[bucket] + rank.reshape(n_tok)
    idx = jnp.zeros((n_slots * MOE_ROWS,), I32).at[pos].set(jnp.arange(n_tok, dtype=I32), unique_indices=True)
    out = _moe(tile_a, tile_b, n_valid, n_active.reshape(1), idx, xa, row(norm_ffn_g[l]),
               w_exp_gate[l].astype(BF16), w_exp_up[l].astype(BF16), w_exp_down[l].astype(BF16), n_slots=n_slots)
    return out.reshape(batch, seq, D_MODEL)
```

```python
import functools
import math

import jax
import jax.numpy as jnp
import numpy as np
from jax import lax
from jax.experimental import pallas as pl
from jax.experimental.pallas import tpu as pltpu

F32 = jnp.float32
BF16 = jnp.bfloat16
I32 = jnp.int32

D_MODEL = 1024
CHUNK = 64
N_MEM = 256
N_HEADS = 4
HEAD_DIM = 128
DIFF_QK_DIM = 64
CROSS_HEAD_DIM = 256
N_BUCKETS_T5 = 32
MAX_DISTANCE = 128
N_GROUPS = 4
EXPERTS_PER_GROUP = 4
N_EXPERTS = N_GROUPS * EXPERTS_PER_GROUP
D_EXPERT = 512
EPS = 1e-6
NEG_INF = -1e30
LAM_INIT = 0.8 - 0.6 * math.exp(-0.3 * 0)
LOG2E = math.log2(math.e)

PAIRS = ((0, 1), (0, 2), (0, 3), (1, 2), (1, 3), (2, 3))
N_PAIR_BUCKETS = N_GROUPS * len(PAIRS)

LANES = 128
VMEM_LIMIT_CAP = 56 * 1024 * 1024

PROJ_ROWS = 512
ATTN_TILE = 256
CROSS_ROWS = 512
MOE_ROWS = 256
XA_ROWS = D_MODEL // LANES + 1


def _vmem_limit(nbytes):
    return int(min(max(nbytes * 5 // 4, 32 * 1024 * 1024), VMEM_LIMIT_CAP))


def _nt_dot(a, b):
    return lax.dot_general(a, b, (((1,), (1,)), ((), ())), preferred_element_type=F32)


def _rms(x, g):
    ms = jnp.mean(x * x, axis=-1, keepdims=True)
    return x * lax.rsqrt(ms + EPS) * g


def _mix_proj_kernel(x_ref, g_ref, w_ref, wvt_ref, wft_ref, bf_ref, qg_ref, o_ref, vt_ref, cum_ref, carry_ref,
                     *, tm):
    si = pl.program_id(1)
    h = _rms(x_ref[...], g_ref[...]).astype(BF16)
    lane = lax.broadcasted_iota(I32, (tm, HEAD_DIM), 1)
    lo = lane < DIFF_QK_DIM
    for kind in range(4):
        p = jnp.dot(h, w_ref[:, kind * 512:(kind + 1) * 512], preferred_element_type=F32)
        for hh in range(N_HEADS):
            ph = p[:, hh * HEAD_DIM:(hh + 1) * HEAD_DIM]
            if kind in (0, 1):
                sq = ph * ph
                s_lo = jnp.sum(jnp.where(lo, sq, 0.0), axis=-1, keepdims=True)
                s_hi = jnp.sum(jnp.where(lo, 0.0, sq), axis=-1, keepdims=True)
                ms = jnp.where(lo, s_lo, s_hi) * (1.0 / DIFF_QK_DIM)
                ph = ph * lax.rsqrt(ms + EPS) * qg_ref[kind:kind + 1, :]
            else:
                ph = _rms(ph, qg_ref[kind:kind + 1, :])
            o_ref[0, kind * N_HEADS + hh] = ph.astype(BF16)
    for kind in range(2):
        pt = _nt_dot(wvt_ref[kind * 512:(kind + 1) * 512, :], h)
        for hh in range(N_HEADS):
            vt_ref[0, kind * N_HEADS + hh] = pt[hh * HEAD_DIM:(hh + 1) * HEAD_DIM, :].astype(BF16)

    z = _nt_dot(wft_ref[...], h) + bf_ref[...]
    logf = jnp.minimum(z, 0.0) - jnp.log(1.0 + jnp.exp(-jnp.abs(z)))
    lane8 = lax.broadcasted_iota(I32, (8, tm), 1)
    c = logf * LOG2E
    k = 1
    while k < tm:
        c = c + jnp.where(lane8 >= k, pltpu.roll(c, k, axis=1), 0.0)
        k *= 2

    @pl.when(si == 0)
    def _():
        carry_ref[...] = jnp.zeros_like(carry_ref)

    c = c + carry_ref[:, 0:1]
    cum_ref[0] = c
    carry_ref[...] = jnp.broadcast_to(c[:, tm - 1:tm], carry_ref.shape)


def _mix_proj(x2d, g, w_qk, w_vt, wft, bf, qg, *, batch, seq):
    tm = min(PROJ_ROWS, seq)
    ns = seq // tm
    est = 2 * (tm * D_MODEL * 4 + D_MODEL * 3072 * 2 + 24 * tm * HEAD_DIM * 2) + 8 * tm * D_MODEL
    return pl.pallas_call(
        functools.partial(_mix_proj_kernel, tm=tm),
        grid=(batch, ns),
        in_specs=[
            pl.BlockSpec((tm, D_MODEL), lambda b, s: (b * ns + s, 0)),
            pl.BlockSpec((1, D_MODEL), lambda b, s: (0, 0)),
            pl.BlockSpec((D_MODEL, 2048), lambda b, s: (0, 0)),
            pl.BlockSpec((1024, D_MODEL), lambda b, s: (0, 0)),
            pl.BlockSpec((8, D_MODEL), lambda b, s: (0, 0)),
            pl.BlockSpec((8, 1), lambda b, s: (0, 0)),
            pl.BlockSpec((8, HEAD_DIM), lambda b, s: (0, 0)),
        ],
        out_specs=[
            pl.BlockSpec((1, 16, tm, HEAD_DIM), lambda b, s: (b, 0, s, 0)),
            pl.BlockSpec((1, 8, HEAD_DIM, tm), lambda b, s: (b, 0, 0, s)),
            pl.BlockSpec((1, 8, tm), lambda b, s: (b, 0, s)),
        ],
        out_shape=[
            jax.ShapeDtypeStruct((batch, 16, seq, HEAD_DIM), BF16),
            jax.ShapeDtypeStruct((batch, 8, HEAD_DIM, seq), BF16),
            jax.ShapeDtypeStruct((batch, 8, seq), F32),
        ],
        scratch_shapes=[pltpu.VMEM((8, LANES), F32)],
        compiler_params=pltpu.CompilerParams(
            dimension_semantics=("arbitrary", "arbitrary"), vmem_limit_bytes=_vmem_limit(est)),
        name="mix_proj",
    )(x2d, g, w_qk, w_vt, wft, bf, qg)


def _init_softmax(m_sc, l_sc, acc_sc):
    m_sc[...] = jnp.full(m_sc.shape, NEG_INF, F32)
    l_sc[...] = jnp.zeros_like(l_sc)
    acc_sc[...] = jnp.zeros_like(acc_sc)


N_CHAINS = 3 * N_HEADS


def _attn_kernel(lam_ref, dq_ref, dk_ref, dvt_ref, fq_ref, fk_ref, fvt_ref, bias_ref, c_ref,
                 gd_ref, gf_ref, o_ref, m_sc, l_sc, acc_sc, qm_sc, ccol_sc, s_sc, mt_sc, *, t, seq):
    qi = pl.program_id(1)
    _init_softmax(m_sc, l_sc, acc_sc)
    key = lax.broadcasted_iota(I32, (t, t), 0)
    qry = lax.broadcasted_iota(I32, (t, t), 1)

    lane = lax.broadcasted_iota(I32, (t, HEAD_DIM), 1)
    for hh in range(N_HEADS):
        q = dq_ref[0, hh]
        zero = jnp.zeros_like(q)
        qm_sc[2 * hh] = jnp.where(lane < DIFF_QK_DIM, q, zero)
        qm_sc[2 * hh + 1] = jnp.where(lane < DIFF_QK_DIM, zero, q)

    @pl.when(qi == 0)
    def _():
        for hh in range(N_HEADS):
            for j in range(seq // t):
                row = c_ref[0, hh, :, j * t:(j + 1) * t]
                ccol_sc[hh, j * t:(j + 1) * t, :] = jnp.sum(
                    jnp.where(key == qry, jnp.broadcast_to(row, (t, t)), 0.0), axis=1, keepdims=True)

    q_off = pl.multiple_of(qi * t, t)

    def tile(j, where):
        off = pl.multiple_of(j * t, t)

        for hh in range(N_HEADS):
            k = dk_ref[0, hh, pl.ds(off, t), :]
            for c in range(2):
                s_t = _nt_dot(k, qm_sc[2 * hh + c])
                if where != "far":
                    s_t = s_t + bias_ref[hh, 1 if where == "near" else 0]
                s_sc[2 * hh + c] = s_t
                mt_sc[2 * hh + c] = jnp.max(s_t, axis=0, keepdims=True)
        for hh in range(N_HEADS):
            k = fk_ref[0, hh, pl.ds(off, t), :]
            c_k = ccol_sc[hh, pl.ds(off, t), :]
            s_t = _nt_dot(k, fq_ref[0, hh]) - c_k
            if where == "diag":
                s_t = jnp.where(key <= qry, s_t, NEG_INF)
            s_sc[2 * N_HEADS + hh] = s_t
            mt_sc[2 * N_HEADS + hh] = jnp.max(s_t, axis=0, keepdims=True) + c_ref[0, hh, :, pl.ds(q_off, t)]

        for slot in range(N_CHAINS):
            vt_ref = dvt_ref if slot < 2 * N_HEADS else fvt_ref
            hh = slot // 2 if slot < 2 * N_HEADS else slot - 2 * N_HEADS
            v_t = vt_ref[0, hh, :, pl.ds(off, t)]
            m_prev = m_sc[slot]
            m_new = jnp.maximum(m_prev, mt_sc[slot])
            alpha = jnp.exp2(m_prev - m_new)
            if slot < 2 * N_HEADS:
                p_t = jnp.exp2(s_sc[slot] - m_new)
            else:
                p_t = jnp.exp2(s_sc[slot] - (m_new - c_ref[0, hh, :, pl.ds(q_off, t)]))
            l_sc[slot] = alpha * l_sc[slot] + jnp.sum(p_t, axis=0, keepdims=True)
            acc_sc[slot] = alpha * acc_sc[slot] + jnp.dot(v_t, p_t.astype(BF16), preferred_element_type=F32)
            m_sc[slot] = m_new

    def far_body(j, carry):
        tile(j, "far")
        return carry

    lax.fori_loop(0, jnp.maximum(qi - 1, 0), far_body, 0)

    @pl.when(qi >= 1)
    def _():
        tile(qi - 1, "near")

    tile(qi, "diag")

    for hh in range(N_HEADS):
        a, b = 2 * hh, 2 * hh + 1
        o_t = acc_sc[a] / l_sc[a] - lam_ref[0] * (acc_sc[b] / l_sc[b])
        o_ref[:, hh * HEAD_DIM:(hh + 1) * HEAD_DIM] = (
            _rms(o_t.T, gd_ref[...]) * (1.0 - LAM_INIT)).astype(BF16)
    for hh in range(N_HEADS):
        s = 2 * N_HEADS + hh
        o_t = acc_sc[s] / l_sc[s]
        o_ref[:, (N_HEADS + hh) * HEAD_DIM:(N_HEADS + hh + 1) * HEAD_DIM] = _rms(o_t.T, gf_ref[...]).astype(BF16)


def _attn(qk, vt, bias_tiles_t, lam, cum4, subln_g, fox_g, *, batch, seq):
    t = min(ATTN_TILE, seq)
    nq = seq // t
    head_blk = N_HEADS * seq * HEAD_DIM * 2
    est = (2 * (2 * N_HEADS * t * HEAD_DIM * 2 + 4 * head_blk + N_HEADS * 2 * t * t * 4 + N_HEADS * 8 * seq * 4
                + t * 2 * N_HEADS * HEAD_DIM * 2)
           + N_CHAINS * (HEAD_DIM + 16) * t * 4 + 2 * N_HEADS * t * HEAD_DIM * 2 + N_HEADS * seq * LANES * 4
           + 48 * t * t * 4)
    qspec = lambda blk: pl.BlockSpec((1, N_HEADS, t, HEAD_DIM), lambda b, i: (b, blk, i, 0))
    kspec = lambda blk: pl.BlockSpec((1, N_HEADS, seq, HEAD_DIM), lambda b, i: (b, blk, 0, 0))
    vspec = lambda blk: pl.BlockSpec((1, N_HEADS, HEAD_DIM, seq), lambda b, i: (b, blk, 0, 0))
    return pl.pallas_call(
        functools.partial(_attn_kernel, t=t, seq=seq),
        grid=(batch, nq),
        in_specs=[
            pl.BlockSpec(memory_space=pltpu.SMEM),
            qspec(0), kspec(1), vspec(0),
            qspec(2), kspec(3), vspec(1),
            pl.BlockSpec((N_HEADS, 2, t, t), lambda b, i: (0, 0, 0, 0)),
            pl.BlockSpec((1, N_HEADS, 1, seq), lambda b, i: (b, 0, 0, 0)),
            pl.BlockSpec((1, HEAD_DIM), lambda b, i: (0, 0)),
            pl.BlockSpec((1, HEAD_DIM), lambda b, i: (0, 0)),
        ],
        out_specs=pl.BlockSpec((t, 2 * N_HEADS * HEAD_DIM), lambda b, i: (b * nq + i, 0)),
        out_shape=jax.ShapeDtypeStruct((batch * seq, 2 * N_HEADS * HEAD_DIM), BF16),
        scratch_shapes=[pltpu.VMEM((N_CHAINS, 1, t), F32), pltpu.VMEM((N_CHAINS, 1, t), F32),
                        pltpu.VMEM((N_CHAINS, HEAD_DIM, t), F32),
                        pltpu.VMEM((2 * N_HEADS, t, HEAD_DIM), BF16),
                        pltpu.VMEM((N_HEADS, seq, 1), F32),
                        pltpu.VMEM((N_CHAINS, t, t), F32), pltpu.VMEM((N_CHAINS, 1, t), F32)],
        compiler_params=pltpu.CompilerParams(
            dimension_semantics=("arbitrary", "arbitrary"), vmem_limit_bytes=_vmem_limit(est)),
        name="attn",
    )(lam, qk, qk, vt, qk, qk, vt, bias_tiles_t, cum4, subln_g, fox_g)


def _out_q_kernel(x_ref, m_ref, wo_ref, gc_ref, wq_ref, qg_ref, x1_ref, qc_ref):
    x1 = x_ref[...] + jnp.dot(m_ref[...], wo_ref[...], preferred_element_type=F32)
    x1_ref[...] = x1
    hc = _rms(x1, gc_ref[...]).astype(BF16)
    q = jnp.dot(hc, wq_ref[...], preferred_element_type=F32)
    for hh in range(N_HEADS):
        sl = slice(hh * CROSS_HEAD_DIM, (hh + 1) * CROSS_HEAD_DIM)
        qh = _rms(q[:, sl], qg_ref[...]) * (CROSS_HEAD_DIM ** -0.5)
        qc_ref[:, sl] = qh.astype(BF16)


def _out_q(x2d, mixed, w_out, g_cross, w_cq, q_g):
    n = x2d.shape[0]
    tm = min(PROJ_ROWS, n)
    est = 2 * (tm * D_MODEL * 4 * 2 + tm * D_MODEL * 2 + 2 * D_MODEL * D_MODEL * 2 + tm * D_MODEL * 2) + 6 * tm * D_MODEL * 4
    return pl.pallas_call(
        _out_q_kernel,
        grid=(n // tm,),
        in_specs=[
            pl.BlockSpec((tm, D_MODEL), lambda i: (i, 0)),
            pl.BlockSpec((tm, D_MODEL), lambda i: (i, 0)),
            pl.BlockSpec((D_MODEL, D_MODEL), lambda i: (0, 0)),
            pl.BlockSpec((1, D_MODEL), lambda i: (0, 0)),
            pl.BlockSpec((D_MODEL, D_MODEL), lambda i: (0, 0)),
            pl.BlockSpec((1, CROSS_HEAD_DIM), lambda i: (0, 0)),
        ],
        out_specs=[pl.BlockSpec((tm, D_MODEL), lambda i: (i, 0)),
                   pl.BlockSpec((tm, D_MODEL), lambda i: (i, 0))],
        out_shape=[jax.ShapeDtypeStruct((n, D_MODEL), F32), jax.ShapeDtypeStruct((n, D_MODEL), BF16)],
        compiler_params=pltpu.CompilerParams(
            dimension_semantics=("arbitrary",), vmem_limit_bytes=_vmem_limit(est)),
        name="out_q",
    )(x2d, mixed, w_out, g_cross, w_cq, q_g)


def _mem_kv_kernel(mem_ref, gm_ref, w_ref, kg_ref, k_ref, v_ref):
    mn = _rms(mem_ref[...], gm_ref[...]).astype(BF16)
    kv = jnp.dot(mn, w_ref[...], preferred_element_type=F32)
    for hh in range(N_HEADS):
        sl = slice(hh * CROSS_HEAD_DIM, (hh + 1) * CROSS_HEAD_DIM)
        k_ref[0, :, sl] = _rms(kv[:, sl], kg_ref[...]).astype(BF16)
    v_ref[0] = kv[:, D_MODEL:].astype(BF16)


def _mem_kv(mem2d, g_mem, w_ckv, k_g, *, batch):
    est = 2 * (N_MEM * D_MODEL * 4 + D_MODEL * 2 * D_MODEL * 2 + 2 * N_MEM * D_MODEL * 2) + 4 * N_MEM * 2 * D_MODEL * 4
    return pl.pallas_call(
        _mem_kv_kernel,
        grid=(batch,),
        in_specs=[
            pl.BlockSpec((N_MEM, D_MODEL), lambda b: (b, 0)),
            pl.BlockSpec((1, D_MODEL), lambda b: (0, 0)),
            pl.BlockSpec((D_MODEL, 2 * D_MODEL), lambda b: (0, 0)),
            pl.BlockSpec((1, CROSS_HEAD_DIM), lambda b: (0, 0)),
        ],
        out_specs=[pl.BlockSpec((1, N_MEM, D_MODEL), lambda b: (b, 0, 0)),
                   pl.BlockSpec((1, N_MEM, D_MODEL), lambda b: (b, 0, 0))],
        out_shape=[jax.ShapeDtypeStruct((batch, N_MEM, D_MODEL), BF16),
                   jax.ShapeDtypeStruct((batch, N_MEM, D_MODEL), BF16)],
        compiler_params=pltpu.CompilerParams(
            dimension_semantics=("arbitrary",), vmem_limit_bytes=_vmem_limit(est)),
        name="mem_kv",
    )(mem2d, g_mem, w_ckv, k_g)


def _route(logits_t):
    gl = logits_t[0:N_GROUPS]
    gmax = jnp.max(gl, axis=0, keepdims=True)
    eg = jnp.exp(gl - gmax)
    p_group = eg / jnp.sum(eg, axis=0, keepdims=True)
    p_g = jnp.max(p_group, axis=0, keepdims=True)
    g_idx = jnp.full_like(p_g, float(N_GROUPS))
    for g in reversed(range(N_GROUPS)):
        g_idx = jnp.where(p_group[g:g + 1] == p_g, float(g), g_idx)

    sel = []
    for j in range(EXPERTS_PER_GROUP):
        acc = jnp.zeros_like(p_g)
        for g in range(N_GROUPS):
            r = N_GROUPS + g * EXPERTS_PER_GROUP + j
            acc = jnp.where(g_idx == float(g), logits_t[r:r + 1], acc)
        sel.append(acc)
    smax = functools.reduce(jnp.maximum, sel)
    es = [jnp.exp(s - smax) for s in sel]
    den = functools.reduce(jnp.add, es)
    p_in = [e / den for e in es]

    v0 = functools.reduce(jnp.maximum, p_in)
    i0 = jnp.full_like(v0, float(EXPERTS_PER_GROUP))
    for j in reversed(range(EXPERTS_PER_GROUP)):
        i0 = jnp.where(p_in[j] == v0, float(j), i0)
    rest = [jnp.where(i0 == float(j), -1.0, p_in[j]) for j in range(EXPERTS_PER_GROUP)]
    v1 = functools.reduce(jnp.maximum, rest)
    i1 = jnp.full_like(v1, float(EXPERTS_PER_GROUP))
    for j in reversed(range(EXPERTS_PER_GROUP)):
        i1 = jnp.where(rest[j] == v1, float(j), i1)

    tot = v0 + v1
    w0 = p_g * (v0 / tot)
    w1 = p_g * (v1 / tot)
    first = i0 < i1
    a = jnp.where(first, i0, i1)
    b = jnp.where(first, i1, i0)
    wa = jnp.where(first, w0, w1)
    wb = jnp.where(first, w1, w0)
    pair = jnp.where(a == 0.0, 0.0, jnp.where(a == 1.0, 3.0, 5.0)) + (b - a - 1.0)
    bucket = g_idx * float(len(PAIRS)) + pair
    return bucket, wa, wb


def _cross_router_kernel(q_ref, k_ref, v_ref, x1_ref, wco_ref, gf_ref, whi_ref, wlo_ref, rb_ref,
                         xa_ref, rt_ref, *, tm):
    q = q_ref[...]
    k = k_ref[0]
    v = v_ref[0]
    outs = []
    for hh in range(N_HEADS):
        sl = slice(hh * CROSS_HEAD_DIM, (hh + 1) * CROSS_HEAD_DIM)
        s = _nt_dot(q[:, sl], k[:, sl])
        p = jnp.exp(s - jnp.max(s, axis=-1, keepdims=True))
        l = jnp.sum(p, axis=-1, keepdims=True)
        o = jnp.dot(p.astype(BF16), v[:, sl], preferred_element_type=F32) / l
        outs.append(o.astype(BF16))
    o = jnp.concatenate(outs, axis=1)
    x2 = x1_ref[...] + jnp.dot(o, wco_ref[...], preferred_element_type=F32)
    for j in range(D_MODEL // LANES):
        xa_ref[pl.ds(j, tm, stride=XA_ROWS), :] = x2[:, j * LANES:(j + 1) * LANES]

    h3 = _rms(x2, gf_ref[...])
    hi = h3.astype(BF16)
    lo = (h3 - hi.astype(F32)).astype(BF16)
    lt = _nt_dot(whi_ref[...], hi) + (_nt_dot(whi_ref[...], lo) + _nt_dot(wlo_ref[...], hi))
    lt = lt + rb_ref[...]
    bucket, wa, wb = _route(lt)

    sub = lax.broadcasted_iota(I32, (8, tm), 0)
    rows = jnp.where(sub == 0, bucket, jnp.where(sub == 1, wa, jnp.where(sub == 2, wb, 0.0)))
    rt_ref[0] = rows
    sub_a = lax.broadcasted_iota(I32, (LANES, tm), 0)
    aux_t = jnp.where(sub_a == 0, bucket, jnp.where(sub_a == 1, wa, jnp.where(sub_a == 2, wb, 0.0)))
    xa_ref[pl.ds(XA_ROWS - 1, tm, stride=XA_ROWS), :] = aux_t.T


def _cross_router(qc, kc, vc, x1, w_co, g_ffn, wr_hi, wr_lo, rb, *, batch, seq):
    tm = min(CROSS_ROWS, seq)
    ns = seq // tm
    n = batch * seq
    est = 2 * (tm * D_MODEL * 2 + 2 * N_MEM * D_MODEL * 2 + tm * D_MODEL * 4 + D_MODEL * D_MODEL * 2
               + tm * XA_ROWS * LANES * 4) + 10 * tm * D_MODEL * 4
    return pl.pallas_call(
        functools.partial(_cross_router_kernel, tm=tm),
        grid=(batch, ns),
        in_specs=[
            pl.BlockSpec((tm, D_MODEL), lambda b, s: (b * ns + s, 0)),
            pl.BlockSpec((1, N_MEM, D_MODEL), lambda b, s: (b, 0, 0)),
            pl.BlockSpec((1, N_MEM, D_MODEL), lambda b, s: (b, 0, 0)),
            pl.BlockSpec((tm, D_MODEL), lambda b, s: (b * ns + s, 0)),
            pl.BlockSpec((D_MODEL, D_MODEL), lambda b, s: (0, 0)),
            pl.BlockSpec((1, D_MODEL), lambda b, s: (0, 0)),
            pl.BlockSpec((32, D_MODEL), lambda b, s: (0, 0)),
            pl.BlockSpec((32, D_MODEL), lambda b, s: (0, 0)),
            pl.BlockSpec((32, 1), lambda b, s: (0, 0)),
        ],
        out_specs=[pl.BlockSpec((tm * XA_ROWS, LANES), lambda b, s: (b * ns + s, 0)),
                   pl.BlockSpec((1, 8, tm), lambda b, s: (b * ns + s, 0, 0))],
        out_shape=[jax.ShapeDtypeStruct((n * XA_ROWS, LANES), F32),
                   jax.ShapeDtypeStruct((n // tm, 8, tm), F32)],
        compiler_params=pltpu.CompilerParams(
            dimension_semantics=("arbitrary", "arbitrary"), vmem_limit_bytes=_vmem_limit(est)),
        name="cross_router",
    )(qc, kc, vc, x1, w_co, g_ffn, wr_hi, wr_lo, rb)


def _rank_kernel(rt_ref, rank_ref, cnt_ref, carry_sc, *, tm):
    @pl.when(pl.program_id(0) == 0)
    def _():
        carry_sc[...] = jnp.zeros_like(carry_sc)

    bucket = rt_ref[0, 0:1, :]
    sub = lax.broadcasted_iota(I32, (32, tm), 0).astype(F32)
    hit = sub == bucket
    r = lax.broadcasted_iota(I32, (tm, tm), 0)
    c = lax.broadcasted_iota(I32, (tm, tm), 1)
    upper = jnp.where(r <= c, 1.0, 0.0).astype(BF16)
    cum = jnp.dot(jnp.where(hit, 1.0, 0.0).astype(BF16), upper, preferred_element_type=F32)
    carry = carry_sc[:, 0:1]
    rank = jnp.sum(jnp.where(hit, cum - 1.0 + carry, 0.0), axis=0, keepdims=True)
    rank_ref[0] = rank.astype(I32)
    new_carry = carry_sc[...] + jnp.broadcast_to(cum[:, tm - 1:tm], carry_sc.shape)
    carry_sc[...] = new_carry
    cnt_ref[...] = new_carry


def _rank(rt):
    nt, _, tm = rt.shape
    return pl.pallas_call(
        functools.partial(_rank_kernel, tm=tm),
        grid=(nt,),
        in_specs=[pl.BlockSpec((1, 8, tm), lambda i: (i, 0, 0))],
        out_specs=[pl.BlockSpec((1, 1, tm), lambda i: (i, 0, 0)),
                   pl.BlockSpec((32, LANES), lambda i: (0, 0))],
        out_shape=[jax.ShapeDtypeStruct((nt, 1, tm), I32), jax.ShapeDtypeStruct((32, LANES), F32)],
        scratch_shapes=[pltpu.VMEM((32, LANES), F32)],
        compiler_params=pltpu.CompilerParams(dimension_semantics=("arbitrary",)),
        name="rank",
    )(rt)


def _invert_kernel(pos_ref, idx_ref, *, n_tok, n_sorted):
    def clear(r, carry):
        idx_ref[r] = 0
        return carry

    def put(t, carry):
        idx_ref[pos_ref[t]] = t
        return carry

    lax.fori_loop(0, n_sorted, clear, 0, unroll=8)
    lax.fori_loop(0, n_tok, put, 0, unroll=8)


def _invert(pos, *, n_sorted):
    n_tok = pos.shape[0]
    return pl.pallas_call(
        functools.partial(_invert_kernel, n_tok=n_tok, n_sorted=n_sorted),
        grid_spec=pltpu.PrefetchScalarGridSpec(
            num_scalar_prefetch=1, grid=(1,), in_specs=[],
            out_specs=pl.BlockSpec(memory_space=pltpu.SMEM)),
        out_shape=jax.ShapeDtypeStruct((n_sorted,), I32),
        name="invert",
    )(pos)


def _moe_kernel(ta_ref, tb_ref, nv_ref, nact_ref, idx_ref, xa_hbm, gf_ref, *refs):
    t_rows = MOE_ROWS
    weights = (refs[0:6], refs[6:12])
    out_hbm, xbuf0, xbuf1, ybuf0, ybuf1, gsem, ssem = refs[12:]
    xbuf = (xbuf0, xbuf1)
    ybuf = (ybuf0, ybuf1)
    n_act = nact_ref[0]

    def gather_row(t, r, par):
        return pltpu.make_async_copy(xa_hbm.at[pl.ds(idx_ref[t * t_rows + r] * XA_ROWS, XA_ROWS)],
                                     xbuf[par].at[pl.ds(r * XA_ROWS, XA_ROWS)], gsem.at[par])

    def gather_all(par):
        return pltpu.make_async_copy(xa_hbm.at[pl.ds(0, t_rows * XA_ROWS)], xbuf[par], gsem.at[par])

    def scatter_row(t, r, par):
        return pltpu.make_async_copy(ybuf[par].at[pl.ds(r, 1)],
                                     out_hbm.at[pl.ds(idx_ref[t * t_rows + r], 1)], ssem.at[par])

    def scatter_all(par):
        return pltpu.make_async_copy(ybuf[par], out_hbm.at[pl.ds(0, t_rows)], ssem.at[par])

    def scatter_some(t, par, wait):
        def body(r, carry):
            cp = scatter_row(t, r, par)
            if wait:
                cp.wait()
            else:
                cp.start()
            return carry
        lax.fori_loop(0, nv_ref[t], body, 0)

    def compute(par):
        x2 = jnp.concatenate([xbuf[par][pl.ds(j, t_rows, stride=XA_ROWS), :]
                              for j in range(D_MODEL // LANES)], axis=1)
        aux = xbuf[par][pl.ds(XA_ROWS - 1, t_rows, stride=XA_ROWS), :]
        wa = aux[:, 1:2]
        wb = aux[:, 2:3]
        h = _rms(x2, gf_ref[...]).astype(BF16)
        wga, wua, wda, wgb, wub, wdb = weights[par]

        def mlp(wg, wu, wd):
            g = jnp.dot(h, wg[0], preferred_element_type=F32)
            u = jnp.dot(h, wu[0], preferred_element_type=F32)
            act = (g / (1.0 + jnp.exp(-g))) * u
            return jnp.dot(act.astype(BF16), wd[0], preferred_element_type=F32)

        ybuf[par][...] = x2 + (wa * mlp(wga, wua, wda) + wb * mlp(wgb, wub, wdb))

    @pl.when(pl.program_id(0) == 0)
    def _():
        for r in range(t_rows):
            gather_row(0, r, 0).start(priority=r % 2)

    for par in range(2):
        t = 2 * pl.program_id(0) + par
        prev = jnp.maximum(t - 1, 0)
        prev_full = jnp.logical_and(t >= 1, nv_ref[prev] == t_rows)

        def front(t=t, par=par):
            gather_all(par).wait()
            nxt = jnp.minimum(t + 1, n_act - 1)
            for r in range(t_rows):
                gather_row(nxt, r, 1 - par).start(priority=r % 2)

        @pl.when(jnp.logical_and(t < n_act, prev_full))
        def _(t=t, par=par, prev=prev, front=front):
            front()
            for r in range(t_rows):
                scatter_row(prev, r, 1 - par).start(priority=r % 2)
            compute(par)
            scatter_all(1 - par).wait()

        @pl.when(jnp.logical_and(t < n_act, jnp.logical_and(t >= 1, jnp.logical_not(prev_full))))
        def _(t=t, par=par, prev=prev, front=front):
            front()
            scatter_some(prev, 1 - par, wait=False)
            compute(par)
            scatter_some(prev, 1 - par, wait=True)

        if par == 0:
            @pl.when(t == 0)
            def _(par=par, front=front):
                front()
                compute(par)

        @pl.when(t == n_act)
        def _(par=par, prev=prev):
            gather_all(par).wait()
            scatter_some(prev, 1 - par, wait=False)
            scatter_some(prev, 1 - par, wait=True)


def _moe(tile_a, tile_b, n_valid, n_active, idx, xa, g_ffn, w_gate, w_up, w_down, *, n_slots):
    n_tok = xa.shape[0] // XA_ROWS
    t = MOE_ROWS

    def wspec(shape, experts, par):
        if experts == "a":
            return pl.BlockSpec(shape, lambda g, ta, tb, nv, na, ix: (ta[2 * g + par], 0, 0))
        return pl.BlockSpec(shape, lambda g, ta, tb, nv, na, ix: (tb[2 * g + par], 0, 0))

    up_shape, down_shape = (1, D_MODEL, D_EXPERT), (1, D_EXPERT, D_MODEL)
    w_specs, w_args = [], []
    for par in range(2):
        for experts in ("a", "b"):
            w_specs += [wspec(up_shape, experts, par), wspec(up_shape, experts, par), wspec(down_shape, experts, par)]
            w_args += [w_gate, w_up, w_down]
    est = (2 * 12 * D_MODEL * D_EXPERT * 2 + 2 * t * XA_ROWS * LANES * 4 + 2 * t * D_MODEL * 4
           + 10 * t * D_MODEL * 4)
    return pl.pallas_call(
        _moe_kernel,
        grid_spec=pltpu.PrefetchScalarGridSpec(
            num_scalar_prefetch=5,
            grid=(n_slots // 2,),
            in_specs=[pl.BlockSpec(memory_space=pl.ANY),
                      pl.BlockSpec((1, D_MODEL), lambda g, *_: (0, 0))] + w_specs,
            out_specs=pl.BlockSpec(memory_space=pl.ANY),
            scratch_shapes=[pltpu.VMEM((t * XA_ROWS, LANES), F32), pltpu.VMEM((t * XA_ROWS, LANES), F32),
                            pltpu.VMEM((t, D_MODEL), F32), pltpu.VMEM((t, D_MODEL), F32),
                            pltpu.SemaphoreType.DMA((2,)), pltpu.SemaphoreType.DMA((2,))],
        ),
        out_shape=jax.ShapeDtypeStruct((n_tok, D_MODEL), F32),
        compiler_params=pltpu.CompilerParams(
            dimension_semantics=("arbitrary",), vmem_limit_bytes=_vmem_limit(est), has_side_effects=True),
        name="moe",
    )(tile_a, tile_b, n_valid, n_active, idx, xa, g_ffn, *w_args)


def _t5_bucket(rel):
    nb = N_BUCKETS_T5 // 2
    max_exact = nb // 2
    ret = (rel > 0).astype(I32) * nb
    n = jnp.abs(rel)
    nf = jnp.maximum(n, 1).astype(F32)
    large = max_exact + (jnp.log(nf / max_exact) / math.log(MAX_DISTANCE / max_exact)
                         * (nb - max_exact)).astype(I32)
    large = jnp.minimum(large, nb - 1)
    return ret + jnp.where(n < max_exact, n, large)


def _diff_bias(rel_bias, t):
    assert t % CHUNK == 0 and t >= MAX_DISTANCE
    n_rel = 3 * t
    vec = rel_bias[_t5_bucket(jnp.arange(n_rel, dtype=I32) - (2 * t - 1))].astype(F32).T
    toe = jnp.tile(vec, (1, t))[:, :t * (n_rel - 1)].reshape(N_HEADS, t, n_rel - 1)
    qpos = jnp.arange(t, dtype=I32)[:, None]
    kpos = jnp.arange(t, dtype=I32)[None, :]
    far = rel_bias[_t5_bucket(jnp.asarray(-2 * t, I32))].astype(F32)[:, None, None]
    b0 = (toe[:, :, 2 * t - 1:3 * t - 1] - far) * LOG2E
    b0 = jnp.where((kpos // CHUNK <= qpos // CHUNK)[None], b0, NEG_INF)
    b1 = (toe[:, :, t - 1:2 * t - 1] - far) * LOG2E
    return jnp.swapaxes(jnp.stack([b0, b1], axis=1), -1, -2)


def _tile_plan(counts, n_slots):
    t = MOE_ROWS
    tiles = (counts + t - 1) // t
    ends = jnp.cumsum(tiles)
    starts = ends - tiles
    n_active = ends[-1]
    slot = jnp.arange(n_slots, dtype=I32)
    tile_idx = jnp.minimum(slot, n_active - 1)
    tile_bucket = jnp.minimum(jnp.sum((ends[None, :] <= tile_idx[:, None]).astype(I32), axis=1), N_PAIR_BUCKETS - 1)
    pa = jnp.asarray([p[0] for p in PAIRS], I32)
    pb = jnp.asarray([p[1] for p in PAIRS], I32)
    grp = tile_bucket // len(PAIRS)
    pair = tile_bucket % len(PAIRS)
    tile_a = grp * EXPERTS_PER_GROUP + pa[pair]
    tile_b = grp * EXPERTS_PER_GROUP + pb[pair]
    row_off = (starts * t).astype(I32)
    in_bucket = tile_idx - starts[tile_bucket]
    n_valid = jnp.clip(counts[tile_bucket] - in_bucket * t, 0, t)
    n_valid = jnp.where(slot < n_active, n_valid, 0)
    return row_off, tile_a.astype(I32), tile_b.astype(I32), n_valid.astype(I32), n_active.astype(I32)


def kernel(x, mem, rel_bias, norm_mix_g, w_in, b_forget, diff_q_norm_g, diff_k_norm_g, diff_lambda_q1, diff_lambda_k1, diff_lambda_q2, diff_lambda_k2, diff_subln_g, fox_q_norm_g, fox_k_norm_g, fox_out_norm_g, w_out, norm_cross_g, norm_mem_g, w_cq, w_ckv, cross_q_norm_g, cross_k_norm_g, w_co, norm_ffn_g, w_group_router, b_group_router, w_expert_router, b_expert_router, w_exp_gate, w_exp_up, w_exp_down):
    batch, seq, d = x.shape
    assert d == D_MODEL and norm_mix_g.shape[0] == 1 and mem.shape[1] == N_MEM
    n_tok = batch * seq
    assert seq % ATTN_TILE == 0 or seq < ATTN_TILE
    l = 0

    w_qk = jnp.concatenate([w_in[l, :, :1024], w_in[l, :, 1536:2560]], axis=1).astype(BF16)
    w_vt = jnp.concatenate([w_in[l, :, 1024:1536], w_in[l, :, 2560:3072]], axis=1).T.astype(BF16)
    wft = jnp.zeros((8, D_MODEL), F32).at[:N_HEADS].set(w_in[l, :, 3072:].T).astype(BF16)
    bf = jnp.zeros((8, 1), F32).at[:N_HEADS, 0].set(b_forget[l].astype(F32))
    ones = jnp.ones((HEAD_DIM,), F32)
    qg = jnp.stack([
        jnp.tile(diff_q_norm_g[l].astype(F32), 2) * (DIFF_QK_DIM ** -0.5 * LOG2E),
        jnp.tile(diff_k_norm_g[l].astype(F32), 2),
        fox_q_norm_g[l].astype(F32) * (HEAD_DIM ** -0.5 * LOG2E), fox_k_norm_g[l].astype(F32),
        ones, ones, ones, ones])
    lam = (jnp.exp(jnp.sum(diff_lambda_q1[l].astype(F32) * diff_lambda_k1[l].astype(F32)))
           - jnp.exp(jnp.sum(diff_lambda_q2[l].astype(F32) * diff_lambda_k2[l].astype(F32)))
           + LAM_INIT).reshape(1)
    t_attn = min(ATTN_TILE, seq)
    bias_tiles_t = _diff_bias(rel_bias, t_attn)
    wr = jnp.zeros((32, D_MODEL), F32)
    wr = wr.at[:N_GROUPS].set(w_group_router[l].T).at[N_GROUPS:N_GROUPS + N_EXPERTS].set(w_expert_router[l].T)
    wr_hi = wr.astype(BF16)
    wr_lo = (wr - wr_hi.astype(F32)).astype(BF16)
    rb = jnp.zeros((32, 1), F32)
    rb = rb.at[:N_GROUPS, 0].set(b_group_router[l]).at[N_GROUPS:N_GROUPS + N_EXPERTS, 0].set(b_expert_router[l])

    row = lambda v: v.astype(F32).reshape(1, -1)
    x2d = x.reshape(n_tok, D_MODEL)

    qk, vt, cum = _mix_proj(x2d, row(norm_mix_g[l]), w_qk, w_vt, wft, bf, qg, batch=batch, seq=seq)
    cum4 = cum[:, :N_HEADS].reshape(batch, N_HEADS, 1, seq)
    mixed = _attn(qk, vt, bias_tiles_t, lam, cum4, row(diff_subln_g[l]), row(fox_out_norm_g[l]),
                  batch=batch, seq=seq)
    x1, qc = _out_q(x2d, mixed, w_out[l].astype(BF16), row(norm_cross_g[l]),
                    w_cq[l].astype(BF16), row(cross_q_norm_g[l]))

    kc, vc = _mem_kv(mem.reshape(batch * N_MEM, D_MODEL), row(norm_mem_g[l]), w_ckv[l].astype(BF16),
                     row(cross_k_norm_g[l]), batch=batch)
    xa, rt = _cross_router(qc, kc, vc, x1, w_co[l].astype(BF16), row(norm_ffn_g[l]), wr_hi, wr_lo, rb,
                           batch=batch, seq=seq)

    rank, cnt = _rank(rt)
    counts = cnt[:N_PAIR_BUCKETS, 0].astype(I32)
    n_slots = 2 * ((n_tok // MOE_ROWS + N_PAIR_BUCKETS + 1) // 2)
    row_off, tile_a, tile_b, n_valid, n_active = _tile_plan(counts, n_slots)
    bucket = rt[:, 0, :].reshape(n_tok).astype(I32)
    pos = row_off[bucket] + rank.reshape(n_tok)
    idx = _invert(pos, n_sorted=n_slots * MOE_ROWS)
    out = _moe(tile_a, tile_b, n_valid, n_active.reshape(1), idx, xa, row(norm_ffn_g[l]),
               w_exp_gate[l].astype(BF16), w_exp_up[l].astype(BF16), w_exp_down[l].astype(BF16), n_slots=n_slots)
    return out.reshape(batch, seq, D_MODEL)
```

```python
import functools
import math

import jax
import jax.numpy as jnp
import numpy as np
from jax import lax
from jax.experimental import pallas as pl
from jax.experimental.pallas import tpu as pltpu

F32 = jnp.float32
BF16 = jnp.bfloat16
I32 = jnp.int32

D_MODEL = 1024
CHUNK = 64
N_MEM = 256
N_HEADS = 4
HEAD_DIM = 128
DIFF_QK_DIM = 64
CROSS_HEAD_DIM = 256
N_BUCKETS_T5 = 32
MAX_DISTANCE = 128
N_GROUPS = 4
EXPERTS_PER_GROUP = 4
N_EXPERTS = N_GROUPS * EXPERTS_PER_GROUP
D_EXPERT = 512
EPS = 1e-6
NEG_INF = -1e30
LAM_INIT = 0.8 - 0.6 * math.exp(-0.3 * 0)
LOG2E = math.log2(math.e)

PAIRS = ((0, 1), (0, 2), (0, 3), (1, 2), (1, 3), (2, 3))
N_PAIR_BUCKETS = N_GROUPS * len(PAIRS)

LANES = 128
VMEM_LIMIT_CAP = 56 * 1024 * 1024

PROJ_ROWS = 512
ATTN_TILE = 256
CROSS_ROWS = 512
MOE_ROWS = 256
COMBINE_ROWS = 256
XA_ROWS = D_MODEL // LANES + 1


def _vmem_limit(nbytes):
    return int(min(max(nbytes * 5 // 4, 32 * 1024 * 1024), VMEM_LIMIT_CAP))


def _nt_dot(a, b):
    return lax.dot_general(a, b, (((1,), (1,)), ((), ())), preferred_element_type=F32)


def _rms(x, g):
    ms = jnp.mean(x * x, axis=-1, keepdims=True)
    return x * lax.rsqrt(ms + EPS) * g


def _mix_proj_kernel(x_ref, g_ref, w_ref, wvt_ref, wft_ref, bf_ref, qg_ref, o_ref, vt_ref, cum_ref, carry_ref,
                     *, tm):
    si = pl.program_id(1)
    h = _rms(x_ref[...], g_ref[...]).astype(BF16)
    lane = lax.broadcasted_iota(I32, (tm, HEAD_DIM), 1)
    lo = lane < DIFF_QK_DIM
    for kind in range(4):
        p = jnp.dot(h, w_ref[:, kind * 512:(kind + 1) * 512], preferred_element_type=F32)
        for hh in range(N_HEADS):
            ph = p[:, hh * HEAD_DIM:(hh + 1) * HEAD_DIM]
            if kind in (0, 1):
                sq = ph * ph
                s_lo = jnp.sum(jnp.where(lo, sq, 0.0), axis=-1, keepdims=True)
                s_hi = jnp.sum(jnp.where(lo, 0.0, sq), axis=-1, keepdims=True)
                ms = jnp.where(lo, s_lo, s_hi) * (1.0 / DIFF_QK_DIM)
                ph = ph * lax.rsqrt(ms + EPS) * qg_ref[kind:kind + 1, :]
            else:
                ph = _rms(ph, qg_ref[kind:kind + 1, :])
            o_ref[0, kind * N_HEADS + hh] = ph.astype(BF16)
    for kind in range(2):
        pt = _nt_dot(wvt_ref[kind * 512:(kind + 1) * 512, :], h)
        for hh in range(N_HEADS):
            vt_ref[0, kind * N_HEADS + hh] = pt[hh * HEAD_DIM:(hh + 1) * HEAD_DIM, :].astype(BF16)

    z = _nt_dot(wft_ref[...], h) + bf_ref[...]
    logf = jnp.minimum(z, 0.0) - jnp.log(1.0 + jnp.exp(-jnp.abs(z)))
    lane8 = lax.broadcasted_iota(I32, (8, tm), 1)
    c = logf * LOG2E
    k = 1
    while k < tm:
        c = c + jnp.where(lane8 >= k, pltpu.roll(c, k, axis=1), 0.0)
        k *= 2

    @pl.when(si == 0)
    def _():
        carry_ref[...] = jnp.zeros_like(carry_ref)

    c = c + carry_ref[:, 0:1]
    cum_ref[0] = c
    carry_ref[...] = jnp.broadcast_to(c[:, tm - 1:tm], carry_ref.shape)


def _mix_proj(x2d, g, w_qk, w_vt, wft, bf, qg, *, batch, seq):
    tm = min(PROJ_ROWS, seq)
    ns = seq // tm
    est = 2 * (tm * D_MODEL * 4 + D_MODEL * 3072 * 2 + 24 * tm * HEAD_DIM * 2) + 8 * tm * D_MODEL
    return pl.pallas_call(
        functools.partial(_mix_proj_kernel, tm=tm),
        grid=(batch, ns),
        in_specs=[
            pl.BlockSpec((tm, D_MODEL), lambda b, s: (b * ns + s, 0)),
            pl.BlockSpec((1, D_MODEL), lambda b, s: (0, 0)),
            pl.BlockSpec((D_MODEL, 2048), lambda b, s: (0, 0)),
            pl.BlockSpec((1024, D_MODEL), lambda b, s: (0, 0)),
            pl.BlockSpec((8, D_MODEL), lambda b, s: (0, 0)),
            pl.BlockSpec((8, 1), lambda b, s: (0, 0)),
            pl.BlockSpec((8, HEAD_DIM), lambda b, s: (0, 0)),
        ],
        out_specs=[
            pl.BlockSpec((1, 16, tm, HEAD_DIM), lambda b, s: (b, 0, s, 0)),
            pl.BlockSpec((1, 8, HEAD_DIM, tm), lambda b, s: (b, 0, 0, s)),
            pl.BlockSpec((1, 8, tm), lambda b, s: (b, 0, s)),
        ],
        out_shape=[
            jax.ShapeDtypeStruct((batch, 16, seq, HEAD_DIM), BF16),
            jax.ShapeDtypeStruct((batch, 8, HEAD_DIM, seq), BF16),
            jax.ShapeDtypeStruct((batch, 8, seq), F32),
        ],
        scratch_shapes=[pltpu.VMEM((8, LANES), F32)],
        compiler_params=pltpu.CompilerParams(
            dimension_semantics=("arbitrary", "arbitrary"), vmem_limit_bytes=_vmem_limit(est)),
        name="mix_proj",
    )(x2d, g, w_qk, w_vt, wft, bf, qg)


def _init_softmax(m_sc, l_sc, acc_sc):
    m_sc[...] = jnp.full(m_sc.shape, NEG_INF, F32)
    l_sc[...] = jnp.zeros_like(l_sc)
    acc_sc[...] = jnp.zeros_like(acc_sc)


N_CHAINS = 3 * N_HEADS


def _attn_kernel(lam_ref, dq_ref, dk_ref, dvt_ref, fq_ref, fk_ref, fvt_ref, bias_ref, c_ref,
                 gd_ref, gf_ref, o_ref, m_sc, l_sc, acc_sc, qm_sc, ccol_sc, s_sc, mt_sc, *, t, seq):
    qi = pl.program_id(1)
    _init_softmax(m_sc, l_sc, acc_sc)
    key = lax.broadcasted_iota(I32, (t, t), 0)
    qry = lax.broadcasted_iota(I32, (t, t), 1)

    lane = lax.broadcasted_iota(I32, (t, HEAD_DIM), 1)
    for hh in range(N_HEADS):
        q = dq_ref[0, hh]
        zero = jnp.zeros_like(q)
        qm_sc[2 * hh] = jnp.where(lane < DIFF_QK_DIM, q, zero)
        qm_sc[2 * hh + 1] = jnp.where(lane < DIFF_QK_DIM, zero, q)

    @pl.when(qi == 0)
    def _():
        for hh in range(N_HEADS):
            for j in range(seq // t):
                row = c_ref[0, hh, :, j * t:(j + 1) * t]
                ccol_sc[hh, j * t:(j + 1) * t, :] = jnp.sum(
                    jnp.where(key == qry, jnp.broadcast_to(row, (t, t)), 0.0), axis=1, keepdims=True)

    q_off = pl.multiple_of(qi * t, t)

    def tile(j, where):
        off = pl.multiple_of(j * t, t)

        for hh in range(N_HEADS):
            k = dk_ref[0, hh, pl.ds(off, t), :]
            for c in range(2):
                s_t = _nt_dot(k, qm_sc[2 * hh + c])
                if where != "far":
                    s_t = s_t + bias_ref[hh, 1 if where == "near" else 0]
                s_sc[2 * hh + c] = s_t
                mt_sc[2 * hh + c] = jnp.max(s_t, axis=0, keepdims=True)
        for hh in range(N_HEADS):
            k = fk_ref[0, hh, pl.ds(off, t), :]
            c_k = ccol_sc[hh, pl.ds(off, t), :]
            s_t = _nt_dot(k, fq_ref[0, hh]) - c_k
            if where == "diag":
                s_t = jnp.where(key <= qry, s_t, NEG_INF)
            s_sc[2 * N_HEADS + hh] = s_t
            mt_sc[2 * N_HEADS + hh] = jnp.max(s_t, axis=0, keepdims=True) + c_ref[0, hh, :, pl.ds(q_off, t)]

        for slot in range(N_CHAINS):
            vt_ref = dvt_ref if slot < 2 * N_HEADS else fvt_ref
            hh = slot // 2 if slot < 2 * N_HEADS else slot - 2 * N_HEADS
            v_t = vt_ref[0, hh, :, pl.ds(off, t)]
            m_prev = m_sc[slot]
            m_new = jnp.maximum(m_prev, mt_sc[slot])
            alpha = jnp.exp2(m_prev - m_new)
            if slot < 2 * N_HEADS:
                p_t = jnp.exp2(s_sc[slot] - m_new)
            else:
                p_t = jnp.exp2(s_sc[slot] - (m_new - c_ref[0, hh, :, pl.ds(q_off, t)]))
            l_sc[slot] = alpha * l_sc[slot] + jnp.sum(p_t, axis=0, keepdims=True)
            acc_sc[slot] = alpha * acc_sc[slot] + jnp.dot(v_t, p_t.astype(BF16), preferred_element_type=F32)
            m_sc[slot] = m_new

    def far_body(j, carry):
        tile(j, "far")
        return carry

    lax.fori_loop(0, jnp.maximum(qi - 1, 0), far_body, 0)

    @pl.when(qi >= 1)
    def _():
        tile(qi - 1, "near")

    tile(qi, "diag")

    for hh in range(N_HEADS):
        a, b = 2 * hh, 2 * hh + 1
        o_t = acc_sc[a] / l_sc[a] - lam_ref[0] * (acc_sc[b] / l_sc[b])
        o_ref[:, hh * HEAD_DIM:(hh + 1) * HEAD_DIM] = (
            _rms(o_t.T, gd_ref[...]) * (1.0 - LAM_INIT)).astype(BF16)
    for hh in range(N_HEADS):
        s = 2 * N_HEADS + hh
        o_t = acc_sc[s] / l_sc[s]
        o_ref[:, (N_HEADS + hh) * HEAD_DIM:(N_HEADS + hh + 1) * HEAD_DIM] = _rms(o_t.T, gf_ref[...]).astype(BF16)


def _attn(qk, vt, bias_tiles_t, lam, cum4, subln_g, fox_g, *, batch, seq):
    t = min(ATTN_TILE, seq)
    nq = seq // t
    head_blk = N_HEADS * seq * HEAD_DIM * 2
    est = (2 * (2 * N_HEADS * t * HEAD_DIM * 2 + 4 * head_blk + N_HEADS * 2 * t * t * 4 + N_HEADS * 8 * seq * 4
                + t * 2 * N_HEADS * HEAD_DIM * 2)
           + N_CHAINS * (HEAD_DIM + 16) * t * 4 + 2 * N_HEADS * t * HEAD_DIM * 2 + N_HEADS * seq * LANES * 4
           + 48 * t * t * 4)
    qspec = lambda blk: pl.BlockSpec((1, N_HEADS, t, HEAD_DIM), lambda b, i: (b, blk, i, 0))
    kspec = lambda blk: pl.BlockSpec((1, N_HEADS, seq, HEAD_DIM), lambda b, i: (b, blk, 0, 0))
    vspec = lambda blk: pl.BlockSpec((1, N_HEADS, HEAD_DIM, seq), lambda b, i: (b, blk, 0, 0))
    return pl.pallas_call(
        functools.partial(_attn_kernel, t=t, seq=seq),
        grid=(batch, nq),
        in_specs=[
            pl.BlockSpec(memory_space=pltpu.SMEM),
            qspec(0), kspec(1), vspec(0),
            qspec(2), kspec(3), vspec(1),
            pl.BlockSpec((N_HEADS, 2, t, t), lambda b, i: (0, 0, 0, 0)),
            pl.BlockSpec((1, N_HEADS, 1, seq), lambda b, i: (b, 0, 0, 0)),
            pl.BlockSpec((1, HEAD_DIM), lambda b, i: (0, 0)),
            pl.BlockSpec((1, HEAD_DIM), lambda b, i: (0, 0)),
        ],
        out_specs=pl.BlockSpec((t, 2 * N_HEADS * HEAD_DIM), lambda b, i: (b * nq + i, 0)),
        out_shape=jax.ShapeDtypeStruct((batch * seq, 2 * N_HEADS * HEAD_DIM), BF16),
        scratch_shapes=[pltpu.VMEM((N_CHAINS, 1, t), F32), pltpu.VMEM((N_CHAINS, 1, t), F32),
                        pltpu.VMEM((N_CHAINS, HEAD_DIM, t), F32),
                        pltpu.VMEM((2 * N_HEADS, t, HEAD_DIM), BF16),
                        pltpu.VMEM((N_HEADS, seq, 1), F32),
                        pltpu.VMEM((N_CHAINS, t, t), F32), pltpu.VMEM((N_CHAINS, 1, t), F32)],
        compiler_params=pltpu.CompilerParams(
            dimension_semantics=("arbitrary", "arbitrary"), vmem_limit_bytes=_vmem_limit(est)),
        name="attn",
    )(lam, qk, qk, vt, qk, qk, vt, bias_tiles_t, cum4, subln_g, fox_g)


def _out_q_kernel(x_ref, m_ref, wo_ref, gc_ref, wq_ref, qg_ref, x1_ref, qc_ref):
    x1 = x_ref[...] + jnp.dot(m_ref[...], wo_ref[...], preferred_element_type=F32)
    x1_ref[...] = x1
    hc = _rms(x1, gc_ref[...]).astype(BF16)
    q = jnp.dot(hc, wq_ref[...], preferred_element_type=F32)
    for hh in range(N_HEADS):
        sl = slice(hh * CROSS_HEAD_DIM, (hh + 1) * CROSS_HEAD_DIM)
        qh = _rms(q[:, sl], qg_ref[...]) * (CROSS_HEAD_DIM ** -0.5)
        qc_ref[:, sl] = qh.astype(BF16)


def _out_q(x2d, mixed, w_out, g_cross, w_cq, q_g):
    n = x2d.shape[0]
    tm = min(PROJ_ROWS, n)
    est = 2 * (tm * D_MODEL * 4 * 2 + tm * D_MODEL * 2 + 2 * D_MODEL * D_MODEL * 2 + tm * D_MODEL * 2) + 6 * tm * D_MODEL * 4
    return pl.pallas_call(
        _out_q_kernel,
        grid=(n // tm,),
        in_specs=[
            pl.BlockSpec((tm, D_MODEL), lambda i: (i, 0)),
            pl.BlockSpec((tm, D_MODEL), lambda i: (i, 0)),
            pl.BlockSpec((D_MODEL, D_MODEL), lambda i: (0, 0)),
            pl.BlockSpec((1, D_MODEL), lambda i: (0, 0)),
            pl.BlockSpec((D_MODEL, D_MODEL), lambda i: (0, 0)),
            pl.BlockSpec((1, CROSS_HEAD_DIM), lambda i: (0, 0)),
        ],
        out_specs=[pl.BlockSpec((tm, D_MODEL), lambda i: (i, 0)),
                   pl.BlockSpec((tm, D_MODEL), lambda i: (i, 0))],
        out_shape=[jax.ShapeDtypeStruct((n, D_MODEL), F32), jax.ShapeDtypeStruct((n, D_MODEL), BF16)],
        compiler_params=pltpu.CompilerParams(
            dimension_semantics=("arbitrary",), vmem_limit_bytes=_vmem_limit(est)),
        name="out_q",
    )(x2d, mixed, w_out, g_cross, w_cq, q_g)


def _mem_kv_kernel(mem_ref, gm_ref, w_ref, kg_ref, k_ref, v_ref):
    mn = _rms(mem_ref[...], gm_ref[...]).astype(BF16)
    kv = jnp.dot(mn, w_ref[...], preferred_element_type=F32)
    for hh in range(N_HEADS):
        sl = slice(hh * CROSS_HEAD_DIM, (hh + 1) * CROSS_HEAD_DIM)
        k_ref[0, :, sl] = _rms(kv[:, sl], kg_ref[...]).astype(BF16)
    v_ref[0] = kv[:, D_MODEL:].astype(BF16)


def _mem_kv(mem2d, g_mem, w_ckv, k_g, *, batch):
    est = 2 * (N_MEM * D_MODEL * 4 + D_MODEL * 2 * D_MODEL * 2 + 2 * N_MEM * D_MODEL * 2) + 4 * N_MEM * 2 * D_MODEL * 4
    return pl.pallas_call(
        _mem_kv_kernel,
        grid=(batch,),
        in_specs=[
            pl.BlockSpec((N_MEM, D_MODEL), lambda b: (b, 0)),
            pl.BlockSpec((1, D_MODEL), lambda b: (0, 0)),
            pl.BlockSpec((D_MODEL, 2 * D_MODEL), lambda b: (0, 0)),
            pl.BlockSpec((1, CROSS_HEAD_DIM), lambda b: (0, 0)),
        ],
        out_specs=[pl.BlockSpec((1, N_MEM, D_MODEL), lambda b: (b, 0, 0)),
                   pl.BlockSpec((1, N_MEM, D_MODEL), lambda b: (b, 0, 0))],
        out_shape=[jax.ShapeDtypeStruct((batch, N_MEM, D_MODEL), BF16),
                   jax.ShapeDtypeStruct((batch, N_MEM, D_MODEL), BF16)],
        compiler_params=pltpu.CompilerParams(
            dimension_semantics=("arbitrary",), vmem_limit_bytes=_vmem_limit(est)),
        name="mem_kv",
    )(mem2d, g_mem, w_ckv, k_g)


def _route(logits_t):
    gl = logits_t[0:N_GROUPS]
    gmax = jnp.max(gl, axis=0, keepdims=True)
    eg = jnp.exp(gl - gmax)
    p_group = eg / jnp.sum(eg, axis=0, keepdims=True)
    p_g = jnp.max(p_group, axis=0, keepdims=True)
    g_idx = jnp.full_like(p_g, float(N_GROUPS))
    for g in reversed(range(N_GROUPS)):
        g_idx = jnp.where(p_group[g:g + 1] == p_g, float(g), g_idx)

    sel = []
    for j in range(EXPERTS_PER_GROUP):
        acc = jnp.zeros_like(p_g)
        for g in range(N_GROUPS):
            r = N_GROUPS + g * EXPERTS_PER_GROUP + j
            acc = jnp.where(g_idx == float(g), logits_t[r:r + 1], acc)
        sel.append(acc)
    smax = functools.reduce(jnp.maximum, sel)
    es = [jnp.exp(s - smax) for s in sel]
    den = functools.reduce(jnp.add, es)
    p_in = [e / den for e in es]

    v0 = functools.reduce(jnp.maximum, p_in)
    i0 = jnp.full_like(v0, float(EXPERTS_PER_GROUP))
    for j in reversed(range(EXPERTS_PER_GROUP)):
        i0 = jnp.where(p_in[j] == v0, float(j), i0)
    rest = [jnp.where(i0 == float(j), -1.0, p_in[j]) for j in range(EXPERTS_PER_GROUP)]
    v1 = functools.reduce(jnp.maximum, rest)
    i1 = jnp.full_like(v1, float(EXPERTS_PER_GROUP))
    for j in reversed(range(EXPERTS_PER_GROUP)):
        i1 = jnp.where(rest[j] == v1, float(j), i1)

    tot = v0 + v1
    w0 = p_g * (v0 / tot)
    w1 = p_g * (v1 / tot)
    first = i0 < i1
    a = jnp.where(first, i0, i1)
    b = jnp.where(first, i1, i0)
    wa = jnp.where(first, w0, w1)
    wb = jnp.where(first, w1, w0)
    pair = jnp.where(a == 0.0, 0.0, jnp.where(a == 1.0, 3.0, 5.0)) + (b - a - 1.0)
    bucket = g_idx * float(len(PAIRS)) + pair
    return bucket, wa, wb


def _cross_router_kernel(q_ref, k_ref, v_ref, x1_ref, wco_ref, gf_ref, whi_ref, wlo_ref, rb_ref,
                         xa_ref, rt_ref, *, tm):
    q = q_ref[...]
    k = k_ref[0]
    v = v_ref[0]
    outs = []
    for hh in range(N_HEADS):
        sl = slice(hh * CROSS_HEAD_DIM, (hh + 1) * CROSS_HEAD_DIM)
        s = _nt_dot(q[:, sl], k[:, sl])
        p = jnp.exp(s - jnp.max(s, axis=-1, keepdims=True))
        l = jnp.sum(p, axis=-1, keepdims=True)
        o = jnp.dot(p.astype(BF16), v[:, sl], preferred_element_type=F32) / l
        outs.append(o.astype(BF16))
    o = jnp.concatenate(outs, axis=1)
    x2 = x1_ref[...] + jnp.dot(o, wco_ref[...], preferred_element_type=F32)
    for j in range(D_MODEL // LANES):
        xa_ref[pl.ds(j, tm, stride=XA_ROWS), :] = x2[:, j * LANES:(j + 1) * LANES]

    h3 = _rms(x2, gf_ref[...])
    hi = h3.astype(BF16)
    lo = (h3 - hi.astype(F32)).astype(BF16)
    lt = _nt_dot(whi_ref[...], hi) + (_nt_dot(whi_ref[...], lo) + _nt_dot(wlo_ref[...], hi))
    lt = lt + rb_ref[...]
    bucket, wa, wb = _route(lt)

    sub = lax.broadcasted_iota(I32, (8, tm), 0)
    rows = jnp.where(sub == 0, bucket, jnp.where(sub == 1, wa, jnp.where(sub == 2, wb, 0.0)))
    rt_ref[0] = rows
    sub_a = lax.broadcasted_iota(I32, (LANES, tm), 0)
    aux_t = jnp.where(sub_a == 0, bucket, jnp.where(sub_a == 1, wa, jnp.where(sub_a == 2, wb, 0.0)))
    xa_ref[pl.ds(XA_ROWS - 1, tm, stride=XA_ROWS), :] = aux_t.T


def _cross_router(qc, kc, vc, x1, w_co, g_ffn, wr_hi, wr_lo, rb, *, batch, seq):
    tm = min(CROSS_ROWS, seq)
    ns = seq // tm
    n = batch * seq
    est = 2 * (tm * D_MODEL * 2 + 2 * N_MEM * D_MODEL * 2 + tm * D_MODEL * 4 + D_MODEL * D_MODEL * 2
               + tm * XA_ROWS * LANES * 4) + 10 * tm * D_MODEL * 4
    return pl.pallas_call(
        functools.partial(_cross_router_kernel, tm=tm),
        grid=(batch, ns),
        in_specs=[
            pl.BlockSpec((tm, D_MODEL), lambda b, s: (b * ns + s, 0)),
            pl.BlockSpec((1, N_MEM, D_MODEL), lambda b, s: (b, 0, 0)),
            pl.BlockSpec((1, N_MEM, D_MODEL), lambda b, s: (b, 0, 0)),
            pl.BlockSpec((tm, D_MODEL), lambda b, s: (b * ns + s, 0)),
            pl.BlockSpec((D_MODEL, D_MODEL), lambda b, s: (0, 0)),
            pl.BlockSpec((1, D_MODEL), lambda b, s: (0, 0)),
            pl.BlockSpec((32, D_MODEL), lambda b, s: (0, 0)),
            pl.BlockSpec((32, D_MODEL), lambda b, s: (0, 0)),
            pl.BlockSpec((32, 1), lambda b, s: (0, 0)),
        ],
        out_specs=[pl.BlockSpec((tm * XA_ROWS, LANES), lambda b, s: (b * ns + s, 0)),
                   pl.BlockSpec((1, 8, tm), lambda b, s: (b * ns + s, 0, 0))],
        out_shape=[jax.ShapeDtypeStruct((n * XA_ROWS, LANES), F32),
                   jax.ShapeDtypeStruct((n // tm, 8, tm), F32)],
        compiler_params=pltpu.CompilerParams(
            dimension_semantics=("arbitrary", "arbitrary"), vmem_limit_bytes=_vmem_limit(est)),
        name="cross_router",
    )(qc, kc, vc, x1, w_co, g_ffn, wr_hi, wr_lo, rb)


def _rank_kernel(rt_ref, rank_ref, cnt_ref, carry_sc, *, tm):
    @pl.when(pl.program_id(0) == 0)
    def _():
        carry_sc[...] = jnp.zeros_like(carry_sc)

    bucket = rt_ref[0, 0:1, :]
    sub = lax.broadcasted_iota(I32, (32, tm), 0).astype(F32)
    hit = sub == bucket
    r = lax.broadcasted_iota(I32, (tm, tm), 0)
    c = lax.broadcasted_iota(I32, (tm, tm), 1)
    upper = jnp.where(r <= c, 1.0, 0.0).astype(BF16)
    cum = jnp.dot(jnp.where(hit, 1.0, 0.0).astype(BF16), upper, preferred_element_type=F32)
    carry = carry_sc[:, 0:1]
    rank = jnp.sum(jnp.where(hit, cum - 1.0 + carry, 0.0), axis=0, keepdims=True)
    rank_ref[0] = rank.astype(I32)
    new_carry = carry_sc[...] + jnp.broadcast_to(cum[:, tm - 1:tm], carry_sc.shape)
    carry_sc[...] = new_carry
    cnt_ref[...] = new_carry


def _rank(rt):
    nt, _, tm = rt.shape
    return pl.pallas_call(
        functools.partial(_rank_kernel, tm=tm),
        grid=(nt,),
        in_specs=[pl.BlockSpec((1, 8, tm), lambda i: (i, 0, 0))],
        out_specs=[pl.BlockSpec((1, 1, tm), lambda i: (i, 0, 0)),
                   pl.BlockSpec((32, LANES), lambda i: (0, 0))],
        out_shape=[jax.ShapeDtypeStruct((nt, 1, tm), I32), jax.ShapeDtypeStruct((32, LANES), F32)],
        scratch_shapes=[pltpu.VMEM((32, LANES), F32)],
        compiler_params=pltpu.CompilerParams(dimension_semantics=("arbitrary",)),
        name="rank",
    )(rt)


def _invert_kernel(pos_ref, idx_ref, *, n_tok, n_sorted):
    def clear(r, carry):
        idx_ref[r] = 0
        return carry

    def put(t, carry):
        idx_ref[pos_ref[t]] = t
        return carry

    lax.fori_loop(0, n_sorted, clear, 0, unroll=8)
    lax.fori_loop(0, n_tok, put, 0, unroll=8)


def _invert(pos, *, n_sorted):
    n_tok = pos.shape[0]
    return pl.pallas_call(
        functools.partial(_invert_kernel, n_tok=n_tok, n_sorted=n_sorted),
        grid_spec=pltpu.PrefetchScalarGridSpec(
            num_scalar_prefetch=1, grid=(1,), in_specs=[],
            out_specs=pl.BlockSpec(memory_space=pltpu.SMEM)),
        out_shape=jax.ShapeDtypeStruct((n_sorted,), I32),
        name="invert",
    )(pos)


def _moe_kernel(ta_ref, tb_ref, nact_ref, idx_ref, xa_hbm, gf_ref, *refs):
    t_rows = MOE_ROWS
    weights = (refs[0:6], refs[6:12])
    ys_ref, xbuf0, xbuf1, gsem = refs[12:]
    xbuf = (xbuf0, xbuf1)
    n_act = nact_ref[0]

    def gather_row(t, r, par):
        return pltpu.make_async_copy(xa_hbm.at[pl.ds(idx_ref[t * t_rows + r] * XA_ROWS, XA_ROWS)],
                                     xbuf[par].at[pl.ds(r * XA_ROWS, XA_ROWS)], gsem.at[par])

    def gather_all(par):
        return pltpu.make_async_copy(xa_hbm.at[pl.ds(0, t_rows * XA_ROWS)], xbuf[par], gsem.at[par])

    def compute(par):
        x2 = jnp.concatenate([xbuf[par][pl.ds(j, t_rows, stride=XA_ROWS), :]
                              for j in range(D_MODEL // LANES)], axis=1)
        aux = xbuf[par][pl.ds(XA_ROWS - 1, t_rows, stride=XA_ROWS), :]
        wa = aux[:, 1:2]
        wb = aux[:, 2:3]
        h = _rms(x2, gf_ref[...]).astype(BF16)
        wga, wua, wda, wgb, wub, wdb = weights[par]

        def mlp(wg, wu, wd):
            g = jnp.dot(h, wg[0], preferred_element_type=F32)
            u = jnp.dot(h, wu[0], preferred_element_type=F32)
            act = (g / (1.0 + jnp.exp(-g))) * u
            return jnp.dot(act.astype(BF16), wd[0], preferred_element_type=F32)

        rows = slice(par * t_rows, (par + 1) * t_rows)
        ys_ref[rows, :] = x2 + (wa * mlp(wga, wua, wda) + wb * mlp(wgb, wub, wdb))

    @pl.when(pl.program_id(0) == 0)
    def _():
        for r in range(t_rows):
            gather_row(0, r, 0).start(priority=r % 2)

    for par in range(2):
        t = 2 * pl.program_id(0) + par

        @pl.when(t < n_act)
        def _(t=t, par=par):
            gather_all(par).wait()
            nxt = jnp.minimum(t + 1, n_act - 1)
            for r in range(t_rows):
                gather_row(nxt, r, 1 - par).start(priority=r % 2)
            compute(par)

        @pl.when(t == n_act)
        def _(par=par):
            gather_all(par).wait()

        @pl.when(t >= n_act)
        def _(par=par):
            ys_ref[par * t_rows:(par + 1) * t_rows, :] = jnp.zeros((t_rows, D_MODEL), F32)


def _moe(tile_a, tile_b, n_active, idx, xa, g_ffn, w_gate, w_up, w_down, *, n_slots):
    t = MOE_ROWS

    def wspec(shape, experts, par):
        if experts == "a":
            return pl.BlockSpec(shape, lambda g, ta, tb, na, ix: (ta[2 * g + par], 0, 0))
        return pl.BlockSpec(shape, lambda g, ta, tb, na, ix: (tb[2 * g + par], 0, 0))

    up_shape, down_shape = (1, D_MODEL, D_EXPERT), (1, D_EXPERT, D_MODEL)
    w_specs, w_args = [], []
    for par in range(2):
        for experts in ("a", "b"):
            w_specs += [wspec(up_shape, experts, par), wspec(up_shape, experts, par), wspec(down_shape, experts, par)]
            w_args += [w_gate, w_up, w_down]
    est = (2 * 12 * D_MODEL * D_EXPERT * 2 + 2 * t * XA_ROWS * LANES * 4 + 2 * 2 * t * D_MODEL * 4
           + 10 * t * D_MODEL * 4)
    return pl.pallas_call(
        _moe_kernel,
        grid_spec=pltpu.PrefetchScalarGridSpec(
            num_scalar_prefetch=4,
            grid=(n_slots // 2,),
            in_specs=[pl.BlockSpec(memory_space=pl.ANY),
                      pl.BlockSpec((1, D_MODEL), lambda g, *_: (0, 0))] + w_specs,
            out_specs=pl.BlockSpec((2 * t, D_MODEL), lambda g, *_: (g, 0)),
            scratch_shapes=[pltpu.VMEM((t * XA_ROWS, LANES), F32), pltpu.VMEM((t * XA_ROWS, LANES), F32),
                            pltpu.SemaphoreType.DMA((2,))],
        ),
        out_shape=jax.ShapeDtypeStruct((n_slots * t, D_MODEL), F32),
        compiler_params=pltpu.CompilerParams(
            dimension_semantics=("arbitrary",), vmem_limit_bytes=_vmem_limit(est), has_side_effects=True),
        name="moe",
    )(tile_a, tile_b, n_active, idx, xa, g_ffn, *w_args)


def _combine_kernel(pos_ref, ys_ref, out_ref, sem, *, rows):
    base = pl.program_id(0) * rows
    for r in range(rows):
        pltpu.make_async_copy(ys_ref.at[pl.ds(pos_ref[base + r], 1)], out_ref.at[pl.ds(r, 1)], sem).start(
            priority=r % 2)
    pltpu.make_async_copy(ys_ref.at[pl.ds(0, rows)], out_ref, sem).wait()


def _combine(pos, ys, *, n_tok):
    rows = min(COMBINE_ROWS, n_tok)
    return pl.pallas_call(
        functools.partial(_combine_kernel, rows=rows),
        grid_spec=pltpu.PrefetchScalarGridSpec(
            num_scalar_prefetch=1,
            grid=(n_tok // rows,),
            in_specs=[pl.BlockSpec(memory_space=pl.ANY)],
            out_specs=pl.BlockSpec((rows, D_MODEL), lambda i, *_: (i, 0)),
            scratch_shapes=[pltpu.SemaphoreType.DMA(())],
        ),
        out_shape=jax.ShapeDtypeStruct((n_tok, D_MODEL), F32),
        compiler_params=pltpu.CompilerParams(dimension_semantics=("arbitrary",), has_side_effects=True),
        name="combine",
    )(pos, ys)


def _t5_bucket(rel):
    nb = N_BUCKETS_T5 // 2
    max_exact = nb // 2
    ret = (rel > 0).astype(I32) * nb
    n = jnp.abs(rel)
    nf = jnp.maximum(n, 1).astype(F32)
    large = max_exact + (jnp.log(nf / max_exact) / math.log(MAX_DISTANCE / max_exact)
                         * (nb - max_exact)).astype(I32)
    large = jnp.minimum(large, nb - 1)
    return ret + jnp.where(n < max_exact, n, large)


def _diff_bias(rel_bias, t):
    assert t % CHUNK == 0 and t >= MAX_DISTANCE
    n_rel = 3 * t
    vec = rel_bias[_t5_bucket(jnp.arange(n_rel, dtype=I32) - (2 * t - 1))].astype(F32).T
    toe = jnp.tile(vec, (1, t))[:, :t * (n_rel - 1)].reshape(N_HEADS, t, n_rel - 1)
    qpos = jnp.arange(t, dtype=I32)[:, None]
    kpos = jnp.arange(t, dtype=I32)[None, :]
    far = rel_bias[_t5_bucket(jnp.asarray(-2 * t, I32))].astype(F32)[:, None, None]
    b0 = (toe[:, :, 2 * t - 1:3 * t - 1] - far) * LOG2E
    b0 = jnp.where((kpos // CHUNK <= qpos // CHUNK)[None], b0, NEG_INF)
    b1 = (toe[:, :, t - 1:2 * t - 1] - far) * LOG2E
    return jnp.swapaxes(jnp.stack([b0, b1], axis=1), -1, -2)


def _tile_plan(counts, n_slots):
    t = MOE_ROWS
    tiles = (counts + t - 1) // t
    ends = jnp.cumsum(tiles)
    starts = ends - tiles
    n_active = ends[-1]
    slot = jnp.arange(n_slots, dtype=I32)
    tile_idx = jnp.minimum(slot, n_active - 1)
    tile_bucket = jnp.minimum(jnp.sum((ends[None, :] <= tile_idx[:, None]).astype(I32), axis=1), N_PAIR_BUCKETS - 1)
    pa = jnp.asarray([p[0] for p in PAIRS], I32)
    pb = jnp.asarray([p[1] for p in PAIRS], I32)
    grp = tile_bucket // len(PAIRS)
    pair = tile_bucket % len(PAIRS)
    tile_a = grp * EXPERTS_PER_GROUP + pa[pair]
    tile_b = grp * EXPERTS_PER_GROUP + pb[pair]
    row_off = (starts * t).astype(I32)
    return row_off, tile_a.astype(I32), tile_b.astype(I32), n_active.astype(I32)


def kernel(x, mem, rel_bias, norm_mix_g, w_in, b_forget, diff_q_norm_g, diff_k_norm_g, diff_lambda_q1, diff_lambda_k1, diff_lambda_q2, diff_lambda_k2, diff_subln_g, fox_q_norm_g, fox_k_norm_g, fox_out_norm_g, w_out, norm_cross_g, norm_mem_g, w_cq, w_ckv, cross_q_norm_g, cross_k_norm_g, w_co, norm_ffn_g, w_group_router, b_group_router, w_expert_router, b_expert_router, w_exp_gate, w_exp_up, w_exp_down):
    batch, seq, d = x.shape
    assert d == D_MODEL and norm_mix_g.shape[0] == 1 and mem.shape[1] == N_MEM
    n_tok = batch * seq
    assert seq % ATTN_TILE == 0 or seq < ATTN_TILE
    l = 0

    w_qk = jnp.concatenate([w_in[l, :, :1024], w_in[l, :, 1536:2560]], axis=1).astype(BF16)
    w_vt = jnp.concatenate([w_in[l, :, 1024:1536], w_in[l, :, 2560:3072]], axis=1).T.astype(BF16)
    wft = jnp.zeros((8, D_MODEL), F32).at[:N_HEADS].set(w_in[l, :, 3072:].T).astype(BF16)
    bf = jnp.zeros((8, 1), F32).at[:N_HEADS, 0].set(b_forget[l].astype(F32))
    ones = jnp.ones((HEAD_DIM,), F32)
    qg = jnp.stack([
        jnp.tile(diff_q_norm_g[l].astype(F32), 2) * (DIFF_QK_DIM ** -0.5 * LOG2E),
        jnp.tile(diff_k_norm_g[l].astype(F32), 2),
        fox_q_norm_g[l].astype(F32) * (HEAD_DIM ** -0.5 * LOG2E), fox_k_norm_g[l].astype(F32),
        ones, ones, ones, ones])
    lam = (jnp.exp(jnp.sum(diff_lambda_q1[l].astype(F32) * diff_lambda_k1[l].astype(F32)))
           - jnp.exp(jnp.sum(diff_lambda_q2[l].astype(F32) * diff_lambda_k2[l].astype(F32)))
           + LAM_INIT).reshape(1)
    t_attn = min(ATTN_TILE, seq)
    bias_tiles_t = _diff_bias(rel_bias, t_attn)
    wr = jnp.zeros((32, D_MODEL), F32)
    wr = wr.at[:N_GROUPS].set(w_group_router[l].T).at[N_GROUPS:N_GROUPS + N_EXPERTS].set(w_expert_router[l].T)
    wr_hi = wr.astype(BF16)
    wr_lo = (wr - wr_hi.astype(F32)).astype(BF16)
    rb = jnp.zeros((32, 1), F32)
    rb = rb.at[:N_GROUPS, 0].set(b_group_router[l]).at[N_GROUPS:N_GROUPS + N_EXPERTS, 0].set(b_expert_router[l])

    row = lambda v: v.astype(F32).reshape(1, -1)
    x2d = x.reshape(n_tok, D_MODEL)

    qk, vt, cum = _mix_proj(x2d, row(norm_mix_g[l]), w_qk, w_vt, wft, bf, qg, batch=batch, seq=seq)
    cum4 = cum[:, :N_HEADS].reshape(batch, N_HEADS, 1, seq)
    mixed = _attn(qk, vt, bias_tiles_t, lam, cum4, row(diff_subln_g[l]), row(fox_out_norm_g[l]),
                  batch=batch, seq=seq)
    x1, qc = _out_q(x2d, mixed, w_out[l].astype(BF16), row(norm_cross_g[l]),
                    w_cq[l].astype(BF16), row(cross_q_norm_g[l]))

    kc, vc = _mem_kv(mem.reshape(batch * N_MEM, D_MODEL), row(norm_mem_g[l]), w_ckv[l].astype(BF16),
                     row(cross_k_norm_g[l]), batch=batch)
    xa, rt = _cross_router(qc, kc, vc, x1, w_co[l].astype(BF16), row(norm_ffn_g[l]), wr_hi, wr_lo, rb,
                           batch=batch, seq=seq)

    rank, cnt = _rank(rt)
    counts = cnt[:N_PAIR_BUCKETS, 0].astype(I32)
    n_slots = 2 * ((n_tok // MOE_ROWS + N_PAIR_BUCKETS + 1) // 2)
    row_off, tile_a, tile_b, n_active = _tile_plan(counts, n_slots)
    bucket = rt[:, 0, :].reshape(n_tok).astype(I32)
    pos = row_off[bucket] + rank.reshape(n_tok)
    idx = _invert(pos, n_sorted=n_slots * MOE_ROWS)
    ys = _moe(tile_a, tile_b, n_active.reshape(1), idx, xa, row(norm_ffn_g[l]),
              w_exp_gate[l].astype(BF16), w_exp_up[l].astype(BF16), w_exp_down[l].astype(BF16), n_slots=n_slots)
    out = _combine(pos, ys, n_tok=n_tok)
    return out.reshape(batch, seq, D_MODEL)
```

```python
import functools
import math

import jax
import jax.numpy as jnp
import numpy as np
from jax import lax
from jax.experimental import pallas as pl
from jax.experimental.pallas import tpu as pltpu

F32 = jnp.float32
BF16 = jnp.bfloat16
I32 = jnp.int32

D_MODEL = 1024
CHUNK = 64
N_MEM = 256
N_HEADS = 4
HEAD_DIM = 128
DIFF_QK_DIM = 64
CROSS_HEAD_DIM = 256
N_BUCKETS_T5 = 32
MAX_DISTANCE = 128
N_GROUPS = 4
EXPERTS_PER_GROUP = 4
N_EXPERTS = N_GROUPS * EXPERTS_PER_GROUP
D_EXPERT = 512
EPS = 1e-6
NEG_INF = -1e30
LAM_INIT = 0.8 - 0.6 * math.exp(-0.3 * 0)
LOG2E = math.log2(math.e)

PAIRS = ((0, 1), (0, 2), (0, 3), (1, 2), (1, 3), (2, 3))
N_PAIR_BUCKETS = N_GROUPS * len(PAIRS)

LANES = 128
VMEM_LIMIT_CAP = 56 * 1024 * 1024

PROJ_ROWS = 512
ATTN_TILE = 256
CROSS_ROWS = 512
MOE_ROWS = 256
AUX_COLS = LANES
ROW_W = D_MODEL + AUX_COLS
DMA_ROWS = 256


def _vmem_limit(nbytes):
    return int(min(max(nbytes * 5 // 4, 32 * 1024 * 1024), VMEM_LIMIT_CAP))


def _nt_dot(a, b):
    return lax.dot_general(a, b, (((1,), (1,)), ((), ())), preferred_element_type=F32)


def _rms(x, g):
    ms = jnp.mean(x * x, axis=-1, keepdims=True)
    return x * lax.rsqrt(ms + EPS) * g


def _mix_proj_kernel(x_ref, g_ref, w_ref, wvt_ref, wft_ref, bf_ref, qg_ref, o_ref, vt_ref, cum_ref, carry_ref,
                     *, tm):
    si = pl.program_id(1)
    h = _rms(x_ref[...], g_ref[...]).astype(BF16)
    lane = lax.broadcasted_iota(I32, (tm, HEAD_DIM), 1)
    lo = lane < DIFF_QK_DIM
    for kind in range(4):
        p = jnp.dot(h, w_ref[:, kind * 512:(kind + 1) * 512], preferred_element_type=F32)
        for hh in range(N_HEADS):
            ph = p[:, hh * HEAD_DIM:(hh + 1) * HEAD_DIM]
            if kind in (0, 1):
                sq = ph * ph
                s_lo = jnp.sum(jnp.where(lo, sq, 0.0), axis=-1, keepdims=True)
                s_hi = jnp.sum(jnp.where(lo, 0.0, sq), axis=-1, keepdims=True)
                ms = jnp.where(lo, s_lo, s_hi) * (1.0 / DIFF_QK_DIM)
                ph = ph * lax.rsqrt(ms + EPS) * qg_ref[kind:kind + 1, :]
            else:
                ph = _rms(ph, qg_ref[kind:kind + 1, :])
            o_ref[0, kind * N_HEADS + hh] = ph.astype(BF16)
    for kind in range(2):
        pt = _nt_dot(wvt_ref[kind * 512:(kind + 1) * 512, :], h)
        for hh in range(N_HEADS):
            vt_ref[0, kind * N_HEADS + hh] = pt[hh * HEAD_DIM:(hh + 1) * HEAD_DIM, :].astype(BF16)

    z = _nt_dot(wft_ref[...], h) + bf_ref[...]
    logf = jnp.minimum(z, 0.0) - jnp.log(1.0 + jnp.exp(-jnp.abs(z)))
    lane8 = lax.broadcasted_iota(I32, (8, tm), 1)
    c = logf * LOG2E
    k = 1
    while k < tm:
        c = c + jnp.where(lane8 >= k, pltpu.roll(c, k, axis=1), 0.0)
        k *= 2

    @pl.when(si == 0)
    def _():
        carry_ref[...] = jnp.zeros_like(carry_ref)

    c = c + carry_ref[:, 0:1]
    cum_ref[0] = c
    carry_ref[...] = jnp.broadcast_to(c[:, tm - 1:tm], carry_ref.shape)


def _mix_proj(x2d, g, w_qk, w_vt, wft, bf, qg, *, batch, seq):
    tm = min(PROJ_ROWS, seq)
    ns = seq // tm
    est = 2 * (tm * D_MODEL * 4 + D_MODEL * 3072 * 2 + 24 * tm * HEAD_DIM * 2) + 8 * tm * D_MODEL
    return pl.pallas_call(
        functools.partial(_mix_proj_kernel, tm=tm),
        grid=(batch, ns),
        in_specs=[
            pl.BlockSpec((tm, D_MODEL), lambda b, s: (b * ns + s, 0)),
            pl.BlockSpec((1, D_MODEL), lambda b, s: (0, 0)),
            pl.BlockSpec((D_MODEL, 2048), lambda b, s: (0, 0)),
            pl.BlockSpec((1024, D_MODEL), lambda b, s: (0, 0)),
            pl.BlockSpec((8, D_MODEL), lambda b, s: (0, 0)),
            pl.BlockSpec((8, 1), lambda b, s: (0, 0)),
            pl.BlockSpec((8, HEAD_DIM), lambda b, s: (0, 0)),
        ],
        out_specs=[
            pl.BlockSpec((1, 16, tm, HEAD_DIM), lambda b, s: (b, 0, s, 0)),
            pl.BlockSpec((1, 8, HEAD_DIM, tm), lambda b, s: (b, 0, 0, s)),
            pl.BlockSpec((1, 8, tm), lambda b, s: (b, 0, s)),
        ],
        out_shape=[
            jax.ShapeDtypeStruct((batch, 16, seq, HEAD_DIM), BF16),
            jax.ShapeDtypeStruct((batch, 8, HEAD_DIM, seq), BF16),
            jax.ShapeDtypeStruct((batch, 8, seq), F32),
        ],
        scratch_shapes=[pltpu.VMEM((8, LANES), F32)],
        compiler_params=pltpu.CompilerParams(
            dimension_semantics=("arbitrary", "arbitrary"), vmem_limit_bytes=_vmem_limit(est)),
        name="mix_proj",
    )(x2d, g, w_qk, w_vt, wft, bf, qg)


def _init_softmax(m_sc, l_sc, acc_sc):
    m_sc[...] = jnp.full(m_sc.shape, NEG_INF, F32)
    l_sc[...] = jnp.zeros_like(l_sc)
    acc_sc[...] = jnp.zeros_like(acc_sc)


N_CHAINS = 3 * N_HEADS


def _attn_kernel(lam_ref, dq_ref, dk_ref, dvt_ref, fq_ref, fk_ref, fvt_ref, bias_ref, c_ref,
                 gd_ref, gf_ref, o_ref, m_sc, l_sc, acc_sc, qm_sc, ccol_sc, s_sc, mt_sc, *, t, seq):
    qi = pl.program_id(1)
    _init_softmax(m_sc, l_sc, acc_sc)
    key = lax.broadcasted_iota(I32, (t, t), 0)
    qry = lax.broadcasted_iota(I32, (t, t), 1)

    lane = lax.broadcasted_iota(I32, (t, HEAD_DIM), 1)
    for hh in range(N_HEADS):
        q = dq_ref[0, hh]
        zero = jnp.zeros_like(q)
        qm_sc[2 * hh] = jnp.where(lane < DIFF_QK_DIM, q, zero)
        qm_sc[2 * hh + 1] = jnp.where(lane < DIFF_QK_DIM, zero, q)

    @pl.when(qi == 0)
    def _():
        for hh in range(N_HEADS):
            for j in range(seq // t):
                row = c_ref[0, hh, :, j * t:(j + 1) * t]
                ccol_sc[hh, j * t:(j + 1) * t, :] = jnp.sum(
                    jnp.where(key == qry, jnp.broadcast_to(row, (t, t)), 0.0), axis=1, keepdims=True)

    q_off = pl.multiple_of(qi * t, t)

    def tile(j, where):
        off = pl.multiple_of(j * t, t)

        for hh in range(N_HEADS):
            k = dk_ref[0, hh, pl.ds(off, t), :]
            for c in range(2):
                s_t = _nt_dot(k, qm_sc[2 * hh + c])
                if where != "far":
                    s_t = s_t + bias_ref[hh, 1 if where == "near" else 0]
                s_sc[2 * hh + c] = s_t
                mt_sc[2 * hh + c] = jnp.max(s_t, axis=0, keepdims=True)
        for hh in range(N_HEADS):
            k = fk_ref[0, hh, pl.ds(off, t), :]
            c_k = ccol_sc[hh, pl.ds(off, t), :]
            s_t = _nt_dot(k, fq_ref[0, hh]) - c_k
            if where == "diag":
                s_t = jnp.where(key <= qry, s_t, NEG_INF)
            s_sc[2 * N_HEADS + hh] = s_t
            mt_sc[2 * N_HEADS + hh] = jnp.max(s_t, axis=0, keepdims=True) + c_ref[0, hh, :, pl.ds(q_off, t)]

        for slot in range(N_CHAINS):
            vt_ref = dvt_ref if slot < 2 * N_HEADS else fvt_ref
            hh = slot // 2 if slot < 2 * N_HEADS else slot - 2 * N_HEADS
            v_t = vt_ref[0, hh, :, pl.ds(off, t)]
            m_prev = m_sc[slot]
            m_new = jnp.maximum(m_prev, mt_sc[slot])
            alpha = jnp.exp2(m_prev - m_new)
            if slot < 2 * N_HEADS:
                p_t = jnp.exp2(s_sc[slot] - m_new)
            else:
                p_t = jnp.exp2(s_sc[slot] - (m_new - c_ref[0, hh, :, pl.ds(q_off, t)]))
            l_sc[slot] = alpha * l_sc[slot] + jnp.sum(p_t, axis=0, keepdims=True)
            acc_sc[slot] = alpha * acc_sc[slot] + jnp.dot(v_t, p_t.astype(BF16), preferred_element_type=F32)
            m_sc[slot] = m_new

    def far_body(j, carry):
        tile(j, "far")
        return carry

    lax.fori_loop(0, jnp.maximum(qi - 1, 0), far_body, 0)

    @pl.when(qi >= 1)
    def _():
        tile(qi - 1, "near")

    tile(qi, "diag")

    for hh in range(N_HEADS):
        a, b = 2 * hh, 2 * hh + 1
        o_t = acc_sc[a] / l_sc[a] - lam_ref[0] * (acc_sc[b] / l_sc[b])
        o_ref[:, hh * HEAD_DIM:(hh + 1) * HEAD_DIM] = (
            _rms(o_t.T, gd_ref[...]) * (1.0 - LAM_INIT)).astype(BF16)
    for hh in range(N_HEADS):
        s = 2 * N_HEADS + hh
        o_t = acc_sc[s] / l_sc[s]
        o_ref[:, (N_HEADS + hh) * HEAD_DIM:(N_HEADS + hh + 1) * HEAD_DIM] = _rms(o_t.T, gf_ref[...]).astype(BF16)


def _attn(qk, vt, bias_tiles_t, lam, cum4, subln_g, fox_g, *, batch, seq):
    t = min(ATTN_TILE, seq)
    nq = seq // t
    head_blk = N_HEADS * seq * HEAD_DIM * 2
    est = (2 * (2 * N_HEADS * t * HEAD_DIM * 2 + 4 * head_blk + N_HEADS * 2 * t * t * 4 + N_HEADS * 8 * seq * 4
                + t * 2 * N_HEADS * HEAD_DIM * 2)
           + N_CHAINS * (HEAD_DIM + 16) * t * 4 + 2 * N_HEADS * t * HEAD_DIM * 2 + N_HEADS * seq * LANES * 4
           + 48 * t * t * 4)
    qspec = lambda blk: pl.BlockSpec((1, N_HEADS, t, HEAD_DIM), lambda b, i: (b, blk, i, 0))
    kspec = lambda blk: pl.BlockSpec((1, N_HEADS, seq, HEAD_DIM), lambda b, i: (b, blk, 0, 0))
    vspec = lambda blk: pl.BlockSpec((1, N_HEADS, HEAD_DIM, seq), lambda b, i: (b, blk, 0, 0))
    return pl.pallas_call(
        functools.partial(_attn_kernel, t=t, seq=seq),
        grid=(batch, nq),
        in_specs=[
            pl.BlockSpec(memory_space=pltpu.SMEM),
            qspec(0), kspec(1), vspec(0),
            qspec(2), kspec(3), vspec(1),
            pl.BlockSpec((N_HEADS, 2, t, t), lambda b, i: (0, 0, 0, 0)),
            pl.BlockSpec((1, N_HEADS, 1, seq), lambda b, i: (b, 0, 0, 0)),
            pl.BlockSpec((1, HEAD_DIM), lambda b, i: (0, 0)),
            pl.BlockSpec((1, HEAD_DIM), lambda b, i: (0, 0)),
        ],
        out_specs=pl.BlockSpec((t, 2 * N_HEADS * HEAD_DIM), lambda b, i: (b * nq + i, 0)),
        out_shape=jax.ShapeDtypeStruct((batch * seq, 2 * N_HEADS * HEAD_DIM), BF16),
        scratch_shapes=[pltpu.VMEM((N_CHAINS, 1, t), F32), pltpu.VMEM((N_CHAINS, 1, t), F32),
                        pltpu.VMEM((N_CHAINS, HEAD_DIM, t), F32),
                        pltpu.VMEM((2 * N_HEADS, t, HEAD_DIM), BF16),
                        pltpu.VMEM((N_HEADS, seq, 1), F32),
                        pltpu.VMEM((N_CHAINS, t, t), F32), pltpu.VMEM((N_CHAINS, 1, t), F32)],
        compiler_params=pltpu.CompilerParams(
            dimension_semantics=("arbitrary", "arbitrary"), vmem_limit_bytes=_vmem_limit(est)),
        name="attn",
    )(lam, qk, qk, vt, qk, qk, vt, bias_tiles_t, cum4, subln_g, fox_g)


def _out_q_kernel(x_ref, m_ref, wo_ref, gc_ref, wq_ref, qg_ref, x1_ref, qc_ref):
    x1 = x_ref[...] + jnp.dot(m_ref[...], wo_ref[...], preferred_element_type=F32)
    x1_ref[...] = x1
    hc = _rms(x1, gc_ref[...]).astype(BF16)
    q = jnp.dot(hc, wq_ref[...], preferred_element_type=F32)
    for hh in range(N_HEADS):
        sl = slice(hh * CROSS_HEAD_DIM, (hh + 1) * CROSS_HEAD_DIM)
        qh = _rms(q[:, sl], qg_ref[...]) * (CROSS_HEAD_DIM ** -0.5)
        qc_ref[:, sl] = qh.astype(BF16)


def _out_q(x2d, mixed, w_out, g_cross, w_cq, q_g):
    n = x2d.shape[0]
    tm = min(PROJ_ROWS, n)
    est = 2 * (tm * D_MODEL * 4 * 2 + tm * D_MODEL * 2 + 2 * D_MODEL * D_MODEL * 2 + tm * D_MODEL * 2) + 6 * tm * D_MODEL * 4
    return pl.pallas_call(
        _out_q_kernel,
        grid=(n // tm,),
        in_specs=[
            pl.BlockSpec((tm, D_MODEL), lambda i: (i, 0)),
            pl.BlockSpec((tm, D_MODEL), lambda i: (i, 0)),
            pl.BlockSpec((D_MODEL, D_MODEL), lambda i: (0, 0)),
            pl.BlockSpec((1, D_MODEL), lambda i: (0, 0)),
            pl.BlockSpec((D_MODEL, D_MODEL), lambda i: (0, 0)),
            pl.BlockSpec((1, CROSS_HEAD_DIM), lambda i: (0, 0)),
        ],
        out_specs=[pl.BlockSpec((tm, D_MODEL), lambda i: (i, 0)),
                   pl.BlockSpec((tm, D_MODEL), lambda i: (i, 0))],
        out_shape=[jax.ShapeDtypeStruct((n, D_MODEL), F32), jax.ShapeDtypeStruct((n, D_MODEL), BF16)],
        compiler_params=pltpu.CompilerParams(
            dimension_semantics=("arbitrary",), vmem_limit_bytes=_vmem_limit(est)),
        name="out_q",
    )(x2d, mixed, w_out, g_cross, w_cq, q_g)


def _mem_kv_kernel(mem_ref, gm_ref, w_ref, kg_ref, k_ref, v_ref):
    mn = _rms(mem_ref[...], gm_ref[...]).astype(BF16)
    kv = jnp.dot(mn, w_ref[...], preferred_element_type=F32)
    for hh in range(N_HEADS):
        sl = slice(hh * CROSS_HEAD_DIM, (hh + 1) * CROSS_HEAD_DIM)
        k_ref[0, :, sl] = _rms(kv[:, sl], kg_ref[...]).astype(BF16)
    v_ref[0] = kv[:, D_MODEL:].astype(BF16)


def _mem_kv(mem2d, g_mem, w_ckv, k_g, *, batch):
    est = 2 * (N_MEM * D_MODEL * 4 + D_MODEL * 2 * D_MODEL * 2 + 2 * N_MEM * D_MODEL * 2) + 4 * N_MEM * 2 * D_MODEL * 4
    return pl.pallas_call(
        _mem_kv_kernel,
        grid=(batch,),
        in_specs=[
            pl.BlockSpec((N_MEM, D_MODEL), lambda b: (b, 0)),
            pl.BlockSpec((1, D_MODEL), lambda b: (0, 0)),
            pl.BlockSpec((D_MODEL, 2 * D_MODEL), lambda b: (0, 0)),
            pl.BlockSpec((1, CROSS_HEAD_DIM), lambda b: (0, 0)),
        ],
        out_specs=[pl.BlockSpec((1, N_MEM, D_MODEL), lambda b: (b, 0, 0)),
                   pl.BlockSpec((1, N_MEM, D_MODEL), lambda b: (b, 0, 0))],
        out_shape=[jax.ShapeDtypeStruct((batch, N_MEM, D_MODEL), BF16),
                   jax.ShapeDtypeStruct((batch, N_MEM, D_MODEL), BF16)],
        compiler_params=pltpu.CompilerParams(
            dimension_semantics=("arbitrary",), vmem_limit_bytes=_vmem_limit(est)),
        name="mem_kv",
    )(mem2d, g_mem, w_ckv, k_g)


def _route(logits_t):
    gl = logits_t[0:N_GROUPS]
    gmax = jnp.max(gl, axis=0, keepdims=True)
    eg = jnp.exp(gl - gmax)
    p_group = eg / jnp.sum(eg, axis=0, keepdims=True)
    p_g = jnp.max(p_group, axis=0, keepdims=True)
    g_idx = jnp.full_like(p_g, float(N_GROUPS))
    for g in reversed(range(N_GROUPS)):
        g_idx = jnp.where(p_group[g:g + 1] == p_g, float(g), g_idx)

    sel = []
    for j in range(EXPERTS_PER_GROUP):
        acc = jnp.zeros_like(p_g)
        for g in range(N_GROUPS):
            r = N_GROUPS + g * EXPERTS_PER_GROUP + j
            acc = jnp.where(g_idx == float(g), logits_t[r:r + 1], acc)
        sel.append(acc)
    smax = functools.reduce(jnp.maximum, sel)
    es = [jnp.exp(s - smax) for s in sel]
    den = functools.reduce(jnp.add, es)
    p_in = [e / den for e in es]

    v0 = functools.reduce(jnp.maximum, p_in)
    i0 = jnp.full_like(v0, float(EXPERTS_PER_GROUP))
    for j in reversed(range(EXPERTS_PER_GROUP)):
        i0 = jnp.where(p_in[j] == v0, float(j), i0)
    rest = [jnp.where(i0 == float(j), -1.0, p_in[j]) for j in range(EXPERTS_PER_GROUP)]
    v1 = functools.reduce(jnp.maximum, rest)
    i1 = jnp.full_like(v1, float(EXPERTS_PER_GROUP))
    for j in reversed(range(EXPERTS_PER_GROUP)):
        i1 = jnp.where(rest[j] == v1, float(j), i1)

    tot = v0 + v1
    w0 = p_g * (v0 / tot)
    w1 = p_g * (v1 / tot)
    first = i0 < i1
    a = jnp.where(first, i0, i1)
    b = jnp.where(first, i1, i0)
    wa = jnp.where(first, w0, w1)
    wb = jnp.where(first, w1, w0)
    pair = jnp.where(a == 0.0, 0.0, jnp.where(a == 1.0, 3.0, 5.0)) + (b - a - 1.0)
    bucket = g_idx * float(len(PAIRS)) + pair
    return bucket, wa, wb


def _cross_router_kernel(q_ref, k_ref, v_ref, x1_ref, wco_ref, gf_ref, whi_ref, wlo_ref, rb_ref,
                         xa_ref, rt_ref, *, tm):
    q = q_ref[...]
    k = k_ref[0]
    v = v_ref[0]
    outs = []
    for hh in range(N_HEADS):
        sl = slice(hh * CROSS_HEAD_DIM, (hh + 1) * CROSS_HEAD_DIM)
        s = _nt_dot(q[:, sl], k[:, sl])
        p = jnp.exp(s - jnp.max(s, axis=-1, keepdims=True))
        l = jnp.sum(p, axis=-1, keepdims=True)
        o = jnp.dot(p.astype(BF16), v[:, sl], preferred_element_type=F32) / l
        outs.append(o.astype(BF16))
    o = jnp.concatenate(outs, axis=1)
    x2 = x1_ref[...] + jnp.dot(o, wco_ref[...], preferred_element_type=F32)
    xa_ref[:, :D_MODEL] = x2

    h3 = _rms(x2, gf_ref[...])
    hi = h3.astype(BF16)
    lo = (h3 - hi.astype(F32)).astype(BF16)
    lt = _nt_dot(whi_ref[...], hi) + (_nt_dot(whi_ref[...], lo) + _nt_dot(wlo_ref[...], hi))
    lt = lt + rb_ref[...]
    bucket, wa, wb = _route(lt)

    sub = lax.broadcasted_iota(I32, (8, tm), 0)
    rows = jnp.where(sub == 0, bucket, jnp.where(sub == 1, wa, jnp.where(sub == 2, wb, 0.0)))
    rt_ref[0] = rows
    sub_a = lax.broadcasted_iota(I32, (AUX_COLS, tm), 0)
    aux_t = jnp.where(sub_a == 0, bucket, jnp.where(sub_a == 1, wa, jnp.where(sub_a == 2, wb, 0.0)))
    xa_ref[:, D_MODEL:] = aux_t.T


def _cross_router(qc, kc, vc, x1, w_co, g_ffn, wr_hi, wr_lo, rb, *, batch, seq):
    tm = min(CROSS_ROWS, seq)
    ns = seq // tm
    n = batch * seq
    est = 2 * (tm * D_MODEL * 2 + 2 * N_MEM * D_MODEL * 2 + tm * D_MODEL * 4 + D_MODEL * D_MODEL * 2
               + tm * ROW_W * 4) + 10 * tm * D_MODEL * 4
    return pl.pallas_call(
        functools.partial(_cross_router_kernel, tm=tm),
        grid=(batch, ns),
        in_specs=[
            pl.BlockSpec((tm, D_MODEL), lambda b, s: (b * ns + s, 0)),
            pl.BlockSpec((1, N_MEM, D_MODEL), lambda b, s: (b, 0, 0)),
            pl.BlockSpec((1, N_MEM, D_MODEL), lambda b, s: (b, 0, 0)),
            pl.BlockSpec((tm, D_MODEL), lambda b, s: (b * ns + s, 0)),
            pl.BlockSpec((D_MODEL, D_MODEL), lambda b, s: (0, 0)),
            pl.BlockSpec((1, D_MODEL), lambda b, s: (0, 0)),
            pl.BlockSpec((32, D_MODEL), lambda b, s: (0, 0)),
            pl.BlockSpec((32, D_MODEL), lambda b, s: (0, 0)),
            pl.BlockSpec((32, 1), lambda b, s: (0, 0)),
        ],
        out_specs=[pl.BlockSpec((tm, ROW_W), lambda b, s: (b * ns + s, 0)),
                   pl.BlockSpec((1, 8, tm), lambda b, s: (b * ns + s, 0, 0))],
        out_shape=[jax.ShapeDtypeStruct((n, ROW_W), F32),
                   jax.ShapeDtypeStruct((n // tm, 8, tm), F32)],
        compiler_params=pltpu.CompilerParams(
            dimension_semantics=("arbitrary", "arbitrary"), vmem_limit_bytes=_vmem_limit(est)),
        name="cross_router",
    )(qc, kc, vc, x1, w_co, g_ffn, wr_hi, wr_lo, rb)


def _rank_kernel(rt_ref, rank_ref, cnt_ref, carry_sc, *, tm):
    @pl.when(pl.program_id(0) == 0)
    def _():
        carry_sc[...] = jnp.zeros_like(carry_sc)

    bucket = rt_ref[0, 0:1, :]
    sub = lax.broadcasted_iota(I32, (32, tm), 0).astype(F32)
    hit = sub == bucket
    r = lax.broadcasted_iota(I32, (tm, tm), 0)
    c = lax.broadcasted_iota(I32, (tm, tm), 1)
    upper = jnp.where(r <= c, 1.0, 0.0).astype(BF16)
    cum = jnp.dot(jnp.where(hit, 1.0, 0.0).astype(BF16), upper, preferred_element_type=F32)
    carry = carry_sc[:, 0:1]
    rank = jnp.sum(jnp.where(hit, cum - 1.0 + carry, 0.0), axis=0, keepdims=True)
    rank_ref[0] = rank.astype(I32)
    new_carry = carry_sc[...] + jnp.broadcast_to(cum[:, tm - 1:tm], carry_sc.shape)
    carry_sc[...] = new_carry
    cnt_ref[...] = new_carry


def _rank(rt):
    nt, _, tm = rt.shape
    return pl.pallas_call(
        functools.partial(_rank_kernel, tm=tm),
        grid=(nt,),
        in_specs=[pl.BlockSpec((1, 8, tm), lambda i: (i, 0, 0))],
        out_specs=[pl.BlockSpec((1, 1, tm), lambda i: (i, 0, 0)),
                   pl.BlockSpec((32, LANES), lambda i: (0, 0))],
        out_shape=[jax.ShapeDtypeStruct((nt, 1, tm), I32), jax.ShapeDtypeStruct((32, LANES), F32)],
        scratch_shapes=[pltpu.VMEM((32, LANES), F32)],
        compiler_params=pltpu.CompilerParams(dimension_semantics=("arbitrary",)),
        name="rank",
    )(rt)


def _dispatch_kernel(pos_ref, zrow_ref, zflag_ref, nact_ref, xa_ref, xs_ref, zbuf, zsem, sem, *,
                     n_tiles, rows):
    @pl.when(pl.program_id(0) == 0)
    def _():
        _zero_partial_tiles(zrow_ref, zflag_ref, nact_ref, xs_ref, zbuf, zsem, n_tiles=n_tiles)

    base = pl.program_id(0) * rows
    for r in range(rows):
        pltpu.make_async_copy(xa_ref.at[pl.ds(r, 1)], xs_ref.at[pl.ds(pos_ref[base + r], 1)], sem).start(
            priority=r % 2)
    pltpu.make_async_copy(xa_ref, xs_ref.at[pl.ds(0, rows)], sem).wait()


def _zero_partial_tiles(zrow_ref, zflag_ref, nact_ref, xs_ref, zbuf, zsem, *, n_tiles):
    zbuf[...] = jnp.zeros_like(zbuf)

    def zero_copy(row):
        return pltpu.make_async_copy(zbuf, xs_ref.at[pl.ds(pl.multiple_of(row, MOE_ROWS), MOE_ROWS)], zsem)

    def bucket_tiles(op):
        def body(b, carry):
            @pl.when(zflag_ref[b] > 0)
            def _():
                op(zero_copy(zrow_ref[b]))
            return carry
        lax.fori_loop(0, N_PAIR_BUCKETS, body, 0)

    def unused_tiles(op):
        def body(i, carry):
            op(zero_copy(i * MOE_ROWS))
            return carry
        lax.fori_loop(nact_ref[0], n_tiles, body, 0)

    bucket_tiles(lambda cp: cp.start())
    unused_tiles(lambda cp: cp.start())
    bucket_tiles(lambda cp: cp.wait())
    unused_tiles(lambda cp: cp.wait())


def _dispatch(pos, zrow, zflag, n_active, xa, *, n_tiles):
    n_tok = xa.shape[0]
    n_sorted = n_tiles * MOE_ROWS
    rows = min(DMA_ROWS, n_tok)
    return pl.pallas_call(
        functools.partial(_dispatch_kernel, n_tiles=n_tiles, rows=rows),
        grid_spec=pltpu.PrefetchScalarGridSpec(
            num_scalar_prefetch=4,
            grid=(n_tok // rows,),
            in_specs=[pl.BlockSpec((rows, ROW_W), lambda i, *_: (i, 0))],
            out_specs=pl.BlockSpec(memory_space=pl.ANY),
            scratch_shapes=[pltpu.VMEM((MOE_ROWS, ROW_W), F32), pltpu.SemaphoreType.DMA(()),
                            pltpu.SemaphoreType.DMA(())],
        ),
        out_shape=jax.ShapeDtypeStruct((n_sorted, ROW_W), F32),
        compiler_params=pltpu.CompilerParams(dimension_semantics=("arbitrary",), has_side_effects=True),
        name="dispatch",
    )(pos, zrow, zflag, n_active, xa)


def _moe_kernel(ta_ref, tb_ref, xi_ref, nact_ref, xs_ref, gf_ref, wga_ref, wua_ref, wda_ref,
                wgb_ref, wub_ref, wdb_ref, ys_ref):
    i = pl.program_id(0)

    @pl.when(i < nact_ref[0])
    def _():
        x2 = xs_ref[:, :D_MODEL]
        aux = xs_ref[:, D_MODEL:]
        wa = aux[:, 1:2]
        wb = aux[:, 2:3]
        h = _rms(x2, gf_ref[...]).astype(BF16)

        def mlp(wg, wu, wd):
            g = jnp.dot(h, wg[0].astype(BF16), preferred_element_type=F32)
            u = jnp.dot(h, wu[0].astype(BF16), preferred_element_type=F32)
            act = (g / (1.0 + jnp.exp(-g))) * u
            return jnp.dot(act.astype(BF16), wd[0].astype(BF16), preferred_element_type=F32)

        ys_ref[...] = x2 + (wa * mlp(wga_ref, wua_ref, wda_ref) + wb * mlp(wgb_ref, wub_ref, wdb_ref))

    @pl.when(i >= nact_ref[0])
    def _():
        ys_ref[...] = jnp.zeros_like(ys_ref)


def _moe(tile_a, tile_b, tile_x, n_active, xs, g_ffn, w_gate, w_up, w_down):
    n_tiles = xs.shape[0] // MOE_ROWS
    t = MOE_ROWS
    wa_map = lambda i, ta, tb, xi, na: (ta[i], 0, 0)
    wb_map = lambda i, ta, tb, xi, na: (tb[i], 0, 0)
    est = 2 * (t * ROW_W * 4 + 6 * D_MODEL * D_EXPERT * 4 + t * D_MODEL * 4) + 10 * t * D_MODEL * 4
    return pl.pallas_call(
        _moe_kernel,
        grid_spec=pltpu.PrefetchScalarGridSpec(
            num_scalar_prefetch=4,
            grid=(n_tiles,),
            in_specs=[
                pl.BlockSpec((t, ROW_W), lambda i, ta, tb, xi, na: (xi[i], 0)),
                pl.BlockSpec((1, D_MODEL), lambda i, ta, tb, xi, na: (0, 0)),
                pl.BlockSpec((1, D_MODEL, D_EXPERT), wa_map),
                pl.BlockSpec((1, D_MODEL, D_EXPERT), wa_map),
                pl.BlockSpec((1, D_EXPERT, D_MODEL), wa_map),
                pl.BlockSpec((1, D_MODEL, D_EXPERT), wb_map),
                pl.BlockSpec((1, D_MODEL, D_EXPERT), wb_map),
                pl.BlockSpec((1, D_EXPERT, D_MODEL), wb_map),
            ],
            out_specs=pl.BlockSpec((t, D_MODEL), lambda i, ta, tb, xi, na: (i, 0)),
        ),
        out_shape=jax.ShapeDtypeStruct((n_tiles * t, D_MODEL), F32),
        compiler_params=pltpu.CompilerParams(
            dimension_semantics=("arbitrary",), vmem_limit_bytes=_vmem_limit(est)),
        name="moe",
    )(tile_a, tile_b, tile_x, n_active, xs, g_ffn, w_gate, w_up, w_down, w_gate, w_up, w_down)


def _combine_kernel(pos_ref, ys_ref, out_ref, sem, *, rows):
    base = pl.program_id(0) * rows
    for r in range(rows):
        pltpu.make_async_copy(ys_ref.at[pl.ds(pos_ref[base + r], 1)], out_ref.at[pl.ds(r, 1)], sem).start(
            priority=r % 2)
    pltpu.make_async_copy(ys_ref.at[pl.ds(0, rows)], out_ref, sem).wait()


def _combine(pos, ys, *, n_tok):
    rows = min(DMA_ROWS, n_tok)
    return pl.pallas_call(
        functools.partial(_combine_kernel, rows=rows),
        grid_spec=pltpu.PrefetchScalarGridSpec(
            num_scalar_prefetch=1,
            grid=(n_tok // rows,),
            in_specs=[pl.BlockSpec(memory_space=pl.ANY)],
            out_specs=pl.BlockSpec((rows, D_MODEL), lambda i, *_: (i, 0)),
            scratch_shapes=[pltpu.SemaphoreType.DMA(())],
        ),
        out_shape=jax.ShapeDtypeStruct((n_tok, D_MODEL), F32),
        compiler_params=pltpu.CompilerParams(dimension_semantics=("arbitrary",), has_side_effects=True),
        name="combine",
    )(pos, ys)


def _t5_bucket(rel):
    nb = N_BUCKETS_T5 // 2
    max_exact = nb // 2
    ret = (rel > 0).astype(I32) * nb
    n = jnp.abs(rel)
    nf = jnp.maximum(n, 1).astype(F32)
    large = max_exact + (jnp.log(nf / max_exact) / math.log(MAX_DISTANCE / max_exact)
                         * (nb - max_exact)).astype(I32)
    large = jnp.minimum(large, nb - 1)
    return ret + jnp.where(n < max_exact, n, large)


def _diff_bias(rel_bias, t):
    assert t % CHUNK == 0 and t >= MAX_DISTANCE
    n_rel = 3 * t
    vec = rel_bias[_t5_bucket(jnp.arange(n_rel, dtype=I32) - (2 * t - 1))].astype(F32).T
    toe = jnp.tile(vec, (1, t))[:, :t * (n_rel - 1)].reshape(N_HEADS, t, n_rel - 1)
    qpos = jnp.arange(t, dtype=I32)[:, None]
    kpos = jnp.arange(t, dtype=I32)[None, :]
    far = rel_bias[_t5_bucket(jnp.asarray(-2 * t, I32))].astype(F32)[:, None, None]
    b0 = (toe[:, :, 2 * t - 1:3 * t - 1] - far) * LOG2E
    b0 = jnp.where((kpos // CHUNK <= qpos // CHUNK)[None], b0, NEG_INF)
    b1 = (toe[:, :, t - 1:2 * t - 1] - far) * LOG2E
    return jnp.swapaxes(jnp.stack([b0, b1], axis=1), -1, -2)


def _tile_plan(counts, n_tiles):
    t = MOE_ROWS
    tiles = (counts + t - 1) // t
    ends = jnp.cumsum(tiles)
    starts = ends - tiles
    n_active = ends[-1]
    tile_idx = jnp.minimum(jnp.arange(n_tiles, dtype=I32), n_active - 1)
    tile_bucket = jnp.minimum(jnp.sum((ends[None, :] <= tile_idx[:, None]).astype(I32), axis=1), N_PAIR_BUCKETS - 1)
    pa = jnp.asarray([p[0] for p in PAIRS], I32)
    pb = jnp.asarray([p[1] for p in PAIRS], I32)
    grp = tile_bucket // len(PAIRS)
    pair = tile_bucket % len(PAIRS)
    tile_a = grp * EXPERTS_PER_GROUP + pa[pair]
    tile_b = grp * EXPERTS_PER_GROUP + pb[pair]
    row_off = (starts * t).astype(I32)
    zrow = ((ends - 1) * t).astype(I32)
    zflag = (counts > 0).astype(I32)
    return row_off, zrow, zflag, tile_a.astype(I32), tile_b.astype(I32), tile_idx.astype(I32), n_active.astype(I32)


def kernel(x, mem, rel_bias, norm_mix_g, w_in, b_forget, diff_q_norm_g, diff_k_norm_g, diff_lambda_q1, diff_lambda_k1, diff_lambda_q2, diff_lambda_k2, diff_subln_g, fox_q_norm_g, fox_k_norm_g, fox_out_norm_g, w_out, norm_cross_g, norm_mem_g, w_cq, w_ckv, cross_q_norm_g, cross_k_norm_g, w_co, norm_ffn_g, w_group_router, b_group_router, w_expert_router, b_expert_router, w_exp_gate, w_exp_up, w_exp_down):
    batch, seq, d = x.shape
    assert d == D_MODEL and norm_mix_g.shape[0] == 1 and mem.shape[1] == N_MEM
    n_tok = batch * seq
    assert seq % ATTN_TILE == 0 or seq < ATTN_TILE
    l = 0

    w_qk = jnp.concatenate([w_in[l, :, :1024], w_in[l, :, 1536:2560]], axis=1).astype(BF16)
    w_vt = jnp.concatenate([w_in[l, :, 1024:1536], w_in[l, :, 2560:3072]], axis=1).T.astype(BF16)
    wft = jnp.zeros((8, D_MODEL), F32).at[:N_HEADS].set(w_in[l, :, 3072:].T).astype(BF16)
    bf = jnp.zeros((8, 1), F32).at[:N_HEADS, 0].set(b_forget[l].astype(F32))
    ones = jnp.ones((HEAD_DIM,), F32)
    qg = jnp.stack([
        jnp.tile(diff_q_norm_g[l].astype(F32), 2) * (DIFF_QK_DIM ** -0.5 * LOG2E),
        jnp.tile(diff_k_norm_g[l].astype(F32), 2),
        fox_q_norm_g[l].astype(F32) * (HEAD_DIM ** -0.5 * LOG2E), fox_k_norm_g[l].astype(F32),
        ones, ones, ones, ones])
    lam = (jnp.exp(jnp.sum(diff_lambda_q1[l].astype(F32) * diff_lambda_k1[l].astype(F32)))
           - jnp.exp(jnp.sum(diff_lambda_q2[l].astype(F32) * diff_lambda_k2[l].astype(F32)))
           + LAM_INIT).reshape(1)
    t_attn = min(ATTN_TILE, seq)
    bias_tiles_t = _diff_bias(rel_bias, t_attn)
    wr = jnp.zeros((32, D_MODEL), F32)
    wr = wr.at[:N_GROUPS].set(w_group_router[l].T).at[N_GROUPS:N_GROUPS + N_EXPERTS].set(w_expert_router[l].T)
    wr_hi = wr.astype(BF16)
    wr_lo = (wr - wr_hi.astype(F32)).astype(BF16)
    rb = jnp.zeros((32, 1), F32)
    rb = rb.at[:N_GROUPS, 0].set(b_group_router[l]).at[N_GROUPS:N_GROUPS + N_EXPERTS, 0].set(b_expert_router[l])

    row = lambda v: v.astype(F32).reshape(1, -1)
    x2d = x.reshape(n_tok, D_MODEL)

    qk, vt, cum = _mix_proj(x2d, row(norm_mix_g[l]), w_qk, w_vt, wft, bf, qg, batch=batch, seq=seq)
    cum4 = cum[:, :N_HEADS].reshape(batch, N_HEADS, 1, seq)
    mixed = _attn(qk, vt, bias_tiles_t, lam, cum4, row(diff_subln_g[l]), row(fox_out_norm_g[l]),
                  batch=batch, seq=seq)
    x1, qc = _out_q(x2d, mixed, w_out[l].astype(BF16), row(norm_cross_g[l]),
                    w_cq[l].astype(BF16), row(cross_q_norm_g[l]))

    kc, vc = _mem_kv(mem.reshape(batch * N_MEM, D_MODEL), row(norm_mem_g[l]), w_ckv[l].astype(BF16),
                     row(cross_k_norm_g[l]), batch=batch)
    xa, rt = _cross_router(qc, kc, vc, x1, w_co[l].astype(BF16), row(norm_ffn_g[l]), wr_hi, wr_lo, rb,
                           batch=batch, seq=seq)

    rank, cnt = _rank(rt)
    counts = cnt[:N_PAIR_BUCKETS, 0].astype(I32)
    n_tiles = n_tok // MOE_ROWS + N_PAIR_BUCKETS
    row_off, zrow, zflag, tile_a, tile_b, tile_x, n_active = _tile_plan(counts, n_tiles)
    bucket = rt[:, 0, :].reshape(n_tok).astype(I32)
    pos = row_off[bucket] + rank.reshape(n_tok)
    n_active = n_active.reshape(1)
    xs = _dispatch(pos, zrow, zflag, n_active, xa, n_tiles=n_tiles)
    ys = _moe(tile_a, tile_b, tile_x, n_active, xs, row(norm_ffn_g[l]),
              w_exp_gate[l].astype(F32), w_exp_up[l].astype(F32), w_exp_down[l].astype(F32))
    out = _combine(pos, ys, n_tok=n_tok)
    return out.reshape(batch, seq, D_MODEL)
```

```python
import functools
import math

import jax
import jax.numpy as jnp
import numpy as np
from jax import lax
from jax.experimental import pallas as pl
from jax.experimental.pallas import tpu as pltpu

F32 = jnp.float32
BF16 = jnp.bfloat16
I32 = jnp.int32

D_MODEL = 1024
CHUNK = 64
N_MEM = 256
N_HEADS = 4
HEAD_DIM = 128
VT_ROWS = HEAD_DIM + 16
DIFF_QK_DIM = 64
CROSS_HEAD_DIM = 256
N_BUCKETS_T5 = 32
MAX_DISTANCE = 128
N_GROUPS = 4
EXPERTS_PER_GROUP = 4
N_EXPERTS = N_GROUPS * EXPERTS_PER_GROUP
D_EXPERT = 512
EPS = 1e-6
NEG_INF = -1e30
LAM_INIT = 0.8 - 0.6 * math.exp(-0.3 * 0)
LOG2E = math.log2(math.e)

PAIRS = ((0, 1), (0, 2), (0, 3), (1, 3), (1, 2), (3, 2))
N_PAIR_BUCKETS = N_GROUPS * len(PAIRS)

LANES = 128
VMEM_LIMIT_CAP = 56 * 1024 * 1024

PROJ_ROWS = 1024
ATTN_TILE = 256
CROSS_ROWS = 512
MOE_ROWS = 256
AUX_COLS = LANES
ROW_W = D_MODEL + AUX_COLS
DMA_ROWS = 256


def _vmem_limit(nbytes):
    return int(min(max(nbytes * 5 // 4, 32 * 1024 * 1024), VMEM_LIMIT_CAP))


def _nt_dot(a, b):
    return lax.dot_general(a, b, (((1,), (1,)), ((), ())), preferred_element_type=F32)


def _rms(x, g):
    ms = jnp.mean(x * x, axis=-1, keepdims=True)
    return x * lax.rsqrt(ms + EPS) * g


def _mix_proj_kernel(x_ref, g_ref, w_ref, wvt_ref, wft_ref, bf_ref, qg_ref, o_ref, vt_ref, cum_ref, carry_ref,
                     *, tm):
    si = pl.program_id(1)
    h = _rms(x_ref[...], g_ref[...]).astype(BF16)
    lane = lax.broadcasted_iota(I32, (tm, HEAD_DIM), 1)
    lo = lane < DIFF_QK_DIM
    for kind in range(4):
        p = jnp.dot(h, w_ref[:, kind * 512:(kind + 1) * 512], preferred_element_type=F32)
        for hh in range(N_HEADS):
            ph = p[:, hh * HEAD_DIM:(hh + 1) * HEAD_DIM]
            if kind in (0, 1):
                sq = ph * ph
                s_lo = jnp.sum(jnp.where(lo, sq, 0.0), axis=-1, keepdims=True)
                s_hi = jnp.sum(jnp.where(lo, 0.0, sq), axis=-1, keepdims=True)
                ms = jnp.where(lo, s_lo, s_hi) * (1.0 / DIFF_QK_DIM)
                ph = ph * lax.rsqrt(ms + EPS) * qg_ref[kind:kind + 1, :]
            else:
                ph = _rms(ph, qg_ref[kind:kind + 1, :])
            o_ref[0, kind * N_HEADS + hh] = ph.astype(BF16)
    for kind in range(2):
        pt = _nt_dot(wvt_ref[kind * 512:(kind + 1) * 512, :], h)
        for hh in range(N_HEADS):
            vt_ref[0, kind * N_HEADS + hh, :HEAD_DIM, :] = pt[hh * HEAD_DIM:(hh + 1) * HEAD_DIM, :].astype(BF16)
            vt_ref[0, kind * N_HEADS + hh, HEAD_DIM:, :] = jnp.ones((VT_ROWS - HEAD_DIM, tm), BF16)

    z = _nt_dot(wft_ref[...], h) + bf_ref[...]
    logf = jnp.minimum(z, 0.0) - jnp.log(1.0 + jnp.exp(-jnp.abs(z)))
    lane8 = lax.broadcasted_iota(I32, (8, tm), 1)
    c = logf * LOG2E
    k = 1
    while k < tm:
        c = c + jnp.where(lane8 >= k, pltpu.roll(c, k, axis=1), 0.0)
        k *= 2

    @pl.when(si == 0)
    def _():
        carry_ref[...] = jnp.zeros_like(carry_ref)

    c = c + carry_ref[:, 0:1]
    cum_ref[0] = c
    carry_ref[...] = jnp.broadcast_to(c[:, tm - 1:tm], carry_ref.shape)


def _mix_proj(x2d, g, w_qk, w_vt, wft, bf, qg, *, batch, seq):
    tm = min(PROJ_ROWS, seq)
    ns = seq // tm
    est = 2 * (tm * D_MODEL * 4 + D_MODEL * 3072 * 2 + 24 * tm * HEAD_DIM * 2) + 8 * tm * D_MODEL
    return pl.pallas_call(
        functools.partial(_mix_proj_kernel, tm=tm),
        grid=(batch, ns),
        in_specs=[
            pl.BlockSpec((tm, D_MODEL), lambda b, s: (b * ns + s, 0)),
            pl.BlockSpec((1, D_MODEL), lambda b, s: (0, 0)),
            pl.BlockSpec((D_MODEL, 2048), lambda b, s: (0, 0)),
            pl.BlockSpec((1024, D_MODEL), lambda b, s: (0, 0)),
            pl.BlockSpec((8, D_MODEL), lambda b, s: (0, 0)),
            pl.BlockSpec((8, 1), lambda b, s: (0, 0)),
            pl.BlockSpec((8, HEAD_DIM), lambda b, s: (0, 0)),
        ],
        out_specs=[
            pl.BlockSpec((1, 16, tm, HEAD_DIM), lambda b, s: (b, 0, s, 0)),
            pl.BlockSpec((1, 8, VT_ROWS, tm), lambda b, s: (b, 0, 0, s)),
            pl.BlockSpec((1, 8, tm), lambda b, s: (b, 0, s)),
        ],
        out_shape=[
            jax.ShapeDtypeStruct((batch, 16, seq, HEAD_DIM), BF16),
            jax.ShapeDtypeStruct((batch, 8, VT_ROWS, seq), BF16),
            jax.ShapeDtypeStruct((batch, 8, seq), F32),
        ],
        scratch_shapes=[pltpu.VMEM((8, LANES), F32)],
        compiler_params=pltpu.CompilerParams(
            dimension_semantics=("arbitrary", "arbitrary"), vmem_limit_bytes=_vmem_limit(est)),
        name="mix_proj",
    )(x2d, g, w_qk, w_vt, wft, bf, qg)


def _init_softmax(m_sc, acc_sc):
    m_sc[...] = jnp.full(m_sc.shape, NEG_INF, F32)
    acc_sc[...] = jnp.zeros_like(acc_sc)


N_CHAINS = 3 * N_HEADS


def _attn_kernel(lam_ref, dq_ref, dk_ref, dvt_ref, fq_ref, fk_ref, fvt_ref, bias_ref, c_ref,
                 gd_ref, gf_ref, o_ref, m_sc, acc_sc, qm_sc, ccol_sc, s_sc, mt_sc, *, t, seq):
    qi = pl.program_id(1)
    _init_softmax(m_sc, acc_sc)
    key = lax.broadcasted_iota(I32, (t, t), 0)
    qry = lax.broadcasted_iota(I32, (t, t), 1)

    lane = lax.broadcasted_iota(I32, (t, HEAD_DIM), 1)
    for hh in range(N_HEADS):
        q = dq_ref[0, hh]
        zero = jnp.zeros_like(q)
        qm_sc[2 * hh] = jnp.where(lane < DIFF_QK_DIM, q, zero)
        qm_sc[2 * hh + 1] = jnp.where(lane < DIFF_QK_DIM, zero, q)

    @pl.when(qi == 0)
    def _():
        for hh in range(N_HEADS):
            for j in range(seq // t):
                row = c_ref[0, hh, :, j * t:(j + 1) * t]
                ccol_sc[hh, j * t:(j + 1) * t, :] = jnp.sum(
                    jnp.where(key == qry, jnp.broadcast_to(row, (t, t)), 0.0), axis=1, keepdims=True)

    q_off = pl.multiple_of(qi * t, t)

    def tile(j, where):
        off = pl.multiple_of(j * t, t)

        for hh in range(N_HEADS):
            k = dk_ref[0, hh, pl.ds(off, t), :]
            for c in range(2):
                s_t = _nt_dot(k, qm_sc[2 * hh + c])
                if where != "far":
                    s_t = s_t + bias_ref[hh, 1 if where == "near" else 0]
                s_sc[2 * hh + c] = s_t
                mt_sc[2 * hh + c] = jnp.max(s_t, axis=0, keepdims=True)
        for hh in range(N_HEADS):
            k = fk_ref[0, hh, pl.ds(off, t), :]
            c_k = ccol_sc[hh, pl.ds(off, t), :]
            s_t = _nt_dot(k, fq_ref[0, hh]) - c_k
            if where == "diag":
                s_t = jnp.where(key <= qry, s_t, NEG_INF)
            s_sc[2 * N_HEADS + hh] = s_t
            mt_sc[2 * N_HEADS + hh] = jnp.max(s_t, axis=0, keepdims=True) + c_ref[0, hh, :, pl.ds(q_off, t)]

        for slot in range(N_CHAINS):
            vt_ref = dvt_ref if slot < 2 * N_HEADS else fvt_ref
            hh = slot // 2 if slot < 2 * N_HEADS else slot - 2 * N_HEADS
            v_t = vt_ref[0, hh, :, pl.ds(off, t)]
            m_prev = m_sc[slot]
            m_new = jnp.maximum(m_prev, mt_sc[slot])
            alpha = jnp.exp2(m_prev - m_new)
            if slot < 2 * N_HEADS:
                p_t = jnp.exp2(s_sc[slot] - m_new)
            else:
                p_t = jnp.exp2(s_sc[slot] - (m_new - c_ref[0, hh, :, pl.ds(q_off, t)]))
            acc_sc[slot] = alpha * acc_sc[slot] + jnp.dot(v_t, p_t.astype(BF16), preferred_element_type=F32)
            m_sc[slot] = m_new

    def far_body(j, carry):
        tile(j, "far")
        return carry

    lax.fori_loop(0, jnp.maximum(qi - 1, 0), far_body, 0)

    @pl.when(qi >= 1)
    def _():
        tile(qi - 1, "near")

    tile(qi, "diag")

    def normalised(slot):
        return acc_sc[slot, :HEAD_DIM, :] / acc_sc[slot, HEAD_DIM:HEAD_DIM + 1, :]

    for hh in range(N_HEADS):
        o_t = normalised(2 * hh) - lam_ref[0] * normalised(2 * hh + 1)
        o_ref[:, hh * HEAD_DIM:(hh + 1) * HEAD_DIM] = (
            _rms(o_t.T, gd_ref[...]) * (1.0 - LAM_INIT)).astype(BF16)
    for hh in range(N_HEADS):
        o_t = normalised(2 * N_HEADS + hh)
        o_ref[:, (N_HEADS + hh) * HEAD_DIM:(N_HEADS + hh + 1) * HEAD_DIM] = _rms(o_t.T, gf_ref[...]).astype(BF16)


def _attn(qk, vt, bias_tiles_t, lam, cum4, subln_g, fox_g, *, batch, seq):
    t = min(ATTN_TILE, seq)
    nq = seq // t
    head_blk = N_HEADS * seq * HEAD_DIM * 2
    est = (2 * (2 * N_HEADS * t * HEAD_DIM * 2 + 4 * head_blk + N_HEADS * 2 * t * t * 4 + N_HEADS * 8 * seq * 4
                + t * 2 * N_HEADS * HEAD_DIM * 2)
           + N_CHAINS * (HEAD_DIM + 16) * t * 4 + 2 * N_HEADS * t * HEAD_DIM * 2 + N_HEADS * seq * LANES * 4
           + 48 * t * t * 4)
    qspec = lambda blk: pl.BlockSpec((1, N_HEADS, t, HEAD_DIM), lambda b, i: (b, blk, i, 0))
    kspec = lambda blk: pl.BlockSpec((1, N_HEADS, seq, HEAD_DIM), lambda b, i: (b, blk, 0, 0))
    vspec = lambda blk: pl.BlockSpec((1, N_HEADS, VT_ROWS, seq), lambda b, i: (b, blk, 0, 0))
    return pl.pallas_call(
        functools.partial(_attn_kernel, t=t, seq=seq),
        grid=(batch, nq),
        in_specs=[
            pl.BlockSpec(memory_space=pltpu.SMEM),
            qspec(0), kspec(1), vspec(0),
            qspec(2), kspec(3), vspec(1),
            pl.BlockSpec((N_HEADS, 2, t, t), lambda b, i: (0, 0, 0, 0)),
            pl.BlockSpec((1, N_HEADS, 1, seq), lambda b, i: (b, 0, 0, 0)),
            pl.BlockSpec((1, HEAD_DIM), lambda b, i: (0, 0)),
            pl.BlockSpec((1, HEAD_DIM), lambda b, i: (0, 0)),
        ],
        out_specs=pl.BlockSpec((t, 2 * N_HEADS * HEAD_DIM), lambda b, i: (b * nq + i, 0)),
        out_shape=jax.ShapeDtypeStruct((batch * seq, 2 * N_HEADS * HEAD_DIM), BF16),
        scratch_shapes=[pltpu.VMEM((N_CHAINS, 1, t), F32),
                        pltpu.VMEM((N_CHAINS, VT_ROWS, t), F32),
                        pltpu.VMEM((2 * N_HEADS, t, HEAD_DIM), BF16),
                        pltpu.VMEM((N_HEADS, seq, 1), F32),
                        pltpu.VMEM((N_CHAINS, t, t), F32), pltpu.VMEM((N_CHAINS, 1, t), F32)],
        compiler_params=pltpu.CompilerParams(
            dimension_semantics=("arbitrary", "arbitrary"), vmem_limit_bytes=_vmem_limit(est)),
        name="attn",
    )(lam, qk, qk, vt, qk, qk, vt, bias_tiles_t, cum4, subln_g, fox_g)


def _out_q_kernel(x_ref, m_ref, wo_ref, gc_ref, wq_ref, qg_ref, x1_ref, qc_ref):
    x1 = x_ref[...] + jnp.dot(m_ref[...], wo_ref[...], preferred_element_type=F32)
    x1_ref[...] = x1
    hc = _rms(x1, gc_ref[...]).astype(BF16)
    q = jnp.dot(hc, wq_ref[...], preferred_element_type=F32)
    for hh in range(N_HEADS):
        sl = slice(hh * CROSS_HEAD_DIM, (hh + 1) * CROSS_HEAD_DIM)
        qh = _rms(q[:, sl], qg_ref[...]) * (CROSS_HEAD_DIM ** -0.5)
        qc_ref[:, sl] = qh.astype(BF16)


def _out_q(x2d, mixed, w_out, g_cross, w_cq, q_g):
    n = x2d.shape[0]
    tm = min(PROJ_ROWS, n)
    est = 2 * (tm * D_MODEL * 4 * 2 + tm * D_MODEL * 2 + 2 * D_MODEL * D_MODEL * 2 + tm * D_MODEL * 2) + 6 * tm * D_MODEL * 4
    return pl.pallas_call(
        _out_q_kernel,
        grid=(n // tm,),
        in_specs=[
            pl.BlockSpec((tm, D_MODEL), lambda i: (i, 0)),
            pl.BlockSpec((tm, D_MODEL), lambda i: (i, 0)),
            pl.BlockSpec((D_MODEL, D_MODEL), lambda i: (0, 0)),
            pl.BlockSpec((1, D_MODEL), lambda i: (0, 0)),
            pl.BlockSpec((D_MODEL, D_MODEL), lambda i: (0, 0)),
            pl.BlockSpec((1, CROSS_HEAD_DIM), lambda i: (0, 0)),
        ],
        out_specs=[pl.BlockSpec((tm, D_MODEL), lambda i: (i, 0)),
                   pl.BlockSpec((tm, D_MODEL), lambda i: (i, 0))],
        out_shape=[jax.ShapeDtypeStruct((n, D_MODEL), F32), jax.ShapeDtypeStruct((n, D_MODEL), BF16)],
        compiler_params=pltpu.CompilerParams(
            dimension_semantics=("arbitrary",), vmem_limit_bytes=_vmem_limit(est)),
        name="out_q",
    )(x2d, mixed, w_out, g_cross, w_cq, q_g)


def _mem_kv_kernel(mem_ref, gm_ref, w_ref, kg_ref, k_ref, v_ref):
    mn = _rms(mem_ref[...], gm_ref[...]).astype(BF16)
    kv = jnp.dot(mn, w_ref[...], preferred_element_type=F32)
    for hh in range(N_HEADS):
        sl = slice(hh * CROSS_HEAD_DIM, (hh + 1) * CROSS_HEAD_DIM)
        k_ref[0, :, sl] = _rms(kv[:, sl], kg_ref[...]).astype(BF16)
    v_ref[0] = kv[:, D_MODEL:].astype(BF16)


def _mem_kv(mem2d, g_mem, w_ckv, k_g, *, batch):
    est = 2 * (N_MEM * D_MODEL * 4 + D_MODEL * 2 * D_MODEL * 2 + 2 * N_MEM * D_MODEL * 2) + 4 * N_MEM * 2 * D_MODEL * 4
    return pl.pallas_call(
        _mem_kv_kernel,
        grid=(batch,),
        in_specs=[
            pl.BlockSpec((N_MEM, D_MODEL), lambda b: (b, 0)),
            pl.BlockSpec((1, D_MODEL), lambda b: (0, 0)),
            pl.BlockSpec((D_MODEL, 2 * D_MODEL), lambda b: (0, 0)),
            pl.BlockSpec((1, CROSS_HEAD_DIM), lambda b: (0, 0)),
        ],
        out_specs=[pl.BlockSpec((1, N_MEM, D_MODEL), lambda b: (b, 0, 0)),
                   pl.BlockSpec((1, N_MEM, D_MODEL), lambda b: (b, 0, 0))],
        out_shape=[jax.ShapeDtypeStruct((batch, N_MEM, D_MODEL), BF16),
                   jax.ShapeDtypeStruct((batch, N_MEM, D_MODEL), BF16)],
        compiler_params=pltpu.CompilerParams(
            dimension_semantics=("arbitrary",), vmem_limit_bytes=_vmem_limit(est)),
        name="mem_kv",
    )(mem2d, g_mem, w_ckv, k_g)


def _route(logits_t):
    gl = logits_t[0:N_GROUPS]
    gmax = jnp.max(gl, axis=0, keepdims=True)
    eg = jnp.exp(gl - gmax)
    p_group = eg / jnp.sum(eg, axis=0, keepdims=True)
    p_g = jnp.max(p_group, axis=0, keepdims=True)
    g_idx = jnp.full_like(p_g, float(N_GROUPS))
    for g in reversed(range(N_GROUPS)):
        g_idx = jnp.where(p_group[g:g + 1] == p_g, float(g), g_idx)

    sel = []
    for j in range(EXPERTS_PER_GROUP):
        acc = jnp.zeros_like(p_g)
        for g in range(N_GROUPS):
            r = N_GROUPS + g * EXPERTS_PER_GROUP + j
            acc = jnp.where(g_idx == float(g), logits_t[r:r + 1], acc)
        sel.append(acc)
    smax = functools.reduce(jnp.maximum, sel)
    es = [jnp.exp(s - smax) for s in sel]
    den = functools.reduce(jnp.add, es)
    p_in = [e / den for e in es]

    v0 = functools.reduce(jnp.maximum, p_in)
    i0 = jnp.full_like(v0, float(EXPERTS_PER_GROUP))
    for j in reversed(range(EXPERTS_PER_GROUP)):
        i0 = jnp.where(p_in[j] == v0, float(j), i0)
    rest = [jnp.where(i0 == float(j), -1.0, p_in[j]) for j in range(EXPERTS_PER_GROUP)]
    v1 = functools.reduce(jnp.maximum, rest)
    i1 = jnp.full_like(v1, float(EXPERTS_PER_GROUP))
    for j in reversed(range(EXPERTS_PER_GROUP)):
        i1 = jnp.where(rest[j] == v1, float(j), i1)

    tot = v0 + v1
    w0 = p_g * (v0 / tot)
    w1 = p_g * (v1 / tot)
    first = i0 < i1
    a = jnp.where(first, i0, i1)
    b = jnp.where(first, i1, i0)
    w_lo = jnp.where(first, w0, w1)
    w_hi = jnp.where(first, w1, w0)
    pair = jnp.where(a == 0.0, b - 1.0, jnp.where(a == 1.0, jnp.where(b == 3.0, 3.0, 4.0), 5.0))
    swap = a == 2.0
    wa = jnp.where(swap, w_hi, w_lo)
    wb = jnp.where(swap, w_lo, w_hi)
    bucket = g_idx * float(len(PAIRS)) + pair
    return bucket, wa, wb


def _cross_router_kernel(q_ref, k_ref, v_ref, x1_ref, wco_ref, gf_ref, whi_ref, wlo_ref, rb_ref,
                         xa_ref, rt_ref, *, tm):
    q = q_ref[...]
    k = k_ref[0]
    v = v_ref[0]
    outs = []
    for hh in range(N_HEADS):
        sl = slice(hh * CROSS_HEAD_DIM, (hh + 1) * CROSS_HEAD_DIM)
        s = _nt_dot(q[:, sl], k[:, sl])
        p = jnp.exp(s - jnp.max(s, axis=-1, keepdims=True))
        l = jnp.sum(p, axis=-1, keepdims=True)
        o = jnp.dot(p.astype(BF16), v[:, sl], preferred_element_type=F32) / l
        outs.append(o.astype(BF16))
    o = jnp.concatenate(outs, axis=1)
    x2 = x1_ref[...] + jnp.dot(o, wco_ref[...], preferred_element_type=F32)
    xa_ref[:, :D_MODEL] = x2

    h3 = _rms(x2, gf_ref[...])
    hi = h3.astype(BF16)
    lo = (h3 - hi.astype(F32)).astype(BF16)
    lt = _nt_dot(whi_ref[...], hi) + (_nt_dot(whi_ref[...], lo) + _nt_dot(wlo_ref[...], hi))
    lt = lt + rb_ref[...]
    bucket, wa, wb = _route(lt)

    sub = lax.broadcasted_iota(I32, (8, tm), 0)
    rows = jnp.where(sub == 0, bucket, jnp.where(sub == 1, wa, jnp.where(sub == 2, wb, 0.0)))
    rt_ref[0] = rows
    sub_a = lax.broadcasted_iota(I32, (AUX_COLS, tm), 0)
    aux_t = jnp.where(sub_a == 0, bucket, jnp.where(sub_a == 1, wa, jnp.where(sub_a == 2, wb, 0.0)))
    xa_ref[:, D_MODEL:] = aux_t.T


def _cross_router(qc, kc, vc, x1, w_co, g_ffn, wr_hi, wr_lo, rb, *, batch, seq):
    tm = min(CROSS_ROWS, seq)
    ns = seq // tm
    n = batch * seq
    est = 2 * (tm * D_MODEL * 2 + 2 * N_MEM * D_MODEL * 2 + tm * D_MODEL * 4 + D_MODEL * D_MODEL * 2
               + tm * ROW_W * 4) + 10 * tm * D_MODEL * 4
    return pl.pallas_call(
        functools.partial(_cross_router_kernel, tm=tm),
        grid=(batch, ns),
        in_specs=[
            pl.BlockSpec((tm, D_MODEL), lambda b, s: (b * ns + s, 0)),
            pl.BlockSpec((1, N_MEM, D_MODEL), lambda b, s: (b, 0, 0)),
            pl.BlockSpec((1, N_MEM, D_MODEL), lambda b, s: (b, 0, 0)),
            pl.BlockSpec((tm, D_MODEL), lambda b, s: (b * ns + s, 0)),
            pl.BlockSpec((D_MODEL, D_MODEL), lambda b, s: (0, 0)),
            pl.BlockSpec((1, D_MODEL), lambda b, s: (0, 0)),
            pl.BlockSpec((32, D_MODEL), lambda b, s: (0, 0)),
            pl.BlockSpec((32, D_MODEL), lambda b, s: (0, 0)),
            pl.BlockSpec((32, 1), lambda b, s: (0, 0)),
        ],
        out_specs=[pl.BlockSpec((tm, ROW_W), lambda b, s: (b * ns + s, 0)),
                   pl.BlockSpec((1, 8, tm), lambda b, s: (b * ns + s, 0, 0))],
        out_shape=[jax.ShapeDtypeStruct((n, ROW_W), F32),
                   jax.ShapeDtypeStruct((n // tm, 8, tm), F32)],
        compiler_params=pltpu.CompilerParams(
            dimension_semantics=("arbitrary", "arbitrary"), vmem_limit_bytes=_vmem_limit(est)),
        name="cross_router",
    )(qc, kc, vc, x1, w_co, g_ffn, wr_hi, wr_lo, rb)


def _rank_kernel(rt_ref, rank_ref, cnt_ref, carry_sc, *, tm):
    @pl.when(pl.program_id(0) == 0)
    def _():
        carry_sc[...] = jnp.zeros_like(carry_sc)

    bucket = rt_ref[0, 0:1, :]
    sub = lax.broadcasted_iota(I32, (32, tm), 0).astype(F32)
    hit = sub == bucket
    r = lax.broadcasted_iota(I32, (tm, tm), 0)
    c = lax.broadcasted_iota(I32, (tm, tm), 1)
    upper = jnp.where(r <= c, 1.0, 0.0).astype(BF16)
    cum = jnp.dot(jnp.where(hit, 1.0, 0.0).astype(BF16), upper, preferred_element_type=F32)
    carry = carry_sc[:, 0:1]
    rank = jnp.sum(jnp.where(hit, cum - 1.0 + carry, 0.0), axis=0, keepdims=True)
    rank_ref[0] = rank.astype(I32)
    new_carry = carry_sc[...] + jnp.broadcast_to(cum[:, tm - 1:tm], carry_sc.shape)
    carry_sc[...] = new_carry
    cnt_ref[...] = new_carry


def _rank(rt):
    nt, _, tm = rt.shape
    return pl.pallas_call(
        functools.partial(_rank_kernel, tm=tm),
        grid=(nt,),
        in_specs=[pl.BlockSpec((1, 8, tm), lambda i: (i, 0, 0))],
        out_specs=[pl.BlockSpec((1, 1, tm), lambda i: (i, 0, 0)),
                   pl.BlockSpec((32, LANES), lambda i: (0, 0))],
        out_shape=[jax.ShapeDtypeStruct((nt, 1, tm), I32), jax.ShapeDtypeStruct((32, LANES), F32)],
        scratch_shapes=[pltpu.VMEM((32, LANES), F32)],
        compiler_params=pltpu.CompilerParams(dimension_semantics=("arbitrary",)),
        name="rank",
    )(rt)


def _dispatch_kernel(pos_ref, zrow_ref, zflag_ref, nact_ref, xa_ref, xs_ref, zbuf, zsem, sem, *,
                     n_tiles, rows):
    @pl.when(pl.program_id(0) == 0)
    def _():
        _zero_partial_tiles(zrow_ref, zflag_ref, nact_ref, xs_ref, zbuf, zsem, n_tiles=n_tiles)

    base = pl.program_id(0) * rows
    for r in range(rows):
        pltpu.make_async_copy(xa_ref.at[pl.ds(r, 1)], xs_ref.at[pl.ds(pos_ref[base + r], 1)], sem).start(
            priority=r % 2)
    pltpu.make_async_copy(xa_ref, xs_ref.at[pl.ds(0, rows)], sem).wait()


def _zero_partial_tiles(zrow_ref, zflag_ref, nact_ref, xs_ref, zbuf, zsem, *, n_tiles):
    zbuf[...] = jnp.zeros_like(zbuf)

    def zero_copy(row):
        return pltpu.make_async_copy(zbuf, xs_ref.at[pl.ds(pl.multiple_of(row, MOE_ROWS), MOE_ROWS)], zsem)

    def bucket_tiles(op):
        def body(b, carry):
            @pl.when(zflag_ref[b] > 0)
            def _():
                op(zero_copy(zrow_ref[b]))
            return carry
        lax.fori_loop(0, N_PAIR_BUCKETS, body, 0)

    def unused_tiles(op):
        def body(i, carry):
            op(zero_copy(i * MOE_ROWS))
            return carry
        lax.fori_loop(nact_ref[0], n_tiles, body, 0)

    bucket_tiles(lambda cp: cp.start())
    unused_tiles(lambda cp: cp.start())
    bucket_tiles(lambda cp: cp.wait())
    unused_tiles(lambda cp: cp.wait())


def _dispatch(pos, zrow, zflag, n_active, xa, *, n_tiles):
    n_tok = xa.shape[0]
    n_sorted = n_tiles * MOE_ROWS
    rows = min(DMA_ROWS, n_tok)
    return pl.pallas_call(
        functools.partial(_dispatch_kernel, n_tiles=n_tiles, rows=rows),
        grid_spec=pltpu.PrefetchScalarGridSpec(
            num_scalar_prefetch=4,
            grid=(n_tok // rows,),
            in_specs=[pl.BlockSpec((rows, ROW_W), lambda i, *_: (i, 0))],
            out_specs=pl.BlockSpec(memory_space=pl.ANY),
            scratch_shapes=[pltpu.VMEM((MOE_ROWS, ROW_W), F32), pltpu.SemaphoreType.DMA(()),
                            pltpu.SemaphoreType.DMA(())],
        ),
        out_shape=jax.ShapeDtypeStruct((n_sorted, ROW_W), F32),
        compiler_params=pltpu.CompilerParams(dimension_semantics=("arbitrary",), has_side_effects=True),
        name="dispatch",
    )(pos, zrow, zflag, n_active, xa)


def _moe_kernel(ta_ref, tb_ref, xi_ref, nact_ref, xs_ref, gf_ref, wga_ref, wua_ref, wda_ref,
                wgb_ref, wub_ref, wdb_ref, ys_ref):
    i = pl.program_id(0)

    @pl.when(i < nact_ref[0])
    def _():
        x2 = xs_ref[:, :D_MODEL]
        aux = xs_ref[:, D_MODEL:]
        wa = aux[:, 1:2]
        wb = aux[:, 2:3]
        h = _rms(x2, gf_ref[...]).astype(BF16)

        def mlp(wg, wu, wd):
            g = jnp.dot(h, wg[0].astype(BF16), preferred_element_type=F32)
            u = jnp.dot(h, wu[0].astype(BF16), preferred_element_type=F32)
            act = (g / (1.0 + jnp.exp(-g))) * u
            return jnp.dot(act.astype(BF16), wd[0].astype(BF16), preferred_element_type=F32)

        ys_ref[...] = x2 + (wa * mlp(wga_ref, wua_ref, wda_ref) + wb * mlp(wgb_ref, wub_ref, wdb_ref))

    @pl.when(i >= nact_ref[0])
    def _():
        ys_ref[...] = jnp.zeros_like(ys_ref)


def _moe(tile_a, tile_b, tile_x, n_active, xs, g_ffn, w_gate, w_up, w_down):
    n_tiles = xs.shape[0] // MOE_ROWS
    t = MOE_ROWS
    wa_map = lambda i, ta, tb, xi, na: (ta[i], 0, 0)
    wb_map = lambda i, ta, tb, xi, na: (tb[i], 0, 0)
    est = 2 * (t * ROW_W * 4 + 6 * D_MODEL * D_EXPERT * 4 + t * D_MODEL * 4) + 10 * t * D_MODEL * 4
    return pl.pallas_call(
        _moe_kernel,
        grid_spec=pltpu.PrefetchScalarGridSpec(
            num_scalar_prefetch=4,
            grid=(n_tiles,),
            in_specs=[
                pl.BlockSpec((t, ROW_W), lambda i, ta, tb, xi, na: (xi[i], 0)),
                pl.BlockSpec((1, D_MODEL), lambda i, ta, tb, xi, na: (0, 0)),
                pl.BlockSpec((1, D_MODEL, D_EXPERT), wa_map),
                pl.BlockSpec((1, D_MODEL, D_EXPERT), wa_map),
                pl.BlockSpec((1, D_EXPERT, D_MODEL), wa_map),
                pl.BlockSpec((1, D_MODEL, D_EXPERT), wb_map),
                pl.BlockSpec((1, D_MODEL, D_EXPERT), wb_map),
                pl.BlockSpec((1, D_EXPERT, D_MODEL), wb_map),
            ],
            out_specs=pl.BlockSpec((t, D_MODEL), lambda i, ta, tb, xi, na: (i, 0)),
        ),
        out_shape=jax.ShapeDtypeStruct((n_tiles * t, D_MODEL), F32),
        compiler_params=pltpu.CompilerParams(
            dimension_semantics=("arbitrary",), vmem_limit_bytes=_vmem_limit(est)),
        name="moe",
    )(tile_a, tile_b, tile_x, n_active, xs, g_ffn, w_gate, w_up, w_down, w_gate, w_up, w_down)


def _combine_kernel(pos_ref, ys_ref, out_ref, sem, *, rows):
    base = pl.program_id(0) * rows
    for r in range(rows):
        pltpu.make_async_copy(ys_ref.at[pl.ds(pos_ref[base + r], 1)], out_ref.at[pl.ds(r, 1)], sem).start(
            priority=r % 2)
    pltpu.make_async_copy(ys_ref.at[pl.ds(0, rows)], out_ref, sem).wait()


def _combine(pos, ys, *, n_tok):
    rows = min(DMA_ROWS, n_tok)
    return pl.pallas_call(
        functools.partial(_combine_kernel, rows=rows),
        grid_spec=pltpu.PrefetchScalarGridSpec(
            num_scalar_prefetch=1,
            grid=(n_tok // rows,),
            in_specs=[pl.BlockSpec(memory_space=pl.ANY)],
            out_specs=pl.BlockSpec((rows, D_MODEL), lambda i, *_: (i, 0)),
            scratch_shapes=[pltpu.SemaphoreType.DMA(())],
        ),
        out_shape=jax.ShapeDtypeStruct((n_tok, D_MODEL), F32),
        compiler_params=pltpu.CompilerParams(dimension_semantics=("arbitrary",), has_side_effects=True),
        name="combine",
    )(pos, ys)


def _t5_bucket(rel):
    nb = N_BUCKETS_T5 // 2
    max_exact = nb // 2
    ret = (rel > 0).astype(I32) * nb
    n = jnp.abs(rel)
    nf = jnp.maximum(n, 1).astype(F32)
    large = max_exact + (jnp.log(nf / max_exact) / math.log(MAX_DISTANCE / max_exact)
                         * (nb - max_exact)).astype(I32)
    large = jnp.minimum(large, nb - 1)
    return ret + jnp.where(n < max_exact, n, large)


def _diff_bias(rel_bias, t):
    assert t % CHUNK == 0 and t >= MAX_DISTANCE
    n_rel = 3 * t
    vec = rel_bias[_t5_bucket(jnp.arange(n_rel, dtype=I32) - (2 * t - 1))].astype(F32).T
    toe = jnp.tile(vec, (1, t))[:, :t * (n_rel - 1)].reshape(N_HEADS, t, n_rel - 1)
    qpos = jnp.arange(t, dtype=I32)[:, None]
    kpos = jnp.arange(t, dtype=I32)[None, :]
    far = rel_bias[_t5_bucket(jnp.asarray(-2 * t, I32))].astype(F32)[:, None, None]
    b0 = (toe[:, :, 2 * t - 1:3 * t - 1] - far) * LOG2E
    b0 = jnp.where((kpos // CHUNK <= qpos // CHUNK)[None], b0, NEG_INF)
    b1 = (toe[:, :, t - 1:2 * t - 1] - far) * LOG2E
    return jnp.swapaxes(jnp.stack([b0, b1], axis=1), -1, -2)


def _tile_plan(counts, n_tiles):
    t = MOE_ROWS
    tiles = (counts + t - 1) // t
    ends = jnp.cumsum(tiles)
    starts = ends - tiles
    n_active = ends[-1]
    tile_idx = jnp.minimum(jnp.arange(n_tiles, dtype=I32), n_active - 1)
    tile_bucket = jnp.minimum(jnp.sum((ends[None, :] <= tile_idx[:, None]).astype(I32), axis=1), N_PAIR_BUCKETS - 1)
    pa = jnp.asarray([p[0] for p in PAIRS], I32)
    pb = jnp.asarray([p[1] for p in PAIRS], I32)
    grp = tile_bucket // len(PAIRS)
    pair = tile_bucket % len(PAIRS)
    tile_a = grp * EXPERTS_PER_GROUP + pa[pair]
    tile_b = grp * EXPERTS_PER_GROUP + pb[pair]
    row_off = (starts * t).astype(I32)
    zrow = ((ends - 1) * t).astype(I32)
    zflag = (counts > 0).astype(I32)
    return row_off, zrow, zflag, tile_a.astype(I32), tile_b.astype(I32), tile_idx.astype(I32), n_active.astype(I32)


def kernel(x, mem, rel_bias, norm_mix_g, w_in, b_forget, diff_q_norm_g, diff_k_norm_g, diff_lambda_q1, diff_lambda_k1, diff_lambda_q2, diff_lambda_k2, diff_subln_g, fox_q_norm_g, fox_k_norm_g, fox_out_norm_g, w_out, norm_cross_g, norm_mem_g, w_cq, w_ckv, cross_q_norm_g, cross_k_norm_g, w_co, norm_ffn_g, w_group_router, b_group_router, w_expert_router, b_expert_router, w_exp_gate, w_exp_up, w_exp_down):
    batch, seq, d = x.shape
    assert d == D_MODEL and norm_mix_g.shape[0] == 1 and mem.shape[1] == N_MEM
    n_tok = batch * seq
    assert seq % ATTN_TILE == 0 or seq < ATTN_TILE
    l = 0

    w_qk = jnp.concatenate([w_in[l, :, :1024], w_in[l, :, 1536:2560]], axis=1).astype(BF16)
    w_vt = jnp.concatenate([w_in[l, :, 1024:1536], w_in[l, :, 2560:3072]], axis=1).T.astype(BF16)
    wft = jnp.zeros((8, D_MODEL), F32).at[:N_HEADS].set(w_in[l, :, 3072:].T).astype(BF16)
    bf = jnp.zeros((8, 1), F32).at[:N_HEADS, 0].set(b_forget[l].astype(F32))
    ones = jnp.ones((HEAD_DIM,), F32)
    qg = jnp.stack([
        jnp.tile(diff_q_norm_g[l].astype(F32), 2) * (DIFF_QK_DIM ** -0.5 * LOG2E),
        jnp.tile(diff_k_norm_g[l].astype(F32), 2),
        fox_q_norm_g[l].astype(F32) * (HEAD_DIM ** -0.5 * LOG2E), fox_k_norm_g[l].astype(F32),
        ones, ones, ones, ones])
    lam = (jnp.exp(jnp.sum(diff_lambda_q1[l].astype(F32) * diff_lambda_k1[l].astype(F32)))
           - jnp.exp(jnp.sum(diff_lambda_q2[l].astype(F32) * diff_lambda_k2[l].astype(F32)))
           + LAM_INIT).reshape(1)
    t_attn = min(ATTN_TILE, seq)
    bias_tiles_t = _diff_bias(rel_bias, t_attn)
    wr = jnp.zeros((32, D_MODEL), F32)
    wr = wr.at[:N_GROUPS].set(w_group_router[l].T).at[N_GROUPS:N_GROUPS + N_EXPERTS].set(w_expert_router[l].T)
    wr_hi = wr.astype(BF16)
    wr_lo = (wr - wr_hi.astype(F32)).astype(BF16)
    rb = jnp.zeros((32, 1), F32)
    rb = rb.at[:N_GROUPS, 0].set(b_group_router[l]).at[N_GROUPS:N_GROUPS + N_EXPERTS, 0].set(b_expert_router[l])

    row = lambda v: v.astype(F32).reshape(1, -1)
    x2d = x.reshape(n_tok, D_MODEL)

    qk, vt, cum = _mix_proj(x2d, row(norm_mix_g[l]), w_qk, w_vt, wft, bf, qg, batch=batch, seq=seq)
    cum4 = cum[:, :N_HEADS].reshape(batch, N_HEADS, 1, seq)
    mixed = _attn(qk, vt, bias_tiles_t, lam, cum4, row(diff_subln_g[l]), row(fox_out_norm_g[l]),
                  batch=batch, seq=seq)
    x1, qc = _out_q(x2d, mixed, w_out[l].astype(BF16), row(norm_cross_g[l]),
                    w_cq[l].astype(BF16), row(cross_q_norm_g[l]))

    kc, vc = _mem_kv(mem.reshape(batch * N_MEM, D_MODEL), row(norm_mem_g[l]), w_ckv[l].astype(BF16),
                     row(cross_k_norm_g[l]), batch=batch)
    xa, rt = _cross_router(qc, kc, vc, x1, w_co[l].astype(BF16), row(norm_ffn_g[l]), wr_hi, wr_lo, rb,
                           batch=batch, seq=seq)

    rank, cnt = _rank(rt)
    counts = cnt[:N_PAIR_BUCKETS, 0].astype(I32)
    n_tiles = n_tok // MOE_ROWS + N_PAIR_BUCKETS
    row_off, zrow, zflag, tile_a, tile_b, tile_x, n_active = _tile_plan(counts, n_tiles)
    bucket = rt[:, 0, :].reshape(n_tok).astype(I32)
    pos = row_off[bucket] + rank.reshape(n_tok)
    n_active = n_active.reshape(1)
    xs = _dispatch(pos, zrow, zflag, n_active, xa, n_tiles=n_tiles)
    ys = _moe(tile_a, tile_b, tile_x, n_active, xs, row(norm_ffn_g[l]),
              w_exp_gate[l].astype(F32), w_exp_up[l].astype(F32), w_exp_down[l].astype(F32))
    out = _combine(pos, ys, n_tok=n_tok)
    return out.reshape(batch, seq, D_MODEL)
```

```python
import functools
import math

import jax
import jax.numpy as jnp
import numpy as np
from jax import lax
from jax.experimental import pallas as pl
from jax.experimental.pallas import tpu as pltpu

F32 = jnp.float32
BF16 = jnp.bfloat16
I32 = jnp.int32

D_MODEL = 1024
CHUNK = 64
N_MEM = 256
N_HEADS = 4
HEAD_DIM = 128
VT_ROWS = HEAD_DIM + 16
DIFF_QK_DIM = 64
CROSS_HEAD_DIM = 256
N_BUCKETS_T5 = 32
MAX_DISTANCE = 128
N_GROUPS = 4
EXPERTS_PER_GROUP = 4
N_EXPERTS = N_GROUPS * EXPERTS_PER_GROUP
D_EXPERT = 512
EPS = 1e-6
NEG_INF = -1e30
LAM_INIT = 0.8 - 0.6 * math.exp(-0.3 * 0)
LOG2E = math.log2(math.e)

PAIRS = ((0, 1), (0, 2), (0, 3), (1, 3), (1, 2), (3, 2))
N_PAIR_BUCKETS = N_GROUPS * len(PAIRS)

LANES = 128
VMEM_LIMIT_CAP = 56 * 1024 * 1024

PROJ_ROWS = 1024
ATTN_TILE = 256
CROSS_ROWS = 512
MOE_ROWS = 256
AUX_COLS = LANES
ROW_W = D_MODEL + AUX_COLS
DMA_ROWS = 256


def _vmem_limit(nbytes):
    return int(min(max(nbytes * 5 // 4, 32 * 1024 * 1024), VMEM_LIMIT_CAP))


def _nt_dot(a, b):
    return lax.dot_general(a, b, (((1,), (1,)), ((), ())), preferred_element_type=F32)


def _rms(x, g):
    ms = jnp.mean(x * x, axis=-1, keepdims=True)
    return x * lax.rsqrt(ms + EPS) * g


def _mix_proj_kernel(x_ref, g_ref, w_ref, wvt_ref, wft_ref, bf_ref, qg_ref, o_ref, vt_ref, cum_ref, carry_ref,
                     *, tm):
    si = pl.program_id(1)
    h = _rms(x_ref[...], g_ref[...]).astype(BF16)
    lane = lax.broadcasted_iota(I32, (tm, HEAD_DIM), 1)
    lo = lane < DIFF_QK_DIM
    for kind in range(4):
        p = jnp.dot(h, w_ref[:, kind * 512:(kind + 1) * 512], preferred_element_type=F32)
        for hh in range(N_HEADS):
            ph = p[:, hh * HEAD_DIM:(hh + 1) * HEAD_DIM]
            if kind in (0, 1):
                sq = ph * ph
                s_lo = jnp.sum(jnp.where(lo, sq, 0.0), axis=-1, keepdims=True)
                s_hi = jnp.sum(jnp.where(lo, 0.0, sq), axis=-1, keepdims=True)
                ms = jnp.where(lo, s_lo, s_hi) * (1.0 / DIFF_QK_DIM)
                ph = ph * lax.rsqrt(ms + EPS) * qg_ref[kind:kind + 1, :]
            else:
                ph = _rms(ph, qg_ref[kind:kind + 1, :])
            o_ref[0, kind * N_HEADS + hh] = ph.astype(BF16)
    for kind in range(2):
        pt = _nt_dot(wvt_ref[kind * 512:(kind + 1) * 512, :], h)
        for hh in range(N_HEADS):
            vt_ref[0, kind * N_HEADS + hh, :HEAD_DIM, :] = pt[hh * HEAD_DIM:(hh + 1) * HEAD_DIM, :].astype(BF16)
            vt_ref[0, kind * N_HEADS + hh, HEAD_DIM:, :] = jnp.ones((VT_ROWS - HEAD_DIM, tm), BF16)

    z = _nt_dot(wft_ref[...], h) + bf_ref[...]
    logf = jnp.minimum(z, 0.0) - jnp.log(1.0 + jnp.exp(-jnp.abs(z)))
    lane8 = lax.broadcasted_iota(I32, (8, tm), 1)
    c = logf * LOG2E
    k = 1
    while k < tm:
        c = c + jnp.where(lane8 >= k, pltpu.roll(c, k, axis=1), 0.0)
        k *= 2

    @pl.when(si == 0)
    def _():
        carry_ref[...] = jnp.zeros_like(carry_ref)

    c = c + carry_ref[:, 0:1]
    cum_ref[0] = c
    carry_ref[...] = jnp.broadcast_to(c[:, tm - 1:tm], carry_ref.shape)


def _mix_proj(x2d, g, w_qk, w_vt, wft, bf, qg, *, batch, seq):
    tm = min(PROJ_ROWS, seq)
    ns = seq // tm
    est = 2 * (tm * D_MODEL * 4 + D_MODEL * 3072 * 2 + 24 * tm * HEAD_DIM * 2) + 8 * tm * D_MODEL
    return pl.pallas_call(
        functools.partial(_mix_proj_kernel, tm=tm),
        grid=(batch, ns),
        in_specs=[
            pl.BlockSpec((tm, D_MODEL), lambda b, s: (b * ns + s, 0)),
            pl.BlockSpec((1, D_MODEL), lambda b, s: (0, 0)),
            pl.BlockSpec((D_MODEL, 2048), lambda b, s: (0, 0)),
            pl.BlockSpec((1024, D_MODEL), lambda b, s: (0, 0)),
            pl.BlockSpec((8, D_MODEL), lambda b, s: (0, 0)),
            pl.BlockSpec((8, 1), lambda b, s: (0, 0)),
            pl.BlockSpec((8, HEAD_DIM), lambda b, s: (0, 0)),
        ],
        out_specs=[
            pl.BlockSpec((1, 16, tm, HEAD_DIM), lambda b, s: (b, 0, s, 0)),
            pl.BlockSpec((1, 8, VT_ROWS, tm), lambda b, s: (b, 0, 0, s)),
            pl.BlockSpec((1, 8, tm), lambda b, s: (b, 0, s)),
        ],
        out_shape=[
            jax.ShapeDtypeStruct((batch, 16, seq, HEAD_DIM), BF16),
            jax.ShapeDtypeStruct((batch, 8, VT_ROWS, seq), BF16),
            jax.ShapeDtypeStruct((batch, 8, seq), F32),
        ],
        scratch_shapes=[pltpu.VMEM((8, LANES), F32)],
        compiler_params=pltpu.CompilerParams(
            dimension_semantics=("arbitrary", "arbitrary"), vmem_limit_bytes=_vmem_limit(est)),
        name="mix_proj",
    )(x2d, g, w_qk, w_vt, wft, bf, qg)


def _init_softmax(m_sc, acc_sc):
    m_sc[...] = jnp.full(m_sc.shape, NEG_INF, F32)
    acc_sc[...] = jnp.zeros_like(acc_sc)


N_CHAINS = 3 * N_HEADS


def _attn_kernel(lam_ref, dq_ref, dk_ref, dvt_ref, fq_ref, fk_ref, fvt_ref, bias_ref, c_ref,
                 gd_ref, gf_ref, o_ref, m_sc, acc_sc, qm_sc, ccol_sc, s_sc, mt_sc, s2_sc, mt2_sc, *, t, seq):
    qi = pl.program_id(1)
    _init_softmax(m_sc, acc_sc)
    key = lax.broadcasted_iota(I32, (t, t), 0)
    qry = lax.broadcasted_iota(I32, (t, t), 1)

    lane = lax.broadcasted_iota(I32, (t, HEAD_DIM), 1)
    for hh in range(N_HEADS):
        q = dq_ref[0, hh]
        zero = jnp.zeros_like(q)
        qm_sc[2 * hh] = jnp.where(lane < DIFF_QK_DIM, q, zero)
        qm_sc[2 * hh + 1] = jnp.where(lane < DIFF_QK_DIM, zero, q)

    @pl.when(qi == 0)
    def _():
        for hh in range(N_HEADS):
            for j in range(seq // t):
                row = c_ref[0, hh, :, j * t:(j + 1) * t]
                ccol_sc[hh, j * t:(j + 1) * t, :] = jnp.sum(
                    jnp.where(key == qry, jnp.broadcast_to(row, (t, t)), 0.0), axis=1, keepdims=True)

    q_off = pl.multiple_of(qi * t, t)

    bufs = ((s_sc, mt_sc), (s2_sc, mt2_sc))

    def logits(j, where, par, slot):
        off = pl.multiple_of(j * t, t)
        s_buf, mt_buf = bufs[par]
        if slot < 2 * N_HEADS:
            hh = slot // 2
            s_t = _nt_dot(dk_ref[0, hh, pl.ds(off, t), :], qm_sc[slot])
            if where != "far":
                s_t = s_t + bias_ref[hh, 1 if where == "near" else 0]
            s_buf[slot] = s_t
            mt_buf[slot] = jnp.max(s_t, axis=0, keepdims=True)
        else:
            hh = slot - 2 * N_HEADS
            c_k = ccol_sc[hh, pl.ds(off, t), :]
            s_t = _nt_dot(fk_ref[0, hh, pl.ds(off, t), :], fq_ref[0, hh]) - c_k
            if where == "diag":
                s_t = jnp.where(key <= qry, s_t, NEG_INF)
            s_buf[slot] = s_t
            mt_buf[slot] = jnp.max(s_t, axis=0, keepdims=True) + c_ref[0, hh, :, pl.ds(q_off, t)]

    def update(j, par, slot):
        off = pl.multiple_of(j * t, t)
        s_buf, mt_buf = bufs[par]
        vt_ref = dvt_ref if slot < 2 * N_HEADS else fvt_ref
        hh = slot // 2 if slot < 2 * N_HEADS else slot - 2 * N_HEADS
        v_t = vt_ref[0, hh, :, pl.ds(off, t)]
        m_prev = m_sc[slot]
        m_new = jnp.maximum(m_prev, mt_buf[slot])
        alpha = jnp.exp2(m_prev - m_new)
        if slot < 2 * N_HEADS:
            p_t = jnp.exp2(s_buf[slot] - m_new)
        else:
            p_t = jnp.exp2(s_buf[slot] - (m_new - c_ref[0, hh, :, pl.ds(q_off, t)]))
        acc_sc[slot] = alpha * acc_sc[slot] + jnp.dot(v_t, p_t.astype(BF16), preferred_element_type=F32)
        m_sc[slot] = m_new

    def phase(upd=None, nxt=None):
        for slot in range(N_CHAINS):
            if nxt is not None:
                logits(*nxt, slot)
            if upd is not None:
                update(*upd, slot)

    n_far = jnp.maximum(qi - 1, 0)
    far_tile = lambda k: jnp.maximum(qi - 2 - k, 0)

    phase(nxt=(qi, "diag", 0))

    @pl.when(qi == 0)
    def _():
        phase(upd=(qi, 0))

    @pl.when(qi == 1)
    def _():
        phase(upd=(qi, 0), nxt=(qi - 1, "near", 1))
        phase(upd=(qi - 1, 1))

    @pl.when(qi >= 2)
    def _():
        phase(upd=(qi, 0), nxt=(qi - 1, "near", 1))
        phase(upd=(qi - 1, 1), nxt=(far_tile(0), "far", 0))

        def far_pair(i, carry):
            k = 2 * i
            phase(upd=(far_tile(k), 0), nxt=(far_tile(k + 1), "far", 1))

            @pl.when(k + 1 < n_far)
            def _():
                phase(upd=(far_tile(k + 1), 1), nxt=(far_tile(k + 2), "far", 0))
            return carry

        lax.fori_loop(0, (n_far + 1) // 2, far_pair, 0)

    def normalised(slot):
        return acc_sc[slot, :HEAD_DIM, :] / acc_sc[slot, HEAD_DIM:HEAD_DIM + 1, :]

    for hh in range(N_HEADS):
        o_t = normalised(2 * hh) - lam_ref[0] * normalised(2 * hh + 1)
        o_ref[:, hh * HEAD_DIM:(hh + 1) * HEAD_DIM] = (
            _rms(o_t.T, gd_ref[...]) * (1.0 - LAM_INIT)).astype(BF16)
    for hh in range(N_HEADS):
        o_t = normalised(2 * N_HEADS + hh)
        o_ref[:, (N_HEADS + hh) * HEAD_DIM:(N_HEADS + hh + 1) * HEAD_DIM] = _rms(o_t.T, gf_ref[...]).astype(BF16)


def _attn(qk, vt, bias_tiles_t, lam, cum4, subln_g, fox_g, *, batch, seq):
    t = min(ATTN_TILE, seq)
    nq = seq // t
    head_blk = N_HEADS * seq * HEAD_DIM * 2
    est = (2 * (2 * N_HEADS * t * HEAD_DIM * 2 + 4 * head_blk + N_HEADS * 2 * t * t * 4 + N_HEADS * 8 * seq * 4
                + t * 2 * N_HEADS * HEAD_DIM * 2)
           + N_CHAINS * (HEAD_DIM + 16) * t * 4 + 2 * N_HEADS * t * HEAD_DIM * 2 + N_HEADS * seq * LANES * 4
           + (2 * N_CHAINS + 36) * t * t * 4)
    qspec = lambda blk: pl.BlockSpec((1, N_HEADS, t, HEAD_DIM), lambda b, i: (b, blk, i, 0))
    kspec = lambda blk: pl.BlockSpec((1, N_HEADS, seq, HEAD_DIM), lambda b, i: (b, blk, 0, 0))
    vspec = lambda blk: pl.BlockSpec((1, N_HEADS, VT_ROWS, seq), lambda b, i: (b, blk, 0, 0))
    return pl.pallas_call(
        functools.partial(_attn_kernel, t=t, seq=seq),
        grid=(batch, nq),
        in_specs=[
            pl.BlockSpec(memory_space=pltpu.SMEM),
            qspec(0), kspec(1), vspec(0),
            qspec(2), kspec(3), vspec(1),
            pl.BlockSpec((N_HEADS, 2, t, t), lambda b, i: (0, 0, 0, 0)),
            pl.BlockSpec((1, N_HEADS, 1, seq), lambda b, i: (b, 0, 0, 0)),
            pl.BlockSpec((1, HEAD_DIM), lambda b, i: (0, 0)),
            pl.BlockSpec((1, HEAD_DIM), lambda b, i: (0, 0)),
        ],
        out_specs=pl.BlockSpec((t, 2 * N_HEADS * HEAD_DIM), lambda b, i: (b * nq + i, 0)),
        out_shape=jax.ShapeDtypeStruct((batch * seq, 2 * N_HEADS * HEAD_DIM), BF16),
        scratch_shapes=[pltpu.VMEM((N_CHAINS, 1, t), F32),
                        pltpu.VMEM((N_CHAINS, VT_ROWS, t), F32),
                        pltpu.VMEM((2 * N_HEADS, t, HEAD_DIM), BF16),
                        pltpu.VMEM((N_HEADS, seq, 1), F32),
                        pltpu.VMEM((N_CHAINS, t, t), F32), pltpu.VMEM((N_CHAINS, 1, t), F32),
                        pltpu.VMEM((N_CHAINS, t, t), F32), pltpu.VMEM((N_CHAINS, 1, t), F32)],
        compiler_params=pltpu.CompilerParams(
            dimension_semantics=("arbitrary", "arbitrary"), vmem_limit_bytes=_vmem_limit(est)),
        name="attn",
    )(lam, qk, qk, vt, qk, qk, vt, bias_tiles_t, cum4, subln_g, fox_g)


def _out_q_kernel(x_ref, m_ref, wo_ref, gc_ref, wq_ref, qg_ref, x1_ref, qc_ref):
    x1 = x_ref[...] + jnp.dot(m_ref[...], wo_ref[...], preferred_element_type=F32)
    x1_ref[...] = x1
    hc = _rms(x1, gc_ref[...]).astype(BF16)
    q = jnp.dot(hc, wq_ref[...], preferred_element_type=F32)
    for hh in range(N_HEADS):
        sl = slice(hh * CROSS_HEAD_DIM, (hh + 1) * CROSS_HEAD_DIM)
        qh = _rms(q[:, sl], qg_ref[...]) * (CROSS_HEAD_DIM ** -0.5)
        qc_ref[:, sl] = qh.astype(BF16)


def _out_q(x2d, mixed, w_out, g_cross, w_cq, q_g):
    n = x2d.shape[0]
    tm = min(PROJ_ROWS, n)
    est = 2 * (tm * D_MODEL * 4 * 2 + tm * D_MODEL * 2 + 2 * D_MODEL * D_MODEL * 2 + tm * D_MODEL * 2) + 6 * tm * D_MODEL * 4
    return pl.pallas_call(
        _out_q_kernel,
        grid=(n // tm,),
        in_specs=[
            pl.BlockSpec((tm, D_MODEL), lambda i: (i, 0)),
            pl.BlockSpec((tm, D_MODEL), lambda i: (i, 0)),
            pl.BlockSpec((D_MODEL, D_MODEL), lambda i: (0, 0)),
            pl.BlockSpec((1, D_MODEL), lambda i: (0, 0)),
            pl.BlockSpec((D_MODEL, D_MODEL), lambda i: (0, 0)),
            pl.BlockSpec((1, CROSS_HEAD_DIM), lambda i: (0, 0)),
        ],
        out_specs=[pl.BlockSpec((tm, D_MODEL), lambda i: (i, 0)),
                   pl.BlockSpec((tm, D_MODEL), lambda i: (i, 0))],
        out_shape=[jax.ShapeDtypeStruct((n, D_MODEL), F32), jax.ShapeDtypeStruct((n, D_MODEL), BF16)],
        compiler_params=pltpu.CompilerParams(
            dimension_semantics=("arbitrary",), vmem_limit_bytes=_vmem_limit(est)),
        name="out_q",
    )(x2d, mixed, w_out, g_cross, w_cq, q_g)


def _mem_kv_kernel(mem_ref, gm_ref, w_ref, kg_ref, k_ref, v_ref):
    mn = _rms(mem_ref[...], gm_ref[...]).astype(BF16)
    kv = jnp.dot(mn, w_ref[...], preferred_element_type=F32)
    for hh in range(N_HEADS):
        sl = slice(hh * CROSS_HEAD_DIM, (hh + 1) * CROSS_HEAD_DIM)
        k_ref[0, :, sl] = _rms(kv[:, sl], kg_ref[...]).astype(BF16)
    v_ref[0] = kv[:, D_MODEL:].astype(BF16)


def _mem_kv(mem2d, g_mem, w_ckv, k_g, *, batch):
    est = 2 * (N_MEM * D_MODEL * 4 + D_MODEL * 2 * D_MODEL * 2 + 2 * N_MEM * D_MODEL * 2) + 4 * N_MEM * 2 * D_MODEL * 4
    return pl.pallas_call(
        _mem_kv_kernel,
        grid=(batch,),
        in_specs=[
            pl.BlockSpec((N_MEM, D_MODEL), lambda b: (b, 0)),
            pl.BlockSpec((1, D_MODEL), lambda b: (0, 0)),
            pl.BlockSpec((D_MODEL, 2 * D_MODEL), lambda b: (0, 0)),
            pl.BlockSpec((1, CROSS_HEAD_DIM), lambda b: (0, 0)),
        ],
        out_specs=[pl.BlockSpec((1, N_MEM, D_MODEL), lambda b: (b, 0, 0)),
                   pl.BlockSpec((1, N_MEM, D_MODEL), lambda b: (b, 0, 0))],
        out_shape=[jax.ShapeDtypeStruct((batch, N_MEM, D_MODEL), BF16),
                   jax.ShapeDtypeStruct((batch, N_MEM, D_MODEL), BF16)],
        compiler_params=pltpu.CompilerParams(
            dimension_semantics=("arbitrary",), vmem_limit_bytes=_vmem_limit(est)),
        name="mem_kv",
    )(mem2d, g_mem, w_ckv, k_g)


def _route(logits_t):
    gl = logits_t[0:N_GROUPS]
    gmax = jnp.max(gl, axis=0, keepdims=True)
    eg = jnp.exp(gl - gmax)
    p_group = eg / jnp.sum(eg, axis=0, keepdims=True)
    p_g = jnp.max(p_group, axis=0, keepdims=True)
    g_idx = jnp.full_like(p_g, float(N_GROUPS))
    for g in reversed(range(N_GROUPS)):
        g_idx = jnp.where(p_group[g:g + 1] == p_g, float(g), g_idx)

    sel = []
    for j in range(EXPERTS_PER_GROUP):
        acc = jnp.zeros_like(p_g)
        for g in range(N_GROUPS):
            r = N_GROUPS + g * EXPERTS_PER_GROUP + j
            acc = jnp.where(g_idx == float(g), logits_t[r:r + 1], acc)
        sel.append(acc)
    smax = functools.reduce(jnp.maximum, sel)
    es = [jnp.exp(s - smax) for s in sel]
    den = functools.reduce(jnp.add, es)
    p_in = [e / den for e in es]

    v0 = functools.reduce(jnp.maximum, p_in)
    i0 = jnp.full_like(v0, float(EXPERTS_PER_GROUP))
    for j in reversed(range(EXPERTS_PER_GROUP)):
        i0 = jnp.where(p_in[j] == v0, float(j), i0)
    rest = [jnp.where(i0 == float(j), -1.0, p_in[j]) for j in range(EXPERTS_PER_GROUP)]
    v1 = functools.reduce(jnp.maximum, rest)
    i1 = jnp.full_like(v1, float(EXPERTS_PER_GROUP))
    for j in reversed(range(EXPERTS_PER_GROUP)):
        i1 = jnp.where(rest[j] == v1, float(j), i1)

    tot = v0 + v1
    w0 = p_g * (v0 / tot)
    w1 = p_g * (v1 / tot)
    first = i0 < i1
    a = jnp.where(first, i0, i1)
    b = jnp.where(first, i1, i0)
    w_lo = jnp.where(first, w0, w1)
    w_hi = jnp.where(first, w1, w0)
    pair = jnp.where(a == 0.0, b - 1.0, jnp.where(a == 1.0, jnp.where(b == 3.0, 3.0, 4.0), 5.0))
    swap = a == 2.0
    wa = jnp.where(swap, w_hi, w_lo)
    wb = jnp.where(swap, w_lo, w_hi)
    bucket = g_idx * float(len(PAIRS)) + pair
    return bucket, wa, wb


def _cross_router_kernel(q_ref, k_ref, v_ref, x1_ref, wco_ref, gf_ref, whi_ref, wlo_ref, rb_ref,
                         xa_ref, rt_ref, *, tm):
    q = q_ref[...]
    k = k_ref[0]
    v = v_ref[0]
    outs = []
    for hh in range(N_HEADS):
        sl = slice(hh * CROSS_HEAD_DIM, (hh + 1) * CROSS_HEAD_DIM)
        s = _nt_dot(q[:, sl], k[:, sl])
        p = jnp.exp(s - jnp.max(s, axis=-1, keepdims=True))
        l = jnp.sum(p, axis=-1, keepdims=True)
        o = jnp.dot(p.astype(BF16), v[:, sl], preferred_element_type=F32) / l
        outs.append(o.astype(BF16))
    o = jnp.concatenate(outs, axis=1)
    x2 = x1_ref[...] + jnp.dot(o, wco_ref[...], preferred_element_type=F32)
    xa_ref[:, :D_MODEL] = x2

    h3 = _rms(x2, gf_ref[...])
    hi = h3.astype(BF16)
    lo = (h3 - hi.astype(F32)).astype(BF16)
    lt = _nt_dot(whi_ref[...], hi) + (_nt_dot(whi_ref[...], lo) + _nt_dot(wlo_ref[...], hi))
    lt = lt + rb_ref[...]
    bucket, wa, wb = _route(lt)

    sub = lax.broadcasted_iota(I32, (8, tm), 0)
    rows = jnp.where(sub == 0, bucket, jnp.where(sub == 1, wa, jnp.where(sub == 2, wb, 0.0)))
    rt_ref[0] = rows
    sub_a = lax.broadcasted_iota(I32, (AUX_COLS, tm), 0)
    aux_t = jnp.where(sub_a == 0, bucket, jnp.where(sub_a == 1, wa, jnp.where(sub_a == 2, wb, 0.0)))
    xa_ref[:, D_MODEL:] = aux_t.T


def _cross_router(qc, kc, vc, x1, w_co, g_ffn, wr_hi, wr_lo, rb, *, batch, seq):
    tm = min(CROSS_ROWS, seq)
    ns = seq // tm
    n = batch * seq
    est = 2 * (tm * D_MODEL * 2 + 2 * N_MEM * D_MODEL * 2 + tm * D_MODEL * 4 + D_MODEL * D_MODEL * 2
               + tm * ROW_W * 4) + 10 * tm * D_MODEL * 4
    return pl.pallas_call(
        functools.partial(_cross_router_kernel, tm=tm),
        grid=(batch, ns),
        in_specs=[
            pl.BlockSpec((tm, D_MODEL), lambda b, s: (b * ns + s, 0)),
            pl.BlockSpec((1, N_MEM, D_MODEL), lambda b, s: (b, 0, 0)),
            pl.BlockSpec((1, N_MEM, D_MODEL), lambda b, s: (b, 0, 0)),
            pl.BlockSpec((tm, D_MODEL), lambda b, s: (b * ns + s, 0)),
            pl.BlockSpec((D_MODEL, D_MODEL), lambda b, s: (0, 0)),
            pl.BlockSpec((1, D_MODEL), lambda b, s: (0, 0)),
            pl.BlockSpec((32, D_MODEL), lambda b, s: (0, 0)),
            pl.BlockSpec((32, D_MODEL), lambda b, s: (0, 0)),
            pl.BlockSpec((32, 1), lambda b, s: (0, 0)),
        ],
        out_specs=[pl.BlockSpec((tm, ROW_W), lambda b, s: (b * ns + s, 0)),
                   pl.BlockSpec((1, 8, tm), lambda b, s: (b * ns + s, 0, 0))],
        out_shape=[jax.ShapeDtypeStruct((n, ROW_W), F32),
                   jax.ShapeDtypeStruct((n // tm, 8, tm), F32)],
        compiler_params=pltpu.CompilerParams(
            dimension_semantics=("arbitrary", "arbitrary"), vmem_limit_bytes=_vmem_limit(est)),
        name="cross_router",
    )(qc, kc, vc, x1, w_co, g_ffn, wr_hi, wr_lo, rb)


def _rank_kernel(rt_ref, rank_ref, cnt_ref, carry_sc, *, tm):
    @pl.when(pl.program_id(0) == 0)
    def _():
        carry_sc[...] = jnp.zeros_like(carry_sc)

    bucket = rt_ref[0, 0:1, :]
    sub = lax.broadcasted_iota(I32, (32, tm), 0).astype(F32)
    hit = sub == bucket
    r = lax.broadcasted_iota(I32, (tm, tm), 0)
    c = lax.broadcasted_iota(I32, (tm, tm), 1)
    upper = jnp.where(r <= c, 1.0, 0.0).astype(BF16)
    cum = jnp.dot(jnp.where(hit, 1.0, 0.0).astype(BF16), upper, preferred_element_type=F32)
    carry = carry_sc[:, 0:1]
    rank = jnp.sum(jnp.where(hit, cum - 1.0 + carry, 0.0), axis=0, keepdims=True)
    rank_ref[0] = rank.astype(I32)
    new_carry = carry_sc[...] + jnp.broadcast_to(cum[:, tm - 1:tm], carry_sc.shape)
    carry_sc[...] = new_carry
    cnt_ref[...] = new_carry


def _rank(rt):
    nt, _, tm = rt.shape
    return pl.pallas_call(
        functools.partial(_rank_kernel, tm=tm),
        grid=(nt,),
        in_specs=[pl.BlockSpec((1, 8, tm), lambda i: (i, 0, 0))],
        out_specs=[pl.BlockSpec((1, 1, tm), lambda i: (i, 0, 0)),
                   pl.BlockSpec((32, LANES), lambda i: (0, 0))],
        out_shape=[jax.ShapeDtypeStruct((nt, 1, tm), I32), jax.ShapeDtypeStruct((32, LANES), F32)],
        scratch_shapes=[pltpu.VMEM((32, LANES), F32)],
        compiler_params=pltpu.CompilerParams(dimension_semantics=("arbitrary",)),
        name="rank",
    )(rt)


def _dispatch_kernel(pos_ref, zrow_ref, zflag_ref, nact_ref, xa_ref, xs_ref, zbuf, zsem, sem, *,
                     n_tiles, rows):
    @pl.when(pl.program_id(0) == 0)
    def _():
        _zero_partial_tiles(zrow_ref, zflag_ref, nact_ref, xs_ref, zbuf, zsem, n_tiles=n_tiles)

    base = pl.program_id(0) * rows
    for r in range(rows):
        pltpu.make_async_copy(xa_ref.at[pl.ds(r, 1)], xs_ref.at[pl.ds(pos_ref[base + r], 1)], sem).start(
            priority=r % 2)
    pltpu.make_async_copy(xa_ref, xs_ref.at[pl.ds(0, rows)], sem).wait()


def _zero_partial_tiles(zrow_ref, zflag_ref, nact_ref, xs_ref, zbuf, zsem, *, n_tiles):
    zbuf[...] = jnp.zeros_like(zbuf)

    def zero_copy(row):
        return pltpu.make_async_copy(zbuf, xs_ref.at[pl.ds(pl.multiple_of(row, MOE_ROWS), MOE_ROWS)], zsem)

    def bucket_tiles(op):
        def body(b, carry):
            @pl.when(zflag_ref[b] > 0)
            def _():
                op(zero_copy(zrow_ref[b]))
            return carry
        lax.fori_loop(0, N_PAIR_BUCKETS, body, 0)

    def unused_tiles(op):
        def body(i, carry):
            op(zero_copy(i * MOE_ROWS))
            return carry
        lax.fori_loop(nact_ref[0], n_tiles, body, 0)

    bucket_tiles(lambda cp: cp.start())
    unused_tiles(lambda cp: cp.start())
    bucket_tiles(lambda cp: cp.wait())
    unused_tiles(lambda cp: cp.wait())


def _dispatch(pos, zrow, zflag, n_active, xa, *, n_tiles):
    n_tok = xa.shape[0]
    n_sorted = n_tiles * MOE_ROWS
    rows = min(DMA_ROWS, n_tok)
    return pl.pallas_call(
        functools.partial(_dispatch_kernel, n_tiles=n_tiles, rows=rows),
        grid_spec=pltpu.PrefetchScalarGridSpec(
            num_scalar_prefetch=4,
            grid=(n_tok // rows,),
            in_specs=[pl.BlockSpec((rows, ROW_W), lambda i, *_: (i, 0))],
            out_specs=pl.BlockSpec(memory_space=pl.ANY),
            scratch_shapes=[pltpu.VMEM((MOE_ROWS, ROW_W), F32), pltpu.SemaphoreType.DMA(()),
                            pltpu.SemaphoreType.DMA(())],
        ),
        out_shape=jax.ShapeDtypeStruct((n_sorted, ROW_W), F32),
        compiler_params=pltpu.CompilerParams(dimension_semantics=("arbitrary",), has_side_effects=True),
        name="dispatch",
    )(pos, zrow, zflag, n_active, xa)


def _moe_kernel(ta_ref, tb_ref, xi_ref, nact_ref, xs_ref, gf_ref, wga_ref, wua_ref, wda_ref,
                wgb_ref, wub_ref, wdb_ref, ys_ref):
    i = pl.program_id(0)

    @pl.when(i < nact_ref[0])
    def _():
        x2 = xs_ref[:, :D_MODEL]
        aux = xs_ref[:, D_MODEL:]
        wa = aux[:, 1:2]
        wb = aux[:, 2:3]
        h = _rms(x2, gf_ref[...]).astype(BF16)

        def mlp(wg, wu, wd):
            g = jnp.dot(h, wg[0].astype(BF16), preferred_element_type=F32)
            u = jnp.dot(h, wu[0].astype(BF16), preferred_element_type=F32)
            act = (g / (1.0 + jnp.exp(-g))) * u
            return jnp.dot(act.astype(BF16), wd[0].astype(BF16), preferred_element_type=F32)

        ys_ref[...] = x2 + (wa * mlp(wga_ref, wua_ref, wda_ref) + wb * mlp(wgb_ref, wub_ref, wdb_ref))

    @pl.when(i >= nact_ref[0])
    def _():
        ys_ref[...] = jnp.zeros_like(ys_ref)


def _moe(tile_a, tile_b, tile_x, n_active, xs, g_ffn, w_gate, w_up, w_down):
    n_tiles = xs.shape[0] // MOE_ROWS
    t = MOE_ROWS
    wa_map = lambda i, ta, tb, xi, na: (ta[i], 0, 0)
    wb_map = lambda i, ta, tb, xi, na: (tb[i], 0, 0)
    est = 2 * (t * ROW_W * 4 + 6 * D_MODEL * D_EXPERT * 4 + t * D_MODEL * 4) + 10 * t * D_MODEL * 4
    return pl.pallas_call(
        _moe_kernel,
        grid_spec=pltpu.PrefetchScalarGridSpec(
            num_scalar_prefetch=4,
            grid=(n_tiles,),
            in_specs=[
                pl.BlockSpec((t, ROW_W), lambda i, ta, tb, xi, na: (xi[i], 0)),
                pl.BlockSpec((1, D_MODEL), lambda i, ta, tb, xi, na: (0, 0)),
                pl.BlockSpec((1, D_MODEL, D_EXPERT), wa_map),
                pl.BlockSpec((1, D_MODEL, D_EXPERT), wa_map),
                pl.BlockSpec((1, D_EXPERT, D_MODEL), wa_map),
                pl.BlockSpec((1, D_MODEL, D_EXPERT), wb_map),
                pl.BlockSpec((1, D_MODEL, D_EXPERT), wb_map),
                pl.BlockSpec((1, D_EXPERT, D_MODEL), wb_map),
            ],
            out_specs=pl.BlockSpec((t, D_MODEL), lambda i, ta, tb, xi, na: (i, 0)),
        ),
        out_shape=jax.ShapeDtypeStruct((n_tiles * t, D_MODEL), F32),
        compiler_params=pltpu.CompilerParams(
            dimension_semantics=("arbitrary",), vmem_limit_bytes=_vmem_limit(est)),
        name="moe",
    )(tile_a, tile_b, tile_x, n_active, xs, g_ffn, w_gate, w_up, w_down, w_gate, w_up, w_down)


def _combine_kernel(pos_ref, ys_ref, out_ref, sem, *, rows):
    base = pl.program_id(0) * rows
    for r in range(rows):
        pltpu.make_async_copy(ys_ref.at[pl.ds(pos_ref[base + r], 1)], out_ref.at[pl.ds(r, 1)], sem).start(
            priority=r % 2)
    pltpu.make_async_copy(ys_ref.at[pl.ds(0, rows)], out_ref, sem).wait()


def _combine(pos, ys, *, n_tok):
    rows = min(DMA_ROWS, n_tok)
    return pl.pallas_call(
        functools.partial(_combine_kernel, rows=rows),
        grid_spec=pltpu.PrefetchScalarGridSpec(
            num_scalar_prefetch=1,
            grid=(n_tok // rows,),
            in_specs=[pl.BlockSpec(memory_space=pl.ANY)],
            out_specs=pl.BlockSpec((rows, D_MODEL), lambda i, *_: (i, 0)),
            scratch_shapes=[pltpu.SemaphoreType.DMA(())],
        ),
        out_shape=jax.ShapeDtypeStruct((n_tok, D_MODEL), F32),
        compiler_params=pltpu.CompilerParams(dimension_semantics=("arbitrary",), has_side_effects=True),
        name="combine",
    )(pos, ys)


def _t5_bucket(rel):
    nb = N_BUCKETS_T5 // 2
    max_exact = nb // 2
    ret = (rel > 0).astype(I32) * nb
    n = jnp.abs(rel)
    nf = jnp.maximum(n, 1).astype(F32)
    large = max_exact + (jnp.log(nf / max_exact) / math.log(MAX_DISTANCE / max_exact)
                         * (nb - max_exact)).astype(I32)
    large = jnp.minimum(large, nb - 1)
    return ret + jnp.where(n < max_exact, n, large)


def _diff_bias(rel_bias, t):
    assert t % CHUNK == 0 and t >= MAX_DISTANCE
    n_rel = 3 * t
    vec = rel_bias[_t5_bucket(jnp.arange(n_rel, dtype=I32) - (2 * t - 1))].astype(F32).T
    toe = jnp.tile(vec, (1, t))[:, :t * (n_rel - 1)].reshape(N_HEADS, t, n_rel - 1)
    qpos = jnp.arange(t, dtype=I32)[:, None]
    kpos = jnp.arange(t, dtype=I32)[None, :]
    far = rel_bias[_t5_bucket(jnp.asarray(-2 * t, I32))].astype(F32)[:, None, None]
    b0 = (toe[:, :, 2 * t - 1:3 * t - 1] - far) * LOG2E
    b0 = jnp.where((kpos // CHUNK <= qpos // CHUNK)[None], b0, NEG_INF)
    b1 = (toe[:, :, t - 1:2 * t - 1] - far) * LOG2E
    return jnp.swapaxes(jnp.stack([b0, b1], axis=1), -1, -2)


def _tile_plan(counts, n_tiles):
    t = MOE_ROWS
    tiles = (counts + t - 1) // t
    ends = jnp.cumsum(tiles)
    starts = ends - tiles
    n_active = ends[-1]
    tile_idx = jnp.minimum(jnp.arange(n_tiles, dtype=I32), n_active - 1)
    tile_bucket = jnp.minimum(jnp.sum((ends[None, :] <= tile_idx[:, None]).astype(I32), axis=1), N_PAIR_BUCKETS - 1)
    pa = jnp.asarray([p[0] for p in PAIRS], I32)
    pb = jnp.asarray([p[1] for p in PAIRS], I32)
    grp = tile_bucket // len(PAIRS)
    pair = tile_bucket % len(PAIRS)
    tile_a = grp * EXPERTS_PER_GROUP + pa[pair]
    tile_b = grp * EXPERTS_PER_GROUP + pb[pair]
    row_off = (starts * t).astype(I32)
    zrow = ((ends - 1) * t).astype(I32)
    zflag = (counts > 0).astype(I32)
    return row_off, zrow, zflag, tile_a.astype(I32), tile_b.astype(I32), tile_idx.astype(I32), n_active.astype(I32)


def kernel(x, mem, rel_bias, norm_mix_g, w_in, b_forget, diff_q_norm_g, diff_k_norm_g, diff_lambda_q1, diff_lambda_k1, diff_lambda_q2, diff_lambda_k2, diff_subln_g, fox_q_norm_g, fox_k_norm_g, fox_out_norm_g, w_out, norm_cross_g, norm_mem_g, w_cq, w_ckv, cross_q_norm_g, cross_k_norm_g, w_co, norm_ffn_g, w_group_router, b_group_router, w_expert_router, b_expert_router, w_exp_gate, w_exp_up, w_exp_down):
    batch, seq, d = x.shape
    assert d == D_MODEL and norm_mix_g.shape[0] == 1 and mem.shape[1] == N_MEM
    n_tok = batch * seq
    assert seq % ATTN_TILE == 0 or seq < ATTN_TILE
    l = 0

    w_qk = jnp.concatenate([w_in[l, :, :1024], w_in[l, :, 1536:2560]], axis=1).astype(BF16)
    w_vt = jnp.concatenate([w_in[l, :, 1024:1536], w_in[l, :, 2560:3072]], axis=1).T.astype(BF16)
    wft = jnp.zeros((8, D_MODEL), F32).at[:N_HEADS].set(w_in[l, :, 3072:].T).astype(BF16)
    bf = jnp.zeros((8, 1), F32).at[:N_HEADS, 0].set(b_forget[l].astype(F32))
    ones = jnp.ones((HEAD_DIM,), F32)
    qg = jnp.stack([
        jnp.tile(diff_q_norm_g[l].astype(F32), 2) * (DIFF_QK_DIM ** -0.5 * LOG2E),
        jnp.tile(diff_k_norm_g[l].astype(F32), 2),
        fox_q_norm_g[l].astype(F32) * (HEAD_DIM ** -0.5 * LOG2E), fox_k_norm_g[l].astype(F32),
        ones, ones, ones, ones])
    lam = (jnp.exp(jnp.sum(diff_lambda_q1[l].astype(F32) * diff_lambda_k1[l].astype(F32)))
           - jnp.exp(jnp.sum(diff_lambda_q2[l].astype(F32) * diff_lambda_k2[l].astype(F32)))
           + LAM_INIT).reshape(1)
    t_attn = min(ATTN_TILE, seq)
    bias_tiles_t = _diff_bias(rel_bias, t_attn)
    wr = jnp.zeros((32, D_MODEL), F32)
    wr = wr.at[:N_GROUPS].set(w_group_router[l].T).at[N_GROUPS:N_GROUPS + N_EXPERTS].set(w_expert_router[l].T)
    wr_hi = wr.astype(BF16)
    wr_lo = (wr - wr_hi.astype(F32)).astype(BF16)
    rb = jnp.zeros((32, 1), F32)
    rb = rb.at[:N_GROUPS, 0].set(b_group_router[l]).at[N_GROUPS:N_GROUPS + N_EXPERTS, 0].set(b_expert_router[l])

    row = lambda v: v.astype(F32).reshape(1, -1)
    x2d = x.reshape(n_tok, D_MODEL)

    qk, vt, cum = _mix_proj(x2d, row(norm_mix_g[l]), w_qk, w_vt, wft, bf, qg, batch=batch, seq=seq)
    cum4 = cum[:, :N_HEADS].reshape(batch, N_HEADS, 1, seq)
    mixed = _attn(qk, vt, bias_tiles_t, lam, cum4, row(diff_subln_g[l]), row(fox_out_norm_g[l]),
                  batch=batch, seq=seq)
    x1, qc = _out_q(x2d, mixed, w_out[l].astype(BF16), row(norm_cross_g[l]),
                    w_cq[l].astype(BF16), row(cross_q_norm_g[l]))

    kc, vc = _mem_kv(mem.reshape(batch * N_MEM, D_MODEL), row(norm_mem_g[l]), w_ckv[l].astype(BF16),
                     row(cross_k_norm_g[l]), batch=batch)
    xa, rt = _cross_router(qc, kc, vc, x1, w_co[l].astype(BF16), row(norm_ffn_g[l]), wr_hi, wr_lo, rb,
                           batch=batch, seq=seq)

    rank, cnt = _rank(rt)
    counts = cnt[:N_PAIR_BUCKETS, 0].astype(I32)
    n_tiles = n_tok // MOE_ROWS + N_PAIR_BUCKETS
    row_off, zrow, zflag, tile_a, tile_b, tile_x, n_active = _tile_plan(counts, n_tiles)
    bucket = rt[:, 0, :].reshape(n_tok).astype(I32)
    pos = row_off[bucket] + rank.reshape(n_tok)
    n_active = n_active.reshape(1)
    xs = _dispatch(pos, zrow, zflag, n_active, xa, n_tiles=n_tiles)
    ys = _moe(tile_a, tile_b, tile_x, n_active, xs, row(norm_ffn_g[l]),
              w_exp_gate[l].astype(F32), w_exp_up[l].astype(F32), w_exp_down[l].astype(F32))
    out = _combine(pos, ys, n_tok=n_tok)
    return out.reshape(batch, seq, D_MODEL)
```

```python
import functools
import math

import jax
import jax.numpy as jnp
import numpy as np
from jax import lax
from jax.experimental import pallas as pl
from jax.experimental.pallas import tpu as pltpu

F32 = jnp.float32
BF16 = jnp.bfloat16
I32 = jnp.int32

D_MODEL = 1024
CHUNK = 64
N_MEM = 256
N_HEADS = 4
HEAD_DIM = 128
VT_ROWS = HEAD_DIM + 16
DIFF_QK_DIM = 64
CROSS_HEAD_DIM = 256
N_BUCKETS_T5 = 32
MAX_DISTANCE = 128
N_GROUPS = 4
EXPERTS_PER_GROUP = 4
N_EXPERTS = N_GROUPS * EXPERTS_PER_GROUP
D_EXPERT = 512
EPS = 1e-6
NEG_INF = -1e30
LAM_INIT = 0.8 - 0.6 * math.exp(-0.3 * 0)
LOG2E = math.log2(math.e)

PAIRS = ((0, 1), (0, 2), (0, 3), (1, 3), (1, 2), (3, 2))
N_PAIR_BUCKETS = N_GROUPS * len(PAIRS)

LANES = 128
VMEM_LIMIT_CAP = 56 * 1024 * 1024

PROJ_ROWS = 1024
ATTN_TILE = 256
CROSS_ROWS = 512
MOE_ROWS = 256
AUX_COLS = LANES
ROW_W = D_MODEL + AUX_COLS
DMA_ROWS = 512


def _vmem_limit(nbytes):
    return int(min(max(nbytes * 5 // 4, 32 * 1024 * 1024), VMEM_LIMIT_CAP))


def _nt_dot(a, b):
    return lax.dot_general(a, b, (((1,), (1,)), ((), ())), preferred_element_type=F32)


def _rms(x, g):
    ms = jnp.mean(x * x, axis=-1, keepdims=True)
    return x * lax.rsqrt(ms + EPS) * g


def _mix_proj_kernel(x_ref, g_ref, w_ref, wvt_ref, wft_ref, bf_ref, qg_ref, o_ref, vt_ref, cum_ref, carry_ref,
                     *, tm):
    si = pl.program_id(1)
    h = _rms(x_ref[...], g_ref[...]).astype(BF16)
    lane = lax.broadcasted_iota(I32, (tm, HEAD_DIM), 1)
    lo = lane < DIFF_QK_DIM
    for kind in range(4):
        p = jnp.dot(h, w_ref[:, kind * 512:(kind + 1) * 512], preferred_element_type=F32)
        for hh in range(N_HEADS):
            ph = p[:, hh * HEAD_DIM:(hh + 1) * HEAD_DIM]
            if kind in (0, 1):
                sq = ph * ph
                s_lo = jnp.sum(jnp.where(lo, sq, 0.0), axis=-1, keepdims=True)
                s_hi = jnp.sum(jnp.where(lo, 0.0, sq), axis=-1, keepdims=True)
                ms = jnp.where(lo, s_lo, s_hi) * (1.0 / DIFF_QK_DIM)
                ph = ph * lax.rsqrt(ms + EPS) * qg_ref[kind:kind + 1, :]
            else:
                ph = _rms(ph, qg_ref[kind:kind + 1, :])
            o_ref[0, kind * N_HEADS + hh] = ph.astype(BF16)
    for kind in range(2):
        pt = _nt_dot(wvt_ref[kind * 512:(kind + 1) * 512, :], h)
        for hh in range(N_HEADS):
            vt_ref[0, kind * N_HEADS + hh, :HEAD_DIM, :] = pt[hh * HEAD_DIM:(hh + 1) * HEAD_DIM, :].astype(BF16)
            vt_ref[0, kind * N_HEADS + hh, HEAD_DIM:, :] = jnp.ones((VT_ROWS - HEAD_DIM, tm), BF16)

    z = _nt_dot(wft_ref[...], h) + bf_ref[...]
    logf = jnp.minimum(z, 0.0) - jnp.log(1.0 + jnp.exp(-jnp.abs(z)))
    lane8 = lax.broadcasted_iota(I32, (8, tm), 1)
    c = logf * LOG2E
    k = 1
    while k < tm:
        c = c + jnp.where(lane8 >= k, pltpu.roll(c, k, axis=1), 0.0)
        k *= 2

    @pl.when(si == 0)
    def _():
        carry_ref[...] = jnp.zeros_like(carry_ref)

    c = c + carry_ref[:, 0:1]
    cum_ref[0] = c
    carry_ref[...] = jnp.broadcast_to(c[:, tm - 1:tm], carry_ref.shape)


def _mix_proj(x2d, g, w_qk, w_vt, wft, bf, qg, *, batch, seq):
    tm = min(PROJ_ROWS, seq)
    ns = seq // tm
    est = 2 * (tm * D_MODEL * 4 + D_MODEL * 3072 * 2 + 24 * tm * HEAD_DIM * 2) + 8 * tm * D_MODEL
    return pl.pallas_call(
        functools.partial(_mix_proj_kernel, tm=tm),
        grid=(batch, ns),
        in_specs=[
            pl.BlockSpec((tm, D_MODEL), lambda b, s: (b * ns + s, 0)),
            pl.BlockSpec((1, D_MODEL), lambda b, s: (0, 0)),
            pl.BlockSpec((D_MODEL, 2048), lambda b, s: (0, 0)),
            pl.BlockSpec((1024, D_MODEL), lambda b, s: (0, 0)),
            pl.BlockSpec((8, D_MODEL), lambda b, s: (0, 0)),
            pl.BlockSpec((8, 1), lambda b, s: (0, 0)),
            pl.BlockSpec((8, HEAD_DIM), lambda b, s: (0, 0)),
        ],
        out_specs=[
            pl.BlockSpec((1, 16, tm, HEAD_DIM), lambda b, s: (b, 0, s, 0)),
            pl.BlockSpec((1, 8, VT_ROWS, tm), lambda b, s: (b, 0, 0, s)),
            pl.BlockSpec((1, 8, tm), lambda b, s: (b, 0, s)),
        ],
        out_shape=[
            jax.ShapeDtypeStruct((batch, 16, seq, HEAD_DIM), BF16),
            jax.ShapeDtypeStruct((batch, 8, VT_ROWS, seq), BF16),
            jax.ShapeDtypeStruct((batch, 8, seq), F32),
        ],
        scratch_shapes=[pltpu.VMEM((8, LANES), F32)],
        compiler_params=pltpu.CompilerParams(
            dimension_semantics=("arbitrary", "arbitrary"), vmem_limit_bytes=_vmem_limit(est)),
        name="mix_proj",
    )(x2d, g, w_qk, w_vt, wft, bf, qg)


def _init_softmax(m_sc, acc_sc):
    m_sc[...] = jnp.full(m_sc.shape, NEG_INF, F32)
    acc_sc[...] = jnp.zeros_like(acc_sc)


N_CHAINS = 3 * N_HEADS


def _attn_kernel(lam_ref, dq_ref, dk_ref, dvt_ref, fq_ref, fk_ref, fvt_ref, bias_ref, c_ref,
                 gd_ref, gf_ref, o_ref, m_sc, acc_sc, qm_sc, ccol_sc, s_sc, mt_sc, s2_sc, mt2_sc, *, t, seq):
    qi = pl.program_id(1)
    _init_softmax(m_sc, acc_sc)
    key = lax.broadcasted_iota(I32, (t, t), 0)
    qry = lax.broadcasted_iota(I32, (t, t), 1)

    lane = lax.broadcasted_iota(I32, (t, HEAD_DIM), 1)
    for hh in range(N_HEADS):
        q = dq_ref[0, hh]
        zero = jnp.zeros_like(q)
        qm_sc[2 * hh] = jnp.where(lane < DIFF_QK_DIM, q, zero)
        qm_sc[2 * hh + 1] = jnp.where(lane < DIFF_QK_DIM, zero, q)

    @pl.when(qi == 0)
    def _():
        for hh in range(N_HEADS):
            for j in range(seq // t):
                row = c_ref[0, hh, :, j * t:(j + 1) * t]
                ccol_sc[hh, j * t:(j + 1) * t, :] = jnp.sum(
                    jnp.where(key == qry, jnp.broadcast_to(row, (t, t)), 0.0), axis=1, keepdims=True)

    q_off = pl.multiple_of(qi * t, t)

    bufs = ((s_sc, mt_sc), (s2_sc, mt2_sc))

    def logits(j, where, par, slot):
        off = pl.multiple_of(j * t, t)
        s_buf, mt_buf = bufs[par]
        if slot < 2 * N_HEADS:
            hh = slot // 2
            s_t = _nt_dot(dk_ref[0, hh, pl.ds(off, t), :], qm_sc[slot])
            if where != "far":
                s_t = s_t + bias_ref[hh, 1 if where == "near" else 0]
            s_buf[slot] = s_t
            mt_buf[slot] = jnp.max(s_t, axis=0, keepdims=True)
        else:
            hh = slot - 2 * N_HEADS
            c_k = ccol_sc[hh, pl.ds(off, t), :]
            s_t = _nt_dot(fk_ref[0, hh, pl.ds(off, t), :], fq_ref[0, hh]) - c_k
            if where == "diag":
                s_t = jnp.where(key <= qry, s_t, NEG_INF)
            s_buf[slot] = s_t
            mt_buf[slot] = jnp.max(s_t, axis=0, keepdims=True) + c_ref[0, hh, :, pl.ds(q_off, t)]

    def update(j, par, slot):
        off = pl.multiple_of(j * t, t)
        s_buf, mt_buf = bufs[par]
        vt_ref = dvt_ref if slot < 2 * N_HEADS else fvt_ref
        hh = slot // 2 if slot < 2 * N_HEADS else slot - 2 * N_HEADS
        v_t = vt_ref[0, hh, :, pl.ds(off, t)]
        m_prev = m_sc[slot]
        m_new = jnp.maximum(m_prev, mt_buf[slot])
        alpha = jnp.exp2(m_prev - m_new)
        if slot < 2 * N_HEADS:
            p_t = jnp.exp2(s_buf[slot] - m_new)
        else:
            p_t = jnp.exp2(s_buf[slot] - (m_new - c_ref[0, hh, :, pl.ds(q_off, t)]))
        acc_sc[slot] = alpha * acc_sc[slot] + jnp.dot(v_t, p_t.astype(BF16), preferred_element_type=F32)
        m_sc[slot] = m_new

    def phase(upd=None, nxt=None):
        for slot in range(N_CHAINS):
            if nxt is not None:
                logits(*nxt, slot)
            if upd is not None:
                update(*upd, slot)

    n_far = jnp.maximum(qi - 1, 0)
    far_tile = lambda k: jnp.maximum(qi - 2 - k, 0)

    phase(nxt=(qi, "diag", 0))

    @pl.when(qi == 0)
    def _():
        phase(upd=(qi, 0))

    @pl.when(qi == 1)
    def _():
        phase(upd=(qi, 0), nxt=(qi - 1, "near", 1))
        phase(upd=(qi - 1, 1))

    @pl.when(qi >= 2)
    def _():
        phase(upd=(qi, 0), nxt=(qi - 1, "near", 1))
        phase(upd=(qi - 1, 1), nxt=(far_tile(0), "far", 0))

        def far_pair(i, carry):
            k = 2 * i
            phase(upd=(far_tile(k), 0), nxt=(far_tile(k + 1), "far", 1))

            @pl.when(k + 1 < n_far)
            def _():
                phase(upd=(far_tile(k + 1), 1), nxt=(far_tile(k + 2), "far", 0))
            return carry

        lax.fori_loop(0, (n_far + 1) // 2, far_pair, 0)

    def normalised(slot):
        return acc_sc[slot, :HEAD_DIM, :] / acc_sc[slot, HEAD_DIM:HEAD_DIM + 1, :]

    def rms_t(o_t, g_t):
        ms = jnp.mean(o_t * o_t, axis=0, keepdims=True)
        return o_t * lax.rsqrt(ms + EPS) * g_t

    for hh in range(N_HEADS):
        o_t = normalised(2 * hh) - lam_ref[0] * normalised(2 * hh + 1)
        o_ref[:, hh * HEAD_DIM:(hh + 1) * HEAD_DIM] = (
            rms_t(o_t, gd_ref[...]) * (1.0 - LAM_INIT)).T.astype(BF16)
    for hh in range(N_HEADS):
        o_t = normalised(2 * N_HEADS + hh)
        o_ref[:, (N_HEADS + hh) * HEAD_DIM:(N_HEADS + hh + 1) * HEAD_DIM] = rms_t(o_t, gf_ref[...]).T.astype(BF16)


def _attn(qk, vt, bias_tiles_t, lam, cum4, subln_g, fox_g, *, batch, seq):
    t = min(ATTN_TILE, seq)
    nq = seq // t
    head_blk = N_HEADS * seq * HEAD_DIM * 2
    est = (2 * (2 * N_HEADS * t * HEAD_DIM * 2 + 4 * head_blk + N_HEADS * 2 * t * t * 4 + N_HEADS * 8 * seq * 4
                + t * 2 * N_HEADS * HEAD_DIM * 2)
           + N_CHAINS * (HEAD_DIM + 16) * t * 4 + 2 * N_HEADS * t * HEAD_DIM * 2 + N_HEADS * seq * LANES * 4
           + (2 * N_CHAINS + 36) * t * t * 4)
    qspec = lambda blk: pl.BlockSpec((1, N_HEADS, t, HEAD_DIM), lambda b, i: (b, blk, i, 0))
    kspec = lambda blk: pl.BlockSpec((1, N_HEADS, seq, HEAD_DIM), lambda b, i: (b, blk, 0, 0))
    vspec = lambda blk: pl.BlockSpec((1, N_HEADS, VT_ROWS, seq), lambda b, i: (b, blk, 0, 0))
    return pl.pallas_call(
        functools.partial(_attn_kernel, t=t, seq=seq),
        grid=(batch, nq),
        in_specs=[
            pl.BlockSpec(memory_space=pltpu.SMEM),
            qspec(0), kspec(1), vspec(0),
            qspec(2), kspec(3), vspec(1),
            pl.BlockSpec((N_HEADS, 2, t, t), lambda b, i: (0, 0, 0, 0)),
            pl.BlockSpec((1, N_HEADS, 1, seq), lambda b, i: (b, 0, 0, 0)),
            pl.BlockSpec((HEAD_DIM, t), lambda b, i: (0, 0)),
            pl.BlockSpec((HEAD_DIM, t), lambda b, i: (0, 0)),
        ],
        out_specs=pl.BlockSpec((t, 2 * N_HEADS * HEAD_DIM), lambda b, i: (b * nq + i, 0)),
        out_shape=jax.ShapeDtypeStruct((batch * seq, 2 * N_HEADS * HEAD_DIM), BF16),
        scratch_shapes=[pltpu.VMEM((N_CHAINS, 1, t), F32),
                        pltpu.VMEM((N_CHAINS, VT_ROWS, t), F32),
                        pltpu.VMEM((2 * N_HEADS, t, HEAD_DIM), BF16),
                        pltpu.VMEM((N_HEADS, seq, 1), F32),
                        pltpu.VMEM((N_CHAINS, t, t), F32), pltpu.VMEM((N_CHAINS, 1, t), F32),
                        pltpu.VMEM((N_CHAINS, t, t), F32), pltpu.VMEM((N_CHAINS, 1, t), F32)],
        compiler_params=pltpu.CompilerParams(
            dimension_semantics=("arbitrary", "arbitrary"), vmem_limit_bytes=_vmem_limit(est)),
        name="attn",
    )(lam, qk, qk, vt, qk, qk, vt, bias_tiles_t, cum4, subln_g, fox_g)


def _out_q_kernel(x_ref, m_ref, wo_ref, gc_ref, wq_ref, qg_ref, x1_ref, qc_ref):
    x1 = x_ref[...] + jnp.dot(m_ref[...], wo_ref[...], preferred_element_type=F32)
    x1_ref[...] = x1
    hc = _rms(x1, gc_ref[...]).astype(BF16)
    q = jnp.dot(hc, wq_ref[...], preferred_element_type=F32)
    for hh in range(N_HEADS):
        sl = slice(hh * CROSS_HEAD_DIM, (hh + 1) * CROSS_HEAD_DIM)
        qh = _rms(q[:, sl], qg_ref[...]) * (CROSS_HEAD_DIM ** -0.5)
        qc_ref[:, sl] = qh.astype(BF16)


def _out_q(x2d, mixed, w_out, g_cross, w_cq, q_g):
    n = x2d.shape[0]
    tm = min(PROJ_ROWS, n)
    est = 2 * (tm * D_MODEL * 4 * 2 + tm * D_MODEL * 2 + 2 * D_MODEL * D_MODEL * 2 + tm * D_MODEL * 2) + 6 * tm * D_MODEL * 4
    return pl.pallas_call(
        _out_q_kernel,
        grid=(n // tm,),
        in_specs=[
            pl.BlockSpec((tm, D_MODEL), lambda i: (i, 0)),
            pl.BlockSpec((tm, D_MODEL), lambda i: (i, 0)),
            pl.BlockSpec((D_MODEL, D_MODEL), lambda i: (0, 0)),
            pl.BlockSpec((1, D_MODEL), lambda i: (0, 0)),
            pl.BlockSpec((D_MODEL, D_MODEL), lambda i: (0, 0)),
            pl.BlockSpec((1, CROSS_HEAD_DIM), lambda i: (0, 0)),
        ],
        out_specs=[pl.BlockSpec((tm, D_MODEL), lambda i: (i, 0)),
                   pl.BlockSpec((tm, D_MODEL), lambda i: (i, 0))],
        out_shape=[jax.ShapeDtypeStruct((n, D_MODEL), F32), jax.ShapeDtypeStruct((n, D_MODEL), BF16)],
        compiler_params=pltpu.CompilerParams(
            dimension_semantics=("arbitrary",), vmem_limit_bytes=_vmem_limit(est)),
        name="out_q",
    )(x2d, mixed, w_out, g_cross, w_cq, q_g)


def _mem_kv_kernel(mem_ref, gm_ref, w_ref, kg_ref, k_ref, v_ref):
    mn = _rms(mem_ref[...], gm_ref[...]).astype(BF16)
    kv = jnp.dot(mn, w_ref[...], preferred_element_type=F32)
    for hh in range(N_HEADS):
        sl = slice(hh * CROSS_HEAD_DIM, (hh + 1) * CROSS_HEAD_DIM)
        k_ref[0, :, sl] = _rms(kv[:, sl], kg_ref[...]).astype(BF16)
    v_ref[0] = kv[:, D_MODEL:].astype(BF16)


def _mem_kv(mem2d, g_mem, w_ckv, k_g, *, batch):
    est = 2 * (N_MEM * D_MODEL * 4 + D_MODEL * 2 * D_MODEL * 2 + 2 * N_MEM * D_MODEL * 2) + 4 * N_MEM * 2 * D_MODEL * 4
    return pl.pallas_call(
        _mem_kv_kernel,
        grid=(batch,),
        in_specs=[
            pl.BlockSpec((N_MEM, D_MODEL), lambda b: (b, 0)),
            pl.BlockSpec((1, D_MODEL), lambda b: (0, 0)),
            pl.BlockSpec((D_MODEL, 2 * D_MODEL), lambda b: (0, 0)),
            pl.BlockSpec((1, CROSS_HEAD_DIM), lambda b: (0, 0)),
        ],
        out_specs=[pl.BlockSpec((1, N_MEM, D_MODEL), lambda b: (b, 0, 0)),
                   pl.BlockSpec((1, N_MEM, D_MODEL), lambda b: (b, 0, 0))],
        out_shape=[jax.ShapeDtypeStruct((batch, N_MEM, D_MODEL), BF16),
                   jax.ShapeDtypeStruct((batch, N_MEM, D_MODEL), BF16)],
        compiler_params=pltpu.CompilerParams(
            dimension_semantics=("arbitrary",), vmem_limit_bytes=_vmem_limit(est)),
        name="mem_kv",
    )(mem2d, g_mem, w_ckv, k_g)


def _route(logits_t):
    gl = logits_t[0:N_GROUPS]
    gmax = jnp.max(gl, axis=0, keepdims=True)
    eg = jnp.exp(gl - gmax)
    p_group = eg / jnp.sum(eg, axis=0, keepdims=True)
    p_g = jnp.max(p_group, axis=0, keepdims=True)
    g_idx = jnp.full_like(p_g, float(N_GROUPS))
    for g in reversed(range(N_GROUPS)):
        g_idx = jnp.where(p_group[g:g + 1] == p_g, float(g), g_idx)

    sel = []
    for j in range(EXPERTS_PER_GROUP):
        acc = jnp.zeros_like(p_g)
        for g in range(N_GROUPS):
            r = N_GROUPS + g * EXPERTS_PER_GROUP + j
            acc = jnp.where(g_idx == float(g), logits_t[r:r + 1], acc)
        sel.append(acc)
    smax = functools.reduce(jnp.maximum, sel)
    es = [jnp.exp(s - smax) for s in sel]
    den = functools.reduce(jnp.add, es)
    p_in = [e / den for e in es]

    v0 = functools.reduce(jnp.maximum, p_in)
    i0 = jnp.full_like(v0, float(EXPERTS_PER_GROUP))
    for j in reversed(range(EXPERTS_PER_GROUP)):
        i0 = jnp.where(p_in[j] == v0, float(j), i0)
    rest = [jnp.where(i0 == float(j), -1.0, p_in[j]) for j in range(EXPERTS_PER_GROUP)]
    v1 = functools.reduce(jnp.maximum, rest)
    i1 = jnp.full_like(v1, float(EXPERTS_PER_GROUP))
    for j in reversed(range(EXPERTS_PER_GROUP)):
        i1 = jnp.where(rest[j] == v1, float(j), i1)

    tot = v0 + v1
    w0 = p_g * (v0 / tot)
    w1 = p_g * (v1 / tot)
    first = i0 < i1
    a = jnp.where(first, i0, i1)
    b = jnp.where(first, i1, i0)
    w_lo = jnp.where(first, w0, w1)
    w_hi = jnp.where(first, w1, w0)
    pair = jnp.where(a == 0.0, b - 1.0, jnp.where(a == 1.0, jnp.where(b == 3.0, 3.0, 4.0), 5.0))
    swap = a == 2.0
    wa = jnp.where(swap, w_hi, w_lo)
    wb = jnp.where(swap, w_lo, w_hi)
    bucket = g_idx * float(len(PAIRS)) + pair
    return bucket, wa, wb


def _cross_router_kernel(q_ref, k_ref, v_ref, x1_ref, wco_ref, gf_ref, whi_ref, wlo_ref, rb_ref,
                         xa_ref, rt_ref, *, tm):
    q = q_ref[...]
    k = k_ref[0]
    v = v_ref[0]
    outs = []
    for hh in range(N_HEADS):
        sl = slice(hh * CROSS_HEAD_DIM, (hh + 1) * CROSS_HEAD_DIM)
        s = _nt_dot(q[:, sl], k[:, sl])
        p = jnp.exp(s - jnp.max(s, axis=-1, keepdims=True))
        l = jnp.sum(p, axis=-1, keepdims=True)
        o = jnp.dot(p.astype(BF16), v[:, sl], preferred_element_type=F32) / l
        outs.append(o.astype(BF16))
    o = jnp.concatenate(outs, axis=1)
    x2 = x1_ref[...] + jnp.dot(o, wco_ref[...], preferred_element_type=F32)
    xa_ref[:, :D_MODEL] = x2

    h3 = _rms(x2, gf_ref[...])
    hi = h3.astype(BF16)
    lo = (h3 - hi.astype(F32)).astype(BF16)
    lt = _nt_dot(whi_ref[...], hi) + (_nt_dot(whi_ref[...], lo) + _nt_dot(wlo_ref[...], hi))
    lt = lt + rb_ref[...]
    bucket, wa, wb = _route(lt)

    sub = lax.broadcasted_iota(I32, (8, tm), 0)
    rows = jnp.where(sub == 0, bucket, jnp.where(sub == 1, wa, jnp.where(sub == 2, wb, 0.0)))
    rt_ref[0] = rows
    sub_a = lax.broadcasted_iota(I32, (AUX_COLS, tm), 0)
    aux_t = jnp.where(sub_a == 0, bucket, jnp.where(sub_a == 1, wa, jnp.where(sub_a == 2, wb, 0.0)))
    xa_ref[:, D_MODEL:] = aux_t.T


def _cross_router(qc, kc, vc, x1, w_co, g_ffn, wr_hi, wr_lo, rb, *, batch, seq):
    tm = min(CROSS_ROWS, seq)
    ns = seq // tm
    n = batch * seq
    est = 2 * (tm * D_MODEL * 2 + 2 * N_MEM * D_MODEL * 2 + tm * D_MODEL * 4 + D_MODEL * D_MODEL * 2
               + tm * ROW_W * 4) + 10 * tm * D_MODEL * 4
    return pl.pallas_call(
        functools.partial(_cross_router_kernel, tm=tm),
        grid=(batch, ns),
        in_specs=[
            pl.BlockSpec((tm, D_MODEL), lambda b, s: (b * ns + s, 0)),
            pl.BlockSpec((1, N_MEM, D_MODEL), lambda b, s: (b, 0, 0)),
            pl.BlockSpec((1, N_MEM, D_MODEL), lambda b, s: (b, 0, 0)),
            pl.BlockSpec((tm, D_MODEL), lambda b, s: (b * ns + s, 0)),
            pl.BlockSpec((D_MODEL, D_MODEL), lambda b, s: (0, 0)),
            pl.BlockSpec((1, D_MODEL), lambda b, s: (0, 0)),
            pl.BlockSpec((32, D_MODEL), lambda b, s: (0, 0)),
            pl.BlockSpec((32, D_MODEL), lambda b, s: (0, 0)),
            pl.BlockSpec((32, 1), lambda b, s: (0, 0)),
        ],
        out_specs=[pl.BlockSpec((tm, ROW_W), lambda b, s: (b * ns + s, 0)),
                   pl.BlockSpec((1, 8, tm), lambda b, s: (b * ns + s, 0, 0))],
        out_shape=[jax.ShapeDtypeStruct((n, ROW_W), F32),
                   jax.ShapeDtypeStruct((n // tm, 8, tm), F32)],
        compiler_params=pltpu.CompilerParams(
            dimension_semantics=("arbitrary", "arbitrary"), vmem_limit_bytes=_vmem_limit(est)),
        name="cross_router",
    )(qc, kc, vc, x1, w_co, g_ffn, wr_hi, wr_lo, rb)


def _rank_kernel(rt_ref, rank_ref, cnt_ref, carry_sc, *, tm):
    @pl.when(pl.program_id(0) == 0)
    def _():
        carry_sc[...] = jnp.zeros_like(carry_sc)

    bucket = rt_ref[0, 0:1, :]
    sub = lax.broadcasted_iota(I32, (32, tm), 0).astype(F32)
    hit = sub == bucket
    r = lax.broadcasted_iota(I32, (tm, tm), 0)
    c = lax.broadcasted_iota(I32, (tm, tm), 1)
    upper = jnp.where(r <= c, 1.0, 0.0).astype(BF16)
    cum = jnp.dot(jnp.where(hit, 1.0, 0.0).astype(BF16), upper, preferred_element_type=F32)
    carry = carry_sc[:, 0:1]
    rank = jnp.sum(jnp.where(hit, cum - 1.0 + carry, 0.0), axis=0, keepdims=True)
    rank_ref[0] = rank.astype(I32)
    new_carry = carry_sc[...] + jnp.broadcast_to(cum[:, tm - 1:tm], carry_sc.shape)
    carry_sc[...] = new_carry
    cnt_ref[...] = new_carry


def _rank(rt):
    nt, _, tm = rt.shape
    return pl.pallas_call(
        functools.partial(_rank_kernel, tm=tm),
        grid=(nt,),
        in_specs=[pl.BlockSpec((1, 8, tm), lambda i: (i, 0, 0))],
        out_specs=[pl.BlockSpec((1, 1, tm), lambda i: (i, 0, 0)),
                   pl.BlockSpec((32, LANES), lambda i: (0, 0))],
        out_shape=[jax.ShapeDtypeStruct((nt, 1, tm), I32), jax.ShapeDtypeStruct((32, LANES), F32)],
        scratch_shapes=[pltpu.VMEM((32, LANES), F32)],
        compiler_params=pltpu.CompilerParams(dimension_semantics=("arbitrary",)),
        name="rank",
    )(rt)


def _dispatch_kernel(pos_ref, zrow_ref, zflag_ref, nact_ref, xa_ref, xs_ref, zbuf, zsem, sem, *,
                     n_tiles, rows):
    @pl.when(pl.program_id(0) == 0)
    def _():
        _zero_partial_tiles(zrow_ref, zflag_ref, nact_ref, xs_ref, zbuf, zsem, n_tiles=n_tiles)

    base = pl.program_id(0) * rows
    for r in range(rows):
        pltpu.make_async_copy(xa_ref.at[pl.ds(r, 1)], xs_ref.at[pl.ds(pos_ref[base + r], 1)], sem).start(
            priority=r % 2)
    pltpu.make_async_copy(xa_ref, xs_ref.at[pl.ds(0, rows)], sem).wait()


def _zero_partial_tiles(zrow_ref, zflag_ref, nact_ref, xs_ref, zbuf, zsem, *, n_tiles):
    zbuf[...] = jnp.zeros_like(zbuf)

    def zero_copy(row):
        return pltpu.make_async_copy(zbuf, xs_ref.at[pl.ds(pl.multiple_of(row, MOE_ROWS), MOE_ROWS)], zsem)

    def bucket_tiles(op):
        def body(b, carry):
            @pl.when(zflag_ref[b] > 0)
            def _():
                op(zero_copy(zrow_ref[b]))
            return carry
        lax.fori_loop(0, N_PAIR_BUCKETS, body, 0)

    def unused_tiles(op):
        def body(i, carry):
            op(zero_copy(i * MOE_ROWS))
            return carry
        lax.fori_loop(nact_ref[0], n_tiles, body, 0)

    bucket_tiles(lambda cp: cp.start())
    unused_tiles(lambda cp: cp.start())
    bucket_tiles(lambda cp: cp.wait())
    unused_tiles(lambda cp: cp.wait())


def _dispatch(pos, zrow, zflag, n_active, xa, *, n_tiles):
    n_tok = xa.shape[0]
    n_sorted = n_tiles * MOE_ROWS
    rows = min(DMA_ROWS, n_tok)
    return pl.pallas_call(
        functools.partial(_dispatch_kernel, n_tiles=n_tiles, rows=rows),
        grid_spec=pltpu.PrefetchScalarGridSpec(
            num_scalar_prefetch=4,
            grid=(n_tok // rows,),
            in_specs=[pl.BlockSpec((rows, ROW_W), lambda i, *_: (i, 0))],
            out_specs=pl.BlockSpec(memory_space=pl.ANY),
            scratch_shapes=[pltpu.VMEM((MOE_ROWS, ROW_W), F32), pltpu.SemaphoreType.DMA(()),
                            pltpu.SemaphoreType.DMA(())],
        ),
        out_shape=jax.ShapeDtypeStruct((n_sorted, ROW_W), F32),
        compiler_params=pltpu.CompilerParams(dimension_semantics=("arbitrary",), has_side_effects=True),
        name="dispatch",
    )(pos, zrow, zflag, n_active, xa)


def _moe_kernel(ta_ref, tb_ref, xi_ref, nact_ref, xs_ref, gf_ref, wga_ref, wua_ref, wda_ref,
                wgb_ref, wub_ref, wdb_ref, ys_ref):
    i = pl.program_id(0)

    @pl.when(i < nact_ref[0])
    def _():
        x2 = xs_ref[:, :D_MODEL]
        aux = xs_ref[:, D_MODEL:]
        wa = aux[:, 1:2]
        wb = aux[:, 2:3]
        h = _rms(x2, gf_ref[...]).astype(BF16)

        def mlp(wg, wu, wd):
            g = jnp.dot(h, wg[0].astype(BF16), preferred_element_type=F32)
            u = jnp.dot(h, wu[0].astype(BF16), preferred_element_type=F32)
            act = (g / (1.0 + jnp.exp(-g))) * u
            return jnp.dot(act.astype(BF16), wd[0].astype(BF16), preferred_element_type=F32)

        ys_ref[...] = x2 + (wa * mlp(wga_ref, wua_ref, wda_ref) + wb * mlp(wgb_ref, wub_ref, wdb_ref))

    @pl.when(i >= nact_ref[0])
    def _():
        ys_ref[...] = jnp.zeros_like(ys_ref)


def _moe(tile_a, tile_b, tile_x, n_active, xs, g_ffn, w_gate, w_up, w_down):
    n_tiles = xs.shape[0] // MOE_ROWS
    t = MOE_ROWS
    wa_map = lambda i, ta, tb, xi, na: (ta[i], 0, 0)
    wb_map = lambda i, ta, tb, xi, na: (tb[i], 0, 0)
    est = 2 * (t * ROW_W * 4 + 6 * D_MODEL * D_EXPERT * 4 + t * D_MODEL * 4) + 10 * t * D_MODEL * 4
    return pl.pallas_call(
        _moe_kernel,
        grid_spec=pltpu.PrefetchScalarGridSpec(
            num_scalar_prefetch=4,
            grid=(n_tiles,),
            in_specs=[
                pl.BlockSpec((t, ROW_W), lambda i, ta, tb, xi, na: (xi[i], 0)),
                pl.BlockSpec((1, D_MODEL), lambda i, ta, tb, xi, na: (0, 0)),
                pl.BlockSpec((1, D_MODEL, D_EXPERT), wa_map),
                pl.BlockSpec((1, D_MODEL, D_EXPERT), wa_map),
                pl.BlockSpec((1, D_EXPERT, D_MODEL), wa_map),
                pl.BlockSpec((1, D_MODEL, D_EXPERT), wb_map),
                pl.BlockSpec((1, D_MODEL, D_EXPERT), wb_map),
                pl.BlockSpec((1, D_EXPERT, D_MODEL), wb_map),
            ],
            out_specs=pl.BlockSpec((t, D_MODEL), lambda i, ta, tb, xi, na: (i, 0)),
        ),
        out_shape=jax.ShapeDtypeStruct((n_tiles * t, D_MODEL), F32),
        compiler_params=pltpu.CompilerParams(
            dimension_semantics=("arbitrary",), vmem_limit_bytes=_vmem_limit(est)),
        name="moe",
    )(tile_a, tile_b, tile_x, n_active, xs, g_ffn, w_gate, w_up, w_down, w_gate, w_up, w_down)


def _combine_kernel(pos_ref, ys_ref, out_ref, sem, *, rows):
    base = pl.program_id(0) * rows
    for r in range(rows):
        pltpu.make_async_copy(ys_ref.at[pl.ds(pos_ref[base + r], 1)], out_ref.at[pl.ds(r, 1)], sem).start(
            priority=r % 2)
    pltpu.make_async_copy(ys_ref.at[pl.ds(0, rows)], out_ref, sem).wait()


def _combine(pos, ys, *, n_tok):
    rows = min(DMA_ROWS, n_tok)
    return pl.pallas_call(
        functools.partial(_combine_kernel, rows=rows),
        grid_spec=pltpu.PrefetchScalarGridSpec(
            num_scalar_prefetch=1,
            grid=(n_tok // rows,),
            in_specs=[pl.BlockSpec(memory_space=pl.ANY)],
            out_specs=pl.BlockSpec((rows, D_MODEL), lambda i, *_: (i, 0)),
            scratch_shapes=[pltpu.SemaphoreType.DMA(())],
        ),
        out_shape=jax.ShapeDtypeStruct((n_tok, D_MODEL), F32),
        compiler_params=pltpu.CompilerParams(dimension_semantics=("arbitrary",), has_side_effects=True),
        name="combine",
    )(pos, ys)


def _t5_bucket(rel):
    nb = N_BUCKETS_T5 // 2
    max_exact = nb // 2
    ret = (rel > 0).astype(I32) * nb
    n = jnp.abs(rel)
    nf = jnp.maximum(n, 1).astype(F32)
    large = max_exact + (jnp.log(nf / max_exact) / math.log(MAX_DISTANCE / max_exact)
                         * (nb - max_exact)).astype(I32)
    large = jnp.minimum(large, nb - 1)
    return ret + jnp.where(n < max_exact, n, large)


def _diff_bias(rel_bias, t):
    assert t % CHUNK == 0 and t >= MAX_DISTANCE
    n_rel = 3 * t
    vec = rel_bias[_t5_bucket(jnp.arange(n_rel, dtype=I32) - (2 * t - 1))].astype(F32).T
    toe = jnp.tile(vec, (1, t))[:, :t * (n_rel - 1)].reshape(N_HEADS, t, n_rel - 1)
    qpos = jnp.arange(t, dtype=I32)[:, None]
    kpos = jnp.arange(t, dtype=I32)[None, :]
    far = rel_bias[_t5_bucket(jnp.asarray(-2 * t, I32))].astype(F32)[:, None, None]
    b0 = (toe[:, :, 2 * t - 1:3 * t - 1] - far) * LOG2E
    b0 = jnp.where((kpos // CHUNK <= qpos // CHUNK)[None], b0, NEG_INF)
    b1 = (toe[:, :, t - 1:2 * t - 1] - far) * LOG2E
    return jnp.swapaxes(jnp.stack([b0, b1], axis=1), -1, -2)


def _tile_plan(counts, n_tiles):
    t = MOE_ROWS
    tiles = (counts + t - 1) // t
    ends = jnp.cumsum(tiles)
    starts = ends - tiles
    n_active = ends[-1]
    tile_idx = jnp.minimum(jnp.arange(n_tiles, dtype=I32), n_active - 1)
    tile_bucket = jnp.minimum(jnp.sum((ends[None, :] <= tile_idx[:, None]).astype(I32), axis=1), N_PAIR_BUCKETS - 1)
    pa = jnp.asarray([p[0] for p in PAIRS], I32)
    pb = jnp.asarray([p[1] for p in PAIRS], I32)
    grp = tile_bucket // len(PAIRS)
    pair = tile_bucket % len(PAIRS)
    tile_a = grp * EXPERTS_PER_GROUP + pa[pair]
    tile_b = grp * EXPERTS_PER_GROUP + pb[pair]
    row_off = (starts * t).astype(I32)
    zrow = ((ends - 1) * t).astype(I32)
    zflag = (counts > 0).astype(I32)
    return row_off, zrow, zflag, tile_a.astype(I32), tile_b.astype(I32), tile_idx.astype(I32), n_active.astype(I32)


def kernel(x, mem, rel_bias, norm_mix_g, w_in, b_forget, diff_q_norm_g, diff_k_norm_g, diff_lambda_q1, diff_lambda_k1, diff_lambda_q2, diff_lambda_k2, diff_subln_g, fox_q_norm_g, fox_k_norm_g, fox_out_norm_g, w_out, norm_cross_g, norm_mem_g, w_cq, w_ckv, cross_q_norm_g, cross_k_norm_g, w_co, norm_ffn_g, w_group_router, b_group_router, w_expert_router, b_expert_router, w_exp_gate, w_exp_up, w_exp_down):
    batch, seq, d = x.shape
    assert d == D_MODEL and norm_mix_g.shape[0] == 1 and mem.shape[1] == N_MEM
    n_tok = batch * seq
    assert seq % ATTN_TILE == 0 or seq < ATTN_TILE
    l = 0

    w_qk = jnp.concatenate([w_in[l, :, :1024], w_in[l, :, 1536:2560]], axis=1).astype(BF16)
    w_vt = jnp.concatenate([w_in[l, :, 1024:1536], w_in[l, :, 2560:3072]], axis=1).T.astype(BF16)
    wft = jnp.zeros((8, D_MODEL), F32).at[:N_HEADS].set(w_in[l, :, 3072:].T).astype(BF16)
    bf = jnp.zeros((8, 1), F32).at[:N_HEADS, 0].set(b_forget[l].astype(F32))
    ones = jnp.ones((HEAD_DIM,), F32)
    qg = jnp.stack([
        jnp.tile(diff_q_norm_g[l].astype(F32), 2) * (DIFF_QK_DIM ** -0.5 * LOG2E),
        jnp.tile(diff_k_norm_g[l].astype(F32), 2),
        fox_q_norm_g[l].astype(F32) * (HEAD_DIM ** -0.5 * LOG2E), fox_k_norm_g[l].astype(F32),
        ones, ones, ones, ones])
    lam = (jnp.exp(jnp.sum(diff_lambda_q1[l].astype(F32) * diff_lambda_k1[l].astype(F32)))
           - jnp.exp(jnp.sum(diff_lambda_q2[l].astype(F32) * diff_lambda_k2[l].astype(F32)))
           + LAM_INIT).reshape(1)
    t_attn = min(ATTN_TILE, seq)
    bias_tiles_t = _diff_bias(rel_bias, t_attn)
    wr = jnp.zeros((32, D_MODEL), F32)
    wr = wr.at[:N_GROUPS].set(w_group_router[l].T).at[N_GROUPS:N_GROUPS + N_EXPERTS].set(w_expert_router[l].T)
    wr_hi = wr.astype(BF16)
    wr_lo = (wr - wr_hi.astype(F32)).astype(BF16)
    rb = jnp.zeros((32, 1), F32)
    rb = rb.at[:N_GROUPS, 0].set(b_group_router[l]).at[N_GROUPS:N_GROUPS + N_EXPERTS, 0].set(b_expert_router[l])

    row = lambda v: v.astype(F32).reshape(1, -1)
    x2d = x.reshape(n_tok, D_MODEL)

    qk, vt, cum = _mix_proj(x2d, row(norm_mix_g[l]), w_qk, w_vt, wft, bf, qg, batch=batch, seq=seq)
    cum4 = cum[:, :N_HEADS].reshape(batch, N_HEADS, 1, seq)
    gain_t = lambda v: jnp.broadcast_to(v.astype(F32)[:, None], (HEAD_DIM, t_attn))
    mixed = _attn(qk, vt, bias_tiles_t, lam, cum4, gain_t(diff_subln_g[l]), gain_t(fox_out_norm_g[l]),
                  batch=batch, seq=seq)
    x1, qc = _out_q(x2d, mixed, w_out[l].astype(BF16), row(norm_cross_g[l]),
                    w_cq[l].astype(BF16), row(cross_q_norm_g[l]))

    kc, vc = _mem_kv(mem.reshape(batch * N_MEM, D_MODEL), row(norm_mem_g[l]), w_ckv[l].astype(BF16),
                     row(cross_k_norm_g[l]), batch=batch)
    xa, rt = _cross_router(qc, kc, vc, x1, w_co[l].astype(BF16), row(norm_ffn_g[l]), wr_hi, wr_lo, rb,
                           batch=batch, seq=seq)

    rank, cnt = _rank(rt)
    counts = cnt[:N_PAIR_BUCKETS, 0].astype(I32)
    n_tiles = n_tok // MOE_ROWS + N_PAIR_BUCKETS
    row_off, zrow, zflag, tile_a, tile_b, tile_x, n_active = _tile_plan(counts, n_tiles)
    bucket = rt[:, 0, :].reshape(n_tok).astype(I32)
    pos = row_off[bucket] + rank.reshape(n_tok)
    n_active = n_active.reshape(1)
    xs = _dispatch(pos, zrow, zflag, n_active, xa, n_tiles=n_tiles)
    ys = _moe(tile_a, tile_b, tile_x, n_active, xs, row(norm_ffn_g[l]),
              w_exp_gate[l].astype(F32), w_exp_up[l].astype(F32), w_exp_down[l].astype(F32))
    out = _combine(pos, ys, n_tok=n_tok)
    return out.reshape(batch, seq, D_MODEL)
```

```python
import functools
import math

import jax
import jax.numpy as jnp
import numpy as np
from jax import lax
from jax.experimental import pallas as pl
from jax.experimental.pallas import tpu as pltpu

F32 = jnp.float32
BF16 = jnp.bfloat16
I32 = jnp.int32

D_MODEL = 1024
CHUNK = 64
N_MEM = 256
N_HEADS = 4
HEAD_DIM = 128
VT_ROWS = HEAD_DIM + 16
DIFF_QK_DIM = 64
CROSS_HEAD_DIM = 256
N_BUCKETS_T5 = 32
MAX_DISTANCE = 128
N_GROUPS = 4
EXPERTS_PER_GROUP = 4
N_EXPERTS = N_GROUPS * EXPERTS_PER_GROUP
D_EXPERT = 512
EPS = 1e-6
NEG_INF = -1e30
LAM_INIT = 0.8 - 0.6 * math.exp(-0.3 * 0)
LOG2E = math.log2(math.e)

PAIRS = ((0, 1), (0, 2), (0, 3), (1, 3), (1, 2), (3, 2))
N_PAIR_BUCKETS = N_GROUPS * len(PAIRS)

LANES = 128
VMEM_LIMIT_CAP = 56 * 1024 * 1024

PROJ_ROWS = 1024
ATTN_TILE = 256
CROSS_ROWS = 512
MOE_ROWS = 256
AUX_COLS = LANES
ROW_W = D_MODEL + AUX_COLS
DMA_ROWS = 2048


def _vmem_limit(nbytes):
    return int(min(max(nbytes * 5 // 4, 32 * 1024 * 1024), VMEM_LIMIT_CAP))


def _nt_dot(a, b):
    return lax.dot_general(a, b, (((1,), (1,)), ((), ())), preferred_element_type=F32)


def _rms(x, g):
    ms = jnp.mean(x * x, axis=-1, keepdims=True)
    return x * lax.rsqrt(ms + EPS) * g


def _mix_proj_kernel(x_ref, g_ref, w_ref, wvt_ref, wft_ref, bf_ref, qg_ref, o_ref, vt_ref, cum_ref, carry_ref,
                     *, tm):
    si = pl.program_id(1)
    h = _rms(x_ref[...], g_ref[...]).astype(BF16)
    lane = lax.broadcasted_iota(I32, (tm, HEAD_DIM), 1)
    lo = lane < DIFF_QK_DIM
    for kind in range(4):
        p = jnp.dot(h, w_ref[:, kind * 512:(kind + 1) * 512], preferred_element_type=F32)
        for hh in range(N_HEADS):
            ph = p[:, hh * HEAD_DIM:(hh + 1) * HEAD_DIM]
            if kind in (0, 1):
                sq = ph * ph
                s_lo = jnp.sum(jnp.where(lo, sq, 0.0), axis=-1, keepdims=True)
                s_hi = jnp.sum(jnp.where(lo, 0.0, sq), axis=-1, keepdims=True)
                ms = jnp.where(lo, s_lo, s_hi) * (1.0 / DIFF_QK_DIM)
                ph = ph * lax.rsqrt(ms + EPS) * qg_ref[kind:kind + 1, :]
            else:
                ph = _rms(ph, qg_ref[kind:kind + 1, :])
            o_ref[0, kind * N_HEADS + hh] = ph.astype(BF16)
    for kind in range(2):
        pt = _nt_dot(wvt_ref[kind * 512:(kind + 1) * 512, :], h)
        for hh in range(N_HEADS):
            vt_ref[0, kind * N_HEADS + hh, :HEAD_DIM, :] = pt[hh * HEAD_DIM:(hh + 1) * HEAD_DIM, :].astype(BF16)
            vt_ref[0, kind * N_HEADS + hh, HEAD_DIM:, :] = jnp.ones((VT_ROWS - HEAD_DIM, tm), BF16)

    z = _nt_dot(wft_ref[...], h) + bf_ref[...]
    logf = jnp.minimum(z, 0.0) - jnp.log(1.0 + jnp.exp(-jnp.abs(z)))
    lane8 = lax.broadcasted_iota(I32, (8, tm), 1)
    c = logf * LOG2E
    k = 1
    while k < tm:
        c = c + jnp.where(lane8 >= k, pltpu.roll(c, k, axis=1), 0.0)
        k *= 2

    @pl.when(si == 0)
    def _():
        carry_ref[...] = jnp.zeros_like(carry_ref)

    c = c + carry_ref[:, 0:1]
    cum_ref[0] = c
    carry_ref[...] = jnp.broadcast_to(c[:, tm - 1:tm], carry_ref.shape)


def _mix_proj(x2d, g, w_qk, w_vt, wft, bf, qg, *, batch, seq):
    tm = min(PROJ_ROWS, seq)
    ns = seq // tm
    est = 2 * (tm * D_MODEL * 4 + D_MODEL * 3072 * 2 + 24 * tm * HEAD_DIM * 2) + 8 * tm * D_MODEL
    return pl.pallas_call(
        functools.partial(_mix_proj_kernel, tm=tm),
        grid=(batch, ns),
        in_specs=[
            pl.BlockSpec((tm, D_MODEL), lambda b, s: (b * ns + s, 0)),
            pl.BlockSpec((1, D_MODEL), lambda b, s: (0, 0)),
            pl.BlockSpec((D_MODEL, 2048), lambda b, s: (0, 0)),
            pl.BlockSpec((1024, D_MODEL), lambda b, s: (0, 0)),
            pl.BlockSpec((8, D_MODEL), lambda b, s: (0, 0)),
            pl.BlockSpec((8, 1), lambda b, s: (0, 0)),
            pl.BlockSpec((8, HEAD_DIM), lambda b, s: (0, 0)),
        ],
        out_specs=[
            pl.BlockSpec((1, 16, tm, HEAD_DIM), lambda b, s: (b, 0, s, 0)),
            pl.BlockSpec((1, 8, VT_ROWS, tm), lambda b, s: (b, 0, 0, s)),
            pl.BlockSpec((1, 8, tm), lambda b, s: (b, 0, s)),
        ],
        out_shape=[
            jax.ShapeDtypeStruct((batch, 16, seq, HEAD_DIM), BF16),
            jax.ShapeDtypeStruct((batch, 8, VT_ROWS, seq), BF16),
            jax.ShapeDtypeStruct((batch, 8, seq), F32),
        ],
        scratch_shapes=[pltpu.VMEM((8, LANES), F32)],
        compiler_params=pltpu.CompilerParams(
            dimension_semantics=("arbitrary", "arbitrary"), vmem_limit_bytes=_vmem_limit(est)),
        name="mix_proj",
    )(x2d, g, w_qk, w_vt, wft, bf, qg)


def _init_softmax(m_sc, acc_sc):
    m_sc[...] = jnp.full(m_sc.shape, NEG_INF, F32)
    acc_sc[...] = jnp.zeros_like(acc_sc)


N_CHAINS = 3 * N_HEADS


def _attn_kernel(lam_ref, dq_ref, dk_ref, dvt_ref, fq_ref, fk_ref, fvt_ref, bias_ref, c_ref,
                 gd_ref, gf_ref, o_ref, m_sc, acc_sc, qm_sc, ccol_sc, s_sc, mt_sc, s2_sc, mt2_sc, *, t, seq):
    qi = pl.program_id(1)
    _init_softmax(m_sc, acc_sc)
    key = lax.broadcasted_iota(I32, (t, t), 0)
    qry = lax.broadcasted_iota(I32, (t, t), 1)

    lane = lax.broadcasted_iota(I32, (t, HEAD_DIM), 1)
    for hh in range(N_HEADS):
        q = dq_ref[0, hh]
        zero = jnp.zeros_like(q)
        qm_sc[2 * hh] = jnp.where(lane < DIFF_QK_DIM, q, zero)
        qm_sc[2 * hh + 1] = jnp.where(lane < DIFF_QK_DIM, zero, q)

    @pl.when(qi == 0)
    def _():
        for hh in range(N_HEADS):
            for j in range(seq // t):
                row = c_ref[0, hh, :, j * t:(j + 1) * t]
                ccol_sc[hh, j * t:(j + 1) * t, :] = jnp.sum(
                    jnp.where(key == qry, jnp.broadcast_to(row, (t, t)), 0.0), axis=1, keepdims=True)

    q_off = pl.multiple_of(qi * t, t)

    bufs = ((s_sc, mt_sc), (s2_sc, mt2_sc))

    def logits(j, where, par, slot):
        off = pl.multiple_of(j * t, t)
        s_buf, mt_buf = bufs[par]
        if slot < 2 * N_HEADS:
            hh = slot // 2
            s_t = _nt_dot(dk_ref[0, hh, pl.ds(off, t), :], qm_sc[slot])
            if where != "far":
                s_t = s_t + bias_ref[hh, 1 if where == "near" else 0]
            s_buf[slot] = s_t
            mt_buf[slot] = jnp.max(s_t, axis=0, keepdims=True)
        else:
            hh = slot - 2 * N_HEADS
            c_k = ccol_sc[hh, pl.ds(off, t), :]
            s_t = _nt_dot(fk_ref[0, hh, pl.ds(off, t), :], fq_ref[0, hh]) - c_k
            if where == "diag":
                s_t = jnp.where(key <= qry, s_t, NEG_INF)
            s_buf[slot] = s_t
            mt_buf[slot] = jnp.max(s_t, axis=0, keepdims=True) + c_ref[0, hh, :, pl.ds(q_off, t)]

    def update(j, par, slot):
        off = pl.multiple_of(j * t, t)
        s_buf, mt_buf = bufs[par]
        vt_ref = dvt_ref if slot < 2 * N_HEADS else fvt_ref
        hh = slot // 2 if slot < 2 * N_HEADS else slot - 2 * N_HEADS
        v_t = vt_ref[0, hh, :, pl.ds(off, t)]
        m_prev = m_sc[slot]
        m_new = jnp.maximum(m_prev, mt_buf[slot])
        alpha = jnp.exp2(m_prev - m_new)
        if slot < 2 * N_HEADS:
            p_t = jnp.exp2(s_buf[slot] - m_new)
        else:
            p_t = jnp.exp2(s_buf[slot] - (m_new - c_ref[0, hh, :, pl.ds(q_off, t)]))
        acc_sc[slot] = alpha * acc_sc[slot] + jnp.dot(v_t, p_t.astype(BF16), preferred_element_type=F32)
        m_sc[slot] = m_new

    def phase(upd=None, nxt=None):
        for slot in range(N_CHAINS):
            if nxt is not None:
                logits(*nxt, slot)
            if upd is not None:
                update(*upd, slot)

    n_far = jnp.maximum(qi - 1, 0)
    far_tile = lambda k: jnp.maximum(qi - 2 - k, 0)

    phase(nxt=(qi, "diag", 0))

    @pl.when(qi == 0)
    def _():
        phase(upd=(qi, 0))

    @pl.when(qi == 1)
    def _():
        phase(upd=(qi, 0), nxt=(qi - 1, "near", 1))
        phase(upd=(qi - 1, 1))

    @pl.when(qi >= 2)
    def _():
        phase(upd=(qi, 0), nxt=(qi - 1, "near", 1))
        phase(upd=(qi - 1, 1), nxt=(far_tile(0), "far", 0))

        def far_pair(i, carry):
            k = 2 * i
            phase(upd=(far_tile(k), 0), nxt=(far_tile(k + 1), "far", 1))

            @pl.when(k + 1 < n_far)
            def _():
                phase(upd=(far_tile(k + 1), 1), nxt=(far_tile(k + 2), "far", 0))
            return carry

        lax.fori_loop(0, (n_far + 1) // 2, far_pair, 0)

    def normalised(slot):
        return acc_sc[slot, :HEAD_DIM, :] / acc_sc[slot, HEAD_DIM:HEAD_DIM + 1, :]

    def rms_t(o_t, g_t):
        ms = jnp.mean(o_t * o_t, axis=0, keepdims=True)
        return o_t * lax.rsqrt(ms + EPS) * g_t

    for hh in range(N_HEADS):
        o_t = normalised(2 * hh) - lam_ref[0] * normalised(2 * hh + 1)
        o_ref[:, hh * HEAD_DIM:(hh + 1) * HEAD_DIM] = (
            rms_t(o_t, gd_ref[...]) * (1.0 - LAM_INIT)).T.astype(BF16)
    for hh in range(N_HEADS):
        o_t = normalised(2 * N_HEADS + hh)
        o_ref[:, (N_HEADS + hh) * HEAD_DIM:(N_HEADS + hh + 1) * HEAD_DIM] = rms_t(o_t, gf_ref[...]).T.astype(BF16)


def _attn(qk, vt, bias_tiles_t, lam, cum4, subln_g, fox_g, *, batch, seq):
    t = min(ATTN_TILE, seq)
    nq = seq // t
    head_blk = N_HEADS * seq * HEAD_DIM * 2
    est = (2 * (2 * N_HEADS * t * HEAD_DIM * 2 + 4 * head_blk + N_HEADS * 2 * t * t * 4 + N_HEADS * 8 * seq * 4
                + t * 2 * N_HEADS * HEAD_DIM * 2)
           + N_CHAINS * (HEAD_DIM + 16) * t * 4 + 2 * N_HEADS * t * HEAD_DIM * 2 + N_HEADS * seq * LANES * 4
           + (2 * N_CHAINS + 36) * t * t * 4)
    qspec = lambda blk: pl.BlockSpec((1, N_HEADS, t, HEAD_DIM), lambda b, i: (b, blk, i, 0))
    kspec = lambda blk: pl.BlockSpec((1, N_HEADS, seq, HEAD_DIM), lambda b, i: (b, blk, 0, 0))
    vspec = lambda blk: pl.BlockSpec((1, N_HEADS, VT_ROWS, seq), lambda b, i: (b, blk, 0, 0))
    return pl.pallas_call(
        functools.partial(_attn_kernel, t=t, seq=seq),
        grid=(batch, nq),
        in_specs=[
            pl.BlockSpec(memory_space=pltpu.SMEM),
            qspec(0), kspec(1), vspec(0),
            qspec(2), kspec(3), vspec(1),
            pl.BlockSpec((N_HEADS, 2, t, t), lambda b, i: (0, 0, 0, 0)),
            pl.BlockSpec((1, N_HEADS, 1, seq), lambda b, i: (b, 0, 0, 0)),
            pl.BlockSpec((HEAD_DIM, t), lambda b, i: (0, 0)),
            pl.BlockSpec((HEAD_DIM, t), lambda b, i: (0, 0)),
        ],
        out_specs=pl.BlockSpec((t, 2 * N_HEADS * HEAD_DIM), lambda b, i: (b * nq + i, 0)),
        out_shape=jax.ShapeDtypeStruct((batch * seq, 2 * N_HEADS * HEAD_DIM), BF16),
        scratch_shapes=[pltpu.VMEM((N_CHAINS, 1, t), F32),
                        pltpu.VMEM((N_CHAINS, VT_ROWS, t), F32),
                        pltpu.VMEM((2 * N_HEADS, t, HEAD_DIM), BF16),
                        pltpu.VMEM((N_HEADS, seq, 1), F32),
                        pltpu.VMEM((N_CHAINS, t, t), F32), pltpu.VMEM((N_CHAINS, 1, t), F32),
                        pltpu.VMEM((N_CHAINS, t, t), F32), pltpu.VMEM((N_CHAINS, 1, t), F32)],
        compiler_params=pltpu.CompilerParams(
            dimension_semantics=("arbitrary", "arbitrary"), vmem_limit_bytes=_vmem_limit(est)),
        name="attn",
    )(lam, qk, qk, vt, qk, qk, vt, bias_tiles_t, cum4, subln_g, fox_g)


def _out_q_kernel(x_ref, m_ref, wo_ref, gc_ref, wq_ref, qg_ref, x1_ref, qc_ref):
    x1 = x_ref[...] + jnp.dot(m_ref[...], wo_ref[...], preferred_element_type=F32)
    x1_ref[...] = x1
    hc = _rms(x1, gc_ref[...]).astype(BF16)
    q = jnp.dot(hc, wq_ref[...], preferred_element_type=F32)
    for hh in range(N_HEADS):
        sl = slice(hh * CROSS_HEAD_DIM, (hh + 1) * CROSS_HEAD_DIM)
        qh = _rms(q[:, sl], qg_ref[...]) * (CROSS_HEAD_DIM ** -0.5)
        qc_ref[:, sl] = qh.astype(BF16)


def _out_q(x2d, mixed, w_out, g_cross, w_cq, q_g):
    n = x2d.shape[0]
    tm = min(PROJ_ROWS, n)
    est = 2 * (tm * D_MODEL * 4 * 2 + tm * D_MODEL * 2 + 2 * D_MODEL * D_MODEL * 2 + tm * D_MODEL * 2) + 6 * tm * D_MODEL * 4
    return pl.pallas_call(
        _out_q_kernel,
        grid=(n // tm,),
        in_specs=[
            pl.BlockSpec((tm, D_MODEL), lambda i: (i, 0)),
            pl.BlockSpec((tm, D_MODEL), lambda i: (i, 0)),
            pl.BlockSpec((D_MODEL, D_MODEL), lambda i: (0, 0)),
            pl.BlockSpec((1, D_MODEL), lambda i: (0, 0)),
            pl.BlockSpec((D_MODEL, D_MODEL), lambda i: (0, 0)),
            pl.BlockSpec((1, CROSS_HEAD_DIM), lambda i: (0, 0)),
        ],
        out_specs=[pl.BlockSpec((tm, D_MODEL), lambda i: (i, 0)),
                   pl.BlockSpec((tm, D_MODEL), lambda i: (i, 0))],
        out_shape=[jax.ShapeDtypeStruct((n, D_MODEL), F32), jax.ShapeDtypeStruct((n, D_MODEL), BF16)],
        compiler_params=pltpu.CompilerParams(
            dimension_semantics=("arbitrary",), vmem_limit_bytes=_vmem_limit(est)),
        name="out_q",
    )(x2d, mixed, w_out, g_cross, w_cq, q_g)


def _mem_kv_kernel(mem_ref, gm_ref, w_ref, kg_ref, k_ref, v_ref):
    mn = _rms(mem_ref[...], gm_ref[...]).astype(BF16)
    kv = jnp.dot(mn, w_ref[...], preferred_element_type=F32)
    for hh in range(N_HEADS):
        sl = slice(hh * CROSS_HEAD_DIM, (hh + 1) * CROSS_HEAD_DIM)
        k_ref[0, :, sl] = _rms(kv[:, sl], kg_ref[...]).astype(BF16)
    v_ref[0] = kv[:, D_MODEL:].astype(BF16)


def _mem_kv(mem2d, g_mem, w_ckv, k_g, *, batch):
    est = 2 * (N_MEM * D_MODEL * 4 + D_MODEL * 2 * D_MODEL * 2 + 2 * N_MEM * D_MODEL * 2) + 4 * N_MEM * 2 * D_MODEL * 4
    return pl.pallas_call(
        _mem_kv_kernel,
        grid=(batch,),
        in_specs=[
            pl.BlockSpec((N_MEM, D_MODEL), lambda b: (b, 0)),
            pl.BlockSpec((1, D_MODEL), lambda b: (0, 0)),
            pl.BlockSpec((D_MODEL, 2 * D_MODEL), lambda b: (0, 0)),
            pl.BlockSpec((1, CROSS_HEAD_DIM), lambda b: (0, 0)),
        ],
        out_specs=[pl.BlockSpec((1, N_MEM, D_MODEL), lambda b: (b, 0, 0)),
                   pl.BlockSpec((1, N_MEM, D_MODEL), lambda b: (b, 0, 0))],
        out_shape=[jax.ShapeDtypeStruct((batch, N_MEM, D_MODEL), BF16),
                   jax.ShapeDtypeStruct((batch, N_MEM, D_MODEL), BF16)],
        compiler_params=pltpu.CompilerParams(
            dimension_semantics=("arbitrary",), vmem_limit_bytes=_vmem_limit(est)),
        name="mem_kv",
    )(mem2d, g_mem, w_ckv, k_g)


def _route(logits_t):
    gl = logits_t[0:N_GROUPS]
    gmax = jnp.max(gl, axis=0, keepdims=True)
    eg = jnp.exp(gl - gmax)
    p_group = eg / jnp.sum(eg, axis=0, keepdims=True)
    p_g = jnp.max(p_group, axis=0, keepdims=True)
    g_idx = jnp.full_like(p_g, float(N_GROUPS))
    for g in reversed(range(N_GROUPS)):
        g_idx = jnp.where(p_group[g:g + 1] == p_g, float(g), g_idx)

    sel = []
    for j in range(EXPERTS_PER_GROUP):
        acc = jnp.zeros_like(p_g)
        for g in range(N_GROUPS):
            r = N_GROUPS + g * EXPERTS_PER_GROUP + j
            acc = jnp.where(g_idx == float(g), logits_t[r:r + 1], acc)
        sel.append(acc)
    smax = functools.reduce(jnp.maximum, sel)
    es = [jnp.exp(s - smax) for s in sel]
    den = functools.reduce(jnp.add, es)
    p_in = [e / den for e in es]

    v0 = functools.reduce(jnp.maximum, p_in)
    i0 = jnp.full_like(v0, float(EXPERTS_PER_GROUP))
    for j in reversed(range(EXPERTS_PER_GROUP)):
        i0 = jnp.where(p_in[j] == v0, float(j), i0)
    rest = [jnp.where(i0 == float(j), -1.0, p_in[j]) for j in range(EXPERTS_PER_GROUP)]
    v1 = functools.reduce(jnp.maximum, rest)
    i1 = jnp.full_like(v1, float(EXPERTS_PER_GROUP))
    for j in reversed(range(EXPERTS_PER_GROUP)):
        i1 = jnp.where(rest[j] == v1, float(j), i1)

    tot = v0 + v1
    w0 = p_g * (v0 / tot)
    w1 = p_g * (v1 / tot)
    first = i0 < i1
    a = jnp.where(first, i0, i1)
    b = jnp.where(first, i1, i0)
    w_lo = jnp.where(first, w0, w1)
    w_hi = jnp.where(first, w1, w0)
    pair = jnp.where(a == 0.0, b - 1.0, jnp.where(a == 1.0, jnp.where(b == 3.0, 3.0, 4.0), 5.0))
    swap = a == 2.0
    wa = jnp.where(swap, w_hi, w_lo)
    wb = jnp.where(swap, w_lo, w_hi)
    bucket = g_idx * float(len(PAIRS)) + pair
    return bucket, wa, wb


def _cross_router_kernel(q_ref, k_ref, v_ref, x1_ref, wco_ref, gf_ref, whi_ref, wlo_ref, rb_ref,
                         xa_ref, rt_ref, *, tm):
    q = q_ref[...]
    k = k_ref[0]
    v = v_ref[0]
    outs = []
    for hh in range(N_HEADS):
        sl = slice(hh * CROSS_HEAD_DIM, (hh + 1) * CROSS_HEAD_DIM)
        s = _nt_dot(q[:, sl], k[:, sl])
        p = jnp.exp(s - jnp.max(s, axis=-1, keepdims=True))
        l = jnp.sum(p, axis=-1, keepdims=True)
        o = jnp.dot(p.astype(BF16), v[:, sl], preferred_element_type=F32) / l
        outs.append(o.astype(BF16))
    o = jnp.concatenate(outs, axis=1)
    x2 = x1_ref[...] + jnp.dot(o, wco_ref[...], preferred_element_type=F32)
    xa_ref[:, :D_MODEL] = x2

    h3 = _rms(x2, gf_ref[...])
    hi = h3.astype(BF16)
    lo = (h3 - hi.astype(F32)).astype(BF16)
    lt = _nt_dot(whi_ref[...], hi) + (_nt_dot(whi_ref[...], lo) + _nt_dot(wlo_ref[...], hi))
    lt = lt + rb_ref[...]
    bucket, wa, wb = _route(lt)

    sub = lax.broadcasted_iota(I32, (8, tm), 0)
    rows = jnp.where(sub == 0, bucket, jnp.where(sub == 1, wa, jnp.where(sub == 2, wb, 0.0)))
    rt_ref[0] = rows
    sub_a = lax.broadcasted_iota(I32, (AUX_COLS, tm), 0)
    aux_t = jnp.where(sub_a == 0, bucket, jnp.where(sub_a == 1, wa, jnp.where(sub_a == 2, wb, 0.0)))
    xa_ref[:, D_MODEL:] = aux_t.T


def _cross_router(qc, kc, vc, x1, w_co, g_ffn, wr_hi, wr_lo, rb, *, batch, seq):
    tm = min(CROSS_ROWS, seq)
    ns = seq // tm
    n = batch * seq
    est = 2 * (tm * D_MODEL * 2 + 2 * N_MEM * D_MODEL * 2 + tm * D_MODEL * 4 + D_MODEL * D_MODEL * 2
               + tm * ROW_W * 4) + 10 * tm * D_MODEL * 4
    return pl.pallas_call(
        functools.partial(_cross_router_kernel, tm=tm),
        grid=(batch, ns),
        in_specs=[
            pl.BlockSpec((tm, D_MODEL), lambda b, s: (b * ns + s, 0)),
            pl.BlockSpec((1, N_MEM, D_MODEL), lambda b, s: (b, 0, 0)),
            pl.BlockSpec((1, N_MEM, D_MODEL), lambda b, s: (b, 0, 0)),
            pl.BlockSpec((tm, D_MODEL), lambda b, s: (b * ns + s, 0)),
            pl.BlockSpec((D_MODEL, D_MODEL), lambda b, s: (0, 0)),
            pl.BlockSpec((1, D_MODEL), lambda b, s: (0, 0)),
            pl.BlockSpec((32, D_MODEL), lambda b, s: (0, 0)),
            pl.BlockSpec((32, D_MODEL), lambda b, s: (0, 0)),
            pl.BlockSpec((32, 1), lambda b, s: (0, 0)),
        ],
        out_specs=[pl.BlockSpec((tm, ROW_W), lambda b, s: (b * ns + s, 0)),
                   pl.BlockSpec((1, 8, tm), lambda b, s: (b * ns + s, 0, 0))],
        out_shape=[jax.ShapeDtypeStruct((n, ROW_W), F32),
                   jax.ShapeDtypeStruct((n // tm, 8, tm), F32)],
        compiler_params=pltpu.CompilerParams(
            dimension_semantics=("arbitrary", "arbitrary"), vmem_limit_bytes=_vmem_limit(est)),
        name="cross_router",
    )(qc, kc, vc, x1, w_co, g_ffn, wr_hi, wr_lo, rb)


def _rank_kernel(rt_ref, rank_ref, cnt_ref, carry_sc, *, tm):
    @pl.when(pl.program_id(0) == 0)
    def _():
        carry_sc[...] = jnp.zeros_like(carry_sc)

    bucket = rt_ref[0, 0:1, :]
    sub = lax.broadcasted_iota(I32, (32, tm), 0).astype(F32)
    hit = sub == bucket
    r = lax.broadcasted_iota(I32, (tm, tm), 0)
    c = lax.broadcasted_iota(I32, (tm, tm), 1)
    upper = jnp.where(r <= c, 1.0, 0.0).astype(BF16)
    cum = jnp.dot(jnp.where(hit, 1.0, 0.0).astype(BF16), upper, preferred_element_type=F32)
    carry = carry_sc[:, 0:1]
    rank = jnp.sum(jnp.where(hit, cum - 1.0 + carry, 0.0), axis=0, keepdims=True)
    rank_ref[0] = rank.astype(I32)
    new_carry = carry_sc[...] + jnp.broadcast_to(cum[:, tm - 1:tm], carry_sc.shape)
    carry_sc[...] = new_carry
    cnt_ref[...] = new_carry


def _rank(rt):
    nt, _, tm = rt.shape
    return pl.pallas_call(
        functools.partial(_rank_kernel, tm=tm),
        grid=(nt,),
        in_specs=[pl.BlockSpec((1, 8, tm), lambda i: (i, 0, 0))],
        out_specs=[pl.BlockSpec((1, 1, tm), lambda i: (i, 0, 0)),
                   pl.BlockSpec((32, LANES), lambda i: (0, 0))],
        out_shape=[jax.ShapeDtypeStruct((nt, 1, tm), I32), jax.ShapeDtypeStruct((32, LANES), F32)],
        scratch_shapes=[pltpu.VMEM((32, LANES), F32)],
        compiler_params=pltpu.CompilerParams(dimension_semantics=("arbitrary",)),
        name="rank",
    )(rt)


def _dispatch_kernel(pos_ref, zrow_ref, zflag_ref, nact_ref, xa_ref, xs_ref, zbuf, zsem, sem, *,
                     n_tiles, rows):
    @pl.when(pl.program_id(0) == 0)
    def _():
        _zero_partial_tiles(zrow_ref, zflag_ref, nact_ref, xs_ref, zbuf, zsem, n_tiles=n_tiles)

    base = pl.program_id(0) * rows
    for r in range(rows):
        pltpu.make_async_copy(xa_ref.at[pl.ds(r, 1)], xs_ref.at[pl.ds(pos_ref[base + r], 1)], sem).start(
            priority=r % 2)
    pltpu.make_async_copy(xa_ref, xs_ref.at[pl.ds(0, rows)], sem).wait()


def _zero_partial_tiles(zrow_ref, zflag_ref, nact_ref, xs_ref, zbuf, zsem, *, n_tiles):
    zbuf[...] = jnp.zeros_like(zbuf)

    def zero_copy(row):
        return pltpu.make_async_copy(zbuf, xs_ref.at[pl.ds(pl.multiple_of(row, MOE_ROWS), MOE_ROWS)], zsem)

    def bucket_tiles(op):
        def body(b, carry):
            @pl.when(zflag_ref[b] > 0)
            def _():
                op(zero_copy(zrow_ref[b]))
            return carry
        lax.fori_loop(0, N_PAIR_BUCKETS, body, 0)

    def unused_tiles(op):
        def body(i, carry):
            op(zero_copy(i * MOE_ROWS))
            return carry
        lax.fori_loop(nact_ref[0], n_tiles, body, 0)

    bucket_tiles(lambda cp: cp.start())
    unused_tiles(lambda cp: cp.start())
    bucket_tiles(lambda cp: cp.wait())
    unused_tiles(lambda cp: cp.wait())


def _dispatch(pos, zrow, zflag, n_active, xa, *, n_tiles):
    n_tok = xa.shape[0]
    n_sorted = n_tiles * MOE_ROWS
    rows = min(DMA_ROWS, n_tok)
    return pl.pallas_call(
        functools.partial(_dispatch_kernel, n_tiles=n_tiles, rows=rows),
        grid_spec=pltpu.PrefetchScalarGridSpec(
            num_scalar_prefetch=4,
            grid=(n_tok // rows,),
            in_specs=[pl.BlockSpec((rows, ROW_W), lambda i, *_: (i, 0))],
            out_specs=pl.BlockSpec(memory_space=pl.ANY),
            scratch_shapes=[pltpu.VMEM((MOE_ROWS, ROW_W), F32), pltpu.SemaphoreType.DMA(()),
                            pltpu.SemaphoreType.DMA(())],
        ),
        out_shape=jax.ShapeDtypeStruct((n_sorted, ROW_W), F32),
        compiler_params=pltpu.CompilerParams(dimension_semantics=("arbitrary",), has_side_effects=True),
        name="dispatch",
    )(pos, zrow, zflag, n_active, xa)


def _moe_kernel(ta_ref, tb_ref, xi_ref, nact_ref, xs_ref, gf_ref, wga_ref, wua_ref, wda_ref,
                wgb_ref, wub_ref, wdb_ref, ys_ref):
    i = pl.program_id(0)

    @pl.when(i < nact_ref[0])
    def _():
        x2 = xs_ref[:, :D_MODEL]
        aux = xs_ref[:, D_MODEL:]
        wa = aux[:, 1:2]
        wb = aux[:, 2:3]
        h = _rms(x2, gf_ref[...]).astype(BF16)

        def mlp(wg, wu, wd):
            g = jnp.dot(h, wg[0].astype(BF16), preferred_element_type=F32)
            u = jnp.dot(h, wu[0].astype(BF16), preferred_element_type=F32)
            act = (g / (1.0 + jnp.exp(-g))) * u
            return jnp.dot(act.astype(BF16), wd[0].astype(BF16), preferred_element_type=F32)

        ys_ref[...] = x2 + (wa * mlp(wga_ref, wua_ref, wda_ref) + wb * mlp(wgb_ref, wub_ref, wdb_ref))

    @pl.when(i >= nact_ref[0])
    def _():
        ys_ref[...] = jnp.zeros_like(ys_ref)


def _moe(tile_a, tile_b, tile_x, n_active, xs, g_ffn, w_gate, w_up, w_down):
    n_tiles = xs.shape[0] // MOE_ROWS
    t = MOE_ROWS
    wa_map = lambda i, ta, tb, xi, na: (ta[i], 0, 0)
    wb_map = lambda i, ta, tb, xi, na: (tb[i], 0, 0)
    wspec = pl.BlockSpec
    est = 2 * (t * ROW_W * 4 + 6 * D_MODEL * D_EXPERT * 4 + t * D_MODEL * 4) + 10 * t * D_MODEL * 4
    return pl.pallas_call(
        _moe_kernel,
        grid_spec=pltpu.PrefetchScalarGridSpec(
            num_scalar_prefetch=4,
            grid=(n_tiles,),
            in_specs=[
                pl.BlockSpec((t, ROW_W), lambda i, ta, tb, xi, na: (xi[i], 0)),
                pl.BlockSpec((1, D_MODEL), lambda i, ta, tb, xi, na: (0, 0)),
                wspec((1, D_MODEL, D_EXPERT), wa_map),
                wspec((1, D_MODEL, D_EXPERT), wa_map),
                wspec((1, D_EXPERT, D_MODEL), wa_map),
                wspec((1, D_MODEL, D_EXPERT), wb_map),
                wspec((1, D_MODEL, D_EXPERT), wb_map),
                wspec((1, D_EXPERT, D_MODEL), wb_map),
            ],
            out_specs=pl.BlockSpec((t, D_MODEL), lambda i, ta, tb, xi, na: (i, 0)),
        ),
        out_shape=jax.ShapeDtypeStruct((n_tiles * t, D_MODEL), F32),
        compiler_params=pltpu.CompilerParams(
            dimension_semantics=("arbitrary",), vmem_limit_bytes=_vmem_limit(est)),
        name="moe",
    )(tile_a, tile_b, tile_x, n_active, xs, g_ffn, w_gate, w_up, w_down, w_gate, w_up, w_down)


def _combine_kernel(pos_ref, ys_ref, out_ref, sem, *, rows):
    base = pl.program_id(0) * rows
    for r in range(rows):
        pltpu.make_async_copy(ys_ref.at[pl.ds(pos_ref[base + r], 1)], out_ref.at[pl.ds(r, 1)], sem).start(
            priority=r % 2)
    pltpu.make_async_copy(ys_ref.at[pl.ds(0, rows)], out_ref, sem).wait()


def _combine(pos, ys, *, n_tok):
    rows = min(DMA_ROWS, n_tok)
    return pl.pallas_call(
        functools.partial(_combine_kernel, rows=rows),
        grid_spec=pltpu.PrefetchScalarGridSpec(
            num_scalar_prefetch=1,
            grid=(n_tok // rows,),
            in_specs=[pl.BlockSpec(memory_space=pl.ANY)],
            out_specs=pl.BlockSpec((rows, D_MODEL), lambda i, *_: (i, 0)),
            scratch_shapes=[pltpu.SemaphoreType.DMA(())],
        ),
        out_shape=jax.ShapeDtypeStruct((n_tok, D_MODEL), F32),
        compiler_params=pltpu.CompilerParams(dimension_semantics=("arbitrary",), has_side_effects=True),
        name="combine",
    )(pos, ys)


def _t5_bucket(rel):
    nb = N_BUCKETS_T5 // 2
    max_exact = nb // 2
    ret = (rel > 0).astype(I32) * nb
    n = jnp.abs(rel)
    nf = jnp.maximum(n, 1).astype(F32)
    large = max_exact + (jnp.log(nf / max_exact) / math.log(MAX_DISTANCE / max_exact)
                         * (nb - max_exact)).astype(I32)
    large = jnp.minimum(large, nb - 1)
    return ret + jnp.where(n < max_exact, n, large)


def _diff_bias(rel_bias, t):
    assert t % CHUNK == 0 and t >= MAX_DISTANCE
    n_rel = 3 * t
    vec = rel_bias[_t5_bucket(jnp.arange(n_rel, dtype=I32) - (2 * t - 1))].astype(F32).T
    toe = jnp.tile(vec, (1, t))[:, :t * (n_rel - 1)].reshape(N_HEADS, t, n_rel - 1)
    qpos = jnp.arange(t, dtype=I32)[:, None]
    kpos = jnp.arange(t, dtype=I32)[None, :]
    far = rel_bias[_t5_bucket(jnp.asarray(-2 * t, I32))].astype(F32)[:, None, None]
    b0 = (toe[:, :, 2 * t - 1:3 * t - 1] - far) * LOG2E
    b0 = jnp.where((kpos // CHUNK <= qpos // CHUNK)[None], b0, NEG_INF)
    b1 = (toe[:, :, t - 1:2 * t - 1] - far) * LOG2E
    return jnp.swapaxes(jnp.stack([b0, b1], axis=1), -1, -2)


def _tile_plan(counts, n_tiles):
    t = MOE_ROWS
    tiles = (counts + t - 1) // t
    ends = jnp.cumsum(tiles)
    starts = ends - tiles
    n_active = ends[-1]
    tile_idx = jnp.minimum(jnp.arange(n_tiles, dtype=I32), n_active - 1)
    tile_bucket = jnp.minimum(jnp.sum((ends[None, :] <= tile_idx[:, None]).astype(I32), axis=1), N_PAIR_BUCKETS - 1)
    pa = jnp.asarray([p[0] for p in PAIRS], I32)
    pb = jnp.asarray([p[1] for p in PAIRS], I32)
    grp = tile_bucket // len(PAIRS)
    pair = tile_bucket % len(PAIRS)
    tile_a = grp * EXPERTS_PER_GROUP + pa[pair]
    tile_b = grp * EXPERTS_PER_GROUP + pb[pair]
    row_off = (starts * t).astype(I32)
    zrow = ((ends - 1) * t).astype(I32)
    zflag = (counts > 0).astype(I32)
    return row_off, zrow, zflag, tile_a.astype(I32), tile_b.astype(I32), tile_idx.astype(I32), n_active.astype(I32)


def kernel(x, mem, rel_bias, norm_mix_g, w_in, b_forget, diff_q_norm_g, diff_k_norm_g, diff_lambda_q1, diff_lambda_k1, diff_lambda_q2, diff_lambda_k2, diff_subln_g, fox_q_norm_g, fox_k_norm_g, fox_out_norm_g, w_out, norm_cross_g, norm_mem_g, w_cq, w_ckv, cross_q_norm_g, cross_k_norm_g, w_co, norm_ffn_g, w_group_router, b_group_router, w_expert_router, b_expert_router, w_exp_gate, w_exp_up, w_exp_down):
    batch, seq, d = x.shape
    assert d == D_MODEL and norm_mix_g.shape[0] == 1 and mem.shape[1] == N_MEM
    n_tok = batch * seq
    assert seq % ATTN_TILE == 0 or seq < ATTN_TILE
    l = 0

    w_qk = jnp.concatenate([w_in[l, :, :1024], w_in[l, :, 1536:2560]], axis=1).astype(BF16)
    w_vt = jnp.concatenate([w_in[l, :, 1024:1536], w_in[l, :, 2560:3072]], axis=1).T.astype(BF16)
    wft = jnp.zeros((8, D_MODEL), F32).at[:N_HEADS].set(w_in[l, :, 3072:].T).astype(BF16)
    bf = jnp.zeros((8, 1), F32).at[:N_HEADS, 0].set(b_forget[l].astype(F32))
    ones = jnp.ones((HEAD_DIM,), F32)
    qg = jnp.stack([
        jnp.tile(diff_q_norm_g[l].astype(F32), 2) * (DIFF_QK_DIM ** -0.5 * LOG2E),
        jnp.tile(diff_k_norm_g[l].astype(F32), 2),
        fox_q_norm_g[l].astype(F32) * (HEAD_DIM ** -0.5 * LOG2E), fox_k_norm_g[l].astype(F32),
        ones, ones, ones, ones])
    lam = (jnp.exp(jnp.sum(diff_lambda_q1[l].astype(F32) * diff_lambda_k1[l].astype(F32)))
           - jnp.exp(jnp.sum(diff_lambda_q2[l].astype(F32) * diff_lambda_k2[l].astype(F32)))
           + LAM_INIT).reshape(1)
    t_attn = min(ATTN_TILE, seq)
    bias_tiles_t = _diff_bias(rel_bias, t_attn)
    wr = jnp.zeros((32, D_MODEL), F32)
    wr = wr.at[:N_GROUPS].set(w_group_router[l].T).at[N_GROUPS:N_GROUPS + N_EXPERTS].set(w_expert_router[l].T)
    wr_hi = wr.astype(BF16)
    wr_lo = (wr - wr_hi.astype(F32)).astype(BF16)
    rb = jnp.zeros((32, 1), F32)
    rb = rb.at[:N_GROUPS, 0].set(b_group_router[l]).at[N_GROUPS:N_GROUPS + N_EXPERTS, 0].set(b_expert_router[l])

    row = lambda v: v.astype(F32).reshape(1, -1)
    x2d = x.reshape(n_tok, D_MODEL)

    qk, vt, cum = _mix_proj(x2d, row(norm_mix_g[l]), w_qk, w_vt, wft, bf, qg, batch=batch, seq=seq)
    cum4 = cum[:, :N_HEADS].reshape(batch, N_HEADS, 1, seq)
    gain_t = lambda v: jnp.broadcast_to(v.astype(F32)[:, None], (HEAD_DIM, t_attn))
    mixed = _attn(qk, vt, bias_tiles_t, lam, cum4, gain_t(diff_subln_g[l]), gain_t(fox_out_norm_g[l]),
                  batch=batch, seq=seq)
    x1, qc = _out_q(x2d, mixed, w_out[l].astype(BF16), row(norm_cross_g[l]),
                    w_cq[l].astype(BF16), row(cross_q_norm_g[l]))

    kc, vc = _mem_kv(mem.reshape(batch * N_MEM, D_MODEL), row(norm_mem_g[l]), w_ckv[l].astype(BF16),
                     row(cross_k_norm_g[l]), batch=batch)
    xa, rt = _cross_router(qc, kc, vc, x1, w_co[l].astype(BF16), row(norm_ffn_g[l]), wr_hi, wr_lo, rb,
                           batch=batch, seq=seq)

    rank, cnt = _rank(rt)
    counts = cnt[:N_PAIR_BUCKETS, 0].astype(I32)
    n_tiles = n_tok // MOE_ROWS + N_PAIR_BUCKETS
    row_off, zrow, zflag, tile_a, tile_b, tile_x, n_active = _tile_plan(counts, n_tiles)
    bucket = rt[:, 0, :].reshape(n_tok).astype(I32)
    pos = row_off[bucket] + rank.reshape(n_tok)
    n_active = n_active.reshape(1)
    xs = _dispatch(pos, zrow, zflag, n_active, xa, n_tiles=n_tiles)
    ys = _moe(tile_a, tile_b, tile_x, n_active, xs, row(norm_ffn_g[l]),
              w_exp_gate[l].astype(F32), w_exp_up[l].astype(F32), w_exp_down[l].astype(F32))
    out = _combine(pos, ys, n_tok=n_tok)
    return out.reshape(batch, seq, D_MODEL)
```

```python
import functools
import math

import jax
import jax.numpy as jnp
import numpy as np
from jax import lax
from jax.experimental import pallas as pl
from jax.experimental.pallas import tpu as pltpu

F32 = jnp.float32
BF16 = jnp.bfloat16
I32 = jnp.int32

D_MODEL = 1024
CHUNK = 64
N_MEM = 256
N_HEADS = 4
HEAD_DIM = 128
VT_ROWS = HEAD_DIM + 16
DIFF_QK_DIM = 64
CROSS_HEAD_DIM = 256
N_BUCKETS_T5 = 32
MAX_DISTANCE = 128
N_GROUPS = 4
EXPERTS_PER_GROUP = 4
N_EXPERTS = N_GROUPS * EXPERTS_PER_GROUP
D_EXPERT = 512
EPS = 1e-6
NEG_INF = -1e30
LAM_INIT = 0.8 - 0.6 * math.exp(-0.3 * 0)
LOG2E = math.log2(math.e)

PAIRS = ((0, 1), (0, 2), (0, 3), (1, 3), (1, 2), (3, 2))
N_PAIR_BUCKETS = N_GROUPS * len(PAIRS)

LANES = 128
VMEM_LIMIT_CAP = 56 * 1024 * 1024

PROJ_ROWS = 1024
ATTN_TILE = 256
CROSS_ROWS = 512
MOE_ROWS = 256
AUX_COLS = LANES
ROW_W = D_MODEL + AUX_COLS
DMA_ROWS = 4096


def _vmem_limit(nbytes):
    return int(min(max(nbytes * 5 // 4, 32 * 1024 * 1024), VMEM_LIMIT_CAP))


def _nt_dot(a, b):
    return lax.dot_general(a, b, (((1,), (1,)), ((), ())), preferred_element_type=F32)


def _rms(x, g):
    ms = jnp.mean(x * x, axis=-1, keepdims=True)
    return x * lax.rsqrt(ms + EPS) * g


def _mix_proj_kernel(x_ref, g_ref, w_ref, wvt_ref, wft_ref, bf_ref, qg_ref, o_ref, vt_ref, cum_ref, carry_ref,
                     *, tm):
    si = pl.program_id(1)
    h = _rms(x_ref[...], g_ref[...]).astype(BF16)
    lane = lax.broadcasted_iota(I32, (tm, HEAD_DIM), 1)
    lo = lane < DIFF_QK_DIM
    for kind in range(4):
        p = jnp.dot(h, w_ref[:, kind * 512:(kind + 1) * 512], preferred_element_type=F32)
        for hh in range(N_HEADS):
            ph = p[:, hh * HEAD_DIM:(hh + 1) * HEAD_DIM]
            if kind in (0, 1):
                sq = ph * ph
                s_lo = jnp.sum(jnp.where(lo, sq, 0.0), axis=-1, keepdims=True)
                s_hi = jnp.sum(jnp.where(lo, 0.0, sq), axis=-1, keepdims=True)
                ms = jnp.where(lo, s_lo, s_hi) * (1.0 / DIFF_QK_DIM)
                ph = ph * lax.rsqrt(ms + EPS) * qg_ref[kind:kind + 1, :]
            else:
                ph = _rms(ph, qg_ref[kind:kind + 1, :])
            o_ref[0, kind * N_HEADS + hh] = ph.astype(BF16)
    for kind in range(2):
        pt = _nt_dot(wvt_ref[kind * 512:(kind + 1) * 512, :], h)
        for hh in range(N_HEADS):
            vt_ref[0, kind * N_HEADS + hh, :HEAD_DIM, :] = pt[hh * HEAD_DIM:(hh + 1) * HEAD_DIM, :].astype(BF16)
            vt_ref[0, kind * N_HEADS + hh, HEAD_DIM:, :] = jnp.ones((VT_ROWS - HEAD_DIM, tm), BF16)

    z = _nt_dot(wft_ref[...], h) + bf_ref[...]
    logf = jnp.minimum(z, 0.0) - jnp.log(1.0 + jnp.exp(-jnp.abs(z)))
    lane8 = lax.broadcasted_iota(I32, (8, tm), 1)
    c = logf * LOG2E
    k = 1
    while k < tm:
        c = c + jnp.where(lane8 >= k, pltpu.roll(c, k, axis=1), 0.0)
        k *= 2

    @pl.when(si == 0)
    def _():
        carry_ref[...] = jnp.zeros_like(carry_ref)

    c = c + carry_ref[:, 0:1]
    cum_ref[0] = c
    carry_ref[...] = jnp.broadcast_to(c[:, tm - 1:tm], carry_ref.shape)


def _mix_proj(x2d, g, w_qk, w_vt, wft, bf, qg, *, batch, seq):
    tm = min(PROJ_ROWS, seq)
    ns = seq // tm
    est = 2 * (tm * D_MODEL * 4 + D_MODEL * 3072 * 2 + 24 * tm * HEAD_DIM * 2) + 8 * tm * D_MODEL
    return pl.pallas_call(
        functools.partial(_mix_proj_kernel, tm=tm),
        grid=(batch, ns),
        in_specs=[
            pl.BlockSpec((tm, D_MODEL), lambda b, s: (b * ns + s, 0)),
            pl.BlockSpec((1, D_MODEL), lambda b, s: (0, 0)),
            pl.BlockSpec((D_MODEL, 2048), lambda b, s: (0, 0)),
            pl.BlockSpec((1024, D_MODEL), lambda b, s: (0, 0)),
            pl.BlockSpec((8, D_MODEL), lambda b, s: (0, 0)),
            pl.BlockSpec((8, 1), lambda b, s: (0, 0)),
            pl.BlockSpec((8, HEAD_DIM), lambda b, s: (0, 0)),
        ],
        out_specs=[
            pl.BlockSpec((1, 16, tm, HEAD_DIM), lambda b, s: (b, 0, s, 0)),
            pl.BlockSpec((1, 8, VT_ROWS, tm), lambda b, s: (b, 0, 0, s)),
            pl.BlockSpec((1, 8, tm), lambda b, s: (b, 0, s)),
        ],
        out_shape=[
            jax.ShapeDtypeStruct((batch, 16, seq, HEAD_DIM), BF16),
            jax.ShapeDtypeStruct((batch, 8, VT_ROWS, seq), BF16),
            jax.ShapeDtypeStruct((batch, 8, seq), F32),
        ],
        scratch_shapes=[pltpu.VMEM((8, LANES), F32)],
        compiler_params=pltpu.CompilerParams(
            dimension_semantics=("arbitrary", "arbitrary"), vmem_limit_bytes=_vmem_limit(est)),
        name="mix_proj",
    )(x2d, g, w_qk, w_vt, wft, bf, qg)


def _init_softmax(m_sc, acc_sc):
    m_sc[...] = jnp.full(m_sc.shape, NEG_INF, F32)
    acc_sc[...] = jnp.zeros_like(acc_sc)


N_CHAINS = 3 * N_HEADS


def _attn_kernel(lam_ref, dq_ref, dk_ref, dvt_ref, fq_ref, fk_ref, fvt_ref, bias_ref, c_ref,
                 gd_ref, gf_ref, o_ref, m_sc, acc_sc, qm_sc, ccol_sc, s_sc, mt_sc, s2_sc, mt2_sc, *, t, seq):
    qi = pl.program_id(1)
    _init_softmax(m_sc, acc_sc)
    key = lax.broadcasted_iota(I32, (t, t), 0)
    qry = lax.broadcasted_iota(I32, (t, t), 1)

    lane = lax.broadcasted_iota(I32, (t, HEAD_DIM), 1)
    for hh in range(N_HEADS):
        q = dq_ref[0, hh]
        zero = jnp.zeros_like(q)
        qm_sc[2 * hh] = jnp.where(lane < DIFF_QK_DIM, q, zero)
        qm_sc[2 * hh + 1] = jnp.where(lane < DIFF_QK_DIM, zero, q)

    @pl.when(qi == 0)
    def _():
        for hh in range(N_HEADS):
            for j in range(seq // t):
                row = c_ref[0, hh, :, j * t:(j + 1) * t]
                ccol_sc[hh, j * t:(j + 1) * t, :] = jnp.sum(
                    jnp.where(key == qry, jnp.broadcast_to(row, (t, t)), 0.0), axis=1, keepdims=True)

    q_off = pl.multiple_of(qi * t, t)

    bufs = ((s_sc, mt_sc), (s2_sc, mt2_sc))

    def logits(j, where, par, slot):
        off = pl.multiple_of(j * t, t)
        s_buf, mt_buf = bufs[par]
        if slot < 2 * N_HEADS:
            hh = slot // 2
            s_t = _nt_dot(dk_ref[0, hh, pl.ds(off, t), :], qm_sc[slot])
            if where != "far":
                s_t = s_t + bias_ref[hh, 1 if where == "near" else 0]
            s_buf[slot] = s_t
            mt_buf[slot] = jnp.max(s_t, axis=0, keepdims=True)
        else:
            hh = slot - 2 * N_HEADS
            c_k = ccol_sc[hh, pl.ds(off, t), :]
            s_t = _nt_dot(fk_ref[0, hh, pl.ds(off, t), :], fq_ref[0, hh]) - c_k
            if where == "diag":
                s_t = jnp.where(key <= qry, s_t, NEG_INF)
            s_buf[slot] = s_t
            mt_buf[slot] = jnp.max(s_t, axis=0, keepdims=True) + c_ref[0, hh, :, pl.ds(q_off, t)]

    def update(j, par, slot):
        off = pl.multiple_of(j * t, t)
        s_buf, mt_buf = bufs[par]
        vt_ref = dvt_ref if slot < 2 * N_HEADS else fvt_ref
        hh = slot // 2 if slot < 2 * N_HEADS else slot - 2 * N_HEADS
        v_t = vt_ref[0, hh, :, pl.ds(off, t)]
        m_prev = m_sc[slot]
        m_new = jnp.maximum(m_prev, mt_buf[slot])
        alpha = jnp.exp2(m_prev - m_new)
        if slot < 2 * N_HEADS:
            p_t = jnp.exp2(s_buf[slot] - m_new)
        else:
            p_t = jnp.exp2(s_buf[slot] - (m_new - c_ref[0, hh, :, pl.ds(q_off, t)]))
        acc_sc[slot] = alpha * acc_sc[slot] + jnp.dot(v_t, p_t.astype(BF16), preferred_element_type=F32)
        m_sc[slot] = m_new

    def phase(upd=None, nxt=None):
        for slot in range(N_CHAINS):
            if nxt is not None:
                logits(*nxt, slot)
            if upd is not None:
                update(*upd, slot)

    n_far = jnp.maximum(qi - 1, 0)
    far_tile = lambda k: jnp.maximum(qi - 2 - k, 0)

    phase(nxt=(qi, "diag", 0))

    @pl.when(qi == 0)
    def _():
        phase(upd=(qi, 0))

    @pl.when(qi == 1)
    def _():
        phase(upd=(qi, 0), nxt=(qi - 1, "near", 1))
        phase(upd=(qi - 1, 1))

    @pl.when(qi >= 2)
    def _():
        phase(upd=(qi, 0), nxt=(qi - 1, "near", 1))
        phase(upd=(qi - 1, 1), nxt=(far_tile(0), "far", 0))

        def far_pair(i, carry):
            k = 2 * i
            phase(upd=(far_tile(k), 0), nxt=(far_tile(k + 1), "far", 1))

            @pl.when(k + 1 < n_far)
            def _():
                phase(upd=(far_tile(k + 1), 1), nxt=(far_tile(k + 2), "far", 0))
            return carry

        lax.fori_loop(0, (n_far + 1) // 2, far_pair, 0)

    def normalised(slot):
        return acc_sc[slot, :HEAD_DIM, :] / acc_sc[slot, HEAD_DIM:HEAD_DIM + 1, :]

    def rms_t(o_t, g_t):
        ms = jnp.mean(o_t * o_t, axis=0, keepdims=True)
        return o_t * lax.rsqrt(ms + EPS) * g_t

    for hh in range(N_HEADS):
        o_t = normalised(2 * hh) - lam_ref[0] * normalised(2 * hh + 1)
        o_ref[:, hh * HEAD_DIM:(hh + 1) * HEAD_DIM] = (
            rms_t(o_t, gd_ref[...]) * (1.0 - LAM_INIT)).T.astype(BF16)
    for hh in range(N_HEADS):
        o_t = normalised(2 * N_HEADS + hh)
        o_ref[:, (N_HEADS + hh) * HEAD_DIM:(N_HEADS + hh + 1) * HEAD_DIM] = rms_t(o_t, gf_ref[...]).T.astype(BF16)


def _attn(qk, vt, bias_tiles_t, lam, cum4, subln_g, fox_g, *, batch, seq):
    t = min(ATTN_TILE, seq)
    nq = seq // t
    head_blk = N_HEADS * seq * HEAD_DIM * 2
    est = (2 * (2 * N_HEADS * t * HEAD_DIM * 2 + 4 * head_blk + N_HEADS * 2 * t * t * 4 + N_HEADS * 8 * seq * 4
                + t * 2 * N_HEADS * HEAD_DIM * 2)
           + N_CHAINS * (HEAD_DIM + 16) * t * 4 + 2 * N_HEADS * t * HEAD_DIM * 2 + N_HEADS * seq * LANES * 4
           + (2 * N_CHAINS + 36) * t * t * 4)
    qspec = lambda blk: pl.BlockSpec((1, N_HEADS, t, HEAD_DIM), lambda b, i: (b, blk, i, 0))
    kspec = lambda blk: pl.BlockSpec((1, N_HEADS, seq, HEAD_DIM), lambda b, i: (b, blk, 0, 0))
    vspec = lambda blk: pl.BlockSpec((1, N_HEADS, VT_ROWS, seq), lambda b, i: (b, blk, 0, 0))
    return pl.pallas_call(
        functools.partial(_attn_kernel, t=t, seq=seq),
        grid=(batch, nq),
        in_specs=[
            pl.BlockSpec(memory_space=pltpu.SMEM),
            qspec(0), kspec(1), vspec(0),
            qspec(2), kspec(3), vspec(1),
            pl.BlockSpec((N_HEADS, 2, t, t), lambda b, i: (0, 0, 0, 0)),
            pl.BlockSpec((1, N_HEADS, 1, seq), lambda b, i: (b, 0, 0, 0)),
            pl.BlockSpec((HEAD_DIM, t), lambda b, i: (0, 0)),
            pl.BlockSpec((HEAD_DIM, t), lambda b, i: (0, 0)),
        ],
        out_specs=pl.BlockSpec((t, 2 * N_HEADS * HEAD_DIM), lambda b, i: (b * nq + i, 0)),
        out_shape=jax.ShapeDtypeStruct((batch * seq, 2 * N_HEADS * HEAD_DIM), BF16),
        scratch_shapes=[pltpu.VMEM((N_CHAINS, 1, t), F32),
                        pltpu.VMEM((N_CHAINS, VT_ROWS, t), F32),
                        pltpu.VMEM((2 * N_HEADS, t, HEAD_DIM), BF16),
                        pltpu.VMEM((N_HEADS, seq, 1), F32),
                        pltpu.VMEM((N_CHAINS, t, t), F32), pltpu.VMEM((N_CHAINS, 1, t), F32),
                        pltpu.VMEM((N_CHAINS, t, t), F32), pltpu.VMEM((N_CHAINS, 1, t), F32)],
        compiler_params=pltpu.CompilerParams(
            dimension_semantics=("arbitrary", "arbitrary"), vmem_limit_bytes=_vmem_limit(est)),
        name="attn",
    )(lam, qk, qk, vt, qk, qk, vt, bias_tiles_t, cum4, subln_g, fox_g)


def _out_q_kernel(x_ref, m_ref, wo_ref, gc_ref, wq_ref, qg_ref, x1_ref, qc_ref):
    x1 = x_ref[...] + jnp.dot(m_ref[...], wo_ref[...], preferred_element_type=F32)
    x1_ref[...] = x1
    hc = _rms(x1, gc_ref[...]).astype(BF16)
    q = jnp.dot(hc, wq_ref[...], preferred_element_type=F32)
    for hh in range(N_HEADS):
        sl = slice(hh * CROSS_HEAD_DIM, (hh + 1) * CROSS_HEAD_DIM)
        qh = _rms(q[:, sl], qg_ref[...]) * (CROSS_HEAD_DIM ** -0.5)
        qc_ref[:, sl] = qh.astype(BF16)


def _out_q(x2d, mixed, w_out, g_cross, w_cq, q_g):
    n = x2d.shape[0]
    tm = min(PROJ_ROWS, n)
    est = 2 * (tm * D_MODEL * 4 * 2 + tm * D_MODEL * 2 + 2 * D_MODEL * D_MODEL * 2 + tm * D_MODEL * 2) + 6 * tm * D_MODEL * 4
    return pl.pallas_call(
        _out_q_kernel,
        grid=(n // tm,),
        in_specs=[
            pl.BlockSpec((tm, D_MODEL), lambda i: (i, 0)),
            pl.BlockSpec((tm, D_MODEL), lambda i: (i, 0)),
            pl.BlockSpec((D_MODEL, D_MODEL), lambda i: (0, 0)),
            pl.BlockSpec((1, D_MODEL), lambda i: (0, 0)),
            pl.BlockSpec((D_MODEL, D_MODEL), lambda i: (0, 0)),
            pl.BlockSpec((1, CROSS_HEAD_DIM), lambda i: (0, 0)),
        ],
        out_specs=[pl.BlockSpec((tm, D_MODEL), lambda i: (i, 0)),
                   pl.BlockSpec((tm, D_MODEL), lambda i: (i, 0))],
        out_shape=[jax.ShapeDtypeStruct((n, D_MODEL), F32), jax.ShapeDtypeStruct((n, D_MODEL), BF16)],
        compiler_params=pltpu.CompilerParams(
            dimension_semantics=("arbitrary",), vmem_limit_bytes=_vmem_limit(est)),
        name="out_q",
    )(x2d, mixed, w_out, g_cross, w_cq, q_g)


def _mem_kv_kernel(mem_ref, gm_ref, w_ref, kg_ref, k_ref, v_ref):
    mn = _rms(mem_ref[...], gm_ref[...]).astype(BF16)
    kv = jnp.dot(mn, w_ref[...], preferred_element_type=F32)
    for hh in range(N_HEADS):
        sl = slice(hh * CROSS_HEAD_DIM, (hh + 1) * CROSS_HEAD_DIM)
        k_ref[0, :, sl] = _rms(kv[:, sl], kg_ref[...]).astype(BF16)
    v_ref[0] = kv[:, D_MODEL:].astype(BF16)


def _mem_kv(mem2d, g_mem, w_ckv, k_g, *, batch):
    est = 2 * (N_MEM * D_MODEL * 4 + D_MODEL * 2 * D_MODEL * 2 + 2 * N_MEM * D_MODEL * 2) + 4 * N_MEM * 2 * D_MODEL * 4
    return pl.pallas_call(
        _mem_kv_kernel,
        grid=(batch,),
        in_specs=[
            pl.BlockSpec((N_MEM, D_MODEL), lambda b: (b, 0)),
            pl.BlockSpec((1, D_MODEL), lambda b: (0, 0)),
            pl.BlockSpec((D_MODEL, 2 * D_MODEL), lambda b: (0, 0)),
            pl.BlockSpec((1, CROSS_HEAD_DIM), lambda b: (0, 0)),
        ],
        out_specs=[pl.BlockSpec((1, N_MEM, D_MODEL), lambda b: (b, 0, 0)),
                   pl.BlockSpec((1, N_MEM, D_MODEL), lambda b: (b, 0, 0))],
        out_shape=[jax.ShapeDtypeStruct((batch, N_MEM, D_MODEL), BF16),
                   jax.ShapeDtypeStruct((batch, N_MEM, D_MODEL), BF16)],
        compiler_params=pltpu.CompilerParams(
            dimension_semantics=("arbitrary",), vmem_limit_bytes=_vmem_limit(est)),
        name="mem_kv",
    )(mem2d, g_mem, w_ckv, k_g)


def _route(logits_t):
    gl = logits_t[0:N_GROUPS]
    gmax = jnp.max(gl, axis=0, keepdims=True)
    eg = jnp.exp(gl - gmax)
    p_group = eg / jnp.sum(eg, axis=0, keepdims=True)
    p_g = jnp.max(p_group, axis=0, keepdims=True)
    g_idx = jnp.full_like(p_g, float(N_GROUPS))
    for g in reversed(range(N_GROUPS)):
        g_idx = jnp.where(p_group[g:g + 1] == p_g, float(g), g_idx)

    sel = []
    for j in range(EXPERTS_PER_GROUP):
        acc = jnp.zeros_like(p_g)
        for g in range(N_GROUPS):
            r = N_GROUPS + g * EXPERTS_PER_GROUP + j
            acc = jnp.where(g_idx == float(g), logits_t[r:r + 1], acc)
        sel.append(acc)
    smax = functools.reduce(jnp.maximum, sel)
    es = [jnp.exp(s - smax) for s in sel]
    den = functools.reduce(jnp.add, es)
    p_in = [e / den for e in es]

    v0 = functools.reduce(jnp.maximum, p_in)
    i0 = jnp.full_like(v0, float(EXPERTS_PER_GROUP))
    for j in reversed(range(EXPERTS_PER_GROUP)):
        i0 = jnp.where(p_in[j] == v0, float(j), i0)
    rest = [jnp.where(i0 == float(j), -1.0, p_in[j]) for j in range(EXPERTS_PER_GROUP)]
    v1 = functools.reduce(jnp.maximum, rest)
    i1 = jnp.full_like(v1, float(EXPERTS_PER_GROUP))
    for j in reversed(range(EXPERTS_PER_GROUP)):
        i1 = jnp.where(rest[j] == v1, float(j), i1)

    tot = v0 + v1
    w0 = p_g * (v0 / tot)
    w1 = p_g * (v1 / tot)
    first = i0 < i1
    a = jnp.where(first, i0, i1)
    b = jnp.where(first, i1, i0)
    w_lo = jnp.where(first, w0, w1)
    w_hi = jnp.where(first, w1, w0)
    pair = jnp.where(a == 0.0, b - 1.0, jnp.where(a == 1.0, jnp.where(b == 3.0, 3.0, 4.0), 5.0))
    swap = a == 2.0
    wa = jnp.where(swap, w_hi, w_lo)
    wb = jnp.where(swap, w_lo, w_hi)
    bucket = g_idx * float(len(PAIRS)) + pair
    return bucket, wa, wb


def _cross_router_kernel(q_ref, k_ref, v_ref, x1_ref, wco_ref, gf_ref, whi_ref, wlo_ref, rb_ref,
                         xa_ref, rt_ref, *, tm):
    q = q_ref[...]
    k = k_ref[0]
    v = v_ref[0]
    outs = []
    for hh in range(N_HEADS):
        sl = slice(hh * CROSS_HEAD_DIM, (hh + 1) * CROSS_HEAD_DIM)
        s = _nt_dot(q[:, sl], k[:, sl])
        p = jnp.exp(s - jnp.max(s, axis=-1, keepdims=True))
        l = jnp.sum(p, axis=-1, keepdims=True)
        o = jnp.dot(p.astype(BF16), v[:, sl], preferred_element_type=F32) / l
        outs.append(o.astype(BF16))
    o = jnp.concatenate(outs, axis=1)
    x2 = x1_ref[...] + jnp.dot(o, wco_ref[...], preferred_element_type=F32)
    xa_ref[:, :D_MODEL] = x2

    h3 = _rms(x2, gf_ref[...])
    hi = h3.astype(BF16)
    lo = (h3 - hi.astype(F32)).astype(BF16)
    lt = _nt_dot(whi_ref[...], hi) + (_nt_dot(whi_ref[...], lo) + _nt_dot(wlo_ref[...], hi))
    lt = lt + rb_ref[...]
    bucket, wa, wb = _route(lt)

    sub = lax.broadcasted_iota(I32, (8, tm), 0)
    rows = jnp.where(sub == 0, bucket, jnp.where(sub == 1, wa, jnp.where(sub == 2, wb, 0.0)))
    rt_ref[0] = rows
    sub_a = lax.broadcasted_iota(I32, (AUX_COLS, tm), 0)
    aux_t = jnp.where(sub_a == 0, bucket, jnp.where(sub_a == 1, wa, jnp.where(sub_a == 2, wb, 0.0)))
    xa_ref[:, D_MODEL:] = aux_t.T


def _cross_router(qc, kc, vc, x1, w_co, g_ffn, wr_hi, wr_lo, rb, *, batch, seq):
    tm = min(CROSS_ROWS, seq)
    ns = seq // tm
    n = batch * seq
    est = 2 * (tm * D_MODEL * 2 + 2 * N_MEM * D_MODEL * 2 + tm * D_MODEL * 4 + D_MODEL * D_MODEL * 2
               + tm * ROW_W * 4) + 10 * tm * D_MODEL * 4
    return pl.pallas_call(
        functools.partial(_cross_router_kernel, tm=tm),
        grid=(batch, ns),
        in_specs=[
            pl.BlockSpec((tm, D_MODEL), lambda b, s: (b * ns + s, 0)),
            pl.BlockSpec((1, N_MEM, D_MODEL), lambda b, s: (b, 0, 0)),
            pl.BlockSpec((1, N_MEM, D_MODEL), lambda b, s: (b, 0, 0)),
            pl.BlockSpec((tm, D_MODEL), lambda b, s: (b * ns + s, 0)),
            pl.BlockSpec((D_MODEL, D_MODEL), lambda b, s: (0, 0)),
            pl.BlockSpec((1, D_MODEL), lambda b, s: (0, 0)),
            pl.BlockSpec((32, D_MODEL), lambda b, s: (0, 0)),
            pl.BlockSpec((32, D_MODEL), lambda b, s: (0, 0)),
            pl.BlockSpec((32, 1), lambda b, s: (0, 0)),
        ],
        out_specs=[pl.BlockSpec((tm, ROW_W), lambda b, s: (b * ns + s, 0)),
                   pl.BlockSpec((1, 8, tm), lambda b, s: (b * ns + s, 0, 0))],
        out_shape=[jax.ShapeDtypeStruct((n, ROW_W), F32),
                   jax.ShapeDtypeStruct((n // tm, 8, tm), F32)],
        compiler_params=pltpu.CompilerParams(
            dimension_semantics=("arbitrary", "arbitrary"), vmem_limit_bytes=_vmem_limit(est)),
        name="cross_router",
    )(qc, kc, vc, x1, w_co, g_ffn, wr_hi, wr_lo, rb)


def _rank_kernel(rt_ref, rank_ref, cnt_ref, carry_sc, *, tm):
    @pl.when(pl.program_id(0) == 0)
    def _():
        carry_sc[...] = jnp.zeros_like(carry_sc)

    bucket = rt_ref[0, 0:1, :]
    sub = lax.broadcasted_iota(I32, (32, tm), 0).astype(F32)
    hit = sub == bucket
    r = lax.broadcasted_iota(I32, (tm, tm), 0)
    c = lax.broadcasted_iota(I32, (tm, tm), 1)
    upper = jnp.where(r <= c, 1.0, 0.0).astype(BF16)
    cum = jnp.dot(jnp.where(hit, 1.0, 0.0).astype(BF16), upper, preferred_element_type=F32)
    carry = carry_sc[:, 0:1]
    rank = jnp.sum(jnp.where(hit, cum - 1.0 + carry, 0.0), axis=0, keepdims=True)
    rank_ref[0] = rank.astype(I32)
    new_carry = carry_sc[...] + jnp.broadcast_to(cum[:, tm - 1:tm], carry_sc.shape)
    carry_sc[...] = new_carry
    cnt_ref[...] = new_carry


def _rank(rt):
    nt, _, tm = rt.shape
    return pl.pallas_call(
        functools.partial(_rank_kernel, tm=tm),
        grid=(nt,),
        in_specs=[pl.BlockSpec((1, 8, tm), lambda i: (i, 0, 0))],
        out_specs=[pl.BlockSpec((1, 1, tm), lambda i: (i, 0, 0)),
                   pl.BlockSpec((32, LANES), lambda i: (0, 0))],
        out_shape=[jax.ShapeDtypeStruct((nt, 1, tm), I32), jax.ShapeDtypeStruct((32, LANES), F32)],
        scratch_shapes=[pltpu.VMEM((32, LANES), F32)],
        compiler_params=pltpu.CompilerParams(dimension_semantics=("arbitrary",)),
        name="rank",
    )(rt)


def _dispatch_kernel(pos_ref, zrow_ref, zflag_ref, nact_ref, xa_ref, xs_ref, zbuf, zsem, sem, *,
                     n_tiles, rows):
    @pl.when(pl.program_id(0) == 0)
    def _():
        _zero_partial_tiles(zrow_ref, zflag_ref, nact_ref, xs_ref, zbuf, zsem, n_tiles=n_tiles)

    base = pl.program_id(0) * rows
    for r in range(rows):
        pltpu.make_async_copy(xa_ref.at[pl.ds(r, 1)], xs_ref.at[pl.ds(pos_ref[base + r], 1)], sem).start(
            priority=r % 2)
    pltpu.make_async_copy(xa_ref, xs_ref.at[pl.ds(0, rows)], sem).wait()


def _zero_partial_tiles(zrow_ref, zflag_ref, nact_ref, xs_ref, zbuf, zsem, *, n_tiles):
    zbuf[...] = jnp.zeros_like(zbuf)

    def zero_copy(row):
        return pltpu.make_async_copy(zbuf, xs_ref.at[pl.ds(pl.multiple_of(row, MOE_ROWS), MOE_ROWS)], zsem)

    def bucket_tiles(op):
        def body(b, carry):
            @pl.when(zflag_ref[b] > 0)
            def _():
                op(zero_copy(zrow_ref[b]))
            return carry
        lax.fori_loop(0, N_PAIR_BUCKETS, body, 0)

    def unused_tiles(op):
        def body(i, carry):
            op(zero_copy(i * MOE_ROWS))
            return carry
        lax.fori_loop(nact_ref[0], n_tiles, body, 0)

    bucket_tiles(lambda cp: cp.start())
    unused_tiles(lambda cp: cp.start())
    bucket_tiles(lambda cp: cp.wait())
    unused_tiles(lambda cp: cp.wait())


def _dispatch(pos, zrow, zflag, n_active, xa, *, n_tiles):
    n_tok = xa.shape[0]
    n_sorted = n_tiles * MOE_ROWS
    rows = min(DMA_ROWS, n_tok)
    return pl.pallas_call(
        functools.partial(_dispatch_kernel, n_tiles=n_tiles, rows=rows),
        grid_spec=pltpu.PrefetchScalarGridSpec(
            num_scalar_prefetch=4,
            grid=(n_tok // rows,),
            in_specs=[pl.BlockSpec((rows, ROW_W), lambda i, *_: (i, 0))],
            out_specs=pl.BlockSpec(memory_space=pl.ANY),
            scratch_shapes=[pltpu.VMEM((MOE_ROWS, ROW_W), F32), pltpu.SemaphoreType.DMA(()),
                            pltpu.SemaphoreType.DMA(())],
        ),
        out_shape=jax.ShapeDtypeStruct((n_sorted, ROW_W), F32),
        compiler_params=pltpu.CompilerParams(
            dimension_semantics=("arbitrary",), has_side_effects=True,
            vmem_limit_bytes=_vmem_limit((2 * rows + MOE_ROWS) * ROW_W * 4)),
        name="dispatch",
    )(pos, zrow, zflag, n_active, xa)


def _moe_kernel(ta_ref, tb_ref, xi_ref, nact_ref, xs_ref, gf_ref, wga_ref, wua_ref, wda_ref,
                wgb_ref, wub_ref, wdb_ref, ys_ref):
    i = pl.program_id(0)

    @pl.when(i < nact_ref[0])
    def _():
        x2 = xs_ref[:, :D_MODEL]
        aux = xs_ref[:, D_MODEL:]
        wa = aux[:, 1:2]
        wb = aux[:, 2:3]
        h = _rms(x2, gf_ref[...]).astype(BF16)

        def mlp(wg, wu, wd):
            g = jnp.dot(h, wg[0].astype(BF16), preferred_element_type=F32)
            u = jnp.dot(h, wu[0].astype(BF16), preferred_element_type=F32)
            act = (g / (1.0 + jnp.exp(-g))) * u
            return jnp.dot(act.astype(BF16), wd[0].astype(BF16), preferred_element_type=F32)

        ys_ref[...] = x2 + (wa * mlp(wga_ref, wua_ref, wda_ref) + wb * mlp(wgb_ref, wub_ref, wdb_ref))

    @pl.when(i >= nact_ref[0])
    def _():
        ys_ref[...] = jnp.zeros_like(ys_ref)


def _moe(tile_a, tile_b, tile_x, n_active, xs, g_ffn, w_gate, w_up, w_down):
    n_tiles = xs.shape[0] // MOE_ROWS
    t = MOE_ROWS
    wa_map = lambda i, ta, tb, xi, na: (ta[i], 0, 0)
    wb_map = lambda i, ta, tb, xi, na: (tb[i], 0, 0)
    wspec = pl.BlockSpec
    est = 2 * (t * ROW_W * 4 + 6 * D_MODEL * D_EXPERT * 4 + t * D_MODEL * 4) + 10 * t * D_MODEL * 4
    return pl.pallas_call(
        _moe_kernel,
        grid_spec=pltpu.PrefetchScalarGridSpec(
            num_scalar_prefetch=4,
            grid=(n_tiles,),
            in_specs=[
                pl.BlockSpec((t, ROW_W), lambda i, ta, tb, xi, na: (xi[i], 0)),
                pl.BlockSpec((1, D_MODEL), lambda i, ta, tb, xi, na: (0, 0)),
                wspec((1, D_MODEL, D_EXPERT), wa_map),
                wspec((1, D_MODEL, D_EXPERT), wa_map),
                wspec((1, D_EXPERT, D_MODEL), wa_map),
                wspec((1, D_MODEL, D_EXPERT), wb_map),
                wspec((1, D_MODEL, D_EXPERT), wb_map),
                wspec((1, D_EXPERT, D_MODEL), wb_map),
            ],
            out_specs=pl.BlockSpec((t, D_MODEL), lambda i, ta, tb, xi, na: (i, 0)),
        ),
        out_shape=jax.ShapeDtypeStruct((n_tiles * t, D_MODEL), F32),
        compiler_params=pltpu.CompilerParams(
            dimension_semantics=("arbitrary",), vmem_limit_bytes=_vmem_limit(est)),
        name="moe",
    )(tile_a, tile_b, tile_x, n_active, xs, g_ffn, w_gate, w_up, w_down, w_gate, w_up, w_down)


def _combine_kernel(pos_ref, ys_ref, out_ref, sem, *, rows):
    base = pl.program_id(0) * rows
    for r in range(rows):
        pltpu.make_async_copy(ys_ref.at[pl.ds(pos_ref[base + r], 1)], out_ref.at[pl.ds(r, 1)], sem).start(
            priority=r % 2)
    pltpu.make_async_copy(ys_ref.at[pl.ds(0, rows)], out_ref, sem).wait()


def _combine(pos, ys, *, n_tok):
    rows = min(DMA_ROWS, n_tok)
    return pl.pallas_call(
        functools.partial(_combine_kernel, rows=rows),
        grid_spec=pltpu.PrefetchScalarGridSpec(
            num_scalar_prefetch=1,
            grid=(n_tok // rows,),
            in_specs=[pl.BlockSpec(memory_space=pl.ANY)],
            out_specs=pl.BlockSpec((rows, D_MODEL), lambda i, *_: (i, 0)),
            scratch_shapes=[pltpu.SemaphoreType.DMA(())],
        ),
        out_shape=jax.ShapeDtypeStruct((n_tok, D_MODEL), F32),
        compiler_params=pltpu.CompilerParams(
            dimension_semantics=("arbitrary",), has_side_effects=True,
            vmem_limit_bytes=_vmem_limit(2 * rows * D_MODEL * 4)),
        name="combine",
    )(pos, ys)


def _t5_bucket(rel):
    nb = N_BUCKETS_T5 // 2
    max_exact = nb // 2
    ret = (rel > 0).astype(I32) * nb
    n = jnp.abs(rel)
    nf = jnp.maximum(n, 1).astype(F32)
    large = max_exact + (jnp.log(nf / max_exact) / math.log(MAX_DISTANCE / max_exact)
                         * (nb - max_exact)).astype(I32)
    large = jnp.minimum(large, nb - 1)
    return ret + jnp.where(n < max_exact, n, large)


def _diff_bias(rel_bias, t):
    assert t % CHUNK == 0 and t >= MAX_DISTANCE
    n_rel = 3 * t
    vec = rel_bias[_t5_bucket(jnp.arange(n_rel, dtype=I32) - (2 * t - 1))].astype(F32).T
    toe = jnp.tile(vec, (1, t))[:, :t * (n_rel - 1)].reshape(N_HEADS, t, n_rel - 1)
    qpos = jnp.arange(t, dtype=I32)[:, None]
    kpos = jnp.arange(t, dtype=I32)[None, :]
    far = rel_bias[_t5_bucket(jnp.asarray(-2 * t, I32))].astype(F32)[:, None, None]
    b0 = (toe[:, :, 2 * t - 1:3 * t - 1] - far) * LOG2E
    b0 = jnp.where((kpos // CHUNK <= qpos // CHUNK)[None], b0, NEG_INF)
    b1 = (toe[:, :, t - 1:2 * t - 1] - far) * LOG2E
    return jnp.swapaxes(jnp.stack([b0, b1], axis=1), -1, -2)


def _tile_plan(counts, n_tiles):
    t = MOE_ROWS
    tiles = (counts + t - 1) // t
    ends = jnp.cumsum(tiles)
    starts = ends - tiles
    n_active = ends[-1]
    tile_idx = jnp.minimum(jnp.arange(n_tiles, dtype=I32), n_active - 1)
    tile_bucket = jnp.minimum(jnp.sum((ends[None, :] <= tile_idx[:, None]).astype(I32), axis=1), N_PAIR_BUCKETS - 1)
    pa = jnp.asarray([p[0] for p in PAIRS], I32)
    pb = jnp.asarray([p[1] for p in PAIRS], I32)
    grp = tile_bucket // len(PAIRS)
    pair = tile_bucket % len(PAIRS)
    tile_a = grp * EXPERTS_PER_GROUP + pa[pair]
    tile_b = grp * EXPERTS_PER_GROUP + pb[pair]
    row_off = (starts * t).astype(I32)
    zrow = ((ends - 1) * t).astype(I32)
    zflag = (counts > 0).astype(I32)
    return row_off, zrow, zflag, tile_a.astype(I32), tile_b.astype(I32), tile_idx.astype(I32), n_active.astype(I32)


def kernel(x, mem, rel_bias, norm_mix_g, w_in, b_forget, diff_q_norm_g, diff_k_norm_g, diff_lambda_q1, diff_lambda_k1, diff_lambda_q2, diff_lambda_k2, diff_subln_g, fox_q_norm_g, fox_k_norm_g, fox_out_norm_g, w_out, norm_cross_g, norm_mem_g, w_cq, w_ckv, cross_q_norm_g, cross_k_norm_g, w_co, norm_ffn_g, w_group_router, b_group_router, w_expert_router, b_expert_router, w_exp_gate, w_exp_up, w_exp_down):
    batch, seq, d = x.shape
    assert d == D_MODEL and norm_mix_g.shape[0] == 1 and mem.shape[1] == N_MEM
    n_tok = batch * seq
    assert seq % ATTN_TILE == 0 or seq < ATTN_TILE
    l = 0

    w_qk = jnp.concatenate([w_in[l, :, :1024], w_in[l, :, 1536:2560]], axis=1).astype(BF16)
    w_vt = jnp.concatenate([w_in[l, :, 1024:1536], w_in[l, :, 2560:3072]], axis=1).T.astype(BF16)
    wft = jnp.zeros((8, D_MODEL), F32).at[:N_HEADS].set(w_in[l, :, 3072:].T).astype(BF16)
    bf = jnp.zeros((8, 1), F32).at[:N_HEADS, 0].set(b_forget[l].astype(F32))
    ones = jnp.ones((HEAD_DIM,), F32)
    qg = jnp.stack([
        jnp.tile(diff_q_norm_g[l].astype(F32), 2) * (DIFF_QK_DIM ** -0.5 * LOG2E),
        jnp.tile(diff_k_norm_g[l].astype(F32), 2),
        fox_q_norm_g[l].astype(F32) * (HEAD_DIM ** -0.5 * LOG2E), fox_k_norm_g[l].astype(F32),
        ones, ones, ones, ones])
    lam = (jnp.exp(jnp.sum(diff_lambda_q1[l].astype(F32) * diff_lambda_k1[l].astype(F32)))
           - jnp.exp(jnp.sum(diff_lambda_q2[l].astype(F32) * diff_lambda_k2[l].astype(F32)))
           + LAM_INIT).reshape(1)
    t_attn = min(ATTN_TILE, seq)
    bias_tiles_t = _diff_bias(rel_bias, t_attn)
    wr = jnp.zeros((32, D_MODEL), F32)
    wr = wr.at[:N_GROUPS].set(w_group_router[l].T).at[N_GROUPS:N_GROUPS + N_EXPERTS].set(w_expert_router[l].T)
    wr_hi = wr.astype(BF16)
    wr_lo = (wr - wr_hi.astype(F32)).astype(BF16)
    rb = jnp.zeros((32, 1), F32)
    rb = rb.at[:N_GROUPS, 0].set(b_group_router[l]).at[N_GROUPS:N_GROUPS + N_EXPERTS, 0].set(b_expert_router[l])

    row = lambda v: v.astype(F32).reshape(1, -1)
    x2d = x.reshape(n_tok, D_MODEL)

    qk, vt, cum = _mix_proj(x2d, row(norm_mix_g[l]), w_qk, w_vt, wft, bf, qg, batch=batch, seq=seq)
    cum4 = cum[:, :N_HEADS].reshape(batch, N_HEADS, 1, seq)
    gain_t = lambda v: jnp.broadcast_to(v.astype(F32)[:, None], (HEAD_DIM, t_attn))
    mixed = _attn(qk, vt, bias_tiles_t, lam, cum4, gain_t(diff_subln_g[l]), gain_t(fox_out_norm_g[l]),
                  batch=batch, seq=seq)
    x1, qc = _out_q(x2d, mixed, w_out[l].astype(BF16), row(norm_cross_g[l]),
                    w_cq[l].astype(BF16), row(cross_q_norm_g[l]))

    kc, vc = _mem_kv(mem.reshape(batch * N_MEM, D_MODEL), row(norm_mem_g[l]), w_ckv[l].astype(BF16),
                     row(cross_k_norm_g[l]), batch=batch)
    xa, rt = _cross_router(qc, kc, vc, x1, w_co[l].astype(BF16), row(norm_ffn_g[l]), wr_hi, wr_lo, rb,
                           batch=batch, seq=seq)

    rank, cnt = _rank(rt)
    counts = cnt[:N_PAIR_BUCKETS, 0].astype(I32)
    n_tiles = n_tok // MOE_ROWS + N_PAIR_BUCKETS
    row_off, zrow, zflag, tile_a, tile_b, tile_x, n_active = _tile_plan(counts, n_tiles)
    bucket = rt[:, 0, :].reshape(n_tok).astype(I32)
    pos = row_off[bucket] + rank.reshape(n_tok)
    n_active = n_active.reshape(1)
    xs = _dispatch(pos, zrow, zflag, n_active, xa, n_tiles=n_tiles)
    ys = _moe(tile_a, tile_b, tile_x, n_active, xs, row(norm_ffn_g[l]),
              w_exp_gate[l].astype(F32), w_exp_up[l].astype(F32), w_exp_down[l].astype(F32))
    out = _combine(pos, ys, n_tok=n_tok)
    return out.reshape(batch, seq, D_MODEL)
```

```python
import functools
import math

import jax
import jax.numpy as jnp
import numpy as np
from jax import lax
from jax.experimental import pallas as pl
from jax.experimental.pallas import tpu as pltpu

F32 = jnp.float32
BF16 = jnp.bfloat16
I32 = jnp.int32

D_MODEL = 1024
CHUNK = 64
N_MEM = 256
N_HEADS = 4
HEAD_DIM = 128
VT_ROWS = HEAD_DIM + 16
DIFF_QK_DIM = 64
CROSS_HEAD_DIM = 256
N_BUCKETS_T5 = 32
MAX_DISTANCE = 128
N_GROUPS = 4
EXPERTS_PER_GROUP = 4
N_EXPERTS = N_GROUPS * EXPERTS_PER_GROUP
D_EXPERT = 512
EPS = 1e-6
NEG_INF = -1e30
LAM_INIT = 0.8 - 0.6 * math.exp(-0.3 * 0)
LOG2E = math.log2(math.e)

PAIRS = ((0, 1), (0, 2), (0, 3), (1, 3), (1, 2), (3, 2))
N_PAIR_BUCKETS = N_GROUPS * len(PAIRS)

LANES = 128
VMEM_LIMIT_CAP = 56 * 1024 * 1024

PROJ_ROWS = 1024
ATTN_TILE = 256
CROSS_ROWS = 512
MOE_ROWS = 256
AUX_COLS = LANES
ROW_W = D_MODEL + AUX_COLS
DMA_ROWS = 2048


def _vmem_limit(nbytes):
    return int(min(max(nbytes * 5 // 4, 32 * 1024 * 1024), VMEM_LIMIT_CAP))


def _nt_dot(a, b):
    return lax.dot_general(a, b, (((1,), (1,)), ((), ())), preferred_element_type=F32)


def _rms(x, g):
    ms = jnp.mean(x * x, axis=-1, keepdims=True)
    return x * lax.rsqrt(ms + EPS) * g


def _mix_proj_kernel(x_ref, g_ref, w_ref, wvt_ref, wft_ref, bf_ref, qg_ref, o_ref, vt_ref, cum_ref, carry_ref,
                     *, tm):
    si = pl.program_id(1)
    h = _rms(x_ref[...], g_ref[...]).astype(BF16)
    lane = lax.broadcasted_iota(I32, (tm, HEAD_DIM), 1)
    lo = lane < DIFF_QK_DIM
    for kind in range(4):
        p = jnp.dot(h, w_ref[:, kind * 512:(kind + 1) * 512], preferred_element_type=F32)
        for hh in range(N_HEADS):
            ph = p[:, hh * HEAD_DIM:(hh + 1) * HEAD_DIM]
            if kind in (0, 1):
                sq = ph * ph
                s_lo = jnp.sum(jnp.where(lo, sq, 0.0), axis=-1, keepdims=True)
                s_hi = jnp.sum(jnp.where(lo, 0.0, sq), axis=-1, keepdims=True)
                ms = jnp.where(lo, s_lo, s_hi) * (1.0 / DIFF_QK_DIM)
                ph = ph * lax.rsqrt(ms + EPS) * qg_ref[kind:kind + 1, :]
            else:
                ph = _rms(ph, qg_ref[kind:kind + 1, :])
            o_ref[0, kind * N_HEADS + hh] = ph.astype(BF16)
    for kind in range(2):
        pt = _nt_dot(wvt_ref[kind * 512:(kind + 1) * 512, :], h)
        for hh in range(N_HEADS):
            vt_ref[0, kind * N_HEADS + hh, :HEAD_DIM, :] = pt[hh * HEAD_DIM:(hh + 1) * HEAD_DIM, :].astype(BF16)
            vt_ref[0, kind * N_HEADS + hh, HEAD_DIM:, :] = jnp.ones((VT_ROWS - HEAD_DIM, tm), BF16)

    z = _nt_dot(wft_ref[...], h) + bf_ref[...]
    logf = jnp.minimum(z, 0.0) - jnp.log(1.0 + jnp.exp(-jnp.abs(z)))
    lane8 = lax.broadcasted_iota(I32, (8, tm), 1)
    c = logf * LOG2E
    k = 1
    while k < tm:
        c = c + jnp.where(lane8 >= k, pltpu.roll(c, k, axis=1), 0.0)
        k *= 2

    @pl.when(si == 0)
    def _():
        carry_ref[...] = jnp.zeros_like(carry_ref)

    c = c + carry_ref[:, 0:1]
    cum_ref[0] = c
    carry_ref[...] = jnp.broadcast_to(c[:, tm - 1:tm], carry_ref.shape)


def _mix_proj(x2d, g, w_qk, w_vt, wft, bf, qg, *, batch, seq):
    tm = min(PROJ_ROWS, seq)
    ns = seq // tm
    est = 2 * (tm * D_MODEL * 4 + D_MODEL * 3072 * 2 + 24 * tm * HEAD_DIM * 2) + 8 * tm * D_MODEL
    return pl.pallas_call(
        functools.partial(_mix_proj_kernel, tm=tm),
        grid=(batch, ns),
        in_specs=[
            pl.BlockSpec((tm, D_MODEL), lambda b, s: (b * ns + s, 0)),
            pl.BlockSpec((1, D_MODEL), lambda b, s: (0, 0)),
            pl.BlockSpec((D_MODEL, 2048), lambda b, s: (0, 0)),
            pl.BlockSpec((1024, D_MODEL), lambda b, s: (0, 0)),
            pl.BlockSpec((8, D_MODEL), lambda b, s: (0, 0)),
            pl.BlockSpec((8, 1), lambda b, s: (0, 0)),
            pl.BlockSpec((8, HEAD_DIM), lambda b, s: (0, 0)),
        ],
        out_specs=[
            pl.BlockSpec((1, 16, tm, HEAD_DIM), lambda b, s: (b, 0, s, 0)),
            pl.BlockSpec((1, 8, VT_ROWS, tm), lambda b, s: (b, 0, 0, s)),
            pl.BlockSpec((1, 8, tm), lambda b, s: (b, 0, s)),
        ],
        out_shape=[
            jax.ShapeDtypeStruct((batch, 16, seq, HEAD_DIM), BF16),
            jax.ShapeDtypeStruct((batch, 8, VT_ROWS, seq), BF16),
            jax.ShapeDtypeStruct((batch, 8, seq), F32),
        ],
        scratch_shapes=[pltpu.VMEM((8, LANES), F32)],
        compiler_params=pltpu.CompilerParams(
            dimension_semantics=("arbitrary", "arbitrary"), vmem_limit_bytes=_vmem_limit(est)),
        name="mix_proj",
    )(x2d, g, w_qk, w_vt, wft, bf, qg)


def _init_softmax(m_sc, acc_sc):
    m_sc[...] = jnp.full(m_sc.shape, NEG_INF, F32)
    acc_sc[...] = jnp.zeros_like(acc_sc)


N_CHAINS = 3 * N_HEADS


def _attn_kernel(lam_ref, dq_ref, dk_ref, dvt_ref, fq_ref, fk_ref, fvt_ref, bias_ref, c_ref,
                 gd_ref, gf_ref, o_ref, m_sc, acc_sc, qm_sc, ccol_sc, s_sc, mt_sc, s2_sc, mt2_sc, *, t, seq):
    qi = pl.program_id(1)
    _init_softmax(m_sc, acc_sc)
    key = lax.broadcasted_iota(I32, (t, t), 0)
    qry = lax.broadcasted_iota(I32, (t, t), 1)

    lane = lax.broadcasted_iota(I32, (t, HEAD_DIM), 1)
    for hh in range(N_HEADS):
        q = dq_ref[0, hh]
        zero = jnp.zeros_like(q)
        qm_sc[2 * hh] = jnp.where(lane < DIFF_QK_DIM, q, zero)
        qm_sc[2 * hh + 1] = jnp.where(lane < DIFF_QK_DIM, zero, q)

    @pl.when(qi == 0)
    def _():
        for hh in range(N_HEADS):
            for j in range(seq // t):
                row = c_ref[0, hh, :, j * t:(j + 1) * t]
                ccol_sc[hh, j * t:(j + 1) * t, :] = jnp.sum(
                    jnp.where(key == qry, jnp.broadcast_to(row, (t, t)), 0.0), axis=1, keepdims=True)

    q_off = pl.multiple_of(qi * t, t)

    bufs = ((s_sc, mt_sc), (s2_sc, mt2_sc))

    def logits(j, where, par, slot):
        off = pl.multiple_of(j * t, t)
        s_buf, mt_buf = bufs[par]
        if slot < 2 * N_HEADS:
            hh = slot // 2
            s_t = _nt_dot(dk_ref[0, hh, pl.ds(off, t), :], qm_sc[slot])
            if where != "far":
                s_t = s_t + bias_ref[hh, 1 if where == "near" else 0]
            s_buf[slot] = s_t
            mt_buf[slot] = jnp.max(s_t, axis=0, keepdims=True)
        else:
            hh = slot - 2 * N_HEADS
            c_k = ccol_sc[hh, pl.ds(off, t), :]
            s_t = _nt_dot(fk_ref[0, hh, pl.ds(off, t), :], fq_ref[0, hh]) - c_k
            if where == "diag":
                s_t = jnp.where(key <= qry, s_t, NEG_INF)
            s_buf[slot] = s_t
            mt_buf[slot] = jnp.max(s_t, axis=0, keepdims=True) + c_ref[0, hh, :, pl.ds(q_off, t)]

    def update(j, par, slot):
        off = pl.multiple_of(j * t, t)
        s_buf, mt_buf = bufs[par]
        vt_ref = dvt_ref if slot < 2 * N_HEADS else fvt_ref
        hh = slot // 2 if slot < 2 * N_HEADS else slot - 2 * N_HEADS
        v_t = vt_ref[0, hh, :, pl.ds(off, t)]
        m_prev = m_sc[slot]
        m_new = jnp.maximum(m_prev, mt_buf[slot])
        alpha = jnp.exp2(m_prev - m_new)
        if slot < 2 * N_HEADS:
            p_t = jnp.exp2(s_buf[slot] - m_new)
        else:
            p_t = jnp.exp2(s_buf[slot] - (m_new - c_ref[0, hh, :, pl.ds(q_off, t)]))
        acc_sc[slot] = alpha * acc_sc[slot] + jnp.dot(v_t, p_t.astype(BF16), preferred_element_type=F32)
        m_sc[slot] = m_new

    def phase(upd=None, nxt=None):
        for slot in range(N_CHAINS):
            if nxt is not None:
                logits(*nxt, slot)
            if upd is not None:
                update(*upd, slot)

    n_far = jnp.maximum(qi - 1, 0)
    far_tile = lambda k: jnp.maximum(qi - 2 - k, 0)

    phase(nxt=(qi, "diag", 0))

    @pl.when(qi == 0)
    def _():
        phase(upd=(qi, 0))

    @pl.when(qi == 1)
    def _():
        phase(upd=(qi, 0), nxt=(qi - 1, "near", 1))
        phase(upd=(qi - 1, 1))

    @pl.when(qi >= 2)
    def _():
        phase(upd=(qi, 0), nxt=(qi - 1, "near", 1))
        phase(upd=(qi - 1, 1), nxt=(far_tile(0), "far", 0))

        def far_pair(i, carry):
            k = 2 * i
            phase(upd=(far_tile(k), 0), nxt=(far_tile(k + 1), "far", 1))

            @pl.when(k + 1 < n_far)
            def _():
                phase(upd=(far_tile(k + 1), 1), nxt=(far_tile(k + 2), "far", 0))
            return carry

        lax.fori_loop(0, (n_far + 1) // 2, far_pair, 0)

    def normalised(slot):
        return acc_sc[slot, :HEAD_DIM, :] / acc_sc[slot, HEAD_DIM:HEAD_DIM + 1, :]

    def rms_t(o_t, g_t):
        ms = jnp.mean(o_t * o_t, axis=0, keepdims=True)
        return o_t * lax.rsqrt(ms + EPS) * g_t

    for hh in range(N_HEADS):
        o_t = normalised(2 * hh) - lam_ref[0] * normalised(2 * hh + 1)
        o_ref[:, hh * HEAD_DIM:(hh + 1) * HEAD_DIM] = (
            rms_t(o_t, gd_ref[...]) * (1.0 - LAM_INIT)).T.astype(BF16)
    for hh in range(N_HEADS):
        o_t = normalised(2 * N_HEADS + hh)
        o_ref[:, (N_HEADS + hh) * HEAD_DIM:(N_HEADS + hh + 1) * HEAD_DIM] = rms_t(o_t, gf_ref[...]).T.astype(BF16)


def _attn(qk, vt, bias_tiles_t, lam, cum4, subln_g, fox_g, *, batch, seq):
    t = min(ATTN_TILE, seq)
    nq = seq // t
    head_blk = N_HEADS * seq * HEAD_DIM * 2
    est = (2 * (2 * N_HEADS * t * HEAD_DIM * 2 + 4 * head_blk + N_HEADS * 2 * t * t * 4 + N_HEADS * 8 * seq * 4
                + t * 2 * N_HEADS * HEAD_DIM * 2)
           + N_CHAINS * (HEAD_DIM + 16) * t * 4 + 2 * N_HEADS * t * HEAD_DIM * 2 + N_HEADS * seq * LANES * 4
           + (2 * N_CHAINS + 36) * t * t * 4)
    qspec = lambda blk: pl.BlockSpec((1, N_HEADS, t, HEAD_DIM), lambda b, i: (b, blk, i, 0))
    kspec = lambda blk: pl.BlockSpec((1, N_HEADS, seq, HEAD_DIM), lambda b, i: (b, blk, 0, 0))
    vspec = lambda blk: pl.BlockSpec((1, N_HEADS, VT_ROWS, seq), lambda b, i: (b, blk, 0, 0))
    return pl.pallas_call(
        functools.partial(_attn_kernel, t=t, seq=seq),
        grid=(batch, nq),
        in_specs=[
            pl.BlockSpec(memory_space=pltpu.SMEM),
            qspec(0), kspec(1), vspec(0),
            qspec(2), kspec(3), vspec(1),
            pl.BlockSpec((N_HEADS, 2, t, t), lambda b, i: (0, 0, 0, 0)),
            pl.BlockSpec((1, N_HEADS, 1, seq), lambda b, i: (b, 0, 0, 0)),
            pl.BlockSpec((HEAD_DIM, t), lambda b, i: (0, 0)),
            pl.BlockSpec((HEAD_DIM, t), lambda b, i: (0, 0)),
        ],
        out_specs=pl.BlockSpec((t, 2 * N_HEADS * HEAD_DIM), lambda b, i: (b * nq + i, 0)),
        out_shape=jax.ShapeDtypeStruct((batch * seq, 2 * N_HEADS * HEAD_DIM), BF16),
        scratch_shapes=[pltpu.VMEM((N_CHAINS, 1, t), F32),
                        pltpu.VMEM((N_CHAINS, VT_ROWS, t), F32),
                        pltpu.VMEM((2 * N_HEADS, t, HEAD_DIM), BF16),
                        pltpu.VMEM((N_HEADS, seq, 1), F32),
                        pltpu.VMEM((N_CHAINS, t, t), F32), pltpu.VMEM((N_CHAINS, 1, t), F32),
                        pltpu.VMEM((N_CHAINS, t, t), F32), pltpu.VMEM((N_CHAINS, 1, t), F32)],
        compiler_params=pltpu.CompilerParams(
            dimension_semantics=("arbitrary", "arbitrary"), vmem_limit_bytes=_vmem_limit(est)),
        name="attn",
    )(lam, qk, qk, vt, qk, qk, vt, bias_tiles_t, cum4, subln_g, fox_g)


def _out_q_kernel(x_ref, m_ref, wo_ref, gc_ref, wq_ref, qg_ref, x1_ref, qc_ref):
    x1 = x_ref[...] + jnp.dot(m_ref[...], wo_ref[...], preferred_element_type=F32)
    x1_ref[...] = x1
    hc = _rms(x1, gc_ref[...]).astype(BF16)
    q = jnp.dot(hc, wq_ref[...], preferred_element_type=F32)
    for hh in range(N_HEADS):
        sl = slice(hh * CROSS_HEAD_DIM, (hh + 1) * CROSS_HEAD_DIM)
        qh = _rms(q[:, sl], qg_ref[...]) * (CROSS_HEAD_DIM ** -0.5)
        qc_ref[:, sl] = qh.astype(BF16)


def _out_q(x2d, mixed, w_out, g_cross, w_cq, q_g):
    n = x2d.shape[0]
    tm = min(PROJ_ROWS, n)
    est = 2 * (tm * D_MODEL * 4 * 2 + tm * D_MODEL * 2 + 2 * D_MODEL * D_MODEL * 2 + tm * D_MODEL * 2) + 6 * tm * D_MODEL * 4
    return pl.pallas_call(
        _out_q_kernel,
        grid=(n // tm,),
        in_specs=[
            pl.BlockSpec((tm, D_MODEL), lambda i: (i, 0)),
            pl.BlockSpec((tm, D_MODEL), lambda i: (i, 0)),
            pl.BlockSpec((D_MODEL, D_MODEL), lambda i: (0, 0)),
            pl.BlockSpec((1, D_MODEL), lambda i: (0, 0)),
            pl.BlockSpec((D_MODEL, D_MODEL), lambda i: (0, 0)),
            pl.BlockSpec((1, CROSS_HEAD_DIM), lambda i: (0, 0)),
        ],
        out_specs=[pl.BlockSpec((tm, D_MODEL), lambda i: (i, 0)),
                   pl.BlockSpec((tm, D_MODEL), lambda i: (i, 0))],
        out_shape=[jax.ShapeDtypeStruct((n, D_MODEL), F32), jax.ShapeDtypeStruct((n, D_MODEL), BF16)],
        compiler_params=pltpu.CompilerParams(
            dimension_semantics=("arbitrary",), vmem_limit_bytes=_vmem_limit(est)),
        name="out_q",
    )(x2d, mixed, w_out, g_cross, w_cq, q_g)


def _mem_kv_kernel(mem_ref, gm_ref, w_ref, kg_ref, k_ref, v_ref):
    mn = _rms(mem_ref[...], gm_ref[...]).astype(BF16)
    kv = jnp.dot(mn, w_ref[...], preferred_element_type=F32)
    for hh in range(N_HEADS):
        sl = slice(hh * CROSS_HEAD_DIM, (hh + 1) * CROSS_HEAD_DIM)
        k_ref[0, :, sl] = _rms(kv[:, sl], kg_ref[...]).astype(BF16)
    v_ref[0] = kv[:, D_MODEL:].astype(BF16)


def _mem_kv(mem2d, g_mem, w_ckv, k_g, *, batch):
    est = 2 * (N_MEM * D_MODEL * 4 + D_MODEL * 2 * D_MODEL * 2 + 2 * N_MEM * D_MODEL * 2) + 4 * N_MEM * 2 * D_MODEL * 4
    return pl.pallas_call(
        _mem_kv_kernel,
        grid=(batch,),
        in_specs=[
            pl.BlockSpec((N_MEM, D_MODEL), lambda b: (b, 0)),
            pl.BlockSpec((1, D_MODEL), lambda b: (0, 0)),
            pl.BlockSpec((D_MODEL, 2 * D_MODEL), lambda b: (0, 0)),
            pl.BlockSpec((1, CROSS_HEAD_DIM), lambda b: (0, 0)),
        ],
        out_specs=[pl.BlockSpec((1, N_MEM, D_MODEL), lambda b: (b, 0, 0)),
                   pl.BlockSpec((1, N_MEM, D_MODEL), lambda b: (b, 0, 0))],
        out_shape=[jax.ShapeDtypeStruct((batch, N_MEM, D_MODEL), BF16),
                   jax.ShapeDtypeStruct((batch, N_MEM, D_MODEL), BF16)],
        compiler_params=pltpu.CompilerParams(
            dimension_semantics=("arbitrary",), vmem_limit_bytes=_vmem_limit(est)),
        name="mem_kv",
    )(mem2d, g_mem, w_ckv, k_g)


def _route(logits_t):
    gl = logits_t[0:N_GROUPS]
    gmax = jnp.max(gl, axis=0, keepdims=True)
    eg = jnp.exp(gl - gmax)
    p_group = eg / jnp.sum(eg, axis=0, keepdims=True)
    p_g = jnp.max(p_group, axis=0, keepdims=True)
    g_idx = jnp.full_like(p_g, float(N_GROUPS))
    for g in reversed(range(N_GROUPS)):
        g_idx = jnp.where(p_group[g:g + 1] == p_g, float(g), g_idx)

    sel = []
    for j in range(EXPERTS_PER_GROUP):
        acc = jnp.zeros_like(p_g)
        for g in range(N_GROUPS):
            r = N_GROUPS + g * EXPERTS_PER_GROUP + j
            acc = jnp.where(g_idx == float(g), logits_t[r:r + 1], acc)
        sel.append(acc)
    smax = functools.reduce(jnp.maximum, sel)
    es = [jnp.exp(s - smax) for s in sel]
    den = functools.reduce(jnp.add, es)
    p_in = [e / den for e in es]

    v0 = functools.reduce(jnp.maximum, p_in)
    i0 = jnp.full_like(v0, float(EXPERTS_PER_GROUP))
    for j in reversed(range(EXPERTS_PER_GROUP)):
        i0 = jnp.where(p_in[j] == v0, float(j), i0)
    rest = [jnp.where(i0 == float(j), -1.0, p_in[j]) for j in range(EXPERTS_PER_GROUP)]
    v1 = functools.reduce(jnp.maximum, rest)
    i1 = jnp.full_like(v1, float(EXPERTS_PER_GROUP))
    for j in reversed(range(EXPERTS_PER_GROUP)):
        i1 = jnp.where(rest[j] == v1, float(j), i1)

    tot = v0 + v1
    w0 = p_g * (v0 / tot)
    w1 = p_g * (v1 / tot)
    first = i0 < i1
    a = jnp.where(first, i0, i1)
    b = jnp.where(first, i1, i0)
    w_lo = jnp.where(first, w0, w1)
    w_hi = jnp.where(first, w1, w0)
    pair = jnp.where(a == 0.0, b - 1.0, jnp.where(a == 1.0, jnp.where(b == 3.0, 3.0, 4.0), 5.0))
    swap = a == 2.0
    wa = jnp.where(swap, w_hi, w_lo)
    wb = jnp.where(swap, w_lo, w_hi)
    bucket = g_idx * float(len(PAIRS)) + pair
    return bucket, wa, wb


def _cross_router_kernel(q_ref, k_ref, v_ref, x1_ref, wco_ref, gf_ref, whi_ref, wlo_ref, rb_ref,
                         xa_ref, rt_ref, *, tm):
    q = q_ref[...]
    k = k_ref[0]
    v = v_ref[0]
    outs = []
    for hh in range(N_HEADS):
        sl = slice(hh * CROSS_HEAD_DIM, (hh + 1) * CROSS_HEAD_DIM)
        s = _nt_dot(q[:, sl], k[:, sl])
        p = jnp.exp(s - jnp.max(s, axis=-1, keepdims=True))
        l = jnp.sum(p, axis=-1, keepdims=True)
        o = jnp.dot(p.astype(BF16), v[:, sl], preferred_element_type=F32) / l
        outs.append(o.astype(BF16))
    o = jnp.concatenate(outs, axis=1)
    x2 = x1_ref[...] + jnp.dot(o, wco_ref[...], preferred_element_type=F32)
    xa_ref[:, :D_MODEL] = x2

    h3 = _rms(x2, gf_ref[...])
    hi = h3.astype(BF16)
    lo = (h3 - hi.astype(F32)).astype(BF16)
    lt = _nt_dot(whi_ref[...], hi) + (_nt_dot(whi_ref[...], lo) + _nt_dot(wlo_ref[...], hi))
    lt = lt + rb_ref[...]
    bucket, wa, wb = _route(lt)

    sub = lax.broadcasted_iota(I32, (8, tm), 0)
    rows = jnp.where(sub == 0, bucket, jnp.where(sub == 1, wa, jnp.where(sub == 2, wb, 0.0)))
    rt_ref[0] = rows
    sub_a = lax.broadcasted_iota(I32, (AUX_COLS, tm), 0)
    aux_t = jnp.where(sub_a == 0, bucket, jnp.where(sub_a == 1, wa, jnp.where(sub_a == 2, wb, 0.0)))
    xa_ref[:, D_MODEL:] = aux_t.T


def _cross_router(qc, kc, vc, x1, w_co, g_ffn, wr_hi, wr_lo, rb, *, batch, seq):
    tm = min(CROSS_ROWS, seq)
    ns = seq // tm
    n = batch * seq
    est = 2 * (tm * D_MODEL * 2 + 2 * N_MEM * D_MODEL * 2 + tm * D_MODEL * 4 + D_MODEL * D_MODEL * 2
               + tm * ROW_W * 4) + 10 * tm * D_MODEL * 4
    return pl.pallas_call(
        functools.partial(_cross_router_kernel, tm=tm),
        grid=(batch, ns),
        in_specs=[
            pl.BlockSpec((tm, D_MODEL), lambda b, s: (b * ns + s, 0)),
            pl.BlockSpec((1, N_MEM, D_MODEL), lambda b, s: (b, 0, 0)),
            pl.BlockSpec((1, N_MEM, D_MODEL), lambda b, s: (b, 0, 0)),
            pl.BlockSpec((tm, D_MODEL), lambda b, s: (b * ns + s, 0)),
            pl.BlockSpec((D_MODEL, D_MODEL), lambda b, s: (0, 0)),
            pl.BlockSpec((1, D_MODEL), lambda b, s: (0, 0)),
            pl.BlockSpec((32, D_MODEL), lambda b, s: (0, 0)),
            pl.BlockSpec((32, D_MODEL), lambda b, s: (0, 0)),
            pl.BlockSpec((32, 1), lambda b, s: (0, 0)),
        ],
        out_specs=[pl.BlockSpec((tm, ROW_W), lambda b, s: (b * ns + s, 0)),
                   pl.BlockSpec((1, 8, tm), lambda b, s: (b * ns + s, 0, 0))],
        out_shape=[jax.ShapeDtypeStruct((n, ROW_W), F32),
                   jax.ShapeDtypeStruct((n // tm, 8, tm), F32)],
        compiler_params=pltpu.CompilerParams(
            dimension_semantics=("arbitrary", "arbitrary"), vmem_limit_bytes=_vmem_limit(est)),
        name="cross_router",
    )(qc, kc, vc, x1, w_co, g_ffn, wr_hi, wr_lo, rb)


def _rank_kernel(rt_ref, rank_ref, cnt_ref, carry_sc, *, tm):
    @pl.when(pl.program_id(0) == 0)
    def _():
        carry_sc[...] = jnp.zeros_like(carry_sc)

    bucket = rt_ref[0, 0:1, :]
    sub = lax.broadcasted_iota(I32, (32, tm), 0).astype(F32)
    hit = sub == bucket
    r = lax.broadcasted_iota(I32, (tm, tm), 0)
    c = lax.broadcasted_iota(I32, (tm, tm), 1)
    upper = jnp.where(r <= c, 1.0, 0.0).astype(BF16)
    cum = jnp.dot(jnp.where(hit, 1.0, 0.0).astype(BF16), upper, preferred_element_type=F32)
    carry = carry_sc[:, 0:1]
    rank = jnp.sum(jnp.where(hit, cum - 1.0 + carry, 0.0), axis=0, keepdims=True)
    rank_ref[0] = rank.astype(I32)
    new_carry = carry_sc[...] + jnp.broadcast_to(cum[:, tm - 1:tm], carry_sc.shape)
    carry_sc[...] = new_carry
    cnt_ref[...] = new_carry


def _rank(rt):
    nt, _, tm = rt.shape
    return pl.pallas_call(
        functools.partial(_rank_kernel, tm=tm),
        grid=(nt,),
        in_specs=[pl.BlockSpec((1, 8, tm), lambda i: (i, 0, 0))],
        out_specs=[pl.BlockSpec((1, 1, tm), lambda i: (i, 0, 0)),
                   pl.BlockSpec((32, LANES), lambda i: (0, 0))],
        out_shape=[jax.ShapeDtypeStruct((nt, 1, tm), I32), jax.ShapeDtypeStruct((32, LANES), F32)],
        scratch_shapes=[pltpu.VMEM((32, LANES), F32)],
        compiler_params=pltpu.CompilerParams(dimension_semantics=("arbitrary",)),
        name="rank",
    )(rt)


def _dispatch_kernel(pos_ref, zrow_ref, zflag_ref, nact_ref, xa_ref, xs_ref, zbuf, zsem, sem, *,
                     n_tiles, rows):
    @pl.when(pl.program_id(0) == 0)
    def _():
        _zero_partial_tiles(zrow_ref, zflag_ref, nact_ref, xs_ref, zbuf, zsem, n_tiles=n_tiles)

    base = pl.program_id(0) * rows
    for r in range(rows):
        pltpu.make_async_copy(xa_ref.at[pl.ds(r, 1)], xs_ref.at[pl.ds(pos_ref[base + r], 1)], sem).start(
            priority=r % 2)
    pltpu.make_async_copy(xa_ref, xs_ref.at[pl.ds(0, rows)], sem).wait()


def _zero_partial_tiles(zrow_ref, zflag_ref, nact_ref, xs_ref, zbuf, zsem, *, n_tiles):
    zbuf[...] = jnp.zeros_like(zbuf)

    def zero_copy(row):
        return pltpu.make_async_copy(zbuf, xs_ref.at[pl.ds(pl.multiple_of(row, MOE_ROWS), MOE_ROWS)], zsem)

    def bucket_tiles(op):
        def body(b, carry):
            @pl.when(zflag_ref[b] > 0)
            def _():
                op(zero_copy(zrow_ref[b]))
            return carry
        lax.fori_loop(0, N_PAIR_BUCKETS, body, 0)

    def unused_tiles(op):
        def body(i, carry):
            op(zero_copy(i * MOE_ROWS))
            return carry
        lax.fori_loop(nact_ref[0], n_tiles, body, 0)

    bucket_tiles(lambda cp: cp.start())
    unused_tiles(lambda cp: cp.start())
    bucket_tiles(lambda cp: cp.wait())
    unused_tiles(lambda cp: cp.wait())


def _dispatch(pos, zrow, zflag, n_active, xa, *, n_tiles):
    n_tok = xa.shape[0]
    n_sorted = n_tiles * MOE_ROWS
    rows = min(DMA_ROWS, n_tok)
    return pl.pallas_call(
        functools.partial(_dispatch_kernel, n_tiles=n_tiles, rows=rows),
        grid_spec=pltpu.PrefetchScalarGridSpec(
            num_scalar_prefetch=4,
            grid=(n_tok // rows,),
            in_specs=[pl.BlockSpec((rows, ROW_W), lambda i, *_: (i, 0))],
            out_specs=pl.BlockSpec(memory_space=pl.ANY),
            scratch_shapes=[pltpu.VMEM((MOE_ROWS, ROW_W), F32), pltpu.SemaphoreType.DMA(()),
                            pltpu.SemaphoreType.DMA(())],
        ),
        out_shape=jax.ShapeDtypeStruct((n_sorted, ROW_W), F32),
        compiler_params=pltpu.CompilerParams(
            dimension_semantics=("arbitrary",), has_side_effects=True,
            vmem_limit_bytes=_vmem_limit((2 * rows + MOE_ROWS) * ROW_W * 4)),
        name="dispatch",
    )(pos, zrow, zflag, n_active, xa)


def _moe_kernel(ta_ref, tb_ref, xi_ref, nact_ref, xs_ref, gf_ref, wga_ref, wua_ref, wda_ref,
                wgb_ref, wub_ref, wdb_ref, ys_ref):
    i = pl.program_id(0)

    @pl.when(i < nact_ref[0])
    def _():
        x2 = xs_ref[:, :D_MODEL]
        aux = xs_ref[:, D_MODEL:]
        wa = aux[:, 1:2]
        wb = aux[:, 2:3]
        h = _rms(x2, gf_ref[...]).astype(BF16)

        def mlp(wg, wu, wd):
            g = jnp.dot(h, wg[0].astype(BF16), preferred_element_type=F32)
            u = jnp.dot(h, wu[0].astype(BF16), preferred_element_type=F32)
            act = (g / (1.0 + jnp.exp(-g))) * u
            return jnp.dot(act.astype(BF16), wd[0].astype(BF16), preferred_element_type=F32)

        ys_ref[...] = x2 + (wa * mlp(wga_ref, wua_ref, wda_ref) + wb * mlp(wgb_ref, wub_ref, wdb_ref))

    @pl.when(i >= nact_ref[0])
    def _():
        ys_ref[...] = jnp.zeros_like(ys_ref)


def _moe(tile_a, tile_b, tile_x, n_active, xs, g_ffn, w_gate, w_up, w_down):
    n_tiles = xs.shape[0] // MOE_ROWS
    t = MOE_ROWS
    wa_map = lambda i, ta, tb, xi, na: (ta[i], 0, 0)
    wb_map = lambda i, ta, tb, xi, na: (tb[i], 0, 0)
    wspec = pl.BlockSpec
    est = 2 * (t * ROW_W * 4 + 6 * D_MODEL * D_EXPERT * 4 + t * D_MODEL * 4) + 10 * t * D_MODEL * 4
    return pl.pallas_call(
        _moe_kernel,
        grid_spec=pltpu.PrefetchScalarGridSpec(
            num_scalar_prefetch=4,
            grid=(n_tiles,),
            in_specs=[
                pl.BlockSpec((t, ROW_W), lambda i, ta, tb, xi, na: (xi[i], 0)),
                pl.BlockSpec((1, D_MODEL), lambda i, ta, tb, xi, na: (0, 0)),
                wspec((1, D_MODEL, D_EXPERT), wa_map),
                wspec((1, D_MODEL, D_EXPERT), wa_map),
                wspec((1, D_EXPERT, D_MODEL), wa_map),
                wspec((1, D_MODEL, D_EXPERT), wb_map),
                wspec((1, D_MODEL, D_EXPERT), wb_map),
                wspec((1, D_EXPERT, D_MODEL), wb_map),
            ],
            out_specs=pl.BlockSpec((t, D_MODEL), lambda i, ta, tb, xi, na: (i, 0)),
        ),
        out_shape=jax.ShapeDtypeStruct((n_tiles * t, D_MODEL), F32),
        compiler_params=pltpu.CompilerParams(
            dimension_semantics=("arbitrary",), vmem_limit_bytes=_vmem_limit(est)),
        name="moe",
    )(tile_a, tile_b, tile_x, n_active, xs, g_ffn, w_gate, w_up, w_down, w_gate, w_up, w_down)


def _combine_kernel(pos_ref, ys_ref, out_ref, sem, *, rows):
    base = pl.program_id(0) * rows
    for r in range(rows):
        pltpu.make_async_copy(ys_ref.at[pl.ds(pos_ref[base + r], 1)], out_ref.at[pl.ds(r, 1)], sem).start(
            priority=r % 2)
    pltpu.make_async_copy(ys_ref.at[pl.ds(0, rows)], out_ref, sem).wait()


def _combine(pos, ys, *, n_tok):
    rows = min(DMA_ROWS, n_tok)
    return pl.pallas_call(
        functools.partial(_combine_kernel, rows=rows),
        grid_spec=pltpu.PrefetchScalarGridSpec(
            num_scalar_prefetch=1,
            grid=(n_tok // rows,),
            in_specs=[pl.BlockSpec(memory_space=pl.ANY)],
            out_specs=pl.BlockSpec((rows, D_MODEL), lambda i, *_: (i, 0)),
            scratch_shapes=[pltpu.SemaphoreType.DMA(())],
        ),
        out_shape=jax.ShapeDtypeStruct((n_tok, D_MODEL), F32),
        compiler_params=pltpu.CompilerParams(
            dimension_semantics=("arbitrary",), has_side_effects=True,
            vmem_limit_bytes=_vmem_limit(2 * rows * D_MODEL * 4)),
        name="combine",
    )(pos, ys)


def _t5_bucket(rel):
    nb = N_BUCKETS_T5 // 2
    max_exact = nb // 2
    ret = (rel > 0).astype(I32) * nb
    n = jnp.abs(rel)
    nf = jnp.maximum(n, 1).astype(F32)
    large = max_exact + (jnp.log(nf / max_exact) / math.log(MAX_DISTANCE / max_exact)
                         * (nb - max_exact)).astype(I32)
    large = jnp.minimum(large, nb - 1)
    return ret + jnp.where(n < max_exact, n, large)


def _diff_bias(rel_bias, t):
    assert t % CHUNK == 0 and t >= MAX_DISTANCE
    n_rel = 3 * t
    vec = rel_bias[_t5_bucket(jnp.arange(n_rel, dtype=I32) - (2 * t - 1))].astype(F32).T
    toe = jnp.tile(vec, (1, t))[:, :t * (n_rel - 1)].reshape(N_HEADS, t, n_rel - 1)
    qpos = jnp.arange(t, dtype=I32)[:, None]
    kpos = jnp.arange(t, dtype=I32)[None, :]
    far = rel_bias[_t5_bucket(jnp.asarray(-2 * t, I32))].astype(F32)[:, None, None]
    b0 = (toe[:, :, 2 * t - 1:3 * t - 1] - far) * LOG2E
    b0 = jnp.where((kpos // CHUNK <= qpos // CHUNK)[None], b0, NEG_INF)
    b1 = (toe[:, :, t - 1:2 * t - 1] - far) * LOG2E
    return jnp.swapaxes(jnp.stack([b0, b1], axis=1), -1, -2)


def _tile_plan(counts, n_tiles):
    t = MOE_ROWS
    tiles = (counts + t - 1) // t
    ends = jnp.cumsum(tiles)
    starts = ends - tiles
    n_active = ends[-1]
    tile_idx = jnp.minimum(jnp.arange(n_tiles, dtype=I32), n_active - 1)
    tile_bucket = jnp.minimum(jnp.sum((ends[None, :] <= tile_idx[:, None]).astype(I32), axis=1), N_PAIR_BUCKETS - 1)
    pa = jnp.asarray([p[0] for p in PAIRS], I32)
    pb = jnp.asarray([p[1] for p in PAIRS], I32)
    grp = tile_bucket // len(PAIRS)
    pair = tile_bucket % len(PAIRS)
    tile_a = grp * EXPERTS_PER_GROUP + pa[pair]
    tile_b = grp * EXPERTS_PER_GROUP + pb[pair]
    row_off = (starts * t).astype(I32)
    zrow = ((ends - 1) * t).astype(I32)
    zflag = (counts > 0).astype(I32)
    return row_off, zrow, zflag, tile_a.astype(I32), tile_b.astype(I32), tile_idx.astype(I32), n_active.astype(I32)


def kernel(x, mem, rel_bias, norm_mix_g, w_in, b_forget, diff_q_norm_g, diff_k_norm_g, diff_lambda_q1, diff_lambda_k1, diff_lambda_q2, diff_lambda_k2, diff_subln_g, fox_q_norm_g, fox_k_norm_g, fox_out_norm_g, w_out, norm_cross_g, norm_mem_g, w_cq, w_ckv, cross_q_norm_g, cross_k_norm_g, w_co, norm_ffn_g, w_group_router, b_group_router, w_expert_router, b_expert_router, w_exp_gate, w_exp_up, w_exp_down):
    batch, seq, d = x.shape
    assert d == D_MODEL and norm_mix_g.shape[0] == 1 and mem.shape[1] == N_MEM
    n_tok = batch * seq
    assert seq % ATTN_TILE == 0 or seq < ATTN_TILE
    l = 0

    w_qk = jnp.concatenate([w_in[l, :, :1024], w_in[l, :, 1536:2560]], axis=1).astype(BF16)
    w_vt = jnp.concatenate([w_in[l, :, 1024:1536], w_in[l, :, 2560:3072]], axis=1).T.astype(BF16)
    wft = jnp.zeros((8, D_MODEL), F32).at[:N_HEADS].set(w_in[l, :, 3072:].T).astype(BF16)
    bf = jnp.zeros((8, 1), F32).at[:N_HEADS, 0].set(b_forget[l].astype(F32))
    ones = jnp.ones((HEAD_DIM,), F32)
    qg = jnp.stack([
        jnp.tile(diff_q_norm_g[l].astype(F32), 2) * (DIFF_QK_DIM ** -0.5 * LOG2E),
        jnp.tile(diff_k_norm_g[l].astype(F32), 2),
        fox_q_norm_g[l].astype(F32) * (HEAD_DIM ** -0.5 * LOG2E), fox_k_norm_g[l].astype(F32),
        ones, ones, ones, ones])
    lam = (jnp.exp(jnp.sum(diff_lambda_q1[l].astype(F32) * diff_lambda_k1[l].astype(F32)))
           - jnp.exp(jnp.sum(diff_lambda_q2[l].astype(F32) * diff_lambda_k2[l].astype(F32)))
           + LAM_INIT).reshape(1)
    t_attn = min(ATTN_TILE, seq)
    bias_tiles_t = _diff_bias(rel_bias, t_attn)
    wr = jnp.zeros((32, D_MODEL), F32)
    wr = wr.at[:N_GROUPS].set(w_group_router[l].T).at[N_GROUPS:N_GROUPS + N_EXPERTS].set(w_expert_router[l].T)
    wr_hi = wr.astype(BF16)
    wr_lo = (wr - wr_hi.astype(F32)).astype(BF16)
    rb = jnp.zeros((32, 1), F32)
    rb = rb.at[:N_GROUPS, 0].set(b_group_router[l]).at[N_GROUPS:N_GROUPS + N_EXPERTS, 0].set(b_expert_router[l])

    row = lambda v: v.astype(F32).reshape(1, -1)
    x2d = x.reshape(n_tok, D_MODEL)

    qk, vt, cum = _mix_proj(x2d, row(norm_mix_g[l]), w_qk, w_vt, wft, bf, qg, batch=batch, seq=seq)
    cum4 = cum[:, :N_HEADS].reshape(batch, N_HEADS, 1, seq)
    gain_t = lambda v: jnp.broadcast_to(v.astype(F32)[:, None], (HEAD_DIM, t_attn))
    mixed = _attn(qk, vt, bias_tiles_t, lam, cum4, gain_t(diff_subln_g[l]), gain_t(fox_out_norm_g[l]),
                  batch=batch, seq=seq)
    x1, qc = _out_q(x2d, mixed, w_out[l].astype(BF16), row(norm_cross_g[l]),
                    w_cq[l].astype(BF16), row(cross_q_norm_g[l]))

    kc, vc = _mem_kv(mem.reshape(batch * N_MEM, D_MODEL), row(norm_mem_g[l]), w_ckv[l].astype(BF16),
                     row(cross_k_norm_g[l]), batch=batch)
    xa, rt = _cross_router(qc, kc, vc, x1, w_co[l].astype(BF16), row(norm_ffn_g[l]), wr_hi, wr_lo, rb,
                           batch=batch, seq=seq)

    rank, cnt = _rank(rt)
    counts = cnt[:N_PAIR_BUCKETS, 0].astype(I32)
    n_tiles = n_tok // MOE_ROWS + N_PAIR_BUCKETS
    row_off, zrow, zflag, tile_a, tile_b, tile_x, n_active = _tile_plan(counts, n_tiles)
    bucket = rt[:, 0, :].astype(I32)
    pos = rank[:, 0, :]
    for b in range(N_PAIR_BUCKETS):
        pos = pos + jnp.where(bucket == b, row_off[b], 0)
    pos = pos.reshape(n_tok)
    n_active = n_active.reshape(1)
    xs = _dispatch(pos, zrow, zflag, n_active, xa, n_tiles=n_tiles)
    ys = _moe(tile_a, tile_b, tile_x, n_active, xs, row(norm_ffn_g[l]),
              w_exp_gate[l].astype(F32), w_exp_up[l].astype(F32), w_exp_down[l].astype(F32))
    out = _combine(pos, ys, n_tok=n_tok)
    return out.reshape(batch, seq, D_MODEL)
```

```python
import functools
import math

import jax
import jax.numpy as jnp
import numpy as np
from jax import lax
from jax.experimental import pallas as pl
from jax.experimental.pallas import tpu as pltpu

F32 = jnp.float32
BF16 = jnp.bfloat16
I32 = jnp.int32

D_MODEL = 1024
CHUNK = 64
N_MEM = 256
N_HEADS = 4
HEAD_DIM = 128
VT_ROWS = HEAD_DIM + 16
DIFF_QK_DIM = 64
CROSS_HEAD_DIM = 256
N_BUCKETS_T5 = 32
MAX_DISTANCE = 128
N_GROUPS = 4
EXPERTS_PER_GROUP = 4
N_EXPERTS = N_GROUPS * EXPERTS_PER_GROUP
D_EXPERT = 512
EPS = 1e-6
NEG_INF = -1e30
LAM_INIT = 0.8 - 0.6 * math.exp(-0.3 * 0)
LOG2E = math.log2(math.e)

PAIRS = ((0, 1), (0, 2), (0, 3), (1, 3), (1, 2), (3, 2))
N_PAIR_BUCKETS = N_GROUPS * len(PAIRS)

LANES = 128
VMEM_LIMIT_CAP = 56 * 1024 * 1024

PROJ_ROWS = 1024
ATTN_TILE = 256
CROSS_ROWS = 512
MOE_ROWS = 256
AUX_COLS = LANES
ROW_W = D_MODEL + AUX_COLS
DMA_ROWS = 2048


def _vmem_limit(nbytes):
    return int(min(max(nbytes * 5 // 4, 32 * 1024 * 1024), VMEM_LIMIT_CAP))


def _nt_dot(a, b):
    return lax.dot_general(a, b, (((1,), (1,)), ((), ())), preferred_element_type=F32)


def _rms(x, g):
    ms = jnp.mean(x * x, axis=-1, keepdims=True)
    return x * lax.rsqrt(ms + EPS) * g


def _mix_proj_kernel(x_ref, g_ref, w_ref, wvt_ref, wft_ref, bf_ref, qg_ref, o_ref, vt_ref, cum_ref, carry_ref,
                     *, tm):
    si = pl.program_id(1)
    h = _rms(x_ref[...], g_ref[...]).astype(BF16)
    lane = lax.broadcasted_iota(I32, (tm, HEAD_DIM), 1)
    lo = lane < DIFF_QK_DIM
    for kind in range(4):
        p = jnp.dot(h, w_ref[:, kind * 512:(kind + 1) * 512], preferred_element_type=F32)
        for hh in range(N_HEADS):
            ph = p[:, hh * HEAD_DIM:(hh + 1) * HEAD_DIM]
            if kind in (0, 1):
                sq = ph * ph
                s_lo = jnp.sum(jnp.where(lo, sq, 0.0), axis=-1, keepdims=True)
                s_hi = jnp.sum(jnp.where(lo, 0.0, sq), axis=-1, keepdims=True)
                ms = jnp.where(lo, s_lo, s_hi) * (1.0 / DIFF_QK_DIM)
                ph = ph * lax.rsqrt(ms + EPS) * qg_ref[kind:kind + 1, :]
            else:
                ph = _rms(ph, qg_ref[kind:kind + 1, :])
            o_ref[0, kind * N_HEADS + hh] = ph.astype(BF16)
    for kind in range(2):
        pt = _nt_dot(wvt_ref[kind * 512:(kind + 1) * 512, :], h)
        for hh in range(N_HEADS):
            vt_ref[0, kind * N_HEADS + hh, :HEAD_DIM, :] = pt[hh * HEAD_DIM:(hh + 1) * HEAD_DIM, :].astype(BF16)
            vt_ref[0, kind * N_HEADS + hh, HEAD_DIM:, :] = jnp.ones((VT_ROWS - HEAD_DIM, tm), BF16)

    z = _nt_dot(wft_ref[...], h) + bf_ref[...]
    logf = jnp.minimum(z, 0.0) - jnp.log(1.0 + jnp.exp(-jnp.abs(z)))
    lane8 = lax.broadcasted_iota(I32, (8, tm), 1)
    c = logf * LOG2E
    k = 1
    while k < tm:
        c = c + jnp.where(lane8 >= k, pltpu.roll(c, k, axis=1), 0.0)
        k *= 2

    @pl.when(si == 0)
    def _():
        carry_ref[...] = jnp.zeros_like(carry_ref)

    c = c + carry_ref[:, 0:1]
    cum_ref[0] = c
    carry_ref[...] = jnp.broadcast_to(c[:, tm - 1:tm], carry_ref.shape)


def _mix_proj(x2d, g, w_qk, w_vt, wft, bf, qg, *, batch, seq):
    tm = min(PROJ_ROWS, seq)
    ns = seq // tm
    est = 2 * (tm * D_MODEL * 4 + D_MODEL * 3072 * 2 + 24 * tm * HEAD_DIM * 2) + 8 * tm * D_MODEL
    return pl.pallas_call(
        functools.partial(_mix_proj_kernel, tm=tm),
        grid=(batch, ns),
        in_specs=[
            pl.BlockSpec((tm, D_MODEL), lambda b, s: (b * ns + s, 0)),
            pl.BlockSpec((1, D_MODEL), lambda b, s: (0, 0)),
            pl.BlockSpec((D_MODEL, 2048), lambda b, s: (0, 0)),
            pl.BlockSpec((1024, D_MODEL), lambda b, s: (0, 0)),
            pl.BlockSpec((8, D_MODEL), lambda b, s: (0, 0)),
            pl.BlockSpec((8, 1), lambda b, s: (0, 0)),
            pl.BlockSpec((8, HEAD_DIM), lambda b, s: (0, 0)),
        ],
        out_specs=[
            pl.BlockSpec((1, 16, tm, HEAD_DIM), lambda b, s: (b, 0, s, 0)),
            pl.BlockSpec((1, 8, VT_ROWS, tm), lambda b, s: (b, 0, 0, s)),
            pl.BlockSpec((1, 8, tm), lambda b, s: (b, 0, s)),
        ],
        out_shape=[
            jax.ShapeDtypeStruct((batch, 16, seq, HEAD_DIM), BF16),
            jax.ShapeDtypeStruct((batch, 8, VT_ROWS, seq), BF16),
            jax.ShapeDtypeStruct((batch, 8, seq), F32),
        ],
        scratch_shapes=[pltpu.VMEM((8, LANES), F32)],
        compiler_params=pltpu.CompilerParams(
            dimension_semantics=("arbitrary", "arbitrary"), vmem_limit_bytes=_vmem_limit(est)),
        name="mix_proj",
    )(x2d, g, w_qk, w_vt, wft, bf, qg)


def _init_softmax(m_sc, acc_sc):
    m_sc[...] = jnp.full(m_sc.shape, NEG_INF, F32)
    acc_sc[...] = jnp.zeros_like(acc_sc)


N_CHAINS = 3 * N_HEADS


def _attn_kernel(lam_ref, dq_ref, dk_ref, dvt_ref, fq_ref, fk_ref, fvt_ref, bias_ref, c_ref,
                 gd_ref, gf_ref, o_ref, m_sc, acc_sc, qm_sc, ccol_sc, s_sc, mt_sc, s2_sc, mt2_sc, *, t, seq):
    qi = pl.program_id(1)
    _init_softmax(m_sc, acc_sc)
    key = lax.broadcasted_iota(I32, (t, t), 0)
    qry = lax.broadcasted_iota(I32, (t, t), 1)

    lane = lax.broadcasted_iota(I32, (t, HEAD_DIM), 1)
    for hh in range(N_HEADS):
        q = dq_ref[0, hh]
        zero = jnp.zeros_like(q)
        qm_sc[2 * hh] = jnp.where(lane < DIFF_QK_DIM, q, zero)
        qm_sc[2 * hh + 1] = jnp.where(lane < DIFF_QK_DIM, zero, q)

    @pl.when(qi == 0)
    def _():
        for hh in range(N_HEADS):
            for j in range(seq // t):
                row = c_ref[0, hh, :, j * t:(j + 1) * t]
                ccol_sc[hh, j * t:(j + 1) * t, :] = jnp.sum(
                    jnp.where(key == qry, jnp.broadcast_to(row, (t, t)), 0.0), axis=1, keepdims=True)

    q_off = pl.multiple_of(qi * t, t)

    bufs = ((s_sc, mt_sc), (s2_sc, mt2_sc))

    def logits(j, where, par, slot):
        off = pl.multiple_of(j * t, t)
        s_buf, mt_buf = bufs[par]
        if slot < 2 * N_HEADS:
            hh = slot // 2
            s_t = _nt_dot(dk_ref[0, hh, pl.ds(off, t), :], qm_sc[slot])
            if where != "far":
                s_t = s_t + bias_ref[hh, 1 if where == "near" else 0]
            s_buf[slot] = s_t
            mt_buf[slot] = jnp.max(s_t, axis=0, keepdims=True)
        else:
            hh = slot - 2 * N_HEADS
            c_k = ccol_sc[hh, pl.ds(off, t), :]
            s_t = _nt_dot(fk_ref[0, hh, pl.ds(off, t), :], fq_ref[0, hh]) - c_k
            if where == "diag":
                s_t = jnp.where(key <= qry, s_t, NEG_INF)
            s_buf[slot] = s_t
            mt_buf[slot] = jnp.max(s_t, axis=0, keepdims=True) + c_ref[0, hh, :, pl.ds(q_off, t)]

    def update(j, par, slot):
        off = pl.multiple_of(j * t, t)
        s_buf, mt_buf = bufs[par]
        vt_ref = dvt_ref if slot < 2 * N_HEADS else fvt_ref
        hh = slot // 2 if slot < 2 * N_HEADS else slot - 2 * N_HEADS
        v_t = vt_ref[0, hh, :, pl.ds(off, t)]
        m_prev = m_sc[slot]
        m_new = jnp.maximum(m_prev, mt_buf[slot])
        alpha = jnp.exp2(m_prev - m_new)
        if slot < 2 * N_HEADS:
            p_t = jnp.exp2(s_buf[slot] - m_new)
        else:
            p_t = jnp.exp2(s_buf[slot] - (m_new - c_ref[0, hh, :, pl.ds(q_off, t)]))
        acc_sc[slot] = alpha * acc_sc[slot] + jnp.dot(v_t, p_t.astype(BF16), preferred_element_type=F32)
        m_sc[slot] = m_new

    def phase(upd=None, nxt=None):
        for slot in range(N_CHAINS):
            if nxt is not None:
                logits(*nxt, slot)
            if upd is not None:
                update(*upd, slot)

    n_far = jnp.maximum(qi - 1, 0)
    far_tile = lambda k: jnp.maximum(qi - 2 - k, 0)

    phase(nxt=(qi, "diag", 0))

    @pl.when(qi == 0)
    def _():
        phase(upd=(qi, 0))

    @pl.when(qi == 1)
    def _():
        phase(upd=(qi, 0), nxt=(qi - 1, "near", 1))
        phase(upd=(qi - 1, 1))

    @pl.when(qi >= 2)
    def _():
        phase(upd=(qi, 0), nxt=(qi - 1, "near", 1))
        phase(upd=(qi - 1, 1), nxt=(far_tile(0), "far", 0))

        def far_pair(i, carry):
            k = 2 * i
            phase(upd=(far_tile(k), 0), nxt=(far_tile(k + 1), "far", 1))

            @pl.when(k + 1 < n_far)
            def _():
                phase(upd=(far_tile(k + 1), 1), nxt=(far_tile(k + 2), "far", 0))
            return carry

        lax.fori_loop(0, (n_far + 1) // 2, far_pair, 0)

    def normalised(slot):
        return acc_sc[slot, :HEAD_DIM, :] / acc_sc[slot, HEAD_DIM:HEAD_DIM + 1, :]

    def rms_t(o_t, g_t):
        ms = jnp.mean(o_t * o_t, axis=0, keepdims=True)
        return o_t * lax.rsqrt(ms + EPS) * g_t

    for hh in range(N_HEADS):
        o_t = normalised(2 * hh) - lam_ref[0] * normalised(2 * hh + 1)
        o_ref[:, hh * HEAD_DIM:(hh + 1) * HEAD_DIM] = (
            rms_t(o_t, gd_ref[...]) * (1.0 - LAM_INIT)).T.astype(BF16)
    for hh in range(N_HEADS):
        o_t = normalised(2 * N_HEADS + hh)
        o_ref[:, (N_HEADS + hh) * HEAD_DIM:(N_HEADS + hh + 1) * HEAD_DIM] = rms_t(o_t, gf_ref[...]).T.astype(BF16)


def _attn(qk, vt, bias_tiles_t, lam, cum4, subln_g, fox_g, *, batch, seq):
    t = min(ATTN_TILE, seq)
    nq = seq // t
    head_blk = N_HEADS * seq * HEAD_DIM * 2
    est = (2 * (2 * N_HEADS * t * HEAD_DIM * 2 + 4 * head_blk + N_HEADS * 2 * t * t * 4 + N_HEADS * 8 * seq * 4
                + t * 2 * N_HEADS * HEAD_DIM * 2)
           + N_CHAINS * (HEAD_DIM + 16) * t * 4 + 2 * N_HEADS * t * HEAD_DIM * 2 + N_HEADS * seq * LANES * 4
           + (2 * N_CHAINS + 36) * t * t * 4)
    qspec = lambda blk: pl.BlockSpec((1, N_HEADS, t, HEAD_DIM), lambda b, i: (b, blk, i, 0))
    kspec = lambda blk: pl.BlockSpec((1, N_HEADS, seq, HEAD_DIM), lambda b, i: (b, blk, 0, 0))
    vspec = lambda blk: pl.BlockSpec((1, N_HEADS, VT_ROWS, seq), lambda b, i: (b, blk, 0, 0))
    return pl.pallas_call(
        functools.partial(_attn_kernel, t=t, seq=seq),
        grid=(batch, nq),
        in_specs=[
            pl.BlockSpec(memory_space=pltpu.SMEM),
            qspec(0), kspec(1), vspec(0),
            qspec(2), kspec(3), vspec(1),
            pl.BlockSpec((N_HEADS, 2, t, t), lambda b, i: (0, 0, 0, 0)),
            pl.BlockSpec((1, N_HEADS, 1, seq), lambda b, i: (b, 0, 0, 0)),
            pl.BlockSpec((HEAD_DIM, t), lambda b, i: (0, 0)),
            pl.BlockSpec((HEAD_DIM, t), lambda b, i: (0, 0)),
        ],
        out_specs=pl.BlockSpec((t, 2 * N_HEADS * HEAD_DIM), lambda b, i: (b * nq + i, 0)),
        out_shape=jax.ShapeDtypeStruct((batch * seq, 2 * N_HEADS * HEAD_DIM), BF16),
        scratch_shapes=[pltpu.VMEM((N_CHAINS, 1, t), F32),
                        pltpu.VMEM((N_CHAINS, VT_ROWS, t), F32),
                        pltpu.VMEM((2 * N_HEADS, t, HEAD_DIM), BF16),
                        pltpu.VMEM((N_HEADS, seq, 1), F32),
                        pltpu.VMEM((N_CHAINS, t, t), F32), pltpu.VMEM((N_CHAINS, 1, t), F32),
                        pltpu.VMEM((N_CHAINS, t, t), F32), pltpu.VMEM((N_CHAINS, 1, t), F32)],
        compiler_params=pltpu.CompilerParams(
            dimension_semantics=("arbitrary", "arbitrary"), vmem_limit_bytes=_vmem_limit(est)),
        name="attn",
    )(lam, qk, qk, vt, qk, qk, vt, bias_tiles_t, cum4, subln_g, fox_g)


def _out_q_kernel(x_ref, m_ref, wo_ref, gc_ref, wq_ref, qg_ref, x1_ref, qc_ref):
    x1 = x_ref[...] + jnp.dot(m_ref[...], wo_ref[...], preferred_element_type=F32)
    x1_ref[...] = x1
    hc = _rms(x1, gc_ref[...]).astype(BF16)
    q = jnp.dot(hc, wq_ref[...], preferred_element_type=F32)
    for hh in range(N_HEADS):
        sl = slice(hh * CROSS_HEAD_DIM, (hh + 1) * CROSS_HEAD_DIM)
        qh = _rms(q[:, sl], qg_ref[...]) * (CROSS_HEAD_DIM ** -0.5)
        qc_ref[:, sl] = qh.astype(BF16)


def _out_q(x2d, mixed, w_out, g_cross, w_cq, q_g):
    n = x2d.shape[0]
    tm = min(PROJ_ROWS, n)
    est = 2 * (tm * D_MODEL * 4 * 2 + tm * D_MODEL * 2 + 2 * D_MODEL * D_MODEL * 2 + tm * D_MODEL * 2) + 6 * tm * D_MODEL * 4
    return pl.pallas_call(
        _out_q_kernel,
        grid=(n // tm,),
        in_specs=[
            pl.BlockSpec((tm, D_MODEL), lambda i: (i, 0)),
            pl.BlockSpec((tm, D_MODEL), lambda i: (i, 0)),
            pl.BlockSpec((D_MODEL, D_MODEL), lambda i: (0, 0)),
            pl.BlockSpec((1, D_MODEL), lambda i: (0, 0)),
            pl.BlockSpec((D_MODEL, D_MODEL), lambda i: (0, 0)),
            pl.BlockSpec((1, CROSS_HEAD_DIM), lambda i: (0, 0)),
        ],
        out_specs=[pl.BlockSpec((tm, D_MODEL), lambda i: (i, 0)),
                   pl.BlockSpec((tm, D_MODEL), lambda i: (i, 0))],
        out_shape=[jax.ShapeDtypeStruct((n, D_MODEL), F32), jax.ShapeDtypeStruct((n, D_MODEL), BF16)],
        compiler_params=pltpu.CompilerParams(
            dimension_semantics=("arbitrary",), vmem_limit_bytes=_vmem_limit(est)),
        name="out_q",
    )(x2d, mixed, w_out, g_cross, w_cq, q_g)


def _mem_kv_kernel(mem_ref, gm_ref, w_ref, kg_ref, k_ref, v_ref):
    mn = _rms(mem_ref[...], gm_ref[...]).astype(BF16)
    kv = jnp.dot(mn, w_ref[...], preferred_element_type=F32)
    for hh in range(N_HEADS):
        sl = slice(hh * CROSS_HEAD_DIM, (hh + 1) * CROSS_HEAD_DIM)
        k_ref[0, :, sl] = _rms(kv[:, sl], kg_ref[...]).astype(BF16)
    v_ref[0] = kv[:, D_MODEL:].astype(BF16)


def _mem_kv(mem2d, g_mem, w_ckv, k_g, *, batch):
    est = 2 * (N_MEM * D_MODEL * 4 + D_MODEL * 2 * D_MODEL * 2 + 2 * N_MEM * D_MODEL * 2) + 4 * N_MEM * 2 * D_MODEL * 4
    return pl.pallas_call(
        _mem_kv_kernel,
        grid=(batch,),
        in_specs=[
            pl.BlockSpec((N_MEM, D_MODEL), lambda b: (b, 0)),
            pl.BlockSpec((1, D_MODEL), lambda b: (0, 0)),
            pl.BlockSpec((D_MODEL, 2 * D_MODEL), lambda b: (0, 0)),
            pl.BlockSpec((1, CROSS_HEAD_DIM), lambda b: (0, 0)),
        ],
        out_specs=[pl.BlockSpec((1, N_MEM, D_MODEL), lambda b: (b, 0, 0)),
                   pl.BlockSpec((1, N_MEM, D_MODEL), lambda b: (b, 0, 0))],
        out_shape=[jax.ShapeDtypeStruct((batch, N_MEM, D_MODEL), BF16),
                   jax.ShapeDtypeStruct((batch, N_MEM, D_MODEL), BF16)],
        compiler_params=pltpu.CompilerParams(
            dimension_semantics=("arbitrary",), vmem_limit_bytes=_vmem_limit(est)),
        name="mem_kv",
    )(mem2d, g_mem, w_ckv, k_g)


def _route(logits_t):
    gl = logits_t[0:N_GROUPS]
    gmax = jnp.max(gl, axis=0, keepdims=True)
    eg = jnp.exp(gl - gmax)
    p_group = eg / jnp.sum(eg, axis=0, keepdims=True)
    p_g = jnp.max(p_group, axis=0, keepdims=True)
    g_idx = jnp.full_like(p_g, float(N_GROUPS))
    for g in reversed(range(N_GROUPS)):
        g_idx = jnp.where(p_group[g:g + 1] == p_g, float(g), g_idx)

    sel = []
    for j in range(EXPERTS_PER_GROUP):
        acc = jnp.zeros_like(p_g)
        for g in range(N_GROUPS):
            r = N_GROUPS + g * EXPERTS_PER_GROUP + j
            acc = jnp.where(g_idx == float(g), logits_t[r:r + 1], acc)
        sel.append(acc)
    smax = functools.reduce(jnp.maximum, sel)
    es = [jnp.exp(s - smax) for s in sel]
    den = functools.reduce(jnp.add, es)
    p_in = [e / den for e in es]

    v0 = functools.reduce(jnp.maximum, p_in)
    i0 = jnp.full_like(v0, float(EXPERTS_PER_GROUP))
    for j in reversed(range(EXPERTS_PER_GROUP)):
        i0 = jnp.where(p_in[j] == v0, float(j), i0)
    rest = [jnp.where(i0 == float(j), -1.0, p_in[j]) for j in range(EXPERTS_PER_GROUP)]
    v1 = functools.reduce(jnp.maximum, rest)
    i1 = jnp.full_like(v1, float(EXPERTS_PER_GROUP))
    for j in reversed(range(EXPERTS_PER_GROUP)):
        i1 = jnp.where(rest[j] == v1, float(j), i1)

    tot = v0 + v1
    w0 = p_g * (v0 / tot)
    w1 = p_g * (v1 / tot)
    first = i0 < i1
    a = jnp.where(first, i0, i1)
    b = jnp.where(first, i1, i0)
    w_lo = jnp.where(first, w0, w1)
    w_hi = jnp.where(first, w1, w0)
    pair = jnp.where(a == 0.0, b - 1.0, jnp.where(a == 1.0, jnp.where(b == 3.0, 3.0, 4.0), 5.0))
    swap = a == 2.0
    wa = jnp.where(swap, w_hi, w_lo)
    wb = jnp.where(swap, w_lo, w_hi)
    bucket = g_idx * float(len(PAIRS)) + pair
    return bucket, wa, wb


def _cross_router_kernel(q_ref, k_ref, v_ref, x1_ref, wco_ref, gf_ref, whi_ref, wlo_ref, rb_ref,
                         xa_ref, rt_ref, *, tm):
    q = q_ref[...]
    k = k_ref[0]
    v = v_ref[0]
    outs = []
    for hh in range(N_HEADS):
        sl = slice(hh * CROSS_HEAD_DIM, (hh + 1) * CROSS_HEAD_DIM)
        s = _nt_dot(q[:, sl], k[:, sl])
        p = jnp.exp(s - jnp.max(s, axis=-1, keepdims=True))
        l = jnp.sum(p, axis=-1, keepdims=True)
        o = jnp.dot(p.astype(BF16), v[:, sl], preferred_element_type=F32) / l
        outs.append(o.astype(BF16))
    o = jnp.concatenate(outs, axis=1)
    x2 = x1_ref[...] + jnp.dot(o, wco_ref[...], preferred_element_type=F32)
    xa_ref[:, :D_MODEL] = x2

    h3 = _rms(x2, gf_ref[...])
    hi = h3.astype(BF16)
    lo = (h3 - hi.astype(F32)).astype(BF16)
    lt = _nt_dot(whi_ref[...], hi) + (_nt_dot(whi_ref[...], lo) + _nt_dot(wlo_ref[...], hi))
    lt = lt + rb_ref[...]
    bucket, wa, wb = _route(lt)

    sub = lax.broadcasted_iota(I32, (8, tm), 0)
    rows = jnp.where(sub == 0, bucket, jnp.where(sub == 1, wa, jnp.where(sub == 2, wb, 0.0)))
    rt_ref[0] = rows
    sub_a = lax.broadcasted_iota(I32, (AUX_COLS, tm), 0)
    aux_t = jnp.where(sub_a == 0, bucket, jnp.where(sub_a == 1, wa, jnp.where(sub_a == 2, wb, 0.0)))
    xa_ref[:, D_MODEL:] = aux_t.T


def _cross_router(qc, kc, vc, x1, w_co, g_ffn, wr_hi, wr_lo, rb, *, batch, seq):
    tm = min(CROSS_ROWS, seq)
    ns = seq // tm
    n = batch * seq
    est = 2 * (tm * D_MODEL * 2 + 2 * N_MEM * D_MODEL * 2 + tm * D_MODEL * 4 + D_MODEL * D_MODEL * 2
               + tm * ROW_W * 4) + 10 * tm * D_MODEL * 4
    return pl.pallas_call(
        functools.partial(_cross_router_kernel, tm=tm),
        grid=(batch, ns),
        in_specs=[
            pl.BlockSpec((tm, D_MODEL), lambda b, s: (b * ns + s, 0)),
            pl.BlockSpec((1, N_MEM, D_MODEL), lambda b, s: (b, 0, 0)),
            pl.BlockSpec((1, N_MEM, D_MODEL), lambda b, s: (b, 0, 0)),
            pl.BlockSpec((tm, D_MODEL), lambda b, s: (b * ns + s, 0)),
            pl.BlockSpec((D_MODEL, D_MODEL), lambda b, s: (0, 0)),
            pl.BlockSpec((1, D_MODEL), lambda b, s: (0, 0)),
            pl.BlockSpec((32, D_MODEL), lambda b, s: (0, 0)),
            pl.BlockSpec((32, D_MODEL), lambda b, s: (0, 0)),
            pl.BlockSpec((32, 1), lambda b, s: (0, 0)),
        ],
        out_specs=[pl.BlockSpec((tm, ROW_W), lambda b, s: (b * ns + s, 0)),
                   pl.BlockSpec((1, 8, tm), lambda b, s: (b * ns + s, 0, 0))],
        out_shape=[jax.ShapeDtypeStruct((n, ROW_W), F32),
                   jax.ShapeDtypeStruct((n // tm, 8, tm), F32)],
        compiler_params=pltpu.CompilerParams(
            dimension_semantics=("arbitrary", "arbitrary"), vmem_limit_bytes=_vmem_limit(est)),
        name="cross_router",
    )(qc, kc, vc, x1, w_co, g_ffn, wr_hi, wr_lo, rb)


def _rank_kernel(rt_ref, pos_ref, cnt_ref, carry_sc, off_sc, *, tm):
    phase = pl.program_id(0)
    i = pl.program_id(1)
    bucket = rt_ref[0, 0:1, :]
    sub = lax.broadcasted_iota(I32, (32, tm), 0).astype(F32)
    hit = sub == bucket

    @pl.when(jnp.logical_and(phase == 0, i == 0))
    def _():
        carry_sc[...] = jnp.zeros_like(carry_sc)

    @pl.when(phase == 0)
    def _():
        n_hit = jnp.sum(jnp.where(hit, 1.0, 0.0), axis=1, keepdims=True)
        carry_sc[...] = carry_sc[...] + jnp.broadcast_to(n_hit, carry_sc.shape)

    @pl.when(jnp.logical_and(phase == 0, i == pl.num_programs(1) - 1))
    def _():
        counts = carry_sc[...]
        cnt_ref[...] = counts
        tiles = jnp.floor((counts + (MOE_ROWS - 1)) * (1.0 / MOE_ROWS))
        r = lax.broadcasted_iota(I32, (32, 32), 0)
        c = lax.broadcasted_iota(I32, (32, 32), 1)
        before = jnp.where(c < r, 1.0, 0.0).astype(BF16)
        off_sc[...] = jnp.dot(before, tiles.astype(BF16), preferred_element_type=F32) * MOE_ROWS
        carry_sc[...] = jnp.zeros_like(carry_sc)

    @pl.when(phase == 1)
    def _():
        r = lax.broadcasted_iota(I32, (tm, tm), 0)
        c = lax.broadcasted_iota(I32, (tm, tm), 1)
        upper = jnp.where(r <= c, 1.0, 0.0).astype(BF16)
        cum = jnp.dot(jnp.where(hit, 1.0, 0.0).astype(BF16), upper, preferred_element_type=F32)
        base = carry_sc[:, 0:1] + off_sc[:, 0:1]
        pos = jnp.sum(jnp.where(hit, cum - 1.0 + base, 0.0), axis=0, keepdims=True)
        pos_ref[0] = pos.astype(I32)
        carry_sc[...] = carry_sc[...] + jnp.broadcast_to(cum[:, tm - 1:tm], carry_sc.shape)


def _rank(rt):
    nt, _, tm = rt.shape
    return pl.pallas_call(
        functools.partial(_rank_kernel, tm=tm),
        grid=(2, nt),
        in_specs=[pl.BlockSpec((1, 8, tm), lambda p, i: (i, 0, 0))],
        out_specs=[pl.BlockSpec((1, 1, tm), lambda p, i: (p * i, 0, 0)),
                   pl.BlockSpec((32, LANES), lambda p, i: (0, 0))],
        out_shape=[jax.ShapeDtypeStruct((nt, 1, tm), I32), jax.ShapeDtypeStruct((32, LANES), F32)],
        scratch_shapes=[pltpu.VMEM((32, LANES), F32), pltpu.VMEM((32, LANES), F32)],
        compiler_params=pltpu.CompilerParams(dimension_semantics=("arbitrary", "arbitrary")),
        name="rank",
    )(rt)


def _dispatch_kernel(pos_ref, zrow_ref, zflag_ref, nact_ref, xa_ref, xs_ref, zbuf, zsem, sem, *,
                     n_tiles, rows):
    @pl.when(pl.program_id(0) == 0)
    def _():
        _zero_partial_tiles(zrow_ref, zflag_ref, nact_ref, xs_ref, zbuf, zsem, n_tiles=n_tiles)

    base = pl.program_id(0) * rows
    for r in range(rows):
        pltpu.make_async_copy(xa_ref.at[pl.ds(r, 1)], xs_ref.at[pl.ds(pos_ref[base + r], 1)], sem).start(
            priority=r % 2)
    pltpu.make_async_copy(xa_ref, xs_ref.at[pl.ds(0, rows)], sem).wait()


def _zero_partial_tiles(zrow_ref, zflag_ref, nact_ref, xs_ref, zbuf, zsem, *, n_tiles):
    zbuf[...] = jnp.zeros_like(zbuf)

    def zero_copy(row):
        return pltpu.make_async_copy(zbuf, xs_ref.at[pl.ds(pl.multiple_of(row, MOE_ROWS), MOE_ROWS)], zsem)

    def bucket_tiles(op):
        def body(b, carry):
            @pl.when(zflag_ref[b] > 0)
            def _():
                op(zero_copy(zrow_ref[b]))
            return carry
        lax.fori_loop(0, N_PAIR_BUCKETS, body, 0)

    def unused_tiles(op):
        def body(i, carry):
            op(zero_copy(i * MOE_ROWS))
            return carry
        lax.fori_loop(nact_ref[0], n_tiles, body, 0)

    bucket_tiles(lambda cp: cp.start())
    unused_tiles(lambda cp: cp.start())
    bucket_tiles(lambda cp: cp.wait())
    unused_tiles(lambda cp: cp.wait())


def _dispatch(pos, zrow, zflag, n_active, xa, *, n_tiles):
    n_tok = xa.shape[0]
    n_sorted = n_tiles * MOE_ROWS
    rows = min(DMA_ROWS, n_tok)
    return pl.pallas_call(
        functools.partial(_dispatch_kernel, n_tiles=n_tiles, rows=rows),
        grid_spec=pltpu.PrefetchScalarGridSpec(
            num_scalar_prefetch=4,
            grid=(n_tok // rows,),
            in_specs=[pl.BlockSpec((rows, ROW_W), lambda i, *_: (i, 0))],
            out_specs=pl.BlockSpec(memory_space=pl.ANY),
            scratch_shapes=[pltpu.VMEM((MOE_ROWS, ROW_W), F32), pltpu.SemaphoreType.DMA(()),
                            pltpu.SemaphoreType.DMA(())],
        ),
        out_shape=jax.ShapeDtypeStruct((n_sorted, ROW_W), F32),
        compiler_params=pltpu.CompilerParams(
            dimension_semantics=("arbitrary",), has_side_effects=True,
            vmem_limit_bytes=_vmem_limit((2 * rows + MOE_ROWS) * ROW_W * 4)),
        name="dispatch",
    )(pos, zrow, zflag, n_active, xa)


def _moe_kernel(ta_ref, tb_ref, xi_ref, nact_ref, xs_ref, gf_ref, wga_ref, wua_ref, wda_ref,
                wgb_ref, wub_ref, wdb_ref, ys_ref):
    i = pl.program_id(0)

    @pl.when(i < nact_ref[0])
    def _():
        x2 = xs_ref[:, :D_MODEL]
        aux = xs_ref[:, D_MODEL:]
        wa = aux[:, 1:2]
        wb = aux[:, 2:3]
        h = _rms(x2, gf_ref[...]).astype(BF16)

        def mlp(wg, wu, wd):
            g = jnp.dot(h, wg[0].astype(BF16), preferred_element_type=F32)
            u = jnp.dot(h, wu[0].astype(BF16), preferred_element_type=F32)
            act = (g / (1.0 + jnp.exp(-g))) * u
            return jnp.dot(act.astype(BF16), wd[0].astype(BF16), preferred_element_type=F32)

        ys_ref[...] = x2 + (wa * mlp(wga_ref, wua_ref, wda_ref) + wb * mlp(wgb_ref, wub_ref, wdb_ref))

    @pl.when(i >= nact_ref[0])
    def _():
        ys_ref[...] = jnp.zeros_like(ys_ref)


def _moe(tile_a, tile_b, tile_x, n_active, xs, g_ffn, w_gate, w_up, w_down):
    n_tiles = xs.shape[0] // MOE_ROWS
    t = MOE_ROWS
    wa_map = lambda i, ta, tb, xi, na: (ta[i], 0, 0)
    wb_map = lambda i, ta, tb, xi, na: (tb[i], 0, 0)
    wspec = pl.BlockSpec
    est = 2 * (t * ROW_W * 4 + 6 * D_MODEL * D_EXPERT * 4 + t * D_MODEL * 4) + 10 * t * D_MODEL * 4
    return pl.pallas_call(
        _moe_kernel,
        grid_spec=pltpu.PrefetchScalarGridSpec(
            num_scalar_prefetch=4,
            grid=(n_tiles,),
            in_specs=[
                pl.BlockSpec((t, ROW_W), lambda i, ta, tb, xi, na: (xi[i], 0)),
                pl.BlockSpec((1, D_MODEL), lambda i, ta, tb, xi, na: (0, 0)),
                wspec((1, D_MODEL, D_EXPERT), wa_map),
                wspec((1, D_MODEL, D_EXPERT), wa_map),
                wspec((1, D_EXPERT, D_MODEL), wa_map),
                wspec((1, D_MODEL, D_EXPERT), wb_map),
                wspec((1, D_MODEL, D_EXPERT), wb_map),
                wspec((1, D_EXPERT, D_MODEL), wb_map),
            ],
            out_specs=pl.BlockSpec((t, D_MODEL), lambda i, ta, tb, xi, na: (i, 0)),
        ),
        out_shape=jax.ShapeDtypeStruct((n_tiles * t, D_MODEL), F32),
        compiler_params=pltpu.CompilerParams(
            dimension_semantics=("arbitrary",), vmem_limit_bytes=_vmem_limit(est)),
        name="moe",
    )(tile_a, tile_b, tile_x, n_active, xs, g_ffn, w_gate, w_up, w_down, w_gate, w_up, w_down)


def _combine_kernel(pos_ref, ys_ref, out_ref, sem, *, rows):
    base = pl.program_id(0) * rows
    for r in range(rows):
        pltpu.make_async_copy(ys_ref.at[pl.ds(pos_ref[base + r], 1)], out_ref.at[pl.ds(r, 1)], sem).start(
            priority=r % 2)
    pltpu.make_async_copy(ys_ref.at[pl.ds(0, rows)], out_ref, sem).wait()


def _combine(pos, ys, *, n_tok):
    rows = min(DMA_ROWS, n_tok)
    return pl.pallas_call(
        functools.partial(_combine_kernel, rows=rows),
        grid_spec=pltpu.PrefetchScalarGridSpec(
            num_scalar_prefetch=1,
            grid=(n_tok // rows,),
            in_specs=[pl.BlockSpec(memory_space=pl.ANY)],
            out_specs=pl.BlockSpec((rows, D_MODEL), lambda i, *_: (i, 0)),
            scratch_shapes=[pltpu.SemaphoreType.DMA(())],
        ),
        out_shape=jax.ShapeDtypeStruct((n_tok, D_MODEL), F32),
        compiler_params=pltpu.CompilerParams(
            dimension_semantics=("arbitrary",), has_side_effects=True,
            vmem_limit_bytes=_vmem_limit(2 * rows * D_MODEL * 4)),
        name="combine",
    )(pos, ys)


def _t5_bucket(rel):
    nb = N_BUCKETS_T5 // 2
    max_exact = nb // 2
    ret = (rel > 0).astype(I32) * nb
    n = jnp.abs(rel)
    nf = jnp.maximum(n, 1).astype(F32)
    large = max_exact + (jnp.log(nf / max_exact) / math.log(MAX_DISTANCE / max_exact)
                         * (nb - max_exact)).astype(I32)
    large = jnp.minimum(large, nb - 1)
    return ret + jnp.where(n < max_exact, n, large)


def _diff_bias(rel_bias, t):
    assert t % CHUNK == 0 and t >= MAX_DISTANCE
    n_rel = 3 * t
    vec = rel_bias[_t5_bucket(jnp.arange(n_rel, dtype=I32) - (2 * t - 1))].astype(F32).T
    toe = jnp.tile(vec, (1, t))[:, :t * (n_rel - 1)].reshape(N_HEADS, t, n_rel - 1)
    qpos = jnp.arange(t, dtype=I32)[:, None]
    kpos = jnp.arange(t, dtype=I32)[None, :]
    far = rel_bias[_t5_bucket(jnp.asarray(-2 * t, I32))].astype(F32)[:, None, None]
    b0 = (toe[:, :, 2 * t - 1:3 * t - 1] - far) * LOG2E
    b0 = jnp.where((kpos // CHUNK <= qpos // CHUNK)[None], b0, NEG_INF)
    b1 = (toe[:, :, t - 1:2 * t - 1] - far) * LOG2E
    return jnp.swapaxes(jnp.stack([b0, b1], axis=1), -1, -2)


def _tile_plan(counts, n_tiles):
    t = MOE_ROWS
    tiles = (counts + t - 1) // t
    ends = jnp.cumsum(tiles)
    n_active = ends[-1]
    tile_idx = jnp.minimum(jnp.arange(n_tiles, dtype=I32), n_active - 1)
    tile_bucket = jnp.minimum(jnp.sum((ends[None, :] <= tile_idx[:, None]).astype(I32), axis=1), N_PAIR_BUCKETS - 1)
    pa = jnp.asarray([p[0] for p in PAIRS], I32)
    pb = jnp.asarray([p[1] for p in PAIRS], I32)
    grp = tile_bucket // len(PAIRS)
    pair = tile_bucket % len(PAIRS)
    tile_a = grp * EXPERTS_PER_GROUP + pa[pair]
    tile_b = grp * EXPERTS_PER_GROUP + pb[pair]
    zrow = ((ends - 1) * t).astype(I32)
    zflag = (counts > 0).astype(I32)
    return zrow, zflag, tile_a.astype(I32), tile_b.astype(I32), tile_idx.astype(I32), n_active.astype(I32)


def kernel(x, mem, rel_bias, norm_mix_g, w_in, b_forget, diff_q_norm_g, diff_k_norm_g, diff_lambda_q1, diff_lambda_k1, diff_lambda_q2, diff_lambda_k2, diff_subln_g, fox_q_norm_g, fox_k_norm_g, fox_out_norm_g, w_out, norm_cross_g, norm_mem_g, w_cq, w_ckv, cross_q_norm_g, cross_k_norm_g, w_co, norm_ffn_g, w_group_router, b_group_router, w_expert_router, b_expert_router, w_exp_gate, w_exp_up, w_exp_down):
    batch, seq, d = x.shape
    assert d == D_MODEL and norm_mix_g.shape[0] == 1 and mem.shape[1] == N_MEM
    n_tok = batch * seq
    assert seq % ATTN_TILE == 0 or seq < ATTN_TILE
    l = 0

    w_qk = jnp.concatenate([w_in[l, :, :1024], w_in[l, :, 1536:2560]], axis=1).astype(BF16)
    w_vt = jnp.concatenate([w_in[l, :, 1024:1536], w_in[l, :, 2560:3072]], axis=1).T.astype(BF16)
    wft = jnp.zeros((8, D_MODEL), F32).at[:N_HEADS].set(w_in[l, :, 3072:].T).astype(BF16)
    bf = jnp.zeros((8, 1), F32).at[:N_HEADS, 0].set(b_forget[l].astype(F32))
    ones = jnp.ones((HEAD_DIM,), F32)
    qg = jnp.stack([
        jnp.tile(diff_q_norm_g[l].astype(F32), 2) * (DIFF_QK_DIM ** -0.5 * LOG2E),
        jnp.tile(diff_k_norm_g[l].astype(F32), 2),
        fox_q_norm_g[l].astype(F32) * (HEAD_DIM ** -0.5 * LOG2E), fox_k_norm_g[l].astype(F32),
        ones, ones, ones, ones])
    lam = (jnp.exp(jnp.sum(diff_lambda_q1[l].astype(F32) * diff_lambda_k1[l].astype(F32)))
           - jnp.exp(jnp.sum(diff_lambda_q2[l].astype(F32) * diff_lambda_k2[l].astype(F32)))
           + LAM_INIT).reshape(1)
    t_attn = min(ATTN_TILE, seq)
    bias_tiles_t = _diff_bias(rel_bias, t_attn)
    wr = jnp.zeros((32, D_MODEL), F32)
    wr = wr.at[:N_GROUPS].set(w_group_router[l].T).at[N_GROUPS:N_GROUPS + N_EXPERTS].set(w_expert_router[l].T)
    wr_hi = wr.astype(BF16)
    wr_lo = (wr - wr_hi.astype(F32)).astype(BF16)
    rb = jnp.zeros((32, 1), F32)
    rb = rb.at[:N_GROUPS, 0].set(b_group_router[l]).at[N_GROUPS:N_GROUPS + N_EXPERTS, 0].set(b_expert_router[l])

    row = lambda v: v.astype(F32).reshape(1, -1)
    x2d = x.reshape(n_tok, D_MODEL)

    qk, vt, cum = _mix_proj(x2d, row(norm_mix_g[l]), w_qk, w_vt, wft, bf, qg, batch=batch, seq=seq)
    cum4 = cum[:, :N_HEADS].reshape(batch, N_HEADS, 1, seq)
    gain_t = lambda v: jnp.broadcast_to(v.astype(F32)[:, None], (HEAD_DIM, t_attn))
    mixed = _attn(qk, vt, bias_tiles_t, lam, cum4, gain_t(diff_subln_g[l]), gain_t(fox_out_norm_g[l]),
                  batch=batch, seq=seq)
    x1, qc = _out_q(x2d, mixed, w_out[l].astype(BF16), row(norm_cross_g[l]),
                    w_cq[l].astype(BF16), row(cross_q_norm_g[l]))

    kc, vc = _mem_kv(mem.reshape(batch * N_MEM, D_MODEL), row(norm_mem_g[l]), w_ckv[l].astype(BF16),
                     row(cross_k_norm_g[l]), batch=batch)
    xa, rt = _cross_router(qc, kc, vc, x1, w_co[l].astype(BF16), row(norm_ffn_g[l]), wr_hi, wr_lo, rb,
                           batch=batch, seq=seq)

    pos, cnt = _rank(rt)
    pos = pos.reshape(n_tok)
    counts = cnt[:N_PAIR_BUCKETS, 0].astype(I32)
    n_tiles = n_tok // MOE_ROWS + N_PAIR_BUCKETS
    zrow, zflag, tile_a, tile_b, tile_x, n_active = _tile_plan(counts, n_tiles)
    n_active = n_active.reshape(1)
    xs = _dispatch(pos, zrow, zflag, n_active, xa, n_tiles=n_tiles)
    ys = _moe(tile_a, tile_b, tile_x, n_active, xs, row(norm_ffn_g[l]),
              w_exp_gate[l].astype(F32), w_exp_up[l].astype(F32), w_exp_down[l].astype(F32))
    out = _combine(pos, ys, n_tok=n_tok)
    return out.reshape(batch, seq, D_MODEL)
```

```python
import functools
import math

import jax
import jax.numpy as jnp
import numpy as np
from jax import lax
from jax.experimental import pallas as pl
from jax.experimental.pallas import tpu as pltpu

F32 = jnp.float32
BF16 = jnp.bfloat16
I32 = jnp.int32

D_MODEL = 1024
CHUNK = 64
N_MEM = 256
N_HEADS = 4
HEAD_DIM = 128
VT_ROWS = HEAD_DIM + 16
DIFF_QK_DIM = 64
CROSS_HEAD_DIM = 256
N_BUCKETS_T5 = 32
MAX_DISTANCE = 128
N_GROUPS = 4
EXPERTS_PER_GROUP = 4
N_EXPERTS = N_GROUPS * EXPERTS_PER_GROUP
D_EXPERT = 512
EPS = 1e-6
NEG_INF = -1e30
LAM_INIT = 0.8 - 0.6 * math.exp(-0.3 * 0)
LOG2E = math.log2(math.e)

PAIRS = ((0, 1), (0, 2), (0, 3), (1, 3), (1, 2), (3, 2))
N_PAIR_BUCKETS = N_GROUPS * len(PAIRS)

LANES = 128
VMEM_LIMIT_CAP = 56 * 1024 * 1024

PROJ_ROWS = 1024
ATTN_TILE = 256
CROSS_ROWS = 512
MOE_ROWS = 256
AUX_COLS = LANES
ROW_W = D_MODEL + AUX_COLS
DMA_ROWS = 2048


def _vmem_limit(nbytes):
    return int(min(max(nbytes * 5 // 4, 32 * 1024 * 1024), VMEM_LIMIT_CAP))


def _nt_dot(a, b):
    return lax.dot_general(a, b, (((1,), (1,)), ((), ())), preferred_element_type=F32)


def _rms(x, g):
    ms = jnp.mean(x * x, axis=-1, keepdims=True)
    return x * lax.rsqrt(ms + EPS) * g


def _mix_proj_kernel(x_ref, g_ref, w_ref, wvt_ref, wft_ref, bf_ref, qg_ref, o_ref, vt_ref, cum_ref, carry_ref,
                     *, tm):
    si = pl.program_id(1)
    h = _rms(x_ref[...], g_ref[...]).astype(BF16)
    lane = lax.broadcasted_iota(I32, (tm, HEAD_DIM), 1)
    lo = lane < DIFF_QK_DIM
    for kind in range(4):
        p = jnp.dot(h, w_ref[:, kind * 512:(kind + 1) * 512], preferred_element_type=F32)
        for hh in range(N_HEADS):
            ph = p[:, hh * HEAD_DIM:(hh + 1) * HEAD_DIM]
            if kind in (0, 1):
                sq = ph * ph
                s_lo = jnp.sum(jnp.where(lo, sq, 0.0), axis=-1, keepdims=True)
                s_hi = jnp.sum(jnp.where(lo, 0.0, sq), axis=-1, keepdims=True)
                ms = jnp.where(lo, s_lo, s_hi) * (1.0 / DIFF_QK_DIM)
                ph = ph * lax.rsqrt(ms + EPS) * qg_ref[kind:kind + 1, :]
            else:
                ph = _rms(ph, qg_ref[kind:kind + 1, :])
            o_ref[0, kind * N_HEADS + hh] = ph.astype(BF16)
    for kind in range(2):
        pt = _nt_dot(wvt_ref[kind * 512:(kind + 1) * 512, :], h)
        for hh in range(N_HEADS):
            vt_ref[0, kind * N_HEADS + hh, :HEAD_DIM, :] = pt[hh * HEAD_DIM:(hh + 1) * HEAD_DIM, :].astype(BF16)
            vt_ref[0, kind * N_HEADS + hh, HEAD_DIM:, :] = jnp.ones((VT_ROWS - HEAD_DIM, tm), BF16)

    z = _nt_dot(wft_ref[...], h) + bf_ref[...]
    logf = jnp.minimum(z, 0.0) - jnp.log(1.0 + jnp.exp(-jnp.abs(z)))
    lane8 = lax.broadcasted_iota(I32, (8, tm), 1)
    c = logf * LOG2E
    k = 1
    while k < tm:
        c = c + jnp.where(lane8 >= k, pltpu.roll(c, k, axis=1), 0.0)
        k *= 2

    @pl.when(si == 0)
    def _():
        carry_ref[...] = jnp.zeros_like(carry_ref)

    c = c + carry_ref[:, 0:1]
    cum_ref[0] = c
    carry_ref[...] = jnp.broadcast_to(c[:, tm - 1:tm], carry_ref.shape)


def _mix_proj(x2d, g, w_qk, w_vt, wft, bf, qg, *, batch, seq):
    tm = min(PROJ_ROWS, seq)
    ns = seq // tm
    est = 2 * (tm * D_MODEL * 4 + D_MODEL * 3072 * 2 + 24 * tm * HEAD_DIM * 2) + 8 * tm * D_MODEL
    return pl.pallas_call(
        functools.partial(_mix_proj_kernel, tm=tm),
        grid=(batch, ns),
        in_specs=[
            pl.BlockSpec((tm, D_MODEL), lambda b, s: (b * ns + s, 0)),
            pl.BlockSpec((1, D_MODEL), lambda b, s: (0, 0)),
            pl.BlockSpec((D_MODEL, 2048), lambda b, s: (0, 0)),
            pl.BlockSpec((1024, D_MODEL), lambda b, s: (0, 0)),
            pl.BlockSpec((8, D_MODEL), lambda b, s: (0, 0)),
            pl.BlockSpec((8, 1), lambda b, s: (0, 0)),
            pl.BlockSpec((8, HEAD_DIM), lambda b, s: (0, 0)),
        ],
        out_specs=[
            pl.BlockSpec((1, 16, tm, HEAD_DIM), lambda b, s: (b, 0, s, 0)),
            pl.BlockSpec((1, 8, VT_ROWS, tm), lambda b, s: (b, 0, 0, s)),
            pl.BlockSpec((1, 8, tm), lambda b, s: (b, 0, s)),
        ],
        out_shape=[
            jax.ShapeDtypeStruct((batch, 16, seq, HEAD_DIM), BF16),
            jax.ShapeDtypeStruct((batch, 8, VT_ROWS, seq), BF16),
            jax.ShapeDtypeStruct((batch, 8, seq), F32),
        ],
        scratch_shapes=[pltpu.VMEM((8, LANES), F32)],
        compiler_params=pltpu.CompilerParams(
            dimension_semantics=("arbitrary", "arbitrary"), vmem_limit_bytes=_vmem_limit(est)),
        name="mix_proj",
    )(x2d, g, w_qk, w_vt, wft, bf, qg)


def _init_softmax(m_sc, acc_sc):
    m_sc[...] = jnp.full(m_sc.shape, NEG_INF, F32)
    acc_sc[...] = jnp.zeros_like(acc_sc)


N_CHAINS = 3 * N_HEADS


def _attn_kernel(lam_ref, dq_ref, dk_ref, dvt_ref, fq_ref, fk_ref, fvt_ref, bias_ref, c_ref,
                 gd_ref, gf_ref, o_ref, m_sc, acc_sc, qm_sc, ccol_sc, s_sc, mt_sc, s2_sc, mt2_sc, *, t, seq):
    qi = pl.program_id(1)
    _init_softmax(m_sc, acc_sc)
    key = lax.broadcasted_iota(I32, (t, t), 0)
    qry = lax.broadcasted_iota(I32, (t, t), 1)

    lane = lax.broadcasted_iota(I32, (t, HEAD_DIM), 1)
    for hh in range(N_HEADS):
        q = dq_ref[0, hh]
        zero = jnp.zeros_like(q)
        qm_sc[2 * hh] = jnp.where(lane < DIFF_QK_DIM, q, zero)
        qm_sc[2 * hh + 1] = jnp.where(lane < DIFF_QK_DIM, zero, q)

    @pl.when(qi == 0)
    def _():
        for hh in range(N_HEADS):
            for j in range(seq // t):
                row = c_ref[0, hh, :, j * t:(j + 1) * t]
                ccol_sc[hh, j * t:(j + 1) * t, :] = jnp.sum(
                    jnp.where(key == qry, jnp.broadcast_to(row, (t, t)), 0.0), axis=1, keepdims=True)

    q_off = pl.multiple_of(qi * t, t)

    bufs = ((s_sc, mt_sc), (s2_sc, mt2_sc))

    def logits(j, where, par, slot):
        off = pl.multiple_of(j * t, t)
        s_buf, mt_buf = bufs[par]
        if slot < 2 * N_HEADS:
            hh = slot // 2
            s_t = _nt_dot(dk_ref[0, hh, pl.ds(off, t), :], qm_sc[slot])
            if where != "far":
                s_t = s_t + bias_ref[hh, 1 if where == "near" else 0]
            s_buf[slot] = s_t
            mt_buf[slot] = jnp.max(s_t, axis=0, keepdims=True)
        else:
            hh = slot - 2 * N_HEADS
            c_k = ccol_sc[hh, pl.ds(off, t), :]
            s_t = _nt_dot(fk_ref[0, hh, pl.ds(off, t), :], fq_ref[0, hh]) - c_k
            if where == "diag":
                s_t = jnp.where(key <= qry, s_t, NEG_INF)
            s_buf[slot] = s_t
            mt_buf[slot] = jnp.max(s_t, axis=0, keepdims=True) + c_ref[0, hh, :, pl.ds(q_off, t)]

    def update(j, par, slot):
        off = pl.multiple_of(j * t, t)
        s_buf, mt_buf = bufs[par]
        vt_ref = dvt_ref if slot < 2 * N_HEADS else fvt_ref
        hh = slot // 2 if slot < 2 * N_HEADS else slot - 2 * N_HEADS
        v_t = vt_ref[0, hh, :, pl.ds(off, t)]
        m_prev = m_sc[slot]
        m_new = jnp.maximum(m_prev, mt_buf[slot])
        alpha = jnp.exp2(m_prev - m_new)
        if slot < 2 * N_HEADS:
            p_t = jnp.exp2(s_buf[slot] - m_new)
        else:
            p_t = jnp.exp2(s_buf[slot] - (m_new - c_ref[0, hh, :, pl.ds(q_off, t)]))
        acc_sc[slot] = alpha * acc_sc[slot] + jnp.dot(v_t, p_t.astype(BF16), preferred_element_type=F32)
        m_sc[slot] = m_new

    def phase(upd=None, nxt=None):
        for slot in range(N_CHAINS):
            if nxt is not None:
                logits(*nxt, slot)
            if upd is not None:
                update(*upd, slot)

    n_far = jnp.maximum(qi - 1, 0)
    far_tile = lambda k: jnp.maximum(qi - 2 - k, 0)

    phase(nxt=(qi, "diag", 0))

    @pl.when(qi == 0)
    def _():
        phase(upd=(qi, 0))

    @pl.when(qi == 1)
    def _():
        phase(upd=(qi, 0), nxt=(qi - 1, "near", 1))
        phase(upd=(qi - 1, 1))

    @pl.when(qi >= 2)
    def _():
        phase(upd=(qi, 0), nxt=(qi - 1, "near", 1))
        phase(upd=(qi - 1, 1), nxt=(far_tile(0), "far", 0))

        def far_pair(i, carry):
            k = 2 * i
            phase(upd=(far_tile(k), 0), nxt=(far_tile(k + 1), "far", 1))

            @pl.when(k + 1 < n_far)
            def _():
                phase(upd=(far_tile(k + 1), 1), nxt=(far_tile(k + 2), "far", 0))
            return carry

        lax.fori_loop(0, (n_far + 1) // 2, far_pair, 0)

    def normalised(slot):
        return acc_sc[slot, :HEAD_DIM, :] / acc_sc[slot, HEAD_DIM:HEAD_DIM + 1, :]

    def rms_t(o_t, g_t):
        ms = jnp.mean(o_t * o_t, axis=0, keepdims=True)
        return o_t * lax.rsqrt(ms + EPS) * g_t

    for hh in range(N_HEADS):
        o_t = normalised(2 * hh) - lam_ref[0] * normalised(2 * hh + 1)
        o_ref[:, hh * HEAD_DIM:(hh + 1) * HEAD_DIM] = (
            rms_t(o_t, gd_ref[...]) * (1.0 - LAM_INIT)).T.astype(BF16)
    for hh in range(N_HEADS):
        o_t = normalised(2 * N_HEADS + hh)
        o_ref[:, (N_HEADS + hh) * HEAD_DIM:(N_HEADS + hh + 1) * HEAD_DIM] = rms_t(o_t, gf_ref[...]).T.astype(BF16)


def _attn(qk, vt, bias_tiles_t, lam, cum4, subln_g, fox_g, *, batch, seq):
    t = min(ATTN_TILE, seq)
    nq = seq // t
    head_blk = N_HEADS * seq * HEAD_DIM * 2
    est = (2 * (2 * N_HEADS * t * HEAD_DIM * 2 + 4 * head_blk + N_HEADS * 2 * t * t * 4 + N_HEADS * 8 * seq * 4
                + t * 2 * N_HEADS * HEAD_DIM * 2)
           + N_CHAINS * (HEAD_DIM + 16) * t * 4 + 2 * N_HEADS * t * HEAD_DIM * 2 + N_HEADS * seq * LANES * 4
           + (2 * N_CHAINS + 36) * t * t * 4)
    qspec = lambda blk: pl.BlockSpec((1, N_HEADS, t, HEAD_DIM), lambda b, i: (b, blk, i, 0))
    kspec = lambda blk: pl.BlockSpec((1, N_HEADS, seq, HEAD_DIM), lambda b, i: (b, blk, 0, 0))
    vspec = lambda blk: pl.BlockSpec((1, N_HEADS, VT_ROWS, seq), lambda b, i: (b, blk, 0, 0))
    return pl.pallas_call(
        functools.partial(_attn_kernel, t=t, seq=seq),
        grid=(batch, nq),
        in_specs=[
            pl.BlockSpec(memory_space=pltpu.SMEM),
            qspec(0), kspec(1), vspec(0),
            qspec(2), kspec(3), vspec(1),
            pl.BlockSpec((N_HEADS, 2, t, t), lambda b, i: (0, 0, 0, 0)),
            pl.BlockSpec((1, N_HEADS, 1, seq), lambda b, i: (b, 0, 0, 0)),
            pl.BlockSpec((HEAD_DIM, t), lambda b, i: (0, 0)),
            pl.BlockSpec((HEAD_DIM, t), lambda b, i: (0, 0)),
        ],
        out_specs=pl.BlockSpec((t, 2 * N_HEADS * HEAD_DIM), lambda b, i: (b * nq + i, 0)),
        out_shape=jax.ShapeDtypeStruct((batch * seq, 2 * N_HEADS * HEAD_DIM), BF16),
        scratch_shapes=[pltpu.VMEM((N_CHAINS, 1, t), F32),
                        pltpu.VMEM((N_CHAINS, VT_ROWS, t), F32),
                        pltpu.VMEM((2 * N_HEADS, t, HEAD_DIM), BF16),
                        pltpu.VMEM((N_HEADS, seq, 1), F32),
                        pltpu.VMEM((N_CHAINS, t, t), F32), pltpu.VMEM((N_CHAINS, 1, t), F32),
                        pltpu.VMEM((N_CHAINS, t, t), F32), pltpu.VMEM((N_CHAINS, 1, t), F32)],
        compiler_params=pltpu.CompilerParams(
            dimension_semantics=("arbitrary", "arbitrary"), vmem_limit_bytes=_vmem_limit(est)),
        name="attn",
    )(lam, qk, qk, vt, qk, qk, vt, bias_tiles_t, cum4, subln_g, fox_g)


def _out_q_kernel(x_ref, m_ref, wo_ref, gc_ref, wq_ref, qg_ref, x1_ref, qc_ref):
    x1 = x_ref[...] + jnp.dot(m_ref[...], wo_ref[...], preferred_element_type=F32)
    x1_ref[...] = x1
    hc = _rms(x1, gc_ref[...]).astype(BF16)
    q = jnp.dot(hc, wq_ref[...], preferred_element_type=F32)
    for hh in range(N_HEADS):
        sl = slice(hh * CROSS_HEAD_DIM, (hh + 1) * CROSS_HEAD_DIM)
        qh = _rms(q[:, sl], qg_ref[...]) * (CROSS_HEAD_DIM ** -0.5)
        qc_ref[:, sl] = qh.astype(BF16)


def _out_q(x2d, mixed, w_out, g_cross, w_cq, q_g):
    n = x2d.shape[0]
    tm = min(PROJ_ROWS, n)
    est = 2 * (tm * D_MODEL * 4 * 2 + tm * D_MODEL * 2 + 2 * D_MODEL * D_MODEL * 2 + tm * D_MODEL * 2) + 6 * tm * D_MODEL * 4
    return pl.pallas_call(
        _out_q_kernel,
        grid=(n // tm,),
        in_specs=[
            pl.BlockSpec((tm, D_MODEL), lambda i: (i, 0)),
            pl.BlockSpec((tm, D_MODEL), lambda i: (i, 0)),
            pl.BlockSpec((D_MODEL, D_MODEL), lambda i: (0, 0)),
            pl.BlockSpec((1, D_MODEL), lambda i: (0, 0)),
            pl.BlockSpec((D_MODEL, D_MODEL), lambda i: (0, 0)),
            pl.BlockSpec((1, CROSS_HEAD_DIM), lambda i: (0, 0)),
        ],
        out_specs=[pl.BlockSpec((tm, D_MODEL), lambda i: (i, 0)),
                   pl.BlockSpec((tm, D_MODEL), lambda i: (i, 0))],
        out_shape=[jax.ShapeDtypeStruct((n, D_MODEL), F32), jax.ShapeDtypeStruct((n, D_MODEL), BF16)],
        compiler_params=pltpu.CompilerParams(
            dimension_semantics=("arbitrary",), vmem_limit_bytes=_vmem_limit(est)),
        name="out_q",
    )(x2d, mixed, w_out, g_cross, w_cq, q_g)


def _mem_kv_kernel(mem_ref, gm_ref, w_ref, kg_ref, k_ref, v_ref):
    mn = _rms(mem_ref[...], gm_ref[...]).astype(BF16)
    kv = jnp.dot(mn, w_ref[...], preferred_element_type=F32)
    for hh in range(N_HEADS):
        sl = slice(hh * CROSS_HEAD_DIM, (hh + 1) * CROSS_HEAD_DIM)
        k_ref[0, :, sl] = _rms(kv[:, sl], kg_ref[...]).astype(BF16)
    v_ref[0] = kv[:, D_MODEL:].astype(BF16)


def _mem_kv(mem2d, g_mem, w_ckv, k_g, *, batch):
    est = 2 * (N_MEM * D_MODEL * 4 + D_MODEL * 2 * D_MODEL * 2 + 2 * N_MEM * D_MODEL * 2) + 4 * N_MEM * 2 * D_MODEL * 4
    return pl.pallas_call(
        _mem_kv_kernel,
        grid=(batch,),
        in_specs=[
            pl.BlockSpec((N_MEM, D_MODEL), lambda b: (b, 0)),
            pl.BlockSpec((1, D_MODEL), lambda b: (0, 0)),
            pl.BlockSpec((D_MODEL, 2 * D_MODEL), lambda b: (0, 0)),
            pl.BlockSpec((1, CROSS_HEAD_DIM), lambda b: (0, 0)),
        ],
        out_specs=[pl.BlockSpec((1, N_MEM, D_MODEL), lambda b: (b, 0, 0)),
                   pl.BlockSpec((1, N_MEM, D_MODEL), lambda b: (b, 0, 0))],
        out_shape=[jax.ShapeDtypeStruct((batch, N_MEM, D_MODEL), BF16),
                   jax.ShapeDtypeStruct((batch, N_MEM, D_MODEL), BF16)],
        compiler_params=pltpu.CompilerParams(
            dimension_semantics=("arbitrary",), vmem_limit_bytes=_vmem_limit(est)),
        name="mem_kv",
    )(mem2d, g_mem, w_ckv, k_g)


def _route(logits_t):
    gl = logits_t[0:N_GROUPS]
    gmax = jnp.max(gl, axis=0, keepdims=True)
    eg = jnp.exp(gl - gmax)
    p_group = eg / jnp.sum(eg, axis=0, keepdims=True)
    p_g = jnp.max(p_group, axis=0, keepdims=True)
    g_idx = jnp.full_like(p_g, float(N_GROUPS))
    for g in reversed(range(N_GROUPS)):
        g_idx = jnp.where(p_group[g:g + 1] == p_g, float(g), g_idx)

    sel = []
    for j in range(EXPERTS_PER_GROUP):
        acc = jnp.zeros_like(p_g)
        for g in range(N_GROUPS):
            r = N_GROUPS + g * EXPERTS_PER_GROUP + j
            acc = jnp.where(g_idx == float(g), logits_t[r:r + 1], acc)
        sel.append(acc)
    smax = functools.reduce(jnp.maximum, sel)
    es = [jnp.exp(s - smax) for s in sel]
    den = functools.reduce(jnp.add, es)
    p_in = [e / den for e in es]

    v0 = functools.reduce(jnp.maximum, p_in)
    i0 = jnp.full_like(v0, float(EXPERTS_PER_GROUP))
    for j in reversed(range(EXPERTS_PER_GROUP)):
        i0 = jnp.where(p_in[j] == v0, float(j), i0)
    rest = [jnp.where(i0 == float(j), -1.0, p_in[j]) for j in range(EXPERTS_PER_GROUP)]
    v1 = functools.reduce(jnp.maximum, rest)
    i1 = jnp.full_like(v1, float(EXPERTS_PER_GROUP))
    for j in reversed(range(EXPERTS_PER_GROUP)):
        i1 = jnp.where(rest[j] == v1, float(j), i1)

    tot = v0 + v1
    w0 = p_g * (v0 / tot)
    w1 = p_g * (v1 / tot)
    first = i0 < i1
    a = jnp.where(first, i0, i1)
    b = jnp.where(first, i1, i0)
    w_lo = jnp.where(first, w0, w1)
    w_hi = jnp.where(first, w1, w0)
    pair = jnp.where(a == 0.0, b - 1.0, jnp.where(a == 1.0, jnp.where(b == 3.0, 3.0, 4.0), 5.0))
    swap = a == 2.0
    wa = jnp.where(swap, w_hi, w_lo)
    wb = jnp.where(swap, w_lo, w_hi)
    bucket = g_idx * float(len(PAIRS)) + pair
    return bucket, wa, wb


def _cross_router_kernel(q_ref, k_ref, v_ref, x1_ref, wco_ref, gf_ref, whi_ref, wlo_ref, rb_ref,
                         xa_ref, rt_ref, *, tm):
    q = q_ref[...]
    k = k_ref[0]
    v = v_ref[0]
    outs = []
    for hh in range(N_HEADS):
        sl = slice(hh * CROSS_HEAD_DIM, (hh + 1) * CROSS_HEAD_DIM)
        s = _nt_dot(q[:, sl], k[:, sl])
        p = jnp.exp(s - jnp.max(s, axis=-1, keepdims=True))
        l = jnp.sum(p, axis=-1, keepdims=True)
        o = jnp.dot(p.astype(BF16), v[:, sl], preferred_element_type=F32) / l
        outs.append(o.astype(BF16))
    o = jnp.concatenate(outs, axis=1)
    x2 = x1_ref[...] + jnp.dot(o, wco_ref[...], preferred_element_type=F32)
    xa_ref[:, :D_MODEL] = x2

    h3 = _rms(x2, gf_ref[...])
    hi = h3.astype(BF16)
    lo = (h3 - hi.astype(F32)).astype(BF16)
    lt = _nt_dot(whi_ref[...], hi) + (_nt_dot(whi_ref[...], lo) + _nt_dot(wlo_ref[...], hi))
    lt = lt + rb_ref[...]
    bucket, wa, wb = _route(lt)

    sub = lax.broadcasted_iota(I32, (8, tm), 0)
    rows = jnp.where(sub == 0, bucket, jnp.where(sub == 1, wa, jnp.where(sub == 2, wb, 0.0)))
    rt_ref[0] = rows
    sub_a = lax.broadcasted_iota(I32, (AUX_COLS, tm), 0)
    aux_t = jnp.where(sub_a == 0, bucket, jnp.where(sub_a == 1, wa, jnp.where(sub_a == 2, wb, 0.0)))
    xa_ref[:, D_MODEL:] = aux_t.T


def _cross_router(qc, kc, vc, x1, w_co, g_ffn, wr_hi, wr_lo, rb, *, batch, seq):
    tm = min(CROSS_ROWS, seq)
    ns = seq // tm
    n = batch * seq
    est = 2 * (tm * D_MODEL * 2 + 2 * N_MEM * D_MODEL * 2 + tm * D_MODEL * 4 + D_MODEL * D_MODEL * 2
               + tm * ROW_W * 4) + 10 * tm * D_MODEL * 4
    return pl.pallas_call(
        functools.partial(_cross_router_kernel, tm=tm),
        grid=(batch, ns),
        in_specs=[
            pl.BlockSpec((tm, D_MODEL), lambda b, s: (b * ns + s, 0)),
            pl.BlockSpec((1, N_MEM, D_MODEL), lambda b, s: (b, 0, 0)),
            pl.BlockSpec((1, N_MEM, D_MODEL), lambda b, s: (b, 0, 0)),
            pl.BlockSpec((tm, D_MODEL), lambda b, s: (b * ns + s, 0)),
            pl.BlockSpec((D_MODEL, D_MODEL), lambda b, s: (0, 0)),
            pl.BlockSpec((1, D_MODEL), lambda b, s: (0, 0)),
            pl.BlockSpec((32, D_MODEL), lambda b, s: (0, 0)),
            pl.BlockSpec((32, D_MODEL), lambda b, s: (0, 0)),
            pl.BlockSpec((32, 1), lambda b, s: (0, 0)),
        ],
        out_specs=[pl.BlockSpec((tm, ROW_W), lambda b, s: (b * ns + s, 0)),
                   pl.BlockSpec((1, 8, tm), lambda b, s: (b * ns + s, 0, 0))],
        out_shape=[jax.ShapeDtypeStruct((n, ROW_W), F32),
                   jax.ShapeDtypeStruct((n // tm, 8, tm), F32)],
        compiler_params=pltpu.CompilerParams(
            dimension_semantics=("arbitrary", "arbitrary"), vmem_limit_bytes=_vmem_limit(est)),
        name="cross_router",
    )(qc, kc, vc, x1, w_co, g_ffn, wr_hi, wr_lo, rb)


def _rank_kernel(rt_ref, pos_ref, cnt_ref, carry_sc, off_sc, *, tm, nt):
    i = pl.program_id(0)
    sub = lax.broadcasted_iota(I32, (32, tm), 0).astype(F32)

    @pl.when(i == 0)
    def _():
        n_hit = jnp.zeros((32, 1), F32)
        for j in range(nt):
            n_hit = n_hit + jnp.sum(jnp.where(sub == rt_ref[j, 0:1, :], 1.0, 0.0), axis=1, keepdims=True)
        counts = jnp.broadcast_to(n_hit, cnt_ref.shape)
        cnt_ref[...] = counts
        tiles = jnp.floor((counts + (MOE_ROWS - 1)) * (1.0 / MOE_ROWS))
        r = lax.broadcasted_iota(I32, (32, 32), 0)
        c = lax.broadcasted_iota(I32, (32, 32), 1)
        before = jnp.where(c < r, 1.0, 0.0).astype(BF16)
        off_sc[...] = jnp.dot(before, tiles.astype(BF16), preferred_element_type=F32) * MOE_ROWS
        carry_sc[...] = jnp.zeros_like(carry_sc)

    @pl.when(i >= 1)
    def _():
        hit = sub == rt_ref[i - 1, 0:1, :]
        r = lax.broadcasted_iota(I32, (tm, tm), 0)
        c = lax.broadcasted_iota(I32, (tm, tm), 1)
        upper = jnp.where(r <= c, 1.0, 0.0).astype(BF16)
        cum = jnp.dot(jnp.where(hit, 1.0, 0.0).astype(BF16), upper, preferred_element_type=F32)
        base = carry_sc[:, 0:1] + off_sc[:, 0:1]
        pos = jnp.sum(jnp.where(hit, cum - 1.0 + base, 0.0), axis=0, keepdims=True)
        pos_ref[0] = pos.astype(I32)
        carry_sc[...] = carry_sc[...] + jnp.broadcast_to(cum[:, tm - 1:tm], carry_sc.shape)


def _rank(rt):
    nt, _, tm = rt.shape
    return pl.pallas_call(
        functools.partial(_rank_kernel, tm=tm, nt=nt),
        grid=(nt + 1,),
        in_specs=[pl.BlockSpec((nt, 8, tm), lambda i: (0, 0, 0))],
        out_specs=[pl.BlockSpec((1, 1, tm), lambda i: (jnp.maximum(i - 1, 0), 0, 0)),
                   pl.BlockSpec((32, LANES), lambda i: (0, 0))],
        out_shape=[jax.ShapeDtypeStruct((nt, 1, tm), I32), jax.ShapeDtypeStruct((32, LANES), F32)],
        scratch_shapes=[pltpu.VMEM((32, LANES), F32), pltpu.VMEM((32, LANES), F32)],
        compiler_params=pltpu.CompilerParams(dimension_semantics=("arbitrary",)),
        name="rank",
    )(rt)


def _dispatch_kernel(pos_ref, zrow_ref, zflag_ref, nact_ref, xa_ref, xs_ref, zbuf, zsem, sem, *,
                     n_tiles, rows):
    @pl.when(pl.program_id(0) == 0)
    def _():
        _zero_partial_tiles(zrow_ref, zflag_ref, nact_ref, xs_ref, zbuf, zsem, n_tiles=n_tiles)

    base = pl.program_id(0) * rows
    for r in range(rows):
        pltpu.make_async_copy(xa_ref.at[pl.ds(r, 1)], xs_ref.at[pl.ds(pos_ref[base + r], 1)], sem).start(
            priority=r % 2)
    pltpu.make_async_copy(xa_ref, xs_ref.at[pl.ds(0, rows)], sem).wait()


def _zero_partial_tiles(zrow_ref, zflag_ref, nact_ref, xs_ref, zbuf, zsem, *, n_tiles):
    zbuf[...] = jnp.zeros_like(zbuf)

    def zero_copy(row):
        return pltpu.make_async_copy(zbuf, xs_ref.at[pl.ds(pl.multiple_of(row, MOE_ROWS), MOE_ROWS)], zsem)

    def bucket_tiles(op):
        def body(b, carry):
            @pl.when(zflag_ref[b] > 0)
            def _():
                op(zero_copy(zrow_ref[b]))
            return carry
        lax.fori_loop(0, N_PAIR_BUCKETS, body, 0)

    def unused_tiles(op):
        def body(i, carry):
            op(zero_copy(i * MOE_ROWS))
            return carry
        lax.fori_loop(nact_ref[0], n_tiles, body, 0)

    bucket_tiles(lambda cp: cp.start())
    unused_tiles(lambda cp: cp.start())
    bucket_tiles(lambda cp: cp.wait())
    unused_tiles(lambda cp: cp.wait())


def _dispatch(pos, zrow, zflag, n_active, xa, *, n_tiles):
    n_tok = xa.shape[0]
    n_sorted = n_tiles * MOE_ROWS
    rows = min(DMA_ROWS, n_tok)
    return pl.pallas_call(
        functools.partial(_dispatch_kernel, n_tiles=n_tiles, rows=rows),
        grid_spec=pltpu.PrefetchScalarGridSpec(
            num_scalar_prefetch=4,
            grid=(n_tok // rows,),
            in_specs=[pl.BlockSpec((rows, ROW_W), lambda i, *_: (i, 0))],
            out_specs=pl.BlockSpec(memory_space=pl.ANY),
            scratch_shapes=[pltpu.VMEM((MOE_ROWS, ROW_W), F32), pltpu.SemaphoreType.DMA(()),
                            pltpu.SemaphoreType.DMA(())],
        ),
        out_shape=jax.ShapeDtypeStruct((n_sorted, ROW_W), F32),
        compiler_params=pltpu.CompilerParams(
            dimension_semantics=("arbitrary",), has_side_effects=True,
            vmem_limit_bytes=_vmem_limit((2 * rows + MOE_ROWS) * ROW_W * 4)),
        name="dispatch",
    )(pos, zrow, zflag, n_active, xa)


def _moe_kernel(ta_ref, tb_ref, xi_ref, nact_ref, xs_ref, gf_ref, wga_ref, wua_ref, wda_ref,
                wgb_ref, wub_ref, wdb_ref, ys_ref):
    i = pl.program_id(0)

    @pl.when(i < nact_ref[0])
    def _():
        x2 = xs_ref[:, :D_MODEL]
        aux = xs_ref[:, D_MODEL:]
        wa = aux[:, 1:2]
        wb = aux[:, 2:3]
        h = _rms(x2, gf_ref[...]).astype(BF16)

        def mlp(wg, wu, wd):
            g = jnp.dot(h, wg[0].astype(BF16), preferred_element_type=F32)
            u = jnp.dot(h, wu[0].astype(BF16), preferred_element_type=F32)
            act = (g / (1.0 + jnp.exp(-g))) * u
            return jnp.dot(act.astype(BF16), wd[0].astype(BF16), preferred_element_type=F32)

        ys_ref[...] = x2 + (wa * mlp(wga_ref, wua_ref, wda_ref) + wb * mlp(wgb_ref, wub_ref, wdb_ref))

    @pl.when(i >= nact_ref[0])
    def _():
        ys_ref[...] = jnp.zeros_like(ys_ref)


def _moe(tile_a, tile_b, tile_x, n_active, xs, g_ffn, w_gate, w_up, w_down):
    n_tiles = xs.shape[0] // MOE_ROWS
    t = MOE_ROWS
    wa_map = lambda i, ta, tb, xi, na: (ta[i], 0, 0)
    wb_map = lambda i, ta, tb, xi, na: (tb[i], 0, 0)
    wspec = pl.BlockSpec
    est = 2 * (t * ROW_W * 4 + 6 * D_MODEL * D_EXPERT * 4 + t * D_MODEL * 4) + 10 * t * D_MODEL * 4
    return pl.pallas_call(
        _moe_kernel,
        grid_spec=pltpu.PrefetchScalarGridSpec(
            num_scalar_prefetch=4,
            grid=(n_tiles,),
            in_specs=[
                pl.BlockSpec((t, ROW_W), lambda i, ta, tb, xi, na: (xi[i], 0)),
                pl.BlockSpec((1, D_MODEL), lambda i, ta, tb, xi, na: (0, 0)),
                wspec((1, D_MODEL, D_EXPERT), wa_map),
                wspec((1, D_MODEL, D_EXPERT), wa_map),
                wspec((1, D_EXPERT, D_MODEL), wa_map),
                wspec((1, D_MODEL, D_EXPERT), wb_map),
                wspec((1, D_MODEL, D_EXPERT), wb_map),
                wspec((1, D_EXPERT, D_MODEL), wb_map),
            ],
            out_specs=pl.BlockSpec((t, D_MODEL), lambda i, ta, tb, xi, na: (i, 0)),
        ),
        out_shape=jax.ShapeDtypeStruct((n_tiles * t, D_MODEL), F32),
        compiler_params=pltpu.CompilerParams(
            dimension_semantics=("arbitrary",), vmem_limit_bytes=_vmem_limit(est)),
        name="moe",
    )(tile_a, tile_b, tile_x, n_active, xs, g_ffn, w_gate, w_up, w_down, w_gate, w_up, w_down)


def _combine_kernel(pos_ref, ys_ref, out_ref, sem, *, rows):
    base = pl.program_id(0) * rows
    for r in range(rows):
        pltpu.make_async_copy(ys_ref.at[pl.ds(pos_ref[base + r], 1)], out_ref.at[pl.ds(r, 1)], sem).start(
            priority=r % 2)
    pltpu.make_async_copy(ys_ref.at[pl.ds(0, rows)], out_ref, sem).wait()


def _combine(pos, ys, *, n_tok):
    rows = min(DMA_ROWS, n_tok)
    return pl.pallas_call(
        functools.partial(_combine_kernel, rows=rows),
        grid_spec=pltpu.PrefetchScalarGridSpec(
            num_scalar_prefetch=1,
            grid=(n_tok // rows,),
            in_specs=[pl.BlockSpec(memory_space=pl.ANY)],
            out_specs=pl.BlockSpec((rows, D_MODEL), lambda i, *_: (i, 0)),
            scratch_shapes=[pltpu.SemaphoreType.DMA(())],
        ),
        out_shape=jax.ShapeDtypeStruct((n_tok, D_MODEL), F32),
        compiler_params=pltpu.CompilerParams(
            dimension_semantics=("arbitrary",), has_side_effects=True,
            vmem_limit_bytes=_vmem_limit(2 * rows * D_MODEL * 4)),
        name="combine",
    )(pos, ys)


def _t5_bucket(rel):
    nb = N_BUCKETS_T5 // 2
    max_exact = nb // 2
    ret = (rel > 0).astype(I32) * nb
    n = jnp.abs(rel)
    nf = jnp.maximum(n, 1).astype(F32)
    large = max_exact + (jnp.log(nf / max_exact) / math.log(MAX_DISTANCE / max_exact)
                         * (nb - max_exact)).astype(I32)
    large = jnp.minimum(large, nb - 1)
    return ret + jnp.where(n < max_exact, n, large)


def _diff_bias(rel_bias, t):
    assert t % CHUNK == 0 and t >= MAX_DISTANCE
    n_rel = 3 * t
    vec = rel_bias[_t5_bucket(jnp.arange(n_rel, dtype=I32) - (2 * t - 1))].astype(F32).T
    toe = jnp.tile(vec, (1, t))[:, :t * (n_rel - 1)].reshape(N_HEADS, t, n_rel - 1)
    qpos = jnp.arange(t, dtype=I32)[:, None]
    kpos = jnp.arange(t, dtype=I32)[None, :]
    far = rel_bias[_t5_bucket(jnp.asarray(-2 * t, I32))].astype(F32)[:, None, None]
    b0 = (toe[:, :, 2 * t - 1:3 * t - 1] - far) * LOG2E
    b0 = jnp.where((kpos // CHUNK <= qpos // CHUNK)[None], b0, NEG_INF)
    b1 = (toe[:, :, t - 1:2 * t - 1] - far) * LOG2E
    return jnp.swapaxes(jnp.stack([b0, b1], axis=1), -1, -2)


def _tile_plan(counts, n_tiles):
    t = MOE_ROWS
    tiles = (counts + t - 1) // t
    ends = jnp.cumsum(tiles)
    n_active = ends[-1]
    tile_idx = jnp.minimum(jnp.arange(n_tiles, dtype=I32), n_active - 1)
    tile_bucket = jnp.minimum(jnp.sum((ends[None, :] <= tile_idx[:, None]).astype(I32), axis=1), N_PAIR_BUCKETS - 1)
    pa = jnp.asarray([p[0] for p in PAIRS], I32)
    pb = jnp.asarray([p[1] for p in PAIRS], I32)
    grp = tile_bucket // len(PAIRS)
    pair = tile_bucket % len(PAIRS)
    tile_a = grp * EXPERTS_PER_GROUP + pa[pair]
    tile_b = grp * EXPERTS_PER_GROUP + pb[pair]
    zrow = ((ends - 1) * t).astype(I32)
    zflag = (counts > 0).astype(I32)
    return zrow, zflag, tile_a.astype(I32), tile_b.astype(I32), tile_idx.astype(I32), n_active.astype(I32)


def kernel(x, mem, rel_bias, norm_mix_g, w_in, b_forget, diff_q_norm_g, diff_k_norm_g, diff_lambda_q1, diff_lambda_k1, diff_lambda_q2, diff_lambda_k2, diff_subln_g, fox_q_norm_g, fox_k_norm_g, fox_out_norm_g, w_out, norm_cross_g, norm_mem_g, w_cq, w_ckv, cross_q_norm_g, cross_k_norm_g, w_co, norm_ffn_g, w_group_router, b_group_router, w_expert_router, b_expert_router, w_exp_gate, w_exp_up, w_exp_down):
    batch, seq, d = x.shape
    assert d == D_MODEL and norm_mix_g.shape[0] == 1 and mem.shape[1] == N_MEM
    n_tok = batch * seq
    assert seq % ATTN_TILE == 0 or seq < ATTN_TILE
    l = 0

    w_qk = jnp.concatenate([w_in[l, :, :1024], w_in[l, :, 1536:2560]], axis=1).astype(BF16)
    w_vt = jnp.concatenate([w_in[l, :, 1024:1536], w_in[l, :, 2560:3072]], axis=1).T.astype(BF16)
    wft = jnp.zeros((8, D_MODEL), F32).at[:N_HEADS].set(w_in[l, :, 3072:].T).astype(BF16)
    bf = jnp.zeros((8, 1), F32).at[:N_HEADS, 0].set(b_forget[l].astype(F32))
    ones = jnp.ones((HEAD_DIM,), F32)
    qg = jnp.stack([
        jnp.tile(diff_q_norm_g[l].astype(F32), 2) * (DIFF_QK_DIM ** -0.5 * LOG2E),
        jnp.tile(diff_k_norm_g[l].astype(F32), 2),
        fox_q_norm_g[l].astype(F32) * (HEAD_DIM ** -0.5 * LOG2E), fox_k_norm_g[l].astype(F32),
        ones, ones, ones, ones])
    lam = (jnp.exp(jnp.sum(diff_lambda_q1[l].astype(F32) * diff_lambda_k1[l].astype(F32)))
           - jnp.exp(jnp.sum(diff_lambda_q2[l].astype(F32) * diff_lambda_k2[l].astype(F32)))
           + LAM_INIT).reshape(1)
    t_attn = min(ATTN_TILE, seq)
    bias_tiles_t = _diff_bias(rel_bias, t_attn)
    wr = jnp.zeros((32, D_MODEL), F32)
    wr = wr.at[:N_GROUPS].set(w_group_router[l].T).at[N_GROUPS:N_GROUPS + N_EXPERTS].set(w_expert_router[l].T)
    wr_hi = wr.astype(BF16)
    wr_lo = (wr - wr_hi.astype(F32)).astype(BF16)
    rb = jnp.zeros((32, 1), F32)
    rb = rb.at[:N_GROUPS, 0].set(b_group_router[l]).at[N_GROUPS:N_GROUPS + N_EXPERTS, 0].set(b_expert_router[l])

    row = lambda v: v.astype(F32).reshape(1, -1)
    x2d = x.reshape(n_tok, D_MODEL)

    qk, vt, cum = _mix_proj(x2d, row(norm_mix_g[l]), w_qk, w_vt, wft, bf, qg, batch=batch, seq=seq)
    cum4 = cum[:, :N_HEADS].reshape(batch, N_HEADS, 1, seq)
    gain_t = lambda v: jnp.broadcast_to(v.astype(F32)[:, None], (HEAD_DIM, t_attn))
    mixed = _attn(qk, vt, bias_tiles_t, lam, cum4, gain_t(diff_subln_g[l]), gain_t(fox_out_norm_g[l]),
                  batch=batch, seq=seq)
    x1, qc = _out_q(x2d, mixed, w_out[l].astype(BF16), row(norm_cross_g[l]),
                    w_cq[l].astype(BF16), row(cross_q_norm_g[l]))

    kc, vc = _mem_kv(mem.reshape(batch * N_MEM, D_MODEL), row(norm_mem_g[l]), w_ckv[l].astype(BF16),
                     row(cross_k_norm_g[l]), batch=batch)
    xa, rt = _cross_router(qc, kc, vc, x1, w_co[l].astype(BF16), row(norm_ffn_g[l]), wr_hi, wr_lo, rb,
                           batch=batch, seq=seq)

    pos, cnt = _rank(rt)
    pos = pos.reshape(n_tok)
    counts = cnt[:N_PAIR_BUCKETS, 0].astype(I32)
    n_tiles = n_tok // MOE_ROWS + N_PAIR_BUCKETS
    zrow, zflag, tile_a, tile_b, tile_x, n_active = _tile_plan(counts, n_tiles)
    n_active = n_active.reshape(1)
    xs = _dispatch(pos, zrow, zflag, n_active, xa, n_tiles=n_tiles)
    ys = _moe(tile_a, tile_b, tile_x, n_active, xs, row(norm_ffn_g[l]),
              w_exp_gate[l].astype(F32), w_exp_up[l].astype(F32), w_exp_down[l].astype(F32))
    out = _combine(pos, ys, n_tok=n_tok)
    return out.reshape(batch, seq, D_MODEL)
```

```python
import functools
import math

import jax
import jax.numpy as jnp
import numpy as np
from jax import lax
from jax.experimental import pallas as pl
from jax.experimental.pallas import tpu as pltpu

F32 = jnp.float32
BF16 = jnp.bfloat16
I32 = jnp.int32

D_MODEL = 1024
CHUNK = 64
N_MEM = 256
N_HEADS = 4
HEAD_DIM = 128
VT_ROWS = HEAD_DIM + 16
DIFF_QK_DIM = 64
CROSS_HEAD_DIM = 256
N_BUCKETS_T5 = 32
MAX_DISTANCE = 128
N_GROUPS = 4
EXPERTS_PER_GROUP = 4
N_EXPERTS = N_GROUPS * EXPERTS_PER_GROUP
D_EXPERT = 512
EPS = 1e-6
NEG_INF = -1e30
LAM_INIT = 0.8 - 0.6 * math.exp(-0.3 * 0)
LOG2E = math.log2(math.e)

PAIRS = ((0, 1), (0, 2), (0, 3), (1, 3), (1, 2), (3, 2))
N_PAIR_BUCKETS = N_GROUPS * len(PAIRS)

LANES = 128
VMEM_LIMIT_CAP = 56 * 1024 * 1024

PROJ_ROWS = 1024
ATTN_TILE = 256
CROSS_ROWS = 512
MOE_ROWS = 256
AUX_COLS = LANES
ROW_W = D_MODEL + AUX_COLS
DMA_ROWS = 2048


def _vmem_limit(nbytes):
    return int(min(max(nbytes * 5 // 4, 32 * 1024 * 1024), VMEM_LIMIT_CAP))


def _nt_dot(a, b):
    return lax.dot_general(a, b, (((1,), (1,)), ((), ())), preferred_element_type=F32)


def _rms(x, g):
    ms = jnp.mean(x * x, axis=-1, keepdims=True)
    return x * lax.rsqrt(ms + EPS) * g


def _mix_proj_kernel(x_ref, g_ref, w_ref, wvt_ref, wft_ref, bf_ref, qg_ref, o_ref, vt_ref, cum_ref, carry_ref,
                     *, tm):
    si = pl.program_id(1)
    h = _rms(x_ref[...], g_ref[...]).astype(BF16)
    lane = lax.broadcasted_iota(I32, (tm, HEAD_DIM), 1)
    lo = lane < DIFF_QK_DIM
    for kind in range(4):
        p = jnp.dot(h, w_ref[:, kind * 512:(kind + 1) * 512], preferred_element_type=F32)
        for hh in range(N_HEADS):
            ph = p[:, hh * HEAD_DIM:(hh + 1) * HEAD_DIM]
            if kind in (0, 1):
                sq = ph * ph
                s_lo = jnp.sum(jnp.where(lo, sq, 0.0), axis=-1, keepdims=True)
                s_hi = jnp.sum(jnp.where(lo, 0.0, sq), axis=-1, keepdims=True)
                ms = jnp.where(lo, s_lo, s_hi) * (1.0 / DIFF_QK_DIM)
                ph = ph * lax.rsqrt(ms + EPS) * qg_ref[kind:kind + 1, :]
            else:
                ph = _rms(ph, qg_ref[kind:kind + 1, :])
            o_ref[0, kind * N_HEADS + hh] = ph.astype(BF16)
    for kind in range(2):
        pt = _nt_dot(wvt_ref[kind * 512:(kind + 1) * 512, :], h)
        for hh in range(N_HEADS):
            vt_ref[0, kind * N_HEADS + hh, :HEAD_DIM, :] = pt[hh * HEAD_DIM:(hh + 1) * HEAD_DIM, :].astype(BF16)
            vt_ref[0, kind * N_HEADS + hh, HEAD_DIM:, :] = jnp.ones((VT_ROWS - HEAD_DIM, tm), BF16)

    z = _nt_dot(wft_ref[...], h) + bf_ref[...]
    logf = jnp.minimum(z, 0.0) - jnp.log(1.0 + jnp.exp(-jnp.abs(z)))
    lane8 = lax.broadcasted_iota(I32, (8, tm), 1)
    c = logf * LOG2E
    k = 1
    while k < tm:
        c = c + jnp.where(lane8 >= k, pltpu.roll(c, k, axis=1), 0.0)
        k *= 2

    @pl.when(si == 0)
    def _():
        carry_ref[...] = jnp.zeros_like(carry_ref)

    c = c + carry_ref[:, 0:1]
    cum_ref[0] = c
    carry_ref[...] = jnp.broadcast_to(c[:, tm - 1:tm], carry_ref.shape)


def _mix_proj(x2d, g, w_qk, w_vt, wft, bf, qg, *, batch, seq):
    tm = min(PROJ_ROWS, seq)
    ns = seq // tm
    est = 2 * (tm * D_MODEL * 4 + D_MODEL * 3072 * 2 + 24 * tm * HEAD_DIM * 2) + 8 * tm * D_MODEL
    return pl.pallas_call(
        functools.partial(_mix_proj_kernel, tm=tm),
        grid=(batch, ns),
        in_specs=[
            pl.BlockSpec((tm, D_MODEL), lambda b, s: (b * ns + s, 0)),
            pl.BlockSpec((1, D_MODEL), lambda b, s: (0, 0)),
            pl.BlockSpec((D_MODEL, 2048), lambda b, s: (0, 0)),
            pl.BlockSpec((1024, D_MODEL), lambda b, s: (0, 0)),
            pl.BlockSpec((8, D_MODEL), lambda b, s: (0, 0)),
            pl.BlockSpec((8, 1), lambda b, s: (0, 0)),
            pl.BlockSpec((8, HEAD_DIM), lambda b, s: (0, 0)),
        ],
        out_specs=[
            pl.BlockSpec((1, 16, tm, HEAD_DIM), lambda b, s: (b, 0, s, 0)),
            pl.BlockSpec((1, 8, VT_ROWS, tm), lambda b, s: (b, 0, 0, s)),
            pl.BlockSpec((1, 8, tm), lambda b, s: (b, 0, s)),
        ],
        out_shape=[
            jax.ShapeDtypeStruct((batch, 16, seq, HEAD_DIM), BF16),
            jax.ShapeDtypeStruct((batch, 8, VT_ROWS, seq), BF16),
            jax.ShapeDtypeStruct((batch, 8, seq), F32),
        ],
        scratch_shapes=[pltpu.VMEM((8, LANES), F32)],
        compiler_params=pltpu.CompilerParams(
            dimension_semantics=("arbitrary", "arbitrary"), vmem_limit_bytes=_vmem_limit(est)),
        name="mix_proj",
    )(x2d, g, w_qk, w_vt, wft, bf, qg)


def _init_softmax(m_sc, acc_sc):
    m_sc[...] = jnp.full(m_sc.shape, NEG_INF, F32)
    acc_sc[...] = jnp.zeros_like(acc_sc)


N_CHAINS = 3 * N_HEADS


def _attn_kernel(lam_ref, dq_ref, dk_ref, dvt_ref, fq_ref, fk_ref, fvt_ref, bias_ref, c_ref,
                 gd_ref, gf_ref, o_ref, m_sc, acc_sc, qm_sc, ccol_sc, s_sc, mt_sc, s2_sc, mt2_sc, *, t, seq):
    qi = pl.program_id(1)
    _init_softmax(m_sc, acc_sc)
    key = lax.broadcasted_iota(I32, (t, t), 0)
    qry = lax.broadcasted_iota(I32, (t, t), 1)

    lane = lax.broadcasted_iota(I32, (t, HEAD_DIM), 1)
    for hh in range(N_HEADS):
        q = dq_ref[0, hh]
        zero = jnp.zeros_like(q)
        qm_sc[2 * hh] = jnp.where(lane < DIFF_QK_DIM, q, zero)
        qm_sc[2 * hh + 1] = jnp.where(lane < DIFF_QK_DIM, zero, q)

    @pl.when(qi == 0)
    def _():
        for hh in range(N_HEADS):
            for j in range(seq // t):
                row = c_ref[0, hh, :, j * t:(j + 1) * t]
                ccol_sc[hh, j * t:(j + 1) * t, :] = jnp.sum(
                    jnp.where(key == qry, jnp.broadcast_to(row, (t, t)), 0.0), axis=1, keepdims=True)

    q_off = pl.multiple_of(qi * t, t)

    bufs = ((s_sc, mt_sc), (s2_sc, mt2_sc))

    def logits(j, where, par, slot):
        off = pl.multiple_of(j * t, t)
        s_buf, mt_buf = bufs[par]
        if slot < 2 * N_HEADS:
            hh = slot // 2
            s_t = _nt_dot(dk_ref[0, hh, pl.ds(off, t), :], qm_sc[slot])
            if where != "far":
                s_t = s_t + bias_ref[hh, 1 if where == "near" else 0]
            s_buf[slot] = s_t
            mt_buf[slot] = jnp.max(s_t, axis=0, keepdims=True)
        else:
            hh = slot - 2 * N_HEADS
            c_k = ccol_sc[hh, pl.ds(off, t), :]
            s_t = _nt_dot(fk_ref[0, hh, pl.ds(off, t), :], fq_ref[0, hh]) - c_k
            if where == "diag":
                s_t = jnp.where(key <= qry, s_t, NEG_INF)
            s_buf[slot] = s_t
            mt_buf[slot] = jnp.max(s_t, axis=0, keepdims=True) + c_ref[0, hh, :, pl.ds(q_off, t)]

    def update(j, par, slot):
        off = pl.multiple_of(j * t, t)
        s_buf, mt_buf = bufs[par]
        vt_ref = dvt_ref if slot < 2 * N_HEADS else fvt_ref
        hh = slot // 2 if slot < 2 * N_HEADS else slot - 2 * N_HEADS
        v_t = vt_ref[0, hh, :, pl.ds(off, t)]
        m_prev = m_sc[slot]
        m_new = jnp.maximum(m_prev, mt_buf[slot])
        alpha = jnp.exp2(m_prev - m_new)
        if slot < 2 * N_HEADS:
            p_t = jnp.exp2(s_buf[slot] - m_new)
        else:
            p_t = jnp.exp2(s_buf[slot] - (m_new - c_ref[0, hh, :, pl.ds(q_off, t)]))
        acc_sc[slot] = alpha * acc_sc[slot] + jnp.dot(v_t, p_t.astype(BF16), preferred_element_type=F32)
        m_sc[slot] = m_new

    def phase(upd=None, nxt=None):
        for slot in range(N_CHAINS):
            if nxt is not None:
                logits(*nxt, slot)
            if upd is not None:
                update(*upd, slot)

    n_far = jnp.maximum(qi - 1, 0)
    far_tile = lambda k: jnp.maximum(qi - 2 - k, 0)

    phase(nxt=(qi, "diag", 0))

    @pl.when(qi == 0)
    def _():
        phase(upd=(qi, 0))

    @pl.when(qi == 1)
    def _():
        phase(upd=(qi, 0), nxt=(qi - 1, "near", 1))
        phase(upd=(qi - 1, 1))

    @pl.when(qi >= 2)
    def _():
        phase(upd=(qi, 0), nxt=(qi - 1, "near", 1))
        phase(upd=(qi - 1, 1), nxt=(far_tile(0), "far", 0))

        def far_pair(i, carry):
            k = 2 * i
            phase(upd=(far_tile(k), 0), nxt=(far_tile(k + 1), "far", 1))

            @pl.when(k + 1 < n_far)
            def _():
                phase(upd=(far_tile(k + 1), 1), nxt=(far_tile(k + 2), "far", 0))
            return carry

        lax.fori_loop(0, (n_far + 1) // 2, far_pair, 0)

    def normalised(slot):
        return acc_sc[slot, :HEAD_DIM, :] / acc_sc[slot, HEAD_DIM:HEAD_DIM + 1, :]

    def rms_t(o_t, g_t):
        ms = jnp.mean(o_t * o_t, axis=0, keepdims=True)
        return o_t * lax.rsqrt(ms + EPS) * g_t

    for hh in range(N_HEADS):
        o_t = normalised(2 * hh) - lam_ref[0] * normalised(2 * hh + 1)
        o_ref[:, hh * HEAD_DIM:(hh + 1) * HEAD_DIM] = (
            rms_t(o_t, gd_ref[...]) * (1.0 - LAM_INIT)).T.astype(BF16)
    for hh in range(N_HEADS):
        o_t = normalised(2 * N_HEADS + hh)
        o_ref[:, (N_HEADS + hh) * HEAD_DIM:(N_HEADS + hh + 1) * HEAD_DIM] = rms_t(o_t, gf_ref[...]).T.astype(BF16)


def _attn(qk, vt, bias_tiles_t, lam, cum4, subln_g, fox_g, *, batch, seq):
    t = min(ATTN_TILE, seq)
    nq = seq // t
    head_blk = N_HEADS * seq * HEAD_DIM * 2
    est = (2 * (2 * N_HEADS * t * HEAD_DIM * 2 + 4 * head_blk + N_HEADS * 2 * t * t * 4 + N_HEADS * 8 * seq * 4
                + t * 2 * N_HEADS * HEAD_DIM * 2)
           + N_CHAINS * (HEAD_DIM + 16) * t * 4 + 2 * N_HEADS * t * HEAD_DIM * 2 + N_HEADS * seq * LANES * 4
           + (2 * N_CHAINS + 36) * t * t * 4)
    qspec = lambda blk: pl.BlockSpec((1, N_HEADS, t, HEAD_DIM), lambda b, i: (b, blk, i, 0))
    kspec = lambda blk: pl.BlockSpec((1, N_HEADS, seq, HEAD_DIM), lambda b, i: (b, blk, 0, 0))
    vspec = lambda blk: pl.BlockSpec((1, N_HEADS, VT_ROWS, seq), lambda b, i: (b, blk, 0, 0))
    return pl.pallas_call(
        functools.partial(_attn_kernel, t=t, seq=seq),
        grid=(batch, nq),
        in_specs=[
            pl.BlockSpec(memory_space=pltpu.SMEM),
            qspec(0), kspec(1), vspec(0),
            qspec(2), kspec(3), vspec(1),
            pl.BlockSpec((N_HEADS, 2, t, t), lambda b, i: (0, 0, 0, 0)),
            pl.BlockSpec((1, N_HEADS, 1, seq), lambda b, i: (b, 0, 0, 0)),
            pl.BlockSpec((HEAD_DIM, t), lambda b, i: (0, 0)),
            pl.BlockSpec((HEAD_DIM, t), lambda b, i: (0, 0)),
        ],
        out_specs=pl.BlockSpec((t, 2 * N_HEADS * HEAD_DIM), lambda b, i: (b * nq + i, 0)),
        out_shape=jax.ShapeDtypeStruct((batch * seq, 2 * N_HEADS * HEAD_DIM), BF16),
        scratch_shapes=[pltpu.VMEM((N_CHAINS, 1, t), F32),
                        pltpu.VMEM((N_CHAINS, VT_ROWS, t), F32),
                        pltpu.VMEM((2 * N_HEADS, t, HEAD_DIM), BF16),
                        pltpu.VMEM((N_HEADS, seq, 1), F32),
                        pltpu.VMEM((N_CHAINS, t, t), F32), pltpu.VMEM((N_CHAINS, 1, t), F32),
                        pltpu.VMEM((N_CHAINS, t, t), F32), pltpu.VMEM((N_CHAINS, 1, t), F32)],
        compiler_params=pltpu.CompilerParams(
            dimension_semantics=("arbitrary", "arbitrary"), vmem_limit_bytes=_vmem_limit(est)),
        name="attn",
    )(lam, qk, qk, vt, qk, qk, vt, bias_tiles_t, cum4, subln_g, fox_g)


def _out_q_kernel(x_ref, m_ref, wo_ref, gc_ref, wq_ref, qg_ref, x1_ref, qc_ref):
    x1 = x_ref[...] + jnp.dot(m_ref[...], wo_ref[...], preferred_element_type=F32)
    x1_ref[...] = x1
    hc = _rms(x1, gc_ref[...]).astype(BF16)
    q = jnp.dot(hc, wq_ref[...], preferred_element_type=F32)
    for hh in range(N_HEADS):
        sl = slice(hh * CROSS_HEAD_DIM, (hh + 1) * CROSS_HEAD_DIM)
        qh = _rms(q[:, sl], qg_ref[...]) * (CROSS_HEAD_DIM ** -0.5)
        qc_ref[:, sl] = qh.astype(BF16)


def _out_q(x2d, mixed, w_out, g_cross, w_cq, q_g):
    n = x2d.shape[0]
    tm = min(PROJ_ROWS, n)
    est = 2 * (tm * D_MODEL * 4 * 2 + tm * D_MODEL * 2 + 2 * D_MODEL * D_MODEL * 2 + tm * D_MODEL * 2) + 6 * tm * D_MODEL * 4
    return pl.pallas_call(
        _out_q_kernel,
        grid=(n // tm,),
        in_specs=[
            pl.BlockSpec((tm, D_MODEL), lambda i: (i, 0)),
            pl.BlockSpec((tm, D_MODEL), lambda i: (i, 0)),
            pl.BlockSpec((D_MODEL, D_MODEL), lambda i: (0, 0)),
            pl.BlockSpec((1, D_MODEL), lambda i: (0, 0)),
            pl.BlockSpec((D_MODEL, D_MODEL), lambda i: (0, 0)),
            pl.BlockSpec((1, CROSS_HEAD_DIM), lambda i: (0, 0)),
        ],
        out_specs=[pl.BlockSpec((tm, D_MODEL), lambda i: (i, 0)),
                   pl.BlockSpec((tm, D_MODEL), lambda i: (i, 0))],
        out_shape=[jax.ShapeDtypeStruct((n, D_MODEL), F32), jax.ShapeDtypeStruct((n, D_MODEL), BF16)],
        compiler_params=pltpu.CompilerParams(
            dimension_semantics=("arbitrary",), vmem_limit_bytes=_vmem_limit(est)),
        name="out_q",
    )(x2d, mixed, w_out, g_cross, w_cq, q_g)


def _mem_kv_kernel(mem_ref, gm_ref, w_ref, kg_ref, k_ref, v_ref):
    mn = _rms(mem_ref[...], gm_ref[...]).astype(BF16)
    kv = jnp.dot(mn, w_ref[...], preferred_element_type=F32)
    for hh in range(N_HEADS):
        sl = slice(hh * CROSS_HEAD_DIM, (hh + 1) * CROSS_HEAD_DIM)
        k_ref[0, :, sl] = _rms(kv[:, sl], kg_ref[...]).astype(BF16)
    v_ref[0] = kv[:, D_MODEL:].astype(BF16)


def _mem_kv(mem2d, g_mem, w_ckv, k_g, *, batch):
    est = 2 * (N_MEM * D_MODEL * 4 + D_MODEL * 2 * D_MODEL * 2 + 2 * N_MEM * D_MODEL * 2) + 4 * N_MEM * 2 * D_MODEL * 4
    return pl.pallas_call(
        _mem_kv_kernel,
        grid=(batch,),
        in_specs=[
            pl.BlockSpec((N_MEM, D_MODEL), lambda b: (b, 0)),
            pl.BlockSpec((1, D_MODEL), lambda b: (0, 0)),
            pl.BlockSpec((D_MODEL, 2 * D_MODEL), lambda b: (0, 0)),
            pl.BlockSpec((1, CROSS_HEAD_DIM), lambda b: (0, 0)),
        ],
        out_specs=[pl.BlockSpec((1, N_MEM, D_MODEL), lambda b: (b, 0, 0)),
                   pl.BlockSpec((1, N_MEM, D_MODEL), lambda b: (b, 0, 0))],
        out_shape=[jax.ShapeDtypeStruct((batch, N_MEM, D_MODEL), BF16),
                   jax.ShapeDtypeStruct((batch, N_MEM, D_MODEL), BF16)],
        compiler_params=pltpu.CompilerParams(
            dimension_semantics=("arbitrary",), vmem_limit_bytes=_vmem_limit(est)),
        name="mem_kv",
    )(mem2d, g_mem, w_ckv, k_g)


def _route(logits_t):
    gl = logits_t[0:N_GROUPS]
    gmax = jnp.max(gl, axis=0, keepdims=True)
    eg = jnp.exp(gl - gmax)
    p_group = eg / jnp.sum(eg, axis=0, keepdims=True)
    p_g = jnp.max(p_group, axis=0, keepdims=True)
    g_idx = jnp.full_like(p_g, float(N_GROUPS))
    for g in reversed(range(N_GROUPS)):
        g_idx = jnp.where(p_group[g:g + 1] == p_g, float(g), g_idx)

    sel = []
    for j in range(EXPERTS_PER_GROUP):
        acc = jnp.zeros_like(p_g)
        for g in range(N_GROUPS):
            r = N_GROUPS + g * EXPERTS_PER_GROUP + j
            acc = jnp.where(g_idx == float(g), logits_t[r:r + 1], acc)
        sel.append(acc)
    smax = functools.reduce(jnp.maximum, sel)
    es = [jnp.exp(s - smax) for s in sel]
    den = functools.reduce(jnp.add, es)
    p_in = [e / den for e in es]

    v0 = functools.reduce(jnp.maximum, p_in)
    i0 = jnp.full_like(v0, float(EXPERTS_PER_GROUP))
    for j in reversed(range(EXPERTS_PER_GROUP)):
        i0 = jnp.where(p_in[j] == v0, float(j), i0)
    rest = [jnp.where(i0 == float(j), -1.0, p_in[j]) for j in range(EXPERTS_PER_GROUP)]
    v1 = functools.reduce(jnp.maximum, rest)
    i1 = jnp.full_like(v1, float(EXPERTS_PER_GROUP))
    for j in reversed(range(EXPERTS_PER_GROUP)):
        i1 = jnp.where(rest[j] == v1, float(j), i1)

    tot = v0 + v1
    w0 = p_g * (v0 / tot)
    w1 = p_g * (v1 / tot)
    first = i0 < i1
    a = jnp.where(first, i0, i1)
    b = jnp.where(first, i1, i0)
    w_lo = jnp.where(first, w0, w1)
    w_hi = jnp.where(first, w1, w0)
    pair = jnp.where(a == 0.0, b - 1.0, jnp.where(a == 1.0, jnp.where(b == 3.0, 3.0, 4.0), 5.0))
    swap = a == 2.0
    wa = jnp.where(swap, w_hi, w_lo)
    wb = jnp.where(swap, w_lo, w_hi)
    bucket = g_idx * float(len(PAIRS)) + pair
    return bucket, wa, wb


def _cross_router_kernel(q_ref, k_ref, v_ref, x1_ref, wco_ref, gf_ref, whi_ref, wlo_ref, rb_ref,
                         xa_ref, rt_ref, *, tm):
    q = q_ref[...]
    k = k_ref[0]
    v = v_ref[0]
    outs = []
    for hh in range(N_HEADS):
        sl = slice(hh * CROSS_HEAD_DIM, (hh + 1) * CROSS_HEAD_DIM)
        s = _nt_dot(q[:, sl], k[:, sl])
        p = jnp.exp(s - jnp.max(s, axis=-1, keepdims=True))
        l = jnp.sum(p, axis=-1, keepdims=True)
        o = jnp.dot(p.astype(BF16), v[:, sl], preferred_element_type=F32) / l
        outs.append(o.astype(BF16))
    o = jnp.concatenate(outs, axis=1)
    x2 = x1_ref[...] + jnp.dot(o, wco_ref[...], preferred_element_type=F32)
    xa_ref[:, :D_MODEL] = x2

    h3 = _rms(x2, gf_ref[...])
    hi = h3.astype(BF16)
    lo = (h3 - hi.astype(F32)).astype(BF16)
    lt = _nt_dot(whi_ref[...], hi) + (_nt_dot(whi_ref[...], lo) + _nt_dot(wlo_ref[...], hi))
    lt = lt + rb_ref[...]
    bucket, wa, wb = _route(lt)

    sub = lax.broadcasted_iota(I32, (8, tm), 0)
    rows = jnp.where(sub == 0, bucket, jnp.where(sub == 1, wa, jnp.where(sub == 2, wb, 0.0)))
    rt_ref[0] = rows
    sub_a = lax.broadcasted_iota(I32, (AUX_COLS, tm), 0)
    aux_t = jnp.where(sub_a == 0, bucket, jnp.where(sub_a == 1, wa, jnp.where(sub_a == 2, wb, 0.0)))
    xa_ref[:, D_MODEL:] = aux_t.T


def _cross_router(qc, kc, vc, x1, w_co, g_ffn, wr_hi, wr_lo, rb, *, batch, seq):
    tm = min(CROSS_ROWS, seq)
    ns = seq // tm
    n = batch * seq
    est = 2 * (tm * D_MODEL * 2 + 2 * N_MEM * D_MODEL * 2 + tm * D_MODEL * 4 + D_MODEL * D_MODEL * 2
               + tm * ROW_W * 4) + 10 * tm * D_MODEL * 4
    return pl.pallas_call(
        functools.partial(_cross_router_kernel, tm=tm),
        grid=(batch, ns),
        in_specs=[
            pl.BlockSpec((tm, D_MODEL), lambda b, s: (b * ns + s, 0)),
            pl.BlockSpec((1, N_MEM, D_MODEL), lambda b, s: (b, 0, 0)),
            pl.BlockSpec((1, N_MEM, D_MODEL), lambda b, s: (b, 0, 0)),
            pl.BlockSpec((tm, D_MODEL), lambda b, s: (b * ns + s, 0)),
            pl.BlockSpec((D_MODEL, D_MODEL), lambda b, s: (0, 0)),
            pl.BlockSpec((1, D_MODEL), lambda b, s: (0, 0)),
            pl.BlockSpec((32, D_MODEL), lambda b, s: (0, 0)),
            pl.BlockSpec((32, D_MODEL), lambda b, s: (0, 0)),
            pl.BlockSpec((32, 1), lambda b, s: (0, 0)),
        ],
        out_specs=[pl.BlockSpec((tm, ROW_W), lambda b, s: (b * ns + s, 0)),
                   pl.BlockSpec((1, 8, tm), lambda b, s: (b * ns + s, 0, 0))],
        out_shape=[jax.ShapeDtypeStruct((n, ROW_W), F32),
                   jax.ShapeDtypeStruct((n // tm, 8, tm), F32)],
        compiler_params=pltpu.CompilerParams(
            dimension_semantics=("arbitrary", "arbitrary"), vmem_limit_bytes=_vmem_limit(est)),
        name="cross_router",
    )(qc, kc, vc, x1, w_co, g_ffn, wr_hi, wr_lo, rb)


def _rank_kernel(rt_ref, pos_ref, cnt_ref, carry_sc, off_sc, *, tm, nt):
    i = pl.program_id(0)
    sub = lax.broadcasted_iota(I32, (32, tm), 0).astype(F32)

    @pl.when(i == 0)
    def _():
        n_hit = jnp.zeros((32, 1), F32)
        for j in range(nt):
            n_hit = n_hit + jnp.sum(jnp.where(sub == rt_ref[j, 0:1, :], 1.0, 0.0), axis=1, keepdims=True)
        counts = jnp.broadcast_to(n_hit, cnt_ref.shape)
        cnt_ref[...] = counts
        tiles = jnp.floor((counts + (MOE_ROWS - 1)) * (1.0 / MOE_ROWS))
        r = lax.broadcasted_iota(I32, (32, 32), 0)
        c = lax.broadcasted_iota(I32, (32, 32), 1)
        before = jnp.where(c < r, 1.0, 0.0).astype(BF16)
        off_sc[...] = jnp.dot(before, tiles.astype(BF16), preferred_element_type=F32) * MOE_ROWS
        carry_sc[...] = jnp.zeros_like(carry_sc)

    @pl.when(i >= 1)
    def _():
        hit = sub == rt_ref[i - 1, 0:1, :]
        r = lax.broadcasted_iota(I32, (tm, tm), 0)
        c = lax.broadcasted_iota(I32, (tm, tm), 1)
        upper = jnp.where(r <= c, 1.0, 0.0).astype(BF16)
        cum = jnp.dot(jnp.where(hit, 1.0, 0.0).astype(BF16), upper, preferred_element_type=F32)
        base = carry_sc[:, 0:1] + off_sc[:, 0:1]
        pos = jnp.sum(jnp.where(hit, cum - 1.0 + base, 0.0), axis=0, keepdims=True)
        pos_ref[0] = pos.astype(I32)
        carry_sc[...] = carry_sc[...] + jnp.broadcast_to(cum[:, tm - 1:tm], carry_sc.shape)


def _rank(rt):
    nt, _, tm = rt.shape
    return pl.pallas_call(
        functools.partial(_rank_kernel, tm=tm, nt=nt),
        grid=(nt + 1,),
        in_specs=[pl.BlockSpec((nt, 8, tm), lambda i: (0, 0, 0))],
        out_specs=[pl.BlockSpec((1, 1, tm), lambda i: (jnp.maximum(i - 1, 0), 0, 0)),
                   pl.BlockSpec((32, LANES), lambda i: (0, 0))],
        out_shape=[jax.ShapeDtypeStruct((nt, 1, tm), I32), jax.ShapeDtypeStruct((32, LANES), F32)],
        scratch_shapes=[pltpu.VMEM((32, LANES), F32), pltpu.VMEM((32, LANES), F32)],
        compiler_params=pltpu.CompilerParams(dimension_semantics=("arbitrary",)),
        name="rank",
    )(rt)


def _dispatch_kernel(pos_ref, zrow_ref, zflag_ref, nact_ref, xa_ref, xs_ref, zbuf, zsem, sem, *,
                     n_tiles, rows):
    @pl.when(pl.program_id(0) == 0)
    def _():
        _zero_partial_tiles(zrow_ref, zflag_ref, nact_ref, xs_ref, zbuf, zsem, n_tiles=n_tiles)

    base = pl.program_id(0) * rows
    for r in range(rows):
        pltpu.make_async_copy(xa_ref.at[pl.ds(r, 1)], xs_ref.at[pl.ds(pos_ref[base + r], 1)], sem).start(
            priority=r % 2)
    pltpu.make_async_copy(xa_ref, xs_ref.at[pl.ds(0, rows)], sem).wait()


def _zero_partial_tiles(zrow_ref, zflag_ref, nact_ref, xs_ref, zbuf, zsem, *, n_tiles):
    zbuf[...] = jnp.zeros_like(zbuf)

    def zero_copy(row):
        return pltpu.make_async_copy(zbuf, xs_ref.at[pl.ds(pl.multiple_of(row, MOE_ROWS), MOE_ROWS)], zsem)

    def bucket_tiles(op):
        def body(b, carry):
            @pl.when(zflag_ref[b] > 0)
            def _():
                op(zero_copy(zrow_ref[b]))
            return carry
        lax.fori_loop(0, N_PAIR_BUCKETS, body, 0)

    def unused_tiles(op):
        def body(i, carry):
            op(zero_copy(i * MOE_ROWS))
            return carry
        lax.fori_loop(nact_ref[0], n_tiles, body, 0)

    bucket_tiles(lambda cp: cp.start())
    unused_tiles(lambda cp: cp.start())
    bucket_tiles(lambda cp: cp.wait())
    unused_tiles(lambda cp: cp.wait())


def _dispatch(pos, zrow, zflag, n_active, xa, *, n_tiles):
    n_tok = xa.shape[0]
    n_sorted = n_tiles * MOE_ROWS
    rows = min(DMA_ROWS, n_tok)
    return pl.pallas_call(
        functools.partial(_dispatch_kernel, n_tiles=n_tiles, rows=rows),
        grid_spec=pltpu.PrefetchScalarGridSpec(
            num_scalar_prefetch=4,
            grid=(n_tok // rows,),
            in_specs=[pl.BlockSpec((rows, ROW_W), lambda i, *_: (i, 0))],
            out_specs=pl.BlockSpec(memory_space=pl.ANY),
            scratch_shapes=[pltpu.VMEM((MOE_ROWS, ROW_W), F32), pltpu.SemaphoreType.DMA(()),
                            pltpu.SemaphoreType.DMA(())],
        ),
        out_shape=jax.ShapeDtypeStruct((n_sorted, ROW_W), F32),
        compiler_params=pltpu.CompilerParams(
            dimension_semantics=("arbitrary",), has_side_effects=True,
            vmem_limit_bytes=_vmem_limit((2 * rows + MOE_ROWS) * ROW_W * 4)),
        name="dispatch",
    )(pos, zrow, zflag, n_active, xa)


def _moe_kernel(plan_ref, nact_ref, xs_ref, gf_ref, wg_hbm, wu_hbm, wd_hbm, ys_ref, wgu_sc, wd_sc, wsem):
    i = pl.program_id(0)

    def weight_copies(ea, eb, buf):
        cps = []
        for k, (src, e) in enumerate(((wg_hbm, ea), (wu_hbm, ea), (wg_hbm, eb), (wu_hbm, eb))):
            cps.append(pltpu.make_async_copy(src.at[e], wgu_sc.at[buf, k], wsem.at[buf]))
        for k, e in enumerate((ea, eb)):
            cps.append(pltpu.make_async_copy(wd_hbm.at[e], wd_sc.at[buf, k], wsem.at[buf]))
        return cps

    @pl.when(i < nact_ref[0])
    def _():
        buf = plan_ref[4, i]

        @pl.when(plan_ref[3, i] == 1)
        def _():
            @pl.when(i == 0)
            def _():
                for cp in weight_copies(plan_ref[1, i], plan_ref[2, i], buf):
                    cp.start()
            for cp in weight_copies(plan_ref[1, i], plan_ref[2, i], buf):
                cp.wait()

            @pl.when(plan_ref[7, i] == 1)
            def _():
                for cp in weight_copies(plan_ref[5, i], plan_ref[6, i], 1 - buf):
                    cp.start()

        x2 = xs_ref[:, :D_MODEL]
        aux = xs_ref[:, D_MODEL:]
        wa = aux[:, 1:2]
        wb = aux[:, 2:3]
        h = _rms(x2, gf_ref[...]).astype(BF16)

        def mlp(slot):
            g = jnp.dot(h, wgu_sc[buf, 2 * slot].astype(BF16), preferred_element_type=F32)
            u = jnp.dot(h, wgu_sc[buf, 2 * slot + 1].astype(BF16), preferred_element_type=F32)
            act = (g / (1.0 + jnp.exp(-g))) * u
            return jnp.dot(act.astype(BF16), wd_sc[buf, slot].astype(BF16), preferred_element_type=F32)

        ys_ref[...] = x2 + (wa * mlp(0) + wb * mlp(1))

    @pl.when(i >= nact_ref[0])
    def _():
        ys_ref[...] = jnp.zeros_like(ys_ref)


def _moe(plan, n_active, xs, g_ffn, w_gate, w_up, w_down):
    n_tiles = xs.shape[0] // MOE_ROWS
    t = MOE_ROWS
    est = (2 * (t * ROW_W * 4 + t * D_MODEL * 4) + 2 * 6 * D_MODEL * D_EXPERT * 4 + 4 * D_MODEL * D_EXPERT * 4
           + 10 * t * D_MODEL * 4)
    return pl.pallas_call(
        _moe_kernel,
        grid_spec=pltpu.PrefetchScalarGridSpec(
            num_scalar_prefetch=2,
            grid=(n_tiles,),
            in_specs=[
                pl.BlockSpec((t, ROW_W), lambda i, plan, na: (plan[0, i], 0)),
                pl.BlockSpec((1, D_MODEL), lambda i, plan, na: (0, 0)),
                pl.BlockSpec(memory_space=pl.ANY),
                pl.BlockSpec(memory_space=pl.ANY),
                pl.BlockSpec(memory_space=pl.ANY),
            ],
            out_specs=pl.BlockSpec((t, D_MODEL), lambda i, plan, na: (i, 0)),
            scratch_shapes=[pltpu.VMEM((2, 4, D_MODEL, D_EXPERT), F32), pltpu.VMEM((2, 2, D_EXPERT, D_MODEL), F32),
                            pltpu.SemaphoreType.DMA((2,))],
        ),
        out_shape=jax.ShapeDtypeStruct((n_tiles * t, D_MODEL), F32),
        compiler_params=pltpu.CompilerParams(
            dimension_semantics=("arbitrary",), vmem_limit_bytes=_vmem_limit(est)),
        name="moe",
    )(plan, n_active, xs, g_ffn, w_gate, w_up, w_down)


def _combine_kernel(pos_ref, ys_ref, out_ref, sem, *, rows):
    base = pl.program_id(0) * rows
    for r in range(rows):
        pltpu.make_async_copy(ys_ref.at[pl.ds(pos_ref[base + r], 1)], out_ref.at[pl.ds(r, 1)], sem).start(
            priority=r % 2)
    pltpu.make_async_copy(ys_ref.at[pl.ds(0, rows)], out_ref, sem).wait()


def _combine(pos, ys, *, n_tok):
    rows = min(DMA_ROWS, n_tok)
    return pl.pallas_call(
        functools.partial(_combine_kernel, rows=rows),
        grid_spec=pltpu.PrefetchScalarGridSpec(
            num_scalar_prefetch=1,
            grid=(n_tok // rows,),
            in_specs=[pl.BlockSpec(memory_space=pl.ANY)],
            out_specs=pl.BlockSpec((rows, D_MODEL), lambda i, *_: (i, 0)),
            scratch_shapes=[pltpu.SemaphoreType.DMA(())],
        ),
        out_shape=jax.ShapeDtypeStruct((n_tok, D_MODEL), F32),
        compiler_params=pltpu.CompilerParams(
            dimension_semantics=("arbitrary",), has_side_effects=True,
            vmem_limit_bytes=_vmem_limit(2 * rows * D_MODEL * 4)),
        name="combine",
    )(pos, ys)


def _t5_bucket(rel):
    nb = N_BUCKETS_T5 // 2
    max_exact = nb // 2
    ret = (rel > 0).astype(I32) * nb
    n = jnp.abs(rel)
    nf = jnp.maximum(n, 1).astype(F32)
    large = max_exact + (jnp.log(nf / max_exact) / math.log(MAX_DISTANCE / max_exact)
                         * (nb - max_exact)).astype(I32)
    large = jnp.minimum(large, nb - 1)
    return ret + jnp.where(n < max_exact, n, large)


def _diff_bias(rel_bias, t):
    assert t % CHUNK == 0 and t >= MAX_DISTANCE
    n_rel = 3 * t
    vec = rel_bias[_t5_bucket(jnp.arange(n_rel, dtype=I32) - (2 * t - 1))].astype(F32).T
    toe = jnp.tile(vec, (1, t))[:, :t * (n_rel - 1)].reshape(N_HEADS, t, n_rel - 1)
    qpos = jnp.arange(t, dtype=I32)[:, None]
    kpos = jnp.arange(t, dtype=I32)[None, :]
    far = rel_bias[_t5_bucket(jnp.asarray(-2 * t, I32))].astype(F32)[:, None, None]
    b0 = (toe[:, :, 2 * t - 1:3 * t - 1] - far) * LOG2E
    b0 = jnp.where((kpos // CHUNK <= qpos // CHUNK)[None], b0, NEG_INF)
    b1 = (toe[:, :, t - 1:2 * t - 1] - far) * LOG2E
    return jnp.swapaxes(jnp.stack([b0, b1], axis=1), -1, -2)


def _tile_plan(counts, n_tiles):
    t = MOE_ROWS
    tiles = (counts + t - 1) // t
    ends = jnp.cumsum(tiles)
    n_active = ends[-1]
    tile_idx = jnp.minimum(jnp.arange(n_tiles, dtype=I32), n_active - 1)
    tile_bucket = jnp.minimum(jnp.sum((ends[None, :] <= tile_idx[:, None]).astype(I32), axis=1), N_PAIR_BUCKETS - 1)
    pa = jnp.asarray([p[0] for p in PAIRS], I32)
    pb = jnp.asarray([p[1] for p in PAIRS], I32)
    grp = tile_bucket // len(PAIRS)
    pair = tile_bucket % len(PAIRS)
    tile_a = grp * EXPERTS_PER_GROUP + pa[pair]
    tile_b = grp * EXPERTS_PER_GROUP + pb[pair]
    zrow = ((ends - 1) * t).astype(I32)
    zflag = (counts > 0).astype(I32)
    prev_bucket = jnp.concatenate([jnp.full((1,), -1, I32), tile_bucket[:-1]])
    first = (tile_bucket != prev_bucket).astype(I32)
    buf = (jnp.cumsum(first) - 1) % 2
    nxt = ends[tile_bucket]
    has_next = (nxt < n_active).astype(I32)
    nxt = jnp.minimum(nxt, n_tiles - 1)
    plan = jnp.stack([tile_idx, tile_a, tile_b, first, buf, tile_a[nxt], tile_b[nxt], has_next]).astype(I32)
    return zrow, zflag, plan, n_active.astype(I32)


def kernel(x, mem, rel_bias, norm_mix_g, w_in, b_forget, diff_q_norm_g, diff_k_norm_g, diff_lambda_q1, diff_lambda_k1, diff_lambda_q2, diff_lambda_k2, diff_subln_g, fox_q_norm_g, fox_k_norm_g, fox_out_norm_g, w_out, norm_cross_g, norm_mem_g, w_cq, w_ckv, cross_q_norm_g, cross_k_norm_g, w_co, norm_ffn_g, w_group_router, b_group_router, w_expert_router, b_expert_router, w_exp_gate, w_exp_up, w_exp_down):
    batch, seq, d = x.shape
    assert d == D_MODEL and norm_mix_g.shape[0] == 1 and mem.shape[1] == N_MEM
    n_tok = batch * seq
    assert seq % ATTN_TILE == 0 or seq < ATTN_TILE
    l = 0

    w_qk = jnp.concatenate([w_in[l, :, :1024], w_in[l, :, 1536:2560]], axis=1).astype(BF16)
    w_vt = jnp.concatenate([w_in[l, :, 1024:1536], w_in[l, :, 2560:3072]], axis=1).T.astype(BF16)
    wft = jnp.zeros((8, D_MODEL), F32).at[:N_HEADS].set(w_in[l, :, 3072:].T).astype(BF16)
    bf = jnp.zeros((8, 1), F32).at[:N_HEADS, 0].set(b_forget[l].astype(F32))
    ones = jnp.ones((HEAD_DIM,), F32)
    qg = jnp.stack([
        jnp.tile(diff_q_norm_g[l].astype(F32), 2) * (DIFF_QK_DIM ** -0.5 * LOG2E),
        jnp.tile(diff_k_norm_g[l].astype(F32), 2),
        fox_q_norm_g[l].astype(F32) * (HEAD_DIM ** -0.5 * LOG2E), fox_k_norm_g[l].astype(F32),
        ones, ones, ones, ones])
    lam = (jnp.exp(jnp.sum(diff_lambda_q1[l].astype(F32) * diff_lambda_k1[l].astype(F32)))
           - jnp.exp(jnp.sum(diff_lambda_q2[l].astype(F32) * diff_lambda_k2[l].astype(F32)))
           + LAM_INIT).reshape(1)
    t_attn = min(ATTN_TILE, seq)
    bias_tiles_t = _diff_bias(rel_bias, t_attn)
    wr = jnp.zeros((32, D_MODEL), F32)
    wr = wr.at[:N_GROUPS].set(w_group_router[l].T).at[N_GROUPS:N_GROUPS + N_EXPERTS].set(w_expert_router[l].T)
    wr_hi = wr.astype(BF16)
    wr_lo = (wr - wr_hi.astype(F32)).astype(BF16)
    rb = jnp.zeros((32, 1), F32)
    rb = rb.at[:N_GROUPS, 0].set(b_group_router[l]).at[N_GROUPS:N_GROUPS + N_EXPERTS, 0].set(b_expert_router[l])

    row = lambda v: v.astype(F32).reshape(1, -1)
    x2d = x.reshape(n_tok, D_MODEL)

    qk, vt, cum = _mix_proj(x2d, row(norm_mix_g[l]), w_qk, w_vt, wft, bf, qg, batch=batch, seq=seq)
    cum4 = cum[:, :N_HEADS].reshape(batch, N_HEADS, 1, seq)
    gain_t = lambda v: jnp.broadcast_to(v.astype(F32)[:, None], (HEAD_DIM, t_attn))
    mixed = _attn(qk, vt, bias_tiles_t, lam, cum4, gain_t(diff_subln_g[l]), gain_t(fox_out_norm_g[l]),
                  batch=batch, seq=seq)
    x1, qc = _out_q(x2d, mixed, w_out[l].astype(BF16), row(norm_cross_g[l]),
                    w_cq[l].astype(BF16), row(cross_q_norm_g[l]))

    kc, vc = _mem_kv(mem.reshape(batch * N_MEM, D_MODEL), row(norm_mem_g[l]), w_ckv[l].astype(BF16),
                     row(cross_k_norm_g[l]), batch=batch)
    xa, rt = _cross_router(qc, kc, vc, x1, w_co[l].astype(BF16), row(norm_ffn_g[l]), wr_hi, wr_lo, rb,
                           batch=batch, seq=seq)

    pos, cnt = _rank(rt)
    pos = pos.reshape(n_tok)
    counts = cnt[:N_PAIR_BUCKETS, 0].astype(I32)
    n_tiles = n_tok // MOE_ROWS + N_PAIR_BUCKETS
    zrow, zflag, plan, n_active = _tile_plan(counts, n_tiles)
    n_active = n_active.reshape(1)
    xs = _dispatch(pos, zrow, zflag, n_active, xa, n_tiles=n_tiles)
    ys = _moe(plan, n_active, xs, row(norm_ffn_g[l]),
              w_exp_gate[l].astype(F32), w_exp_up[l].astype(F32), w_exp_down[l].astype(F32))
    out = _combine(pos, ys, n_tok=n_tok)
    return out.reshape(batch, seq, D_MODEL)
```

```python
import functools
import math

import jax
import jax.numpy as jnp
import numpy as np
from jax import lax
from jax.experimental import pallas as pl
from jax.experimental.pallas import tpu as pltpu

F32 = jnp.float32
BF16 = jnp.bfloat16
I32 = jnp.int32

D_MODEL = 1024
CHUNK = 64
N_MEM = 256
N_HEADS = 4
HEAD_DIM = 128
VT_ROWS = HEAD_DIM + 16
DIFF_QK_DIM = 64
CROSS_HEAD_DIM = 256
N_BUCKETS_T5 = 32
MAX_DISTANCE = 128
N_GROUPS = 4
EXPERTS_PER_GROUP = 4
N_EXPERTS = N_GROUPS * EXPERTS_PER_GROUP
D_EXPERT = 512
EPS = 1e-6
NEG_INF = -1e30
LAM_INIT = 0.8 - 0.6 * math.exp(-0.3 * 0)
LOG2E = math.log2(math.e)

PAIRS = ((0, 1), (0, 2), (0, 3), (1, 3), (1, 2), (3, 2))
N_PAIR_BUCKETS = N_GROUPS * len(PAIRS)

LANES = 128
VMEM_LIMIT_CAP = 56 * 1024 * 1024

PROJ_ROWS = 1024
ATTN_TILE = 256
CROSS_ROWS = 512
MOE_ROWS = 256
AUX_COLS = LANES
ROW_W = D_MODEL + AUX_COLS
DMA_ROWS = 2048


def _vmem_limit(nbytes):
    return int(min(max(nbytes * 5 // 4, 32 * 1024 * 1024), VMEM_LIMIT_CAP))


def _nt_dot(a, b):
    return lax.dot_general(a, b, (((1,), (1,)), ((), ())), preferred_element_type=F32)


def _rms(x, g):
    ms = jnp.mean(x * x, axis=-1, keepdims=True)
    return x * lax.rsqrt(ms + EPS) * g


def _mix_proj_kernel(x_ref, g_ref, w_ref, wv_ref, wft_ref, bf_ref, qg_ref, o_ref, vt_ref, cum_ref, carry_ref,
                     v_sc, *, tm):
    si = pl.program_id(1)
    h = _rms(x_ref[...], g_ref[...]).astype(BF16)
    lane = lax.broadcasted_iota(I32, (tm, HEAD_DIM), 1)
    lo = lane < DIFF_QK_DIM
    for kind in range(4):
        p = jnp.dot(h, w_ref[:, kind * 512:(kind + 1) * 512], preferred_element_type=F32)
        for hh in range(N_HEADS):
            ph = p[:, hh * HEAD_DIM:(hh + 1) * HEAD_DIM]
            if kind in (0, 1):
                sq = ph * ph
                s_lo = jnp.sum(jnp.where(lo, sq, 0.0), axis=-1, keepdims=True)
                s_hi = jnp.sum(jnp.where(lo, 0.0, sq), axis=-1, keepdims=True)
                ms = jnp.where(lo, s_lo, s_hi) * (1.0 / DIFF_QK_DIM)
                ph = ph * lax.rsqrt(ms + EPS) * qg_ref[kind:kind + 1, :]
            else:
                ph = _rms(ph, qg_ref[kind:kind + 1, :])
            o_ref[0, kind * N_HEADS + hh] = ph.astype(BF16)
    for kind in range(2):
        v_sc[...] = jnp.dot(h, wv_ref[:, kind * 512:(kind + 1) * 512], preferred_element_type=F32)
        pt = v_sc[...].T
        for hh in range(N_HEADS):
            vt_ref[0, kind * N_HEADS + hh, :HEAD_DIM, :] = pt[hh * HEAD_DIM:(hh + 1) * HEAD_DIM, :].astype(BF16)
            vt_ref[0, kind * N_HEADS + hh, HEAD_DIM:, :] = jnp.ones((VT_ROWS - HEAD_DIM, tm), BF16)

    z = _nt_dot(wft_ref[...], h) + bf_ref[...]
    logf = jnp.minimum(z, 0.0) - jnp.log(1.0 + jnp.exp(-jnp.abs(z)))
    lane8 = lax.broadcasted_iota(I32, (8, tm), 1)
    c = logf * LOG2E
    k = 1
    while k < tm:
        c = c + jnp.where(lane8 >= k, pltpu.roll(c, k, axis=1), 0.0)
        k *= 2

    @pl.when(si == 0)
    def _():
        carry_ref[...] = jnp.zeros_like(carry_ref)

    c = c + carry_ref[:, 0:1]
    cum_ref[0] = c
    carry_ref[...] = jnp.broadcast_to(c[:, tm - 1:tm], carry_ref.shape)


def _mix_proj(x2d, g, w_qk, w_vt, wft, bf, qg, *, batch, seq):
    tm = min(PROJ_ROWS, seq)
    ns = seq // tm
    est = 2 * (tm * D_MODEL * 4 + D_MODEL * 3072 * 2 + 24 * tm * HEAD_DIM * 2) + 8 * tm * D_MODEL
    return pl.pallas_call(
        functools.partial(_mix_proj_kernel, tm=tm),
        grid=(batch, ns),
        in_specs=[
            pl.BlockSpec((tm, D_MODEL), lambda b, s: (b * ns + s, 0)),
            pl.BlockSpec((1, D_MODEL), lambda b, s: (0, 0)),
            pl.BlockSpec((D_MODEL, 2048), lambda b, s: (0, 0)),
            pl.BlockSpec((1024, D_MODEL), lambda b, s: (0, 0)),
            pl.BlockSpec((8, D_MODEL), lambda b, s: (0, 0)),
            pl.BlockSpec((8, 1), lambda b, s: (0, 0)),
            pl.BlockSpec((8, HEAD_DIM), lambda b, s: (0, 0)),
        ],
        out_specs=[
            pl.BlockSpec((1, 16, tm, HEAD_DIM), lambda b, s: (b, 0, s, 0)),
            pl.BlockSpec((1, 8, VT_ROWS, tm), lambda b, s: (b, 0, 0, s)),
            pl.BlockSpec((1, 8, tm), lambda b, s: (b, 0, s)),
        ],
        out_shape=[
            jax.ShapeDtypeStruct((batch, 16, seq, HEAD_DIM), BF16),
            jax.ShapeDtypeStruct((batch, 8, VT_ROWS, seq), BF16),
            jax.ShapeDtypeStruct((batch, 8, seq), F32),
        ],
        scratch_shapes=[pltpu.VMEM((8, LANES), F32), pltpu.VMEM((tm, 512), F32)],
        compiler_params=pltpu.CompilerParams(
            dimension_semantics=("arbitrary", "arbitrary"), vmem_limit_bytes=_vmem_limit(est)),
        name="mix_proj",
    )(x2d, g, w_qk, w_vt, wft, bf, qg)


def _init_softmax(m_sc, acc_sc):
    m_sc[...] = jnp.full(m_sc.shape, NEG_INF, F32)
    acc_sc[...] = jnp.zeros_like(acc_sc)


N_CHAINS = 3 * N_HEADS


def _attn_kernel(lam_ref, dq_ref, dk_ref, dvt_ref, fq_ref, fk_ref, fvt_ref, bias_ref, c_ref,
                 gd_ref, gf_ref, o_ref, m_sc, acc_sc, qm_sc, ccol_sc, s_sc, mt_sc, s2_sc, mt2_sc, *, t, seq):
    qi = pl.program_id(1)
    _init_softmax(m_sc, acc_sc)
    key = lax.broadcasted_iota(I32, (t, t), 0)
    qry = lax.broadcasted_iota(I32, (t, t), 1)

    lane = lax.broadcasted_iota(I32, (t, HEAD_DIM), 1)
    for hh in range(N_HEADS):
        q = dq_ref[0, hh]
        zero = jnp.zeros_like(q)
        qm_sc[2 * hh] = jnp.where(lane < DIFF_QK_DIM, q, zero)
        qm_sc[2 * hh + 1] = jnp.where(lane < DIFF_QK_DIM, zero, q)

    @pl.when(qi == 0)
    def _():
        for hh in range(N_HEADS):
            for j in range(seq // t):
                row = c_ref[0, hh, :, j * t:(j + 1) * t]
                ccol_sc[hh, j * t:(j + 1) * t, :] = jnp.sum(
                    jnp.where(key == qry, jnp.broadcast_to(row, (t, t)), 0.0), axis=1, keepdims=True)

    q_off = pl.multiple_of(qi * t, t)

    bufs = ((s_sc, mt_sc), (s2_sc, mt2_sc))

    def logits(j, where, par, slot):
        off = pl.multiple_of(j * t, t)
        s_buf, mt_buf = bufs[par]
        if slot < 2 * N_HEADS:
            hh = slot // 2
            s_t = _nt_dot(dk_ref[0, hh, pl.ds(off, t), :], qm_sc[slot])
            if where != "far":
                s_t = s_t + bias_ref[hh, 1 if where == "near" else 0]
            s_buf[slot] = s_t
            mt_buf[slot] = jnp.max(s_t, axis=0, keepdims=True)
        else:
            hh = slot - 2 * N_HEADS
            c_k = ccol_sc[hh, pl.ds(off, t), :]
            s_t = _nt_dot(fk_ref[0, hh, pl.ds(off, t), :], fq_ref[0, hh]) - c_k
            if where == "diag":
                s_t = jnp.where(key <= qry, s_t, NEG_INF)
            s_buf[slot] = s_t
            mt_buf[slot] = jnp.max(s_t, axis=0, keepdims=True) + c_ref[0, hh, :, pl.ds(q_off, t)]

    def update(j, par, slot):
        off = pl.multiple_of(j * t, t)
        s_buf, mt_buf = bufs[par]
        vt_ref = dvt_ref if slot < 2 * N_HEADS else fvt_ref
        hh = slot // 2 if slot < 2 * N_HEADS else slot - 2 * N_HEADS
        v_t = vt_ref[0, hh, :, pl.ds(off, t)]
        m_prev = m_sc[slot]
        m_new = jnp.maximum(m_prev, mt_buf[slot])
        alpha = jnp.exp2(m_prev - m_new)
        if slot < 2 * N_HEADS:
            p_t = jnp.exp2(s_buf[slot] - m_new)
        else:
            p_t = jnp.exp2(s_buf[slot] - (m_new - c_ref[0, hh, :, pl.ds(q_off, t)]))
        acc_sc[slot] = alpha * acc_sc[slot] + jnp.dot(v_t, p_t.astype(BF16), preferred_element_type=F32)
        m_sc[slot] = m_new

    def phase(upd=None, nxt=None):
        for slot in range(N_CHAINS):
            if nxt is not None:
                logits(*nxt, slot)
            if upd is not None:
                update(*upd, slot)

    n_far = jnp.maximum(qi - 1, 0)
    far_tile = lambda k: jnp.maximum(qi - 2 - k, 0)

    phase(nxt=(qi, "diag", 0))

    @pl.when(qi == 0)
    def _():
        phase(upd=(qi, 0))

    @pl.when(qi == 1)
    def _():
        phase(upd=(qi, 0), nxt=(qi - 1, "near", 1))
        phase(upd=(qi - 1, 1))

    @pl.when(qi >= 2)
    def _():
        phase(upd=(qi, 0), nxt=(qi - 1, "near", 1))
        phase(upd=(qi - 1, 1), nxt=(far_tile(0), "far", 0))

        def far_pair(i, carry):
            k = 2 * i
            phase(upd=(far_tile(k), 0), nxt=(far_tile(k + 1), "far", 1))

            @pl.when(k + 1 < n_far)
            def _():
                phase(upd=(far_tile(k + 1), 1), nxt=(far_tile(k + 2), "far", 0))
            return carry

        lax.fori_loop(0, (n_far + 1) // 2, far_pair, 0)

    def normalised(slot):
        return acc_sc[slot, :HEAD_DIM, :] / acc_sc[slot, HEAD_DIM:HEAD_DIM + 1, :]

    def rms_t(o_t, g_t):
        ms = jnp.mean(o_t * o_t, axis=0, keepdims=True)
        return o_t * lax.rsqrt(ms + EPS) * g_t

    for hh in range(N_HEADS):
        o_t = normalised(2 * hh) - lam_ref[0] * normalised(2 * hh + 1)
        o_ref[:, hh * HEAD_DIM:(hh + 1) * HEAD_DIM] = (
            rms_t(o_t, gd_ref[...]) * (1.0 - LAM_INIT)).T.astype(BF16)
    for hh in range(N_HEADS):
        o_t = normalised(2 * N_HEADS + hh)
        o_ref[:, (N_HEADS + hh) * HEAD_DIM:(N_HEADS + hh + 1) * HEAD_DIM] = rms_t(o_t, gf_ref[...]).T.astype(BF16)


def _attn(qk, vt, bias_tiles_t, lam, cum4, subln_g, fox_g, *, batch, seq):
    t = min(ATTN_TILE, seq)
    nq = seq // t
    head_blk = N_HEADS * seq * HEAD_DIM * 2
    est = (2 * (2 * N_HEADS * t * HEAD_DIM * 2 + 4 * head_blk + N_HEADS * 2 * t * t * 4 + N_HEADS * 8 * seq * 4
                + t * 2 * N_HEADS * HEAD_DIM * 2)
           + N_CHAINS * (HEAD_DIM + 16) * t * 4 + 2 * N_HEADS * t * HEAD_DIM * 2 + N_HEADS * seq * LANES * 4
           + (2 * N_CHAINS + 36) * t * t * 4)
    qspec = lambda blk: pl.BlockSpec((1, N_HEADS, t, HEAD_DIM), lambda b, i: (b, blk, i, 0))
    kspec = lambda blk: pl.BlockSpec((1, N_HEADS, seq, HEAD_DIM), lambda b, i: (b, blk, 0, 0))
    vspec = lambda blk: pl.BlockSpec((1, N_HEADS, VT_ROWS, seq), lambda b, i: (b, blk, 0, 0))
    return pl.pallas_call(
        functools.partial(_attn_kernel, t=t, seq=seq),
        grid=(batch, nq),
        in_specs=[
            pl.BlockSpec(memory_space=pltpu.SMEM),
            qspec(0), kspec(1), vspec(0),
            qspec(2), kspec(3), vspec(1),
            pl.BlockSpec((N_HEADS, 2, t, t), lambda b, i: (0, 0, 0, 0)),
            pl.BlockSpec((1, N_HEADS, 1, seq), lambda b, i: (b, 0, 0, 0)),
            pl.BlockSpec((HEAD_DIM, t), lambda b, i: (0, 0)),
            pl.BlockSpec((HEAD_DIM, t), lambda b, i: (0, 0)),
        ],
        out_specs=pl.BlockSpec((t, 2 * N_HEADS * HEAD_DIM), lambda b, i: (b * nq + i, 0)),
        out_shape=jax.ShapeDtypeStruct((batch * seq, 2 * N_HEADS * HEAD_DIM), BF16),
        scratch_shapes=[pltpu.VMEM((N_CHAINS, 1, t), F32),
                        pltpu.VMEM((N_CHAINS, VT_ROWS, t), F32),
                        pltpu.VMEM((2 * N_HEADS, t, HEAD_DIM), BF16),
                        pltpu.VMEM((N_HEADS, seq, 1), F32),
                        pltpu.VMEM((N_CHAINS, t, t), F32), pltpu.VMEM((N_CHAINS, 1, t), F32),
                        pltpu.VMEM((N_CHAINS, t, t), F32), pltpu.VMEM((N_CHAINS, 1, t), F32)],
        compiler_params=pltpu.CompilerParams(
            dimension_semantics=("arbitrary", "arbitrary"), vmem_limit_bytes=_vmem_limit(est)),
        name="attn",
    )(lam, qk, qk, vt, qk, qk, vt, bias_tiles_t, cum4, subln_g, fox_g)


def _out_q_kernel(x_ref, m_ref, wo_ref, gc_ref, wq_ref, qg_ref, x1_ref, qc_ref):
    x1 = x_ref[...] + jnp.dot(m_ref[...], wo_ref[...], preferred_element_type=F32)
    x1_ref[...] = x1
    hc = _rms(x1, gc_ref[...]).astype(BF16)
    q = jnp.dot(hc, wq_ref[...], preferred_element_type=F32)
    for hh in range(N_HEADS):
        sl = slice(hh * CROSS_HEAD_DIM, (hh + 1) * CROSS_HEAD_DIM)
        qh = _rms(q[:, sl], qg_ref[...]) * (CROSS_HEAD_DIM ** -0.5)
        qc_ref[:, sl] = qh.astype(BF16)


def _out_q(x2d, mixed, w_out, g_cross, w_cq, q_g):
    n = x2d.shape[0]
    tm = min(PROJ_ROWS, n)
    est = 2 * (tm * D_MODEL * 4 * 2 + tm * D_MODEL * 2 + 2 * D_MODEL * D_MODEL * 2 + tm * D_MODEL * 2) + 6 * tm * D_MODEL * 4
    return pl.pallas_call(
        _out_q_kernel,
        grid=(n // tm,),
        in_specs=[
            pl.BlockSpec((tm, D_MODEL), lambda i: (i, 0)),
            pl.BlockSpec((tm, D_MODEL), lambda i: (i, 0)),
            pl.BlockSpec((D_MODEL, D_MODEL), lambda i: (0, 0)),
            pl.BlockSpec((1, D_MODEL), lambda i: (0, 0)),
            pl.BlockSpec((D_MODEL, D_MODEL), lambda i: (0, 0)),
            pl.BlockSpec((1, CROSS_HEAD_DIM), lambda i: (0, 0)),
        ],
        out_specs=[pl.BlockSpec((tm, D_MODEL), lambda i: (i, 0)),
                   pl.BlockSpec((tm, D_MODEL), lambda i: (i, 0))],
        out_shape=[jax.ShapeDtypeStruct((n, D_MODEL), F32), jax.ShapeDtypeStruct((n, D_MODEL), BF16)],
        compiler_params=pltpu.CompilerParams(
            dimension_semantics=("arbitrary",), vmem_limit_bytes=_vmem_limit(est)),
        name="out_q",
    )(x2d, mixed, w_out, g_cross, w_cq, q_g)


def _mem_kv_kernel(mem_ref, gm_ref, w_ref, kg_ref, k_ref, v_ref):
    mn = _rms(mem_ref[...], gm_ref[...]).astype(BF16)
    kv = jnp.dot(mn, w_ref[...], preferred_element_type=F32)
    for hh in range(N_HEADS):
        sl = slice(hh * CROSS_HEAD_DIM, (hh + 1) * CROSS_HEAD_DIM)
        k_ref[0, :, sl] = _rms(kv[:, sl], kg_ref[...]).astype(BF16)
    v_ref[0] = kv[:, D_MODEL:].astype(BF16)


def _mem_kv(mem2d, g_mem, w_ckv, k_g, *, batch):
    est = 2 * (N_MEM * D_MODEL * 4 + D_MODEL * 2 * D_MODEL * 2 + 2 * N_MEM * D_MODEL * 2) + 4 * N_MEM * 2 * D_MODEL * 4
    return pl.pallas_call(
        _mem_kv_kernel,
        grid=(batch,),
        in_specs=[
            pl.BlockSpec((N_MEM, D_MODEL), lambda b: (b, 0)),
            pl.BlockSpec((1, D_MODEL), lambda b: (0, 0)),
            pl.BlockSpec((D_MODEL, 2 * D_MODEL), lambda b: (0, 0)),
            pl.BlockSpec((1, CROSS_HEAD_DIM), lambda b: (0, 0)),
        ],
        out_specs=[pl.BlockSpec((1, N_MEM, D_MODEL), lambda b: (b, 0, 0)),
                   pl.BlockSpec((1, N_MEM, D_MODEL), lambda b: (b, 0, 0))],
        out_shape=[jax.ShapeDtypeStruct((batch, N_MEM, D_MODEL), BF16),
                   jax.ShapeDtypeStruct((batch, N_MEM, D_MODEL), BF16)],
        compiler_params=pltpu.CompilerParams(
            dimension_semantics=("arbitrary",), vmem_limit_bytes=_vmem_limit(est)),
        name="mem_kv",
    )(mem2d, g_mem, w_ckv, k_g)


def _route(logits_t):
    gl = logits_t[0:N_GROUPS]
    gmax = jnp.max(gl, axis=0, keepdims=True)
    eg = jnp.exp(gl - gmax)
    p_group = eg / jnp.sum(eg, axis=0, keepdims=True)
    p_g = jnp.max(p_group, axis=0, keepdims=True)
    g_idx = jnp.full_like(p_g, float(N_GROUPS))
    for g in reversed(range(N_GROUPS)):
        g_idx = jnp.where(p_group[g:g + 1] == p_g, float(g), g_idx)

    sel = []
    for j in range(EXPERTS_PER_GROUP):
        acc = jnp.zeros_like(p_g)
        for g in range(N_GROUPS):
            r = N_GROUPS + g * EXPERTS_PER_GROUP + j
            acc = jnp.where(g_idx == float(g), logits_t[r:r + 1], acc)
        sel.append(acc)
    smax = functools.reduce(jnp.maximum, sel)
    es = [jnp.exp(s - smax) for s in sel]
    den = functools.reduce(jnp.add, es)
    p_in = [e / den for e in es]

    v0 = functools.reduce(jnp.maximum, p_in)
    i0 = jnp.full_like(v0, float(EXPERTS_PER_GROUP))
    for j in reversed(range(EXPERTS_PER_GROUP)):
        i0 = jnp.where(p_in[j] == v0, float(j), i0)
    rest = [jnp.where(i0 == float(j), -1.0, p_in[j]) for j in range(EXPERTS_PER_GROUP)]
    v1 = functools.reduce(jnp.maximum, rest)
    i1 = jnp.full_like(v1, float(EXPERTS_PER_GROUP))
    for j in reversed(range(EXPERTS_PER_GROUP)):
        i1 = jnp.where(rest[j] == v1, float(j), i1)

    tot = v0 + v1
    w0 = p_g * (v0 / tot)
    w1 = p_g * (v1 / tot)
    first = i0 < i1
    a = jnp.where(first, i0, i1)
    b = jnp.where(first, i1, i0)
    w_lo = jnp.where(first, w0, w1)
    w_hi = jnp.where(first, w1, w0)
    pair = jnp.where(a == 0.0, b - 1.0, jnp.where(a == 1.0, jnp.where(b == 3.0, 3.0, 4.0), 5.0))
    swap = a == 2.0
    wa = jnp.where(swap, w_hi, w_lo)
    wb = jnp.where(swap, w_lo, w_hi)
    bucket = g_idx * float(len(PAIRS)) + pair
    return bucket, wa, wb


def _cross_router_kernel(q_ref, k_ref, v_ref, x1_ref, wco_ref, gf_ref, whi_ref, wlo_ref, rb_ref,
                         xa_ref, rt_ref, *, tm):
    q = q_ref[...]
    k = k_ref[0]
    v = v_ref[0]
    outs = []
    for hh in range(N_HEADS):
        sl = slice(hh * CROSS_HEAD_DIM, (hh + 1) * CROSS_HEAD_DIM)
        s = _nt_dot(q[:, sl], k[:, sl])
        p = jnp.exp(s - jnp.max(s, axis=-1, keepdims=True))
        l = jnp.sum(p, axis=-1, keepdims=True)
        o = jnp.dot(p.astype(BF16), v[:, sl], preferred_element_type=F32) / l
        outs.append(o.astype(BF16))
    o = jnp.concatenate(outs, axis=1)
    x2 = x1_ref[...] + jnp.dot(o, wco_ref[...], preferred_element_type=F32)
    xa_ref[:, :D_MODEL] = x2

    h3 = _rms(x2, gf_ref[...])
    hi = h3.astype(BF16)
    lo = (h3 - hi.astype(F32)).astype(BF16)
    lt = _nt_dot(whi_ref[...], hi) + (_nt_dot(whi_ref[...], lo) + _nt_dot(wlo_ref[...], hi))
    lt = lt + rb_ref[...]
    bucket, wa, wb = _route(lt)

    sub = lax.broadcasted_iota(I32, (8, tm), 0)
    rows = jnp.where(sub == 0, bucket, jnp.where(sub == 1, wa, jnp.where(sub == 2, wb, 0.0)))
    rt_ref[0] = rows
    sub_a = lax.broadcasted_iota(I32, (AUX_COLS, tm), 0)
    aux_t = jnp.where(sub_a == 0, bucket, jnp.where(sub_a == 1, wa, jnp.where(sub_a == 2, wb, 0.0)))
    xa_ref[:, D_MODEL:] = aux_t.T


def _cross_router(qc, kc, vc, x1, w_co, g_ffn, wr_hi, wr_lo, rb, *, batch, seq):
    tm = min(CROSS_ROWS, seq)
    ns = seq // tm
    n = batch * seq
    est = 2 * (tm * D_MODEL * 2 + 2 * N_MEM * D_MODEL * 2 + tm * D_MODEL * 4 + D_MODEL * D_MODEL * 2
               + tm * ROW_W * 4) + 10 * tm * D_MODEL * 4
    return pl.pallas_call(
        functools.partial(_cross_router_kernel, tm=tm),
        grid=(batch, ns),
        in_specs=[
            pl.BlockSpec((tm, D_MODEL), lambda b, s: (b * ns + s, 0)),
            pl.BlockSpec((1, N_MEM, D_MODEL), lambda b, s: (b, 0, 0)),
            pl.BlockSpec((1, N_MEM, D_MODEL), lambda b, s: (b, 0, 0)),
            pl.BlockSpec((tm, D_MODEL), lambda b, s: (b * ns + s, 0)),
            pl.BlockSpec((D_MODEL, D_MODEL), lambda b, s: (0, 0)),
            pl.BlockSpec((1, D_MODEL), lambda b, s: (0, 0)),
            pl.BlockSpec((32, D_MODEL), lambda b, s: (0, 0)),
            pl.BlockSpec((32, D_MODEL), lambda b, s: (0, 0)),
            pl.BlockSpec((32, 1), lambda b, s: (0, 0)),
        ],
        out_specs=[pl.BlockSpec((tm, ROW_W), lambda b, s: (b * ns + s, 0)),
                   pl.BlockSpec((1, 8, tm), lambda b, s: (b * ns + s, 0, 0))],
        out_shape=[jax.ShapeDtypeStruct((n, ROW_W), F32),
                   jax.ShapeDtypeStruct((n // tm, 8, tm), F32)],
        compiler_params=pltpu.CompilerParams(
            dimension_semantics=("arbitrary", "arbitrary"), vmem_limit_bytes=_vmem_limit(est)),
        name="cross_router",
    )(qc, kc, vc, x1, w_co, g_ffn, wr_hi, wr_lo, rb)


def _rank_kernel(rt_ref, pos_ref, cnt_ref, carry_sc, off_sc, *, tm, nt):
    i = pl.program_id(0)
    sub = lax.broadcasted_iota(I32, (32, tm), 0).astype(F32)

    @pl.when(i == 0)
    def _():
        n_hit = jnp.zeros((32, 1), F32)
        for j in range(nt):
            n_hit = n_hit + jnp.sum(jnp.where(sub == rt_ref[j, 0:1, :], 1.0, 0.0), axis=1, keepdims=True)
        counts = jnp.broadcast_to(n_hit, cnt_ref.shape)
        cnt_ref[...] = counts
        tiles = jnp.floor((counts + (MOE_ROWS - 1)) * (1.0 / MOE_ROWS))
        r = lax.broadcasted_iota(I32, (32, 32), 0)
        c = lax.broadcasted_iota(I32, (32, 32), 1)
        before = jnp.where(c < r, 1.0, 0.0).astype(BF16)
        off_sc[...] = jnp.dot(before, tiles.astype(BF16), preferred_element_type=F32) * MOE_ROWS
        carry_sc[...] = jnp.zeros_like(carry_sc)

    @pl.when(i >= 1)
    def _():
        hit = sub == rt_ref[i - 1, 0:1, :]
        r = lax.broadcasted_iota(I32, (tm, tm), 0)
        c = lax.broadcasted_iota(I32, (tm, tm), 1)
        upper = jnp.where(r <= c, 1.0, 0.0).astype(BF16)
        cum = jnp.dot(jnp.where(hit, 1.0, 0.0).astype(BF16), upper, preferred_element_type=F32)
        base = carry_sc[:, 0:1] + off_sc[:, 0:1]
        pos = jnp.sum(jnp.where(hit, cum - 1.0 + base, 0.0), axis=0, keepdims=True)
        pos_ref[0] = pos.astype(I32)
        carry_sc[...] = carry_sc[...] + jnp.broadcast_to(cum[:, tm - 1:tm], carry_sc.shape)


def _rank(rt):
    nt, _, tm = rt.shape
    return pl.pallas_call(
        functools.partial(_rank_kernel, tm=tm, nt=nt),
        grid=(nt + 1,),
        in_specs=[pl.BlockSpec((nt, 8, tm), lambda i: (0, 0, 0))],
        out_specs=[pl.BlockSpec((1, 1, tm), lambda i: (jnp.maximum(i - 1, 0), 0, 0)),
                   pl.BlockSpec((32, LANES), lambda i: (0, 0))],
        out_shape=[jax.ShapeDtypeStruct((nt, 1, tm), I32), jax.ShapeDtypeStruct((32, LANES), F32)],
        scratch_shapes=[pltpu.VMEM((32, LANES), F32), pltpu.VMEM((32, LANES), F32)],
        compiler_params=pltpu.CompilerParams(dimension_semantics=("arbitrary",)),
        name="rank",
    )(rt)


def _dispatch_kernel(pos_ref, zrow_ref, zflag_ref, nact_ref, xa_ref, xs_ref, zbuf, zsem, sem, *,
                     n_tiles, rows):
    @pl.when(pl.program_id(0) == 0)
    def _():
        _zero_partial_tiles(zrow_ref, zflag_ref, nact_ref, xs_ref, zbuf, zsem, n_tiles=n_tiles)

    base = pl.program_id(0) * rows
    for r in range(rows):
        pltpu.make_async_copy(xa_ref.at[pl.ds(r, 1)], xs_ref.at[pl.ds(pos_ref[base + r], 1)], sem).start(
            priority=r % 2)
    pltpu.make_async_copy(xa_ref, xs_ref.at[pl.ds(0, rows)], sem).wait()


def _zero_partial_tiles(zrow_ref, zflag_ref, nact_ref, xs_ref, zbuf, zsem, *, n_tiles):
    zbuf[...] = jnp.zeros_like(zbuf)

    def zero_copy(row):
        return pltpu.make_async_copy(zbuf, xs_ref.at[pl.ds(pl.multiple_of(row, MOE_ROWS), MOE_ROWS)], zsem)

    def bucket_tiles(op):
        def body(b, carry):
            @pl.when(zflag_ref[b] > 0)
            def _():
                op(zero_copy(zrow_ref[b]))
            return carry
        lax.fori_loop(0, N_PAIR_BUCKETS, body, 0)

    def unused_tiles(op):
        def body(i, carry):
            op(zero_copy(i * MOE_ROWS))
            return carry
        lax.fori_loop(nact_ref[0], n_tiles, body, 0)

    bucket_tiles(lambda cp: cp.start())
    unused_tiles(lambda cp: cp.start())
    bucket_tiles(lambda cp: cp.wait())
    unused_tiles(lambda cp: cp.wait())


def _dispatch(pos, zrow, zflag, n_active, xa, *, n_tiles):
    n_tok = xa.shape[0]
    n_sorted = n_tiles * MOE_ROWS
    rows = min(DMA_ROWS, n_tok)
    return pl.pallas_call(
        functools.partial(_dispatch_kernel, n_tiles=n_tiles, rows=rows),
        grid_spec=pltpu.PrefetchScalarGridSpec(
            num_scalar_prefetch=4,
            grid=(n_tok // rows,),
            in_specs=[pl.BlockSpec((rows, ROW_W), lambda i, *_: (i, 0))],
            out_specs=pl.BlockSpec(memory_space=pl.ANY),
            scratch_shapes=[pltpu.VMEM((MOE_ROWS, ROW_W), F32), pltpu.SemaphoreType.DMA(()),
                            pltpu.SemaphoreType.DMA(())],
        ),
        out_shape=jax.ShapeDtypeStruct((n_sorted, ROW_W), F32),
        compiler_params=pltpu.CompilerParams(
            dimension_semantics=("arbitrary",), has_side_effects=True,
            vmem_limit_bytes=_vmem_limit((2 * rows + MOE_ROWS) * ROW_W * 4)),
        name="dispatch",
    )(pos, zrow, zflag, n_active, xa)


def _moe_kernel(plan_ref, nact_ref, xs_ref, gf_ref, wg_hbm, wu_hbm, wd_hbm, ys_ref, wgu_sc, wd_sc, wsem):
    i = pl.program_id(0)

    def weight_copies(ea, eb, buf):
        cps = []
        for k, (src, e) in enumerate(((wg_hbm, ea), (wu_hbm, ea), (wg_hbm, eb), (wu_hbm, eb))):
            cps.append(pltpu.make_async_copy(src.at[e], wgu_sc.at[buf, k], wsem.at[buf]))
        for k, e in enumerate((ea, eb)):
            cps.append(pltpu.make_async_copy(wd_hbm.at[e], wd_sc.at[buf, k], wsem.at[buf]))
        return cps

    @pl.when(i < nact_ref[0])
    def _():
        buf = plan_ref[4, i]

        @pl.when(plan_ref[3, i] == 1)
        def _():
            @pl.when(i == 0)
            def _():
                for cp in weight_copies(plan_ref[1, i], plan_ref[2, i], buf):
                    cp.start()
            for cp in weight_copies(plan_ref[1, i], plan_ref[2, i], buf):
                cp.wait()

            @pl.when(plan_ref[7, i] == 1)
            def _():
                for cp in weight_copies(plan_ref[5, i], plan_ref[6, i], 1 - buf):
                    cp.start()

        x2 = xs_ref[:, :D_MODEL]
        aux = xs_ref[:, D_MODEL:]
        wa = aux[:, 1:2]
        wb = aux[:, 2:3]
        h = _rms(x2, gf_ref[...]).astype(BF16)

        def mlp(slot):
            g = jnp.dot(h, wgu_sc[buf, 2 * slot].astype(BF16), preferred_element_type=F32)
            u = jnp.dot(h, wgu_sc[buf, 2 * slot + 1].astype(BF16), preferred_element_type=F32)
            act = (g / (1.0 + jnp.exp(-g))) * u
            return jnp.dot(act.astype(BF16), wd_sc[buf, slot].astype(BF16), preferred_element_type=F32)

        ys_ref[...] = x2 + (wa * mlp(0) + wb * mlp(1))

    @pl.when(i >= nact_ref[0])
    def _():
        ys_ref[...] = jnp.zeros_like(ys_ref)


def _moe(plan, n_active, xs, g_ffn, w_gate, w_up, w_down):
    n_tiles = xs.shape[0] // MOE_ROWS
    t = MOE_ROWS
    est = (2 * (t * ROW_W * 4 + t * D_MODEL * 4) + 2 * 6 * D_MODEL * D_EXPERT * 4 + 4 * D_MODEL * D_EXPERT * 4
           + 10 * t * D_MODEL * 4)
    return pl.pallas_call(
        _moe_kernel,
        grid_spec=pltpu.PrefetchScalarGridSpec(
            num_scalar_prefetch=2,
            grid=(n_tiles,),
            in_specs=[
                pl.BlockSpec((t, ROW_W), lambda i, plan, na: (plan[0, i], 0)),
                pl.BlockSpec((1, D_MODEL), lambda i, plan, na: (0, 0)),
                pl.BlockSpec(memory_space=pl.ANY),
                pl.BlockSpec(memory_space=pl.ANY),
                pl.BlockSpec(memory_space=pl.ANY),
            ],
            out_specs=pl.BlockSpec((t, D_MODEL), lambda i, plan, na: (i, 0)),
            scratch_shapes=[pltpu.VMEM((2, 4, D_MODEL, D_EXPERT), F32), pltpu.VMEM((2, 2, D_EXPERT, D_MODEL), F32),
                            pltpu.SemaphoreType.DMA((2,))],
        ),
        out_shape=jax.ShapeDtypeStruct((n_tiles * t, D_MODEL), F32),
        compiler_params=pltpu.CompilerParams(
            dimension_semantics=("arbitrary",), vmem_limit_bytes=_vmem_limit(est)),
        name="moe",
    )(plan, n_active, xs, g_ffn, w_gate, w_up, w_down)


def _combine_kernel(pos_ref, ys_ref, out_ref, sem, *, rows):
    base = pl.program_id(0) * rows
    for r in range(rows):
        pltpu.make_async_copy(ys_ref.at[pl.ds(pos_ref[base + r], 1)], out_ref.at[pl.ds(r, 1)], sem).start(
            priority=r % 2)
    pltpu.make_async_copy(ys_ref.at[pl.ds(0, rows)], out_ref, sem).wait()


def _combine(pos, ys, *, n_tok):
    rows = min(DMA_ROWS, n_tok)
    return pl.pallas_call(
        functools.partial(_combine_kernel, rows=rows),
        grid_spec=pltpu.PrefetchScalarGridSpec(
            num_scalar_prefetch=1,
            grid=(n_tok // rows,),
            in_specs=[pl.BlockSpec(memory_space=pl.ANY)],
            out_specs=pl.BlockSpec((rows, D_MODEL), lambda i, *_: (i, 0)),
            scratch_shapes=[pltpu.SemaphoreType.DMA(())],
        ),
        out_shape=jax.ShapeDtypeStruct((n_tok, D_MODEL), F32),
        compiler_params=pltpu.CompilerParams(
            dimension_semantics=("arbitrary",), has_side_effects=True,
            vmem_limit_bytes=_vmem_limit(2 * rows * D_MODEL * 4)),
        name="combine",
    )(pos, ys)


def _t5_bucket(rel):
    nb = N_BUCKETS_T5 // 2
    max_exact = nb // 2
    ret = (rel > 0).astype(I32) * nb
    n = jnp.abs(rel)
    nf = jnp.maximum(n, 1).astype(F32)
    large = max_exact + (jnp.log(nf / max_exact) / math.log(MAX_DISTANCE / max_exact)
                         * (nb - max_exact)).astype(I32)
    large = jnp.minimum(large, nb - 1)
    return ret + jnp.where(n < max_exact, n, large)


def _diff_bias(rel_bias, t):
    assert t % CHUNK == 0 and t >= MAX_DISTANCE
    n_rel = 3 * t
    vec = rel_bias[_t5_bucket(jnp.arange(n_rel, dtype=I32) - (2 * t - 1))].astype(F32).T
    toe = jnp.tile(vec, (1, t))[:, :t * (n_rel - 1)].reshape(N_HEADS, t, n_rel - 1)
    qpos = jnp.arange(t, dtype=I32)[:, None]
    kpos = jnp.arange(t, dtype=I32)[None, :]
    far = rel_bias[_t5_bucket(jnp.asarray(-2 * t, I32))].astype(F32)[:, None, None]
    b0 = (toe[:, :, 2 * t - 1:3 * t - 1] - far) * LOG2E
    b0 = jnp.where((kpos // CHUNK <= qpos // CHUNK)[None], b0, NEG_INF)
    b1 = (toe[:, :, t - 1:2 * t - 1] - far) * LOG2E
    return jnp.swapaxes(jnp.stack([b0, b1], axis=1), -1, -2)


def _tile_plan(counts, n_tiles):
    t = MOE_ROWS
    tiles = (counts + t - 1) // t
    ends = jnp.cumsum(tiles)
    n_active = ends[-1]
    tile_idx = jnp.minimum(jnp.arange(n_tiles, dtype=I32), n_active - 1)
    tile_bucket = jnp.minimum(jnp.sum((ends[None, :] <= tile_idx[:, None]).astype(I32), axis=1), N_PAIR_BUCKETS - 1)
    pa = jnp.asarray([p[0] for p in PAIRS], I32)
    pb = jnp.asarray([p[1] for p in PAIRS], I32)
    grp = tile_bucket // len(PAIRS)
    pair = tile_bucket % len(PAIRS)
    tile_a = grp * EXPERTS_PER_GROUP + pa[pair]
    tile_b = grp * EXPERTS_PER_GROUP + pb[pair]
    zrow = ((ends - 1) * t).astype(I32)
    zflag = (counts > 0).astype(I32)
    prev_bucket = jnp.concatenate([jnp.full((1,), -1, I32), tile_bucket[:-1]])
    first = (tile_bucket != prev_bucket).astype(I32)
    buf = (jnp.cumsum(first) - 1) % 2
    nxt = ends[tile_bucket]
    has_next = (nxt < n_active).astype(I32)
    nxt = jnp.minimum(nxt, n_tiles - 1)
    plan = jnp.stack([tile_idx, tile_a, tile_b, first, buf, tile_a[nxt], tile_b[nxt], has_next]).astype(I32)
    return zrow, zflag, plan, n_active.astype(I32)


def kernel(x, mem, rel_bias, norm_mix_g, w_in, b_forget, diff_q_norm_g, diff_k_norm_g, diff_lambda_q1, diff_lambda_k1, diff_lambda_q2, diff_lambda_k2, diff_subln_g, fox_q_norm_g, fox_k_norm_g, fox_out_norm_g, w_out, norm_cross_g, norm_mem_g, w_cq, w_ckv, cross_q_norm_g, cross_k_norm_g, w_co, norm_ffn_g, w_group_router, b_group_router, w_expert_router, b_expert_router, w_exp_gate, w_exp_up, w_exp_down):
    batch, seq, d = x.shape
    assert d == D_MODEL and norm_mix_g.shape[0] == 1 and mem.shape[1] == N_MEM
    n_tok = batch * seq
    assert seq % ATTN_TILE == 0 or seq < ATTN_TILE
    l = 0

    w_qk = jnp.concatenate([w_in[l, :, :1024], w_in[l, :, 1536:2560]], axis=1).astype(BF16)
    w_vt = jnp.concatenate([w_in[l, :, 1024:1536], w_in[l, :, 2560:3072]], axis=1).astype(BF16)
    wft = jnp.zeros((8, D_MODEL), F32).at[:N_HEADS].set(w_in[l, :, 3072:].T).astype(BF16)
    bf = jnp.zeros((8, 1), F32).at[:N_HEADS, 0].set(b_forget[l].astype(F32))
    ones = jnp.ones((HEAD_DIM,), F32)
    qg = jnp.stack([
        jnp.tile(diff_q_norm_g[l].astype(F32), 2) * (DIFF_QK_DIM ** -0.5 * LOG2E),
        jnp.tile(diff_k_norm_g[l].astype(F32), 2),
        fox_q_norm_g[l].astype(F32) * (HEAD_DIM ** -0.5 * LOG2E), fox_k_norm_g[l].astype(F32),
        ones, ones, ones, ones])
    lam = (jnp.exp(jnp.sum(diff_lambda_q1[l].astype(F32) * diff_lambda_k1[l].astype(F32)))
           - jnp.exp(jnp.sum(diff_lambda_q2[l].astype(F32) * diff_lambda_k2[l].astype(F32)))
           + LAM_INIT).reshape(1)
    t_attn = min(ATTN_TILE, seq)
    bias_tiles_t = _diff_bias(rel_bias, t_attn)
    wr = jnp.zeros((32, D_MODEL), F32)
    wr = wr.at[:N_GROUPS].set(w_group_router[l].T).at[N_GROUPS:N_GROUPS + N_EXPERTS].set(w_expert_router[l].T)
    wr_hi = wr.astype(BF16)
    wr_lo = (wr - wr_hi.astype(F32)).astype(BF16)
    rb = jnp.zeros((32, 1), F32)
    rb = rb.at[:N_GROUPS, 0].set(b_group_router[l]).at[N_GROUPS:N_GROUPS + N_EXPERTS, 0].set(b_expert_router[l])

    row = lambda v: v.astype(F32).reshape(1, -1)
    x2d = x.reshape(n_tok, D_MODEL)

    qk, vt, cum = _mix_proj(x2d, row(norm_mix_g[l]), w_qk, w_vt, wft, bf, qg, batch=batch, seq=seq)
    cum4 = cum[:, :N_HEADS].reshape(batch, N_HEADS, 1, seq)
    gain_t = lambda v: jnp.broadcast_to(v.astype(F32)[:, None], (HEAD_DIM, t_attn))
    mixed = _attn(qk, vt, bias_tiles_t, lam, cum4, gain_t(diff_subln_g[l]), gain_t(fox_out_norm_g[l]),
                  batch=batch, seq=seq)
    x1, qc = _out_q(x2d, mixed, w_out[l].astype(BF16), row(norm_cross_g[l]),
                    w_cq[l].astype(BF16), row(cross_q_norm_g[l]))

    kc, vc = _mem_kv(mem.reshape(batch * N_MEM, D_MODEL), row(norm_mem_g[l]), w_ckv[l].astype(BF16),
                     row(cross_k_norm_g[l]), batch=batch)
    xa, rt = _cross_router(qc, kc, vc, x1, w_co[l].astype(BF16), row(norm_ffn_g[l]), wr_hi, wr_lo, rb,
                           batch=batch, seq=seq)

    pos, cnt = _rank(rt)
    pos = pos.reshape(n_tok)
    counts = cnt[:N_PAIR_BUCKETS, 0].astype(I32)
    n_tiles = n_tok // MOE_ROWS + N_PAIR_BUCKETS
    zrow, zflag, plan, n_active = _tile_plan(counts, n_tiles)
    n_active = n_active.reshape(1)
    xs = _dispatch(pos, zrow, zflag, n_active, xa, n_tiles=n_tiles)
    ys = _moe(plan, n_active, xs, row(norm_ffn_g[l]),
              w_exp_gate[l].astype(F32), w_exp_up[l].astype(F32), w_exp_down[l].astype(F32))
    out = _combine(pos, ys, n_tok=n_tok)
    return out.reshape(batch, seq, D_MODEL)
```

```python
import functools
import math

import jax
import jax.numpy as jnp
import numpy as np
from jax import lax
from jax.experimental import pallas as pl
from jax.experimental.pallas import tpu as pltpu

F32 = jnp.float32
BF16 = jnp.bfloat16
I32 = jnp.int32

D_MODEL = 1024
CHUNK = 64
N_MEM = 256
N_HEADS = 4
HEAD_DIM = 128
VT_ROWS = HEAD_DIM + 16
DIFF_QK_DIM = 64
CROSS_HEAD_DIM = 256
N_BUCKETS_T5 = 32
MAX_DISTANCE = 128
N_GROUPS = 4
EXPERTS_PER_GROUP = 4
N_EXPERTS = N_GROUPS * EXPERTS_PER_GROUP
D_EXPERT = 512
EPS = 1e-6
NEG_INF = -1e30
LAM_INIT = 0.8 - 0.6 * math.exp(-0.3 * 0)
LOG2E = math.log2(math.e)

PAIRS = ((0, 1), (0, 2), (0, 3), (1, 3), (1, 2), (3, 2))
N_PAIR_BUCKETS = N_GROUPS * len(PAIRS)

LANES = 128
VMEM_LIMIT_CAP = 56 * 1024 * 1024

PROJ_ROWS = 1024
ATTN_TILE = 256
CROSS_ROWS = 512
MOE_ROWS = 256
AUX_COLS = LANES
ROW_W = D_MODEL + AUX_COLS
DMA_ROWS = 2048


def _vmem_limit(nbytes):
    return int(min(max(nbytes * 5 // 4, 32 * 1024 * 1024), VMEM_LIMIT_CAP))


def _nt_dot(a, b):
    return lax.dot_general(a, b, (((1,), (1,)), ((), ())), preferred_element_type=F32)


def _rms(x, g):
    ms = jnp.mean(x * x, axis=-1, keepdims=True)
    return x * lax.rsqrt(ms + EPS) * g


def _mix_proj_kernel(x_ref, g_ref, w_ref, wv_ref, wft_ref, bf_ref, qg_ref, o_ref, vt_ref, cum_ref, carry_ref,
                     v_sc, *, tm):
    si = pl.program_id(1)
    h = _rms(x_ref[...], g_ref[...]).astype(BF16)
    lane = lax.broadcasted_iota(I32, (tm, HEAD_DIM), 1)
    lo = lane < DIFF_QK_DIM
    for kind in range(4):
        p = jnp.dot(h, w_ref[:, kind * 512:(kind + 1) * 512], preferred_element_type=F32)
        for hh in range(N_HEADS):
            ph = p[:, hh * HEAD_DIM:(hh + 1) * HEAD_DIM]
            if kind in (0, 1):
                sq = ph * ph
                s_lo = jnp.sum(jnp.where(lo, sq, 0.0), axis=-1, keepdims=True)
                s_hi = jnp.sum(jnp.where(lo, 0.0, sq), axis=-1, keepdims=True)
                ms = jnp.where(lo, s_lo, s_hi) * (1.0 / DIFF_QK_DIM)
                ph = ph * lax.rsqrt(ms + EPS) * qg_ref[kind:kind + 1, :]
            else:
                ph = _rms(ph, qg_ref[kind:kind + 1, :])
            o_ref[0, kind * N_HEADS + hh] = ph.astype(BF16)
    for kind in range(2):
        v_sc[...] = jnp.dot(h, wv_ref[:, kind * 512:(kind + 1) * 512], preferred_element_type=F32)
        pt = v_sc[...].T
        for hh in range(N_HEADS):
            vt_ref[0, kind * N_HEADS + hh, :HEAD_DIM, :] = pt[hh * HEAD_DIM:(hh + 1) * HEAD_DIM, :].astype(BF16)
            vt_ref[0, kind * N_HEADS + hh, HEAD_DIM:, :] = jnp.ones((VT_ROWS - HEAD_DIM, tm), BF16)

    v_sc[:, :LANES] = jnp.dot(h, wft_ref[...], preferred_element_type=F32)
    z = v_sc[:, :LANES].T[:8, :] + bf_ref[...]
    logf = jnp.minimum(z, 0.0) - jnp.log(1.0 + jnp.exp(-jnp.abs(z)))
    lane8 = lax.broadcasted_iota(I32, (8, tm), 1)
    c = logf * LOG2E
    k = 1
    while k < tm:
        c = c + jnp.where(lane8 >= k, pltpu.roll(c, k, axis=1), 0.0)
        k *= 2

    @pl.when(si == 0)
    def _():
        carry_ref[...] = jnp.zeros_like(carry_ref)

    c = c + carry_ref[:, 0:1]
    cum_ref[0] = c
    carry_ref[...] = jnp.broadcast_to(c[:, tm - 1:tm], carry_ref.shape)


def _mix_proj(x2d, g, w_qk, w_vt, wft, bf, qg, *, batch, seq):
    tm = min(PROJ_ROWS, seq)
    ns = seq // tm
    est = 2 * (tm * D_MODEL * 4 + D_MODEL * 3072 * 2 + 24 * tm * HEAD_DIM * 2) + 8 * tm * D_MODEL
    return pl.pallas_call(
        functools.partial(_mix_proj_kernel, tm=tm),
        grid=(batch, ns),
        in_specs=[
            pl.BlockSpec((tm, D_MODEL), lambda b, s: (b * ns + s, 0)),
            pl.BlockSpec((1, D_MODEL), lambda b, s: (0, 0)),
            pl.BlockSpec((D_MODEL, 2048), lambda b, s: (0, 0)),
            pl.BlockSpec((D_MODEL, 1024), lambda b, s: (0, 0)),
            pl.BlockSpec((D_MODEL, LANES), lambda b, s: (0, 0)),
            pl.BlockSpec((8, 1), lambda b, s: (0, 0)),
            pl.BlockSpec((8, HEAD_DIM), lambda b, s: (0, 0)),
        ],
        out_specs=[
            pl.BlockSpec((1, 16, tm, HEAD_DIM), lambda b, s: (b, 0, s, 0)),
            pl.BlockSpec((1, 8, VT_ROWS, tm), lambda b, s: (b, 0, 0, s)),
            pl.BlockSpec((1, 8, tm), lambda b, s: (b, 0, s)),
        ],
        out_shape=[
            jax.ShapeDtypeStruct((batch, 16, seq, HEAD_DIM), BF16),
            jax.ShapeDtypeStruct((batch, 8, VT_ROWS, seq), BF16),
            jax.ShapeDtypeStruct((batch, 8, seq), F32),
        ],
        scratch_shapes=[pltpu.VMEM((8, LANES), F32), pltpu.VMEM((tm, 512), F32)],
        compiler_params=pltpu.CompilerParams(
            dimension_semantics=("arbitrary", "arbitrary"), vmem_limit_bytes=_vmem_limit(est)),
        name="mix_proj",
    )(x2d, g, w_qk, w_vt, wft, bf, qg)


def _init_softmax(m_sc, acc_sc):
    m_sc[...] = jnp.full(m_sc.shape, NEG_INF, F32)
    acc_sc[...] = jnp.zeros_like(acc_sc)


N_CHAINS = 3 * N_HEADS


def _attn_kernel(lam_ref, dq_ref, dk_ref, dvt_ref, fq_ref, fk_ref, fvt_ref, bias_ref, c_ref,
                 gd_ref, gf_ref, o_ref, m_sc, acc_sc, qm_sc, ccol_sc, s_sc, mt_sc, s2_sc, mt2_sc, *, t, seq):
    qi = pl.program_id(1)
    _init_softmax(m_sc, acc_sc)
    key = lax.broadcasted_iota(I32, (t, t), 0)
    qry = lax.broadcasted_iota(I32, (t, t), 1)

    lane = lax.broadcasted_iota(I32, (t, HEAD_DIM), 1)
    for hh in range(N_HEADS):
        q = dq_ref[0, hh]
        zero = jnp.zeros_like(q)
        qm_sc[2 * hh] = jnp.where(lane < DIFF_QK_DIM, q, zero)
        qm_sc[2 * hh + 1] = jnp.where(lane < DIFF_QK_DIM, zero, q)

    @pl.when(qi == 0)
    def _():
        for hh in range(N_HEADS):
            for j in range(seq // t):
                row = c_ref[0, hh, :, j * t:(j + 1) * t]
                ccol_sc[hh, j * t:(j + 1) * t, :] = jnp.sum(
                    jnp.where(key == qry, jnp.broadcast_to(row, (t, t)), 0.0), axis=1, keepdims=True)

    q_off = pl.multiple_of(qi * t, t)

    bufs = ((s_sc, mt_sc), (s2_sc, mt2_sc))

    def logits(j, where, par, slot):
        off = pl.multiple_of(j * t, t)
        s_buf, mt_buf = bufs[par]
        if slot < 2 * N_HEADS:
            hh = slot // 2
            s_t = _nt_dot(dk_ref[0, hh, pl.ds(off, t), :], qm_sc[slot])
            if where != "far":
                s_t = s_t + bias_ref[hh, 1 if where == "near" else 0]
            s_buf[slot] = s_t
            mt_buf[slot] = jnp.max(s_t, axis=0, keepdims=True)
        else:
            hh = slot - 2 * N_HEADS
            c_k = ccol_sc[hh, pl.ds(off, t), :]
            s_t = _nt_dot(fk_ref[0, hh, pl.ds(off, t), :], fq_ref[0, hh]) - c_k
            if where == "diag":
                s_t = jnp.where(key <= qry, s_t, NEG_INF)
            s_buf[slot] = s_t
            mt_buf[slot] = jnp.max(s_t, axis=0, keepdims=True) + c_ref[0, hh, :, pl.ds(q_off, t)]

    def update(j, par, slot):
        off = pl.multiple_of(j * t, t)
        s_buf, mt_buf = bufs[par]
        vt_ref = dvt_ref if slot < 2 * N_HEADS else fvt_ref
        hh = slot // 2 if slot < 2 * N_HEADS else slot - 2 * N_HEADS
        v_t = vt_ref[0, hh, :, pl.ds(off, t)]
        m_prev = m_sc[slot]
        m_new = jnp.maximum(m_prev, mt_buf[slot])
        alpha = jnp.exp2(m_prev - m_new)
        if slot < 2 * N_HEADS:
            p_t = jnp.exp2(s_buf[slot] - m_new)
        else:
            p_t = jnp.exp2(s_buf[slot] - (m_new - c_ref[0, hh, :, pl.ds(q_off, t)]))
        acc_sc[slot] = alpha * acc_sc[slot] + jnp.dot(v_t, p_t.astype(BF16), preferred_element_type=F32)
        m_sc[slot] = m_new

    def phase(upd=None, nxt=None):
        for slot in range(N_CHAINS):
            if nxt is not None:
                logits(*nxt, slot)
            if upd is not None:
                update(*upd, slot)

    n_far = jnp.maximum(qi - 1, 0)
    far_tile = lambda k: jnp.maximum(qi - 2 - k, 0)

    phase(nxt=(qi, "diag", 0))

    @pl.when(qi == 0)
    def _():
        phase(upd=(qi, 0))

    @pl.when(qi == 1)
    def _():
        phase(upd=(qi, 0), nxt=(qi - 1, "near", 1))
        phase(upd=(qi - 1, 1))

    @pl.when(qi >= 2)
    def _():
        phase(upd=(qi, 0), nxt=(qi - 1, "near", 1))
        phase(upd=(qi - 1, 1), nxt=(far_tile(0), "far", 0))

        def far_pair(i, carry):
            k = 2 * i
            phase(upd=(far_tile(k), 0), nxt=(far_tile(k + 1), "far", 1))

            @pl.when(k + 1 < n_far)
            def _():
                phase(upd=(far_tile(k + 1), 1), nxt=(far_tile(k + 2), "far", 0))
            return carry

        lax.fori_loop(0, (n_far + 1) // 2, far_pair, 0)

    def normalised(slot):
        return acc_sc[slot, :HEAD_DIM, :] / acc_sc[slot, HEAD_DIM:HEAD_DIM + 1, :]

    def rms_t(o_t, g_t):
        ms = jnp.mean(o_t * o_t, axis=0, keepdims=True)
        return o_t * lax.rsqrt(ms + EPS) * g_t

    for hh in range(N_HEADS):
        o_t = normalised(2 * hh) - lam_ref[0] * normalised(2 * hh + 1)
        o_ref[:, hh * HEAD_DIM:(hh + 1) * HEAD_DIM] = (
            rms_t(o_t, gd_ref[...]) * (1.0 - LAM_INIT)).T.astype(BF16)
    for hh in range(N_HEADS):
        o_t = normalised(2 * N_HEADS + hh)
        o_ref[:, (N_HEADS + hh) * HEAD_DIM:(N_HEADS + hh + 1) * HEAD_DIM] = rms_t(o_t, gf_ref[...]).T.astype(BF16)


def _attn(qk, vt, bias_tiles_t, lam, cum4, subln_g, fox_g, *, batch, seq):
    t = min(ATTN_TILE, seq)
    nq = seq // t
    head_blk = N_HEADS * seq * HEAD_DIM * 2
    est = (2 * (2 * N_HEADS * t * HEAD_DIM * 2 + 4 * head_blk + N_HEADS * 2 * t * t * 4 + N_HEADS * 8 * seq * 4
                + t * 2 * N_HEADS * HEAD_DIM * 2)
           + N_CHAINS * (HEAD_DIM + 16) * t * 4 + 2 * N_HEADS * t * HEAD_DIM * 2 + N_HEADS * seq * LANES * 4
           + (2 * N_CHAINS + 36) * t * t * 4)
    qspec = lambda blk: pl.BlockSpec((1, N_HEADS, t, HEAD_DIM), lambda b, i: (b, blk, i, 0))
    kspec = lambda blk: pl.BlockSpec((1, N_HEADS, seq, HEAD_DIM), lambda b, i: (b, blk, 0, 0))
    vspec = lambda blk: pl.BlockSpec((1, N_HEADS, VT_ROWS, seq), lambda b, i: (b, blk, 0, 0))
    return pl.pallas_call(
        functools.partial(_attn_kernel, t=t, seq=seq),
        grid=(batch, nq),
        in_specs=[
            pl.BlockSpec(memory_space=pltpu.SMEM),
            qspec(0), kspec(1), vspec(0),
            qspec(2), kspec(3), vspec(1),
            pl.BlockSpec((N_HEADS, 2, t, t), lambda b, i: (0, 0, 0, 0)),
            pl.BlockSpec((1, N_HEADS, 1, seq), lambda b, i: (b, 0, 0, 0)),
            pl.BlockSpec((HEAD_DIM, t), lambda b, i: (0, 0)),
            pl.BlockSpec((HEAD_DIM, t), lambda b, i: (0, 0)),
        ],
        out_specs=pl.BlockSpec((t, 2 * N_HEADS * HEAD_DIM), lambda b, i: (b * nq + i, 0)),
        out_shape=jax.ShapeDtypeStruct((batch * seq, 2 * N_HEADS * HEAD_DIM), BF16),
        scratch_shapes=[pltpu.VMEM((N_CHAINS, 1, t), F32),
                        pltpu.VMEM((N_CHAINS, VT_ROWS, t), F32),
                        pltpu.VMEM((2 * N_HEADS, t, HEAD_DIM), BF16),
                        pltpu.VMEM((N_HEADS, seq, 1), F32),
                        pltpu.VMEM((N_CHAINS, t, t), F32), pltpu.VMEM((N_CHAINS, 1, t), F32),
                        pltpu.VMEM((N_CHAINS, t, t), F32), pltpu.VMEM((N_CHAINS, 1, t), F32)],
        compiler_params=pltpu.CompilerParams(
            dimension_semantics=("arbitrary", "arbitrary"), vmem_limit_bytes=_vmem_limit(est)),
        name="attn",
    )(lam, qk, qk, vt, qk, qk, vt, bias_tiles_t, cum4, subln_g, fox_g)


def _out_q_kernel(x_ref, m_ref, wo_ref, gc_ref, wq_ref, qg_ref, x1_ref, qc_ref):
    x1 = x_ref[...] + jnp.dot(m_ref[...], wo_ref[...], preferred_element_type=F32)
    x1_ref[...] = x1
    hc = _rms(x1, gc_ref[...]).astype(BF16)
    q = jnp.dot(hc, wq_ref[...], preferred_element_type=F32)
    for hh in range(N_HEADS):
        sl = slice(hh * CROSS_HEAD_DIM, (hh + 1) * CROSS_HEAD_DIM)
        qh = _rms(q[:, sl], qg_ref[...]) * (CROSS_HEAD_DIM ** -0.5)
        qc_ref[:, sl] = qh.astype(BF16)


def _out_q(x2d, mixed, w_out, g_cross, w_cq, q_g):
    n = x2d.shape[0]
    tm = min(PROJ_ROWS, n)
    est = 2 * (tm * D_MODEL * 4 * 2 + tm * D_MODEL * 2 + 2 * D_MODEL * D_MODEL * 2 + tm * D_MODEL * 2) + 6 * tm * D_MODEL * 4
    return pl.pallas_call(
        _out_q_kernel,
        grid=(n // tm,),
        in_specs=[
            pl.BlockSpec((tm, D_MODEL), lambda i: (i, 0)),
            pl.BlockSpec((tm, D_MODEL), lambda i: (i, 0)),
            pl.BlockSpec((D_MODEL, D_MODEL), lambda i: (0, 0)),
            pl.BlockSpec((1, D_MODEL), lambda i: (0, 0)),
            pl.BlockSpec((D_MODEL, D_MODEL), lambda i: (0, 0)),
            pl.BlockSpec((1, CROSS_HEAD_DIM), lambda i: (0, 0)),
        ],
        out_specs=[pl.BlockSpec((tm, D_MODEL), lambda i: (i, 0)),
                   pl.BlockSpec((tm, D_MODEL), lambda i: (i, 0))],
        out_shape=[jax.ShapeDtypeStruct((n, D_MODEL), F32), jax.ShapeDtypeStruct((n, D_MODEL), BF16)],
        compiler_params=pltpu.CompilerParams(
            dimension_semantics=("arbitrary",), vmem_limit_bytes=_vmem_limit(est)),
        name="out_q",
    )(x2d, mixed, w_out, g_cross, w_cq, q_g)


def _mem_kv_kernel(mem_ref, gm_ref, w_ref, kg_ref, k_ref, v_ref):
    mn = _rms(mem_ref[...], gm_ref[...]).astype(BF16)
    kv = jnp.dot(mn, w_ref[...], preferred_element_type=F32)
    for hh in range(N_HEADS):
        sl = slice(hh * CROSS_HEAD_DIM, (hh + 1) * CROSS_HEAD_DIM)
        k_ref[0, :, sl] = _rms(kv[:, sl], kg_ref[...]).astype(BF16)
    v_ref[0] = kv[:, D_MODEL:].astype(BF16)


def _mem_kv(mem2d, g_mem, w_ckv, k_g, *, batch):
    est = 2 * (N_MEM * D_MODEL * 4 + D_MODEL * 2 * D_MODEL * 2 + 2 * N_MEM * D_MODEL * 2) + 4 * N_MEM * 2 * D_MODEL * 4
    return pl.pallas_call(
        _mem_kv_kernel,
        grid=(batch,),
        in_specs=[
            pl.BlockSpec((N_MEM, D_MODEL), lambda b: (b, 0)),
            pl.BlockSpec((1, D_MODEL), lambda b: (0, 0)),
            pl.BlockSpec((D_MODEL, 2 * D_MODEL), lambda b: (0, 0)),
            pl.BlockSpec((1, CROSS_HEAD_DIM), lambda b: (0, 0)),
        ],
        out_specs=[pl.BlockSpec((1, N_MEM, D_MODEL), lambda b: (b, 0, 0)),
                   pl.BlockSpec((1, N_MEM, D_MODEL), lambda b: (b, 0, 0))],
        out_shape=[jax.ShapeDtypeStruct((batch, N_MEM, D_MODEL), BF16),
                   jax.ShapeDtypeStruct((batch, N_MEM, D_MODEL), BF16)],
        compiler_params=pltpu.CompilerParams(
            dimension_semantics=("arbitrary",), vmem_limit_bytes=_vmem_limit(est)),
        name="mem_kv",
    )(mem2d, g_mem, w_ckv, k_g)


def _route(logits_t):
    gl = logits_t[0:N_GROUPS]
    gmax = jnp.max(gl, axis=0, keepdims=True)
    eg = jnp.exp(gl - gmax)
    p_group = eg / jnp.sum(eg, axis=0, keepdims=True)
    p_g = jnp.max(p_group, axis=0, keepdims=True)
    g_idx = jnp.full_like(p_g, float(N_GROUPS))
    for g in reversed(range(N_GROUPS)):
        g_idx = jnp.where(p_group[g:g + 1] == p_g, float(g), g_idx)

    sel = []
    for j in range(EXPERTS_PER_GROUP):
        acc = jnp.zeros_like(p_g)
        for g in range(N_GROUPS):
            r = N_GROUPS + g * EXPERTS_PER_GROUP + j
            acc = jnp.where(g_idx == float(g), logits_t[r:r + 1], acc)
        sel.append(acc)
    smax = functools.reduce(jnp.maximum, sel)
    es = [jnp.exp(s - smax) for s in sel]
    den = functools.reduce(jnp.add, es)
    p_in = [e / den for e in es]

    v0 = functools.reduce(jnp.maximum, p_in)
    i0 = jnp.full_like(v0, float(EXPERTS_PER_GROUP))
    for j in reversed(range(EXPERTS_PER_GROUP)):
        i0 = jnp.where(p_in[j] == v0, float(j), i0)
    rest = [jnp.where(i0 == float(j), -1.0, p_in[j]) for j in range(EXPERTS_PER_GROUP)]
    v1 = functools.reduce(jnp.maximum, rest)
    i1 = jnp.full_like(v1, float(EXPERTS_PER_GROUP))
    for j in reversed(range(EXPERTS_PER_GROUP)):
        i1 = jnp.where(rest[j] == v1, float(j), i1)

    tot = v0 + v1
    w0 = p_g * (v0 / tot)
    w1 = p_g * (v1 / tot)
    first = i0 < i1
    a = jnp.where(first, i0, i1)
    b = jnp.where(first, i1, i0)
    w_lo = jnp.where(first, w0, w1)
    w_hi = jnp.where(first, w1, w0)
    pair = jnp.where(a == 0.0, b - 1.0, jnp.where(a == 1.0, jnp.where(b == 3.0, 3.0, 4.0), 5.0))
    swap = a == 2.0
    wa = jnp.where(swap, w_hi, w_lo)
    wb = jnp.where(swap, w_lo, w_hi)
    bucket = g_idx * float(len(PAIRS)) + pair
    return bucket, wa, wb


def _cross_router_kernel(q_ref, k_ref, v_ref, x1_ref, wco_ref, gf_ref, whi_ref, wlo_ref, rb_ref,
                         xa_ref, rt_ref, *, tm):
    q = q_ref[...]
    k = k_ref[0]
    v = v_ref[0]
    outs = []
    for hh in range(N_HEADS):
        sl = slice(hh * CROSS_HEAD_DIM, (hh + 1) * CROSS_HEAD_DIM)
        s = _nt_dot(q[:, sl], k[:, sl])
        p = jnp.exp(s - jnp.max(s, axis=-1, keepdims=True))
        l = jnp.sum(p, axis=-1, keepdims=True)
        o = jnp.dot(p.astype(BF16), v[:, sl], preferred_element_type=F32) / l
        outs.append(o.astype(BF16))
    o = jnp.concatenate(outs, axis=1)
    x2 = x1_ref[...] + jnp.dot(o, wco_ref[...], preferred_element_type=F32)
    xa_ref[:, :D_MODEL] = x2

    h3 = _rms(x2, gf_ref[...])
    hi = h3.astype(BF16)
    lo = (h3 - hi.astype(F32)).astype(BF16)
    lt = _nt_dot(whi_ref[...], hi) + (_nt_dot(whi_ref[...], lo) + _nt_dot(wlo_ref[...], hi))
    lt = lt + rb_ref[...]
    bucket, wa, wb = _route(lt)

    sub = lax.broadcasted_iota(I32, (8, tm), 0)
    rows = jnp.where(sub == 0, bucket, jnp.where(sub == 1, wa, jnp.where(sub == 2, wb, 0.0)))
    rt_ref[0] = rows
    sub_a = lax.broadcasted_iota(I32, (AUX_COLS, tm), 0)
    aux_t = jnp.where(sub_a == 0, bucket, jnp.where(sub_a == 1, wa, jnp.where(sub_a == 2, wb, 0.0)))
    xa_ref[:, D_MODEL:] = aux_t.T


def _cross_router(qc, kc, vc, x1, w_co, g_ffn, wr_hi, wr_lo, rb, *, batch, seq):
    tm = min(CROSS_ROWS, seq)
    ns = seq // tm
    n = batch * seq
    est = 2 * (tm * D_MODEL * 2 + 2 * N_MEM * D_MODEL * 2 + tm * D_MODEL * 4 + D_MODEL * D_MODEL * 2
               + tm * ROW_W * 4) + 10 * tm * D_MODEL * 4
    return pl.pallas_call(
        functools.partial(_cross_router_kernel, tm=tm),
        grid=(batch, ns),
        in_specs=[
            pl.BlockSpec((tm, D_MODEL), lambda b, s: (b * ns + s, 0)),
            pl.BlockSpec((1, N_MEM, D_MODEL), lambda b, s: (b, 0, 0)),
            pl.BlockSpec((1, N_MEM, D_MODEL), lambda b, s: (b, 0, 0)),
            pl.BlockSpec((tm, D_MODEL), lambda b, s: (b * ns + s, 0)),
            pl.BlockSpec((D_MODEL, D_MODEL), lambda b, s: (0, 0)),
            pl.BlockSpec((1, D_MODEL), lambda b, s: (0, 0)),
            pl.BlockSpec((32, D_MODEL), lambda b, s: (0, 0)),
            pl.BlockSpec((32, D_MODEL), lambda b, s: (0, 0)),
            pl.BlockSpec((32, 1), lambda b, s: (0, 0)),
        ],
        out_specs=[pl.BlockSpec((tm, ROW_W), lambda b, s: (b * ns + s, 0)),
                   pl.BlockSpec((1, 8, tm), lambda b, s: (b * ns + s, 0, 0))],
        out_shape=[jax.ShapeDtypeStruct((n, ROW_W), F32),
                   jax.ShapeDtypeStruct((n // tm, 8, tm), F32)],
        compiler_params=pltpu.CompilerParams(
            dimension_semantics=("arbitrary", "arbitrary"), vmem_limit_bytes=_vmem_limit(est)),
        name="cross_router",
    )(qc, kc, vc, x1, w_co, g_ffn, wr_hi, wr_lo, rb)


def _rank_kernel(rt_ref, pos_ref, cnt_ref, carry_sc, off_sc, *, tm, nt):
    i = pl.program_id(0)
    sub = lax.broadcasted_iota(I32, (32, tm), 0).astype(F32)

    @pl.when(i == 0)
    def _():
        n_hit = jnp.zeros((32, 1), F32)
        for j in range(nt):
            n_hit = n_hit + jnp.sum(jnp.where(sub == rt_ref[j, 0:1, :], 1.0, 0.0), axis=1, keepdims=True)
        counts = jnp.broadcast_to(n_hit, cnt_ref.shape)
        cnt_ref[...] = counts
        tiles = jnp.floor((counts + (MOE_ROWS - 1)) * (1.0 / MOE_ROWS))
        r = lax.broadcasted_iota(I32, (32, 32), 0)
        c = lax.broadcasted_iota(I32, (32, 32), 1)
        before = jnp.where(c < r, 1.0, 0.0).astype(BF16)
        off_sc[...] = jnp.dot(before, tiles.astype(BF16), preferred_element_type=F32) * MOE_ROWS
        carry_sc[...] = jnp.zeros_like(carry_sc)

    @pl.when(i >= 1)
    def _():
        hit = sub == rt_ref[i - 1, 0:1, :]
        r = lax.broadcasted_iota(I32, (tm, tm), 0)
        c = lax.broadcasted_iota(I32, (tm, tm), 1)
        upper = jnp.where(r <= c, 1.0, 0.0).astype(BF16)
        cum = jnp.dot(jnp.where(hit, 1.0, 0.0).astype(BF16), upper, preferred_element_type=F32)
        base = carry_sc[:, 0:1] + off_sc[:, 0:1]
        pos = jnp.sum(jnp.where(hit, cum - 1.0 + base, 0.0), axis=0, keepdims=True)
        pos_ref[0] = pos.astype(I32)
        carry_sc[...] = carry_sc[...] + jnp.broadcast_to(cum[:, tm - 1:tm], carry_sc.shape)


def _rank(rt):
    nt, _, tm = rt.shape
    return pl.pallas_call(
        functools.partial(_rank_kernel, tm=tm, nt=nt),
        grid=(nt + 1,),
        in_specs=[pl.BlockSpec((nt, 8, tm), lambda i: (0, 0, 0))],
        out_specs=[pl.BlockSpec((1, 1, tm), lambda i: (jnp.maximum(i - 1, 0), 0, 0)),
                   pl.BlockSpec((32, LANES), lambda i: (0, 0))],
        out_shape=[jax.ShapeDtypeStruct((nt, 1, tm), I32), jax.ShapeDtypeStruct((32, LANES), F32)],
        scratch_shapes=[pltpu.VMEM((32, LANES), F32), pltpu.VMEM((32, LANES), F32)],
        compiler_params=pltpu.CompilerParams(dimension_semantics=("arbitrary",)),
        name="rank",
    )(rt)


def _dispatch_kernel(pos_ref, zrow_ref, zflag_ref, nact_ref, xa_ref, xs_ref, zbuf, zsem, sem, *,
                     n_tiles, rows):
    @pl.when(pl.program_id(0) == 0)
    def _():
        _zero_partial_tiles(zrow_ref, zflag_ref, nact_ref, xs_ref, zbuf, zsem, n_tiles=n_tiles)

    base = pl.program_id(0) * rows
    for r in range(rows):
        pltpu.make_async_copy(xa_ref.at[pl.ds(r, 1)], xs_ref.at[pl.ds(pos_ref[base + r], 1)], sem).start(
            priority=r % 2)
    pltpu.make_async_copy(xa_ref, xs_ref.at[pl.ds(0, rows)], sem).wait()


def _zero_partial_tiles(zrow_ref, zflag_ref, nact_ref, xs_ref, zbuf, zsem, *, n_tiles):
    zbuf[...] = jnp.zeros_like(zbuf)

    def zero_copy(row):
        return pltpu.make_async_copy(zbuf, xs_ref.at[pl.ds(pl.multiple_of(row, MOE_ROWS), MOE_ROWS)], zsem)

    def bucket_tiles(op):
        def body(b, carry):
            @pl.when(zflag_ref[b] > 0)
            def _():
                op(zero_copy(zrow_ref[b]))
            return carry
        lax.fori_loop(0, N_PAIR_BUCKETS, body, 0)

    def unused_tiles(op):
        def body(i, carry):
            op(zero_copy(i * MOE_ROWS))
            return carry
        lax.fori_loop(nact_ref[0], n_tiles, body, 0)

    bucket_tiles(lambda cp: cp.start())
    unused_tiles(lambda cp: cp.start())
    bucket_tiles(lambda cp: cp.wait())
    unused_tiles(lambda cp: cp.wait())


def _dispatch(pos, zrow, zflag, n_active, xa, *, n_tiles):
    n_tok = xa.shape[0]
    n_sorted = n_tiles * MOE_ROWS
    rows = min(DMA_ROWS, n_tok)
    return pl.pallas_call(
        functools.partial(_dispatch_kernel, n_tiles=n_tiles, rows=rows),
        grid_spec=pltpu.PrefetchScalarGridSpec(
            num_scalar_prefetch=4,
            grid=(n_tok // rows,),
            in_specs=[pl.BlockSpec((rows, ROW_W), lambda i, *_: (i, 0))],
            out_specs=pl.BlockSpec(memory_space=pl.ANY),
            scratch_shapes=[pltpu.VMEM((MOE_ROWS, ROW_W), F32), pltpu.SemaphoreType.DMA(()),
                            pltpu.SemaphoreType.DMA(())],
        ),
        out_shape=jax.ShapeDtypeStruct((n_sorted, ROW_W), F32),
        compiler_params=pltpu.CompilerParams(
            dimension_semantics=("arbitrary",), has_side_effects=True,
            vmem_limit_bytes=_vmem_limit((2 * rows + MOE_ROWS) * ROW_W * 4)),
        name="dispatch",
    )(pos, zrow, zflag, n_active, xa)


def _moe_kernel(plan_ref, nact_ref, xs_ref, gf_ref, wg_hbm, wu_hbm, wd_hbm, ys_ref, wgu_sc, wd_sc, wsem):
    i = pl.program_id(0)

    def weight_copies(ea, eb, buf):
        cps = []
        for k, (src, e) in enumerate(((wg_hbm, ea), (wu_hbm, ea), (wg_hbm, eb), (wu_hbm, eb))):
            cps.append(pltpu.make_async_copy(src.at[e], wgu_sc.at[buf, k], wsem.at[buf]))
        for k, e in enumerate((ea, eb)):
            cps.append(pltpu.make_async_copy(wd_hbm.at[e], wd_sc.at[buf, k], wsem.at[buf]))
        return cps

    @pl.when(i < nact_ref[0])
    def _():
        buf = plan_ref[4, i]

        @pl.when(plan_ref[3, i] == 1)
        def _():
            @pl.when(i == 0)
            def _():
                for cp in weight_copies(plan_ref[1, i], plan_ref[2, i], buf):
                    cp.start()
            for cp in weight_copies(plan_ref[1, i], plan_ref[2, i], buf):
                cp.wait()

            @pl.when(plan_ref[7, i] == 1)
            def _():
                for cp in weight_copies(plan_ref[5, i], plan_ref[6, i], 1 - buf):
                    cp.start()

        x2 = xs_ref[:, :D_MODEL]
        aux = xs_ref[:, D_MODEL:]
        wa = aux[:, 1:2]
        wb = aux[:, 2:3]
        h = _rms(x2, gf_ref[...]).astype(BF16)

        def mlp(slot):
            g = jnp.dot(h, wgu_sc[buf, 2 * slot].astype(BF16), preferred_element_type=F32)
            u = jnp.dot(h, wgu_sc[buf, 2 * slot + 1].astype(BF16), preferred_element_type=F32)
            act = (g / (1.0 + jnp.exp(-g))) * u
            return jnp.dot(act.astype(BF16), wd_sc[buf, slot].astype(BF16), preferred_element_type=F32)

        ys_ref[...] = x2 + (wa * mlp(0) + wb * mlp(1))

    @pl.when(i >= nact_ref[0])
    def _():
        ys_ref[...] = jnp.zeros_like(ys_ref)


def _moe(plan, n_active, xs, g_ffn, w_gate, w_up, w_down):
    n_tiles = xs.shape[0] // MOE_ROWS
    t = MOE_ROWS
    est = (2 * (t * ROW_W * 4 + t * D_MODEL * 4) + 2 * 6 * D_MODEL * D_EXPERT * 4 + 4 * D_MODEL * D_EXPERT * 4
           + 10 * t * D_MODEL * 4)
    return pl.pallas_call(
        _moe_kernel,
        grid_spec=pltpu.PrefetchScalarGridSpec(
            num_scalar_prefetch=2,
            grid=(n_tiles,),
            in_specs=[
                pl.BlockSpec((t, ROW_W), lambda i, plan, na: (plan[0, i], 0)),
                pl.BlockSpec((1, D_MODEL), lambda i, plan, na: (0, 0)),
                pl.BlockSpec(memory_space=pl.ANY),
                pl.BlockSpec(memory_space=pl.ANY),
                pl.BlockSpec(memory_space=pl.ANY),
            ],
            out_specs=pl.BlockSpec((t, D_MODEL), lambda i, plan, na: (i, 0)),
            scratch_shapes=[pltpu.VMEM((2, 4, D_MODEL, D_EXPERT), F32), pltpu.VMEM((2, 2, D_EXPERT, D_MODEL), F32),
                            pltpu.SemaphoreType.DMA((2,))],
        ),
        out_shape=jax.ShapeDtypeStruct((n_tiles * t, D_MODEL), F32),
        compiler_params=pltpu.CompilerParams(
            dimension_semantics=("arbitrary",), vmem_limit_bytes=_vmem_limit(est)),
        name="moe",
    )(plan, n_active, xs, g_ffn, w_gate, w_up, w_down)


def _combine_kernel(pos_ref, ys_ref, out_ref, sem, *, rows):
    base = pl.program_id(0) * rows
    for r in range(rows):
        pltpu.make_async_copy(ys_ref.at[pl.ds(pos_ref[base + r], 1)], out_ref.at[pl.ds(r, 1)], sem).start(
            priority=r % 2)
    pltpu.make_async_copy(ys_ref.at[pl.ds(0, rows)], out_ref, sem).wait()


def _combine(pos, ys, *, n_tok):
    rows = min(DMA_ROWS, n_tok)
    return pl.pallas_call(
        functools.partial(_combine_kernel, rows=rows),
        grid_spec=pltpu.PrefetchScalarGridSpec(
            num_scalar_prefetch=1,
            grid=(n_tok // rows,),
            in_specs=[pl.BlockSpec(memory_space=pl.ANY)],
            out_specs=pl.BlockSpec((rows, D_MODEL), lambda i, *_: (i, 0)),
            scratch_shapes=[pltpu.SemaphoreType.DMA(())],
        ),
        out_shape=jax.ShapeDtypeStruct((n_tok, D_MODEL), F32),
        compiler_params=pltpu.CompilerParams(
            dimension_semantics=("arbitrary",), has_side_effects=True,
            vmem_limit_bytes=_vmem_limit(2 * rows * D_MODEL * 4)),
        name="combine",
    )(pos, ys)


def _t5_bucket(rel):
    nb = N_BUCKETS_T5 // 2
    max_exact = nb // 2
    ret = (rel > 0).astype(I32) * nb
    n = jnp.abs(rel)
    nf = jnp.maximum(n, 1).astype(F32)
    large = max_exact + (jnp.log(nf / max_exact) / math.log(MAX_DISTANCE / max_exact)
                         * (nb - max_exact)).astype(I32)
    large = jnp.minimum(large, nb - 1)
    return ret + jnp.where(n < max_exact, n, large)


def _diff_bias(rel_bias, t):
    assert t % CHUNK == 0 and t >= MAX_DISTANCE
    n_rel = 3 * t
    vec = rel_bias[_t5_bucket(jnp.arange(n_rel, dtype=I32) - (2 * t - 1))].astype(F32).T
    toe = jnp.tile(vec, (1, t))[:, :t * (n_rel - 1)].reshape(N_HEADS, t, n_rel - 1)
    qpos = jnp.arange(t, dtype=I32)[:, None]
    kpos = jnp.arange(t, dtype=I32)[None, :]
    far = rel_bias[_t5_bucket(jnp.asarray(-2 * t, I32))].astype(F32)[:, None, None]
    b0 = (toe[:, :, 2 * t - 1:3 * t - 1] - far) * LOG2E
    b0 = jnp.where((kpos // CHUNK <= qpos // CHUNK)[None], b0, NEG_INF)
    b1 = (toe[:, :, t - 1:2 * t - 1] - far) * LOG2E
    return jnp.swapaxes(jnp.stack([b0, b1], axis=1), -1, -2)


def _tile_plan(counts, n_tiles):
    t = MOE_ROWS
    tiles = (counts + t - 1) // t
    ends = jnp.cumsum(tiles)
    n_active = ends[-1]
    tile_idx = jnp.minimum(jnp.arange(n_tiles, dtype=I32), n_active - 1)
    tile_bucket = jnp.minimum(jnp.sum((ends[None, :] <= tile_idx[:, None]).astype(I32), axis=1), N_PAIR_BUCKETS - 1)
    pa = jnp.asarray([p[0] for p in PAIRS], I32)
    pb = jnp.asarray([p[1] for p in PAIRS], I32)
    grp = tile_bucket // len(PAIRS)
    pair = tile_bucket % len(PAIRS)
    tile_a = grp * EXPERTS_PER_GROUP + pa[pair]
    tile_b = grp * EXPERTS_PER_GROUP + pb[pair]
    zrow = ((ends - 1) * t).astype(I32)
    zflag = (counts > 0).astype(I32)
    prev_bucket = jnp.concatenate([jnp.full((1,), -1, I32), tile_bucket[:-1]])
    first = (tile_bucket != prev_bucket).astype(I32)
    buf = (jnp.cumsum(first) - 1) % 2
    nxt = ends[tile_bucket]
    has_next = (nxt < n_active).astype(I32)
    nxt = jnp.minimum(nxt, n_tiles - 1)
    plan = jnp.stack([tile_idx, tile_a, tile_b, first, buf, tile_a[nxt], tile_b[nxt], has_next]).astype(I32)
    return zrow, zflag, plan, n_active.astype(I32)


def kernel(x, mem, rel_bias, norm_mix_g, w_in, b_forget, diff_q_norm_g, diff_k_norm_g, diff_lambda_q1, diff_lambda_k1, diff_lambda_q2, diff_lambda_k2, diff_subln_g, fox_q_norm_g, fox_k_norm_g, fox_out_norm_g, w_out, norm_cross_g, norm_mem_g, w_cq, w_ckv, cross_q_norm_g, cross_k_norm_g, w_co, norm_ffn_g, w_group_router, b_group_router, w_expert_router, b_expert_router, w_exp_gate, w_exp_up, w_exp_down):
    batch, seq, d = x.shape
    assert d == D_MODEL and norm_mix_g.shape[0] == 1 and mem.shape[1] == N_MEM
    n_tok = batch * seq
    assert seq % ATTN_TILE == 0 or seq < ATTN_TILE
    l = 0

    w_qk = jnp.concatenate([w_in[l, :, :1024], w_in[l, :, 1536:2560]], axis=1).astype(BF16)
    w_vt = jnp.concatenate([w_in[l, :, 1024:1536], w_in[l, :, 2560:3072]], axis=1).astype(BF16)
    wft = jnp.zeros((D_MODEL, LANES), F32).at[:, :N_HEADS].set(w_in[l, :, 3072:]).astype(BF16)
    bf = jnp.zeros((8, 1), F32).at[:N_HEADS, 0].set(b_forget[l].astype(F32))
    ones = jnp.ones((HEAD_DIM,), F32)
    qg = jnp.stack([
        jnp.tile(diff_q_norm_g[l].astype(F32), 2) * (DIFF_QK_DIM ** -0.5 * LOG2E),
        jnp.tile(diff_k_norm_g[l].astype(F32), 2),
        fox_q_norm_g[l].astype(F32) * (HEAD_DIM ** -0.5 * LOG2E), fox_k_norm_g[l].astype(F32),
        ones, ones, ones, ones])
    lam = (jnp.exp(jnp.sum(diff_lambda_q1[l].astype(F32) * diff_lambda_k1[l].astype(F32)))
           - jnp.exp(jnp.sum(diff_lambda_q2[l].astype(F32) * diff_lambda_k2[l].astype(F32)))
           + LAM_INIT).reshape(1)
    t_attn = min(ATTN_TILE, seq)
    bias_tiles_t = _diff_bias(rel_bias, t_attn)
    wr = jnp.zeros((32, D_MODEL), F32)
    wr = wr.at[:N_GROUPS].set(w_group_router[l].T).at[N_GROUPS:N_GROUPS + N_EXPERTS].set(w_expert_router[l].T)
    wr_hi = wr.astype(BF16)
    wr_lo = (wr - wr_hi.astype(F32)).astype(BF16)
    rb = jnp.zeros((32, 1), F32)
    rb = rb.at[:N_GROUPS, 0].set(b_group_router[l]).at[N_GROUPS:N_GROUPS + N_EXPERTS, 0].set(b_expert_router[l])

    row = lambda v: v.astype(F32).reshape(1, -1)
    x2d = x.reshape(n_tok, D_MODEL)

    qk, vt, cum = _mix_proj(x2d, row(norm_mix_g[l]), w_qk, w_vt, wft, bf, qg, batch=batch, seq=seq)
    cum4 = cum[:, :N_HEADS].reshape(batch, N_HEADS, 1, seq)
    gain_t = lambda v: jnp.broadcast_to(v.astype(F32)[:, None], (HEAD_DIM, t_attn))
    mixed = _attn(qk, vt, bias_tiles_t, lam, cum4, gain_t(diff_subln_g[l]), gain_t(fox_out_norm_g[l]),
                  batch=batch, seq=seq)
    x1, qc = _out_q(x2d, mixed, w_out[l].astype(BF16), row(norm_cross_g[l]),
                    w_cq[l].astype(BF16), row(cross_q_norm_g[l]))

    kc, vc = _mem_kv(mem.reshape(batch * N_MEM, D_MODEL), row(norm_mem_g[l]), w_ckv[l].astype(BF16),
                     row(cross_k_norm_g[l]), batch=batch)
    xa, rt = _cross_router(qc, kc, vc, x1, w_co[l].astype(BF16), row(norm_ffn_g[l]), wr_hi, wr_lo, rb,
                           batch=batch, seq=seq)

    pos, cnt = _rank(rt)
    pos = pos.reshape(n_tok)
    counts = cnt[:N_PAIR_BUCKETS, 0].astype(I32)
    n_tiles = n_tok // MOE_ROWS + N_PAIR_BUCKETS
    zrow, zflag, plan, n_active = _tile_plan(counts, n_tiles)
    n_active = n_active.reshape(1)
    xs = _dispatch(pos, zrow, zflag, n_active, xa, n_tiles=n_tiles)
    ys = _moe(plan, n_active, xs, row(norm_ffn_g[l]),
              w_exp_gate[l].astype(F32), w_exp_up[l].astype(F32), w_exp_down[l].astype(F32))
    out = _combine(pos, ys, n_tok=n_tok)
    return out.reshape(batch, seq, D_MODEL)
```

```python
import functools
import math

import jax
import jax.numpy as jnp
import numpy as np
from jax import lax
from jax.experimental import pallas as pl
from jax.experimental.pallas import tpu as pltpu

F32 = jnp.float32
BF16 = jnp.bfloat16
I32 = jnp.int32

D_MODEL = 1024
CHUNK = 64
N_MEM = 256
N_HEADS = 4
HEAD_DIM = 128
VT_ROWS = HEAD_DIM + 16
DIFF_QK_DIM = 64
CROSS_HEAD_DIM = 256
N_BUCKETS_T5 = 32
MAX_DISTANCE = 128
N_GROUPS = 4
EXPERTS_PER_GROUP = 4
N_EXPERTS = N_GROUPS * EXPERTS_PER_GROUP
D_EXPERT = 512
EPS = 1e-6
NEG_INF = -1e30
LAM_INIT = 0.8 - 0.6 * math.exp(-0.3 * 0)
LOG2E = math.log2(math.e)

PAIRS = ((0, 1), (0, 2), (0, 3), (1, 3), (1, 2), (3, 2))
N_PAIR_BUCKETS = N_GROUPS * len(PAIRS)

LANES = 128
VMEM_LIMIT_CAP = 56 * 1024 * 1024

PROJ_ROWS = 1024
ATTN_TILE = 256
CROSS_ROWS = 1024
MOE_ROWS = 256
AUX_COLS = LANES
ROW_W = D_MODEL + AUX_COLS
DMA_ROWS = 2048


def _vmem_limit(nbytes):
    return int(min(max(nbytes * 5 // 4, 32 * 1024 * 1024), VMEM_LIMIT_CAP))


def _nt_dot(a, b):
    return lax.dot_general(a, b, (((1,), (1,)), ((), ())), preferred_element_type=F32)


def _rms(x, g):
    ms = jnp.mean(x * x, axis=-1, keepdims=True)
    return x * lax.rsqrt(ms + EPS) * g


def _mix_proj_kernel(x_ref, g_ref, w_ref, wvt_ref, wft_ref, bf_ref, qg_ref, o_ref, vt_ref, cum_ref, carry_ref,
                     *, tm):
    si = pl.program_id(1)
    h = _rms(x_ref[...], g_ref[...]).astype(BF16)
    lane = lax.broadcasted_iota(I32, (tm, HEAD_DIM), 1)
    lo = lane < DIFF_QK_DIM
    for kind in range(4):
        p = jnp.dot(h, w_ref[:, kind * 512:(kind + 1) * 512], preferred_element_type=F32)
        for hh in range(N_HEADS):
            ph = p[:, hh * HEAD_DIM:(hh + 1) * HEAD_DIM]
            if kind in (0, 1):
                sq = ph * ph
                s_lo = jnp.sum(jnp.where(lo, sq, 0.0), axis=-1, keepdims=True)
                s_hi = jnp.sum(jnp.where(lo, 0.0, sq), axis=-1, keepdims=True)
                ms = jnp.where(lo, s_lo, s_hi) * (1.0 / DIFF_QK_DIM)
                ph = ph * lax.rsqrt(ms + EPS) * qg_ref[kind:kind + 1, :]
            else:
                ph = _rms(ph, qg_ref[kind:kind + 1, :])
            o_ref[0, kind * N_HEADS + hh] = ph.astype(BF16)
    for kind in range(2):
        pt = _nt_dot(wvt_ref[kind * 512:(kind + 1) * 512, :], h)
        for hh in range(N_HEADS):
            vt_ref[0, kind * N_HEADS + hh, :HEAD_DIM, :] = pt[hh * HEAD_DIM:(hh + 1) * HEAD_DIM, :].astype(BF16)
            vt_ref[0, kind * N_HEADS + hh, HEAD_DIM:, :] = jnp.ones((VT_ROWS - HEAD_DIM, tm), BF16)

    z = _nt_dot(wft_ref[...], h) + bf_ref[...]
    logf = jnp.minimum(z, 0.0) - jnp.log(1.0 + jnp.exp(-jnp.abs(z)))
    lane8 = lax.broadcasted_iota(I32, (8, tm), 1)
    c = logf * LOG2E
    k = 1
    while k < tm:
        c = c + jnp.where(lane8 >= k, pltpu.roll(c, k, axis=1), 0.0)
        k *= 2

    @pl.when(si == 0)
    def _():
        carry_ref[...] = jnp.zeros_like(carry_ref)

    c = c + carry_ref[:, 0:1]
    cum_ref[0] = c
    carry_ref[...] = jnp.broadcast_to(c[:, tm - 1:tm], carry_ref.shape)


def _mix_proj(x2d, g, w_qk, w_vt, wft, bf, qg, *, batch, seq):
    tm = min(PROJ_ROWS, seq)
    ns = seq // tm
    est = 2 * (tm * D_MODEL * 4 + D_MODEL * 3072 * 2 + 24 * tm * HEAD_DIM * 2) + 8 * tm * D_MODEL
    return pl.pallas_call(
        functools.partial(_mix_proj_kernel, tm=tm),
        grid=(batch, ns),
        in_specs=[
            pl.BlockSpec((tm, D_MODEL), lambda b, s: (b * ns + s, 0)),
            pl.BlockSpec((1, D_MODEL), lambda b, s: (0, 0)),
            pl.BlockSpec((D_MODEL, 2048), lambda b, s: (0, 0)),
            pl.BlockSpec((1024, D_MODEL), lambda b, s: (0, 0)),
            pl.BlockSpec((8, D_MODEL), lambda b, s: (0, 0)),
            pl.BlockSpec((8, 1), lambda b, s: (0, 0)),
            pl.BlockSpec((8, HEAD_DIM), lambda b, s: (0, 0)),
        ],
        out_specs=[
            pl.BlockSpec((1, 16, tm, HEAD_DIM), lambda b, s: (b, 0, s, 0)),
            pl.BlockSpec((1, 8, VT_ROWS, tm), lambda b, s: (b, 0, 0, s)),
            pl.BlockSpec((1, 8, tm), lambda b, s: (b, 0, s)),
        ],
        out_shape=[
            jax.ShapeDtypeStruct((batch, 16, seq, HEAD_DIM), BF16),
            jax.ShapeDtypeStruct((batch, 8, VT_ROWS, seq), BF16),
            jax.ShapeDtypeStruct((batch, 8, seq), F32),
        ],
        scratch_shapes=[pltpu.VMEM((8, LANES), F32)],
        compiler_params=pltpu.CompilerParams(
            dimension_semantics=("arbitrary", "arbitrary"), vmem_limit_bytes=_vmem_limit(est)),
        name="mix_proj",
    )(x2d, g, w_qk, w_vt, wft, bf, qg)


def _init_softmax(m_sc, acc_sc):
    m_sc[...] = jnp.full(m_sc.shape, NEG_INF, F32)
    acc_sc[...] = jnp.zeros_like(acc_sc)


N_CHAINS = 3 * N_HEADS


def _attn_kernel(lam_ref, dq_ref, dk_ref, dvt_ref, fq_ref, fk_ref, fvt_ref, bias_ref, c_ref,
                 gd_ref, gf_ref, o_ref, m_sc, acc_sc, qm_sc, ccol_sc, s_sc, mt_sc, s2_sc, mt2_sc, *, t, seq):
    qi = pl.program_id(1)
    _init_softmax(m_sc, acc_sc)
    key = lax.broadcasted_iota(I32, (t, t), 0)
    qry = lax.broadcasted_iota(I32, (t, t), 1)

    lane = lax.broadcasted_iota(I32, (t, HEAD_DIM), 1)
    for hh in range(N_HEADS):
        q = dq_ref[0, hh]
        zero = jnp.zeros_like(q)
        qm_sc[2 * hh] = jnp.where(lane < DIFF_QK_DIM, q, zero)
        qm_sc[2 * hh + 1] = jnp.where(lane < DIFF_QK_DIM, zero, q)

    @pl.when(qi == 0)
    def _():
        for hh in range(N_HEADS):
            for j in range(seq // t):
                row = c_ref[0, hh, :, j * t:(j + 1) * t]
                ccol_sc[hh, j * t:(j + 1) * t, :] = jnp.sum(
                    jnp.where(key == qry, jnp.broadcast_to(row, (t, t)), 0.0), axis=1, keepdims=True)

    q_off = pl.multiple_of(qi * t, t)

    bufs = ((s_sc, mt_sc), (s2_sc, mt2_sc))

    def logits(j, where, par, slot):
        off = pl.multiple_of(j * t, t)
        s_buf, mt_buf = bufs[par]
        if slot < 2 * N_HEADS:
            hh = slot // 2
            s_t = _nt_dot(dk_ref[0, hh, pl.ds(off, t), :], qm_sc[slot])
            if where != "far":
                s_t = s_t + bias_ref[hh, 1 if where == "near" else 0]
            s_buf[slot] = s_t
            mt_buf[slot] = jnp.max(s_t, axis=0, keepdims=True)
        else:
            hh = slot - 2 * N_HEADS
            c_k = ccol_sc[hh, pl.ds(off, t), :]
            s_t = _nt_dot(fk_ref[0, hh, pl.ds(off, t), :], fq_ref[0, hh]) - c_k
            if where == "diag":
                s_t = jnp.where(key <= qry, s_t, NEG_INF)
            s_buf[slot] = s_t
            mt_buf[slot] = jnp.max(s_t, axis=0, keepdims=True) + c_ref[0, hh, :, pl.ds(q_off, t)]

    def update(j, par, slot):
        off = pl.multiple_of(j * t, t)
        s_buf, mt_buf = bufs[par]
        vt_ref = dvt_ref if slot < 2 * N_HEADS else fvt_ref
        hh = slot // 2 if slot < 2 * N_HEADS else slot - 2 * N_HEADS
        v_t = vt_ref[0, hh, :, pl.ds(off, t)]
        m_prev = m_sc[slot]
        m_new = jnp.maximum(m_prev, mt_buf[slot])
        alpha = jnp.exp2(m_prev - m_new)
        if slot < 2 * N_HEADS:
            p_t = jnp.exp2(s_buf[slot] - m_new)
        else:
            p_t = jnp.exp2(s_buf[slot] - (m_new - c_ref[0, hh, :, pl.ds(q_off, t)]))
        acc_sc[slot] = alpha * acc_sc[slot] + jnp.dot(v_t, p_t.astype(BF16), preferred_element_type=F32)
        m_sc[slot] = m_new

    def phase(upd=None, nxt=None):
        for slot in range(N_CHAINS):
            if nxt is not None:
                logits(*nxt, slot)
            if upd is not None:
                update(*upd, slot)

    n_far = jnp.maximum(qi - 1, 0)
    far_tile = lambda k: jnp.maximum(qi - 2 - k, 0)

    phase(nxt=(qi, "diag", 0))

    @pl.when(qi == 0)
    def _():
        phase(upd=(qi, 0))

    @pl.when(qi == 1)
    def _():
        phase(upd=(qi, 0), nxt=(qi - 1, "near", 1))
        phase(upd=(qi - 1, 1))

    @pl.when(qi >= 2)
    def _():
        phase(upd=(qi, 0), nxt=(qi - 1, "near", 1))
        phase(upd=(qi - 1, 1), nxt=(far_tile(0), "far", 0))

        def far_pair(i, carry):
            k = 2 * i
            phase(upd=(far_tile(k), 0), nxt=(far_tile(k + 1), "far", 1))

            @pl.when(k + 1 < n_far)
            def _():
                phase(upd=(far_tile(k + 1), 1), nxt=(far_tile(k + 2), "far", 0))
            return carry

        lax.fori_loop(0, (n_far + 1) // 2, far_pair, 0)

    def normalised(slot):
        return acc_sc[slot, :HEAD_DIM, :] / acc_sc[slot, HEAD_DIM:HEAD_DIM + 1, :]

    def rms_t(o_t, g_t):
        ms = jnp.mean(o_t * o_t, axis=0, keepdims=True)
        return o_t * lax.rsqrt(ms + EPS) * g_t

    for hh in range(N_HEADS):
        o_t = normalised(2 * hh) - lam_ref[0] * normalised(2 * hh + 1)
        o_ref[:, hh * HEAD_DIM:(hh + 1) * HEAD_DIM] = (
            rms_t(o_t, gd_ref[...]) * (1.0 - LAM_INIT)).T.astype(BF16)
    for hh in range(N_HEADS):
        o_t = normalised(2 * N_HEADS + hh)
        o_ref[:, (N_HEADS + hh) * HEAD_DIM:(N_HEADS + hh + 1) * HEAD_DIM] = rms_t(o_t, gf_ref[...]).T.astype(BF16)


def _attn(qk, vt, bias_tiles_t, lam, cum4, subln_g, fox_g, *, batch, seq):
    t = min(ATTN_TILE, seq)
    nq = seq // t
    head_blk = N_HEADS * seq * HEAD_DIM * 2
    est = (2 * (2 * N_HEADS * t * HEAD_DIM * 2 + 4 * head_blk + N_HEADS * 2 * t * t * 4 + N_HEADS * 8 * seq * 4
                + t * 2 * N_HEADS * HEAD_DIM * 2)
           + N_CHAINS * (HEAD_DIM + 16) * t * 4 + 2 * N_HEADS * t * HEAD_DIM * 2 + N_HEADS * seq * LANES * 4
           + (2 * N_CHAINS + 36) * t * t * 4)
    qspec = lambda blk: pl.BlockSpec((1, N_HEADS, t, HEAD_DIM), lambda b, i: (b, blk, i, 0))
    kspec = lambda blk: pl.BlockSpec((1, N_HEADS, seq, HEAD_DIM), lambda b, i: (b, blk, 0, 0))
    vspec = lambda blk: pl.BlockSpec((1, N_HEADS, VT_ROWS, seq), lambda b, i: (b, blk, 0, 0))
    return pl.pallas_call(
        functools.partial(_attn_kernel, t=t, seq=seq),
        grid=(batch, nq),
        in_specs=[
            pl.BlockSpec(memory_space=pltpu.SMEM),
            qspec(0), kspec(1), vspec(0),
            qspec(2), kspec(3), vspec(1),
            pl.BlockSpec((N_HEADS, 2, t, t), lambda b, i: (0, 0, 0, 0)),
            pl.BlockSpec((1, N_HEADS, 1, seq), lambda b, i: (b, 0, 0, 0)),
            pl.BlockSpec((HEAD_DIM, t), lambda b, i: (0, 0)),
            pl.BlockSpec((HEAD_DIM, t), lambda b, i: (0, 0)),
        ],
        out_specs=pl.BlockSpec((t, 2 * N_HEADS * HEAD_DIM), lambda b, i: (b * nq + i, 0)),
        out_shape=jax.ShapeDtypeStruct((batch * seq, 2 * N_HEADS * HEAD_DIM), BF16),
        scratch_shapes=[pltpu.VMEM((N_CHAINS, 1, t), F32),
                        pltpu.VMEM((N_CHAINS, VT_ROWS, t), F32),
                        pltpu.VMEM((2 * N_HEADS, t, HEAD_DIM), BF16),
                        pltpu.VMEM((N_HEADS, seq, 1), F32),
                        pltpu.VMEM((N_CHAINS, t, t), F32), pltpu.VMEM((N_CHAINS, 1, t), F32),
                        pltpu.VMEM((N_CHAINS, t, t), F32), pltpu.VMEM((N_CHAINS, 1, t), F32)],
        compiler_params=pltpu.CompilerParams(
            dimension_semantics=("arbitrary", "arbitrary"), vmem_limit_bytes=_vmem_limit(est)),
        name="attn",
    )(lam, qk, qk, vt, qk, qk, vt, bias_tiles_t, cum4, subln_g, fox_g)


def _out_q_kernel(x_ref, m_ref, wo_ref, gc_ref, wq_ref, qg_ref, x1_ref, qc_ref):
    x1 = x_ref[...] + jnp.dot(m_ref[...], wo_ref[...], preferred_element_type=F32)
    x1_ref[...] = x1
    hc = _rms(x1, gc_ref[...]).astype(BF16)
    q = jnp.dot(hc, wq_ref[...], preferred_element_type=F32)
    for hh in range(N_HEADS):
        sl = slice(hh * CROSS_HEAD_DIM, (hh + 1) * CROSS_HEAD_DIM)
        qh = _rms(q[:, sl], qg_ref[...]) * (CROSS_HEAD_DIM ** -0.5)
        qc_ref[:, sl] = qh.astype(BF16)


def _out_q(x2d, mixed, w_out, g_cross, w_cq, q_g):
    n = x2d.shape[0]
    tm = min(PROJ_ROWS, n)
    est = 2 * (tm * D_MODEL * 4 * 2 + tm * D_MODEL * 2 + 2 * D_MODEL * D_MODEL * 2 + tm * D_MODEL * 2) + 6 * tm * D_MODEL * 4
    return pl.pallas_call(
        _out_q_kernel,
        grid=(n // tm,),
        in_specs=[
            pl.BlockSpec((tm, D_MODEL), lambda i: (i, 0)),
            pl.BlockSpec((tm, D_MODEL), lambda i: (i, 0)),
            pl.BlockSpec((D_MODEL, D_MODEL), lambda i: (0, 0)),
            pl.BlockSpec((1, D_MODEL), lambda i: (0, 0)),
            pl.BlockSpec((D_MODEL, D_MODEL), lambda i: (0, 0)),
            pl.BlockSpec((1, CROSS_HEAD_DIM), lambda i: (0, 0)),
        ],
        out_specs=[pl.BlockSpec((tm, D_MODEL), lambda i: (i, 0)),
                   pl.BlockSpec((tm, D_MODEL), lambda i: (i, 0))],
        out_shape=[jax.ShapeDtypeStruct((n, D_MODEL), F32), jax.ShapeDtypeStruct((n, D_MODEL), BF16)],
        compiler_params=pltpu.CompilerParams(
            dimension_semantics=("arbitrary",), vmem_limit_bytes=_vmem_limit(est)),
        name="out_q",
    )(x2d, mixed, w_out, g_cross, w_cq, q_g)


def _mem_kv_kernel(mem_ref, gm_ref, w_ref, kg_ref, k_ref, v_ref):
    mn = _rms(mem_ref[...], gm_ref[...]).astype(BF16)
    kv = jnp.dot(mn, w_ref[...], preferred_element_type=F32)
    for hh in range(N_HEADS):
        sl = slice(hh * CROSS_HEAD_DIM, (hh + 1) * CROSS_HEAD_DIM)
        k_ref[0, :, sl] = _rms(kv[:, sl], kg_ref[...]).astype(BF16)
    v_ref[0] = kv[:, D_MODEL:].astype(BF16)


def _mem_kv(mem2d, g_mem, w_ckv, k_g, *, batch):
    est = 2 * (N_MEM * D_MODEL * 4 + D_MODEL * 2 * D_MODEL * 2 + 2 * N_MEM * D_MODEL * 2) + 4 * N_MEM * 2 * D_MODEL * 4
    return pl.pallas_call(
        _mem_kv_kernel,
        grid=(batch,),
        in_specs=[
            pl.BlockSpec((N_MEM, D_MODEL), lambda b: (b, 0)),
            pl.BlockSpec((1, D_MODEL), lambda b: (0, 0)),
            pl.BlockSpec((D_MODEL, 2 * D_MODEL), lambda b: (0, 0)),
            pl.BlockSpec((1, CROSS_HEAD_DIM), lambda b: (0, 0)),
        ],
        out_specs=[pl.BlockSpec((1, N_MEM, D_MODEL), lambda b: (b, 0, 0)),
                   pl.BlockSpec((1, N_MEM, D_MODEL), lambda b: (b, 0, 0))],
        out_shape=[jax.ShapeDtypeStruct((batch, N_MEM, D_MODEL), BF16),
                   jax.ShapeDtypeStruct((batch, N_MEM, D_MODEL), BF16)],
        compiler_params=pltpu.CompilerParams(
            dimension_semantics=("arbitrary",), vmem_limit_bytes=_vmem_limit(est)),
        name="mem_kv",
    )(mem2d, g_mem, w_ckv, k_g)


def _route(logits_t):
    gl = logits_t[0:N_GROUPS]
    gmax = jnp.max(gl, axis=0, keepdims=True)
    eg = jnp.exp(gl - gmax)
    p_group = eg / jnp.sum(eg, axis=0, keepdims=True)
    p_g = jnp.max(p_group, axis=0, keepdims=True)
    g_idx = jnp.full_like(p_g, float(N_GROUPS))
    for g in reversed(range(N_GROUPS)):
        g_idx = jnp.where(p_group[g:g + 1] == p_g, float(g), g_idx)

    sel = []
    for j in range(EXPERTS_PER_GROUP):
        acc = jnp.zeros_like(p_g)
        for g in range(N_GROUPS):
            r = N_GROUPS + g * EXPERTS_PER_GROUP + j
            acc = jnp.where(g_idx == float(g), logits_t[r:r + 1], acc)
        sel.append(acc)
    smax = functools.reduce(jnp.maximum, sel)
    es = [jnp.exp(s - smax) for s in sel]
    den = functools.reduce(jnp.add, es)
    p_in = [e / den for e in es]

    v0 = functools.reduce(jnp.maximum, p_in)
    i0 = jnp.full_like(v0, float(EXPERTS_PER_GROUP))
    for j in reversed(range(EXPERTS_PER_GROUP)):
        i0 = jnp.where(p_in[j] == v0, float(j), i0)
    rest = [jnp.where(i0 == float(j), -1.0, p_in[j]) for j in range(EXPERTS_PER_GROUP)]
    v1 = functools.reduce(jnp.maximum, rest)
    i1 = jnp.full_like(v1, float(EXPERTS_PER_GROUP))
    for j in reversed(range(EXPERTS_PER_GROUP)):
        i1 = jnp.where(rest[j] == v1, float(j), i1)

    tot = v0 + v1
    w0 = p_g * (v0 / tot)
    w1 = p_g * (v1 / tot)
    first = i0 < i1
    a = jnp.where(first, i0, i1)
    b = jnp.where(first, i1, i0)
    w_lo = jnp.where(first, w0, w1)
    w_hi = jnp.where(first, w1, w0)
    pair = jnp.where(a == 0.0, b - 1.0, jnp.where(a == 1.0, jnp.where(b == 3.0, 3.0, 4.0), 5.0))
    swap = a == 2.0
    wa = jnp.where(swap, w_hi, w_lo)
    wb = jnp.where(swap, w_lo, w_hi)
    bucket = g_idx * float(len(PAIRS)) + pair
    return bucket, wa, wb


def _cross_router_kernel(q_ref, k_ref, v_ref, x1_ref, wco_ref, gf_ref, whi_ref, wlo_ref, rb_ref,
                         xa_ref, rt_ref, *, tm):
    q = q_ref[...]
    k = k_ref[0]
    v = v_ref[0]
    outs = []
    for hh in range(N_HEADS):
        sl = slice(hh * CROSS_HEAD_DIM, (hh + 1) * CROSS_HEAD_DIM)
        s = _nt_dot(q[:, sl], k[:, sl])
        p = jnp.exp(s - jnp.max(s, axis=-1, keepdims=True))
        l = jnp.sum(p, axis=-1, keepdims=True)
        o = jnp.dot(p.astype(BF16), v[:, sl], preferred_element_type=F32) / l
        outs.append(o.astype(BF16))
    o = jnp.concatenate(outs, axis=1)
    x2 = x1_ref[...] + jnp.dot(o, wco_ref[...], preferred_element_type=F32)
    xa_ref[:, :D_MODEL] = x2

    h3 = _rms(x2, gf_ref[...])
    hi = h3.astype(BF16)
    lo = (h3 - hi.astype(F32)).astype(BF16)
    lt = _nt_dot(whi_ref[...], hi) + (_nt_dot(whi_ref[...], lo) + _nt_dot(wlo_ref[...], hi))
    lt = lt + rb_ref[...]
    bucket, wa, wb = _route(lt)

    sub = lax.broadcasted_iota(I32, (8, tm), 0)
    rows = jnp.where(sub == 0, bucket, jnp.where(sub == 1, wa, jnp.where(sub == 2, wb, 0.0)))
    rt_ref[0] = rows
    sub_a = lax.broadcasted_iota(I32, (AUX_COLS, tm), 0)
    aux_t = jnp.where(sub_a == 0, bucket, jnp.where(sub_a == 1, wa, jnp.where(sub_a == 2, wb, 0.0)))
    xa_ref[:, D_MODEL:] = aux_t.T


def _cross_router(qc, kc, vc, x1, w_co, g_ffn, wr_hi, wr_lo, rb, *, batch, seq):
    tm = min(CROSS_ROWS, seq)
    ns = seq // tm
    n = batch * seq
    est = 2 * (tm * D_MODEL * 2 + 2 * N_MEM * D_MODEL * 2 + tm * D_MODEL * 4 + D_MODEL * D_MODEL * 2
               + tm * ROW_W * 4) + 10 * tm * D_MODEL * 4
    return pl.pallas_call(
        functools.partial(_cross_router_kernel, tm=tm),
        grid=(batch, ns),
        in_specs=[
            pl.BlockSpec((tm, D_MODEL), lambda b, s: (b * ns + s, 0)),
            pl.BlockSpec((1, N_MEM, D_MODEL), lambda b, s: (b, 0, 0)),
            pl.BlockSpec((1, N_MEM, D_MODEL), lambda b, s: (b, 0, 0)),
            pl.BlockSpec((tm, D_MODEL), lambda b, s: (b * ns + s, 0)),
            pl.BlockSpec((D_MODEL, D_MODEL), lambda b, s: (0, 0)),
            pl.BlockSpec((1, D_MODEL), lambda b, s: (0, 0)),
            pl.BlockSpec((32, D_MODEL), lambda b, s: (0, 0)),
            pl.BlockSpec((32, D_MODEL), lambda b, s: (0, 0)),
            pl.BlockSpec((32, 1), lambda b, s: (0, 0)),
        ],
        out_specs=[pl.BlockSpec((tm, ROW_W), lambda b, s: (b * ns + s, 0)),
                   pl.BlockSpec((1, 8, tm), lambda b, s: (b * ns + s, 0, 0))],
        out_shape=[jax.ShapeDtypeStruct((n, ROW_W), F32),
                   jax.ShapeDtypeStruct((n // tm, 8, tm), F32)],
        compiler_params=pltpu.CompilerParams(
            dimension_semantics=("arbitrary", "arbitrary"), vmem_limit_bytes=_vmem_limit(est)),
        name="cross_router",
    )(qc, kc, vc, x1, w_co, g_ffn, wr_hi, wr_lo, rb)


def _rank_kernel(rt_ref, pos_ref, cnt_ref, carry_sc, off_sc, *, tm, nt):
    i = pl.program_id(0)
    sub = lax.broadcasted_iota(I32, (32, tm), 0).astype(F32)

    @pl.when(i == 0)
    def _():
        n_hit = jnp.zeros((32, 1), F32)
        for j in range(nt):
            n_hit = n_hit + jnp.sum(jnp.where(sub == rt_ref[j, 0:1, :], 1.0, 0.0), axis=1, keepdims=True)
        counts = jnp.broadcast_to(n_hit, cnt_ref.shape)
        cnt_ref[...] = counts
        tiles = jnp.floor((counts + (MOE_ROWS - 1)) * (1.0 / MOE_ROWS))
        r = lax.broadcasted_iota(I32, (32, 32), 0)
        c = lax.broadcasted_iota(I32, (32, 32), 1)
        before = jnp.where(c < r, 1.0, 0.0).astype(BF16)
        off_sc[...] = jnp.dot(before, tiles.astype(BF16), preferred_element_type=F32) * MOE_ROWS
        carry_sc[...] = jnp.zeros_like(carry_sc)

    @pl.when(i >= 1)
    def _():
        hit = sub == rt_ref[i - 1, 0:1, :]
        r = lax.broadcasted_iota(I32, (tm, tm), 0)
        c = lax.broadcasted_iota(I32, (tm, tm), 1)
        upper = jnp.where(r <= c, 1.0, 0.0).astype(BF16)
        cum = jnp.dot(jnp.where(hit, 1.0, 0.0).astype(BF16), upper, preferred_element_type=F32)
        base = carry_sc[:, 0:1] + off_sc[:, 0:1]
        pos = jnp.sum(jnp.where(hit, cum - 1.0 + base, 0.0), axis=0, keepdims=True)
        pos_ref[0] = pos.astype(I32)
        carry_sc[...] = carry_sc[...] + jnp.broadcast_to(cum[:, tm - 1:tm], carry_sc.shape)


def _rank(rt):
    nt, _, tm = rt.shape
    return pl.pallas_call(
        functools.partial(_rank_kernel, tm=tm, nt=nt),
        grid=(nt + 1,),
        in_specs=[pl.BlockSpec((nt, 8, tm), lambda i: (0, 0, 0))],
        out_specs=[pl.BlockSpec((1, 1, tm), lambda i: (jnp.maximum(i - 1, 0), 0, 0)),
                   pl.BlockSpec((32, LANES), lambda i: (0, 0))],
        out_shape=[jax.ShapeDtypeStruct((nt, 1, tm), I32), jax.ShapeDtypeStruct((32, LANES), F32)],
        scratch_shapes=[pltpu.VMEM((32, LANES), F32), pltpu.VMEM((32, LANES), F32)],
        compiler_params=pltpu.CompilerParams(dimension_semantics=("arbitrary",)),
        name="rank",
    )(rt)


def _dispatch_kernel(pos_ref, zrow_ref, zflag_ref, nact_ref, xa_ref, xs_ref, zbuf, zsem, sem, *,
                     n_tiles, rows):
    @pl.when(pl.program_id(0) == 0)
    def _():
        _zero_partial_tiles(zrow_ref, zflag_ref, nact_ref, xs_ref, zbuf, zsem, n_tiles=n_tiles)

    base = pl.program_id(0) * rows
    for r in range(rows):
        pltpu.make_async_copy(xa_ref.at[pl.ds(r, 1)], xs_ref.at[pl.ds(pos_ref[base + r], 1)], sem).start(
            priority=r % 2)
    pltpu.make_async_copy(xa_ref, xs_ref.at[pl.ds(0, rows)], sem).wait()


def _zero_partial_tiles(zrow_ref, zflag_ref, nact_ref, xs_ref, zbuf, zsem, *, n_tiles):
    zbuf[...] = jnp.zeros_like(zbuf)

    def zero_copy(row):
        return pltpu.make_async_copy(zbuf, xs_ref.at[pl.ds(pl.multiple_of(row, MOE_ROWS), MOE_ROWS)], zsem)

    def bucket_tiles(op):
        def body(b, carry):
            @pl.when(zflag_ref[b] > 0)
            def _():
                op(zero_copy(zrow_ref[b]))
            return carry
        lax.fori_loop(0, N_PAIR_BUCKETS, body, 0)

    def unused_tiles(op):
        def body(i, carry):
            op(zero_copy(i * MOE_ROWS))
            return carry
        lax.fori_loop(nact_ref[0], n_tiles, body, 0)

    bucket_tiles(lambda cp: cp.start())
    unused_tiles(lambda cp: cp.start())
    bucket_tiles(lambda cp: cp.wait())
    unused_tiles(lambda cp: cp.wait())


def _dispatch(pos, zrow, zflag, n_active, xa, *, n_tiles):
    n_tok = xa.shape[0]
    n_sorted = n_tiles * MOE_ROWS
    rows = min(DMA_ROWS, n_tok)
    return pl.pallas_call(
        functools.partial(_dispatch_kernel, n_tiles=n_tiles, rows=rows),
        grid_spec=pltpu.PrefetchScalarGridSpec(
            num_scalar_prefetch=4,
            grid=(n_tok // rows,),
            in_specs=[pl.BlockSpec((rows, ROW_W), lambda i, *_: (i, 0))],
            out_specs=pl.BlockSpec(memory_space=pl.ANY),
            scratch_shapes=[pltpu.VMEM((MOE_ROWS, ROW_W), F32), pltpu.SemaphoreType.DMA(()),
                            pltpu.SemaphoreType.DMA(())],
        ),
        out_shape=jax.ShapeDtypeStruct((n_sorted, ROW_W), F32),
        compiler_params=pltpu.CompilerParams(
            dimension_semantics=("arbitrary",), has_side_effects=True,
            vmem_limit_bytes=_vmem_limit((2 * rows + MOE_ROWS) * ROW_W * 4)),
        name="dispatch",
    )(pos, zrow, zflag, n_active, xa)


def _moe_kernel(plan_ref, nact_ref, xs_ref, gf_ref, wg_hbm, wu_hbm, wd_hbm, ys_ref, wgu_sc, wd_sc, wsem):
    i = pl.program_id(0)

    def weight_copies(ea, eb, buf):
        cps = []
        for k, (src, e) in enumerate(((wg_hbm, ea), (wu_hbm, ea), (wg_hbm, eb), (wu_hbm, eb))):
            cps.append(pltpu.make_async_copy(src.at[e], wgu_sc.at[buf, k], wsem.at[buf]))
        for k, e in enumerate((ea, eb)):
            cps.append(pltpu.make_async_copy(wd_hbm.at[e], wd_sc.at[buf, k], wsem.at[buf]))
        return cps

    @pl.when(i < nact_ref[0])
    def _():
        buf = plan_ref[4, i]

        @pl.when(plan_ref[3, i] == 1)
        def _():
            @pl.when(i == 0)
            def _():
                for cp in weight_copies(plan_ref[1, i], plan_ref[2, i], buf):
                    cp.start()
            for cp in weight_copies(plan_ref[1, i], plan_ref[2, i], buf):
                cp.wait()

            @pl.when(plan_ref[7, i] == 1)
            def _():
                for cp in weight_copies(plan_ref[5, i], plan_ref[6, i], 1 - buf):
                    cp.start()

        x2 = xs_ref[:, :D_MODEL]
        aux = xs_ref[:, D_MODEL:]
        wa = aux[:, 1:2]
        wb = aux[:, 2:3]
        h = _rms(x2, gf_ref[...]).astype(BF16)

        def mlp(slot):
            g = jnp.dot(h, wgu_sc[buf, 2 * slot].astype(BF16), preferred_element_type=F32)
            u = jnp.dot(h, wgu_sc[buf, 2 * slot + 1].astype(BF16), preferred_element_type=F32)
            act = (g / (1.0 + jnp.exp(-g))) * u
            return jnp.dot(act.astype(BF16), wd_sc[buf, slot].astype(BF16), preferred_element_type=F32)

        ys_ref[...] = x2 + (wa * mlp(0) + wb * mlp(1))

    @pl.when(i >= nact_ref[0])
    def _():
        ys_ref[...] = jnp.zeros_like(ys_ref)


def _moe(plan, n_active, xs, g_ffn, w_gate, w_up, w_down):
    n_tiles = xs.shape[0] // MOE_ROWS
    t = MOE_ROWS
    est = (2 * (t * ROW_W * 4 + t * D_MODEL * 4) + 2 * 6 * D_MODEL * D_EXPERT * 4 + 4 * D_MODEL * D_EXPERT * 4
           + 10 * t * D_MODEL * 4)
    return pl.pallas_call(
        _moe_kernel,
        grid_spec=pltpu.PrefetchScalarGridSpec(
            num_scalar_prefetch=2,
            grid=(n_tiles,),
            in_specs=[
                pl.BlockSpec((t, ROW_W), lambda i, plan, na: (plan[0, i], 0)),
                pl.BlockSpec((1, D_MODEL), lambda i, plan, na: (0, 0)),
                pl.BlockSpec(memory_space=pl.ANY),
                pl.BlockSpec(memory_space=pl.ANY),
                pl.BlockSpec(memory_space=pl.ANY),
            ],
            out_specs=pl.BlockSpec((t, D_MODEL), lambda i, plan, na: (i, 0)),
            scratch_shapes=[pltpu.VMEM((2, 4, D_MODEL, D_EXPERT), F32), pltpu.VMEM((2, 2, D_EXPERT, D_MODEL), F32),
                            pltpu.SemaphoreType.DMA((2,))],
        ),
        out_shape=jax.ShapeDtypeStruct((n_tiles * t, D_MODEL), F32),
        compiler_params=pltpu.CompilerParams(
            dimension_semantics=("arbitrary",), vmem_limit_bytes=_vmem_limit(est)),
        name="moe",
    )(plan, n_active, xs, g_ffn, w_gate, w_up, w_down)


def _combine_kernel(pos_ref, ys_ref, out_ref, sem, *, rows):
    base = pl.program_id(0) * rows
    for r in range(rows):
        pltpu.make_async_copy(ys_ref.at[pl.ds(pos_ref[base + r], 1)], out_ref.at[pl.ds(r, 1)], sem).start(
            priority=r % 2)
    pltpu.make_async_copy(ys_ref.at[pl.ds(0, rows)], out_ref, sem).wait()


def _combine(pos, ys, *, n_tok):
    rows = min(DMA_ROWS, n_tok)
    return pl.pallas_call(
        functools.partial(_combine_kernel, rows=rows),
        grid_spec=pltpu.PrefetchScalarGridSpec(
            num_scalar_prefetch=1,
            grid=(n_tok // rows,),
            in_specs=[pl.BlockSpec(memory_space=pl.ANY)],
            out_specs=pl.BlockSpec((rows, D_MODEL), lambda i, *_: (i, 0)),
            scratch_shapes=[pltpu.SemaphoreType.DMA(())],
        ),
        out_shape=jax.ShapeDtypeStruct((n_tok, D_MODEL), F32),
        compiler_params=pltpu.CompilerParams(
            dimension_semantics=("arbitrary",), has_side_effects=True,
            vmem_limit_bytes=_vmem_limit(2 * rows * D_MODEL * 4)),
        name="combine",
    )(pos, ys)


def _t5_bucket(rel):
    nb = N_BUCKETS_T5 // 2
    max_exact = nb // 2
    ret = (rel > 0).astype(I32) * nb
    n = jnp.abs(rel)
    nf = jnp.maximum(n, 1).astype(F32)
    large = max_exact + (jnp.log(nf / max_exact) / math.log(MAX_DISTANCE / max_exact)
                         * (nb - max_exact)).astype(I32)
    large = jnp.minimum(large, nb - 1)
    return ret + jnp.where(n < max_exact, n, large)


def _diff_bias(rel_bias, t):
    assert t % CHUNK == 0 and t >= MAX_DISTANCE
    n_rel = 3 * t
    vec = rel_bias[_t5_bucket(jnp.arange(n_rel, dtype=I32) - (2 * t - 1))].astype(F32).T
    toe = jnp.tile(vec, (1, t))[:, :t * (n_rel - 1)].reshape(N_HEADS, t, n_rel - 1)
    qpos = jnp.arange(t, dtype=I32)[:, None]
    kpos = jnp.arange(t, dtype=I32)[None, :]
    far = rel_bias[_t5_bucket(jnp.asarray(-2 * t, I32))].astype(F32)[:, None, None]
    b0 = (toe[:, :, 2 * t - 1:3 * t - 1] - far) * LOG2E
    b0 = jnp.where((kpos // CHUNK <= qpos // CHUNK)[None], b0, NEG_INF)
    b1 = (toe[:, :, t - 1:2 * t - 1] - far) * LOG2E
    return jnp.swapaxes(jnp.stack([b0, b1], axis=1), -1, -2)


def _tile_plan(counts, n_tiles):
    t = MOE_ROWS
    tiles = (counts + t - 1) // t
    ends = jnp.cumsum(tiles)
    n_active = ends[-1]
    tile_idx = jnp.minimum(jnp.arange(n_tiles, dtype=I32), n_active - 1)
    tile_bucket = jnp.minimum(jnp.sum((ends[None, :] <= tile_idx[:, None]).astype(I32), axis=1), N_PAIR_BUCKETS - 1)
    pa = jnp.asarray([p[0] for p in PAIRS], I32)
    pb = jnp.asarray([p[1] for p in PAIRS], I32)
    grp = tile_bucket // len(PAIRS)
    pair = tile_bucket % len(PAIRS)
    tile_a = grp * EXPERTS_PER_GROUP + pa[pair]
    tile_b = grp * EXPERTS_PER_GROUP + pb[pair]
    zrow = ((ends - 1) * t).astype(I32)
    zflag = (counts > 0).astype(I32)
    prev_bucket = jnp.concatenate([jnp.full((1,), -1, I32), tile_bucket[:-1]])
    first = (tile_bucket != prev_bucket).astype(I32)
    buf = (jnp.cumsum(first) - 1) % 2
    nxt = ends[tile_bucket]
    has_next = (nxt < n_active).astype(I32)
    nxt = jnp.minimum(nxt, n_tiles - 1)
    plan = jnp.stack([tile_idx, tile_a, tile_b, first, buf, tile_a[nxt], tile_b[nxt], has_next]).astype(I32)
    return zrow, zflag, plan, n_active.astype(I32)


def kernel(x, mem, rel_bias, norm_mix_g, w_in, b_forget, diff_q_norm_g, diff_k_norm_g, diff_lambda_q1, diff_lambda_k1, diff_lambda_q2, diff_lambda_k2, diff_subln_g, fox_q_norm_g, fox_k_norm_g, fox_out_norm_g, w_out, norm_cross_g, norm_mem_g, w_cq, w_ckv, cross_q_norm_g, cross_k_norm_g, w_co, norm_ffn_g, w_group_router, b_group_router, w_expert_router, b_expert_router, w_exp_gate, w_exp_up, w_exp_down):
    batch, seq, d = x.shape
    assert d == D_MODEL and norm_mix_g.shape[0] == 1 and mem.shape[1] == N_MEM
    n_tok = batch * seq
    assert seq % ATTN_TILE == 0 or seq < ATTN_TILE
    l = 0

    w_qk = jnp.concatenate([w_in[l, :, :1024], w_in[l, :, 1536:2560]], axis=1).astype(BF16)
    w_vt = jnp.concatenate([w_in[l, :, 1024:1536], w_in[l, :, 2560:3072]], axis=1).T.astype(BF16)
    wft = jnp.zeros((8, D_MODEL), F32).at[:N_HEADS].set(w_in[l, :, 3072:].T).astype(BF16)
    bf = jnp.zeros((8, 1), F32).at[:N_HEADS, 0].set(b_forget[l].astype(F32))
    ones = jnp.ones((HEAD_DIM,), F32)
    qg = jnp.stack([
        jnp.tile(diff_q_norm_g[l].astype(F32), 2) * (DIFF_QK_DIM ** -0.5 * LOG2E),
        jnp.tile(diff_k_norm_g[l].astype(F32), 2),
        fox_q_norm_g[l].astype(F32) * (HEAD_DIM ** -0.5 * LOG2E), fox_k_norm_g[l].astype(F32),
        ones, ones, ones, ones])
    lam = (jnp.exp(jnp.sum(diff_lambda_q1[l].astype(F32) * diff_lambda_k1[l].astype(F32)))
           - jnp.exp(jnp.sum(diff_lambda_q2[l].astype(F32) * diff_lambda_k2[l].astype(F32)))
           + LAM_INIT).reshape(1)
    t_attn = min(ATTN_TILE, seq)
    bias_tiles_t = _diff_bias(rel_bias, t_attn)
    wr = jnp.zeros((32, D_MODEL), F32)
    wr = wr.at[:N_GROUPS].set(w_group_router[l].T).at[N_GROUPS:N_GROUPS + N_EXPERTS].set(w_expert_router[l].T)
    wr_hi = wr.astype(BF16)
    wr_lo = (wr - wr_hi.astype(F32)).astype(BF16)
    rb = jnp.zeros((32, 1), F32)
    rb = rb.at[:N_GROUPS, 0].set(b_group_router[l]).at[N_GROUPS:N_GROUPS + N_EXPERTS, 0].set(b_expert_router[l])

    row = lambda v: v.astype(F32).reshape(1, -1)
    x2d = x.reshape(n_tok, D_MODEL)

    qk, vt, cum = _mix_proj(x2d, row(norm_mix_g[l]), w_qk, w_vt, wft, bf, qg, batch=batch, seq=seq)
    cum4 = cum[:, :N_HEADS].reshape(batch, N_HEADS, 1, seq)
    gain_t = lambda v: jnp.broadcast_to(v.astype(F32)[:, None], (HEAD_DIM, t_attn))
    mixed = _attn(qk, vt, bias_tiles_t, lam, cum4, gain_t(diff_subln_g[l]), gain_t(fox_out_norm_g[l]),
                  batch=batch, seq=seq)
    x1, qc = _out_q(x2d, mixed, w_out[l].astype(BF16), row(norm_cross_g[l]),
                    w_cq[l].astype(BF16), row(cross_q_norm_g[l]))

    kc, vc = _mem_kv(mem.reshape(batch * N_MEM, D_MODEL), row(norm_mem_g[l]), w_ckv[l].astype(BF16),
                     row(cross_k_norm_g[l]), batch=batch)
    xa, rt = _cross_router(qc, kc, vc, x1, w_co[l].astype(BF16), row(norm_ffn_g[l]), wr_hi, wr_lo, rb,
                           batch=batch, seq=seq)

    pos, cnt = _rank(rt)
    pos = pos.reshape(n_tok)
    counts = cnt[:N_PAIR_BUCKETS, 0].astype(I32)
    n_tiles = n_tok // MOE_ROWS + N_PAIR_BUCKETS
    zrow, zflag, plan, n_active = _tile_plan(counts, n_tiles)
    n_active = n_active.reshape(1)
    xs = _dispatch(pos, zrow, zflag, n_active, xa, n_tiles=n_tiles)
    ys = _moe(plan, n_active, xs, row(norm_ffn_g[l]),
              w_exp_gate[l].astype(F32), w_exp_up[l].astype(F32), w_exp_down[l].astype(F32))
    out = _combine(pos, ys, n_tok=n_tok)
    return out.reshape(batch, seq, D_MODEL)
```

```python
import functools
import math

import jax
import jax.numpy as jnp
from jax import lax
from jax.experimental import pallas as pl
from jax.experimental.pallas import tpu as pltpu

F32 = jnp.float32
BF16 = jnp.bfloat16
I32 = jnp.int32

D_MODEL = 1024
CHUNK = 64
N_MEM = 256
N_HEADS = 4
HEAD_DIM = 128
VT_ROWS = HEAD_DIM + 16
DIFF_QK_DIM = 64
CROSS_HEAD_DIM = 256
N_BUCKETS_T5 = 32
MAX_DISTANCE = 128
N_GROUPS = 4
EXPERTS_PER_GROUP = 4
N_EXPERTS = N_GROUPS * EXPERTS_PER_GROUP
D_EXPERT = 512
EPS = 1e-6
NEG_INF = -1e30
LAM_INIT = 0.8 - 0.6 * math.exp(-0.3 * 0)
LOG2E = math.log2(math.e)

PAIRS = ((0, 1), (0, 2), (0, 3), (1, 3), (1, 2), (3, 2))
N_PAIR_BUCKETS = N_GROUPS * len(PAIRS)
GROUP_W = N_HEADS * HEAD_DIM
ROUTE_ROWS = 32

LANES = 128
VMEM_LIMIT_CAP = 56 * 1024 * 1024

PROJ_ROWS = 1024
ATTN_TILE = 256
CROSS_ROWS = 1024
MOE_ROWS = 256
AUX_COLS = LANES
ROW_W = D_MODEL + AUX_COLS
DMA_ROWS = 2048


def _vmem_limit(nbytes):
    return int(min(max(nbytes * 5 // 4, 32 * 1024 * 1024), VMEM_LIMIT_CAP))


def _nt_dot(a, b):
    return lax.dot_general(a, b, (((1,), (1,)), ((), ())), preferred_element_type=F32)


def _rms(x, g):
    ms = jnp.mean(x * x, axis=-1, keepdims=True)
    return x * lax.rsqrt(ms + EPS) * g


def _mix_proj_kernel(x_ref, g_ref, w_ref, wvt_ref, wft_ref, bf_ref, qg_ref, o_ref, vt_ref, cum_ref, carry_ref,
                     *, tm):
    si = pl.program_id(1)
    h = _rms(x_ref[...], g_ref[...]).astype(BF16)
    lane = lax.broadcasted_iota(I32, (tm, HEAD_DIM), 1)
    lo = lane < DIFF_QK_DIM
    for kind in range(4):
        p = jnp.dot(h, w_ref[:, kind * GROUP_W:(kind + 1) * GROUP_W], preferred_element_type=F32)
        for hh in range(N_HEADS):
            ph = p[:, hh * HEAD_DIM:(hh + 1) * HEAD_DIM]
            if kind in (0, 1):
                sq = ph * ph
                s_lo = jnp.sum(jnp.where(lo, sq, 0.0), axis=-1, keepdims=True)
                s_hi = jnp.sum(jnp.where(lo, 0.0, sq), axis=-1, keepdims=True)
                ms = jnp.where(lo, s_lo, s_hi) * (1.0 / DIFF_QK_DIM)
                ph = ph * lax.rsqrt(ms + EPS) * qg_ref[kind:kind + 1, :]
            else:
                ph = _rms(ph, qg_ref[kind:kind + 1, :])
            o_ref[0, kind * N_HEADS + hh] = ph.astype(BF16)
    for kind in range(2):
        pt = _nt_dot(wvt_ref[kind * GROUP_W:(kind + 1) * GROUP_W, :], h)
        for hh in range(N_HEADS):
            vt_ref[0, kind * N_HEADS + hh, :HEAD_DIM, :] = pt[hh * HEAD_DIM:(hh + 1) * HEAD_DIM, :].astype(BF16)
            vt_ref[0, kind * N_HEADS + hh, HEAD_DIM:, :] = jnp.ones((VT_ROWS - HEAD_DIM, tm), BF16)

    z = _nt_dot(wft_ref[...], h) + bf_ref[...]
    logf = jnp.minimum(z, 0.0) - jnp.log(1.0 + jnp.exp(-jnp.abs(z)))
    lane8 = lax.broadcasted_iota(I32, (8, tm), 1)
    c = logf * LOG2E
    k = 1
    while k < tm:
        c = c + jnp.where(lane8 >= k, pltpu.roll(c, k, axis=1), 0.0)
        k *= 2

    @pl.when(si == 0)
    def _():
        carry_ref[...] = jnp.zeros_like(carry_ref)

    c = c + carry_ref[:, 0:1]
    cum_ref[0] = c
    carry_ref[...] = jnp.broadcast_to(c[:, tm - 1:tm], carry_ref.shape)


def _mix_proj(x2d, g, w_qk, w_vt, wft, bf, qg, *, batch, seq):
    tm = min(PROJ_ROWS, seq)
    ns = seq // tm
    est = 2 * (tm * D_MODEL * 4 + D_MODEL * 6 * GROUP_W * 2 + 24 * tm * HEAD_DIM * 2) + 8 * tm * D_MODEL
    return pl.pallas_call(
        functools.partial(_mix_proj_kernel, tm=tm),
        grid=(batch, ns),
        in_specs=[
            pl.BlockSpec((tm, D_MODEL), lambda b, s: (b * ns + s, 0)),
            pl.BlockSpec((1, D_MODEL), lambda b, s: (0, 0)),
            pl.BlockSpec((D_MODEL, 4 * GROUP_W), lambda b, s: (0, 0)),
            pl.BlockSpec((2 * GROUP_W, D_MODEL), lambda b, s: (0, 0)),
            pl.BlockSpec((8, D_MODEL), lambda b, s: (0, 0)),
            pl.BlockSpec((8, 1), lambda b, s: (0, 0)),
            pl.BlockSpec((8, HEAD_DIM), lambda b, s: (0, 0)),
        ],
        out_specs=[
            pl.BlockSpec((1, 16, tm, HEAD_DIM), lambda b, s: (b, 0, s, 0)),
            pl.BlockSpec((1, 8, VT_ROWS, tm), lambda b, s: (b, 0, 0, s)),
            pl.BlockSpec((1, 8, tm), lambda b, s: (b, 0, s)),
        ],
        out_shape=[
            jax.ShapeDtypeStruct((batch, 16, seq, HEAD_DIM), BF16),
            jax.ShapeDtypeStruct((batch, 8, VT_ROWS, seq), BF16),
            jax.ShapeDtypeStruct((batch, 8, seq), F32),
        ],
        scratch_shapes=[pltpu.VMEM((8, LANES), F32)],
        compiler_params=pltpu.CompilerParams(
            dimension_semantics=("arbitrary", "arbitrary"), vmem_limit_bytes=_vmem_limit(est)),
        name="mix_proj",
    )(x2d, g, w_qk, w_vt, wft, bf, qg)


def _init_softmax(m_sc, acc_sc):
    m_sc[...] = jnp.full(m_sc.shape, NEG_INF, F32)
    acc_sc[...] = jnp.zeros_like(acc_sc)


N_CHAINS = 3 * N_HEADS


def _attn_kernel(lam_ref, dq_ref, dk_ref, dvt_ref, fq_ref, fk_ref, fvt_ref, bias_ref, c_ref,
                 gd_ref, gf_ref, o_ref, m_sc, acc_sc, qm_sc, ccol_sc, s_sc, mt_sc, s2_sc, mt2_sc, *, t, seq):
    qi = pl.program_id(1)
    _init_softmax(m_sc, acc_sc)
    key = lax.broadcasted_iota(I32, (t, t), 0)
    qry = lax.broadcasted_iota(I32, (t, t), 1)

    lane = lax.broadcasted_iota(I32, (t, HEAD_DIM), 1)
    for hh in range(N_HEADS):
        q = dq_ref[0, hh]
        zero = jnp.zeros_like(q)
        qm_sc[2 * hh] = jnp.where(lane < DIFF_QK_DIM, q, zero)
        qm_sc[2 * hh + 1] = jnp.where(lane < DIFF_QK_DIM, zero, q)

    @pl.when(qi == 0)
    def _():
        for hh in range(N_HEADS):
            for j in range(seq // t):
                row = c_ref[0, hh, :, j * t:(j + 1) * t]
                ccol_sc[hh, j * t:(j + 1) * t, :] = jnp.sum(
                    jnp.where(key == qry, jnp.broadcast_to(row, (t, t)), 0.0), axis=1, keepdims=True)

    q_off = pl.multiple_of(qi * t, t)

    bufs = ((s_sc, mt_sc), (s2_sc, mt2_sc))

    def logits(j, where, par, slot):
        off = pl.multiple_of(j * t, t)
        s_buf, mt_buf = bufs[par]
        if slot < 2 * N_HEADS:
            hh = slot // 2
            s_t = _nt_dot(dk_ref[0, hh, pl.ds(off, t), :], qm_sc[slot])
            if where != "far":
                s_t = s_t + bias_ref[hh, 1 if where == "near" else 0]
            s_buf[slot] = s_t
            mt_buf[slot] = jnp.max(s_t, axis=0, keepdims=True)
        else:
            hh = slot - 2 * N_HEADS
            c_k = ccol_sc[hh, pl.ds(off, t), :]
            s_t = _nt_dot(fk_ref[0, hh, pl.ds(off, t), :], fq_ref[0, hh]) - c_k
            if where == "diag":
                s_t = jnp.where(key <= qry, s_t, NEG_INF)
            s_buf[slot] = s_t
            mt_buf[slot] = jnp.max(s_t, axis=0, keepdims=True) + c_ref[0, hh, :, pl.ds(q_off, t)]

    def update(j, par, slot):
        off = pl.multiple_of(j * t, t)
        s_buf, mt_buf = bufs[par]
        vt_ref = dvt_ref if slot < 2 * N_HEADS else fvt_ref
        hh = slot // 2 if slot < 2 * N_HEADS else slot - 2 * N_HEADS
        v_t = vt_ref[0, hh, :, pl.ds(off, t)]
        m_prev = m_sc[slot]
        m_new = jnp.maximum(m_prev, mt_buf[slot])
        alpha = jnp.exp2(m_prev - m_new)
        if slot < 2 * N_HEADS:
            p_t = jnp.exp2(s_buf[slot] - m_new)
        else:
            p_t = jnp.exp2(s_buf[slot] - (m_new - c_ref[0, hh, :, pl.ds(q_off, t)]))
        acc_sc[slot] = alpha * acc_sc[slot] + jnp.dot(v_t, p_t.astype(BF16), preferred_element_type=F32)
        m_sc[slot] = m_new

    def phase(upd=None, nxt=None):
        for slot in range(N_CHAINS):
            if nxt is not None:
                logits(*nxt, slot)
            if upd is not None:
                update(*upd, slot)

    n_far = jnp.maximum(qi - 1, 0)
    far_tile = lambda k: jnp.maximum(qi - 2 - k, 0)

    phase(nxt=(qi, "diag", 0))

    @pl.when(qi == 0)
    def _():
        phase(upd=(qi, 0))

    @pl.when(qi == 1)
    def _():
        phase(upd=(qi, 0), nxt=(qi - 1, "near", 1))
        phase(upd=(qi - 1, 1))

    @pl.when(qi >= 2)
    def _():
        phase(upd=(qi, 0), nxt=(qi - 1, "near", 1))
        phase(upd=(qi - 1, 1), nxt=(far_tile(0), "far", 0))

        def far_pair(i, carry):
            k = 2 * i
            phase(upd=(far_tile(k), 0), nxt=(far_tile(k + 1), "far", 1))

            @pl.when(k + 1 < n_far)
            def _():
                phase(upd=(far_tile(k + 1), 1), nxt=(far_tile(k + 2), "far", 0))
            return carry

        lax.fori_loop(0, (n_far + 1) // 2, far_pair, 0)

    def normalised(slot):
        return acc_sc[slot, :HEAD_DIM, :] / acc_sc[slot, HEAD_DIM:HEAD_DIM + 1, :]

    def rms_t(o_t, g_t):
        ms = jnp.mean(o_t * o_t, axis=0, keepdims=True)
        return o_t * lax.rsqrt(ms + EPS) * g_t

    for hh in range(N_HEADS):
        o_t = normalised(2 * hh) - lam_ref[0] * normalised(2 * hh + 1)
        o_ref[:, hh * HEAD_DIM:(hh + 1) * HEAD_DIM] = (
            rms_t(o_t, gd_ref[...]) * (1.0 - LAM_INIT)).T.astype(BF16)
    for hh in range(N_HEADS):
        o_t = normalised(2 * N_HEADS + hh)
        o_ref[:, (N_HEADS + hh) * HEAD_DIM:(N_HEADS + hh + 1) * HEAD_DIM] = rms_t(o_t, gf_ref[...]).T.astype(BF16)


def _attn(qk, vt, bias_tiles_t, lam, cum4, subln_g, fox_g, *, batch, seq):
    t = min(ATTN_TILE, seq)
    nq = seq // t
    head_blk = N_HEADS * seq * HEAD_DIM * 2
    est = (2 * (2 * N_HEADS * t * HEAD_DIM * 2 + 4 * head_blk + N_HEADS * 2 * t * t * 4 + N_HEADS * 8 * seq * 4
                + t * 2 * N_HEADS * HEAD_DIM * 2)
           + N_CHAINS * (HEAD_DIM + 16) * t * 4 + 2 * N_HEADS * t * HEAD_DIM * 2 + N_HEADS * seq * LANES * 4
           + (2 * N_CHAINS + 36) * t * t * 4)
    qspec = lambda blk: pl.BlockSpec((1, N_HEADS, t, HEAD_DIM), lambda b, i: (b, blk, i, 0))
    kspec = lambda blk: pl.BlockSpec((1, N_HEADS, seq, HEAD_DIM), lambda b, i: (b, blk, 0, 0))
    vspec = lambda blk: pl.BlockSpec((1, N_HEADS, VT_ROWS, seq), lambda b, i: (b, blk, 0, 0))
    return pl.pallas_call(
        functools.partial(_attn_kernel, t=t, seq=seq),
        grid=(batch, nq),
        in_specs=[
            pl.BlockSpec(memory_space=pltpu.SMEM),
            qspec(0), kspec(1), vspec(0),
            qspec(2), kspec(3), vspec(1),
            pl.BlockSpec((N_HEADS, 2, t, t), lambda b, i: (0, 0, 0, 0)),
            pl.BlockSpec((1, N_HEADS, 1, seq), lambda b, i: (b, 0, 0, 0)),
            pl.BlockSpec((HEAD_DIM, t), lambda b, i: (0, 0)),
            pl.BlockSpec((HEAD_DIM, t), lambda b, i: (0, 0)),
        ],
        out_specs=pl.BlockSpec((t, 2 * N_HEADS * HEAD_DIM), lambda b, i: (b * nq + i, 0)),
        out_shape=jax.ShapeDtypeStruct((batch * seq, 2 * N_HEADS * HEAD_DIM), BF16),
        scratch_shapes=[pltpu.VMEM((N_CHAINS, 1, t), F32),
                        pltpu.VMEM((N_CHAINS, VT_ROWS, t), F32),
                        pltpu.VMEM((2 * N_HEADS, t, HEAD_DIM), BF16),
                        pltpu.VMEM((N_HEADS, seq, 1), F32),
                        pltpu.VMEM((N_CHAINS, t, t), F32), pltpu.VMEM((N_CHAINS, 1, t), F32),
                        pltpu.VMEM((N_CHAINS, t, t), F32), pltpu.VMEM((N_CHAINS, 1, t), F32)],
        compiler_params=pltpu.CompilerParams(
            dimension_semantics=("arbitrary", "arbitrary"), vmem_limit_bytes=_vmem_limit(est)),
        name="attn",
    )(lam, qk, qk, vt, qk, qk, vt, bias_tiles_t, cum4, subln_g, fox_g)


def _out_q_kernel(x_ref, m_ref, wo_ref, gc_ref, wq_ref, qg_ref, x1_ref, qc_ref):
    x1 = x_ref[...] + jnp.dot(m_ref[...], wo_ref[...], preferred_element_type=F32)
    x1_ref[...] = x1
    hc = _rms(x1, gc_ref[...]).astype(BF16)
    q = jnp.dot(hc, wq_ref[...], preferred_element_type=F32)
    for hh in range(N_HEADS):
        sl = slice(hh * CROSS_HEAD_DIM, (hh + 1) * CROSS_HEAD_DIM)
        qh = _rms(q[:, sl], qg_ref[...]) * (CROSS_HEAD_DIM ** -0.5)
        qc_ref[:, sl] = qh.astype(BF16)


def _out_q(x2d, mixed, w_out, g_cross, w_cq, q_g):
    n = x2d.shape[0]
    tm = min(PROJ_ROWS, n)
    est = 2 * (tm * D_MODEL * 4 * 2 + tm * D_MODEL * 2 + 2 * D_MODEL * D_MODEL * 2 + tm * D_MODEL * 2) + 6 * tm * D_MODEL * 4
    return pl.pallas_call(
        _out_q_kernel,
        grid=(n // tm,),
        in_specs=[
            pl.BlockSpec((tm, D_MODEL), lambda i: (i, 0)),
            pl.BlockSpec((tm, D_MODEL), lambda i: (i, 0)),
            pl.BlockSpec((D_MODEL, D_MODEL), lambda i: (0, 0)),
            pl.BlockSpec((1, D_MODEL), lambda i: (0, 0)),
            pl.BlockSpec((D_MODEL, D_MODEL), lambda i: (0, 0)),
            pl.BlockSpec((1, CROSS_HEAD_DIM), lambda i: (0, 0)),
        ],
        out_specs=[pl.BlockSpec((tm, D_MODEL), lambda i: (i, 0)),
                   pl.BlockSpec((tm, D_MODEL), lambda i: (i, 0))],
        out_shape=[jax.ShapeDtypeStruct((n, D_MODEL), F32), jax.ShapeDtypeStruct((n, D_MODEL), BF16)],
        compiler_params=pltpu.CompilerParams(
            dimension_semantics=("arbitrary",), vmem_limit_bytes=_vmem_limit(est)),
        name="out_q",
    )(x2d, mixed, w_out, g_cross, w_cq, q_g)


def _mem_kv_kernel(mem_ref, gm_ref, w_ref, kg_ref, k_ref, v_ref):
    mn = _rms(mem_ref[...], gm_ref[...]).astype(BF16)
    kv = jnp.dot(mn, w_ref[...], preferred_element_type=F32)
    for hh in range(N_HEADS):
        sl = slice(hh * CROSS_HEAD_DIM, (hh + 1) * CROSS_HEAD_DIM)
        k_ref[0, :, sl] = _rms(kv[:, sl], kg_ref[...]).astype(BF16)
    v_ref[0] = kv[:, D_MODEL:].astype(BF16)


def _mem_kv(mem2d, g_mem, w_ckv, k_g, *, batch):
    est = 2 * (N_MEM * D_MODEL * 4 + D_MODEL * 2 * D_MODEL * 2 + 2 * N_MEM * D_MODEL * 2) + 4 * N_MEM * 2 * D_MODEL * 4
    return pl.pallas_call(
        _mem_kv_kernel,
        grid=(batch,),
        in_specs=[
            pl.BlockSpec((N_MEM, D_MODEL), lambda b: (b, 0)),
            pl.BlockSpec((1, D_MODEL), lambda b: (0, 0)),
            pl.BlockSpec((D_MODEL, 2 * D_MODEL), lambda b: (0, 0)),
            pl.BlockSpec((1, CROSS_HEAD_DIM), lambda b: (0, 0)),
        ],
        out_specs=[pl.BlockSpec((1, N_MEM, D_MODEL), lambda b: (b, 0, 0)),
                   pl.BlockSpec((1, N_MEM, D_MODEL), lambda b: (b, 0, 0))],
        out_shape=[jax.ShapeDtypeStruct((batch, N_MEM, D_MODEL), BF16),
                   jax.ShapeDtypeStruct((batch, N_MEM, D_MODEL), BF16)],
        compiler_params=pltpu.CompilerParams(
            dimension_semantics=("arbitrary",), vmem_limit_bytes=_vmem_limit(est)),
        name="mem_kv",
    )(mem2d, g_mem, w_ckv, k_g)


def _route(logits_t):
    gl = logits_t[0:N_GROUPS]
    gmax = jnp.max(gl, axis=0, keepdims=True)
    eg = jnp.exp(gl - gmax)
    p_group = eg / jnp.sum(eg, axis=0, keepdims=True)
    p_g = jnp.max(p_group, axis=0, keepdims=True)
    g_idx = jnp.full_like(p_g, float(N_GROUPS))
    for g in reversed(range(N_GROUPS)):
        g_idx = jnp.where(p_group[g:g + 1] == p_g, float(g), g_idx)

    sel = []
    for j in range(EXPERTS_PER_GROUP):
        acc = jnp.zeros_like(p_g)
        for g in range(N_GROUPS):
            r = N_GROUPS + g * EXPERTS_PER_GROUP + j
            acc = jnp.where(g_idx == float(g), logits_t[r:r + 1], acc)
        sel.append(acc)
    smax = functools.reduce(jnp.maximum, sel)
    es = [jnp.exp(s - smax) for s in sel]
    den = functools.reduce(jnp.add, es)
    p_in = [e / den for e in es]

    v0 = functools.reduce(jnp.maximum, p_in)
    i0 = jnp.full_like(v0, float(EXPERTS_PER_GROUP))
    for j in reversed(range(EXPERTS_PER_GROUP)):
        i0 = jnp.where(p_in[j] == v0, float(j), i0)
    rest = [jnp.where(i0 == float(j), -1.0, p_in[j]) for j in range(EXPERTS_PER_GROUP)]
    v1 = functools.reduce(jnp.maximum, rest)
    i1 = jnp.full_like(v1, float(EXPERTS_PER_GROUP))
    for j in reversed(range(EXPERTS_PER_GROUP)):
        i1 = jnp.where(rest[j] == v1, float(j), i1)

    tot = v0 + v1
    w0 = p_g * (v0 / tot)
    w1 = p_g * (v1 / tot)
    first = i0 < i1
    a = jnp.where(first, i0, i1)
    b = jnp.where(first, i1, i0)
    w_lo = jnp.where(first, w0, w1)
    w_hi = jnp.where(first, w1, w0)
    pair = jnp.where(a == 0.0, b - 1.0, jnp.where(a == 1.0, jnp.where(b == 3.0, 3.0, 4.0), 5.0))
    swap = a == 2.0
    wa = jnp.where(swap, w_hi, w_lo)
    wb = jnp.where(swap, w_lo, w_hi)
    bucket = g_idx * float(len(PAIRS)) + pair
    return bucket, wa, wb


def _cross_router_kernel(q_ref, k_ref, v_ref, x1_ref, wco_ref, gf_ref, whi_ref, wlo_ref, rb_ref,
                         xa_ref, rt_ref, *, tm):
    q = q_ref[...]
    k = k_ref[0]
    v = v_ref[0]
    outs = []
    for hh in range(N_HEADS):
        sl = slice(hh * CROSS_HEAD_DIM, (hh + 1) * CROSS_HEAD_DIM)
        s = _nt_dot(q[:, sl], k[:, sl])
        p = jnp.exp(s - jnp.max(s, axis=-1, keepdims=True))
        l = jnp.sum(p, axis=-1, keepdims=True)
        o = jnp.dot(p.astype(BF16), v[:, sl], preferred_element_type=F32) / l
        outs.append(o.astype(BF16))
    o = jnp.concatenate(outs, axis=1)
    x2 = x1_ref[...] + jnp.dot(o, wco_ref[...], preferred_element_type=F32)
    xa_ref[:, :D_MODEL] = x2

    h3 = _rms(x2, gf_ref[...])
    hi = h3.astype(BF16)
    lo = (h3 - hi.astype(F32)).astype(BF16)
    lt = _nt_dot(whi_ref[...], hi) + (_nt_dot(whi_ref[...], lo) + _nt_dot(wlo_ref[...], hi))
    lt = lt + rb_ref[...]
    bucket, wa, wb = _route(lt)

    sub = lax.broadcasted_iota(I32, (8, tm), 0)
    rows = jnp.where(sub == 0, bucket, jnp.where(sub == 1, wa, jnp.where(sub == 2, wb, 0.0)))
    rt_ref[0] = rows
    sub_a = lax.broadcasted_iota(I32, (AUX_COLS, tm), 0)
    aux_t = jnp.where(sub_a == 0, bucket, jnp.where(sub_a == 1, wa, jnp.where(sub_a == 2, wb, 0.0)))
    xa_ref[:, D_MODEL:] = aux_t.T


def _cross_router(qc, kc, vc, x1, w_co, g_ffn, wr_hi, wr_lo, rb, *, batch, seq):
    tm = min(CROSS_ROWS, seq)
    ns = seq // tm
    n = batch * seq
    est = 2 * (tm * D_MODEL * 2 + 2 * N_MEM * D_MODEL * 2 + tm * D_MODEL * 4 + D_MODEL * D_MODEL * 2
               + tm * ROW_W * 4) + 10 * tm * D_MODEL * 4
    return pl.pallas_call(
        functools.partial(_cross_router_kernel, tm=tm),
        grid=(batch, ns),
        in_specs=[
            pl.BlockSpec((tm, D_MODEL), lambda b, s: (b * ns + s, 0)),
            pl.BlockSpec((1, N_MEM, D_MODEL), lambda b, s: (b, 0, 0)),
            pl.BlockSpec((1, N_MEM, D_MODEL), lambda b, s: (b, 0, 0)),
            pl.BlockSpec((tm, D_MODEL), lambda b, s: (b * ns + s, 0)),
            pl.BlockSpec((D_MODEL, D_MODEL), lambda b, s: (0, 0)),
            pl.BlockSpec((1, D_MODEL), lambda b, s: (0, 0)),
            pl.BlockSpec((ROUTE_ROWS, D_MODEL), lambda b, s: (0, 0)),
            pl.BlockSpec((ROUTE_ROWS, D_MODEL), lambda b, s: (0, 0)),
            pl.BlockSpec((ROUTE_ROWS, 1), lambda b, s: (0, 0)),
        ],
        out_specs=[pl.BlockSpec((tm, ROW_W), lambda b, s: (b * ns + s, 0)),
                   pl.BlockSpec((1, 8, tm), lambda b, s: (b * ns + s, 0, 0))],
        out_shape=[jax.ShapeDtypeStruct((n, ROW_W), F32),
                   jax.ShapeDtypeStruct((n // tm, 8, tm), F32)],
        compiler_params=pltpu.CompilerParams(
            dimension_semantics=("arbitrary", "arbitrary"), vmem_limit_bytes=_vmem_limit(est)),
        name="cross_router",
    )(qc, kc, vc, x1, w_co, g_ffn, wr_hi, wr_lo, rb)


def _rank_kernel(rt_ref, pos_ref, cnt_ref, carry_sc, off_sc, *, tm, nt):
    i = pl.program_id(0)
    sub = lax.broadcasted_iota(I32, (ROUTE_ROWS, tm), 0).astype(F32)

    @pl.when(i == 0)
    def _():
        n_hit = jnp.zeros((ROUTE_ROWS, 1), F32)
        for j in range(nt):
            n_hit = n_hit + jnp.sum(jnp.where(sub == rt_ref[j, 0:1, :], 1.0, 0.0), axis=1, keepdims=True)
        counts = jnp.broadcast_to(n_hit, cnt_ref.shape)
        cnt_ref[...] = counts
        tiles = jnp.floor((counts + (MOE_ROWS - 1)) * (1.0 / MOE_ROWS))
        r = lax.broadcasted_iota(I32, (ROUTE_ROWS, ROUTE_ROWS), 0)
        c = lax.broadcasted_iota(I32, (ROUTE_ROWS, ROUTE_ROWS), 1)
        before = jnp.where(c < r, 1.0, 0.0).astype(BF16)
        off_sc[...] = jnp.dot(before, tiles.astype(BF16), preferred_element_type=F32) * MOE_ROWS
        carry_sc[...] = jnp.zeros_like(carry_sc)

    @pl.when(i >= 1)
    def _():
        hit = sub == rt_ref[i - 1, 0:1, :]
        r = lax.broadcasted_iota(I32, (tm, tm), 0)
        c = lax.broadcasted_iota(I32, (tm, tm), 1)
        upper = jnp.where(r <= c, 1.0, 0.0).astype(BF16)
        cum = jnp.dot(jnp.where(hit, 1.0, 0.0).astype(BF16), upper, preferred_element_type=F32)
        base = carry_sc[:, 0:1] + off_sc[:, 0:1]
        pos = jnp.sum(jnp.where(hit, cum - 1.0 + base, 0.0), axis=0, keepdims=True)
        pos_ref[0] = pos.astype(I32)
        carry_sc[...] = carry_sc[...] + jnp.broadcast_to(cum[:, tm - 1:tm], carry_sc.shape)


def _rank(rt):
    nt, _, tm = rt.shape
    return pl.pallas_call(
        functools.partial(_rank_kernel, tm=tm, nt=nt),
        grid=(nt + 1,),
        in_specs=[pl.BlockSpec((nt, 8, tm), lambda i: (0, 0, 0))],
        out_specs=[pl.BlockSpec((1, 1, tm), lambda i: (jnp.maximum(i - 1, 0), 0, 0)),
                   pl.BlockSpec((ROUTE_ROWS, LANES), lambda i: (0, 0))],
        out_shape=[jax.ShapeDtypeStruct((nt, 1, tm), I32), jax.ShapeDtypeStruct((ROUTE_ROWS, LANES), F32)],
        scratch_shapes=[pltpu.VMEM((ROUTE_ROWS, LANES), F32), pltpu.VMEM((ROUTE_ROWS, LANES), F32)],
        compiler_params=pltpu.CompilerParams(dimension_semantics=("arbitrary",)),
        name="rank",
    )(rt)


def _dispatch_kernel(pos_ref, zrow_ref, zflag_ref, nact_ref, xa_ref, xs_ref, zbuf, zsem, sem, *,
                     n_tiles, rows):
    @pl.when(pl.program_id(0) == 0)
    def _():
        _zero_partial_tiles(zrow_ref, zflag_ref, nact_ref, xs_ref, zbuf, zsem, n_tiles=n_tiles)

    base = pl.program_id(0) * rows
    for r in range(rows):
        pltpu.make_async_copy(xa_ref.at[pl.ds(r, 1)], xs_ref.at[pl.ds(pos_ref[base + r], 1)], sem).start(
            priority=r % 2)
    pltpu.make_async_copy(xa_ref, xs_ref.at[pl.ds(0, rows)], sem).wait()


def _zero_partial_tiles(zrow_ref, zflag_ref, nact_ref, xs_ref, zbuf, zsem, *, n_tiles):
    zbuf[...] = jnp.zeros_like(zbuf)

    def zero_copy(row):
        return pltpu.make_async_copy(zbuf, xs_ref.at[pl.ds(pl.multiple_of(row, MOE_ROWS), MOE_ROWS)], zsem)

    def bucket_tiles(op):
        def body(b, carry):
            @pl.when(zflag_ref[b] > 0)
            def _():
                op(zero_copy(zrow_ref[b]))
            return carry
        lax.fori_loop(0, N_PAIR_BUCKETS, body, 0)

    def unused_tiles(op):
        def body(i, carry):
            op(zero_copy(i * MOE_ROWS))
            return carry
        lax.fori_loop(nact_ref[0], n_tiles, body, 0)

    bucket_tiles(lambda cp: cp.start())
    unused_tiles(lambda cp: cp.start())
    bucket_tiles(lambda cp: cp.wait())
    unused_tiles(lambda cp: cp.wait())


def _dispatch(pos, zrow, zflag, n_active, xa, *, n_tiles):
    n_tok = xa.shape[0]
    n_sorted = n_tiles * MOE_ROWS
    rows = min(DMA_ROWS, n_tok)
    return pl.pallas_call(
        functools.partial(_dispatch_kernel, n_tiles=n_tiles, rows=rows),
        grid_spec=pltpu.PrefetchScalarGridSpec(
            num_scalar_prefetch=4,
            grid=(n_tok // rows,),
            in_specs=[pl.BlockSpec((rows, ROW_W), lambda i, *_: (i, 0))],
            out_specs=pl.BlockSpec(memory_space=pl.ANY),
            scratch_shapes=[pltpu.VMEM((MOE_ROWS, ROW_W), F32), pltpu.SemaphoreType.DMA(()),
                            pltpu.SemaphoreType.DMA(())],
        ),
        out_shape=jax.ShapeDtypeStruct((n_sorted, ROW_W), F32),
        compiler_params=pltpu.CompilerParams(
            dimension_semantics=("arbitrary",), has_side_effects=True,
            vmem_limit_bytes=_vmem_limit((2 * rows + MOE_ROWS) * ROW_W * 4)),
        name="dispatch",
    )(pos, zrow, zflag, n_active, xa)


def _moe_kernel(plan_ref, nact_ref, xs_ref, gf_ref, wg_hbm, wu_hbm, wd_hbm, ys_ref, wgu_sc, wd_sc, wsem):
    i = pl.program_id(0)

    def weight_copies(ea, eb, buf):
        cps = []
        for k, (src, e) in enumerate(((wg_hbm, ea), (wu_hbm, ea), (wg_hbm, eb), (wu_hbm, eb))):
            cps.append(pltpu.make_async_copy(src.at[e], wgu_sc.at[buf, k], wsem.at[buf]))
        for k, e in enumerate((ea, eb)):
            cps.append(pltpu.make_async_copy(wd_hbm.at[e], wd_sc.at[buf, k], wsem.at[buf]))
        return cps

    @pl.when(i < nact_ref[0])
    def _():
        buf = plan_ref[4, i]

        @pl.when(plan_ref[3, i] == 1)
        def _():
            @pl.when(i == 0)
            def _():
                for cp in weight_copies(plan_ref[1, i], plan_ref[2, i], buf):
                    cp.start()
            for cp in weight_copies(plan_ref[1, i], plan_ref[2, i], buf):
                cp.wait()

            @pl.when(plan_ref[7, i] == 1)
            def _():
                for cp in weight_copies(plan_ref[5, i], plan_ref[6, i], 1 - buf):
                    cp.start()

        x2 = xs_ref[:, :D_MODEL]
        aux = xs_ref[:, D_MODEL:]
        wa = aux[:, 1:2]
        wb = aux[:, 2:3]
        h = _rms(x2, gf_ref[...]).astype(BF16)

        def mlp(slot):
            g = jnp.dot(h, wgu_sc[buf, 2 * slot].astype(BF16), preferred_element_type=F32)
            u = jnp.dot(h, wgu_sc[buf, 2 * slot + 1].astype(BF16), preferred_element_type=F32)
            act = (g / (1.0 + jnp.exp(-g))) * u
            return jnp.dot(act.astype(BF16), wd_sc[buf, slot].astype(BF16), preferred_element_type=F32)

        ys_ref[...] = x2 + (wa * mlp(0) + wb * mlp(1))

    @pl.when(i >= nact_ref[0])
    def _():
        ys_ref[...] = jnp.zeros_like(ys_ref)


def _moe(plan, n_active, xs, g_ffn, w_gate, w_up, w_down):
    n_tiles = xs.shape[0] // MOE_ROWS
    t = MOE_ROWS
    est = (2 * (t * ROW_W * 4 + t * D_MODEL * 4) + 2 * 6 * D_MODEL * D_EXPERT * 4 + 4 * D_MODEL * D_EXPERT * 4
           + 10 * t * D_MODEL * 4)
    return pl.pallas_call(
        _moe_kernel,
        grid_spec=pltpu.PrefetchScalarGridSpec(
            num_scalar_prefetch=2,
            grid=(n_tiles,),
            in_specs=[
                pl.BlockSpec((t, ROW_W), lambda i, plan, na: (plan[0, i], 0)),
                pl.BlockSpec((1, D_MODEL), lambda i, plan, na: (0, 0)),
                pl.BlockSpec(memory_space=pl.ANY),
                pl.BlockSpec(memory_space=pl.ANY),
                pl.BlockSpec(memory_space=pl.ANY),
            ],
            out_specs=pl.BlockSpec((t, D_MODEL), lambda i, plan, na: (i, 0)),
            scratch_shapes=[pltpu.VMEM((2, 4, D_MODEL, D_EXPERT), F32), pltpu.VMEM((2, 2, D_EXPERT, D_MODEL), F32),
                            pltpu.SemaphoreType.DMA((2,))],
        ),
        out_shape=jax.ShapeDtypeStruct((n_tiles * t, D_MODEL), F32),
        compiler_params=pltpu.CompilerParams(
            dimension_semantics=("arbitrary",), vmem_limit_bytes=_vmem_limit(est)),
        name="moe",
    )(plan, n_active, xs, g_ffn, w_gate, w_up, w_down)


def _combine_kernel(pos_ref, ys_ref, out_ref, sem, *, rows):
    base = pl.program_id(0) * rows
    for r in range(rows):
        pltpu.make_async_copy(ys_ref.at[pl.ds(pos_ref[base + r], 1)], out_ref.at[pl.ds(r, 1)], sem).start(
            priority=r % 2)
    pltpu.make_async_copy(ys_ref.at[pl.ds(0, rows)], out_ref, sem).wait()


def _combine(pos, ys, *, n_tok):
    rows = min(DMA_ROWS, n_tok)
    return pl.pallas_call(
        functools.partial(_combine_kernel, rows=rows),
        grid_spec=pltpu.PrefetchScalarGridSpec(
            num_scalar_prefetch=1,
            grid=(n_tok // rows,),
            in_specs=[pl.BlockSpec(memory_space=pl.ANY)],
            out_specs=pl.BlockSpec((rows, D_MODEL), lambda i, *_: (i, 0)),
            scratch_shapes=[pltpu.SemaphoreType.DMA(())],
        ),
        out_shape=jax.ShapeDtypeStruct((n_tok, D_MODEL), F32),
        compiler_params=pltpu.CompilerParams(
            dimension_semantics=("arbitrary",), has_side_effects=True,
            vmem_limit_bytes=_vmem_limit(2 * rows * D_MODEL * 4)),
        name="combine",
    )(pos, ys)


def _t5_bucket(rel):
    nb = N_BUCKETS_T5 // 2
    max_exact = nb // 2
    ret = (rel > 0).astype(I32) * nb
    n = jnp.abs(rel)
    nf = jnp.maximum(n, 1).astype(F32)
    large = max_exact + (jnp.log(nf / max_exact) / math.log(MAX_DISTANCE / max_exact)
                         * (nb - max_exact)).astype(I32)
    large = jnp.minimum(large, nb - 1)
    return ret + jnp.where(n < max_exact, n, large)


def _diff_bias(rel_bias, t):
    assert t % CHUNK == 0 and t >= MAX_DISTANCE
    n_rel = 3 * t
    vec = rel_bias[_t5_bucket(jnp.arange(n_rel, dtype=I32) - (2 * t - 1))].astype(F32).T
    toe = jnp.tile(vec, (1, t))[:, :t * (n_rel - 1)].reshape(N_HEADS, t, n_rel - 1)
    qpos = jnp.arange(t, dtype=I32)[:, None]
    kpos = jnp.arange(t, dtype=I32)[None, :]
    far = rel_bias[_t5_bucket(jnp.asarray(-2 * t, I32))].astype(F32)[:, None, None]
    b0 = (toe[:, :, 2 * t - 1:3 * t - 1] - far) * LOG2E
    b0 = jnp.where((kpos // CHUNK <= qpos // CHUNK)[None], b0, NEG_INF)
    b1 = (toe[:, :, t - 1:2 * t - 1] - far) * LOG2E
    return jnp.swapaxes(jnp.stack([b0, b1], axis=1), -1, -2)


def _tile_plan(counts, n_tiles):
    t = MOE_ROWS
    tiles = (counts + t - 1) // t
    ends = jnp.cumsum(tiles)
    n_active = ends[-1]
    tile_idx = jnp.minimum(jnp.arange(n_tiles, dtype=I32), n_active - 1)
    tile_bucket = jnp.minimum(jnp.sum((ends[None, :] <= tile_idx[:, None]).astype(I32), axis=1), N_PAIR_BUCKETS - 1)
    pa = jnp.asarray([p[0] for p in PAIRS], I32)
    pb = jnp.asarray([p[1] for p in PAIRS], I32)
    grp = tile_bucket // len(PAIRS)
    pair = tile_bucket % len(PAIRS)
    tile_a = grp * EXPERTS_PER_GROUP + pa[pair]
    tile_b = grp * EXPERTS_PER_GROUP + pb[pair]
    zrow = ((ends - 1) * t).astype(I32)
    zflag = (counts > 0).astype(I32)
    prev_bucket = jnp.concatenate([jnp.full((1,), -1, I32), tile_bucket[:-1]])
    first = (tile_bucket != prev_bucket).astype(I32)
    buf = (jnp.cumsum(first) - 1) % 2
    nxt = ends[tile_bucket]
    has_next = (nxt < n_active).astype(I32)
    nxt = jnp.minimum(nxt, n_tiles - 1)
    plan = jnp.stack([tile_idx, tile_a, tile_b, first, buf, tile_a[nxt], tile_b[nxt], has_next]).astype(I32)
    return zrow, zflag, plan, n_active.astype(I32)


def kernel(x, mem, rel_bias, norm_mix_g, w_in, b_forget, diff_q_norm_g, diff_k_norm_g, diff_lambda_q1, diff_lambda_k1, diff_lambda_q2, diff_lambda_k2, diff_subln_g, fox_q_norm_g, fox_k_norm_g, fox_out_norm_g, w_out, norm_cross_g, norm_mem_g, w_cq, w_ckv, cross_q_norm_g, cross_k_norm_g, w_co, norm_ffn_g, w_group_router, b_group_router, w_expert_router, b_expert_router, w_exp_gate, w_exp_up, w_exp_down):
    batch, seq, d = x.shape
    assert d == D_MODEL and norm_mix_g.shape[0] == 1 and mem.shape[1] == N_MEM
    n_tok = batch * seq
    assert seq % ATTN_TILE == 0 or seq < ATTN_TILE
    l = 0

    gw = GROUP_W
    w_qk = jnp.concatenate([w_in[l, :, :2 * gw], w_in[l, :, 3 * gw:5 * gw]], axis=1).astype(BF16)
    w_vt = jnp.concatenate([w_in[l, :, 2 * gw:3 * gw], w_in[l, :, 5 * gw:6 * gw]], axis=1).T.astype(BF16)
    wft = jnp.zeros((8, D_MODEL), F32).at[:N_HEADS].set(w_in[l, :, 6 * gw:].T).astype(BF16)
    bf = jnp.zeros((8, 1), F32).at[:N_HEADS, 0].set(b_forget[l].astype(F32))
    ones = jnp.ones((HEAD_DIM,), F32)
    qg = jnp.stack([
        jnp.tile(diff_q_norm_g[l].astype(F32), 2) * (DIFF_QK_DIM ** -0.5 * LOG2E),
        jnp.tile(diff_k_norm_g[l].astype(F32), 2),
        fox_q_norm_g[l].astype(F32) * (HEAD_DIM ** -0.5 * LOG2E), fox_k_norm_g[l].astype(F32),
        ones, ones, ones, ones])
    lam = (jnp.exp(jnp.sum(diff_lambda_q1[l].astype(F32) * diff_lambda_k1[l].astype(F32)))
           - jnp.exp(jnp.sum(diff_lambda_q2[l].astype(F32) * diff_lambda_k2[l].astype(F32)))
           + LAM_INIT).reshape(1)
    t_attn = min(ATTN_TILE, seq)
    bias_tiles_t = _diff_bias(rel_bias, t_attn)
    wr = jnp.zeros((ROUTE_ROWS, D_MODEL), F32)
    wr = wr.at[:N_GROUPS].set(w_group_router[l].T).at[N_GROUPS:N_GROUPS + N_EXPERTS].set(w_expert_router[l].T)
    wr_hi = wr.astype(BF16)
    wr_lo = (wr - wr_hi.astype(F32)).astype(BF16)
    rb = jnp.zeros((ROUTE_ROWS, 1), F32)
    rb = rb.at[:N_GROUPS, 0].set(b_group_router[l]).at[N_GROUPS:N_GROUPS + N_EXPERTS, 0].set(b_expert_router[l])

    row = lambda v: v.astype(F32).reshape(1, -1)
    x2d = x.reshape(n_tok, D_MODEL)

    qk, vt, cum = _mix_proj(x2d, row(norm_mix_g[l]), w_qk, w_vt, wft, bf, qg, batch=batch, seq=seq)
    cum4 = cum[:, :N_HEADS].reshape(batch, N_HEADS, 1, seq)
    gain_t = lambda v: jnp.broadcast_to(v.astype(F32)[:, None], (HEAD_DIM, t_attn))
    mixed = _attn(qk, vt, bias_tiles_t, lam, cum4, gain_t(diff_subln_g[l]), gain_t(fox_out_norm_g[l]),
                  batch=batch, seq=seq)
    x1, qc = _out_q(x2d, mixed, w_out[l].astype(BF16), row(norm_cross_g[l]),
                    w_cq[l].astype(BF16), row(cross_q_norm_g[l]))

    kc, vc = _mem_kv(mem.reshape(batch * N_MEM, D_MODEL), row(norm_mem_g[l]), w_ckv[l].astype(BF16),
                     row(cross_k_norm_g[l]), batch=batch)
    xa, rt = _cross_router(qc, kc, vc, x1, w_co[l].astype(BF16), row(norm_ffn_g[l]), wr_hi, wr_lo, rb,
                           batch=batch, seq=seq)

    pos, cnt = _rank(rt)
    pos = pos.reshape(n_tok)
    counts = cnt[:N_PAIR_BUCKETS, 0].astype(I32)
    n_tiles = n_tok // MOE_ROWS + N_PAIR_BUCKETS
    zrow, zflag, plan, n_active = _tile_plan(counts, n_tiles)
    n_active = n_active.reshape(1)
    xs = _dispatch(pos, zrow, zflag, n_active, xa, n_tiles=n_tiles)
    ys = _moe(plan, n_active, xs, row(norm_ffn_g[l]),
              w_exp_gate[l].astype(F32), w_exp_up[l].astype(F32), w_exp_down[l].astype(F32))
    out = _combine(pos, ys, n_tok=n_tok)
    return out.reshape(batch, seq, D_MODEL)
```

```python
import functools
import math

import jax
import jax.numpy as jnp
from jax import lax
from jax.experimental import pallas as pl
from jax.experimental.pallas import tpu as pltpu

F32 = jnp.float32
BF16 = jnp.bfloat16
I32 = jnp.int32

D_MODEL = 1024
CHUNK = 64
N_MEM = 256
N_HEADS = 4
HEAD_DIM = 128
VT_ROWS = HEAD_DIM + 16
DIFF_QK_DIM = 64
CROSS_HEAD_DIM = 256
N_BUCKETS_T5 = 32
MAX_DISTANCE = 128
N_GROUPS = 4
EXPERTS_PER_GROUP = 4
N_EXPERTS = N_GROUPS * EXPERTS_PER_GROUP
D_EXPERT = 512
EPS = 1e-6
NEG_INF = -1e30
LAM_INIT = 0.8 - 0.6 * math.exp(-0.3 * 0)
LOG2E = math.log2(math.e)

PAIRS = ((0, 1), (0, 2), (0, 3), (1, 3), (1, 2), (3, 2))
N_PAIR_BUCKETS = N_GROUPS * len(PAIRS)
GROUP_W = N_HEADS * HEAD_DIM
ROUTE_ROWS = 32

LANES = 128
VMEM_LIMIT_CAP = 56 * 1024 * 1024

PROJ_ROWS = 1024
ATTN_TILE = 256
CROSS_ROWS = 1024
MOE_ROWS = 256
AUX_COLS = LANES
ROW_W = D_MODEL + AUX_COLS
DMA_ROWS = 2048


def _vmem_limit(nbytes):
    return int(min(max(nbytes * 5 // 4, 32 * 1024 * 1024), VMEM_LIMIT_CAP))


def _nt_dot(a, b):
    return lax.dot_general(a, b, (((1,), (1,)), ((), ())), preferred_element_type=F32)


def _rms(x, g):
    ms = jnp.mean(x * x, axis=-1, keepdims=True)
    return x * lax.rsqrt(ms + EPS) * g


def _mix_proj_kernel(x_ref, g_ref, w_ref, wvt_ref, wft_ref, bf_ref, qg_ref, o_ref, vt_ref, cum_ref, carry_ref,
                     *, tm):
    si = pl.program_id(1)
    h = _rms(x_ref[...], g_ref[...]).astype(BF16)
    lane = lax.broadcasted_iota(I32, (tm, HEAD_DIM), 1)
    lo = lane < DIFF_QK_DIM
    for kind in range(4):
        p = jnp.dot(h, w_ref[:, kind * GROUP_W:(kind + 1) * GROUP_W], preferred_element_type=F32)
        for hh in range(N_HEADS):
            ph = p[:, hh * HEAD_DIM:(hh + 1) * HEAD_DIM]
            if kind in (0, 1):
                sq = ph * ph
                s_lo = jnp.sum(jnp.where(lo, sq, 0.0), axis=-1, keepdims=True)
                s_hi = jnp.sum(jnp.where(lo, 0.0, sq), axis=-1, keepdims=True)
                ms = jnp.where(lo, s_lo, s_hi) * (1.0 / DIFF_QK_DIM)
                ph = ph * lax.rsqrt(ms + EPS) * qg_ref[kind:kind + 1, :]
            else:
                ph = _rms(ph, qg_ref[kind:kind + 1, :])
            o_ref[0, kind * N_HEADS + hh] = ph.astype(BF16)
    for kind in range(2):
        pt = _nt_dot(wvt_ref[kind * GROUP_W:(kind + 1) * GROUP_W, :], h)
        for hh in range(N_HEADS):
            vt_ref[0, kind * N_HEADS + hh, :HEAD_DIM, :] = pt[hh * HEAD_DIM:(hh + 1) * HEAD_DIM, :].astype(BF16)
            vt_ref[0, kind * N_HEADS + hh, HEAD_DIM:, :] = jnp.ones((VT_ROWS - HEAD_DIM, tm), BF16)

    z = _nt_dot(wft_ref[...], h) + bf_ref[...]
    logf = jnp.minimum(z, 0.0) - jnp.log(1.0 + jnp.exp(-jnp.abs(z)))
    lane8 = lax.broadcasted_iota(I32, (8, tm), 1)
    c = logf * LOG2E
    k = 1
    while k < tm:
        c = c + jnp.where(lane8 >= k, pltpu.roll(c, k, axis=1), 0.0)
        k *= 2

    @pl.when(si == 0)
    def _():
        carry_ref[...] = jnp.zeros_like(carry_ref)

    c = c + carry_ref[:, 0:1]
    cum_ref[0] = c
    carry_ref[...] = jnp.broadcast_to(c[:, tm - 1:tm], carry_ref.shape)


def _mix_proj(x2d, g, w_qk, w_vt, wft, bf, qg, *, batch, seq):
    tm = min(PROJ_ROWS, seq)
    ns = seq // tm
    est = 2 * (tm * D_MODEL * 4 + D_MODEL * 6 * GROUP_W * 2 + 24 * tm * HEAD_DIM * 2) + 8 * tm * D_MODEL
    return pl.pallas_call(
        functools.partial(_mix_proj_kernel, tm=tm),
        grid=(batch, ns),
        in_specs=[
            pl.BlockSpec((tm, D_MODEL), lambda b, s: (b * ns + s, 0)),
            pl.BlockSpec((1, D_MODEL), lambda b, s: (0, 0)),
            pl.BlockSpec((D_MODEL, 4 * GROUP_W), lambda b, s: (0, 0)),
            pl.BlockSpec((2 * GROUP_W, D_MODEL), lambda b, s: (0, 0)),
            pl.BlockSpec((8, D_MODEL), lambda b, s: (0, 0)),
            pl.BlockSpec((8, 1), lambda b, s: (0, 0)),
            pl.BlockSpec((8, HEAD_DIM), lambda b, s: (0, 0)),
        ],
        out_specs=[
            pl.BlockSpec((1, 16, tm, HEAD_DIM), lambda b, s: (b, 0, s, 0)),
            pl.BlockSpec((1, 8, VT_ROWS, tm), lambda b, s: (b, 0, 0, s)),
            pl.BlockSpec((1, 8, tm), lambda b, s: (b, 0, s)),
        ],
        out_shape=[
            jax.ShapeDtypeStruct((batch, 16, seq, HEAD_DIM), BF16),
            jax.ShapeDtypeStruct((batch, 8, VT_ROWS, seq), BF16),
            jax.ShapeDtypeStruct((batch, 8, seq), F32),
        ],
        scratch_shapes=[pltpu.VMEM((8, LANES), F32)],
        compiler_params=pltpu.CompilerParams(
            dimension_semantics=("arbitrary", "arbitrary"), vmem_limit_bytes=_vmem_limit(est)),
        name="mix_proj",
    )(x2d, g, w_qk, w_vt, wft, bf, qg)


def _init_softmax(m_sc, acc_sc):
    m_sc[...] = jnp.full(m_sc.shape, NEG_INF, F32)
    acc_sc[...] = jnp.zeros_like(acc_sc)


N_CHAINS = 3 * N_HEADS


def _attn_kernel(lam_ref, dq_ref, dk_ref, dvt_ref, fq_ref, fk_ref, fvt_ref, bias_ref, c_ref,
                 gd_ref, gf_ref, o_ref, m_sc, acc_sc, qm_sc, ccol_sc, s_sc, mt_sc, s2_sc, mt2_sc, *, t, seq):
    qi = pl.program_id(1)
    _init_softmax(m_sc, acc_sc)
    key = lax.broadcasted_iota(I32, (t, t), 0)
    qry = lax.broadcasted_iota(I32, (t, t), 1)

    lane = lax.broadcasted_iota(I32, (t, HEAD_DIM), 1)
    for hh in range(N_HEADS):
        q = dq_ref[0, hh]
        zero = jnp.zeros_like(q)
        qm_sc[2 * hh] = jnp.where(lane < DIFF_QK_DIM, q, zero)
        qm_sc[2 * hh + 1] = jnp.where(lane < DIFF_QK_DIM, zero, q)

    @pl.when(qi == 0)
    def _():
        for hh in range(N_HEADS):
            for j in range(seq // t):
                row = c_ref[0, hh, :, j * t:(j + 1) * t]
                ccol_sc[hh, j * t:(j + 1) * t, :] = jnp.sum(
                    jnp.where(key == qry, jnp.broadcast_to(row, (t, t)), 0.0), axis=1, keepdims=True)

    q_off = pl.multiple_of(qi * t, t)

    bufs = ((s_sc, mt_sc), (s2_sc, mt2_sc))

    def logits(j, where, par, slot):
        off = pl.multiple_of(j * t, t)
        s_buf, mt_buf = bufs[par]
        if slot < 2 * N_HEADS:
            hh = slot // 2
            s_t = _nt_dot(dk_ref[0, hh, pl.ds(off, t), :], qm_sc[slot])
            if where != "far":
                s_t = s_t + bias_ref[hh, 1 if where == "near" else 0]
            s_buf[slot] = s_t
            mt_buf[slot] = jnp.max(s_t, axis=0, keepdims=True)
        else:
            hh = slot - 2 * N_HEADS
            c_k = ccol_sc[hh, pl.ds(off, t), :]
            s_t = _nt_dot(fk_ref[0, hh, pl.ds(off, t), :], fq_ref[0, hh]) - c_k
            if where == "diag":
                s_t = jnp.where(key <= qry, s_t, NEG_INF)
            s_buf[slot] = s_t
            mt_buf[slot] = jnp.max(s_t, axis=0, keepdims=True) + c_ref[0, hh, :, pl.ds(q_off, t)]

    def update(j, par, slot):
        off = pl.multiple_of(j * t, t)
        s_buf, mt_buf = bufs[par]
        vt_ref = dvt_ref if slot < 2 * N_HEADS else fvt_ref
        hh = slot // 2 if slot < 2 * N_HEADS else slot - 2 * N_HEADS
        v_t = vt_ref[0, hh, :, pl.ds(off, t)]
        m_prev = m_sc[slot]
        m_new = jnp.maximum(m_prev, mt_buf[slot])
        alpha = jnp.exp2(m_prev - m_new)
        if slot < 2 * N_HEADS:
            p_t = jnp.exp2(s_buf[slot] - m_new)
        else:
            p_t = jnp.exp2(s_buf[slot] - (m_new - c_ref[0, hh, :, pl.ds(q_off, t)]))
        acc_sc[slot] = alpha * acc_sc[slot] + jnp.dot(v_t, p_t.astype(BF16), preferred_element_type=F32)
        m_sc[slot] = m_new

    def phase(upd=None, nxt=None):
        for slot in range(N_CHAINS):
            if nxt is not None:
                logits(*nxt, slot)
            if upd is not None:
                update(*upd, slot)

    n_far = jnp.maximum(qi - 1, 0)
    far_tile = lambda k: jnp.maximum(qi - 2 - k, 0)

    phase(nxt=(qi, "diag", 0))

    @pl.when(qi == 0)
    def _():
        phase(upd=(qi, 0))

    @pl.when(qi == 1)
    def _():
        phase(upd=(qi, 0), nxt=(qi - 1, "near", 1))
        phase(upd=(qi - 1, 1))

    @pl.when(qi >= 2)
    def _():
        phase(upd=(qi, 0), nxt=(qi - 1, "near", 1))
        phase(upd=(qi - 1, 1), nxt=(far_tile(0), "far", 0))

        def far_pair(i, carry):
            k = 2 * i
            phase(upd=(far_tile(k), 0), nxt=(far_tile(k + 1), "far", 1))

            @pl.when(k + 1 < n_far)
            def _():
                phase(upd=(far_tile(k + 1), 1), nxt=(far_tile(k + 2), "far", 0))
            return carry

        lax.fori_loop(0, (n_far + 1) // 2, far_pair, 0)

    def normalised(slot):
        return acc_sc[slot, :HEAD_DIM, :] / acc_sc[slot, HEAD_DIM:HEAD_DIM + 1, :]

    def rms_t(o_t, g_t):
        ms = jnp.mean(o_t * o_t, axis=0, keepdims=True)
        return o_t * lax.rsqrt(ms + EPS) * g_t

    for hh in range(N_HEADS):
        o_t = normalised(2 * hh) - lam_ref[0] * normalised(2 * hh + 1)
        o_ref[:, hh * HEAD_DIM:(hh + 1) * HEAD_DIM] = (
            rms_t(o_t, gd_ref[...]) * (1.0 - LAM_INIT)).T.astype(BF16)
    for hh in range(N_HEADS):
        o_t = normalised(2 * N_HEADS + hh)
        o_ref[:, (N_HEADS + hh) * HEAD_DIM:(N_HEADS + hh + 1) * HEAD_DIM] = rms_t(o_t, gf_ref[...]).T.astype(BF16)


def _attn(qk, vt, bias_tiles_t, lam, cum4, subln_g, fox_g, *, batch, seq):
    t = min(ATTN_TILE, seq)
    nq = seq // t
    head_blk = N_HEADS * seq * HEAD_DIM * 2
    est = (2 * (2 * N_HEADS * t * HEAD_DIM * 2 + 4 * head_blk + N_HEADS * 2 * t * t * 4 + N_HEADS * 8 * seq * 4
                + t * 2 * N_HEADS * HEAD_DIM * 2)
           + N_CHAINS * (HEAD_DIM + 16) * t * 4 + 2 * N_HEADS * t * HEAD_DIM * 2 + N_HEADS * seq * LANES * 4
           + (2 * N_CHAINS + 36) * t * t * 4)
    qspec = lambda blk: pl.BlockSpec((1, N_HEADS, t, HEAD_DIM), lambda b, i: (b, blk, i, 0))
    kspec = lambda blk: pl.BlockSpec((1, N_HEADS, seq, HEAD_DIM), lambda b, i: (b, blk, 0, 0))
    vspec = lambda blk: pl.BlockSpec((1, N_HEADS, VT_ROWS, seq), lambda b, i: (b, blk, 0, 0))
    return pl.pallas_call(
        functools.partial(_attn_kernel, t=t, seq=seq),
        grid=(batch, nq),
        in_specs=[
            pl.BlockSpec(memory_space=pltpu.SMEM),
            qspec(0), kspec(1), vspec(0),
            qspec(2), kspec(3), vspec(1),
            pl.BlockSpec((N_HEADS, 2, t, t), lambda b, i: (0, 0, 0, 0)),
            pl.BlockSpec((1, N_HEADS, 1, seq), lambda b, i: (b, 0, 0, 0)),
            pl.BlockSpec((HEAD_DIM, t), lambda b, i: (0, 0)),
            pl.BlockSpec((HEAD_DIM, t), lambda b, i: (0, 0)),
        ],
        out_specs=pl.BlockSpec((t, 2 * N_HEADS * HEAD_DIM), lambda b, i: (b * nq + i, 0)),
        out_shape=jax.ShapeDtypeStruct((batch * seq, 2 * N_HEADS * HEAD_DIM), BF16),
        scratch_shapes=[pltpu.VMEM((N_CHAINS, 1, t), F32),
                        pltpu.VMEM((N_CHAINS, VT_ROWS, t), F32),
                        pltpu.VMEM((2 * N_HEADS, t, HEAD_DIM), BF16),
                        pltpu.VMEM((N_HEADS, seq, 1), F32),
                        pltpu.VMEM((N_CHAINS, t, t), F32), pltpu.VMEM((N_CHAINS, 1, t), F32),
                        pltpu.VMEM((N_CHAINS, t, t), F32), pltpu.VMEM((N_CHAINS, 1, t), F32)],
        compiler_params=pltpu.CompilerParams(
            dimension_semantics=("arbitrary", "arbitrary"), vmem_limit_bytes=_vmem_limit(est)),
        name="attn",
    )(lam, qk, qk, vt, qk, qk, vt, bias_tiles_t, cum4, subln_g, fox_g)


def _out_q_kernel(x_ref, m_ref, wo_ref, gc_ref, wq_ref, qg_ref, x1_ref, qc_ref):
    x1 = x_ref[...] + jnp.dot(m_ref[...], wo_ref[...], preferred_element_type=F32)
    x1_ref[...] = x1
    hc = _rms(x1, gc_ref[...]).astype(BF16)
    q = jnp.dot(hc, wq_ref[...], preferred_element_type=F32)
    for hh in range(N_HEADS):
        sl = slice(hh * CROSS_HEAD_DIM, (hh + 1) * CROSS_HEAD_DIM)
        qh = _rms(q[:, sl], qg_ref[...]) * (CROSS_HEAD_DIM ** -0.5)
        qc_ref[:, sl] = qh.astype(BF16)


def _out_q(x2d, mixed, w_out, g_cross, w_cq, q_g):
    n = x2d.shape[0]
    tm = min(PROJ_ROWS, n)
    est = 2 * (tm * D_MODEL * 4 * 2 + tm * D_MODEL * 2 + 2 * D_MODEL * D_MODEL * 2 + tm * D_MODEL * 2) + 6 * tm * D_MODEL * 4
    return pl.pallas_call(
        _out_q_kernel,
        grid=(n // tm,),
        in_specs=[
            pl.BlockSpec((tm, D_MODEL), lambda i: (i, 0)),
            pl.BlockSpec((tm, D_MODEL), lambda i: (i, 0)),
            pl.BlockSpec((D_MODEL, D_MODEL), lambda i: (0, 0)),
            pl.BlockSpec((1, D_MODEL), lambda i: (0, 0)),
            pl.BlockSpec((D_MODEL, D_MODEL), lambda i: (0, 0)),
            pl.BlockSpec((1, CROSS_HEAD_DIM), lambda i: (0, 0)),
        ],
        out_specs=[pl.BlockSpec((tm, D_MODEL), lambda i: (i, 0)),
                   pl.BlockSpec((tm, D_MODEL), lambda i: (i, 0))],
        out_shape=[jax.ShapeDtypeStruct((n, D_MODEL), F32), jax.ShapeDtypeStruct((n, D_MODEL), BF16)],
        compiler_params=pltpu.CompilerParams(
            dimension_semantics=("arbitrary",), vmem_limit_bytes=_vmem_limit(est)),
        name="out_q",
    )(x2d, mixed, w_out, g_cross, w_cq, q_g)


def _mem_kv_kernel(mem_ref, gm_ref, w_ref, kg_ref, k_ref, v_ref):
    mn = _rms(mem_ref[...], gm_ref[...]).astype(BF16)
    kv = jnp.dot(mn, w_ref[...], preferred_element_type=F32)
    for hh in range(N_HEADS):
        sl = slice(hh * CROSS_HEAD_DIM, (hh + 1) * CROSS_HEAD_DIM)
        k_ref[0, :, sl] = _rms(kv[:, sl], kg_ref[...]).astype(BF16)
    v_ref[0] = kv[:, D_MODEL:].astype(BF16)


def _mem_kv(mem2d, g_mem, w_ckv, k_g, *, batch):
    est = 2 * (N_MEM * D_MODEL * 4 + D_MODEL * 2 * D_MODEL * 2 + 2 * N_MEM * D_MODEL * 2) + 4 * N_MEM * 2 * D_MODEL * 4
    return pl.pallas_call(
        _mem_kv_kernel,
        grid=(batch,),
        in_specs=[
            pl.BlockSpec((N_MEM, D_MODEL), lambda b: (b, 0)),
            pl.BlockSpec((1, D_MODEL), lambda b: (0, 0)),
            pl.BlockSpec((D_MODEL, 2 * D_MODEL), lambda b: (0, 0)),
            pl.BlockSpec((1, CROSS_HEAD_DIM), lambda b: (0, 0)),
        ],
        out_specs=[pl.BlockSpec((1, N_MEM, D_MODEL), lambda b: (b, 0, 0)),
                   pl.BlockSpec((1, N_MEM, D_MODEL), lambda b: (b, 0, 0))],
        out_shape=[jax.ShapeDtypeStruct((batch, N_MEM, D_MODEL), BF16),
                   jax.ShapeDtypeStruct((batch, N_MEM, D_MODEL), BF16)],
        compiler_params=pltpu.CompilerParams(
            dimension_semantics=("arbitrary",), vmem_limit_bytes=_vmem_limit(est)),
        name="mem_kv",
    )(mem2d, g_mem, w_ckv, k_g)


def _route(logits_t):
    gl = logits_t[0:N_GROUPS]
    gmax = jnp.max(gl, axis=0, keepdims=True)
    eg = jnp.exp(gl - gmax)
    p_group = eg / jnp.sum(eg, axis=0, keepdims=True)
    p_g = jnp.max(p_group, axis=0, keepdims=True)
    g_idx = jnp.full_like(p_g, float(N_GROUPS))
    for g in reversed(range(N_GROUPS)):
        g_idx = jnp.where(p_group[g:g + 1] == p_g, float(g), g_idx)

    sel = []
    for j in range(EXPERTS_PER_GROUP):
        acc = jnp.zeros_like(p_g)
        for g in range(N_GROUPS):
            r = N_GROUPS + g * EXPERTS_PER_GROUP + j
            acc = jnp.where(g_idx == float(g), logits_t[r:r + 1], acc)
        sel.append(acc)
    smax = functools.reduce(jnp.maximum, sel)
    es = [jnp.exp(s - smax) for s in sel]
    den = functools.reduce(jnp.add, es)
    p_in = [e / den for e in es]

    v0 = functools.reduce(jnp.maximum, p_in)
    i0 = jnp.full_like(v0, float(EXPERTS_PER_GROUP))
    for j in reversed(range(EXPERTS_PER_GROUP)):
        i0 = jnp.where(p_in[j] == v0, float(j), i0)
    rest = [jnp.where(i0 == float(j), -1.0, p_in[j]) for j in range(EXPERTS_PER_GROUP)]
    v1 = functools.reduce(jnp.maximum, rest)
    i1 = jnp.full_like(v1, float(EXPERTS_PER_GROUP))
    for j in reversed(range(EXPERTS_PER_GROUP)):
        i1 = jnp.where(rest[j] == v1, float(j), i1)

    tot = v0 + v1
    w0 = p_g * (v0 / tot)
    w1 = p_g * (v1 / tot)
    first = i0 < i1
    a = jnp.where(first, i0, i1)
    b = jnp.where(first, i1, i0)
    w_lo = jnp.where(first, w0, w1)
    w_hi = jnp.where(first, w1, w0)
    pair = jnp.where(a == 0.0, b - 1.0, jnp.where(a == 1.0, jnp.where(b == 3.0, 3.0, 4.0), 5.0))
    swap = a == 2.0
    wa = jnp.where(swap, w_hi, w_lo)
    wb = jnp.where(swap, w_lo, w_hi)
    bucket = g_idx * float(len(PAIRS)) + pair
    return bucket, wa, wb


def _cross_router_kernel(q_ref, k_ref, v_ref, x1_ref, wco_ref, gf_ref, whi_ref, wlo_ref, rb_ref,
                         xa_ref, rt_ref, *, tm):
    q = q_ref[...]
    k = k_ref[0]
    v = v_ref[0]
    outs = []
    for hh in range(N_HEADS):
        sl = slice(hh * CROSS_HEAD_DIM, (hh + 1) * CROSS_HEAD_DIM)
        s = _nt_dot(q[:, sl], k[:, sl])
        p = jnp.exp(s - jnp.max(s, axis=-1, keepdims=True))
        l = jnp.sum(p, axis=-1, keepdims=True)
        o = jnp.dot(p.astype(BF16), v[:, sl], preferred_element_type=F32) / l
        outs.append(o.astype(BF16))
    o = jnp.concatenate(outs, axis=1)
    x2 = x1_ref[...] + jnp.dot(o, wco_ref[...], preferred_element_type=F32)
    xa_ref[:, :D_MODEL] = x2

    h3 = _rms(x2, gf_ref[...])
    hi = h3.astype(BF16)
    lo = (h3 - hi.astype(F32)).astype(BF16)
    lt = _nt_dot(whi_ref[...], hi) + (_nt_dot(whi_ref[...], lo) + _nt_dot(wlo_ref[...], hi))
    lt = lt + rb_ref[...]
    bucket, wa, wb = _route(lt)

    sub = lax.broadcasted_iota(I32, (8, tm), 0)
    rows = jnp.where(sub == 0, bucket, jnp.where(sub == 1, wa, jnp.where(sub == 2, wb, 0.0)))
    rt_ref[0] = rows
    sub_a = lax.broadcasted_iota(I32, (AUX_COLS, tm), 0)
    aux_t = jnp.where(sub_a == 0, bucket, jnp.where(sub_a == 1, wa, jnp.where(sub_a == 2, wb, 0.0)))
    xa_ref[:, D_MODEL:] = aux_t.T


def _cross_router(qc, kc, vc, x1, w_co, g_ffn, wr_hi, wr_lo, rb, *, batch, seq):
    tm = min(CROSS_ROWS, seq)
    ns = seq // tm
    n = batch * seq
    est = 2 * (tm * D_MODEL * 2 + 2 * N_MEM * D_MODEL * 2 + tm * D_MODEL * 4 + D_MODEL * D_MODEL * 2
               + tm * ROW_W * 4) + 10 * tm * D_MODEL * 4
    return pl.pallas_call(
        functools.partial(_cross_router_kernel, tm=tm),
        grid=(batch, ns),
        in_specs=[
            pl.BlockSpec((tm, D_MODEL), lambda b, s: (b * ns + s, 0)),
            pl.BlockSpec((1, N_MEM, D_MODEL), lambda b, s: (b, 0, 0)),
            pl.BlockSpec((1, N_MEM, D_MODEL), lambda b, s: (b, 0, 0)),
            pl.BlockSpec((tm, D_MODEL), lambda b, s: (b * ns + s, 0)),
            pl.BlockSpec((D_MODEL, D_MODEL), lambda b, s: (0, 0)),
            pl.BlockSpec((1, D_MODEL), lambda b, s: (0, 0)),
            pl.BlockSpec((ROUTE_ROWS, D_MODEL), lambda b, s: (0, 0)),
            pl.BlockSpec((ROUTE_ROWS, D_MODEL), lambda b, s: (0, 0)),
            pl.BlockSpec((ROUTE_ROWS, 1), lambda b, s: (0, 0)),
        ],
        out_specs=[pl.BlockSpec((tm, ROW_W), lambda b, s: (b * ns + s, 0)),
                   pl.BlockSpec((1, 8, tm), lambda b, s: (b * ns + s, 0, 0))],
        out_shape=[jax.ShapeDtypeStruct((n, ROW_W), F32),
                   jax.ShapeDtypeStruct((n // tm, 8, tm), F32)],
        compiler_params=pltpu.CompilerParams(
            dimension_semantics=("arbitrary", "arbitrary"), vmem_limit_bytes=_vmem_limit(est)),
        name="cross_router",
    )(qc, kc, vc, x1, w_co, g_ffn, wr_hi, wr_lo, rb)


def _rank_kernel(rt_ref, pos_ref, cnt_ref, carry_sc, off_sc, *, tm, nt):
    i = pl.program_id(0)
    sub = lax.broadcasted_iota(I32, (ROUTE_ROWS, tm), 0).astype(F32)

    @pl.when(i == 0)
    def _():
        n_hit = jnp.zeros((ROUTE_ROWS, 1), F32)
        for j in range(nt):
            n_hit = n_hit + jnp.sum(jnp.where(sub == rt_ref[j, 0:1, :], 1.0, 0.0), axis=1, keepdims=True)
        counts = jnp.broadcast_to(n_hit, cnt_ref.shape)
        cnt_ref[...] = counts
        tiles = jnp.floor((counts + (MOE_ROWS - 1)) * (1.0 / MOE_ROWS))
        r = lax.broadcasted_iota(I32, (ROUTE_ROWS, ROUTE_ROWS), 0)
        c = lax.broadcasted_iota(I32, (ROUTE_ROWS, ROUTE_ROWS), 1)
        before = jnp.where(c < r, 1.0, 0.0).astype(BF16)
        off_sc[...] = jnp.dot(before, tiles.astype(BF16), preferred_element_type=F32) * MOE_ROWS
        carry_sc[...] = jnp.zeros_like(carry_sc)

    @pl.when(i >= 1)
    def _():
        hit = sub == rt_ref[i - 1, 0:1, :]
        r = lax.broadcasted_iota(I32, (tm, tm), 0)
        c = lax.broadcasted_iota(I32, (tm, tm), 1)
        upper = jnp.where(r <= c, 1.0, 0.0).astype(BF16)
        cum = jnp.dot(jnp.where(hit, 1.0, 0.0).astype(BF16), upper, preferred_element_type=F32)
        base = carry_sc[:, 0:1] + off_sc[:, 0:1]
        pos = jnp.sum(jnp.where(hit, cum - 1.0 + base, 0.0), axis=0, keepdims=True)
        pos_ref[0] = pos.astype(I32)
        carry_sc[...] = carry_sc[...] + jnp.broadcast_to(cum[:, tm - 1:tm], carry_sc.shape)


def _rank(rt):
    nt, _, tm = rt.shape
    return pl.pallas_call(
        functools.partial(_rank_kernel, tm=tm, nt=nt),
        grid=(nt + 1,),
        in_specs=[pl.BlockSpec((nt, 8, tm), lambda i: (0, 0, 0))],
        out_specs=[pl.BlockSpec((1, 1, tm), lambda i: (jnp.maximum(i - 1, 0), 0, 0)),
                   pl.BlockSpec((ROUTE_ROWS, LANES), lambda i: (0, 0))],
        out_shape=[jax.ShapeDtypeStruct((nt, 1, tm), I32), jax.ShapeDtypeStruct((ROUTE_ROWS, LANES), F32)],
        scratch_shapes=[pltpu.VMEM((ROUTE_ROWS, LANES), F32), pltpu.VMEM((ROUTE_ROWS, LANES), F32)],
        compiler_params=pltpu.CompilerParams(dimension_semantics=("arbitrary",)),
        name="rank",
    )(rt)


def _dispatch_kernel(pos_ref, zrow_ref, zflag_ref, nact_ref, xa_ref, xs_ref, zbuf, zsem, usem, sem, *,
                     n_tiles, rows):
    @pl.when(pl.program_id(0) == 0)
    def _():
        _zero_partial_tiles(zrow_ref, zflag_ref, nact_ref, xs_ref, zbuf, zsem, usem, n_tiles=n_tiles)

    base = pl.program_id(0) * rows
    for r in range(rows):
        pltpu.make_async_copy(xa_ref.at[pl.ds(r, 1)], xs_ref.at[pl.ds(pos_ref[base + r], 1)], sem).start(
            priority=r % 2)
    pltpu.make_async_copy(xa_ref, xs_ref.at[pl.ds(0, rows)], sem).wait()

    @pl.when(pl.program_id(0) == pl.num_programs(0) - 1)
    def _():
        _unused_tiles(nact_ref, xs_ref, zbuf, usem, n_tiles, lambda cp: cp.wait())


def _zero_copy(xs_ref, zbuf, row, sem):
    return pltpu.make_async_copy(zbuf, xs_ref.at[pl.ds(pl.multiple_of(row, MOE_ROWS), MOE_ROWS)], sem)


def _unused_tiles(nact_ref, xs_ref, zbuf, usem, n_tiles, op):
    def body(i, carry):
        op(_zero_copy(xs_ref, zbuf, i * MOE_ROWS, usem))
        return carry
    lax.fori_loop(nact_ref[0], n_tiles, body, 0)


def _zero_partial_tiles(zrow_ref, zflag_ref, nact_ref, xs_ref, zbuf, zsem, usem, *, n_tiles):
    zbuf[...] = jnp.zeros_like(zbuf)

    def bucket_tiles(op):
        def body(b, carry):
            @pl.when(zflag_ref[b] > 0)
            def _():
                op(_zero_copy(xs_ref, zbuf, zrow_ref[b], zsem))
            return carry
        lax.fori_loop(0, N_PAIR_BUCKETS, body, 0)

    bucket_tiles(lambda cp: cp.start())
    _unused_tiles(nact_ref, xs_ref, zbuf, usem, n_tiles, lambda cp: cp.start())
    bucket_tiles(lambda cp: cp.wait())


def _dispatch(pos, zrow, zflag, n_active, xa, *, n_tiles):
    n_tok = xa.shape[0]
    n_sorted = n_tiles * MOE_ROWS
    rows = min(DMA_ROWS, n_tok)
    return pl.pallas_call(
        functools.partial(_dispatch_kernel, n_tiles=n_tiles, rows=rows),
        grid_spec=pltpu.PrefetchScalarGridSpec(
            num_scalar_prefetch=4,
            grid=(n_tok // rows,),
            in_specs=[pl.BlockSpec((rows, ROW_W), lambda i, *_: (i, 0))],
            out_specs=pl.BlockSpec(memory_space=pl.ANY),
            scratch_shapes=[pltpu.VMEM((MOE_ROWS, ROW_W), F32), pltpu.SemaphoreType.DMA(()),
                            pltpu.SemaphoreType.DMA(()), pltpu.SemaphoreType.DMA(())],
        ),
        out_shape=jax.ShapeDtypeStruct((n_sorted, ROW_W), F32),
        compiler_params=pltpu.CompilerParams(
            dimension_semantics=("arbitrary",), has_side_effects=True,
            vmem_limit_bytes=_vmem_limit((2 * rows + MOE_ROWS) * ROW_W * 4)),
        name="dispatch",
    )(pos, zrow, zflag, n_active, xa)


def _moe_kernel(plan_ref, nact_ref, xs_ref, gf_ref, wg_hbm, wu_hbm, wd_hbm, ys_ref, wgu_sc, wd_sc, wsem):
    i = pl.program_id(0)

    def weight_copies(ea, eb, buf):
        cps = []
        for k, (src, e) in enumerate(((wg_hbm, ea), (wu_hbm, ea), (wg_hbm, eb), (wu_hbm, eb))):
            cps.append(pltpu.make_async_copy(src.at[e], wgu_sc.at[buf, k], wsem.at[buf]))
        for k, e in enumerate((ea, eb)):
            cps.append(pltpu.make_async_copy(wd_hbm.at[e], wd_sc.at[buf, k], wsem.at[buf]))
        return cps

    @pl.when(i < nact_ref[0])
    def _():
        buf = plan_ref[4, i]

        @pl.when(plan_ref[3, i] == 1)
        def _():
            @pl.when(i == 0)
            def _():
                for cp in weight_copies(plan_ref[1, i], plan_ref[2, i], buf):
                    cp.start()
            for cp in weight_copies(plan_ref[1, i], plan_ref[2, i], buf):
                cp.wait()

            @pl.when(plan_ref[7, i] == 1)
            def _():
                for cp in weight_copies(plan_ref[5, i], plan_ref[6, i], 1 - buf):
                    cp.start()

        x2 = xs_ref[:, :D_MODEL]
        aux = xs_ref[:, D_MODEL:]
        wa = aux[:, 1:2]
        wb = aux[:, 2:3]
        h = _rms(x2, gf_ref[...]).astype(BF16)

        def mlp(slot):
            g = jnp.dot(h, wgu_sc[buf, 2 * slot].astype(BF16), preferred_element_type=F32)
            u = jnp.dot(h, wgu_sc[buf, 2 * slot + 1].astype(BF16), preferred_element_type=F32)
            act = (g / (1.0 + jnp.exp(-g))) * u
            return jnp.dot(act.astype(BF16), wd_sc[buf, slot].astype(BF16), preferred_element_type=F32)

        ys_ref[...] = x2 + (wa * mlp(0) + wb * mlp(1))

    @pl.when(i >= nact_ref[0])
    def _():
        ys_ref[...] = jnp.zeros_like(ys_ref)


def _moe(plan, n_active, xs, g_ffn, w_gate, w_up, w_down):
    n_tiles = xs.shape[0] // MOE_ROWS
    t = MOE_ROWS
    est = (2 * (t * ROW_W * 4 + t * D_MODEL * 4) + 2 * 6 * D_MODEL * D_EXPERT * 4 + 4 * D_MODEL * D_EXPERT * 4
           + 10 * t * D_MODEL * 4)
    return pl.pallas_call(
        _moe_kernel,
        grid_spec=pltpu.PrefetchScalarGridSpec(
            num_scalar_prefetch=2,
            grid=(n_tiles,),
            in_specs=[
                pl.BlockSpec((t, ROW_W), lambda i, plan, na: (plan[0, i], 0)),
                pl.BlockSpec((1, D_MODEL), lambda i, plan, na: (0, 0)),
                pl.BlockSpec(memory_space=pl.ANY),
                pl.BlockSpec(memory_space=pl.ANY),
                pl.BlockSpec(memory_space=pl.ANY),
            ],
            out_specs=pl.BlockSpec((t, D_MODEL), lambda i, plan, na: (i, 0)),
            scratch_shapes=[pltpu.VMEM((2, 4, D_MODEL, D_EXPERT), F32), pltpu.VMEM((2, 2, D_EXPERT, D_MODEL), F32),
                            pltpu.SemaphoreType.DMA((2,))],
        ),
        out_shape=jax.ShapeDtypeStruct((n_tiles * t, D_MODEL), F32),
        compiler_params=pltpu.CompilerParams(
            dimension_semantics=("arbitrary",), vmem_limit_bytes=_vmem_limit(est)),
        name="moe",
    )(plan, n_active, xs, g_ffn, w_gate, w_up, w_down)


def _combine_kernel(pos_ref, ys_ref, out_ref, sem, *, rows):
    base = pl.program_id(0) * rows
    for r in range(rows):
        pltpu.make_async_copy(ys_ref.at[pl.ds(pos_ref[base + r], 1)], out_ref.at[pl.ds(r, 1)], sem).start(
            priority=r % 2)
    pltpu.make_async_copy(ys_ref.at[pl.ds(0, rows)], out_ref, sem).wait()


def _combine(pos, ys, *, n_tok):
    rows = min(DMA_ROWS, n_tok)
    return pl.pallas_call(
        functools.partial(_combine_kernel, rows=rows),
        grid_spec=pltpu.PrefetchScalarGridSpec(
            num_scalar_prefetch=1,
            grid=(n_tok // rows,),
            in_specs=[pl.BlockSpec(memory_space=pl.ANY)],
            out_specs=pl.BlockSpec((rows, D_MODEL), lambda i, *_: (i, 0)),
            scratch_shapes=[pltpu.SemaphoreType.DMA(())],
        ),
        out_shape=jax.ShapeDtypeStruct((n_tok, D_MODEL), F32),
        compiler_params=pltpu.CompilerParams(
            dimension_semantics=("arbitrary",), has_side_effects=True,
            vmem_limit_bytes=_vmem_limit(2 * rows * D_MODEL * 4)),
        name="combine",
    )(pos, ys)


def _t5_bucket(rel):
    nb = N_BUCKETS_T5 // 2
    max_exact = nb // 2
    ret = (rel > 0).astype(I32) * nb
    n = jnp.abs(rel)
    nf = jnp.maximum(n, 1).astype(F32)
    large = max_exact + (jnp.log(nf / max_exact) / math.log(MAX_DISTANCE / max_exact)
                         * (nb - max_exact)).astype(I32)
    large = jnp.minimum(large, nb - 1)
    return ret + jnp.where(n < max_exact, n, large)


def _diff_bias(rel_bias, t):
    assert t % CHUNK == 0 and t >= MAX_DISTANCE
    n_rel = 3 * t
    vec = rel_bias[_t5_bucket(jnp.arange(n_rel, dtype=I32) - (2 * t - 1))].astype(F32).T
    toe = jnp.tile(vec, (1, t))[:, :t * (n_rel - 1)].reshape(N_HEADS, t, n_rel - 1)
    qpos = jnp.arange(t, dtype=I32)[:, None]
    kpos = jnp.arange(t, dtype=I32)[None, :]
    far = rel_bias[_t5_bucket(jnp.asarray(-2 * t, I32))].astype(F32)[:, None, None]
    b0 = (toe[:, :, 2 * t - 1:3 * t - 1] - far) * LOG2E
    b0 = jnp.where((kpos // CHUNK <= qpos // CHUNK)[None], b0, NEG_INF)
    b1 = (toe[:, :, t - 1:2 * t - 1] - far) * LOG2E
    return jnp.swapaxes(jnp.stack([b0, b1], axis=1), -1, -2)


def _tile_plan(counts, n_tiles):
    t = MOE_ROWS
    tiles = (counts + t - 1) // t
    ends = jnp.cumsum(tiles)
    n_active = ends[-1]
    tile_idx = jnp.minimum(jnp.arange(n_tiles, dtype=I32), n_active - 1)
    tile_bucket = jnp.minimum(jnp.sum((ends[None, :] <= tile_idx[:, None]).astype(I32), axis=1), N_PAIR_BUCKETS - 1)
    pa = jnp.asarray([p[0] for p in PAIRS], I32)
    pb = jnp.asarray([p[1] for p in PAIRS], I32)
    grp = tile_bucket // len(PAIRS)
    pair = tile_bucket % len(PAIRS)
    tile_a = grp * EXPERTS_PER_GROUP + pa[pair]
    tile_b = grp * EXPERTS_PER_GROUP + pb[pair]
    zrow = ((ends - 1) * t).astype(I32)
    zflag = (counts > 0).astype(I32)
    prev_bucket = jnp.concatenate([jnp.full((1,), -1, I32), tile_bucket[:-1]])
    first = (tile_bucket != prev_bucket).astype(I32)
    buf = (jnp.cumsum(first) - 1) % 2
    nxt = ends[tile_bucket]
    has_next = (nxt < n_active).astype(I32)
    nxt = jnp.minimum(nxt, n_tiles - 1)
    plan = jnp.stack([tile_idx, tile_a, tile_b, first, buf, tile_a[nxt], tile_b[nxt], has_next]).astype(I32)
    return zrow, zflag, plan, n_active.astype(I32)


def kernel(x, mem, rel_bias, norm_mix_g, w_in, b_forget, diff_q_norm_g, diff_k_norm_g, diff_lambda_q1, diff_lambda_k1, diff_lambda_q2, diff_lambda_k2, diff_subln_g, fox_q_norm_g, fox_k_norm_g, fox_out_norm_g, w_out, norm_cross_g, norm_mem_g, w_cq, w_ckv, cross_q_norm_g, cross_k_norm_g, w_co, norm_ffn_g, w_group_router, b_group_router, w_expert_router, b_expert_router, w_exp_gate, w_exp_up, w_exp_down):
    batch, seq, d = x.shape
    assert d == D_MODEL and norm_mix_g.shape[0] == 1 and mem.shape[1] == N_MEM
    n_tok = batch * seq
    assert seq % ATTN_TILE == 0 or seq < ATTN_TILE
    l = 0

    gw = GROUP_W
    w_qk = jnp.concatenate([w_in[l, :, :2 * gw], w_in[l, :, 3 * gw:5 * gw]], axis=1).astype(BF16)
    w_vt = jnp.concatenate([w_in[l, :, 2 * gw:3 * gw], w_in[l, :, 5 * gw:6 * gw]], axis=1).T.astype(BF16)
    wft = jnp.zeros((8, D_MODEL), F32).at[:N_HEADS].set(w_in[l, :, 6 * gw:].T).astype(BF16)
    bf = jnp.zeros((8, 1), F32).at[:N_HEADS, 0].set(b_forget[l].astype(F32))
    ones = jnp.ones((HEAD_DIM,), F32)
    qg = jnp.stack([
        jnp.tile(diff_q_norm_g[l].astype(F32), 2) * (DIFF_QK_DIM ** -0.5 * LOG2E),
        jnp.tile(diff_k_norm_g[l].astype(F32), 2),
        fox_q_norm_g[l].astype(F32) * (HEAD_DIM ** -0.5 * LOG2E), fox_k_norm_g[l].astype(F32),
        ones, ones, ones, ones])
    lam = (jnp.exp(jnp.sum(diff_lambda_q1[l].astype(F32) * diff_lambda_k1[l].astype(F32)))
           - jnp.exp(jnp.sum(diff_lambda_q2[l].astype(F32) * diff_lambda_k2[l].astype(F32)))
           + LAM_INIT).reshape(1)
    t_attn = min(ATTN_TILE, seq)
    bias_tiles_t = _diff_bias(rel_bias, t_attn)
    wr = jnp.zeros((ROUTE_ROWS, D_MODEL), F32)
    wr = wr.at[:N_GROUPS].set(w_group_router[l].T).at[N_GROUPS:N_GROUPS + N_EXPERTS].set(w_expert_router[l].T)
    wr_hi = wr.astype(BF16)
    wr_lo = (wr - wr_hi.astype(F32)).astype(BF16)
    rb = jnp.zeros((ROUTE_ROWS, 1), F32)
    rb = rb.at[:N_GROUPS, 0].set(b_group_router[l]).at[N_GROUPS:N_GROUPS + N_EXPERTS, 0].set(b_expert_router[l])

    row = lambda v: v.astype(F32).reshape(1, -1)
    x2d = x.reshape(n_tok, D_MODEL)

    qk, vt, cum = _mix_proj(x2d, row(norm_mix_g[l]), w_qk, w_vt, wft, bf, qg, batch=batch, seq=seq)
    cum4 = cum[:, :N_HEADS].reshape(batch, N_HEADS, 1, seq)
    gain_t = lambda v: jnp.broadcast_to(v.astype(F32)[:, None], (HEAD_DIM, t_attn))
    mixed = _attn(qk, vt, bias_tiles_t, lam, cum4, gain_t(diff_subln_g[l]), gain_t(fox_out_norm_g[l]),
                  batch=batch, seq=seq)
    x1, qc = _out_q(x2d, mixed, w_out[l].astype(BF16), row(norm_cross_g[l]),
                    w_cq[l].astype(BF16), row(cross_q_norm_g[l]))

    kc, vc = _mem_kv(mem.reshape(batch * N_MEM, D_MODEL), row(norm_mem_g[l]), w_ckv[l].astype(BF16),
                     row(cross_k_norm_g[l]), batch=batch)
    xa, rt = _cross_router(qc, kc, vc, x1, w_co[l].astype(BF16), row(norm_ffn_g[l]), wr_hi, wr_lo, rb,
                           batch=batch, seq=seq)

    pos, cnt = _rank(rt)
    pos = pos.reshape(n_tok)
    counts = cnt[:N_PAIR_BUCKETS, 0].astype(I32)
    n_tiles = n_tok // MOE_ROWS + N_PAIR_BUCKETS
    zrow, zflag, plan, n_active = _tile_plan(counts, n_tiles)
    n_active = n_active.reshape(1)
    xs = _dispatch(pos, zrow, zflag, n_active, xa, n_tiles=n_tiles)
    ys = _moe(plan, n_active, xs, row(norm_ffn_g[l]),
              w_exp_gate[l].astype(F32), w_exp_up[l].astype(F32), w_exp_down[l].astype(F32))
    out = _combine(pos, ys, n_tok=n_tok)
    return out.reshape(batch, seq, D_MODEL)
```
